```python
import jax, jax.numpy as jnp
from jax import lax
import numpy as np

D_MODEL = 4096
BATCH = 8
SEQ = 4096
DEPTH = 1

CHUNK = 64
Q_BLOCK = 128
PLE_DIM = 256
RET_HEADS = 8
RET_HEAD_DIM = D_MODEL // (2 * RET_HEADS)
RET_WIDTH = RET_HEADS * RET_HEAD_DIM
MLA_HEADS = 16
MLA_NOPE = 128
MLA_ROPE = 64
MLA_V = (D_MODEL - RET_WIDTH) // MLA_HEADS
Q_LORA = D_MODEL // 4
KV_LORA = D_MODEL // 8
D_FF = ((8 * D_MODEL // 3 + 255) // 256) * 256
CONV_WIDTH = 3
ROPE_BASE = 10000.0
EPS = 1e-6
IN_WIDTH = 4 * RET_WIDTH + Q_LORA + KV_LORA + MLA_ROPE
IN_SPLITS = (RET_WIDTH, 2 * RET_WIDTH, 3 * RET_WIDTH, 4 * RET_WIDTH,
             4 * RET_WIDTH + Q_LORA, 4 * RET_WIDTH + Q_LORA + KV_LORA)

kernel_name = "hybrid_retention_mla_convffn_ple"


def rms_norm(x, g):
    xf = x.astype(jnp.float32)
    y = xf * lax.rsqrt(jnp.mean(xf * xf, axis=-1, keepdims=True) + EPS)
    return (y * g.astype(jnp.float32)).astype(x.dtype)


def head_norm(o):
    of = o.astype(jnp.float32)
    mu = jnp.mean(of, axis=-1, keepdims=True)
    var = jnp.mean(jnp.square(of - mu), axis=-1, keepdims=True)
    return ((of - mu) * lax.rsqrt(var + EPS)).astype(o.dtype)


def rope_tables(seq, dim):
    inv = 1.0 / (ROPE_BASE ** (jnp.arange(0, dim, 2, dtype=jnp.float32) / dim))
    ang = jnp.arange(seq, dtype=jnp.float32)[:, None] * inv[None, :]
    return jnp.cos(ang), jnp.sin(ang)


def apply_rope(x, cos, sin):
    x1, x2 = jnp.split(x, 2, axis=-1)
    c = cos[None, :, None, :].astype(x.dtype)
    s = sin[None, :, None, :].astype(x.dtype)
    return jnp.concatenate([x1 * c - x2 * s, x2 * c + x1 * s], axis=-1)


def retention(q, k, v):
    B, S, H, dk = q.shape
    dv = v.shape[-1]
    nC = S // CHUNK
    dt = q.dtype
    log_g = jnp.log1p(-jnp.exp2(-5.0 - jnp.arange(H, dtype=jnp.float32)))
    idx = jnp.arange(CHUNK, dtype=jnp.float32)
    inner = jnp.exp(log_g[:, None, None] * jnp.abs(idx[:, None] - idx[None, :])).astype(dt)
    q_dec = jnp.exp(log_g[:, None] * (idx + 1.0))[..., None].astype(dt)
    k_dec = jnp.exp(log_g[:, None] * (CHUNK - 1.0 - idx))[..., None].astype(dt)
    s_dec = jnp.exp(log_g * CHUNK)[:, None, None].astype(dt)
    k = k * (dk ** -0.5)

    def to_chunks(t):
        return t.reshape(B, nC, CHUNK, H, t.shape[-1]).transpose(1, 0, 3, 2, 4)

    def step(state, qkv):
        qc, kc, vc = qkv
        scores = jnp.einsum('bhnd,bhmd->bhnm', qc, kc) * inner
        out = (jnp.einsum('bhnm,bhmv->bhnv', scores, vc)
               + jnp.einsum('bhnd,bhdv->bhnv', qc * q_dec, state))
        state = state * s_dec + jnp.einsum('bhmd,bhmv->bhdv', kc * k_dec, vc)
        return state, out

    s0 = jnp.zeros((B, H, dk, dv), dt)
    _, out = lax.scan(step, s0, (to_chunks(q), to_chunks(k), to_chunks(v)))
    return out.transpose(1, 0, 3, 2, 4).reshape(B, S, H, dv)


def mla_attention(q_nope, q_rope, k_nope, k_rope, v):
    B, S, H, _ = q_nope.shape
    nQ = S // Q_BLOCK
    scale = (MLA_NOPE + MLA_ROPE) ** -0.5
    key_chunk = jnp.arange(S) // CHUNK

    def blocks(t):
        return t.reshape(B, nQ, Q_BLOCK, *t.shape[2:]).swapaxes(0, 1)

    def one_block(args):
        qn, qr, b = args
        s = (jnp.einsum('bqhd,bkhd->bhqk', qn, k_nope)
             + jnp.einsum('bqhr,bkr->bhqk', qr, k_rope)).astype(jnp.float32) * scale
        q_chunk = (b * Q_BLOCK + jnp.arange(Q_BLOCK)) // CHUNK
        mask = key_chunk[None, :] <= q_chunk[:, None]
        s = jnp.where(mask[None, None], s, -jnp.inf)
        w = jax.nn.softmax(s, axis=-1).astype(v.dtype)
        return jnp.einsum('bhqk,bkhv->bqhv', w, v)

    out = lax.map(one_block, (blocks(q_nope), blocks(q_rope), jnp.arange(nQ)))
    return out.swapaxes(0, 1).reshape(B, S, H, v.shape[-1])


def causal_dwconv(h, w, b):
    S = h.shape[1]
    hp = jnp.pad(h, ((0, 0), (CONV_WIDTH - 1, 0), (0, 0)))
    out = b
    for j in range(CONV_WIDTH):
        out = out + hp[:, j:j + S, :] * w[j]
    return out


def _fwd_setup_inputs(seed: int = 0) -> dict:
    key = jax.random.key(seed)
    ks = jax.random.split(key, 20)

    def nrm(k, shape, fan_in):
        return jax.random.normal(k, shape, jnp.float32) * (fan_in ** -0.5)

    def gain(k, shape):
        return 1.0 + 0.01 * jax.random.normal(k, shape, jnp.float32)

    L = DEPTH
    return {
        "x": jax.random.normal(ks[0], (BATCH, SEQ, D_MODEL), jnp.float32),
        "p": jax.random.normal(ks[1], (DEPTH, BATCH, SEQ, PLE_DIM), jnp.float32),
        "w_in": nrm(ks[2], (L, D_MODEL, IN_WIDTH), D_MODEL),
        "g_attn": gain(ks[3], (L, D_MODEL)),
        "g_q_lora": gain(ks[4], (L, Q_LORA)),
        "g_kv_lora": gain(ks[5], (L, KV_LORA)),
        "w_uq": nrm(ks[6], (L, Q_LORA, MLA_HEADS * (MLA_NOPE + MLA_ROPE)), Q_LORA),
        "w_ukv": nrm(ks[7], (L, KV_LORA, MLA_HEADS * (MLA_NOPE + MLA_V)), KV_LORA),
        "w_o": nrm(ks[8], (L, RET_WIDTH + MLA_HEADS * MLA_V, D_MODEL), D_MODEL),
        "g_ffn": gain(ks[9], (L, D_MODEL)),
        "w_ffn_gate": nrm(ks[10], (L, D_MODEL, D_FF), D_MODEL),
        "w_ffn_up": nrm(ks[11], (L, D_MODEL, D_FF), D_MODEL),
        "conv_w": nrm(ks[12], (L, CONV_WIDTH, D_FF), CONV_WIDTH),
        "conv_b": 0.01 * jax.random.normal(ks[13], (L, D_FF), jnp.float32),
        "w_ffn_down": nrm(ks[14], (L, D_FF, D_MODEL), D_FF),
        "g_ple": gain(ks[15], (L, D_MODEL)),
        "w_ple_gate": nrm(ks[16], (L, D_MODEL, D_MODEL), D_MODEL),
        "w_ple_proj": nrm(ks[17], (L, PLE_DIM, D_MODEL), PLE_DIM),
        "g_final": gain(ks[18], (D_MODEL,)),
    }


def _fwd_reference(x, p, w_in, g_attn, g_q_lora, g_kv_lora, w_uq, w_ukv, w_o, g_ffn,
              w_ffn_gate, w_ffn_up, conv_w, conv_b, w_ffn_down, g_ple,
              w_ple_gate, w_ple_proj, g_final):
    B, S, _ = x.shape
    cos_r, sin_r = rope_tables(S, RET_HEAD_DIM)
    cos_m, sin_m = rope_tables(S, MLA_ROPE)
    h = x
    for i in range(DEPTH):
        hn = rms_norm(h, g_attn[i])
        proj = hn @ w_in[i]
        rq, rk, rv, rg, cq, ckv, kr = jnp.split(proj, IN_SPLITS, axis=-1)

        rq = apply_rope(rq.reshape(B, S, RET_HEADS, RET_HEAD_DIM), cos_r, sin_r)
        rk = apply_rope(rk.reshape(B, S, RET_HEADS, RET_HEAD_DIM), cos_r, sin_r)
        rv = rv.reshape(B, S, RET_HEADS, RET_HEAD_DIM)
        ro = head_norm(retention(rq, rk, rv)).reshape(B, S, RET_WIDTH)
        ro = jax.nn.silu(rg) * ro

        q = (rms_norm(cq, g_q_lora[i]) @ w_uq[i]).reshape(B, S, MLA_HEADS, MLA_NOPE + MLA_ROPE)
        q_nope = q[..., :MLA_NOPE]
        q_rope = apply_rope(q[..., MLA_NOPE:], cos_m, sin_m)
        kv = (rms_norm(ckv, g_kv_lora[i]) @ w_ukv[i]).reshape(B, S, MLA_HEADS, MLA_NOPE + MLA_V)
        k_nope = kv[..., :MLA_NOPE]
        v = kv[..., MLA_NOPE:]
        k_rope = apply_rope(kr[:, :, None, :], cos_m, sin_m)[:, :, 0, :]
        mo = mla_attention(q_nope, q_rope, k_nope, k_rope, v).reshape(B, S, MLA_HEADS * MLA_V)

        h = h + jnp.concatenate([ro, mo], axis=-1) @ w_o[i]

        hn = rms_norm(h, g_ffn[i])
        a = causal_dwconv(hn @ w_ffn_gate[i], conv_w[i], conv_b[i])
        h = h + (jax.nn.silu(a) * (hn @ w_ffn_up[i])) @ w_ffn_down[i]

        gate = jax.nn.sigmoid(rms_norm(h, g_ple[i]) @ w_ple_gate[i])
        h = h + gate * (p[i] @ w_ple_proj[i])
    return rms_norm(h, g_final)


import jax as _jax
import jax.numpy as _jnp

TWIN_FORMAT = 'train_step'
FWD_PARAMS = ['x', 'p', 'w_in', 'g_attn', 'g_q_lora', 'g_kv_lora', 'w_uq', 'w_ukv', 'w_o', 'g_ffn', 'w_ffn_gate', 'w_ffn_up', 'conv_w', 'conv_b', 'w_ffn_down', 'g_ple', 'w_ple_gate', 'w_ple_proj', 'g_final']
TWIN_WEIGHTS = ['w_in', 'g_attn', 'g_q_lora', 'g_kv_lora', 'w_uq', 'w_ukv', 'w_o', 'g_ffn', 'w_ffn_gate', 'w_ffn_up', 'conv_w', 'conv_b', 'w_ffn_down', 'g_ple', 'w_ple_gate', 'w_ple_proj', 'g_final']
TWIN_DIFF_INPUT = 'x'
TWIN_INPUTS = ['x', 'p', 'w_in', 'g_attn', 'g_q_lora', 'g_kv_lora', 'w_uq', 'w_ukv', 'w_o', 'g_ffn', 'w_ffn_gate', 'w_ffn_up', 'conv_w', 'conv_b', 'w_ffn_down', 'g_ple', 'w_ple_gate', 'w_ple_proj', 'g_final', 'loss_target', 'm_w_in', 'm_g_attn', 'm_g_q_lora', 'm_g_kv_lora', 'm_w_uq', 'm_w_ukv', 'm_w_o', 'm_g_ffn', 'm_w_ffn_gate', 'm_w_ffn_up', 'm_conv_w', 'm_conv_b', 'm_w_ffn_down', 'm_g_ple', 'm_w_ple_gate', 'm_w_ple_proj', 'm_g_final', 'v_w_in', 'v_g_attn', 'v_g_q_lora', 'v_g_kv_lora', 'v_w_uq', 'v_w_ukv', 'v_w_o', 'v_g_ffn', 'v_w_ffn_gate', 'v_w_ffn_up', 'v_conv_w', 'v_conv_b', 'v_w_ffn_down', 'v_g_ple', 'v_w_ple_gate', 'v_w_ple_proj', 'v_g_final']
TWIN_OUTPUTS = ['loss', 'grad_x', 'grad_w_in', 'grad_g_attn', 'grad_g_q_lora', 'grad_g_kv_lora', 'grad_w_uq', 'grad_w_ukv', 'grad_w_o', 'grad_g_ffn', 'grad_w_ffn_gate', 'grad_w_ffn_up', 'grad_conv_w', 'grad_conv_b', 'grad_w_ffn_down', 'grad_g_ple', 'grad_w_ple_gate', 'grad_w_ple_proj', 'grad_g_final', 'delta_w_in', 'delta_g_attn', 'delta_g_q_lora', 'delta_g_kv_lora', 'delta_w_uq', 'delta_w_ukv', 'delta_w_o', 'delta_g_ffn', 'delta_w_ffn_gate', 'delta_w_ffn_up', 'delta_conv_w', 'delta_conv_b', 'delta_w_ffn_down', 'delta_g_ple', 'delta_w_ple_gate', 'delta_w_ple_proj', 'delta_g_final', 'new_m_w_in', 'new_m_g_attn', 'new_m_g_q_lora', 'new_m_g_kv_lora', 'new_m_w_uq', 'new_m_w_ukv', 'new_m_w_o', 'new_m_g_ffn', 'new_m_w_ffn_gate', 'new_m_w_ffn_up', 'new_m_conv_w', 'new_m_conv_b', 'new_m_w_ffn_down', 'new_m_g_ple', 'new_m_w_ple_gate', 'new_m_w_ple_proj', 'new_m_g_final', 'new_v_w_in', 'new_v_g_attn', 'new_v_g_q_lora', 'new_v_g_kv_lora', 'new_v_w_uq', 'new_v_w_ukv', 'new_v_w_o', 'new_v_g_ffn', 'new_v_w_ffn_gate', 'new_v_w_ffn_up', 'new_v_conv_w', 'new_v_conv_b', 'new_v_w_ffn_down', 'new_v_g_ple', 'new_v_w_ple_gate', 'new_v_w_ple_proj', 'new_v_g_final']
TWIN_LEAF_KINDS = {'loss': 'loss', 'grad_x': 'grad_x', 'grad_w_in': 'grad_w', 'grad_g_attn': 'grad_w', 'grad_g_q_lora': 'grad_w', 'grad_g_kv_lora': 'grad_w', 'grad_w_uq': 'grad_w', 'grad_w_ukv': 'grad_w', 'grad_w_o': 'grad_w', 'grad_g_ffn': 'grad_w', 'grad_w_ffn_gate': 'grad_w', 'grad_w_ffn_up': 'grad_w', 'grad_conv_w': 'grad_w', 'grad_conv_b': 'grad_w', 'grad_w_ffn_down': 'grad_w', 'grad_g_ple': 'grad_w', 'grad_w_ple_gate': 'grad_w', 'grad_w_ple_proj': 'grad_w', 'grad_g_final': 'grad_w', 'delta_w_in': 'delta_w', 'delta_g_attn': 'delta_w', 'delta_g_q_lora': 'delta_w', 'delta_g_kv_lora': 'delta_w', 'delta_w_uq': 'delta_w', 'delta_w_ukv': 'delta_w', 'delta_w_o': 'delta_w', 'delta_g_ffn': 'delta_w', 'delta_w_ffn_gate': 'delta_w', 'delta_w_ffn_up': 'delta_w', 'delta_conv_w': 'delta_w', 'delta_conv_b': 'delta_w', 'delta_w_ffn_down': 'delta_w', 'delta_g_ple': 'delta_w', 'delta_w_ple_gate': 'delta_w', 'delta_w_ple_proj': 'delta_w', 'delta_g_final': 'delta_w', 'new_m_w_in': 'new_m', 'new_m_g_attn': 'new_m', 'new_m_g_q_lora': 'new_m', 'new_m_g_kv_lora': 'new_m', 'new_m_w_uq': 'new_m', 'new_m_w_ukv': 'new_m', 'new_m_w_o': 'new_m', 'new_m_g_ffn': 'new_m', 'new_m_w_ffn_gate': 'new_m', 'new_m_w_ffn_up': 'new_m', 'new_m_conv_w': 'new_m', 'new_m_conv_b': 'new_m', 'new_m_w_ffn_down': 'new_m', 'new_m_g_ple': 'new_m', 'new_m_w_ple_gate': 'new_m', 'new_m_w_ple_proj': 'new_m', 'new_m_g_final': 'new_m', 'new_v_w_in': 'new_v', 'new_v_g_attn': 'new_v', 'new_v_g_q_lora': 'new_v', 'new_v_g_kv_lora': 'new_v', 'new_v_w_uq': 'new_v', 'new_v_w_ukv': 'new_v', 'new_v_w_o': 'new_v', 'new_v_g_ffn': 'new_v', 'new_v_w_ffn_gate': 'new_v', 'new_v_w_ffn_up': 'new_v', 'new_v_conv_w': 'new_v', 'new_v_conv_b': 'new_v', 'new_v_w_ffn_down': 'new_v', 'new_v_g_ple': 'new_v', 'new_v_w_ple_gate': 'new_v', 'new_v_w_ple_proj': 'new_v', 'new_v_g_final': 'new_v'}


def _forward(args):
    return _fwd_reference(*[args[k] for k in FWD_PARAMS])


def _output_shape():
    out = _jax.eval_shape(lambda: _forward(_fwd_setup_inputs(0)))
    return out.shape, out.dtype

N_MICROBATCH = 1
ADAM_LR = 0.001
ADAM_B1 = 0.9
ADAM_B2 = 0.999
ADAM_EPS = 1e-08
ADAM_WD = 0.01
ADAM_STEP = 10
PER_EXAMPLE_BATCH_AXIS = {'x': 0, 'p': 1, 'loss_target': 0}
SHARED_INPUTS = []
_WEIGHT_DTYPES = {'w_in': _jnp.float32, 'g_attn': _jnp.float32, 'g_q_lora': _jnp.float32, 'g_kv_lora': _jnp.float32, 'w_uq': _jnp.float32, 'w_ukv': _jnp.float32, 'w_o': _jnp.float32, 'g_ffn': _jnp.float32, 'w_ffn_gate': _jnp.float32, 'w_ffn_up': _jnp.float32, 'conv_w': _jnp.float32, 'conv_b': _jnp.float32, 'w_ffn_down': _jnp.float32, 'g_ple': _jnp.float32, 'w_ple_gate': _jnp.float32, 'w_ple_proj': _jnp.float32, 'g_final': _jnp.float32}
MOMENT_SCALE = {'w_in': 2.386614e-02, 'g_attn': 3.762103e-02, 'g_q_lora': 8.404679e-03, 'g_kv_lora': 1.741244e-02, 'w_uq': 4.941285e-03, 'w_ukv': 5.681739e-03, 'w_o': 1.838694e-02, 'g_ffn': 2.905864e-02, 'w_ffn_gate': 1.280620e-02, 'w_ffn_up': 1.241159e-02, 'conv_w': 1.278235e-02, 'conv_b': 1.209007e-02, 'w_ffn_down': 2.036044e-02, 'g_ple': 6.954928e-03, 'w_ple_gate': 6.886364e-03, 'w_ple_proj': 1.765340e-02, 'g_final': 7.983673e+00}


def _to_microbatches(a, axis):
    t = _jnp.moveaxis(a, axis, 0)
    t = t.reshape((N_MICROBATCH, t.shape[0] // N_MICROBATCH) + t.shape[1:])
    return _jnp.moveaxis(t, 1, axis + 1)


def setup_inputs(seed: int = 0) -> dict:
    inp = _fwd_setup_inputs(seed)
    key = _jax.random.fold_in(_jax.random.key(seed), 7919)
    shape, _ = _output_shape()
    out = dict(inp)
    out["loss_target"] = _jax.random.normal(_jax.random.fold_in(key, 0), shape, _jnp.float32)
    for i, name in enumerate(TWIN_WEIGHTS):
        w = inp[name].astype(_jnp.float32)
        if MOMENT_SCALE is None:
            s = _jnp.sqrt(_jnp.mean(_jnp.square(w)) + 1e-30)
        else:
            s = MOMENT_SCALE[name]
        km, kv = _jax.random.split(_jax.random.fold_in(key, i + 1))
        out[name] = w
        out["m_" + name] = s * _jax.random.normal(km, w.shape, _jnp.float32)
        out["v_" + name] = (s * s) * _jax.random.uniform(kv, w.shape, _jnp.float32, 0.5, 1.5)
    if N_MICROBATCH > 1:
        for name, axis in PER_EXAMPLE_BATCH_AXIS.items():
            out[name] = _to_microbatches(out[name], axis)
    return {'x': out['x'], 'p': out['p'], 'w_in': out['w_in'], 'g_attn': out['g_attn'], 'g_q_lora': out['g_q_lora'], 'g_kv_lora': out['g_kv_lora'], 'w_uq': out['w_uq'], 'w_ukv': out['w_ukv'], 'w_o': out['w_o'], 'g_ffn': out['g_ffn'], 'w_ffn_gate': out['w_ffn_gate'], 'w_ffn_up': out['w_ffn_up'], 'conv_w': out['conv_w'], 'conv_b': out['conv_b'], 'w_ffn_down': out['w_ffn_down'], 'g_ple': out['g_ple'], 'w_ple_gate': out['w_ple_gate'], 'w_ple_proj': out['w_ple_proj'], 'g_final': out['g_final'], 'loss_target': out['loss_target'], 'm_w_in': out['m_w_in'], 'm_g_attn': out['m_g_attn'], 'm_g_q_lora': out['m_g_q_lora'], 'm_g_kv_lora': out['m_g_kv_lora'], 'm_w_uq': out['m_w_uq'], 'm_w_ukv': out['m_w_ukv'], 'm_w_o': out['m_w_o'], 'm_g_ffn': out['m_g_ffn'], 'm_w_ffn_gate': out['m_w_ffn_gate'], 'm_w_ffn_up': out['m_w_ffn_up'], 'm_conv_w': out['m_conv_w'], 'm_conv_b': out['m_conv_b'], 'm_w_ffn_down': out['m_w_ffn_down'], 'm_g_ple': out['m_g_ple'], 'm_w_ple_gate': out['m_w_ple_gate'], 'm_w_ple_proj': out['m_w_ple_proj'], 'm_g_final': out['m_g_final'], 'v_w_in': out['v_w_in'], 'v_g_attn': out['v_g_attn'], 'v_g_q_lora': out['v_g_q_lora'], 'v_g_kv_lora': out['v_g_kv_lora'], 'v_w_uq': out['v_w_uq'], 'v_w_ukv': out['v_w_ukv'], 'v_w_o': out['v_w_o'], 'v_g_ffn': out['v_g_ffn'], 'v_w_ffn_gate': out['v_w_ffn_gate'], 'v_w_ffn_up': out['v_w_ffn_up'], 'v_conv_w': out['v_conv_w'], 'v_conv_b': out['v_conv_b'], 'v_w_ffn_down': out['v_w_ffn_down'], 'v_g_ple': out['v_g_ple'], 'v_w_ple_gate': out['v_w_ple_gate'], 'v_w_ple_proj': out['v_w_ple_proj'], 'v_g_final': out['v_g_final']}


def _loss(weights, diff, rest, loss_target):
    with _jax.named_scope("forward"):
        args = {**rest, TWIN_DIFF_INPUT: diff, **{k: w.astype(_WEIGHT_DTYPES[k]) for k, w in weights.items()}}
        y = _forward(args)
    with _jax.named_scope("loss_head"):
        err = _jnp.square(y.astype(_jnp.float32) - loss_target)
        return 0.5 * _jnp.sum(_jnp.mean(err, axis=-1)) if err.ndim else 0.5 * err


def _adamw(w, g, m, v):
    m = ADAM_B1 * m + (1.0 - ADAM_B1) * g
    v = ADAM_B2 * v + (1.0 - ADAM_B2) * _jnp.square(g)
    m_hat = m / (1.0 - ADAM_B1 ** ADAM_STEP)
    v_hat = v / (1.0 - ADAM_B2 ** ADAM_STEP)
    delta = -ADAM_LR * (m_hat / (_jnp.sqrt(v_hat) + ADAM_EPS) + ADAM_WD * w)
    return delta, m, v


def reference(x, p, w_in, g_attn, g_q_lora, g_kv_lora, w_uq, w_ukv, w_o, g_ffn, w_ffn_gate, w_ffn_up, conv_w, conv_b, w_ffn_down, g_ple, w_ple_gate, w_ple_proj, g_final, loss_target, m_w_in, m_g_attn, m_g_q_lora, m_g_kv_lora, m_w_uq, m_w_ukv, m_w_o, m_g_ffn, m_w_ffn_gate, m_w_ffn_up, m_conv_w, m_conv_b, m_w_ffn_down, m_g_ple, m_w_ple_gate, m_w_ple_proj, m_g_final, v_w_in, v_g_attn, v_g_q_lora, v_g_kv_lora, v_w_uq, v_w_ukv, v_w_o, v_g_ffn, v_w_ffn_gate, v_w_ffn_up, v_conv_w, v_conv_b, v_w_ffn_down, v_g_ple, v_w_ple_gate, v_w_ple_proj, v_g_final):
    given = dict(x=x, p=p, w_in=w_in, g_attn=g_attn, g_q_lora=g_q_lora, g_kv_lora=g_kv_lora, w_uq=w_uq, w_ukv=w_ukv, w_o=w_o, g_ffn=g_ffn, w_ffn_gate=w_ffn_gate, w_ffn_up=w_ffn_up, conv_w=conv_w, conv_b=conv_b, w_ffn_down=w_ffn_down, g_ple=g_ple, w_ple_gate=w_ple_gate, w_ple_proj=w_ple_proj, g_final=g_final, loss_target=loss_target, m_w_in=m_w_in, m_g_attn=m_g_attn, m_g_q_lora=m_g_q_lora, m_g_kv_lora=m_g_kv_lora, m_w_uq=m_w_uq, m_w_ukv=m_w_ukv, m_w_o=m_w_o, m_g_ffn=m_g_ffn, m_w_ffn_gate=m_w_ffn_gate, m_w_ffn_up=m_w_ffn_up, m_conv_w=m_conv_w, m_conv_b=m_conv_b, m_w_ffn_down=m_w_ffn_down, m_g_ple=m_g_ple, m_w_ple_gate=m_w_ple_gate, m_w_ple_proj=m_w_ple_proj, m_g_final=m_g_final, v_w_in=v_w_in, v_g_attn=v_g_attn, v_g_q_lora=v_g_q_lora, v_g_kv_lora=v_g_kv_lora, v_w_uq=v_w_uq, v_w_ukv=v_w_ukv, v_w_o=v_w_o, v_g_ffn=v_g_ffn, v_w_ffn_gate=v_w_ffn_gate, v_w_ffn_up=v_w_ffn_up, v_conv_w=v_conv_w, v_conv_b=v_conv_b, v_w_ffn_down=v_w_ffn_down, v_g_ple=v_g_ple, v_w_ple_gate=v_w_ple_gate, v_w_ple_proj=v_w_ple_proj, v_g_final=v_g_final)
    weights = {n: given[n] for n in TWIN_WEIGHTS}
    shared = {n: given[n] for n in SHARED_INPUTS}
    per_example = {n: given[n] for n in ['x', 'p']}
    grad_fn = _jax.value_and_grad(_loss, argnums=(0, 1))

    def one_microbatch(ex, loss_target):
        ex = dict(ex)
        diff = ex.pop(TWIN_DIFF_INPUT)
        return grad_fn(weights, diff, {**shared, **ex}, loss_target)

    if N_MICROBATCH == 1:
        loss, (grad_w, grad_x) = one_microbatch(per_example, given["loss_target"])
    else:
        def body(carry, xs):
            loss_sum, grad_sum = carry
            l_k, (gw_k, gx_k) = one_microbatch(xs[0], xs[1])
            with _jax.named_scope("update"):
                return (loss_sum + l_k, _jax.tree.map(_jnp.add, grad_sum, gw_k)), gx_k

        init = (_jnp.zeros((), _jnp.float32), _jax.tree.map(_jnp.zeros_like, weights))
        (loss, grad_w), grad_x = _jax.lax.scan(body, init, (per_example, given["loss_target"]))
    with _jax.named_scope("update"):
        delta_w, new_m, new_v = {}, {}, {}
        for n in TWIN_WEIGHTS:
            delta_w[n], new_m[n], new_v[n] = _adamw(weights[n], grad_w[n], given["m_" + n], given["v_" + n])
    return (loss, grad_x, *[grad_w[n] for n in TWIN_WEIGHTS], *[delta_w[n] for n in TWIN_WEIGHTS],
            *[new_m[n] for n in TWIN_WEIGHTS], *[new_v[n] for n in TWIN_WEIGHTS])
```

```python
import functools

import numpy as np

import jax
import jax.numpy as jnp
from jax import lax
from jax.experimental import pallas as pl
from jax.experimental.pallas import tpu as pltpu

D_MODEL = 4096
CHUNK = 64
PLE_DIM = 256
RET_HEADS = 8
RET_HEAD_DIM = 256
RET_WIDTH = 2048
MLA_HEADS = 16
MLA_NOPE = 128
MLA_ROPE = 64
MLA_V = 128
Q_LORA = 1024
KV_LORA = 512
D_FF = 11008
CONV_WIDTH = 3
ROPE_BASE = 10000.0
EPS = 1e-6
IN_WIDTH = 9792
ADAM_LR, ADAM_B1, ADAM_B2, ADAM_EPS, ADAM_WD, ADAM_STEP = 0.001, 0.9, 0.999, 1e-08, 0.01, 10

IN_WIDTH_PAD = 10240
D_FF_PAD = 11264
MLA_QK_PAD = 256
Q_WIDTH_PAD = MLA_HEADS * MLA_QK_PAD

N_DEV = 8
MXU_DTYPE = jnp.bfloat16
ATTN_BLOCK = 512
VMEM_LIMIT = 56 * 1024 * 1024

F32 = jnp.float32


def _tile(n, want, align=128):
    if n <= want:
        return n
    t = (want // align) * align
    while t >= align:
        if n % t == 0:
            return t
        t -= align
    return n


def _params(sem):
    return pltpu.CompilerParams(dimension_semantics=sem, vmem_limit_bytes=VMEM_LIMIT)


def _sigmoid(x):
    return 1.0 / (1.0 + jnp.exp(-x))


def _matmul(a, b, *, name, ta=False, tb=False, out_dtype=F32, add=None, tm=1024, tn=1024, tk=1024):
    m, k = (a.shape[1], a.shape[0]) if ta else a.shape
    k2, n = (b.shape[1], b.shape[0]) if tb else b.shape
    assert k == k2, (a.shape, b.shape, ta, tb)
    tm, tn, tk = _tile(m, tm), _tile(n, tn), _tile(k, tk)
    nk = k // tk
    dims = (((0 if ta else 1,), (1 if tb else 0,)), ((), ()))

    def body(*refs):
        if add is None:
            a_ref, b_ref, o_ref, acc_ref = refs
            c_ref = None
        else:
            a_ref, b_ref, c_ref, o_ref, acc_ref = refs
        kk = pl.program_id(2)
        part = lax.dot_general(a_ref[...].astype(MXU_DTYPE), b_ref[...].astype(MXU_DTYPE), dims,
                               preferred_element_type=F32)

        @pl.when(kk == 0)
        def _():
            acc_ref[...] = part

        @pl.when(kk > 0)
        def _():
            acc_ref[...] += part

        @pl.when(kk == nk - 1)
        def _():
            r = acc_ref[...]
            if c_ref is not None:
                r = r + c_ref[...].astype(F32)
            o_ref[...] = r.astype(out_dtype)

    a_spec = pl.BlockSpec((tk, tm), lambda i, j, kk: (kk, i)) if ta else pl.BlockSpec((tm, tk), lambda i, j, kk: (i, kk))
    b_spec = pl.BlockSpec((tn, tk), lambda i, j, kk: (j, kk)) if tb else pl.BlockSpec((tk, tn), lambda i, j, kk: (kk, j))
    in_specs = [a_spec, b_spec]
    args = [a, b]
    if add is not None:
        in_specs.append(pl.BlockSpec((tm, tn), lambda i, j, kk: (i, j)))
        args.append(add)
    return pl.pallas_call(
        body, name=name, grid=(m // tm, n // tn, nk),
        in_specs=in_specs, out_specs=pl.BlockSpec((tm, tn), lambda i, j, kk: (i, j)),
        out_shape=jax.ShapeDtypeStruct((m, n), out_dtype),
        scratch_shapes=[pltpu.VMEM((tm, tn), F32)],
        compiler_params=_params(("parallel", "parallel", "arbitrary")),
    )(*args)


def _rms_fwd(x, g, *, name):
    t, d = x.shape
    tr = _tile(t, 256, 8)

    def body(x_ref, g_ref, o_ref):
        xf = x_ref[...]
        r = lax.rsqrt(jnp.mean(xf * xf, axis=-1, keepdims=True) + EPS)
        o_ref[...] = (xf * r * g_ref[...]).astype(o_ref.dtype)

    return pl.pallas_call(
        body, name=name, grid=(t // tr,),
        in_specs=[pl.BlockSpec((tr, d), lambda i: (i, 0)), pl.BlockSpec((1, d), lambda i: (0, 0))],
        out_specs=pl.BlockSpec((tr, d), lambda i: (i, 0)),
        out_shape=jax.ShapeDtypeStruct((t, d), MXU_DTYPE),
        compiler_params=_params(("parallel",)),
    )(x, g)


def _rms_bwd(x, dhn, g, res, *, name, low_copy=False):
    t, d = x.shape
    tr = _tile(t, 256, 8)

    def body(*refs):
        if res is None:
            x_ref, dh_ref, g_ref = refs[:3]
            outs = refs[3:]
            res_ref = None
        else:
            x_ref, dh_ref, g_ref, res_ref = refs[:4]
            outs = refs[4:]
        dx_ref, dg_ref = outs[0], outs[-1]
        xf = x_ref[...]
        dh = dh_ref[...].astype(F32)
        r = lax.rsqrt(jnp.mean(xf * xf, axis=-1, keepdims=True) + EPS)
        dyg = dh * g_ref[...]
        dx = r * dyg - xf * (r * r * r) * jnp.mean(dyg * xf, axis=-1, keepdims=True)
        if res_ref is not None:
            dx = dx + res_ref[...]
        dx_ref[...] = dx
        if low_copy:
            outs[1][...] = dx.astype(outs[1].dtype)
        part = jnp.sum(dh * xf * r, axis=0, keepdims=True)

        @pl.when(pl.program_id(0) == 0)
        def _():
            dg_ref[...] = part

        @pl.when(pl.program_id(0) > 0)
        def _():
            dg_ref[...] += part

    row = pl.BlockSpec((tr, d), lambda i: (i, 0))
    vec = pl.BlockSpec((1, d), lambda i: (0, 0))
    in_specs = [row, row, vec] + ([] if res is None else [row])
    args = [x, dhn, g] + ([] if res is None else [res])
    out_specs = [row] + ([row] if low_copy else []) + [vec]
    out_shape = [jax.ShapeDtypeStruct((t, d), F32)] + ([jax.ShapeDtypeStruct((t, d), MXU_DTYPE)] if low_copy else []) \
        + [jax.ShapeDtypeStruct((1, d), F32)]
    return pl.pallas_call(
        body, name=name, grid=(t // tr,), in_specs=in_specs, out_specs=out_specs, out_shape=out_shape,
        compiler_params=_params(("arbitrary",)),
    )(*args)


def _rope_tables(t):
    pos = jnp.arange(t, dtype=F32)[:, None]
    inv_r = 1.0 / (ROPE_BASE ** (jnp.arange(0, RET_HEAD_DIM, 2, dtype=F32) / RET_HEAD_DIM))
    ang_r = pos * inv_r[None, :]
    inv_m = 1.0 / (ROPE_BASE ** (jnp.arange(0, MLA_ROPE, 2, dtype=F32) / MLA_ROPE))
    ang_m = pos * inv_m[None, :]
    cm, sm = jnp.cos(ang_m), jnp.sin(ang_m)
    z = jnp.zeros_like(cm)
    cos_m = jnp.concatenate([cm, cm, z, z], axis=1)
    sin_m = jnp.concatenate([-sm, sm, z, z], axis=1)
    return jnp.cos(ang_r), jnp.sin(ang_r), cos_m, sin_m


def _rope256(x, c, s, inverse=False):
    x1, x2 = x[:, :128], x[:, 128:]
    if inverse:
        s = -s
    return jnp.concatenate([x1 * c - x2 * s, x2 * c + x1 * s], axis=1)


def _rope64(x, cos_m, sin_m, inverse=False):
    lane = lax.broadcasted_iota(jnp.int32, x.shape, 1)
    partner = jnp.where(lane < 32, pltpu.roll(x, 96, 1), pltpu.roll(x, 32, 1))
    s = -sin_m if inverse else sin_m
    return x * cos_m + partner * s


def _ret_prep(proj, cos_r, sin_r):
    t = proj.shape[0]
    tr = _tile(t, 512, 8)
    hb = RET_WIDTH // RET_HEAD_DIM

    def body(q_ref, k_ref, v_ref, c_ref, s_ref, qo_ref, ko_ref, vo_ref):
        c, s = c_ref[...], s_ref[...]
        qo_ref[...] = _rope256(q_ref[...], c, s).astype(qo_ref.dtype)
        ko_ref[...] = (_rope256(k_ref[...], c, s) * (RET_HEAD_DIM ** -0.5)).astype(ko_ref.dtype)
        vo_ref[...] = v_ref[...].astype(vo_ref.dtype)

    head = lambda off: pl.BlockSpec((tr, RET_HEAD_DIM), lambda i, h: (i, h + off))
    tab = pl.BlockSpec((tr, 128), lambda i, h: (i, 0))
    out = jax.ShapeDtypeStruct((t, RET_WIDTH), MXU_DTYPE)
    return pl.pallas_call(
        body, name="ret_prep", grid=(t // tr, RET_HEADS),
        in_specs=[head(0), head(hb), head(2 * hb), tab, tab],
        out_specs=[head(0), head(0), head(0)], out_shape=[out, out, out],
        compiler_params=_params(("parallel", "parallel")),
    )(proj, proj, proj, cos_r, sin_r)


def _ret_log_gamma():
    return jnp.asarray(np.log1p(-np.exp2(-5.0 - np.arange(RET_HEADS, dtype=np.float64))), dtype=F32)


def _decay_full(lg, i, j, blk):
    r = lax.broadcasted_iota(jnp.int32, (blk, 1), 0).astype(F32)
    c = lax.broadcasted_iota(jnp.int32, (1, blk), 1).astype(F32)
    off = ((i - j) * blk).astype(F32)
    return jnp.exp(lg * r), jnp.exp(lg * (off - c))


def _decay_diag(lg, blk):
    r = lax.broadcasted_iota(jnp.int32, (blk, blk), 0)
    c = lax.broadcasted_iota(jnp.int32, (blk, blk), 1)
    ok = (c // CHUNK) <= (r // CHUNK)
    return jnp.where(ok, jnp.exp(lg * jnp.abs(r - c).astype(F32)), 0.0)


_NT = (((1,), (1,)), ((), ()))
_TN = (((0,), (0,)), ((), ()))
_NN = (((1,), (0,)), ((), ()))


def _ret_fwd(q, k, v, proj, lg):
    t = q.shape[0]
    blk = _tile(t, ATTN_BLOCK)
    nb = t // blk
    gate_off = 3 * RET_WIDTH // RET_HEAD_DIM

    def body(lg_ref, q_ref, k_ref, v_ref, g_ref, raw_ref, ro_ref, acc_ref):
        h, i, j = pl.program_id(0), pl.program_id(1), pl.program_id(2)
        lgh = lg_ref[h]

        @pl.when(j == 0)
        def _():
            acc_ref[...] = jnp.zeros_like(acc_ref)

        def accumulate(w):
            acc_ref[...] += lax.dot_general(w.astype(MXU_DTYPE), v_ref[...], _NN, preferred_element_type=F32)

        @pl.when(j < i)
        def _():
            s = lax.dot_general(q_ref[...], k_ref[...], _NT, preferred_element_type=F32)
            a, b = _decay_full(lgh, i, j, blk)
            accumulate(s * a * b)

        @pl.when(j == i)
        def _():
            s = lax.dot_general(q_ref[...], k_ref[...], _NT, preferred_element_type=F32)
            accumulate(s * _decay_diag(lgh, blk))
            o = acc_ref[...]
            raw_ref[...] = o
            mu = jnp.mean(o, axis=-1, keepdims=True)
            var = jnp.mean(jnp.square(o - mu), axis=-1, keepdims=True)
            hn = (o - mu) * lax.rsqrt(var + EPS)
            g = g_ref[...]
            ro_ref[...] = (g * _sigmoid(g) * hn).astype(ro_ref.dtype)

    qs = pl.BlockSpec((blk, RET_HEAD_DIM), lambda h, i, j: (i, h))
    ks = pl.BlockSpec((blk, RET_HEAD_DIM), lambda h, i, j: (jnp.minimum(i, j), h))
    gs = pl.BlockSpec((blk, RET_HEAD_DIM), lambda h, i, j: (i, h + gate_off))
    return pl.pallas_call(
        body, name="ret_fwd", grid=(RET_HEADS, nb, nb),
        in_specs=[pl.BlockSpec(memory_space=pltpu.SMEM), qs, ks, ks, gs],
        out_specs=[qs, qs],
        out_shape=[jax.ShapeDtypeStruct((t, RET_WIDTH), F32), jax.ShapeDtypeStruct((t, RET_WIDTH), MXU_DTYPE)],
        scratch_shapes=[pltpu.VMEM((blk, RET_HEAD_DIM), F32)],
        compiler_params=_params(("parallel", "parallel", "arbitrary")),
    )(lg, q, k, v, proj)


def _ret_gate_bwd(raw, proj, dattn):
    t = raw.shape[0]
    tr = _tile(t, 512, 8)
    gate_off = 3 * RET_WIDTH // RET_HEAD_DIM

    def body(o_ref, g_ref, d_ref, do_ref, dg_ref):
        o, g, d = o_ref[...], g_ref[...], d_ref[...]
        mu = jnp.mean(o, axis=-1, keepdims=True)
        rstd = lax.rsqrt(jnp.mean(jnp.square(o - mu), axis=-1, keepdims=True) + EPS)
        hn = (o - mu) * rstd
        sg = _sigmoid(g)
        dg_ref[...] = (d * hn * (sg * (1.0 + g * (1.0 - sg)))).astype(dg_ref.dtype)
        dhn = d * (g * sg)
        do = rstd * (dhn - jnp.mean(dhn, axis=-1, keepdims=True) - hn * jnp.mean(dhn * hn, axis=-1, keepdims=True))
        do_ref[...] = do.astype(do_ref.dtype)

    hs = pl.BlockSpec((tr, RET_HEAD_DIM), lambda i, h: (i, h))
    gs = pl.BlockSpec((tr, RET_HEAD_DIM), lambda i, h: (i, h + gate_off))
    out = jax.ShapeDtypeStruct((t, RET_WIDTH), MXU_DTYPE)
    return pl.pallas_call(
        body, name="ret_gate_bwd", grid=(t // tr, RET_HEADS),
        in_specs=[hs, gs, hs], out_specs=[hs, hs], out_shape=[out, out],
        compiler_params=_params(("parallel", "parallel")),
    )(raw, proj, dattn)


def _ret_bwd(q, k, v, do, lg):
    t = q.shape[0]
    blk = _tile(t, ATTN_BLOCK)
    nb = t // blk

    def body(lg_ref, q_ref, k_ref, v_ref, do_ref, dq_ref, dk_ref, dv_ref, dk_acc, dv_acc):
        h, j, i = pl.program_id(0), pl.program_id(1), pl.program_id(2)
        lgh = lg_ref[h]

        @pl.when((j == 0) & (i == 0))
        def _():
            dq_ref[...] = jnp.zeros_like(dq_ref)

        @pl.when(i == 0)
        def _():
            dk_acc[...] = jnp.zeros_like(dk_acc)
            dv_acc[...] = jnp.zeros_like(dv_acc)

        def step(decay):
            qb, kb, vb, dob = q_ref[...], k_ref[...], v_ref[...], do_ref[...]
            s = lax.dot_general(qb, kb, _NT, preferred_element_type=F32)
            w = (s * decay).astype(MXU_DTYPE)
            dv_acc[...] += lax.dot_general(w, dob, _TN, preferred_element_type=F32)
            dw = lax.dot_general(dob, vb, _NT, preferred_element_type=F32)
            ds = (dw * decay).astype(MXU_DTYPE)
            rows = pl.ds(pl.multiple_of(i * blk, blk), blk)
            dq_ref[rows, :] += lax.dot_general(ds, kb, _NN, preferred_element_type=F32)
            dk_acc[...] += lax.dot_general(ds, qb, _TN, preferred_element_type=F32)

        @pl.when(i > j)
        def _():
            a, b = _decay_full(lgh, i, j, blk)
            step(a * b)

        @pl.when(i == j)
        def _():
            step(_decay_diag(lgh, blk))

        @pl.when(i == nb - 1)
        def _():
            dk_ref[...] = dk_acc[...]
            dv_ref[...] = dv_acc[...].astype(dv_ref.dtype)

    qs = pl.BlockSpec((blk, RET_HEAD_DIM), lambda h, j, i: (jnp.maximum(i, j), h))
    ks = pl.BlockSpec((blk, RET_HEAD_DIM), lambda h, j, i: (j, h))
    return pl.pallas_call(
        body, name="ret_bwd", grid=(RET_HEADS, nb, nb),
        in_specs=[pl.BlockSpec(memory_space=pltpu.SMEM), qs, ks, ks, qs],
        out_specs=[pl.BlockSpec((t, RET_HEAD_DIM), lambda h, j, i: (0, h)), ks, ks],
        out_shape=[jax.ShapeDtypeStruct((t, RET_WIDTH), F32), jax.ShapeDtypeStruct((t, RET_WIDTH), F32),
                   jax.ShapeDtypeStruct((t, RET_WIDTH), MXU_DTYPE)],
        scratch_shapes=[pltpu.VMEM((blk, RET_HEAD_DIM), F32), pltpu.VMEM((blk, RET_HEAD_DIM), F32)],
        compiler_params=_params(("parallel", "arbitrary", "arbitrary")),
    )(lg, q, k, v, do)


def _ret_unrope(dq, dk, cos_r, sin_r):
    t = dq.shape[0]
    tr = _tile(t, 512, 8)

    def body(dq_ref, dk_ref, c_ref, s_ref, oq_ref, ok_ref):
        c, s = c_ref[...], s_ref[...]
        oq_ref[...] = _rope256(dq_ref[...], c, s, inverse=True).astype(oq_ref.dtype)
        ok_ref[...] = (_rope256(dk_ref[...], c, s, inverse=True) * (RET_HEAD_DIM ** -0.5)).astype(ok_ref.dtype)

    hs = pl.BlockSpec((tr, RET_HEAD_DIM), lambda i, h: (i, h))
    tab = pl.BlockSpec((tr, 128), lambda i, h: (i, 0))
    out = jax.ShapeDtypeStruct((t, RET_WIDTH), MXU_DTYPE)
    return pl.pallas_call(
        body, name="ret_unrope", grid=(t // tr, RET_HEADS),
        in_specs=[hs, hs, tab, tab], out_specs=[hs, hs], out_shape=[out, out],
        compiler_params=_params(("parallel", "parallel")),
    )(dq, dk, cos_r, sin_r)


def _mla_prep(cq, ckv, kr, g_q, g_kv, cos_m, sin_m):
    t = cq.shape[0]
    tr = _tile(t, 512, 8)

    def body(cq_ref, ckv_ref, kr_ref, gq_ref, gkv_ref, c_ref, s_ref, cqn_ref, kvn_ref, kro_ref):
        for x_ref, g_ref, o_ref in ((cq_ref, gq_ref, cqn_ref), (ckv_ref, gkv_ref, kvn_ref)):
            xf = x_ref[...]
            r = lax.rsqrt(jnp.mean(xf * xf, axis=-1, keepdims=True) + EPS)
            o_ref[...] = (xf * r * g_ref[...]).astype(o_ref.dtype)
        kro_ref[...] = _rope64(kr_ref[...], c_ref[...], s_ref[...]).astype(kro_ref.dtype)

    row = lambda w: pl.BlockSpec((tr, w), lambda i: (i, 0))
    vec = lambda w: pl.BlockSpec((1, w), lambda i: (0, 0))
    return pl.pallas_call(
        body, name="mla_prep", grid=(t // tr,),
        in_specs=[row(Q_LORA), row(KV_LORA), row(128), vec(Q_LORA), vec(KV_LORA), row(128), row(128)],
        out_specs=[row(Q_LORA), row(KV_LORA), row(128)],
        out_shape=[jax.ShapeDtypeStruct((t, Q_LORA), MXU_DTYPE), jax.ShapeDtypeStruct((t, KV_LORA), MXU_DTYPE),
                   jax.ShapeDtypeStruct((t, 128), MXU_DTYPE)],
        compiler_params=_params(("parallel",)),
    )(cq, ckv, kr, g_q, g_kv, cos_m, sin_m)


def _mla_q_rope(q_lin, cos_m, sin_m, *, inverse, name):
    t = q_lin.shape[0]
    tr = _tile(t, 512, 8)

    def body(q_ref, c_ref, s_ref, o_ref):
        x = q_ref[...].astype(F32)
        roped = _rope64(x[:, 128:], c_ref[...], s_ref[...], inverse=inverse)
        o_ref[...] = jnp.concatenate([x[:, :128], roped], axis=1).astype(o_ref.dtype)

    hs = pl.BlockSpec((tr, MLA_QK_PAD), lambda i, h: (i, h))
    tab = pl.BlockSpec((tr, 128), lambda i, h: (i, 0))
    return pl.pallas_call(
        body, name=name, grid=(t // tr, MLA_HEADS),
        in_specs=[hs, tab, tab], out_specs=hs, out_shape=jax.ShapeDtypeStruct((t, Q_WIDTH_PAD), MXU_DTYPE),
        compiler_params=_params(("parallel", "parallel")),
    )(q_lin, cos_m, sin_m)


_MLA_SCALE = (MLA_NOPE + MLA_ROPE) ** -0.5
_NEG = -1e30


def _mla_mask(blk):
    r = lax.broadcasted_iota(jnp.int32, (blk, blk), 0)
    c = lax.broadcasted_iota(jnp.int32, (blk, blk), 1)
    return (c // CHUNK) <= (r // CHUNK)


def _mla_fwd(q, kv, kr):
    t = q.shape[0]
    blk = _tile(t, ATTN_BLOCK)
    nb = t // blk

    def body(q_ref, kn_ref, v_ref, kr_ref, o_ref, lse_ref, m_ref, l_ref, acc_ref):
        i, j = pl.program_id(1), pl.program_id(2)

        @pl.when(j == 0)
        def _():
            m_ref[...] = jnp.full_like(m_ref, _NEG)
            l_ref[...] = jnp.zeros_like(l_ref)
            acc_ref[...] = jnp.zeros_like(acc_ref)

        def step(masked):
            kb = jnp.concatenate([kn_ref[...], kr_ref[...]], axis=1)
            s = lax.dot_general(q_ref[...], kb, _NT, preferred_element_type=F32) * _MLA_SCALE
            if masked:
                s = jnp.where(_mla_mask(blk), s, _NEG)
            m_prev = m_ref[...]
            m_new = jnp.maximum(m_prev, jnp.max(s, axis=-1, keepdims=True))
            alpha = jnp.exp(m_prev - m_new)
            p = jnp.exp(s - m_new)
            l_ref[...] = alpha * l_ref[...] + jnp.sum(p, axis=-1, keepdims=True)
            acc_ref[...] = alpha * acc_ref[...] + lax.dot_general(p.astype(MXU_DTYPE), v_ref[...], _NN,
                                                                  preferred_element_type=F32)
            m_ref[...] = m_new

        @pl.when(j < i)
        def _():
            step(False)

        @pl.when(j == i)
        def _():
            step(True)
            o_ref[...] = (acc_ref[...] / l_ref[...]).astype(o_ref.dtype)
            lse_ref[...] = jnp.broadcast_to(m_ref[...] + jnp.log(l_ref[...]), lse_ref.shape)

    kmap = lambda off: (lambda h, i, j: (jnp.minimum(i, j), 2 * h + off))
    os_ = pl.BlockSpec((blk, 128), lambda h, i, j: (i, h))
    return pl.pallas_call(
        body, name="mla_fwd", grid=(MLA_HEADS, nb, nb),
        in_specs=[pl.BlockSpec((blk, MLA_QK_PAD), lambda h, i, j: (i, h)),
                  pl.BlockSpec((blk, 128), kmap(0)), pl.BlockSpec((blk, 128), kmap(1)),
                  pl.BlockSpec((blk, 128), lambda h, i, j: (jnp.minimum(i, j), 0))],
        out_specs=[os_, os_],
        out_shape=[jax.ShapeDtypeStruct((t, MLA_HEADS * MLA_V), MXU_DTYPE),
                   jax.ShapeDtypeStruct((t, MLA_HEADS * 128), F32)],
        scratch_shapes=[pltpu.VMEM((blk, 1), F32), pltpu.VMEM((blk, 1), F32), pltpu.VMEM((blk, MLA_V), F32)],
        compiler_params=_params(("parallel", "parallel", "arbitrary")),
    )(q, kv, kv, kr)


def _mla_bwd(q, kv, kr, o, lse, dattn):
    t = q.shape[0]
    blk = _tile(t, ATTN_BLOCK)
    nb = t // blk
    do_off = RET_WIDTH // 128

    def body(q_ref, kn_ref, v_ref, kr_ref, o_ref, lse_ref, do_ref, dq_ref, dk_ref, dv_ref, dk_acc, dv_acc):
        j, i = pl.program_id(1), pl.program_id(2)

        @pl.when((j == 0) & (i == 0))
        def _():
            dq_ref[...] = jnp.zeros_like(dq_ref)

        @pl.when(i == 0)
        def _():
            dk_acc[...] = jnp.zeros_like(dk_acc)
            dv_acc[...] = jnp.zeros_like(dv_acc)

        def step(masked):
            qb, vb = q_ref[...], v_ref[...]
            kb = jnp.concatenate([kn_ref[...], kr_ref[...]], axis=1)
            dof = do_ref[...]
            dob = dof.astype(MXU_DTYPE)
            s = lax.dot_general(qb, kb, _NT, preferred_element_type=F32) * _MLA_SCALE
            if masked:
                s = jnp.where(_mla_mask(blk), s, _NEG)
            p = jnp.exp(s - lse_ref[...][:, :1])
            delta = jnp.sum(dof * o_ref[...].astype(F32), axis=-1, keepdims=True)
            dv_acc[...] += lax.dot_general(p.astype(MXU_DTYPE), dob, _TN, preferred_element_type=F32)
            dp = lax.dot_general(dob, vb, _NT, preferred_element_type=F32)
            ds = (p * (dp - delta) * _MLA_SCALE).astype(MXU_DTYPE)
            rows = pl.ds(pl.multiple_of(i * blk, blk), blk)
            dq_ref[rows, :] += lax.dot_general(ds, kb, _NN, preferred_element_type=F32)
            dk_acc[...] += lax.dot_general(ds, qb, _TN, preferred_element_type=F32)

        @pl.when(i > j)
        def _():
            step(False)

        @pl.when(i == j)
        def _():
            step(True)

        @pl.when(i == nb - 1)
        def _():
            dk_ref[...] = dk_acc[...]
            dv_ref[...] = dv_acc[...].astype(dv_ref.dtype)

    qmap = lambda off: (lambda h, j, i: (jnp.maximum(i, j), h + off))
    kmap = lambda off: (lambda h, j, i: (j, 2 * h + off))
    return pl.pallas_call(
        body, name="mla_bwd", grid=(MLA_HEADS, nb, nb),
        in_specs=[pl.BlockSpec((blk, MLA_QK_PAD), qmap(0)),
                  pl.BlockSpec((blk, 128), kmap(0)), pl.BlockSpec((blk, 128), kmap(1)),
                  pl.BlockSpec((blk, 128), lambda h, j, i: (j, 0)),
                  pl.BlockSpec((blk, 128), qmap(0)), pl.BlockSpec((blk, 128), qmap(0)),
                  pl.BlockSpec((blk, 128), qmap(do_off))],
        out_specs=[pl.BlockSpec((t, MLA_QK_PAD), lambda h, j, i: (0, h)),
                   pl.BlockSpec((blk, MLA_QK_PAD), lambda h, j, i: (j, h)),
                   pl.BlockSpec((blk, MLA_V), lambda h, j, i: (j, h))],
        out_shape=[jax.ShapeDtypeStruct((t, Q_WIDTH_PAD), F32), jax.ShapeDtypeStruct((t, Q_WIDTH_PAD), F32),
                   jax.ShapeDtypeStruct((t, MLA_HEADS * MLA_V), MXU_DTYPE)],
        scratch_shapes=[pltpu.VMEM((blk, MLA_QK_PAD), F32), pltpu.VMEM((blk, MLA_V), F32)],
        compiler_params=_params(("parallel", "arbitrary", "arbitrary")),
    )(q, kv, kv, kr, o, lse, dattn)


def _mla_kv_grad(dk, dv, cos_m, sin_m):
    t = dk.shape[0]
    tr = _tile(t, 256, 8)

    def body(dk_ref, dv_ref, c_ref, s_ref, dkv_ref, dkr_ref):
        acc = jnp.zeros((tr, 128), F32)
        for h in range(MLA_HEADS):
            dkv_ref[:, h * 256:h * 256 + 128] = dk_ref[:, h * 256:h * 256 + 128].astype(dkv_ref.dtype)
            dkv_ref[:, h * 256 + 128:h * 256 + 256] = dv_ref[:, h * 128:(h + 1) * 128].astype(dkv_ref.dtype)
            acc = acc + dk_ref[:, h * 256 + 128:h * 256 + 256]
        dkr_ref[...] = _rope64(acc, c_ref[...], s_ref[...], inverse=True).astype(dkr_ref.dtype)

    row = lambda w: pl.BlockSpec((tr, w), lambda i: (i, 0))
    return pl.pallas_call(
        body, name="mla_kv_grad", grid=(t // tr,),
        in_specs=[row(Q_WIDTH_PAD), row(MLA_HEADS * MLA_V), row(128), row(128)],
        out_specs=[row(Q_WIDTH_PAD), row(128)],
        out_shape=[jax.ShapeDtypeStruct((t, Q_WIDTH_PAD), MXU_DTYPE), jax.ShapeDtypeStruct((t, 128), MXU_DTYPE)],
        compiler_params=_params(("parallel",)),
    )(dk, dv, cos_m, sin_m)


_FFN_COLS = 256
_FFN_ROWS = 256


def _shift_down(cur, prev8, n):
    row = lax.broadcasted_iota(jnp.int32, cur.shape, 0)
    out = pltpu.roll(cur, n, 0)
    for r in range(n):
        out = jnp.where(row == r, prev8[8 - n + r:8 - n + r + 1, :], out)
    return out


def _shift_up(cur, next8, n):
    rows = cur.shape[0]
    row = lax.broadcasted_iota(jnp.int32, cur.shape, 0)
    out = pltpu.roll(cur, rows - n, 0)
    for r in range(n):
        out = jnp.where(row == rows - n + r, next8[r:r + 1, :], out)
    return out


def _conv_pre(g_ref, cw_ref, cb_ref, c, rc):
    r0 = pl.multiple_of(c * rc, rc)
    cur = g_ref[pl.ds(r0, rc), :]
    prev8 = g_ref[pl.ds(pl.multiple_of(jnp.maximum(r0 - 8, 0), 8), 8), :]
    prev8 = jnp.where(c > 0, prev8, 0.0)
    s1, s2 = _shift_down(cur, prev8, 1), _shift_down(cur, prev8, 2)
    a = cb_ref[...] + cw_ref[2:3, :] * cur + cw_ref[1:2, :] * s1 + cw_ref[0:1, :] * s2
    return r0, cur, s1, s2, a


def _ffn_act_fwd(gpre, u, cw, cb):
    t, f = gpre.shape
    tc = _tile(f, _FFN_COLS)
    rc = _tile(t, _FFN_ROWS, 8)

    def body(g_ref, u_ref, cw_ref, cb_ref, o_ref):
        def chunk(c, carry):
            r0, _, _, _, a = _conv_pre(g_ref, cw_ref, cb_ref, c, rc)
            o_ref[pl.ds(r0, rc), :] = (a * _sigmoid(a) * u_ref[pl.ds(r0, rc), :]).astype(o_ref.dtype)
            return carry
        lax.fori_loop(0, t // rc, chunk, 0)

    col = pl.BlockSpec((t, tc), lambda j: (0, j))
    return pl.pallas_call(
        body, name="ffn_act_fwd", grid=(f // tc,),
        in_specs=[col, col, pl.BlockSpec((CONV_WIDTH, tc), lambda j: (0, j)), pl.BlockSpec((1, tc), lambda j: (0, j))],
        out_specs=col, out_shape=jax.ShapeDtypeStruct((t, f), MXU_DTYPE),
        compiler_params=_params(("parallel",)),
    )(gpre, u, cw, cb)


def _ffn_act_bwd(gpre, u, dact, cw, cb):
    t, f = gpre.shape
    tc = _tile(f, _FFN_COLS)
    rc = _tile(t, _FFN_ROWS, 8)
    nc = t // rc

    def body(g_ref, u_ref, d_ref, cw_ref, cb_ref, dg_ref, du_ref, dcw_ref, dcb_ref, da_ref):
        def chunk(c, carry):
            w0, w1, w2, b = carry
            r0, cur, s1, s2, a = _conv_pre(g_ref, cw_ref, cb_ref, c, rc)
            sg = _sigmoid(a)
            d = d_ref[pl.ds(r0, rc), :]
            du_ref[pl.ds(r0, rc), :] = (d * (a * sg)).astype(du_ref.dtype)
            da = d * u_ref[pl.ds(r0, rc), :] * (sg * (1.0 + a * (1.0 - sg)))
            da_ref[pl.ds(r0, rc), :] = da
            return (w0 + jnp.sum(da * s2, axis=0, keepdims=True), w1 + jnp.sum(da * s1, axis=0, keepdims=True),
                    w2 + jnp.sum(da * cur, axis=0, keepdims=True), b + jnp.sum(da, axis=0, keepdims=True))
        z = jnp.zeros((1, tc), F32)
        w0, w1, w2, b = lax.fori_loop(0, nc, chunk, (z, z, z, z))
        dcw_ref[0:1, :] = w0
        dcw_ref[1:2, :] = w1
        dcw_ref[2:3, :] = w2
        dcb_ref[...] = b

        def chunk2(c, carry):
            r0 = pl.multiple_of(c * rc, rc)
            cur = da_ref[pl.ds(r0, rc), :]
            nxt = da_ref[pl.ds(pl.multiple_of(jnp.minimum(r0 + rc, t - 8), 8), 8), :]
            nxt = jnp.where(c < nc - 1, nxt, 0.0)
            dg = cw_ref[2:3, :] * cur + cw_ref[1:2, :] * _shift_up(cur, nxt, 1) + cw_ref[0:1, :] * _shift_up(cur, nxt, 2)
            dg_ref[pl.ds(r0, rc), :] = dg.astype(dg_ref.dtype)
            return carry
        lax.fori_loop(0, nc, chunk2, 0)

    col = pl.BlockSpec((t, tc), lambda j: (0, j))
    w3 = pl.BlockSpec((CONV_WIDTH, tc), lambda j: (0, j))
    w1 = pl.BlockSpec((1, tc), lambda j: (0, j))
    low = jax.ShapeDtypeStruct((t, f), MXU_DTYPE)
    return pl.pallas_call(
        body, name="ffn_act_bwd", grid=(f // tc,),
        in_specs=[col, col, col, w3, w1], out_specs=[col, col, w3, w1],
        out_shape=[low, low, jax.ShapeDtypeStruct((CONV_WIDTH, f), F32), jax.ShapeDtypeStruct((1, f), F32)],
        scratch_shapes=[pltpu.VMEM((t, tc), F32)],
        compiler_params=_params(("parallel",)),
    )(gpre, u, dact, cw, cb)


def _head_fwd_bwd(h2, glin, pp, target, g_final):
    t, d = h2.shape
    tr = _tile(t, 128, 8)

    def body(h_ref, gl_ref, pp_ref, t_ref, g_ref, loss_ref, dh_ref, dgl_ref, dpp_ref, dg_ref):
        gate = _sigmoid(gl_ref[...])
        ppv = pp_ref[...]
        h3 = h_ref[...] + gate * ppv
        r = lax.rsqrt(jnp.mean(h3 * h3, axis=-1, keepdims=True) + EPS)
        yh = h3 * r
        g = g_ref[...]
        diff = yh * g - t_ref[...]
        lpart = 0.5 * jnp.sum(jnp.mean(diff * diff, axis=-1, keepdims=True), axis=0, keepdims=True)
        dy = diff * (1.0 / d)
        dyg = dy * g
        dh3 = r * dyg - h3 * (r * r * r) * jnp.mean(dyg * h3, axis=-1, keepdims=True)
        dh_ref[...] = dh3
        dgl_ref[...] = (dh3 * ppv * gate * (1.0 - gate)).astype(dgl_ref.dtype)
        dpp_ref[...] = (dh3 * gate).astype(dpp_ref.dtype)
        dgp = jnp.sum(dy * yh, axis=0, keepdims=True)

        @pl.when(pl.program_id(0) == 0)
        def _():
            loss_ref[...] = jnp.broadcast_to(lpart, loss_ref.shape)
            dg_ref[...] = dgp

        @pl.when(pl.program_id(0) > 0)
        def _():
            loss_ref[...] += jnp.broadcast_to(lpart, loss_ref.shape)
            dg_ref[...] += dgp

    row = pl.BlockSpec((tr, d), lambda i: (i, 0))
    vec = pl.BlockSpec((1, d), lambda i: (0, 0))
    low = jax.ShapeDtypeStruct((t, d), MXU_DTYPE)
    return pl.pallas_call(
        body, name="head_fwd_bwd", grid=(t // tr,),
        in_specs=[row, row, row, row, vec],
        out_specs=[pl.BlockSpec((8, 128), lambda i: (0, 0)), row, row, row, vec],
        out_shape=[jax.ShapeDtypeStruct((8, 128), F32), jax.ShapeDtypeStruct((t, d), F32), low, low,
                   jax.ShapeDtypeStruct((1, d), F32)],
        compiler_params=_params(("arbitrary",)),
    )(h2, glin, pp, target, g_final)


def _local_step(x, p, target, w, vec):
    t = x.shape[0]
    cos_r, sin_r, cos_m, sin_m = _rope_tables(t)
    lg = _ret_log_gamma()
    low = MXU_DTYPE

    hn1 = _rms_fwd(x, vec["g_attn"], name="rms1_fwd")
    proj = _matmul(hn1, w["w_in"], name="mm_proj")
    rq, rk, rv = _ret_prep(proj, cos_r, sin_r)
    ret_raw, ro = _ret_fwd(rq, rk, rv, proj, lg)
    c0 = 4 * RET_WIDTH
    cq = proj[:, c0:c0 + Q_LORA]
    ckv = proj[:, c0 + Q_LORA:c0 + Q_LORA + KV_LORA]
    kr_in = proj[:, c0 + Q_LORA + KV_LORA:c0 + Q_LORA + KV_LORA + 128]
    cqn, kvn, kr = _mla_prep(cq, ckv, kr_in, vec["g_q_lora"], vec["g_kv_lora"], cos_m, sin_m)
    q_lin = _matmul(cqn, w["w_uq"], name="mm_q")
    q = _mla_q_rope(q_lin, cos_m, sin_m, inverse=False, name="mla_q_rope")
    kv = _matmul(kvn, w["w_ukv"], name="mm_kv", out_dtype=low)
    mo, lse = _mla_fwd(q, kv, kr)
    attn = jnp.concatenate([ro, mo], axis=1)
    h1 = _matmul(attn, w["w_o"], name="mm_o", add=x)
    hn2 = _rms_fwd(h1, vec["g_ffn"], name="rms2_fwd")
    gpre = _matmul(hn2, w["w_ffn_gate"], name="mm_gate")
    u = _matmul(hn2, w["w_ffn_up"], name="mm_up")
    act = _ffn_act_fwd(gpre, u, vec["conv_w"], vec["conv_b"])
    h2 = _matmul(act, w["w_ffn_down"], name="mm_down", add=h1)
    hn3 = _rms_fwd(h2, vec["g_ple"], name="rms3_fwd")
    glin = _matmul(hn3, w["w_ple_gate"], name="mm_ple_gate")
    p_low = p.astype(low)
    pp = _matmul(p_low, w["w_ple_proj"], name="mm_ple_proj")
    loss_part, dh3, dglin, dpp, dg_final = _head_fwd_bwd(h2, glin, pp, target, vec["g_final"])

    gw = {}
    gw["w_ple_proj"] = _matmul(p_low, dpp, ta=True, name="mm_d_ple_proj", out_dtype=low)
    gw["w_ple_gate"] = _matmul(hn3, dglin, ta=True, name="mm_d_ple_gate", out_dtype=low)
    dhn3 = _matmul(dglin, w["w_ple_gate"], tb=True, name="mm_dhn3")
    dh2, dh2_low, dg_ple = _rms_bwd(h2, dhn3, vec["g_ple"], dh3, name="rms3_bwd", low_copy=True)
    gw["w_ffn_down"] = _matmul(act, dh2_low, ta=True, name="mm_d_down", out_dtype=low)
    dact = _matmul(dh2_low, w["w_ffn_down"], tb=True, name="mm_dact")
    dgpre, du, dconv_w, dconv_b = _ffn_act_bwd(gpre, u, dact, vec["conv_w"], vec["conv_b"])
    gw["w_ffn_gate"] = _matmul(hn2, dgpre, ta=True, name="mm_d_gate", out_dtype=low)
    gw["w_ffn_up"] = _matmul(hn2, du, ta=True, name="mm_d_up", out_dtype=low)
    dhn2 = _matmul(dgpre, w["w_ffn_gate"], tb=True, name="mm_dhn2_a")
    dhn2 = _matmul(du, w["w_ffn_up"], tb=True, name="mm_dhn2_b", add=dhn2)
    dh1, dh1_low, dg_ffn = _rms_bwd(h1, dhn2, vec["g_ffn"], dh2, name="rms2_bwd", low_copy=True)
    gw["w_o"] = _matmul(attn, dh1_low, ta=True, name="mm_d_o", out_dtype=low)
    dattn = _matmul(dh1_low, w["w_o"], tb=True, name="mm_dattn")

    dq_r, dk_full, dv = _mla_bwd(q, kv, kr, mo, lse, dattn)
    dq_lin = _mla_q_rope(dq_r, cos_m, sin_m, inverse=True, name="mla_q_unrope")
    dkv, dkr = _mla_kv_grad(dk_full, dv, cos_m, sin_m)
    gw["w_uq"] = _matmul(cqn, dq_lin, ta=True, name="mm_d_uq", out_dtype=low)
    dcqn = _matmul(dq_lin, w["w_uq"], tb=True, name="mm_dcqn")
    gw["w_ukv"] = _matmul(kvn, dkv, ta=True, name="mm_d_ukv", out_dtype=low)
    dkvn = _matmul(dkv, w["w_ukv"], tb=True, name="mm_dkvn")
    dcq, dcq_low, dg_q = _rms_bwd(cq, dcqn, vec["g_q_lora"], None, name="rmsq_bwd", low_copy=True)
    dckv, dckv_low, dg_kv = _rms_bwd(ckv, dkvn, vec["g_kv_lora"], None, name="rmskv_bwd", low_copy=True)

    do_ret, drg = _ret_gate_bwd(ret_raw, proj, dattn)
    dq_ret, dk_ret, drv = _ret_bwd(rq, rk, rv, do_ret, lg)
    drq, drk = _ret_unrope(dq_ret, dk_ret, cos_r, sin_r)

    pad = jnp.zeros((t, IN_WIDTH_PAD - IN_WIDTH - 64), low)
    dproj = jnp.concatenate([drq, drk, drv, drg, dcq_low, dckv_low, dkr, pad], axis=1)
    gw["w_in"] = _matmul(hn1, dproj, ta=True, name="mm_d_in", out_dtype=low)
    dhn1 = _matmul(dproj, w["w_in"], tb=True, name="mm_dhn1")
    grad_x, dg_attn = _rms_bwd(x, dhn1, vec["g_attn"], dh1, name="rms1_bwd")

    gs = {"g_attn": dg_attn, "g_q_lora": dg_q, "g_kv_lora": dg_kv, "g_ffn": dg_ffn, "conv_w": dconv_w,
          "conv_b": dconv_b, "g_ple": dg_ple, "g_final": dg_final}
    return loss_part, grad_x, gw, gs


def _cols_join(a):
    n, r, cs = a.shape
    return a.transpose(1, 0, 2).reshape(r, n * cs)


def _cols_split(gf, width):
    r = gf.shape[0]
    cs = width // N_DEV
    return gf[:, :width].reshape(r, N_DEV, cs).transpose(1, 0, 2).reshape(N_DEV * r, cs)


def _rows_join(a):
    return a.reshape(a.shape[0] * a.shape[1], a.shape[2])


def _prepare_weights(g):
    uq = _cols_join(g["w_uq"]).reshape(Q_LORA, MLA_HEADS, MLA_NOPE + MLA_ROPE)
    uq = jnp.pad(uq, ((0, 0), (0, 0), (0, MLA_QK_PAD - MLA_NOPE - MLA_ROPE))).reshape(Q_LORA, Q_WIDTH_PAD)
    ffn_pad = ((0, 0), (0, D_FF_PAD - D_FF))
    return {
        "w_in": jnp.pad(_cols_join(g["w_in"]), ((0, 0), (0, IN_WIDTH_PAD - IN_WIDTH))),
        "w_uq": uq,
        "w_ukv": _cols_join(g["w_ukv"]),
        "w_o": _rows_join(g["w_o"]),
        "w_ffn_gate": jnp.pad(_cols_join(g["w_ffn_gate"]), ffn_pad),
        "w_ffn_up": jnp.pad(_cols_join(g["w_ffn_up"]), ffn_pad),
        "w_ffn_down": jnp.pad(_rows_join(g["w_ffn_down"]), ((0, D_FF_PAD - D_FF), (0, 0))),
        "w_ple_gate": _rows_join(g["w_ple_gate"]),
        "w_ple_proj": _cols_join(g["w_ple_proj"]),
    }


def _split_grads(gw):
    uq = gw["w_uq"].reshape(Q_LORA, MLA_HEADS, MLA_QK_PAD)[:, :, :MLA_NOPE + MLA_ROPE]
    uq = uq.reshape(Q_LORA, MLA_HEADS * (MLA_NOPE + MLA_ROPE))
    return {
        "w_in": _cols_split(gw["w_in"], IN_WIDTH),
        "w_uq": _cols_split(uq, uq.shape[1]),
        "w_ukv": _cols_split(gw["w_ukv"], gw["w_ukv"].shape[1]),
        "w_o": gw["w_o"],
        "w_ffn_gate": _cols_split(gw["w_ffn_gate"], D_FF),
        "w_ffn_up": _cols_split(gw["w_ffn_up"], D_FF),
        "w_ffn_down": gw["w_ffn_down"][:D_FF],
        "w_ple_gate": gw["w_ple_gate"],
        "w_ple_proj": _cols_split(gw["w_ple_proj"], gw["w_ple_proj"].shape[1]),
    }


_MESH = pl.DeviceIdType.MESH
_ANY = pl.BlockSpec(memory_space=pl.ANY)


def _place():
    x, y, c = lax.axis_index("x"), lax.axis_index("y"), lax.axis_index("c")
    chips = [(1 - x, y), (x, 1 - y), (1 - x, 1 - y)]
    return x, y, c, chips


def _all_gather(shard, *, name):
    def body(x_ref, out_ref, send_sems, recv_sems, local_sem):
        x, y, c, chips = _place()
        sibling = (x, y, 1 - c)

        def slot(px, py, pc):
            return out_ref.at[4 * px + 2 * py + pc]

        def copy(k, block, to, src=None):
            return pltpu.make_async_remote_copy(
                src_ref=slot(*block) if src is None else src, dst_ref=slot(*block),
                send_sem=send_sems.at[k], recv_sem=recv_sems.at[k], device_id=to, device_id_type=_MESH)

        mine = pltpu.make_async_copy(x_ref, slot(x, y, c), local_sem)
        mine.start()
        first = [copy(0, (x, y, c), sibling, src=x_ref)]
        first += [copy(1 + j, (x, y, c), (*chip, c), src=x_ref) for j, chip in enumerate(chips)]
        for cp in first:
            cp.start()
        passed = [copy(4 + j, (*chip, c), sibling) for j, chip in enumerate(chips)]
        for j, chip in enumerate(chips):
            copy(1 + j, (*chip, c), (x, y, c)).wait_recv()
            passed[j].start()
        copy(0, sibling, (x, y, c)).wait_recv()
        for j, chip in enumerate(chips):
            copy(4 + j, (*chip, 1 - c), (x, y, c)).wait_recv()
        for cp in first + passed:
            cp.wait_send()
        mine.wait()

    return pl.pallas_call(
        body, name=name, out_shape=jax.ShapeDtypeStruct((N_DEV,) + shard.shape, shard.dtype),
        in_specs=[_ANY], out_specs=_ANY,
        scratch_shapes=[pltpu.SemaphoreType.DMA((7,)), pltpu.SemaphoreType.DMA((7,)), pltpu.SemaphoreType.DMA(())],
    )(shard)


def _exchange_sibling(g, *, name):
    def body(g_ref, out_ref, send_sems, recv_sems):
        x, y, c, _ = _place()
        sibling = (x, y, 1 - c)
        copies = []
        for chip in range(4):
            cp = pltpu.make_async_remote_copy(
                src_ref=g_ref.at[2 * chip + (1 - c)], dst_ref=out_ref.at[chip],
                send_sem=send_sems.at[chip], recv_sem=recv_sems.at[chip], device_id=sibling, device_id_type=_MESH)
            cp.start()
            copies.append(cp)
        for cp in copies:
            cp.wait_recv()
        for cp in copies:
            cp.wait_send()

    return pl.pallas_call(
        body, name=name, out_shape=jax.ShapeDtypeStruct((4,) + g.shape[1:], g.dtype),
        in_specs=[_ANY], out_specs=_ANY,
        scratch_shapes=[pltpu.SemaphoreType.DMA((4,)), pltpu.SemaphoreType.DMA((4,))],
    )(g)


def _add_sibling(g, recv, *, name):
    _, r, cdim = g.shape
    tr = _row_tile(r, cdim, 4)
    g4 = g.reshape(4, 2, r, cdim)
    core = lax.axis_index("c").astype(jnp.int32).reshape(1)

    def body(c_ref, g_ref, r_ref, o_ref):
        o_ref[...] = (g_ref[...].astype(F32) + r_ref[...].astype(F32)).astype(o_ref.dtype)

    return pl.pallas_call(
        body, name=name,
        grid_spec=pltpu.PrefetchScalarGridSpec(
            num_scalar_prefetch=1, grid=(4, r // tr),
            in_specs=[pl.BlockSpec((None, None, tr, cdim), lambda ch, i, c_ref: (ch, c_ref[0], i, 0)),
                      pl.BlockSpec((None, tr, cdim), lambda ch, i, c_ref: (ch, i, 0))],
            out_specs=pl.BlockSpec((None, tr, cdim), lambda ch, i, c_ref: (ch, i, 0))),
        out_shape=jax.ShapeDtypeStruct((4, r, cdim), g.dtype),
        compiler_params=_params(("parallel", "parallel")),
    )(core, g4, recv)


def _exchange_chips(pch, *, name):
    def body(p_ref, out_ref, send_sems, recv_sems, local_sem):
        x, y, c, chips = _place()
        me = 2 * x + y
        mine = pltpu.make_async_copy(p_ref.at[me], out_ref.at[me], local_sem)
        mine.start()
        copies = []
        for j, (px, py) in enumerate(chips):
            cp = pltpu.make_async_remote_copy(
                src_ref=p_ref.at[2 * px + py], dst_ref=out_ref.at[me],
                send_sem=send_sems.at[j], recv_sem=recv_sems.at[j], device_id=(px, py, c), device_id_type=_MESH)
            cp.start()
            copies.append(cp)
        for j, (px, py) in enumerate(chips):
            pltpu.make_async_remote_copy(
                src_ref=p_ref.at[me], dst_ref=out_ref.at[2 * px + py],
                send_sem=send_sems.at[j], recv_sem=recv_sems.at[j], device_id=(px, py, c), device_id_type=_MESH).wait_recv()
        for cp in copies:
            cp.wait_send()
        mine.wait()

    return pl.pallas_call(
        body, name=name, out_shape=jax.ShapeDtypeStruct(pch.shape, pch.dtype),
        in_specs=[_ANY], out_specs=_ANY,
        scratch_shapes=[pltpu.SemaphoreType.DMA((3,)), pltpu.SemaphoreType.DMA((3,)), pltpu.SemaphoreType.DMA(())],
    )(pch)


def _all_reduce_small(v, *, name):
    r = v.shape[0]

    def body(x_ref, out_ref, buf_ref, send_sems, recv_sems):
        x, y, c, chips = _place()
        sibling = (x, y, 1 - c)

        def slot(px, py, pc):
            return buf_ref.at[4 * px + 2 * py + pc]

        def copy(k, block, to, src=None):
            return pltpu.make_async_remote_copy(
                src_ref=slot(*block) if src is None else src, dst_ref=slot(*block),
                send_sem=send_sems.at[k], recv_sem=recv_sems.at[k], device_id=to, device_id_type=_MESH)

        first = [copy(0, (x, y, c), sibling, src=x_ref)]
        first += [copy(1 + j, (x, y, c), (*chip, c), src=x_ref) for j, chip in enumerate(chips)]
        for cp in first:
            cp.start()
        buf_ref[4 * x + 2 * y + c] = x_ref[...]
        passed = [copy(4 + j, (*chip, c), sibling) for j, chip in enumerate(chips)]
        for j, chip in enumerate(chips):
            copy(1 + j, (*chip, c), (x, y, c)).wait_recv()
            passed[j].start()
        copy(0, sibling, (x, y, c)).wait_recv()
        for j, chip in enumerate(chips):
            copy(4 + j, (*chip, 1 - c), (x, y, c)).wait_recv()
        for cp in first + passed:
            cp.wait_send()
        total = buf_ref[0]
        for k in range(1, N_DEV):
            total = total + buf_ref[k]
        out_ref[...] = total

    vm = pl.BlockSpec(memory_space=pltpu.VMEM)
    return pl.pallas_call(
        body, name=name, out_shape=jax.ShapeDtypeStruct(v.shape, v.dtype),
        in_specs=[vm], out_specs=vm,
        scratch_shapes=[pltpu.VMEM((N_DEV,) + v.shape, v.dtype), pltpu.SemaphoreType.DMA((7,)),
                        pltpu.SemaphoreType.DMA((7,))],
    )(v)


def _row_tile(r, c, n_arrays):
    budget = (24 * 1024 * 1024) // (8 * n_arrays * max(c, 128))
    return _tile(r, max(16, budget), 16)


def _adam_math(w, g, m, v):
    m = ADAM_B1 * m + (1.0 - ADAM_B1) * g
    v = ADAM_B2 * v + (1.0 - ADAM_B2) * jnp.square(g)
    m_hat = m / (1.0 - ADAM_B1 ** ADAM_STEP)
    v_hat = v / (1.0 - ADAM_B2 ** ADAM_STEP)
    delta = -ADAM_LR * (m_hat / (jnp.sqrt(v_hat) + ADAM_EPS) + ADAM_WD * w)
    return delta, m, v


def _adam(w, g, m, v, *, name, parts=None):
    r, cdim = w.shape
    tr = _row_tile(r, cdim, 8)

    def body(w_ref, g_ref, m_ref, v_ref, go_ref, d_ref, mo_ref, vo_ref):
        if parts is None:
            g = g_ref[...]
        else:
            g = g_ref[0].astype(F32)
            for k in range(1, parts):
                g = g + g_ref[k].astype(F32)
        delta, m, v = _adam_math(w_ref[...], g, m_ref[...], v_ref[...])
        go_ref[...] = g
        d_ref[...] = delta
        mo_ref[...] = m
        vo_ref[...] = v

    blk = pl.BlockSpec((tr, cdim), lambda i: (i, 0))
    gblk = blk if parts is None else pl.BlockSpec((parts, tr, cdim), lambda i: (0, i, 0))
    out = jax.ShapeDtypeStruct((r, cdim), F32)
    return pl.pallas_call(
        body, name=name, grid=(r // tr,), in_specs=[blk, gblk, blk, blk], out_specs=[blk] * 4, out_shape=[out] * 4,
        compiler_params=_params(("parallel",)),
    )(w, g, m, v)


_BIG = ("w_in", "w_uq", "w_ukv", "w_o", "w_ffn_gate", "w_ffn_up", "w_ffn_down", "w_ple_gate", "w_ple_proj")
_WEIGHTS = ("w_in", "g_attn", "g_q_lora", "g_kv_lora", "w_uq", "w_ukv", "w_o", "g_ffn", "w_ffn_gate", "w_ffn_up",
            "conv_w", "conv_b", "w_ffn_down", "g_ple", "w_ple_gate", "w_ple_proj", "g_final")
_SMALL_PACK = (("g_attn", 1, D_MODEL), ("g_q_lora", 1, Q_LORA), ("g_kv_lora", 1, KV_LORA), ("g_ffn", 1, D_MODEL),
               ("conv_w", CONV_WIDTH, D_FF_PAD), ("conv_b", 1, D_FF_PAD), ("g_ple", 1, D_MODEL), ("g_final", 1, D_MODEL))


def _pack_small(gs):
    flat = jnp.concatenate([gs[n].reshape(-1) for n, _, _ in _SMALL_PACK])
    rows = -(-flat.shape[0] // 128)
    rows = -(-rows // 8) * 8
    return jnp.pad(flat, (0, rows * 128 - flat.shape[0])).reshape(rows, 128)


def _unpack_small(packed):
    flat = packed.reshape(-1)
    out, off = {}, 0
    for n, r, c in _SMALL_PACK:
        out[n] = flat[off:off + r * c].reshape(r, c)
        off += r * c
    return out


def kernel(x, p, w_in, g_attn, g_q_lora, g_kv_lora, w_uq, w_ukv, w_o, g_ffn, w_ffn_gate, w_ffn_up, conv_w, conv_b, w_ffn_down, g_ple, w_ple_gate, w_ple_proj, g_final, loss_target, m_w_in, m_g_attn, m_g_q_lora, m_g_kv_lora, m_w_uq, m_w_ukv, m_w_o, m_g_ffn, m_w_ffn_gate, m_w_ffn_up, m_conv_w, m_conv_b, m_w_ffn_down, m_g_ple, m_w_ple_gate, m_w_ple_proj, m_g_final, v_w_in, v_g_attn, v_g_q_lora, v_g_kv_lora, v_w_uq, v_w_ukv, v_w_o, v_g_ffn, v_w_ffn_gate, v_w_ffn_up, v_conv_w, v_conv_b, v_w_ffn_down, v_g_ple, v_w_ple_gate, v_w_ple_proj, v_g_final):
    given = dict(locals())
    wts = {n: given[n] for n in _WEIGHTS}
    mom = {n: given["m_" + n] for n in _WEIGHTS}
    var = {n: given["v_" + n] for n in _WEIGHTS}
    me = 4 * lax.axis_index("x") + 2 * lax.axis_index("y") + lax.axis_index("c")

    gathered = {n: _all_gather(wts[n][0].astype(MXU_DTYPE), name="ag_" + n) for n in _BIG}
    full = _prepare_weights(gathered)
    conv_full = _all_gather(conv_w, name="ag_conv_w")
    conv_full = conv_full[:, 0].transpose(1, 0, 2).reshape(CONV_WIDTH, D_FF)
    ffn_pad = ((0, 0), (0, D_FF_PAD - D_FF))
    vec = {"g_attn": g_attn, "g_q_lora": g_q_lora, "g_kv_lora": g_kv_lora, "g_ffn": g_ffn, "g_ple": g_ple,
           "g_final": g_final[None, :], "conv_w": jnp.pad(conv_full, ffn_pad), "conv_b": jnp.pad(conv_b, ffn_pad)}

    loss_part, grad_x, gw, gs = _local_step(x[0], p[0, 0], loss_target[0], full, vec)
    loss = lax.psum(loss_part[0, 0], ("x", "y", "c"))

    grads, deltas, new_m, new_v = {}, {}, {}, {}
    chunks = _split_grads(gw)
    for n in _BIG:
        shard = wts[n][0]
        r, cdim = shard.shape
        g3 = chunks[n].reshape(N_DEV, r, cdim)
        from_sibling = _exchange_sibling(g3, name="rs_sib_" + n)
        per_chip = _add_sibling(g3, from_sibling, name="rs_add_" + n)
        from_chips = _exchange_chips(per_chip, name="rs_chip_" + n)
        g, d, m2, v2 = _adam(shard, from_chips, mom[n][0], var[n][0], name="adam_" + n, parts=4)
        grads[n], deltas[n], new_m[n], new_v[n] = g[None], d[None], m2[None], v2[None]

    small = _unpack_small(_all_reduce_small(_pack_small(gs), name="ar_small"))
    cs = D_FF // N_DEV
    small_g = {
        "g_attn": small["g_attn"], "g_q_lora": small["g_q_lora"], "g_kv_lora": small["g_kv_lora"],
        "g_ffn": small["g_ffn"], "g_ple": small["g_ple"], "g_final": small["g_final"],
        "conv_b": small["conv_b"][:, :D_FF],
        "conv_w": lax.dynamic_slice(small["conv_w"], (jnp.zeros((), jnp.int32), (me * cs).astype(jnp.int32)), (CONV_WIDTH, cs)),
    }
    for n, g in small_g.items():
        shape = wts[n].shape
        as2d = lambda a: a.reshape(g.shape)
        go, d, m2, v2 = _adam(as2d(wts[n]), g, as2d(mom[n]), as2d(var[n]), name="adam_" + n)
        grads[n], deltas[n], new_m[n], new_v[n] = (a.reshape(shape) for a in (go, d, m2, v2))

    return (loss, grad_x[None], *[grads[n] for n in _WEIGHTS], *[deltas[n] for n in _WEIGHTS],
            *[new_m[n] for n in _WEIGHTS], *[new_v[n] for n in _WEIGHTS])
```

```python
import functools

import numpy as np

import jax
import jax.numpy as jnp
from jax import lax
from jax.experimental import pallas as pl
from jax.experimental.pallas import tpu as pltpu
from jax.experimental.pallas import tpu_sc as plsc

D_MODEL = 4096
CHUNK = 64
PLE_DIM = 256
RET_HEADS = 8
RET_HEAD_DIM = 256
RET_WIDTH = 2048
MLA_HEADS = 16
MLA_NOPE = 128
MLA_ROPE = 64
MLA_V = 128
Q_LORA = 1024
KV_LORA = 512
D_FF = 11008
CONV_WIDTH = 3
ROPE_BASE = 10000.0
EPS = 1e-6
IN_WIDTH = 9792
ADAM_LR, ADAM_B1, ADAM_B2, ADAM_EPS, ADAM_WD, ADAM_STEP = 0.001, 0.9, 0.999, 1e-08, 0.01, 10

IN_WIDTH_PAD = 10240
D_FF_PAD = 11264
MLA_QK_PAD = 256
Q_WIDTH_PAD = MLA_HEADS * MLA_QK_PAD

N_DEV = 8
MXU_DTYPE = jnp.bfloat16
ATTN_BLOCK = 512
VMEM_LIMIT = 56 * 1024 * 1024

F32 = jnp.float32


def _tile(n, want, align=128):
    if n <= want:
        return n
    t = (want // align) * align
    while t >= align:
        if n % t == 0:
            return t
        t -= align
    return n


def _params(sem):
    return pltpu.CompilerParams(dimension_semantics=sem, vmem_limit_bytes=VMEM_LIMIT)


def _sigmoid(x):
    return 1.0 / (1.0 + jnp.exp(-x))


def _matmul(a, b, *, name, ta=False, tb=False, out_dtype=F32, add=None, tm=1024, tn=512, tk=4096):
    m, k = (a.shape[1], a.shape[0]) if ta else a.shape
    k2, n = (b.shape[1], b.shape[0]) if tb else b.shape
    assert k == k2, (a.shape, b.shape, ta, tb)
    tm, tn, tk = _tile(m, tm), _tile(n, tn), _tile(k, tk)
    nk = k // tk
    dims = (((0 if ta else 1,), (1 if tb else 0,)), ((), ()))

    def body(*refs):
        a_ref, b_ref, o_ref = refs[0], refs[1], refs[3 if add is not None else 2]
        c_ref = refs[2] if add is not None else None
        part = lax.dot_general(a_ref[...].astype(MXU_DTYPE), b_ref[...].astype(MXU_DTYPE), dims,
                               preferred_element_type=F32)

        def finish(r):
            if c_ref is not None:
                r = r + c_ref[...].astype(F32)
            o_ref[...] = r.astype(out_dtype)

        if nk == 1:
            finish(part)
            return
        acc_ref = refs[-1]
        kk = pl.program_id(2)

        @pl.when(kk == 0)
        def _():
            acc_ref[...] = part

        @pl.when((kk > 0) & (kk < nk - 1))
        def _():
            acc_ref[...] += part

        @pl.when(kk == nk - 1)
        def _():
            finish(acc_ref[...] + part)

    a_spec = pl.BlockSpec((tk, tm), lambda i, j, kk: (kk, i)) if ta else pl.BlockSpec((tm, tk), lambda i, j, kk: (i, kk))
    b_spec = pl.BlockSpec((tn, tk), lambda i, j, kk: (j, kk)) if tb else pl.BlockSpec((tk, tn), lambda i, j, kk: (kk, j))
    in_specs = [a_spec, b_spec]
    args = [a, b]
    if add is not None:
        in_specs.append(pl.BlockSpec((tm, tn), lambda i, j, kk: (i, j)))
        args.append(add)
    return pl.pallas_call(
        body, name=name, grid=(m // tm, n // tn, nk),
        in_specs=in_specs, out_specs=pl.BlockSpec((tm, tn), lambda i, j, kk: (i, j)),
        out_shape=jax.ShapeDtypeStruct((m, n), out_dtype),
        scratch_shapes=[] if nk == 1 else [pltpu.VMEM((tm, tn), F32)],
        compiler_params=_params(("parallel", "parallel", "arbitrary")),
    )(*args)


def _rms_fwd(x, g, *, name):
    t, d = x.shape
    tr = _tile(t, 256, 8)

    def body(x_ref, g_ref, o_ref):
        xf = x_ref[...]
        r = lax.rsqrt(jnp.mean(xf * xf, axis=-1, keepdims=True) + EPS)
        o_ref[...] = (xf * r * g_ref[...]).astype(o_ref.dtype)

    return pl.pallas_call(
        body, name=name, grid=(t // tr,),
        in_specs=[pl.BlockSpec((tr, d), lambda i: (i, 0)), pl.BlockSpec((1, d), lambda i: (0, 0))],
        out_specs=pl.BlockSpec((tr, d), lambda i: (i, 0)),
        out_shape=jax.ShapeDtypeStruct((t, d), MXU_DTYPE),
        compiler_params=_params(("parallel",)),
    )(x, g)


def _rms_bwd(x, dhn, g, res, *, name, low_copy=False):
    t, d = x.shape
    tr = _tile(t, 256, 8)

    def body(*refs):
        if res is None:
            x_ref, dh_ref, g_ref = refs[:3]
            outs = refs[3:]
            res_ref = None
        else:
            x_ref, dh_ref, g_ref, res_ref = refs[:4]
            outs = refs[4:]
        dx_ref, dg_ref = outs[0], outs[-1]
        xf = x_ref[...]
        dh = dh_ref[...].astype(F32)
        r = lax.rsqrt(jnp.mean(xf * xf, axis=-1, keepdims=True) + EPS)
        dyg = dh * g_ref[...]
        dx = r * dyg - xf * (r * r * r) * jnp.mean(dyg * xf, axis=-1, keepdims=True)
        if res_ref is not None:
            dx = dx + res_ref[...]
        dx_ref[...] = dx
        if low_copy:
            outs[1][...] = dx.astype(outs[1].dtype)
        part = jnp.sum(dh * xf * r, axis=0, keepdims=True)

        @pl.when(pl.program_id(0) == 0)
        def _():
            dg_ref[...] = part

        @pl.when(pl.program_id(0) > 0)
        def _():
            dg_ref[...] += part

    row = pl.BlockSpec((tr, d), lambda i: (i, 0))
    vec = pl.BlockSpec((1, d), lambda i: (0, 0))
    in_specs = [row, row, vec] + ([] if res is None else [row])
    args = [x, dhn, g] + ([] if res is None else [res])
    out_specs = [row] + ([row] if low_copy else []) + [vec]
    out_shape = [jax.ShapeDtypeStruct((t, d), F32)] + ([jax.ShapeDtypeStruct((t, d), MXU_DTYPE)] if low_copy else []) \
        + [jax.ShapeDtypeStruct((1, d), F32)]
    return pl.pallas_call(
        body, name=name, grid=(t // tr,), in_specs=in_specs, out_specs=out_specs, out_shape=out_shape,
        compiler_params=_params(("arbitrary",)),
    )(*args)


def _rope_tables(t):
    pos = jnp.arange(t, dtype=F32)[:, None]
    inv_r = 1.0 / (ROPE_BASE ** (jnp.arange(0, RET_HEAD_DIM, 2, dtype=F32) / RET_HEAD_DIM))
    ang_r = pos * inv_r[None, :]
    inv_m = 1.0 / (ROPE_BASE ** (jnp.arange(0, MLA_ROPE, 2, dtype=F32) / MLA_ROPE))
    ang_m = pos * inv_m[None, :]
    cm, sm = jnp.cos(ang_m), jnp.sin(ang_m)
    z = jnp.zeros_like(cm)
    cos_m = jnp.concatenate([cm, cm, z, z], axis=1)
    sin_m = jnp.concatenate([-sm, sm, z, z], axis=1)
    return jnp.cos(ang_r), jnp.sin(ang_r), cos_m, sin_m


def _rope256(x, c, s, inverse=False):
    x1, x2 = x[:, :128], x[:, 128:]
    if inverse:
        s = -s
    return jnp.concatenate([x1 * c - x2 * s, x2 * c + x1 * s], axis=1)


def _rope64(x, cos_m, sin_m, inverse=False):
    lane = lax.broadcasted_iota(jnp.int32, x.shape, 1)
    partner = jnp.where(lane < 32, pltpu.roll(x, 96, 1), pltpu.roll(x, 32, 1))
    s = -sin_m if inverse else sin_m
    return x * cos_m + partner * s


def _ret_prep(proj, cos_r, sin_r):
    t = proj.shape[0]
    tr = _tile(t, 512, 8)
    hb = RET_WIDTH // RET_HEAD_DIM

    def body(q_ref, k_ref, v_ref, c_ref, s_ref, qo_ref, ko_ref, vo_ref):
        c, s = c_ref[...], s_ref[...]
        qo_ref[...] = _rope256(q_ref[...], c, s).astype(qo_ref.dtype)
        ko_ref[...] = (_rope256(k_ref[...], c, s) * (RET_HEAD_DIM ** -0.5)).astype(ko_ref.dtype)
        vo_ref[...] = v_ref[...].astype(vo_ref.dtype)

    head = lambda off: pl.BlockSpec((tr, RET_HEAD_DIM), lambda i, h: (i, h + off))
    tab = pl.BlockSpec((tr, 128), lambda i, h: (i, 0))
    out = jax.ShapeDtypeStruct((t, RET_WIDTH), MXU_DTYPE)
    return pl.pallas_call(
        body, name="ret_prep", grid=(t // tr, RET_HEADS),
        in_specs=[head(0), head(hb), head(2 * hb), tab, tab],
        out_specs=[head(0), head(0), head(0)], out_shape=[out, out, out],
        compiler_params=_params(("parallel", "parallel")),
    )(proj, proj, proj, cos_r, sin_r)


def _ret_log_gamma():
    return jnp.asarray(np.log1p(-np.exp2(-5.0 - np.arange(RET_HEADS, dtype=np.float64))), dtype=F32)


def _decay_full(lg, i, j, blk):
    r = lax.broadcasted_iota(jnp.int32, (blk, 1), 0).astype(F32)
    c = lax.broadcasted_iota(jnp.int32, (1, blk), 1).astype(F32)
    off = ((i - j) * blk).astype(F32)
    return jnp.exp(lg * r), jnp.exp(lg * (off - c))


def _decay_diag(lg, blk):
    r = lax.broadcasted_iota(jnp.int32, (blk, blk), 0)
    c = lax.broadcasted_iota(jnp.int32, (blk, blk), 1)
    ok = (c // CHUNK) <= (r // CHUNK)
    return jnp.where(ok, jnp.exp(lg * jnp.abs(r - c).astype(F32)), 0.0)


_NT = (((1,), (1,)), ((), ()))
_TN = (((0,), (0,)), ((), ()))
_NN = (((1,), (0,)), ((), ()))


def _causal_pairs(nb, query_major):
    if query_major:
        pairs = [(i, j) for i in range(nb) for j in range(i + 1)]
    else:
        pairs = [(i, j) for j in range(nb) for i in range(j, nb)]
    arr = np.asarray(pairs, dtype=np.int32)
    return jnp.asarray(arr[:, 0]), jnp.asarray(arr[:, 1])


def _ret_fwd(q, k, v, proj, lg):
    t = q.shape[0]
    blk = _tile(t, ATTN_BLOCK)
    nb = t // blk
    gate_off = 3 * RET_WIDTH // RET_HEAD_DIM

    def body(ii_ref, jj_ref, lg_ref, q_ref, k_ref, v_ref, g_ref, raw_ref, ro_ref, acc_ref):
        h, pair = pl.program_id(0), pl.program_id(1)
        i, j = ii_ref[pair], jj_ref[pair]
        lgh = lg_ref[h]

        @pl.when(j == 0)
        def _():
            acc_ref[...] = jnp.zeros_like(acc_ref)

        def accumulate(w):
            acc_ref[...] += lax.dot_general(w.astype(MXU_DTYPE), v_ref[...], _NN, preferred_element_type=F32)

        @pl.when(j < i)
        def _():
            s = lax.dot_general(q_ref[...], k_ref[...], _NT, preferred_element_type=F32)
            a, b = _decay_full(lgh, i, j, blk)
            accumulate(s * a * b)

        @pl.when(j == i)
        def _():
            s = lax.dot_general(q_ref[...], k_ref[...], _NT, preferred_element_type=F32)
            accumulate(s * _decay_diag(lgh, blk))
            o = acc_ref[...]
            raw_ref[...] = o
            mu = jnp.mean(o, axis=-1, keepdims=True)
            var = jnp.mean(jnp.square(o - mu), axis=-1, keepdims=True)
            hn = (o - mu) * lax.rsqrt(var + EPS)
            g = g_ref[...]
            ro_ref[...] = (g * _sigmoid(g) * hn).astype(ro_ref.dtype)

    qs = pl.BlockSpec((blk, RET_HEAD_DIM), lambda h, p, ii, jj: (ii[p], h))
    ks = pl.BlockSpec((blk, RET_HEAD_DIM), lambda h, p, ii, jj: (jj[p], h))
    gs = pl.BlockSpec((blk, RET_HEAD_DIM), lambda h, p, ii, jj: (ii[p], h + gate_off))
    ii, jj = _causal_pairs(nb, query_major=True)
    return pl.pallas_call(
        body, name="ret_fwd",
        grid_spec=pltpu.PrefetchScalarGridSpec(
            num_scalar_prefetch=2, grid=(RET_HEADS, ii.shape[0]),
            in_specs=[pl.BlockSpec(memory_space=pltpu.SMEM), qs, ks, ks, gs], out_specs=[qs, qs],
            scratch_shapes=[pltpu.VMEM((blk, RET_HEAD_DIM), F32)]),
        out_shape=[jax.ShapeDtypeStruct((t, RET_WIDTH), F32), jax.ShapeDtypeStruct((t, RET_WIDTH), MXU_DTYPE)],
        compiler_params=_params(("parallel", "arbitrary")),
    )(ii, jj, lg, q, k, v, proj)


def _ret_gate_bwd(raw, proj, dattn):
    t = raw.shape[0]
    tr = _tile(t, 512, 8)
    gate_off = 3 * RET_WIDTH // RET_HEAD_DIM

    def body(o_ref, g_ref, d_ref, do_ref, dg_ref):
        o, g, d = o_ref[...], g_ref[...], d_ref[...]
        mu = jnp.mean(o, axis=-1, keepdims=True)
        rstd = lax.rsqrt(jnp.mean(jnp.square(o - mu), axis=-1, keepdims=True) + EPS)
        hn = (o - mu) * rstd
        sg = _sigmoid(g)
        dg_ref[...] = (d * hn * (sg * (1.0 + g * (1.0 - sg)))).astype(dg_ref.dtype)
        dhn = d * (g * sg)
        do = rstd * (dhn - jnp.mean(dhn, axis=-1, keepdims=True) - hn * jnp.mean(dhn * hn, axis=-1, keepdims=True))
        do_ref[...] = do.astype(do_ref.dtype)

    hs = pl.BlockSpec((tr, RET_HEAD_DIM), lambda i, h: (i, h))
    gs = pl.BlockSpec((tr, RET_HEAD_DIM), lambda i, h: (i, h + gate_off))
    out = jax.ShapeDtypeStruct((t, RET_WIDTH), MXU_DTYPE)
    return pl.pallas_call(
        body, name="ret_gate_bwd", grid=(t // tr, RET_HEADS),
        in_specs=[hs, gs, hs], out_specs=[hs, hs], out_shape=[out, out],
        compiler_params=_params(("parallel", "parallel")),
    )(raw, proj, dattn)


def _ret_bwd(q, k, v, do, lg):
    t = q.shape[0]
    blk = _tile(t, ATTN_BLOCK)
    nb = t // blk

    def body(ii_ref, jj_ref, lg_ref, q_ref, k_ref, v_ref, do_ref, dq_ref, dk_ref, dv_ref, dk_acc, dv_acc):
        h, pair = pl.program_id(0), pl.program_id(1)
        i, j = ii_ref[pair], jj_ref[pair]
        lgh = lg_ref[h]

        @pl.when(pair == 0)
        def _():
            dq_ref[...] = jnp.zeros_like(dq_ref)

        @pl.when(i == j)
        def _():
            dk_acc[...] = jnp.zeros_like(dk_acc)
            dv_acc[...] = jnp.zeros_like(dv_acc)

        def step(decay):
            qb, kb, vb, dob = q_ref[...], k_ref[...], v_ref[...], do_ref[...]
            s = lax.dot_general(qb, kb, _NT, preferred_element_type=F32)
            w = (s * decay).astype(MXU_DTYPE)
            dv_acc[...] += lax.dot_general(w, dob, _TN, preferred_element_type=F32)
            dw = lax.dot_general(dob, vb, _NT, preferred_element_type=F32)
            ds = (dw * decay).astype(MXU_DTYPE)
            rows = pl.ds(pl.multiple_of(i * blk, blk), blk)
            dq_ref[rows, :] += lax.dot_general(ds, kb, _NN, preferred_element_type=F32)
            dk_acc[...] += lax.dot_general(ds, qb, _TN, preferred_element_type=F32)

        @pl.when(i > j)
        def _():
            a, b = _decay_full(lgh, i, j, blk)
            step(a * b)

        @pl.when(i == j)
        def _():
            step(_decay_diag(lgh, blk))

        @pl.when(i == nb - 1)
        def _():
            dk_ref[...] = dk_acc[...]
            dv_ref[...] = dv_acc[...].astype(dv_ref.dtype)

    qs = pl.BlockSpec((blk, RET_HEAD_DIM), lambda h, p, ii, jj: (ii[p], h))
    ks = pl.BlockSpec((blk, RET_HEAD_DIM), lambda h, p, ii, jj: (jj[p], h))
    ii, jj = _causal_pairs(nb, query_major=False)
    return pl.pallas_call(
        body, name="ret_bwd",
        grid_spec=pltpu.PrefetchScalarGridSpec(
            num_scalar_prefetch=2, grid=(RET_HEADS, ii.shape[0]),
            in_specs=[pl.BlockSpec(memory_space=pltpu.SMEM), qs, ks, ks, qs],
            out_specs=[pl.BlockSpec((t, RET_HEAD_DIM), lambda h, p, ii, jj: (0, h)), ks, ks],
            scratch_shapes=[pltpu.VMEM((blk, RET_HEAD_DIM), F32), pltpu.VMEM((blk, RET_HEAD_DIM), F32)]),
        out_shape=[jax.ShapeDtypeStruct((t, RET_WIDTH), F32), jax.ShapeDtypeStruct((t, RET_WIDTH), F32),
                   jax.ShapeDtypeStruct((t, RET_WIDTH), MXU_DTYPE)],
        compiler_params=_params(("parallel", "arbitrary")),
    )(ii, jj, lg, q, k, v, do)


def _ret_unrope(dq, dk, cos_r, sin_r):
    t = dq.shape[0]
    tr = _tile(t, 512, 8)

    def body(dq_ref, dk_ref, c_ref, s_ref, oq_ref, ok_ref):
        c, s = c_ref[...], s_ref[...]
        oq_ref[...] = _rope256(dq_ref[...], c, s, inverse=True).astype(oq_ref.dtype)
        ok_ref[...] = (_rope256(dk_ref[...], c, s, inverse=True) * (RET_HEAD_DIM ** -0.5)).astype(ok_ref.dtype)

    hs = pl.BlockSpec((tr, RET_HEAD_DIM), lambda i, h: (i, h))
    tab = pl.BlockSpec((tr, 128), lambda i, h: (i, 0))
    out = jax.ShapeDtypeStruct((t, RET_WIDTH), MXU_DTYPE)
    return pl.pallas_call(
        body, name="ret_unrope", grid=(t // tr, RET_HEADS),
        in_specs=[hs, hs, tab, tab], out_specs=[hs, hs], out_shape=[out, out],
        compiler_params=_params(("parallel", "parallel")),
    )(dq, dk, cos_r, sin_r)


def _mla_prep(cq, ckv, kr, g_q, g_kv, cos_m, sin_m):
    t = cq.shape[0]
    tr = _tile(t, 512, 8)

    def body(cq_ref, ckv_ref, kr_ref, gq_ref, gkv_ref, c_ref, s_ref, cqn_ref, kvn_ref, kro_ref):
        for x_ref, g_ref, o_ref in ((cq_ref, gq_ref, cqn_ref), (ckv_ref, gkv_ref, kvn_ref)):
            xf = x_ref[...]
            r = lax.rsqrt(jnp.mean(xf * xf, axis=-1, keepdims=True) + EPS)
            o_ref[...] = (xf * r * g_ref[...]).astype(o_ref.dtype)
        kro_ref[...] = _rope64(kr_ref[...], c_ref[...], s_ref[...]).astype(kro_ref.dtype)

    row = lambda w: pl.BlockSpec((tr, w), lambda i: (i, 0))
    vec = lambda w: pl.BlockSpec((1, w), lambda i: (0, 0))
    return pl.pallas_call(
        body, name="mla_prep", grid=(t // tr,),
        in_specs=[row(Q_LORA), row(KV_LORA), row(128), vec(Q_LORA), vec(KV_LORA), row(128), row(128)],
        out_specs=[row(Q_LORA), row(KV_LORA), row(128)],
        out_shape=[jax.ShapeDtypeStruct((t, Q_LORA), MXU_DTYPE), jax.ShapeDtypeStruct((t, KV_LORA), MXU_DTYPE),
                   jax.ShapeDtypeStruct((t, 128), MXU_DTYPE)],
        compiler_params=_params(("parallel",)),
    )(cq, ckv, kr, g_q, g_kv, cos_m, sin_m)


def _mla_q_rope(q_lin, cos_m, sin_m, *, inverse, name):
    t = q_lin.shape[0]
    tr = _tile(t, 512, 8)

    def body(q_ref, c_ref, s_ref, o_ref):
        x = q_ref[...].astype(F32)
        roped = _rope64(x[:, 128:], c_ref[...], s_ref[...], inverse=inverse)
        o_ref[...] = jnp.concatenate([x[:, :128], roped], axis=1).astype(o_ref.dtype)

    hs = pl.BlockSpec((tr, MLA_QK_PAD), lambda i, h: (i, h))
    tab = pl.BlockSpec((tr, 128), lambda i, h: (i, 0))
    return pl.pallas_call(
        body, name=name, grid=(t // tr, MLA_HEADS),
        in_specs=[hs, tab, tab], out_specs=hs, out_shape=jax.ShapeDtypeStruct((t, Q_WIDTH_PAD), MXU_DTYPE),
        compiler_params=_params(("parallel", "parallel")),
    )(q_lin, cos_m, sin_m)


_MLA_SCALE = (MLA_NOPE + MLA_ROPE) ** -0.5
_NEG = -1e30


def _mla_mask(blk):
    r = lax.broadcasted_iota(jnp.int32, (blk, blk), 0)
    c = lax.broadcasted_iota(jnp.int32, (blk, blk), 1)
    return (c // CHUNK) <= (r // CHUNK)


def _mla_fwd(q, kv, kr):
    t = q.shape[0]
    blk = _tile(t, ATTN_BLOCK)
    nb = t // blk

    def body(ii_ref, jj_ref, q_ref, kn_ref, v_ref, kr_ref, o_ref, lse_ref, m_ref, l_ref, acc_ref):
        pair = pl.program_id(1)
        i, j = ii_ref[pair], jj_ref[pair]

        @pl.when(j == 0)
        def _():
            m_ref[...] = jnp.full_like(m_ref, _NEG)
            l_ref[...] = jnp.zeros_like(l_ref)
            acc_ref[...] = jnp.zeros_like(acc_ref)

        def step(masked):
            kb = jnp.concatenate([kn_ref[...], kr_ref[...]], axis=1)
            s = lax.dot_general(q_ref[...], kb, _NT, preferred_element_type=F32) * _MLA_SCALE
            if masked:
                s = jnp.where(_mla_mask(blk), s, _NEG)
            m_prev = m_ref[...]
            m_new = jnp.maximum(m_prev, jnp.max(s, axis=-1, keepdims=True))
            alpha = jnp.exp(m_prev - m_new)
            p = jnp.exp(s - m_new)
            l_ref[...] = alpha * l_ref[...] + jnp.sum(p, axis=-1, keepdims=True)
            acc_ref[...] = alpha * acc_ref[...] + lax.dot_general(p.astype(MXU_DTYPE), v_ref[...], _NN,
                                                                  preferred_element_type=F32)
            m_ref[...] = m_new

        @pl.when(j < i)
        def _():
            step(False)

        @pl.when(j == i)
        def _():
            step(True)
            o_ref[...] = (acc_ref[...] / l_ref[...]).astype(o_ref.dtype)
            lse_ref[...] = jnp.broadcast_to(m_ref[...] + jnp.log(l_ref[...]), lse_ref.shape)

    kmap = lambda off: (lambda h, p, ii, jj: (jj[p], 2 * h + off))
    os_ = pl.BlockSpec((blk, 128), lambda h, p, ii, jj: (ii[p], h))
    ii, jj = _causal_pairs(nb, query_major=True)
    return pl.pallas_call(
        body, name="mla_fwd",
        grid_spec=pltpu.PrefetchScalarGridSpec(
            num_scalar_prefetch=2, grid=(MLA_HEADS, ii.shape[0]),
            in_specs=[pl.BlockSpec((blk, MLA_QK_PAD), lambda h, p, ii, jj: (ii[p], h)),
                      pl.BlockSpec((blk, 128), kmap(0)), pl.BlockSpec((blk, 128), kmap(1)),
                      pl.BlockSpec((blk, 128), lambda h, p, ii, jj: (jj[p], 0))],
            out_specs=[os_, os_],
            scratch_shapes=[pltpu.VMEM((blk, 1), F32), pltpu.VMEM((blk, 1), F32), pltpu.VMEM((blk, MLA_V), F32)]),
        out_shape=[jax.ShapeDtypeStruct((t, MLA_HEADS * MLA_V), MXU_DTYPE),
                   jax.ShapeDtypeStruct((t, MLA_HEADS * 128), F32)],
        compiler_params=_params(("parallel", "arbitrary")),
    )(ii, jj, q, kv, kv, kr)


def _mla_bwd(q, kv, kr, o, lse, dattn):
    t = q.shape[0]
    blk = _tile(t, ATTN_BLOCK)
    nb = t // blk
    do_off = RET_WIDTH // 128

    def body(ii_ref, jj_ref, q_ref, kn_ref, v_ref, kr_ref, o_ref, lse_ref, do_ref, dq_ref, dk_ref, dv_ref, dk_acc, dv_acc):
        pair = pl.program_id(1)
        i, j = ii_ref[pair], jj_ref[pair]

        @pl.when(pair == 0)
        def _():
            dq_ref[...] = jnp.zeros_like(dq_ref)

        @pl.when(i == j)
        def _():
            dk_acc[...] = jnp.zeros_like(dk_acc)
            dv_acc[...] = jnp.zeros_like(dv_acc)

        def step(masked):
            qb, vb = q_ref[...], v_ref[...]
            kb = jnp.concatenate([kn_ref[...], kr_ref[...]], axis=1)
            dof = do_ref[...]
            dob = dof.astype(MXU_DTYPE)
            s = lax.dot_general(qb, kb, _NT, preferred_element_type=F32) * _MLA_SCALE
            if masked:
                s = jnp.where(_mla_mask(blk), s, _NEG)
            p = jnp.exp(s - lse_ref[...][:, :1])
            delta = jnp.sum(dof * o_ref[...].astype(F32), axis=-1, keepdims=True)
            dv_acc[...] += lax.dot_general(p.astype(MXU_DTYPE), dob, _TN, preferred_element_type=F32)
            dp = lax.dot_general(dob, vb, _NT, preferred_element_type=F32)
            ds = (p * (dp - delta) * _MLA_SCALE).astype(MXU_DTYPE)
            rows = pl.ds(pl.multiple_of(i * blk, blk), blk)
            dq_ref[rows, :] += lax.dot_general(ds, kb, _NN, preferred_element_type=F32)
            dk_acc[...] += lax.dot_general(ds, qb, _TN, preferred_element_type=F32)

        @pl.when(i > j)
        def _():
            step(False)

        @pl.when(i == j)
        def _():
            step(True)

        @pl.when(i == nb - 1)
        def _():
            dk_ref[...] = dk_acc[...]
            dv_ref[...] = dv_acc[...].astype(dv_ref.dtype)

    qmap = lambda off: (lambda h, p, ii, jj: (ii[p], h + off))
    kmap = lambda off: (lambda h, p, ii, jj: (jj[p], 2 * h + off))
    ii, jj = _causal_pairs(nb, query_major=False)
    return pl.pallas_call(
        body, name="mla_bwd",
        grid_spec=pltpu.PrefetchScalarGridSpec(
            num_scalar_prefetch=2, grid=(MLA_HEADS, ii.shape[0]),
            in_specs=[pl.BlockSpec((blk, MLA_QK_PAD), qmap(0)),
                      pl.BlockSpec((blk, 128), kmap(0)), pl.BlockSpec((blk, 128), kmap(1)),
                      pl.BlockSpec((blk, 128), lambda h, p, ii, jj: (jj[p], 0)),
                      pl.BlockSpec((blk, 128), qmap(0)), pl.BlockSpec((blk, 128), qmap(0)),
                      pl.BlockSpec((blk, 128), qmap(do_off))],
            out_specs=[pl.BlockSpec((t, MLA_QK_PAD), lambda h, p, ii, jj: (0, h)),
                       pl.BlockSpec((blk, MLA_QK_PAD), lambda h, p, ii, jj: (jj[p], h)),
                       pl.BlockSpec((blk, MLA_V), lambda h, p, ii, jj: (jj[p], h))],
            scratch_shapes=[pltpu.VMEM((blk, MLA_QK_PAD), F32), pltpu.VMEM((blk, MLA_V), F32)]),
        out_shape=[jax.ShapeDtypeStruct((t, Q_WIDTH_PAD), F32), jax.ShapeDtypeStruct((t, Q_WIDTH_PAD), F32),
                   jax.ShapeDtypeStruct((t, MLA_HEADS * MLA_V), MXU_DTYPE)],
        compiler_params=_params(("parallel", "arbitrary")),
    )(ii, jj, q, kv, kv, kr, o, lse, dattn)


def _mla_kv_grad(dk, dv, cos_m, sin_m):
    t = dk.shape[0]
    tr = _tile(t, 256, 8)

    def body(dk_ref, dv_ref, c_ref, s_ref, dkv_ref, dkr_ref):
        acc = jnp.zeros((tr, 128), F32)
        for h in range(MLA_HEADS):
            dkv_ref[:, h * 256:h * 256 + 128] = dk_ref[:, h * 256:h * 256 + 128].astype(dkv_ref.dtype)
            dkv_ref[:, h * 256 + 128:h * 256 + 256] = dv_ref[:, h * 128:(h + 1) * 128].astype(dkv_ref.dtype)
            acc = acc + dk_ref[:, h * 256 + 128:h * 256 + 256]
        dkr_ref[...] = _rope64(acc, c_ref[...], s_ref[...], inverse=True).astype(dkr_ref.dtype)

    row = lambda w: pl.BlockSpec((tr, w), lambda i: (i, 0))
    return pl.pallas_call(
        body, name="mla_kv_grad", grid=(t // tr,),
        in_specs=[row(Q_WIDTH_PAD), row(MLA_HEADS * MLA_V), row(128), row(128)],
        out_specs=[row(Q_WIDTH_PAD), row(128)],
        out_shape=[jax.ShapeDtypeStruct((t, Q_WIDTH_PAD), MXU_DTYPE), jax.ShapeDtypeStruct((t, 128), MXU_DTYPE)],
        compiler_params=_params(("parallel",)),
    )(dk, dv, cos_m, sin_m)


_FFN_COLS = 256
_FFN_ROWS = 256


def _shift_down(cur, prev8, n):
    row = lax.broadcasted_iota(jnp.int32, cur.shape, 0)
    out = pltpu.roll(cur, n, 0)
    for r in range(n):
        out = jnp.where(row == r, prev8[8 - n + r:8 - n + r + 1, :], out)
    return out


def _shift_up(cur, next8, n):
    rows = cur.shape[0]
    row = lax.broadcasted_iota(jnp.int32, cur.shape, 0)
    out = pltpu.roll(cur, rows - n, 0)
    for r in range(n):
        out = jnp.where(row == rows - n + r, next8[r:r + 1, :], out)
    return out


def _conv_pre(g_ref, cw_ref, cb_ref, c, rc):
    r0 = pl.multiple_of(c * rc, rc)
    cur = g_ref[pl.ds(r0, rc), :]
    prev8 = g_ref[pl.ds(pl.multiple_of(jnp.maximum(r0 - 8, 0), 8), 8), :]
    prev8 = jnp.where(c > 0, prev8, 0.0)
    s1, s2 = _shift_down(cur, prev8, 1), _shift_down(cur, prev8, 2)
    a = cb_ref[...] + cw_ref[2:3, :] * cur + cw_ref[1:2, :] * s1 + cw_ref[0:1, :] * s2
    return r0, cur, s1, s2, a


def _ffn_act_fwd(gpre, u, cw, cb):
    t, f = gpre.shape
    tc = _tile(f, _FFN_COLS)
    rc = _tile(t, _FFN_ROWS, 8)

    def body(g_ref, u_ref, cw_ref, cb_ref, o_ref):
        def chunk(c, carry):
            r0, _, _, _, a = _conv_pre(g_ref, cw_ref, cb_ref, c, rc)
            o_ref[pl.ds(r0, rc), :] = (a * _sigmoid(a) * u_ref[pl.ds(r0, rc), :]).astype(o_ref.dtype)
            return carry
        lax.fori_loop(0, t // rc, chunk, 0)

    col = pl.BlockSpec((t, tc), lambda j: (0, j))
    return pl.pallas_call(
        body, name="ffn_act_fwd", grid=(f // tc,),
        in_specs=[col, col, pl.BlockSpec((CONV_WIDTH, tc), lambda j: (0, j)), pl.BlockSpec((1, tc), lambda j: (0, j))],
        out_specs=col, out_shape=jax.ShapeDtypeStruct((t, f), MXU_DTYPE),
        compiler_params=_params(("parallel",)),
    )(gpre, u, cw, cb)


def _ffn_act_bwd(gpre, u, dact, cw, cb):
    t, f = gpre.shape
    tc = _tile(f, _FFN_COLS)
    rc = _tile(t, _FFN_ROWS, 8)
    nc = t // rc

    def body(g_ref, u_ref, d_ref, cw_ref, cb_ref, dg_ref, du_ref, dcw_ref, dcb_ref, da_ref):
        def chunk(c, carry):
            w0, w1, w2, b = carry
            r0, cur, s1, s2, a = _conv_pre(g_ref, cw_ref, cb_ref, c, rc)
            sg = _sigmoid(a)
            d = d_ref[pl.ds(r0, rc), :]
            du_ref[pl.ds(r0, rc), :] = (d * (a * sg)).astype(du_ref.dtype)
            da = d * u_ref[pl.ds(r0, rc), :] * (sg * (1.0 + a * (1.0 - sg)))
            da_ref[pl.ds(r0, rc), :] = da
            return (w0 + jnp.sum(da * s2, axis=0, keepdims=True), w1 + jnp.sum(da * s1, axis=0, keepdims=True),
                    w2 + jnp.sum(da * cur, axis=0, keepdims=True), b + jnp.sum(da, axis=0, keepdims=True))
        z = jnp.zeros((1, tc), F32)
        w0, w1, w2, b = lax.fori_loop(0, nc, chunk, (z, z, z, z))
        dcw_ref[0:1, :] = w0
        dcw_ref[1:2, :] = w1
        dcw_ref[2:3, :] = w2
        dcb_ref[...] = b

        def chunk2(c, carry):
            r0 = pl.multiple_of(c * rc, rc)
            cur = da_ref[pl.ds(r0, rc), :]
            nxt = da_ref[pl.ds(pl.multiple_of(jnp.minimum(r0 + rc, t - 8), 8), 8), :]
            nxt = jnp.where(c < nc - 1, nxt, 0.0)
            dg = cw_ref[2:3, :] * cur + cw_ref[1:2, :] * _shift_up(cur, nxt, 1) + cw_ref[0:1, :] * _shift_up(cur, nxt, 2)
            dg_ref[pl.ds(r0, rc), :] = dg.astype(dg_ref.dtype)
            return carry
        lax.fori_loop(0, nc, chunk2, 0)

    col = pl.BlockSpec((t, tc), lambda j: (0, j))
    w3 = pl.BlockSpec((CONV_WIDTH, tc), lambda j: (0, j))
    w1 = pl.BlockSpec((1, tc), lambda j: (0, j))
    low = jax.ShapeDtypeStruct((t, f), MXU_DTYPE)
    return pl.pallas_call(
        body, name="ffn_act_bwd", grid=(f // tc,),
        in_specs=[col, col, col, w3, w1], out_specs=[col, col, w3, w1],
        out_shape=[low, low, jax.ShapeDtypeStruct((CONV_WIDTH, f), F32), jax.ShapeDtypeStruct((1, f), F32)],
        scratch_shapes=[pltpu.VMEM((t, tc), F32)],
        compiler_params=_params(("parallel",)),
    )(gpre, u, dact, cw, cb)


def _head_fwd_bwd(h2, glin, pp, target, g_final):
    t, d = h2.shape
    tr = _tile(t, 128, 8)

    def body(h_ref, gl_ref, pp_ref, t_ref, g_ref, loss_ref, dh_ref, dgl_ref, dpp_ref, dg_ref):
        gate = _sigmoid(gl_ref[...])
        ppv = pp_ref[...]
        h3 = h_ref[...] + gate * ppv
        r = lax.rsqrt(jnp.mean(h3 * h3, axis=-1, keepdims=True) + EPS)
        yh = h3 * r
        g = g_ref[...]
        diff = yh * g - t_ref[...]
        lpart = 0.5 * jnp.sum(jnp.mean(diff * diff, axis=-1, keepdims=True), axis=0, keepdims=True)
        dy = diff * (1.0 / d)
        dyg = dy * g
        dh3 = r * dyg - h3 * (r * r * r) * jnp.mean(dyg * h3, axis=-1, keepdims=True)
        dh_ref[...] = dh3
        dgl_ref[...] = (dh3 * ppv * gate * (1.0 - gate)).astype(dgl_ref.dtype)
        dpp_ref[...] = (dh3 * gate).astype(dpp_ref.dtype)
        dgp = jnp.sum(dy * yh, axis=0, keepdims=True)

        @pl.when(pl.program_id(0) == 0)
        def _():
            loss_ref[...] = jnp.broadcast_to(lpart, loss_ref.shape)
            dg_ref[...] = dgp

        @pl.when(pl.program_id(0) > 0)
        def _():
            loss_ref[...] += jnp.broadcast_to(lpart, loss_ref.shape)
            dg_ref[...] += dgp

    row = pl.BlockSpec((tr, d), lambda i: (i, 0))
    vec = pl.BlockSpec((1, d), lambda i: (0, 0))
    low = jax.ShapeDtypeStruct((t, d), MXU_DTYPE)
    return pl.pallas_call(
        body, name="head_fwd_bwd", grid=(t // tr,),
        in_specs=[row, row, row, row, vec],
        out_specs=[pl.BlockSpec((8, 128), lambda i: (0, 0)), row, row, row, vec],
        out_shape=[jax.ShapeDtypeStruct((8, 128), F32), jax.ShapeDtypeStruct((t, d), F32), low, low,
                   jax.ShapeDtypeStruct((1, d), F32)],
        compiler_params=_params(("arbitrary",)),
    )(h2, glin, pp, target, g_final)


def _local_step(x, p, target, w, vec):
    t = x.shape[0]
    cos_r, sin_r, cos_m, sin_m = _rope_tables(t)
    lg = _ret_log_gamma()
    low = MXU_DTYPE

    hn1 = _rms_fwd(x, vec["g_attn"], name="rms1_fwd")
    proj = _matmul(hn1, w["w_in"], name="mm_proj")
    rq, rk, rv = _ret_prep(proj, cos_r, sin_r)
    ret_raw, ro = _ret_fwd(rq, rk, rv, proj, lg)
    c0 = 4 * RET_WIDTH
    cq = proj[:, c0:c0 + Q_LORA]
    ckv = proj[:, c0 + Q_LORA:c0 + Q_LORA + KV_LORA]
    kr_in = proj[:, c0 + Q_LORA + KV_LORA:c0 + Q_LORA + KV_LORA + 128]
    cqn, kvn, kr = _mla_prep(cq, ckv, kr_in, vec["g_q_lora"], vec["g_kv_lora"], cos_m, sin_m)
    q_lin = _matmul(cqn, w["w_uq"], name="mm_q")
    q = _mla_q_rope(q_lin, cos_m, sin_m, inverse=False, name="mla_q_rope")
    kv = _matmul(kvn, w["w_ukv"], name="mm_kv", out_dtype=low)
    mo, lse = _mla_fwd(q, kv, kr)
    attn = jnp.concatenate([ro, mo], axis=1)
    h1 = _matmul(attn, w["w_o"], name="mm_o", add=x)
    hn2 = _rms_fwd(h1, vec["g_ffn"], name="rms2_fwd")
    gpre = _matmul(hn2, w["w_ffn_gate"], name="mm_gate")
    u = _matmul(hn2, w["w_ffn_up"], name="mm_up")
    act = _ffn_act_fwd(gpre, u, vec["conv_w"], vec["conv_b"])
    h2 = _matmul(act, w["w_ffn_down"], name="mm_down", add=h1)
    hn3 = _rms_fwd(h2, vec["g_ple"], name="rms3_fwd")
    glin = _matmul(hn3, w["w_ple_gate"], name="mm_ple_gate")
    p_low = p.astype(low)
    pp = _matmul(p_low, w["w_ple_proj"], name="mm_ple_proj")
    loss_part, dh3, dglin, dpp, dg_final = _head_fwd_bwd(h2, glin, pp, target, vec["g_final"])

    gw = {}
    gw["w_ple_proj"] = _matmul(p_low, dpp, ta=True, name="mm_d_ple_proj", out_dtype=low)
    gw["w_ple_gate"] = _matmul(hn3, dglin, ta=True, name="mm_d_ple_gate", out_dtype=low)
    dhn3 = _matmul(dglin, w["w_ple_gate"], tb=True, name="mm_dhn3")
    dh2, dh2_low, dg_ple = _rms_bwd(h2, dhn3, vec["g_ple"], dh3, name="rms3_bwd", low_copy=True)
    gw["w_ffn_down"] = _matmul(act, dh2_low, ta=True, name="mm_d_down", out_dtype=low)
    dact = _matmul(dh2_low, w["w_ffn_down"], tb=True, name="mm_dact")
    dgpre, du, dconv_w, dconv_b = _ffn_act_bwd(gpre, u, dact, vec["conv_w"], vec["conv_b"])
    gw["w_ffn_gate"] = _matmul(hn2, dgpre, ta=True, name="mm_d_gate", out_dtype=low)
    gw["w_ffn_up"] = _matmul(hn2, du, ta=True, name="mm_d_up", out_dtype=low)
    dhn2 = _matmul(dgpre, w["w_ffn_gate"], tb=True, name="mm_dhn2_a")
    dhn2 = _matmul(du, w["w_ffn_up"], tb=True, name="mm_dhn2_b", add=dhn2)
    dh1, dh1_low, dg_ffn = _rms_bwd(h1, dhn2, vec["g_ffn"], dh2, name="rms2_bwd", low_copy=True)
    gw["w_o"] = _matmul(attn, dh1_low, ta=True, name="mm_d_o", out_dtype=low)
    dattn = _matmul(dh1_low, w["w_o"], tb=True, name="mm_dattn")

    dq_r, dk_full, dv = _mla_bwd(q, kv, kr, mo, lse, dattn)
    dq_lin = _mla_q_rope(dq_r, cos_m, sin_m, inverse=True, name="mla_q_unrope")
    dkv, dkr = _mla_kv_grad(dk_full, dv, cos_m, sin_m)
    gw["w_uq"] = _matmul(cqn, dq_lin, ta=True, name="mm_d_uq", out_dtype=low)
    dcqn = _matmul(dq_lin, w["w_uq"], tb=True, name="mm_dcqn")
    gw["w_ukv"] = _matmul(kvn, dkv, ta=True, name="mm_d_ukv", out_dtype=low)
    dkvn = _matmul(dkv, w["w_ukv"], tb=True, name="mm_dkvn")
    dcq, dcq_low, dg_q = _rms_bwd(cq, dcqn, vec["g_q_lora"], None, name="rmsq_bwd", low_copy=True)
    dckv, dckv_low, dg_kv = _rms_bwd(ckv, dkvn, vec["g_kv_lora"], None, name="rmskv_bwd", low_copy=True)

    do_ret, drg = _ret_gate_bwd(ret_raw, proj, dattn)
    dq_ret, dk_ret, drv = _ret_bwd(rq, rk, rv, do_ret, lg)
    drq, drk = _ret_unrope(dq_ret, dk_ret, cos_r, sin_r)

    pad = jnp.zeros((t, IN_WIDTH_PAD - IN_WIDTH - 64), low)
    dproj = jnp.concatenate([drq, drk, drv, drg, dcq_low, dckv_low, dkr, pad], axis=1)
    gw["w_in"] = _matmul(hn1, dproj, ta=True, name="mm_d_in", out_dtype=low)
    dhn1 = _matmul(dproj, w["w_in"], tb=True, name="mm_dhn1")
    grad_x, dg_attn = _rms_bwd(x, dhn1, vec["g_attn"], dh1, name="rms1_bwd")

    gs = {"g_attn": dg_attn, "g_q_lora": dg_q, "g_kv_lora": dg_kv, "g_ffn": dg_ffn, "conv_w": dconv_w,
          "conv_b": dconv_b, "g_ple": dg_ple, "g_final": dg_final}
    return loss_part, grad_x, gw, gs


def _cols_join(a):
    n, r, cs = a.shape
    return a.transpose(1, 0, 2).reshape(r, n * cs)


def _cols_split(gf, width):
    r = gf.shape[0]
    cs = width // N_DEV
    return gf[:, :width].reshape(r, N_DEV, cs).transpose(1, 0, 2).reshape(N_DEV * r, cs)


def _rows_join(a):
    return a.reshape(a.shape[0] * a.shape[1], a.shape[2])


def _prepare_weights(g):
    uq = _cols_join(g["w_uq"]).reshape(Q_LORA, MLA_HEADS, MLA_NOPE + MLA_ROPE)
    uq = jnp.pad(uq, ((0, 0), (0, 0), (0, MLA_QK_PAD - MLA_NOPE - MLA_ROPE))).reshape(Q_LORA, Q_WIDTH_PAD)
    ffn_pad = ((0, 0), (0, D_FF_PAD - D_FF))
    return {
        "w_in": jnp.pad(_cols_join(g["w_in"]), ((0, 0), (0, IN_WIDTH_PAD - IN_WIDTH))),
        "w_uq": uq,
        "w_ukv": _cols_join(g["w_ukv"]),
        "w_o": _rows_join(g["w_o"]),
        "w_ffn_gate": jnp.pad(_cols_join(g["w_ffn_gate"]), ffn_pad),
        "w_ffn_up": jnp.pad(_cols_join(g["w_ffn_up"]), ffn_pad),
        "w_ffn_down": jnp.pad(_rows_join(g["w_ffn_down"]), ((0, D_FF_PAD - D_FF), (0, 0))),
        "w_ple_gate": _rows_join(g["w_ple_gate"]),
        "w_ple_proj": _cols_join(g["w_ple_proj"]),
    }


def _split_grads(gw):
    uq = gw["w_uq"].reshape(Q_LORA, MLA_HEADS, MLA_QK_PAD)[:, :, :MLA_NOPE + MLA_ROPE]
    uq = uq.reshape(Q_LORA, MLA_HEADS * (MLA_NOPE + MLA_ROPE))
    return {
        "w_in": _cols_split(gw["w_in"], IN_WIDTH),
        "w_uq": _cols_split(uq, uq.shape[1]),
        "w_ukv": _cols_split(gw["w_ukv"], gw["w_ukv"].shape[1]),
        "w_o": gw["w_o"],
        "w_ffn_gate": _cols_split(gw["w_ffn_gate"], D_FF),
        "w_ffn_up": _cols_split(gw["w_ffn_up"], D_FF),
        "w_ffn_down": gw["w_ffn_down"][:D_FF],
        "w_ple_gate": gw["w_ple_gate"],
        "w_ple_proj": _cols_split(gw["w_ple_proj"], gw["w_ple_proj"].shape[1]),
    }


_MESH = pl.DeviceIdType.MESH
_ANY = pl.BlockSpec(memory_space=pl.ANY)


def _place():
    x, y, c = lax.axis_index("x"), lax.axis_index("y"), lax.axis_index("c")
    chips = [(1 - x, y), (x, 1 - y), (1 - x, 1 - y)]
    return x, y, c, chips


def _all_gather(shard, *, name):
    def body(x_ref, out_ref, send_sems, recv_sems, local_sem):
        x, y, c, chips = _place()
        sibling = (x, y, 1 - c)

        def slot(px, py, pc):
            return out_ref.at[4 * px + 2 * py + pc]

        def copy(k, block, to, src=None):
            return pltpu.make_async_remote_copy(
                src_ref=slot(*block) if src is None else src, dst_ref=slot(*block),
                send_sem=send_sems.at[k], recv_sem=recv_sems.at[k], device_id=to, device_id_type=_MESH)

        mine = pltpu.make_async_copy(x_ref, slot(x, y, c), local_sem)
        mine.start()
        first = [copy(0, (x, y, c), sibling, src=x_ref)]
        first += [copy(1 + j, (x, y, c), (*chip, c), src=x_ref) for j, chip in enumerate(chips)]
        for cp in first:
            cp.start()
        passed = [copy(4 + j, (*chip, c), sibling) for j, chip in enumerate(chips)]
        for j, chip in enumerate(chips):
            copy(1 + j, (*chip, c), (x, y, c)).wait_recv()
            passed[j].start()
        copy(0, sibling, (x, y, c)).wait_recv()
        for j, chip in enumerate(chips):
            copy(4 + j, (*chip, 1 - c), (x, y, c)).wait_recv()
        for cp in first + passed:
            cp.wait_send()
        mine.wait()

    return pl.pallas_call(
        body, name=name, out_shape=jax.ShapeDtypeStruct((N_DEV,) + shard.shape, shard.dtype),
        in_specs=[_ANY], out_specs=_ANY,
        scratch_shapes=[pltpu.SemaphoreType.DMA((7,)), pltpu.SemaphoreType.DMA((7,)), pltpu.SemaphoreType.DMA(())],
    )(shard)


def _handshake(peers):
    barrier = pltpu.get_barrier_semaphore()
    for peer in peers:
        pl.semaphore_signal(barrier, inc=1, device_id=peer, device_id_type=_MESH)
    pl.semaphore_wait(barrier, len(peers))


_SEQUENCER = dict(axis_name="seq", num_cores=1)
_AG_COLLECTIVE_ID = 1
_RS_SIBLING_COLLECTIVE_ID = 2
_RS_CHIPS_COLLECTIVE_ID = 3


def _all_gather_seq(shard, *, name):
    def body(x_ref, out_ref, send_sems, recv_sems, local_sem):
        x, y, c, chips = _place()
        sibling = (x, y, 1 - c)
        _handshake([sibling] + [(*chip, c) for chip in chips])

        def slot(px, py, pc):
            return out_ref.at[4 * px + 2 * py + pc]

        def copy(k, block, to, src=None):
            return pltpu.make_async_remote_copy(
                src_ref=slot(*block) if src is None else src, dst_ref=slot(*block),
                send_sem=send_sems.at[k], recv_sem=recv_sems.at[k], device_id=to, device_id_type=_MESH)

        mine = pltpu.make_async_copy(x_ref, slot(x, y, c), local_sem)
        mine.start()
        first = [copy(0, (x, y, c), sibling, src=x_ref)]
        first += [copy(1 + j, (x, y, c), (*chip, c), src=x_ref) for j, chip in enumerate(chips)]
        for cp in first:
            cp.start()
        passed = [copy(4 + j, (*chip, c), sibling) for j, chip in enumerate(chips)]
        for j, chip in enumerate(chips):
            copy(1 + j, (*chip, c), (x, y, c)).wait_recv()
            passed[j].start()
        copy(0, sibling, (x, y, c)).wait_recv()
        for j, chip in enumerate(chips):
            copy(4 + j, (*chip, 1 - c), (x, y, c)).wait_recv()
        for cp in first + passed:
            cp.wait_send()
        mine.wait()

    return pl.kernel(
        body, out_type=jax.ShapeDtypeStruct((N_DEV,) + shard.shape, shard.dtype),
        mesh=plsc.ScalarSubcoreMesh(**_SEQUENCER), name=name,
        scratch_types=[pltpu.SemaphoreType.DMA((7,)), pltpu.SemaphoreType.DMA((7,)), pltpu.SemaphoreType.DMA(())],
        compiler_params=pltpu.CompilerParams(collective_id=_AG_COLLECTIVE_ID),
    )(shard)


def _exchange_sibling(g, *, name):
    def body(g_ref, out_ref, send_sems, recv_sems):
        x, y, c, _ = _place()
        sibling = (x, y, 1 - c)
        _handshake([sibling])
        copies = []
        for chip in range(4):
            cp = pltpu.make_async_remote_copy(
                src_ref=g_ref.at[2 * chip + (1 - c)], dst_ref=out_ref.at[chip],
                send_sem=send_sems.at[chip], recv_sem=recv_sems.at[chip], device_id=sibling, device_id_type=_MESH)
            cp.start()
            copies.append(cp)
        for cp in copies:
            cp.wait_recv()
        for cp in copies:
            cp.wait_send()

    return pl.kernel(
        body, out_type=jax.ShapeDtypeStruct((4,) + g.shape[1:], g.dtype),
        mesh=plsc.ScalarSubcoreMesh(**_SEQUENCER), name=name,
        scratch_types=[pltpu.SemaphoreType.DMA((4,)), pltpu.SemaphoreType.DMA((4,))],
        compiler_params=pltpu.CompilerParams(collective_id=_RS_SIBLING_COLLECTIVE_ID),
    )(g)


def _add_sibling(g, recv, *, name):
    _, r, cdim = g.shape
    tr = _row_tile(r, cdim, 4)
    g4 = g.reshape(4, 2, r, cdim)
    core = lax.axis_index("c").astype(jnp.int32).reshape(1)

    def body(c_ref, g_ref, r_ref, o_ref):
        o_ref[...] = (g_ref[...].astype(F32) + r_ref[...].astype(F32)).astype(o_ref.dtype)

    return pl.pallas_call(
        body, name=name,
        grid_spec=pltpu.PrefetchScalarGridSpec(
            num_scalar_prefetch=1, grid=(4, r // tr),
            in_specs=[pl.BlockSpec((None, None, tr, cdim), lambda ch, i, c_ref: (ch, c_ref[0], i, 0)),
                      pl.BlockSpec((None, tr, cdim), lambda ch, i, c_ref: (ch, i, 0))],
            out_specs=pl.BlockSpec((None, tr, cdim), lambda ch, i, c_ref: (ch, i, 0))),
        out_shape=jax.ShapeDtypeStruct((4, r, cdim), g.dtype),
        compiler_params=_params(("parallel", "parallel")),
    )(core, g4, recv)


def _exchange_chips(pch, *, name):
    def body(p_ref, out_ref, send_sems, recv_sems, local_sem):
        x, y, c, chips = _place()
        _handshake([(*chip, c) for chip in chips])
        me = 2 * x + y
        mine = pltpu.make_async_copy(p_ref.at[me], out_ref.at[me], local_sem)
        mine.start()
        copies = []
        for j, (px, py) in enumerate(chips):
            cp = pltpu.make_async_remote_copy(
                src_ref=p_ref.at[2 * px + py], dst_ref=out_ref.at[me],
                send_sem=send_sems.at[j], recv_sem=recv_sems.at[j], device_id=(px, py, c), device_id_type=_MESH)
            cp.start()
            copies.append(cp)
        for j, (px, py) in enumerate(chips):
            pltpu.make_async_remote_copy(
                src_ref=p_ref.at[me], dst_ref=out_ref.at[2 * px + py],
                send_sem=send_sems.at[j], recv_sem=recv_sems.at[j], device_id=(px, py, c), device_id_type=_MESH).wait_recv()
        for cp in copies:
            cp.wait_send()
        mine.wait()

    return pl.kernel(
        body, out_type=jax.ShapeDtypeStruct(pch.shape, pch.dtype),
        mesh=plsc.ScalarSubcoreMesh(**_SEQUENCER), name=name,
        scratch_types=[pltpu.SemaphoreType.DMA((3,)), pltpu.SemaphoreType.DMA((3,)), pltpu.SemaphoreType.DMA(())],
        compiler_params=pltpu.CompilerParams(collective_id=_RS_CHIPS_COLLECTIVE_ID),
    )(pch)


def _all_reduce_small(v, *, name):
    r = v.shape[0]

    def body(x_ref, out_ref, buf_ref, send_sems, recv_sems):
        x, y, c, chips = _place()
        sibling = (x, y, 1 - c)

        def slot(px, py, pc):
            return buf_ref.at[4 * px + 2 * py + pc]

        def copy(k, block, to, src=None):
            return pltpu.make_async_remote_copy(
                src_ref=slot(*block) if src is None else src, dst_ref=slot(*block),
                send_sem=send_sems.at[k], recv_sem=recv_sems.at[k], device_id=to, device_id_type=_MESH)

        first = [copy(0, (x, y, c), sibling, src=x_ref)]
        first += [copy(1 + j, (x, y, c), (*chip, c), src=x_ref) for j, chip in enumerate(chips)]
        for cp in first:
            cp.start()
        buf_ref[4 * x + 2 * y + c] = x_ref[...]
        passed = [copy(4 + j, (*chip, c), sibling) for j, chip in enumerate(chips)]
        for j, chip in enumerate(chips):
            copy(1 + j, (*chip, c), (x, y, c)).wait_recv()
            passed[j].start()
        copy(0, sibling, (x, y, c)).wait_recv()
        for j, chip in enumerate(chips):
            copy(4 + j, (*chip, 1 - c), (x, y, c)).wait_recv()
        for cp in first + passed:
            cp.wait_send()
        total = buf_ref[0]
        for k in range(1, N_DEV):
            total = total + buf_ref[k]
        out_ref[...] = total

    vm = pl.BlockSpec(memory_space=pltpu.VMEM)
    return pl.pallas_call(
        body, name=name, out_shape=jax.ShapeDtypeStruct(v.shape, v.dtype),
        in_specs=[vm], out_specs=vm,
        scratch_shapes=[pltpu.VMEM((N_DEV,) + v.shape, v.dtype), pltpu.SemaphoreType.DMA((7,)),
                        pltpu.SemaphoreType.DMA((7,))],
    )(v)


def _row_tile(r, c, n_arrays):
    budget = (24 * 1024 * 1024) // (8 * n_arrays * max(c, 128))
    return _tile(r, max(16, budget), 16)


def _adam_math(w, g, m, v):
    m = ADAM_B1 * m + (1.0 - ADAM_B1) * g
    v = ADAM_B2 * v + (1.0 - ADAM_B2) * jnp.square(g)
    m_hat = m / (1.0 - ADAM_B1 ** ADAM_STEP)
    v_hat = v / (1.0 - ADAM_B2 ** ADAM_STEP)
    delta = -ADAM_LR * (m_hat / (jnp.sqrt(v_hat) + ADAM_EPS) + ADAM_WD * w)
    return delta, m, v


def _adam(w, g, m, v, *, name, parts=None):
    r, cdim = w.shape
    tr = _row_tile(r, cdim, 8)

    def body(w_ref, g_ref, m_ref, v_ref, go_ref, d_ref, mo_ref, vo_ref):
        if parts is None:
            g = g_ref[...]
        else:
            g = g_ref[0].astype(F32)
            for k in range(1, parts):
                g = g + g_ref[k].astype(F32)
        delta, m, v = _adam_math(w_ref[...], g, m_ref[...], v_ref[...])
        go_ref[...] = g
        d_ref[...] = delta
        mo_ref[...] = m
        vo_ref[...] = v

    blk = pl.BlockSpec((tr, cdim), lambda i: (i, 0))
    gblk = blk if parts is None else pl.BlockSpec((parts, tr, cdim), lambda i: (0, i, 0))
    out = jax.ShapeDtypeStruct((r, cdim), F32)
    return pl.pallas_call(
        body, name=name, grid=(r // tr,), in_specs=[blk, gblk, blk, blk], out_specs=[blk] * 4, out_shape=[out] * 4,
        compiler_params=_params(("parallel",)),
    )(w, g, m, v)


_BIG = ("w_in", "w_uq", "w_ukv", "w_o", "w_ffn_gate", "w_ffn_up", "w_ffn_down", "w_ple_gate", "w_ple_proj")
_WEIGHTS = ("w_in", "g_attn", "g_q_lora", "g_kv_lora", "w_uq", "w_ukv", "w_o", "g_ffn", "w_ffn_gate", "w_ffn_up",
            "conv_w", "conv_b", "w_ffn_down", "g_ple", "w_ple_gate", "w_ple_proj", "g_final")
_SMALL_PACK = (("g_attn", 1, D_MODEL), ("g_q_lora", 1, Q_LORA), ("g_kv_lora", 1, KV_LORA), ("g_ffn", 1, D_MODEL),
               ("conv_w", CONV_WIDTH, D_FF_PAD), ("conv_b", 1, D_FF_PAD), ("g_ple", 1, D_MODEL), ("g_final", 1, D_MODEL))


def _pack_small(gs):
    flat = jnp.concatenate([gs[n].reshape(-1) for n, _, _ in _SMALL_PACK])
    rows = -(-flat.shape[0] // 128)
    rows = -(-rows // 8) * 8
    return jnp.pad(flat, (0, rows * 128 - flat.shape[0])).reshape(rows, 128)


def _unpack_small(packed):
    flat = packed.reshape(-1)
    out, off = {}, 0
    for n, r, c in _SMALL_PACK:
        out[n] = flat[off:off + r * c].reshape(r, c)
        off += r * c
    return out


def kernel(x, p, w_in, g_attn, g_q_lora, g_kv_lora, w_uq, w_ukv, w_o, g_ffn, w_ffn_gate, w_ffn_up, conv_w, conv_b, w_ffn_down, g_ple, w_ple_gate, w_ple_proj, g_final, loss_target, m_w_in, m_g_attn, m_g_q_lora, m_g_kv_lora, m_w_uq, m_w_ukv, m_w_o, m_g_ffn, m_w_ffn_gate, m_w_ffn_up, m_conv_w, m_conv_b, m_w_ffn_down, m_g_ple, m_w_ple_gate, m_w_ple_proj, m_g_final, v_w_in, v_g_attn, v_g_q_lora, v_g_kv_lora, v_w_uq, v_w_ukv, v_w_o, v_g_ffn, v_w_ffn_gate, v_w_ffn_up, v_conv_w, v_conv_b, v_w_ffn_down, v_g_ple, v_w_ple_gate, v_w_ple_proj, v_g_final):
    given = dict(locals())
    wts = {n: given[n] for n in _WEIGHTS}
    mom = {n: given["m_" + n] for n in _WEIGHTS}
    var = {n: given["v_" + n] for n in _WEIGHTS}
    me = 4 * lax.axis_index("x") + 2 * lax.axis_index("y") + lax.axis_index("c")

    gathered = {n: _all_gather_seq(wts[n][0].astype(MXU_DTYPE), name="ag_" + n) for n in _BIG}
    full = _prepare_weights(gathered)
    conv_full = _all_gather(conv_w, name="ag_conv_w")
    conv_full = conv_full[:, 0].transpose(1, 0, 2).reshape(CONV_WIDTH, D_FF)
    ffn_pad = ((0, 0), (0, D_FF_PAD - D_FF))
    vec = {"g_attn": g_attn, "g_q_lora": g_q_lora, "g_kv_lora": g_kv_lora, "g_ffn": g_ffn, "g_ple": g_ple,
           "g_final": g_final[None, :], "conv_w": jnp.pad(conv_full, ffn_pad), "conv_b": jnp.pad(conv_b, ffn_pad)}

    loss_part, grad_x, gw, gs = _local_step(x[0], p[0, 0], loss_target[0], full, vec)
    loss = lax.psum(loss_part[0, 0], ("x", "y", "c"))

    grads, deltas, new_m, new_v = {}, {}, {}, {}
    chunks = _split_grads(gw)
    for n in _BIG:
        shard = wts[n][0]
        r, cdim = shard.shape
        g3 = chunks[n].reshape(N_DEV, r, cdim)
        from_sibling = _exchange_sibling(g3, name="rs_sib_" + n)
        per_chip = _add_sibling(g3, from_sibling, name="rs_add_" + n)
        from_chips = _exchange_chips(per_chip, name="rs_chip_" + n)
        g, d, m2, v2 = _adam(shard, from_chips, mom[n][0], var[n][0], name="adam_" + n, parts=4)
        grads[n], deltas[n], new_m[n], new_v[n] = g[None], d[None], m2[None], v2[None]

    small = _unpack_small(_all_reduce_small(_pack_small(gs), name="ar_small"))
    cs = D_FF // N_DEV
    small_g = {
        "g_attn": small["g_attn"], "g_q_lora": small["g_q_lora"], "g_kv_lora": small["g_kv_lora"],
        "g_ffn": small["g_ffn"], "g_ple": small["g_ple"], "g_final": small["g_final"],
        "conv_b": small["conv_b"][:, :D_FF],
        "conv_w": lax.dynamic_slice(small["conv_w"], (jnp.zeros((), jnp.int32), (me * cs).astype(jnp.int32)), (CONV_WIDTH, cs)),
    }
    for n, g in small_g.items():
        shape = wts[n].shape
        as2d = lambda a: a.reshape(g.shape)
        go, d, m2, v2 = _adam(as2d(wts[n]), g, as2d(mom[n]), as2d(var[n]), name="adam_" + n)
        grads[n], deltas[n], new_m[n], new_v[n] = (a.reshape(shape) for a in (go, d, m2, v2))

    return (loss, grad_x[None], *[grads[n] for n in _WEIGHTS], *[deltas[n] for n in _WEIGHTS],
            *[new_m[n] for n in _WEIGHTS], *[new_v[n] for n in _WEIGHTS])
```

```python
import functools

import numpy as np

import jax
import jax.numpy as jnp
from jax import lax
from jax.experimental import pallas as pl
from jax.experimental.pallas import tpu as pltpu
from jax.experimental.pallas import tpu_sc as plsc

D_MODEL = 4096
CHUNK = 64
PLE_DIM = 256
RET_HEADS = 8
RET_HEAD_DIM = 256
RET_WIDTH = 2048
MLA_HEADS = 16
MLA_NOPE = 128
MLA_ROPE = 64
MLA_V = 128
Q_LORA = 1024
KV_LORA = 512
D_FF = 11008
CONV_WIDTH = 3
ROPE_BASE = 10000.0
EPS = 1e-6
IN_WIDTH = 9792
ADAM_LR, ADAM_B1, ADAM_B2, ADAM_EPS, ADAM_WD, ADAM_STEP = 0.001, 0.9, 0.999, 1e-08, 0.01, 10

IN_WIDTH_PAD = 10240
D_FF_PAD = 11264
MLA_QK_PAD = 256
Q_WIDTH_PAD = MLA_HEADS * MLA_QK_PAD

N_DEV = 8
MXU_DTYPE = jnp.bfloat16
ATTN_BLOCK = 512
VMEM_LIMIT = 56 * 1024 * 1024

F32 = jnp.float32


def _tile(n, want, align=128):
    if n <= want:
        return n
    t = (want // align) * align
    while t >= align:
        if n % t == 0:
            return t
        t -= align
    return n


def _params(sem):
    return pltpu.CompilerParams(dimension_semantics=sem, vmem_limit_bytes=VMEM_LIMIT)


def _sigmoid(x):
    return 1.0 / (1.0 + jnp.exp(-x))


def _matmul(a, b, *, name, ta=False, tb=False, out_dtype=F32, add=None, tm=1024, tn=512, tk=4096):
    m, k = (a.shape[1], a.shape[0]) if ta else a.shape
    k2, n = (b.shape[1], b.shape[0]) if tb else b.shape
    assert k == k2, (a.shape, b.shape, ta, tb)
    tm, tn, tk = _tile(m, tm), _tile(n, tn), _tile(k, tk)
    nk = k // tk
    dims = (((0 if ta else 1,), (1 if tb else 0,)), ((), ()))

    def body(*refs):
        a_ref, b_ref, o_ref = refs[0], refs[1], refs[3 if add is not None else 2]
        c_ref = refs[2] if add is not None else None
        part = lax.dot_general(a_ref[...].astype(MXU_DTYPE), b_ref[...].astype(MXU_DTYPE), dims,
                               preferred_element_type=F32)

        def finish(r):
            if c_ref is not None:
                r = r + c_ref[...].astype(F32)
            o_ref[...] = r.astype(out_dtype)

        if nk == 1:
            finish(part)
            return
        acc_ref = refs[-1]
        kk = pl.program_id(2)

        @pl.when(kk == 0)
        def _():
            acc_ref[...] = part

        @pl.when((kk > 0) & (kk < nk - 1))
        def _():
            acc_ref[...] += part

        @pl.when(kk == nk - 1)
        def _():
            finish(acc_ref[...] + part)

    a_spec = pl.BlockSpec((tk, tm), lambda i, j, kk: (kk, i)) if ta else pl.BlockSpec((tm, tk), lambda i, j, kk: (i, kk))
    b_spec = pl.BlockSpec((tn, tk), lambda i, j, kk: (j, kk)) if tb else pl.BlockSpec((tk, tn), lambda i, j, kk: (kk, j))
    in_specs = [a_spec, b_spec]
    args = [a, b]
    if add is not None:
        in_specs.append(pl.BlockSpec((tm, tn), lambda i, j, kk: (i, j)))
        args.append(add)
    return pl.pallas_call(
        body, name=name, grid=(m // tm, n // tn, nk),
        in_specs=in_specs, out_specs=pl.BlockSpec((tm, tn), lambda i, j, kk: (i, j)),
        out_shape=jax.ShapeDtypeStruct((m, n), out_dtype),
        scratch_shapes=[] if nk == 1 else [pltpu.VMEM((tm, tn), F32)],
        compiler_params=_params(("parallel", "parallel", "arbitrary")),
    )(*args)


def _rms_fwd(x, g, *, name):
    t, d = x.shape
    tr = _tile(t, 256, 8)

    def body(x_ref, g_ref, o_ref):
        xf = x_ref[...]
        r = lax.rsqrt(jnp.mean(xf * xf, axis=-1, keepdims=True) + EPS)
        o_ref[...] = (xf * r * g_ref[...]).astype(o_ref.dtype)

    return pl.pallas_call(
        body, name=name, grid=(t // tr,),
        in_specs=[pl.BlockSpec((tr, d), lambda i: (i, 0)), pl.BlockSpec((1, d), lambda i: (0, 0))],
        out_specs=pl.BlockSpec((tr, d), lambda i: (i, 0)),
        out_shape=jax.ShapeDtypeStruct((t, d), MXU_DTYPE),
        compiler_params=_params(("parallel",)),
    )(x, g)


def _rms_bwd(x, dhn, g, res, *, name, low_copy=False):
    t, d = x.shape
    tr = _tile(t, 256, 8)

    def body(*refs):
        if res is None:
            x_ref, dh_ref, g_ref = refs[:3]
            outs = refs[3:]
            res_ref = None
        else:
            x_ref, dh_ref, g_ref, res_ref = refs[:4]
            outs = refs[4:]
        dx_ref, dg_ref = outs[0], outs[-1]
        xf = x_ref[...]
        dh = dh_ref[...].astype(F32)
        r = lax.rsqrt(jnp.mean(xf * xf, axis=-1, keepdims=True) + EPS)
        dyg = dh * g_ref[...]
        dx = r * dyg - xf * (r * r * r) * jnp.mean(dyg * xf, axis=-1, keepdims=True)
        if res_ref is not None:
            dx = dx + res_ref[...]
        dx_ref[...] = dx
        if low_copy:
            outs[1][...] = dx.astype(outs[1].dtype)
        part = jnp.sum(dh * xf * r, axis=0, keepdims=True)

        @pl.when(pl.program_id(0) == 0)
        def _():
            dg_ref[...] = part

        @pl.when(pl.program_id(0) > 0)
        def _():
            dg_ref[...] += part

    row = pl.BlockSpec((tr, d), lambda i: (i, 0))
    vec = pl.BlockSpec((1, d), lambda i: (0, 0))
    in_specs = [row, row, vec] + ([] if res is None else [row])
    args = [x, dhn, g] + ([] if res is None else [res])
    out_specs = [row] + ([row] if low_copy else []) + [vec]
    out_shape = [jax.ShapeDtypeStruct((t, d), F32)] + ([jax.ShapeDtypeStruct((t, d), MXU_DTYPE)] if low_copy else []) \
        + [jax.ShapeDtypeStruct((1, d), F32)]
    return pl.pallas_call(
        body, name=name, grid=(t // tr,), in_specs=in_specs, out_specs=out_specs, out_shape=out_shape,
        compiler_params=_params(("arbitrary",)),
    )(*args)


def _rope_tables(t):
    pos = jnp.arange(t, dtype=F32)[:, None]
    inv_r = 1.0 / (ROPE_BASE ** (jnp.arange(0, RET_HEAD_DIM, 2, dtype=F32) / RET_HEAD_DIM))
    ang_r = pos * inv_r[None, :]
    inv_m = 1.0 / (ROPE_BASE ** (jnp.arange(0, MLA_ROPE, 2, dtype=F32) / MLA_ROPE))
    ang_m = pos * inv_m[None, :]
    cm, sm = jnp.cos(ang_m), jnp.sin(ang_m)
    z = jnp.zeros_like(cm)
    cos_m = jnp.concatenate([cm, cm, z, z], axis=1)
    sin_m = jnp.concatenate([-sm, sm, z, z], axis=1)
    return jnp.cos(ang_r), jnp.sin(ang_r), cos_m, sin_m


def _rope256(x, c, s, inverse=False):
    x1, x2 = x[:, :128], x[:, 128:]
    if inverse:
        s = -s
    return jnp.concatenate([x1 * c - x2 * s, x2 * c + x1 * s], axis=1)


def _rope64(x, cos_m, sin_m, inverse=False):
    lane = lax.broadcasted_iota(jnp.int32, x.shape, 1)
    partner = jnp.where(lane < 32, pltpu.roll(x, 96, 1), pltpu.roll(x, 32, 1))
    s = -sin_m if inverse else sin_m
    return x * cos_m + partner * s


def _ret_prep(proj, cos_r, sin_r):
    t = proj.shape[0]
    tr = _tile(t, 512, 8)
    hb = RET_WIDTH // RET_HEAD_DIM

    def body(q_ref, k_ref, v_ref, c_ref, s_ref, qo_ref, ko_ref, vo_ref):
        c, s = c_ref[...], s_ref[...]
        qo_ref[...] = _rope256(q_ref[...], c, s).astype(qo_ref.dtype)
        ko_ref[...] = (_rope256(k_ref[...], c, s) * (RET_HEAD_DIM ** -0.5)).astype(ko_ref.dtype)
        vo_ref[...] = v_ref[...].astype(vo_ref.dtype)

    head = lambda off: pl.BlockSpec((tr, RET_HEAD_DIM), lambda i, h: (i, h + off))
    tab = pl.BlockSpec((tr, 128), lambda i, h: (i, 0))
    out = jax.ShapeDtypeStruct((t, RET_WIDTH), MXU_DTYPE)
    return pl.pallas_call(
        body, name="ret_prep", grid=(t // tr, RET_HEADS),
        in_specs=[head(0), head(hb), head(2 * hb), tab, tab],
        out_specs=[head(0), head(0), head(0)], out_shape=[out, out, out],
        compiler_params=_params(("parallel", "parallel")),
    )(proj, proj, proj, cos_r, sin_r)


def _ret_log_gamma():
    return jnp.asarray(np.log1p(-np.exp2(-5.0 - np.arange(RET_HEADS, dtype=np.float64))), dtype=F32)


def _decay_full(lg, i, j, blk):
    r = lax.broadcasted_iota(jnp.int32, (blk, 1), 0).astype(F32)
    c = lax.broadcasted_iota(jnp.int32, (1, blk), 1).astype(F32)
    off = ((i - j) * blk).astype(F32)
    return jnp.exp(lg * r), jnp.exp(lg * (off - c))


def _decay_diag(lg, blk):
    r = lax.broadcasted_iota(jnp.int32, (blk, blk), 0)
    c = lax.broadcasted_iota(jnp.int32, (blk, blk), 1)
    ok = (c // CHUNK) <= (r // CHUNK)
    return jnp.where(ok, jnp.exp(lg * jnp.abs(r - c).astype(F32)), 0.0)


_NT = (((1,), (1,)), ((), ()))
_TN = (((0,), (0,)), ((), ()))
_NN = (((1,), (0,)), ((), ()))


def _causal_pairs(nb, query_major):
    if query_major:
        pairs = [(i, j) for i in range(nb) for j in range(i + 1)]
    else:
        pairs = [(i, j) for j in range(nb) for i in range(j, nb)]
    arr = np.asarray(pairs, dtype=np.int32)
    return jnp.asarray(arr[:, 0]), jnp.asarray(arr[:, 1])


def _ret_fwd(q, k, v, proj, lg):
    t = q.shape[0]
    blk = _tile(t, ATTN_BLOCK)
    nb = t // blk
    gate_off = 3 * RET_WIDTH // RET_HEAD_DIM

    def body(ii_ref, jj_ref, lg_ref, q_ref, k_ref, v_ref, g_ref, raw_ref, ro_ref, acc_ref):
        h, pair = pl.program_id(0), pl.program_id(1)
        i, j = ii_ref[pair], jj_ref[pair]
        lgh = lg_ref[h]

        @pl.when(j == 0)
        def _():
            acc_ref[...] = jnp.zeros_like(acc_ref)

        def accumulate(w):
            acc_ref[...] += lax.dot_general(w.astype(MXU_DTYPE), v_ref[...], _NN, preferred_element_type=F32)

        @pl.when(j < i)
        def _():
            s = lax.dot_general(q_ref[...], k_ref[...], _NT, preferred_element_type=F32)
            a, b = _decay_full(lgh, i, j, blk)
            accumulate(s * a * b)

        @pl.when(j == i)
        def _():
            s = lax.dot_general(q_ref[...], k_ref[...], _NT, preferred_element_type=F32)
            accumulate(s * _decay_diag(lgh, blk))
            o = acc_ref[...]
            raw_ref[...] = o
            mu = jnp.mean(o, axis=-1, keepdims=True)
            var = jnp.mean(jnp.square(o - mu), axis=-1, keepdims=True)
            hn = (o - mu) * lax.rsqrt(var + EPS)
            g = g_ref[...]
            ro_ref[...] = (g * _sigmoid(g) * hn).astype(ro_ref.dtype)

    qs = pl.BlockSpec((blk, RET_HEAD_DIM), lambda h, p, ii, jj: (ii[p], h))
    ks = pl.BlockSpec((blk, RET_HEAD_DIM), lambda h, p, ii, jj: (jj[p], h))
    gs = pl.BlockSpec((blk, RET_HEAD_DIM), lambda h, p, ii, jj: (ii[p], h + gate_off))
    ii, jj = _causal_pairs(nb, query_major=True)
    return pl.pallas_call(
        body, name="ret_fwd",
        grid_spec=pltpu.PrefetchScalarGridSpec(
            num_scalar_prefetch=2, grid=(RET_HEADS, ii.shape[0]),
            in_specs=[pl.BlockSpec(memory_space=pltpu.SMEM), qs, ks, ks, gs], out_specs=[qs, qs],
            scratch_shapes=[pltpu.VMEM((blk, RET_HEAD_DIM), F32)]),
        out_shape=[jax.ShapeDtypeStruct((t, RET_WIDTH), F32), jax.ShapeDtypeStruct((t, RET_WIDTH), MXU_DTYPE)],
        compiler_params=_params(("parallel", "arbitrary")),
    )(ii, jj, lg, q, k, v, proj)


def _ret_gate_bwd(raw, proj, dattn):
    t = raw.shape[0]
    tr = _tile(t, 512, 8)
    gate_off = 3 * RET_WIDTH // RET_HEAD_DIM

    def body(o_ref, g_ref, d_ref, do_ref, dg_ref):
        o, g, d = o_ref[...], g_ref[...], d_ref[...]
        mu = jnp.mean(o, axis=-1, keepdims=True)
        rstd = lax.rsqrt(jnp.mean(jnp.square(o - mu), axis=-1, keepdims=True) + EPS)
        hn = (o - mu) * rstd
        sg = _sigmoid(g)
        dg_ref[...] = (d * hn * (sg * (1.0 + g * (1.0 - sg)))).astype(dg_ref.dtype)
        dhn = d * (g * sg)
        do = rstd * (dhn - jnp.mean(dhn, axis=-1, keepdims=True) - hn * jnp.mean(dhn * hn, axis=-1, keepdims=True))
        do_ref[...] = do.astype(do_ref.dtype)

    hs = pl.BlockSpec((tr, RET_HEAD_DIM), lambda i, h: (i, h))
    gs = pl.BlockSpec((tr, RET_HEAD_DIM), lambda i, h: (i, h + gate_off))
    out = jax.ShapeDtypeStruct((t, RET_WIDTH), MXU_DTYPE)
    return pl.pallas_call(
        body, name="ret_gate_bwd", grid=(t // tr, RET_HEADS),
        in_specs=[hs, gs, hs], out_specs=[hs, hs], out_shape=[out, out],
        compiler_params=_params(("parallel", "parallel")),
    )(raw, proj, dattn)


def _ret_bwd(q, k, v, do, lg):
    t = q.shape[0]
    blk = _tile(t, ATTN_BLOCK)
    nb = t // blk

    def body(ii_ref, jj_ref, lg_ref, q_ref, k_ref, v_ref, do_ref, dq_ref, dk_ref, dv_ref, dk_acc, dv_acc):
        h, pair = pl.program_id(0), pl.program_id(1)
        i, j = ii_ref[pair], jj_ref[pair]
        lgh = lg_ref[h]

        @pl.when(pair == 0)
        def _():
            dq_ref[...] = jnp.zeros_like(dq_ref)

        @pl.when(i == j)
        def _():
            dk_acc[...] = jnp.zeros_like(dk_acc)
            dv_acc[...] = jnp.zeros_like(dv_acc)

        def step(decay):
            qb, kb, vb, dob = q_ref[...], k_ref[...], v_ref[...], do_ref[...]
            s = lax.dot_general(qb, kb, _NT, preferred_element_type=F32)
            w = (s * decay).astype(MXU_DTYPE)
            dv_acc[...] += lax.dot_general(w, dob, _TN, preferred_element_type=F32)
            dw = lax.dot_general(dob, vb, _NT, preferred_element_type=F32)
            ds = (dw * decay).astype(MXU_DTYPE)
            rows = pl.ds(pl.multiple_of(i * blk, blk), blk)
            dq_ref[rows, :] += lax.dot_general(ds, kb, _NN, preferred_element_type=F32)
            dk_acc[...] += lax.dot_general(ds, qb, _TN, preferred_element_type=F32)

        @pl.when(i > j)
        def _():
            a, b = _decay_full(lgh, i, j, blk)
            step(a * b)

        @pl.when(i == j)
        def _():
            step(_decay_diag(lgh, blk))

        @pl.when(i == nb - 1)
        def _():
            dk_ref[...] = dk_acc[...]
            dv_ref[...] = dv_acc[...].astype(dv_ref.dtype)

    qs = pl.BlockSpec((blk, RET_HEAD_DIM), lambda h, p, ii, jj: (ii[p], h))
    ks = pl.BlockSpec((blk, RET_HEAD_DIM), lambda h, p, ii, jj: (jj[p], h))
    ii, jj = _causal_pairs(nb, query_major=False)
    return pl.pallas_call(
        body, name="ret_bwd",
        grid_spec=pltpu.PrefetchScalarGridSpec(
            num_scalar_prefetch=2, grid=(RET_HEADS, ii.shape[0]),
            in_specs=[pl.BlockSpec(memory_space=pltpu.SMEM), qs, ks, ks, qs],
            out_specs=[pl.BlockSpec((t, RET_HEAD_DIM), lambda h, p, ii, jj: (0, h)), ks, ks],
            scratch_shapes=[pltpu.VMEM((blk, RET_HEAD_DIM), F32), pltpu.VMEM((blk, RET_HEAD_DIM), F32)]),
        out_shape=[jax.ShapeDtypeStruct((t, RET_WIDTH), F32), jax.ShapeDtypeStruct((t, RET_WIDTH), F32),
                   jax.ShapeDtypeStruct((t, RET_WIDTH), MXU_DTYPE)],
        compiler_params=_params(("parallel", "arbitrary")),
    )(ii, jj, lg, q, k, v, do)


def _ret_unrope(dq, dk, cos_r, sin_r):
    t = dq.shape[0]
    tr = _tile(t, 512, 8)

    def body(dq_ref, dk_ref, c_ref, s_ref, oq_ref, ok_ref):
        c, s = c_ref[...], s_ref[...]
        oq_ref[...] = _rope256(dq_ref[...], c, s, inverse=True).astype(oq_ref.dtype)
        ok_ref[...] = (_rope256(dk_ref[...], c, s, inverse=True) * (RET_HEAD_DIM ** -0.5)).astype(ok_ref.dtype)

    hs = pl.BlockSpec((tr, RET_HEAD_DIM), lambda i, h: (i, h))
    tab = pl.BlockSpec((tr, 128), lambda i, h: (i, 0))
    out = jax.ShapeDtypeStruct((t, RET_WIDTH), MXU_DTYPE)
    return pl.pallas_call(
        body, name="ret_unrope", grid=(t // tr, RET_HEADS),
        in_specs=[hs, hs, tab, tab], out_specs=[hs, hs], out_shape=[out, out],
        compiler_params=_params(("parallel", "parallel")),
    )(dq, dk, cos_r, sin_r)


def _mla_prep(cq, ckv, kr, g_q, g_kv, cos_m, sin_m):
    t = cq.shape[0]
    tr = _tile(t, 512, 8)

    def body(cq_ref, ckv_ref, kr_ref, gq_ref, gkv_ref, c_ref, s_ref, cqn_ref, kvn_ref, kro_ref):
        for x_ref, g_ref, o_ref in ((cq_ref, gq_ref, cqn_ref), (ckv_ref, gkv_ref, kvn_ref)):
            xf = x_ref[...]
            r = lax.rsqrt(jnp.mean(xf * xf, axis=-1, keepdims=True) + EPS)
            o_ref[...] = (xf * r * g_ref[...]).astype(o_ref.dtype)
        kro_ref[...] = _rope64(kr_ref[...], c_ref[...], s_ref[...]).astype(kro_ref.dtype)

    row = lambda w: pl.BlockSpec((tr, w), lambda i: (i, 0))
    vec = lambda w: pl.BlockSpec((1, w), lambda i: (0, 0))
    return pl.pallas_call(
        body, name="mla_prep", grid=(t // tr,),
        in_specs=[row(Q_LORA), row(KV_LORA), row(128), vec(Q_LORA), vec(KV_LORA), row(128), row(128)],
        out_specs=[row(Q_LORA), row(KV_LORA), row(128)],
        out_shape=[jax.ShapeDtypeStruct((t, Q_LORA), MXU_DTYPE), jax.ShapeDtypeStruct((t, KV_LORA), MXU_DTYPE),
                   jax.ShapeDtypeStruct((t, 128), MXU_DTYPE)],
        compiler_params=_params(("parallel",)),
    )(cq, ckv, kr, g_q, g_kv, cos_m, sin_m)


def _mla_q_rope(q_lin, cos_m, sin_m, *, inverse, name):
    t = q_lin.shape[0]
    tr = _tile(t, 512, 8)

    def body(q_ref, c_ref, s_ref, o_ref):
        x = q_ref[...].astype(F32)
        roped = _rope64(x[:, 128:], c_ref[...], s_ref[...], inverse=inverse)
        o_ref[...] = jnp.concatenate([x[:, :128], roped], axis=1).astype(o_ref.dtype)

    hs = pl.BlockSpec((tr, MLA_QK_PAD), lambda i, h: (i, h))
    tab = pl.BlockSpec((tr, 128), lambda i, h: (i, 0))
    return pl.pallas_call(
        body, name=name, grid=(t // tr, MLA_HEADS),
        in_specs=[hs, tab, tab], out_specs=hs, out_shape=jax.ShapeDtypeStruct((t, Q_WIDTH_PAD), MXU_DTYPE),
        compiler_params=_params(("parallel", "parallel")),
    )(q_lin, cos_m, sin_m)


_MLA_SCALE = (MLA_NOPE + MLA_ROPE) ** -0.5
_NEG = -1e30


def _mla_mask(blk):
    r = lax.broadcasted_iota(jnp.int32, (blk, blk), 0)
    c = lax.broadcasted_iota(jnp.int32, (blk, blk), 1)
    return (c // CHUNK) <= (r // CHUNK)


def _mla_fwd(q, kv, kr):
    t = q.shape[0]
    blk = _tile(t, ATTN_BLOCK)
    nb = t // blk

    def body(ii_ref, jj_ref, q_ref, kn_ref, v_ref, kr_ref, o_ref, lse_ref, m_ref, l_ref, acc_ref):
        pair = pl.program_id(1)
        i, j = ii_ref[pair], jj_ref[pair]

        @pl.when(j == 0)
        def _():
            m_ref[...] = jnp.full_like(m_ref, _NEG)
            l_ref[...] = jnp.zeros_like(l_ref)
            acc_ref[...] = jnp.zeros_like(acc_ref)

        def step(masked):
            kb = jnp.concatenate([kn_ref[...], kr_ref[...]], axis=1)
            s = lax.dot_general(q_ref[...], kb, _NT, preferred_element_type=F32) * _MLA_SCALE
            if masked:
                s = jnp.where(_mla_mask(blk), s, _NEG)
            m_prev = m_ref[...]
            m_new = jnp.maximum(m_prev, jnp.max(s, axis=-1, keepdims=True))
            alpha = jnp.exp(m_prev - m_new)
            p = jnp.exp(s - m_new)
            l_ref[...] = alpha * l_ref[...] + jnp.sum(p, axis=-1, keepdims=True)
            acc_ref[...] = alpha * acc_ref[...] + lax.dot_general(p.astype(MXU_DTYPE), v_ref[...], _NN,
                                                                  preferred_element_type=F32)
            m_ref[...] = m_new

        @pl.when(j < i)
        def _():
            step(False)

        @pl.when(j == i)
        def _():
            step(True)
            o_ref[...] = (acc_ref[...] / l_ref[...]).astype(o_ref.dtype)
            lse_ref[...] = jnp.broadcast_to(m_ref[...] + jnp.log(l_ref[...]), lse_ref.shape)

    kmap = lambda off: (lambda h, p, ii, jj: (jj[p], 2 * h + off))
    os_ = pl.BlockSpec((blk, 128), lambda h, p, ii, jj: (ii[p], h))
    ii, jj = _causal_pairs(nb, query_major=True)
    return pl.pallas_call(
        body, name="mla_fwd",
        grid_spec=pltpu.PrefetchScalarGridSpec(
            num_scalar_prefetch=2, grid=(MLA_HEADS, ii.shape[0]),
            in_specs=[pl.BlockSpec((blk, MLA_QK_PAD), lambda h, p, ii, jj: (ii[p], h)),
                      pl.BlockSpec((blk, 128), kmap(0)), pl.BlockSpec((blk, 128), kmap(1)),
                      pl.BlockSpec((blk, 128), lambda h, p, ii, jj: (jj[p], 0))],
            out_specs=[os_, os_],
            scratch_shapes=[pltpu.VMEM((blk, 1), F32), pltpu.VMEM((blk, 1), F32), pltpu.VMEM((blk, MLA_V), F32)]),
        out_shape=[jax.ShapeDtypeStruct((t, MLA_HEADS * MLA_V), MXU_DTYPE),
                   jax.ShapeDtypeStruct((t, MLA_HEADS * 128), F32)],
        compiler_params=_params(("parallel", "arbitrary")),
    )(ii, jj, q, kv, kv, kr)


def _mla_bwd(q, kv, kr, o, lse, dattn):
    t = q.shape[0]
    blk = _tile(t, ATTN_BLOCK)
    nb = t // blk
    do_off = RET_WIDTH // 128

    def body(ii_ref, jj_ref, q_ref, kn_ref, v_ref, kr_ref, o_ref, lse_ref, do_ref, dq_ref, dk_ref, dv_ref, dk_acc, dv_acc):
        pair = pl.program_id(1)
        i, j = ii_ref[pair], jj_ref[pair]

        @pl.when(pair == 0)
        def _():
            dq_ref[...] = jnp.zeros_like(dq_ref)

        @pl.when(i == j)
        def _():
            dk_acc[...] = jnp.zeros_like(dk_acc)
            dv_acc[...] = jnp.zeros_like(dv_acc)

        def step(masked):
            qb, vb = q_ref[...], v_ref[...]
            kb = jnp.concatenate([kn_ref[...], kr_ref[...]], axis=1)
            dof = do_ref[...]
            dob = dof.astype(MXU_DTYPE)
            s = lax.dot_general(qb, kb, _NT, preferred_element_type=F32) * _MLA_SCALE
            if masked:
                s = jnp.where(_mla_mask(blk), s, _NEG)
            p = jnp.exp(s - lse_ref[...][:, :1])
            delta = jnp.sum(dof * o_ref[...].astype(F32), axis=-1, keepdims=True)
            dv_acc[...] += lax.dot_general(p.astype(MXU_DTYPE), dob, _TN, preferred_element_type=F32)
            dp = lax.dot_general(dob, vb, _NT, preferred_element_type=F32)
            ds = (p * (dp - delta) * _MLA_SCALE).astype(MXU_DTYPE)
            rows = pl.ds(pl.multiple_of(i * blk, blk), blk)
            dq_ref[rows, :] += lax.dot_general(ds, kb, _NN, preferred_element_type=F32)
            dk_acc[...] += lax.dot_general(ds, qb, _TN, preferred_element_type=F32)

        @pl.when(i > j)
        def _():
            step(False)

        @pl.when(i == j)
        def _():
            step(True)

        @pl.when(i == nb - 1)
        def _():
            dk_ref[...] = dk_acc[...]
            dv_ref[...] = dv_acc[...].astype(dv_ref.dtype)

    qmap = lambda off: (lambda h, p, ii, jj: (ii[p], h + off))
    kmap = lambda off: (lambda h, p, ii, jj: (jj[p], 2 * h + off))
    ii, jj = _causal_pairs(nb, query_major=False)
    return pl.pallas_call(
        body, name="mla_bwd",
        grid_spec=pltpu.PrefetchScalarGridSpec(
            num_scalar_prefetch=2, grid=(MLA_HEADS, ii.shape[0]),
            in_specs=[pl.BlockSpec((blk, MLA_QK_PAD), qmap(0)),
                      pl.BlockSpec((blk, 128), kmap(0)), pl.BlockSpec((blk, 128), kmap(1)),
                      pl.BlockSpec((blk, 128), lambda h, p, ii, jj: (jj[p], 0)),
                      pl.BlockSpec((blk, 128), qmap(0)), pl.BlockSpec((blk, 128), qmap(0)),
                      pl.BlockSpec((blk, 128), qmap(do_off))],
            out_specs=[pl.BlockSpec((t, MLA_QK_PAD), lambda h, p, ii, jj: (0, h)),
                       pl.BlockSpec((blk, MLA_QK_PAD), lambda h, p, ii, jj: (jj[p], h)),
                       pl.BlockSpec((blk, MLA_V), lambda h, p, ii, jj: (jj[p], h))],
            scratch_shapes=[pltpu.VMEM((blk, MLA_QK_PAD), F32), pltpu.VMEM((blk, MLA_V), F32)]),
        out_shape=[jax.ShapeDtypeStruct((t, Q_WIDTH_PAD), F32), jax.ShapeDtypeStruct((t, Q_WIDTH_PAD), F32),
                   jax.ShapeDtypeStruct((t, MLA_HEADS * MLA_V), MXU_DTYPE)],
        compiler_params=_params(("parallel", "arbitrary")),
    )(ii, jj, q, kv, kv, kr, o, lse, dattn)


def _mla_kv_grad(dk, dv, cos_m, sin_m):
    t = dk.shape[0]
    tr = _tile(t, 256, 8)

    def body(dk_ref, dv_ref, c_ref, s_ref, dkv_ref, dkr_ref):
        acc = jnp.zeros((tr, 128), F32)
        for h in range(MLA_HEADS):
            dkv_ref[:, h * 256:h * 256 + 128] = dk_ref[:, h * 256:h * 256 + 128].astype(dkv_ref.dtype)
            dkv_ref[:, h * 256 + 128:h * 256 + 256] = dv_ref[:, h * 128:(h + 1) * 128].astype(dkv_ref.dtype)
            acc = acc + dk_ref[:, h * 256 + 128:h * 256 + 256]
        dkr_ref[...] = _rope64(acc, c_ref[...], s_ref[...], inverse=True).astype(dkr_ref.dtype)

    row = lambda w: pl.BlockSpec((tr, w), lambda i: (i, 0))
    return pl.pallas_call(
        body, name="mla_kv_grad", grid=(t // tr,),
        in_specs=[row(Q_WIDTH_PAD), row(MLA_HEADS * MLA_V), row(128), row(128)],
        out_specs=[row(Q_WIDTH_PAD), row(128)],
        out_shape=[jax.ShapeDtypeStruct((t, Q_WIDTH_PAD), MXU_DTYPE), jax.ShapeDtypeStruct((t, 128), MXU_DTYPE)],
        compiler_params=_params(("parallel",)),
    )(dk, dv, cos_m, sin_m)


_FFN_COLS = 256
_FFN_ROWS = 256


def _shift_down(cur, prev8, n):
    row = lax.broadcasted_iota(jnp.int32, cur.shape, 0)
    out = pltpu.roll(cur, n, 0)
    for r in range(n):
        out = jnp.where(row == r, prev8[8 - n + r:8 - n + r + 1, :], out)
    return out


def _shift_up(cur, next8, n):
    rows = cur.shape[0]
    row = lax.broadcasted_iota(jnp.int32, cur.shape, 0)
    out = pltpu.roll(cur, rows - n, 0)
    for r in range(n):
        out = jnp.where(row == rows - n + r, next8[r:r + 1, :], out)
    return out


def _conv_pre(g_ref, cw_ref, cb_ref, c, rc):
    r0 = pl.multiple_of(c * rc, rc)
    cur = g_ref[pl.ds(r0, rc), :]
    prev8 = g_ref[pl.ds(pl.multiple_of(jnp.maximum(r0 - 8, 0), 8), 8), :]
    prev8 = jnp.where(c > 0, prev8, 0.0)
    s1, s2 = _shift_down(cur, prev8, 1), _shift_down(cur, prev8, 2)
    a = cb_ref[...] + cw_ref[2:3, :] * cur + cw_ref[1:2, :] * s1 + cw_ref[0:1, :] * s2
    return r0, cur, s1, s2, a


def _ffn_act_fwd(gpre, u, cw, cb):
    t, f = gpre.shape
    tc = _tile(f, _FFN_COLS)
    rc = _tile(t, _FFN_ROWS, 8)

    def body(g_ref, u_ref, cw_ref, cb_ref, o_ref):
        def chunk(c, carry):
            r0, _, _, _, a = _conv_pre(g_ref, cw_ref, cb_ref, c, rc)
            o_ref[pl.ds(r0, rc), :] = (a * _sigmoid(a) * u_ref[pl.ds(r0, rc), :]).astype(o_ref.dtype)
            return carry
        lax.fori_loop(0, t // rc, chunk, 0)

    col = pl.BlockSpec((t, tc), lambda j: (0, j))
    return pl.pallas_call(
        body, name="ffn_act_fwd", grid=(f // tc,),
        in_specs=[col, col, pl.BlockSpec((CONV_WIDTH, tc), lambda j: (0, j)), pl.BlockSpec((1, tc), lambda j: (0, j))],
        out_specs=col, out_shape=jax.ShapeDtypeStruct((t, f), MXU_DTYPE),
        compiler_params=_params(("parallel",)),
    )(gpre, u, cw, cb)


def _ffn_act_bwd(gpre, u, dact, cw, cb):
    t, f = gpre.shape
    tc = _tile(f, _FFN_COLS)
    rc = _tile(t, _FFN_ROWS, 8)
    nc = t // rc

    def body(g_ref, u_ref, d_ref, cw_ref, cb_ref, dg_ref, du_ref, dcw_ref, dcb_ref, da_ref):
        def chunk(c, carry):
            w0, w1, w2, b = carry
            r0, cur, s1, s2, a = _conv_pre(g_ref, cw_ref, cb_ref, c, rc)
            sg = _sigmoid(a)
            d = d_ref[pl.ds(r0, rc), :]
            du_ref[pl.ds(r0, rc), :] = (d * (a * sg)).astype(du_ref.dtype)
            da = d * u_ref[pl.ds(r0, rc), :] * (sg * (1.0 + a * (1.0 - sg)))
            da_ref[pl.ds(r0, rc), :] = da
            return (w0 + jnp.sum(da * s2, axis=0, keepdims=True), w1 + jnp.sum(da * s1, axis=0, keepdims=True),
                    w2 + jnp.sum(da * cur, axis=0, keepdims=True), b + jnp.sum(da, axis=0, keepdims=True))
        z = jnp.zeros((1, tc), F32)
        w0, w1, w2, b = lax.fori_loop(0, nc, chunk, (z, z, z, z))
        dcw_ref[0:1, :] = w0
        dcw_ref[1:2, :] = w1
        dcw_ref[2:3, :] = w2
        dcb_ref[...] = b

        def chunk2(c, carry):
            r0 = pl.multiple_of(c * rc, rc)
            cur = da_ref[pl.ds(r0, rc), :]
            nxt = da_ref[pl.ds(pl.multiple_of(jnp.minimum(r0 + rc, t - 8), 8), 8), :]
            nxt = jnp.where(c < nc - 1, nxt, 0.0)
            dg = cw_ref[2:3, :] * cur + cw_ref[1:2, :] * _shift_up(cur, nxt, 1) + cw_ref[0:1, :] * _shift_up(cur, nxt, 2)
            dg_ref[pl.ds(r0, rc), :] = dg.astype(dg_ref.dtype)
            return carry
        lax.fori_loop(0, nc, chunk2, 0)

    col = pl.BlockSpec((t, tc), lambda j: (0, j))
    w3 = pl.BlockSpec((CONV_WIDTH, tc), lambda j: (0, j))
    w1 = pl.BlockSpec((1, tc), lambda j: (0, j))
    low = jax.ShapeDtypeStruct((t, f), MXU_DTYPE)
    return pl.pallas_call(
        body, name="ffn_act_bwd", grid=(f // tc,),
        in_specs=[col, col, col, w3, w1], out_specs=[col, col, w3, w1],
        out_shape=[low, low, jax.ShapeDtypeStruct((CONV_WIDTH, f), F32), jax.ShapeDtypeStruct((1, f), F32)],
        scratch_shapes=[pltpu.VMEM((t, tc), F32)],
        compiler_params=_params(("parallel",)),
    )(gpre, u, dact, cw, cb)


def _head_fwd_bwd(h2, glin, pp, target, g_final):
    t, d = h2.shape
    tr = _tile(t, 128, 8)

    def body(h_ref, gl_ref, pp_ref, t_ref, g_ref, loss_ref, dh_ref, dgl_ref, dpp_ref, dg_ref):
        gate = _sigmoid(gl_ref[...])
        ppv = pp_ref[...]
        h3 = h_ref[...] + gate * ppv
        r = lax.rsqrt(jnp.mean(h3 * h3, axis=-1, keepdims=True) + EPS)
        yh = h3 * r
        g = g_ref[...]
        diff = yh * g - t_ref[...]
        lpart = 0.5 * jnp.sum(jnp.mean(diff * diff, axis=-1, keepdims=True), axis=0, keepdims=True)
        dy = diff * (1.0 / d)
        dyg = dy * g
        dh3 = r * dyg - h3 * (r * r * r) * jnp.mean(dyg * h3, axis=-1, keepdims=True)
        dh_ref[...] = dh3
        dgl_ref[...] = (dh3 * ppv * gate * (1.0 - gate)).astype(dgl_ref.dtype)
        dpp_ref[...] = (dh3 * gate).astype(dpp_ref.dtype)
        dgp = jnp.sum(dy * yh, axis=0, keepdims=True)

        @pl.when(pl.program_id(0) == 0)
        def _():
            loss_ref[...] = jnp.broadcast_to(lpart, loss_ref.shape)
            dg_ref[...] = dgp

        @pl.when(pl.program_id(0) > 0)
        def _():
            loss_ref[...] += jnp.broadcast_to(lpart, loss_ref.shape)
            dg_ref[...] += dgp

    row = pl.BlockSpec((tr, d), lambda i: (i, 0))
    vec = pl.BlockSpec((1, d), lambda i: (0, 0))
    low = jax.ShapeDtypeStruct((t, d), MXU_DTYPE)
    return pl.pallas_call(
        body, name="head_fwd_bwd", grid=(t // tr,),
        in_specs=[row, row, row, row, vec],
        out_specs=[pl.BlockSpec((8, 128), lambda i: (0, 0)), row, row, row, vec],
        out_shape=[jax.ShapeDtypeStruct((8, 128), F32), jax.ShapeDtypeStruct((t, d), F32), low, low,
                   jax.ShapeDtypeStruct((1, d), F32)],
        compiler_params=_params(("arbitrary",)),
    )(h2, glin, pp, target, g_final)


class _Order:
    def __init__(self):
        self.last = None

    def tie(self, x):
        return x if self.last is None else lax.optimization_barrier((x, self.last))[0]

    def run(self, fn, first, *args, **kwargs):
        out = fn(self.tie(first), *args, **kwargs)
        self.last = out[0] if isinstance(out, (tuple, list)) else out
        return out


def _local_step(x, p, target, vec, ops):
    t = x.shape[0]
    cos_r, sin_r, cos_m, sin_m = _rope_tables(t)
    lg = _ret_log_gamma()
    low = MXU_DTYPE
    run = ops.order.run
    w = ops.weight

    ops.start_gather("w_in")
    hn1 = run(_rms_fwd, x, vec["g_attn"], name="rms1_fwd")
    for n in ("w_uq", "w_ukv", "w_o"):
        ops.start_gather(n, after=hn1)
    proj = run(_matmul, hn1, w("w_in"), tb=True, name="mm_proj")
    ops.start_gather("w_ffn_gate", after=proj)
    rq, rk, rv = run(_ret_prep, proj, cos_r, sin_r)
    ops.start_gather("w_ffn_up", after=rq)
    c0 = 4 * RET_WIDTH
    cq = proj[:, c0:c0 + Q_LORA]
    ckv = proj[:, c0 + Q_LORA:c0 + Q_LORA + KV_LORA]
    kr_in = proj[:, c0 + Q_LORA + KV_LORA:c0 + Q_LORA + KV_LORA + 128]
    cqn, kvn, kr = run(_mla_prep, cq, ckv, kr_in, vec["g_q_lora"], vec["g_kv_lora"], cos_m, sin_m)
    ops.start_gather("w_ffn_down", after=cqn)
    q_lin = run(_matmul, cqn, w("w_uq"), tb=True, name="mm_q")
    q = run(_mla_q_rope, q_lin, cos_m, sin_m, inverse=False, name="mla_q_rope")
    kv = run(_matmul, kvn, w("w_ukv"), tb=True, name="mm_kv", out_dtype=low)
    mo, lse = run(_mla_fwd, q, kv, kr)
    ops.start_gather("w_ple_gate", after=mo)
    ops.start_gather("w_ple_proj", after=mo)
    ret_raw, ro = run(_ret_fwd, rq, rk, rv, proj, lg)
    attn = jnp.concatenate([ro, mo], axis=1)
    h1 = run(_matmul, attn, w("w_o"), name="mm_o", add=x)
    hn2 = run(_rms_fwd, h1, vec["g_ffn"], name="rms2_fwd")
    gpre = run(_matmul, hn2, w("w_ffn_gate"), tb=True, name="mm_gate")
    u = run(_matmul, hn2, w("w_ffn_up"), tb=True, name="mm_up")
    act = run(_ffn_act_fwd, gpre, u, vec["conv_w"], vec["conv_b"])
    h2 = run(_matmul, act, w("w_ffn_down"), name="mm_down", add=h1)
    hn3 = run(_rms_fwd, h2, vec["g_ple"], name="rms3_fwd")
    glin = run(_matmul, hn3, w("w_ple_gate"), name="mm_ple_gate")
    p_low = p.astype(low)
    pp = run(_matmul, p_low, w("w_ple_proj"), tb=True, name="mm_ple_proj")
    loss_part, dh3, dglin, dpp, dg_final = run(_head_fwd_bwd, h2, glin, pp, target, vec["g_final"])

    ops.grad("w_ple_proj", run(_matmul, dpp, p_low, ta=True, name="mm_d_ple_proj", out_dtype=low))
    ops.grad("w_ple_gate", run(_matmul, hn3, dglin, ta=True, name="mm_d_ple_gate", out_dtype=low))
    dhn3 = run(_matmul, dglin, w("w_ple_gate"), tb=True, name="mm_dhn3")
    dh2, dh2_low, dg_ple = run(_rms_bwd, h2, dhn3, vec["g_ple"], dh3, name="rms3_bwd", low_copy=True)
    ops.reduce_add("w_ple_proj")
    ops.reduce_add("w_ple_gate")
    ops.grad("w_ffn_down", run(_matmul, act, dh2_low, ta=True, name="mm_d_down", out_dtype=low))
    dact = run(_matmul, dh2_low, w("w_ffn_down"), tb=True, name="mm_dact")
    ops.reduce_add("w_ffn_down")
    dgpre, du, dconv_w, dconv_b = run(_ffn_act_bwd, gpre, u, dact, vec["conv_w"], vec["conv_b"])
    ops.update("w_ple_proj")
    ops.update("w_ple_gate")
    ops.grad("w_ffn_gate", run(_matmul, dgpre, hn2, ta=True, name="mm_d_gate", out_dtype=low))
    ops.grad("w_ffn_up", run(_matmul, du, hn2, ta=True, name="mm_d_up", out_dtype=low))
    ops.reduce_add("w_ffn_gate")
    dhn2 = run(_matmul, dgpre, w("w_ffn_gate"), name="mm_dhn2_a")
    ops.reduce_add("w_ffn_up")
    dhn2 = run(_matmul, du, w("w_ffn_up"), name="mm_dhn2_b", add=dhn2)
    dh1, dh1_low, dg_ffn = run(_rms_bwd, h1, dhn2, vec["g_ffn"], dh2, name="rms2_bwd", low_copy=True)
    ops.update("w_ffn_down")
    ops.grad("w_o", run(_matmul, attn, dh1_low, ta=True, name="mm_d_o", out_dtype=low))
    dattn = run(_matmul, dh1_low, w("w_o"), tb=True, name="mm_dattn")
    ops.reduce_add("w_o")

    dq_r, dk_full, dv = run(_mla_bwd, q, kv, kr, mo, lse, dattn)
    ops.update("w_ffn_gate")
    dq_lin = run(_mla_q_rope, dq_r, cos_m, sin_m, inverse=True, name="mla_q_unrope")
    dkv, dkr = run(_mla_kv_grad, dk_full, dv, cos_m, sin_m)
    ops.grad("w_uq", run(_matmul, dq_lin, cqn, ta=True, name="mm_d_uq", out_dtype=low))
    dcqn = run(_matmul, dq_lin, w("w_uq"), name="mm_dcqn")
    ops.grad("w_ukv", run(_matmul, dkv, kvn, ta=True, name="mm_d_ukv", out_dtype=low))
    dkvn = run(_matmul, dkv, w("w_ukv"), name="mm_dkvn")
    dcq, dcq_low, dg_q = run(_rms_bwd, cq, dcqn, vec["g_q_lora"], None, name="rmsq_bwd", low_copy=True)
    dckv, dckv_low, dg_kv = run(_rms_bwd, ckv, dkvn, vec["g_kv_lora"], None, name="rmskv_bwd", low_copy=True)
    ops.reduce_add("w_uq")
    ops.reduce_add("w_ukv")

    do_ret, drg = run(_ret_gate_bwd, ret_raw, proj, dattn)
    dq_ret, dk_ret, drv = run(_ret_bwd, rq, rk, rv, do_ret, lg)
    drq, drk = run(_ret_unrope, dq_ret, dk_ret, cos_r, sin_r)

    pad = jnp.zeros((t, IN_WIDTH_PAD - IN_WIDTH - 64), low)
    dproj = jnp.concatenate([drq, drk, drv, drg, dcq_low, dckv_low, dkr, pad], axis=1)
    ops.grad("w_in", run(_matmul, dproj, hn1, ta=True, name="mm_d_in", out_dtype=low))
    for n in ("w_ffn_up", "w_o", "w_uq", "w_ukv"):
        ops.update(n)
    ops.reduce_add("w_in")
    dhn1 = run(_matmul, dproj, w("w_in"), name="mm_dhn1")
    grad_x, dg_attn = run(_rms_bwd, x, dhn1, vec["g_attn"], dh1, name="rms1_bwd")

    gs = {"g_attn": dg_attn, "g_q_lora": dg_q, "g_kv_lora": dg_kv, "g_ffn": dg_ffn, "conv_w": dconv_w,
          "conv_b": dconv_b, "g_ple": dg_ple, "g_final": dg_final}
    return loss_part, grad_x, gs


_COL_SHARDED = ("w_in", "w_uq", "w_ukv", "w_ffn_gate", "w_ffn_up", "w_ple_proj")
_FFN_SHARD = D_FF // N_DEV
_FFN_SHARD_PAD = D_FF_PAD // N_DEV
_HEADS_PER_SHARD = MLA_HEADS // N_DEV
_QK = MLA_NOPE + MLA_ROPE


def _pad_rows(name, a):
    lead = a.shape[:-2]
    if name == "w_uq":
        a = a.reshape(lead + (_HEADS_PER_SHARD, _QK, a.shape[-1]))
        a = jnp.pad(a, [(0, 0)] * len(lead) + [(0, 0), (0, MLA_QK_PAD - _QK), (0, 0)])
        return a.reshape(lead + (_HEADS_PER_SHARD * MLA_QK_PAD, a.shape[-1]))
    if name in ("w_ffn_gate", "w_ffn_up", "w_ffn_down"):
        return jnp.pad(a, [(0, 0)] * len(lead) + [(0, _FFN_SHARD_PAD - _FFN_SHARD), (0, 0)])
    return a


def _unpad_rows(name, a):
    lead = a.shape[:-2]
    if name == "w_uq":
        a = a.reshape(lead + (_HEADS_PER_SHARD, MLA_QK_PAD, a.shape[-1]))[..., :_QK, :]
        return a.reshape(lead + (_HEADS_PER_SHARD * _QK, a.shape[-1]))
    if name in ("w_ffn_gate", "w_ffn_up", "w_ffn_down"):
        return a[..., :_FFN_SHARD, :]
    return a


def _rows_view(name, a):
    return jnp.swapaxes(a, 0, 1) if name in _COL_SHARDED else a


def _shard_payload(name, shard):
    return _pad_rows(name, _rows_view(name, shard).astype(MXU_DTYPE))


def _full_from_gathered(name, g):
    full = g.reshape(g.shape[0] * g.shape[1], g.shape[2])
    if name == "w_in":
        full = jnp.pad(full, ((0, IN_WIDTH_PAD - IN_WIDTH), (0, 0)))
    return full


def _grad_chunks(name, gfull):
    if name == "w_in":
        gfull = gfull[:IN_WIDTH]
    return gfull.reshape(N_DEV, gfull.shape[0] // N_DEV, gfull.shape[1])


def _ffn_vec_layout(a):
    a = a.reshape(a.shape[0], N_DEV, _FFN_SHARD)
    return jnp.pad(a, ((0, 0), (0, 0), (0, _FFN_SHARD_PAD - _FFN_SHARD))).reshape(a.shape[0], D_FF_PAD)


def _ffn_vec_shards(a):
    return a.reshape(a.shape[0], N_DEV, _FFN_SHARD_PAD)[:, :, :_FFN_SHARD]


_MESH = pl.DeviceIdType.MESH
_ANY = pl.BlockSpec(memory_space=pl.ANY)


def _place():
    x, y, c = lax.axis_index("x"), lax.axis_index("y"), lax.axis_index("c")
    chips = [(1 - x, y), (x, 1 - y), (1 - x, 1 - y)]
    return x, y, c, chips


def _all_gather(shard, *, name):
    def body(x_ref, out_ref, send_sems, recv_sems, local_sem):
        x, y, c, chips = _place()
        sibling = (x, y, 1 - c)

        def slot(px, py, pc):
            return out_ref.at[4 * px + 2 * py + pc]

        def copy(k, block, to, src=None):
            return pltpu.make_async_remote_copy(
                src_ref=slot(*block) if src is None else src, dst_ref=slot(*block),
                send_sem=send_sems.at[k], recv_sem=recv_sems.at[k], device_id=to, device_id_type=_MESH)

        mine = pltpu.make_async_copy(x_ref, slot(x, y, c), local_sem)
        mine.start()
        first = [copy(0, (x, y, c), sibling, src=x_ref)]
        first += [copy(1 + j, (x, y, c), (*chip, c), src=x_ref) for j, chip in enumerate(chips)]
        for cp in first:
            cp.start()
        passed = [copy(4 + j, (*chip, c), sibling) for j, chip in enumerate(chips)]
        for j, chip in enumerate(chips):
            copy(1 + j, (*chip, c), (x, y, c)).wait_recv()
            passed[j].start()
        copy(0, sibling, (x, y, c)).wait_recv()
        for j, chip in enumerate(chips):
            copy(4 + j, (*chip, 1 - c), (x, y, c)).wait_recv()
        for cp in first + passed:
            cp.wait_send()
        mine.wait()

    return pl.pallas_call(
        body, name=name, out_shape=jax.ShapeDtypeStruct((N_DEV,) + shard.shape, shard.dtype),
        in_specs=[_ANY], out_specs=_ANY,
        scratch_shapes=[pltpu.SemaphoreType.DMA((7,)), pltpu.SemaphoreType.DMA((7,)), pltpu.SemaphoreType.DMA(())],
    )(shard)


def _handshake(peers):
    barrier = pltpu.get_barrier_semaphore()
    for peer in peers:
        pl.semaphore_signal(barrier, inc=1, device_id=peer, device_id_type=_MESH)
    pl.semaphore_wait(barrier, len(peers))


_SEQUENCER = dict(axis_name="seq", num_cores=1)
_AG_COLLECTIVE_ID = 1
_RS_SIBLING_COLLECTIVE_ID = 2
_RS_CHIPS_COLLECTIVE_ID = 3


def _all_gather_seq(shard, *, name):
    def body(x_ref, out_ref, send_sems, recv_sems, local_sem):
        x, y, c, chips = _place()
        sibling = (x, y, 1 - c)
        _handshake([sibling] + [(*chip, c) for chip in chips])

        def slot(px, py, pc):
            return out_ref.at[4 * px + 2 * py + pc]

        def copy(k, block, to, src=None):
            return pltpu.make_async_remote_copy(
                src_ref=slot(*block) if src is None else src, dst_ref=slot(*block),
                send_sem=send_sems.at[k], recv_sem=recv_sems.at[k], device_id=to, device_id_type=_MESH)

        mine = pltpu.make_async_copy(x_ref, slot(x, y, c), local_sem)
        mine.start()
        first = [copy(0, (x, y, c), sibling, src=x_ref)]
        first += [copy(1 + j, (x, y, c), (*chip, c), src=x_ref) for j, chip in enumerate(chips)]
        for cp in first:
            cp.start()
        passed = [copy(4 + j, (*chip, c), sibling) for j, chip in enumerate(chips)]
        for j, chip in enumerate(chips):
            copy(1 + j, (*chip, c), (x, y, c)).wait_recv()
            passed[j].start()
        copy(0, sibling, (x, y, c)).wait_recv()
        for j, chip in enumerate(chips):
            copy(4 + j, (*chip, 1 - c), (x, y, c)).wait_recv()
        for cp in first + passed:
            cp.wait_send()
        mine.wait()

    return pl.kernel(
        body, out_type=jax.ShapeDtypeStruct((N_DEV,) + shard.shape, shard.dtype),
        mesh=plsc.ScalarSubcoreMesh(**_SEQUENCER), name=name,
        scratch_types=[pltpu.SemaphoreType.DMA((7,)), pltpu.SemaphoreType.DMA((7,)), pltpu.SemaphoreType.DMA(())],
        compiler_params=pltpu.CompilerParams(collective_id=_AG_COLLECTIVE_ID),
    )(shard)


def _exchange_sibling(g, *, name):
    def body(g_ref, out_ref, send_sems, recv_sems):
        x, y, c, _ = _place()
        sibling = (x, y, 1 - c)
        _handshake([sibling])
        copies = []
        for chip in range(4):
            cp = pltpu.make_async_remote_copy(
                src_ref=g_ref.at[2 * chip + (1 - c)], dst_ref=out_ref.at[chip],
                send_sem=send_sems.at[chip], recv_sem=recv_sems.at[chip], device_id=sibling, device_id_type=_MESH)
            cp.start()
            copies.append(cp)
        for cp in copies:
            cp.wait_recv()
        for cp in copies:
            cp.wait_send()

    return pl.kernel(
        body, out_type=jax.ShapeDtypeStruct((4,) + g.shape[1:], g.dtype),
        mesh=plsc.ScalarSubcoreMesh(**_SEQUENCER), name=name,
        scratch_types=[pltpu.SemaphoreType.DMA((4,)), pltpu.SemaphoreType.DMA((4,))],
        compiler_params=pltpu.CompilerParams(collective_id=_RS_SIBLING_COLLECTIVE_ID),
    )(g)


def _add_sibling(g, recv, *, name):
    _, r, cdim = g.shape
    tr, tc = _tile_2d(r, cdim, 6)
    g4 = g.reshape(4, 2, r, cdim)
    core = lax.axis_index("c").astype(jnp.int32).reshape(1)

    def body(c_ref, g_ref, r_ref, o_ref):
        o_ref[...] = (g_ref[...].astype(F32) + r_ref[...].astype(F32)).astype(o_ref.dtype)

    return pl.pallas_call(
        body, name=name,
        grid_spec=pltpu.PrefetchScalarGridSpec(
            num_scalar_prefetch=1, grid=(4, r // tr, cdim // tc),
            in_specs=[pl.BlockSpec((None, None, tr, tc), lambda ch, i, j, c_ref: (ch, c_ref[0], i, j)),
                      pl.BlockSpec((None, tr, tc), lambda ch, i, j, c_ref: (ch, i, j))],
            out_specs=pl.BlockSpec((None, tr, tc), lambda ch, i, j, c_ref: (ch, i, j))),
        out_shape=jax.ShapeDtypeStruct((4, r, cdim), g.dtype),
        compiler_params=_params(("parallel", "parallel", "parallel")),
    )(core, g4, recv)


def _exchange_chips(pch, *, name):
    def body(p_ref, out_ref, send_sems, recv_sems, local_sem):
        x, y, c, chips = _place()
        _handshake([(*chip, c) for chip in chips])
        me = 2 * x + y
        mine = pltpu.make_async_copy(p_ref.at[me], out_ref.at[me], local_sem)
        mine.start()
        copies = []
        for j, (px, py) in enumerate(chips):
            cp = pltpu.make_async_remote_copy(
                src_ref=p_ref.at[2 * px + py], dst_ref=out_ref.at[me],
                send_sem=send_sems.at[j], recv_sem=recv_sems.at[j], device_id=(px, py, c), device_id_type=_MESH)
            cp.start()
            copies.append(cp)
        for j, (px, py) in enumerate(chips):
            pltpu.make_async_remote_copy(
                src_ref=p_ref.at[me], dst_ref=out_ref.at[2 * px + py],
                send_sem=send_sems.at[j], recv_sem=recv_sems.at[j], device_id=(px, py, c), device_id_type=_MESH).wait_recv()
        for cp in copies:
            cp.wait_send()
        mine.wait()

    return pl.kernel(
        body, out_type=jax.ShapeDtypeStruct(pch.shape, pch.dtype),
        mesh=plsc.ScalarSubcoreMesh(**_SEQUENCER), name=name,
        scratch_types=[pltpu.SemaphoreType.DMA((3,)), pltpu.SemaphoreType.DMA((3,)), pltpu.SemaphoreType.DMA(())],
        compiler_params=pltpu.CompilerParams(collective_id=_RS_CHIPS_COLLECTIVE_ID),
    )(pch)


def _all_reduce_small(v, *, name):
    r = v.shape[0]

    def body(x_ref, out_ref, buf_ref, send_sems, recv_sems):
        x, y, c, chips = _place()
        sibling = (x, y, 1 - c)

        def slot(px, py, pc):
            return buf_ref.at[4 * px + 2 * py + pc]

        def copy(k, block, to, src=None):
            return pltpu.make_async_remote_copy(
                src_ref=slot(*block) if src is None else src, dst_ref=slot(*block),
                send_sem=send_sems.at[k], recv_sem=recv_sems.at[k], device_id=to, device_id_type=_MESH)

        first = [copy(0, (x, y, c), sibling, src=x_ref)]
        first += [copy(1 + j, (x, y, c), (*chip, c), src=x_ref) for j, chip in enumerate(chips)]
        for cp in first:
            cp.start()
        buf_ref[4 * x + 2 * y + c] = x_ref[...]
        passed = [copy(4 + j, (*chip, c), sibling) for j, chip in enumerate(chips)]
        for j, chip in enumerate(chips):
            copy(1 + j, (*chip, c), (x, y, c)).wait_recv()
            passed[j].start()
        copy(0, sibling, (x, y, c)).wait_recv()
        for j, chip in enumerate(chips):
            copy(4 + j, (*chip, 1 - c), (x, y, c)).wait_recv()
        for cp in first + passed:
            cp.wait_send()
        total = buf_ref[0]
        for k in range(1, N_DEV):
            total = total + buf_ref[k]
        out_ref[...] = total

    vm = pl.BlockSpec(memory_space=pltpu.VMEM)
    return pl.pallas_call(
        body, name=name, out_shape=jax.ShapeDtypeStruct(v.shape, v.dtype),
        in_specs=[vm], out_specs=vm,
        scratch_shapes=[pltpu.VMEM((N_DEV,) + v.shape, v.dtype), pltpu.SemaphoreType.DMA((7,)),
                        pltpu.SemaphoreType.DMA((7,))],
    )(v)


_ELEMENTWISE_VMEM = 24 * 1024 * 1024


def _tile_2d(r, c, n_arrays):
    per_block = _ELEMENTWISE_VMEM // (8 * n_arrays)
    tr = _tile(r, max(16, per_block // max(c, 128)), 16)
    if tr * c <= per_block:
        return tr, c
    return r, _tile(c, max(128, (per_block // r) // 128 * 128))


def _adam_math(w, g, m, v):
    m = ADAM_B1 * m + (1.0 - ADAM_B1) * g
    v = ADAM_B2 * v + (1.0 - ADAM_B2) * jnp.square(g)
    m_hat = m / (1.0 - ADAM_B1 ** ADAM_STEP)
    v_hat = v / (1.0 - ADAM_B2 ** ADAM_STEP)
    delta = -ADAM_LR * (m_hat / (jnp.sqrt(v_hat) + ADAM_EPS) + ADAM_WD * w)
    return delta, m, v


def _adam(w, g, m, v, *, name, parts=None):
    r, cdim = w.shape
    tr, tc = _tile_2d(r, cdim, 8)

    def body(w_ref, g_ref, m_ref, v_ref, go_ref, d_ref, mo_ref, vo_ref):
        if parts is None:
            g = g_ref[...]
        else:
            g = g_ref[0].astype(F32)
            for k in range(1, parts):
                g = g + g_ref[k].astype(F32)
        delta, m, v = _adam_math(w_ref[...], g, m_ref[...], v_ref[...])
        go_ref[...] = g
        d_ref[...] = delta
        mo_ref[...] = m
        vo_ref[...] = v

    blk = pl.BlockSpec((tr, tc), lambda i, j: (i, j))
    gblk = blk if parts is None else pl.BlockSpec((parts, tr, tc), lambda i, j: (0, i, j))
    out = jax.ShapeDtypeStruct((r, cdim), F32)
    return pl.pallas_call(
        body, name=name, grid=(r // tr, cdim // tc), in_specs=[blk, gblk, blk, blk], out_specs=[blk] * 4,
        out_shape=[out] * 4, compiler_params=_params(("parallel", "parallel")),
    )(w, g, m, v)


_BIG = ("w_in", "w_uq", "w_ukv", "w_o", "w_ffn_gate", "w_ffn_up", "w_ffn_down", "w_ple_gate", "w_ple_proj")
_WEIGHTS = ("w_in", "g_attn", "g_q_lora", "g_kv_lora", "w_uq", "w_ukv", "w_o", "g_ffn", "w_ffn_gate", "w_ffn_up",
            "conv_w", "conv_b", "w_ffn_down", "g_ple", "w_ple_gate", "w_ple_proj", "g_final")
_SMALL_PACK = (("g_attn", 1, D_MODEL), ("g_q_lora", 1, Q_LORA), ("g_kv_lora", 1, KV_LORA), ("g_ffn", 1, D_MODEL),
               ("conv_w", CONV_WIDTH, D_FF_PAD), ("conv_b", 1, D_FF_PAD), ("g_ple", 1, D_MODEL), ("g_final", 1, D_MODEL))


def _pack_small(gs):
    flat = jnp.concatenate([gs[n].reshape(-1) for n, _, _ in _SMALL_PACK])
    rows = -(-flat.shape[0] // 128)
    rows = -(-rows // 8) * 8
    return jnp.pad(flat, (0, rows * 128 - flat.shape[0])).reshape(rows, 128)


def _unpack_small(packed):
    flat = packed.reshape(-1)
    out, off = {}, 0
    for n, r, c in _SMALL_PACK:
        out[n] = flat[off:off + r * c].reshape(r, c)
        off += r * c
    return out


def kernel(x, p, w_in, g_attn, g_q_lora, g_kv_lora, w_uq, w_ukv, w_o, g_ffn, w_ffn_gate, w_ffn_up, conv_w, conv_b, w_ffn_down, g_ple, w_ple_gate, w_ple_proj, g_final, loss_target, m_w_in, m_g_attn, m_g_q_lora, m_g_kv_lora, m_w_uq, m_w_ukv, m_w_o, m_g_ffn, m_w_ffn_gate, m_w_ffn_up, m_conv_w, m_conv_b, m_w_ffn_down, m_g_ple, m_w_ple_gate, m_w_ple_proj, m_g_final, v_w_in, v_g_attn, v_g_q_lora, v_g_kv_lora, v_w_uq, v_w_ukv, v_w_o, v_g_ffn, v_w_ffn_gate, v_w_ffn_up, v_conv_w, v_conv_b, v_w_ffn_down, v_g_ple, v_w_ple_gate, v_w_ple_proj, v_g_final):
    given = dict(locals())
    wts = {n: given[n] for n in _WEIGHTS}
    mom = {n: given["m_" + n] for n in _WEIGHTS}
    var = {n: given["v_" + n] for n in _WEIGHTS}
    me = (4 * lax.axis_index("x") + 2 * lax.axis_index("y") + lax.axis_index("c")).astype(jnp.int32)
    ops = _ShardedWeights(wts, mom, var)

    conv_full = _all_gather(conv_w, name="ag_conv_w")[:, 0].transpose(1, 0, 2).reshape(CONV_WIDTH, D_FF)
    vec = {"g_attn": g_attn, "g_q_lora": g_q_lora, "g_kv_lora": g_kv_lora, "g_ffn": g_ffn, "g_ple": g_ple,
           "g_final": g_final[None, :], "conv_w": _ffn_vec_layout(conv_full), "conv_b": _ffn_vec_layout(conv_b)}

    loss_part, grad_x, gs = _local_step(x[0], p[0, 0], loss_target[0], vec, ops)
    loss = lax.psum(loss_part[0, 0], ("x", "y", "c"))

    small = _unpack_small(_all_reduce_small(ops.order.tie(_pack_small(gs)), name="ar_small"))
    conv_w_shards = _ffn_vec_shards(small["conv_w"])
    small_g = {
        "g_attn": small["g_attn"], "g_q_lora": small["g_q_lora"], "g_kv_lora": small["g_kv_lora"],
        "g_ffn": small["g_ffn"], "g_ple": small["g_ple"], "g_final": small["g_final"],
        "conv_b": _ffn_vec_shards(small["conv_b"]).reshape(1, D_FF),
        "conv_w": lax.dynamic_index_in_dim(conv_w_shards, me, axis=1, keepdims=False),
    }
    results = dict(ops.results)
    for n, g in small_g.items():
        shape = wts[n].shape
        outs = ops.order.run(_adam, wts[n].reshape(g.shape), g, mom[n].reshape(g.shape), var[n].reshape(g.shape),
                             name="adam_" + n)
        results[n] = tuple(a.reshape(shape) for a in outs)
    ops.update("w_in")
    results["w_in"] = ops.results["w_in"]

    return (loss, grad_x[None], *[results[n][0] for n in _WEIGHTS], *[results[n][1] for n in _WEIGHTS],
            *[results[n][2] for n in _WEIGHTS], *[results[n][3] for n in _WEIGHTS])


class _ShardedWeights:
    def __init__(self, wts, mom, var):
        self.wts, self.mom, self.var = wts, mom, var
        self.order = _Order()
        self.full, self.stage, self.results = {}, {}, {}

    def start_gather(self, name, after=None):
        payload = _shard_payload(name, self.wts[name][0])
        if after is not None:
            payload = lax.optimization_barrier((payload, after))[0]
        self.full[name] = _full_from_gathered(name, _all_gather_seq(payload, name="ag_" + name))

    def weight(self, name):
        return self.full[name]

    def grad(self, name, gfull):
        chunks = _grad_chunks(name, gfull)
        self.stage[name] = (chunks, _exchange_sibling(chunks, name="rs_sib_" + name))

    def reduce_add(self, name):
        chunks, from_sibling = self.stage[name]
        per_chip = self.order.run(_add_sibling, chunks, from_sibling, name="rs_add_" + name)
        self.stage[name] = _exchange_chips(per_chip, name="rs_chip_" + name)

    def update(self, name):
        parts = self.stage[name]
        if name == "w_uq":
            parts = _unpad_rows(name, parts)
        rows = lambda a: _rows_view(name, a[0])
        outs = self.order.run(_adam, rows(self.wts[name]), parts, rows(self.mom[name]), rows(self.var[name]),
                              name="adam_" + name, parts=4)
        self.results[name] = tuple(_rows_view(name, a)[None] for a in outs)
```

```python
import functools

import numpy as np

import jax
import jax.numpy as jnp
from jax import lax
from jax.experimental import pallas as pl
from jax.experimental.pallas import tpu as pltpu
from jax.experimental.pallas import tpu_sc as plsc

D_MODEL = 4096
CHUNK = 64
PLE_DIM = 256
RET_HEADS = 8
RET_HEAD_DIM = 256
RET_WIDTH = 2048
MLA_HEADS = 16
MLA_NOPE = 128
MLA_ROPE = 64
MLA_V = 128
Q_LORA = 1024
KV_LORA = 512
D_FF = 11008
CONV_WIDTH = 3
ROPE_BASE = 10000.0
EPS = 1e-6
IN_WIDTH = 9792
ADAM_LR, ADAM_B1, ADAM_B2, ADAM_EPS, ADAM_WD, ADAM_STEP = 0.001, 0.9, 0.999, 1e-08, 0.01, 10

IN_WIDTH_PAD = 10240
D_FF_PAD = 11264
MLA_QK_PAD = 256
Q_WIDTH_PAD = MLA_HEADS * MLA_QK_PAD

N_DEV = 8
MXU_DTYPE = jnp.bfloat16
ATTN_BLOCK = 512
HEADS_PER_STEP = 2
VMEM_LIMIT = 56 * 1024 * 1024

F32 = jnp.float32


def _tile(n, want, align=128):
    if n <= want:
        return n
    t = (want // align) * align
    while t >= align:
        if n % t == 0:
            return t
        t -= align
    return n


def _params(sem):
    return pltpu.CompilerParams(dimension_semantics=sem, vmem_limit_bytes=VMEM_LIMIT)


def _sigmoid(x):
    return 1.0 / (1.0 + jnp.exp(-x))


def _matmul(a, b, *, name, ta=False, tb=False, out_dtype=F32, add=None, tm=1024, tn=512, tk=4096):
    m, k = (a.shape[1], a.shape[0]) if ta else a.shape
    k2, n = (b.shape[1], b.shape[0]) if tb else b.shape
    assert k == k2, (a.shape, b.shape, ta, tb)
    tk = _tile(k, tk)
    nk = k // tk
    if nk > 1:
        tn = 2 * tn
    tm, tn = _tile(m, tm), _tile(n, tn)
    dims = (((0 if ta else 1,), (1 if tb else 0,)), ((), ()))

    def body(*refs):
        a_ref, b_ref, o_ref = refs[0], refs[1], refs[3 if add is not None else 2]
        c_ref = refs[2] if add is not None else None
        part = lax.dot_general(a_ref[...].astype(MXU_DTYPE), b_ref[...].astype(MXU_DTYPE), dims,
                               preferred_element_type=F32)

        def finish(r):
            if c_ref is not None:
                r = r + c_ref[...].astype(F32)
            o_ref[...] = r.astype(out_dtype)

        if nk == 1:
            finish(part)
            return
        acc_ref = refs[-1]
        kk = pl.program_id(2)

        @pl.when(kk == 0)
        def _():
            acc_ref[...] = part

        @pl.when((kk > 0) & (kk < nk - 1))
        def _():
            acc_ref[...] += part

        @pl.when(kk == nk - 1)
        def _():
            finish(acc_ref[...] + part)

    a_spec = pl.BlockSpec((tk, tm), lambda i, j, kk: (kk, i)) if ta else pl.BlockSpec((tm, tk), lambda i, j, kk: (i, kk))
    b_spec = pl.BlockSpec((tn, tk), lambda i, j, kk: (j, kk)) if tb else pl.BlockSpec((tk, tn), lambda i, j, kk: (kk, j))
    in_specs = [a_spec, b_spec]
    args = [a, b]
    if add is not None:
        in_specs.append(pl.BlockSpec((tm, tn), lambda i, j, kk: (i, j)))
        args.append(add)
    return pl.pallas_call(
        body, name=name, grid=(m // tm, n // tn, nk),
        in_specs=in_specs, out_specs=pl.BlockSpec((tm, tn), lambda i, j, kk: (i, j)),
        out_shape=jax.ShapeDtypeStruct((m, n), out_dtype),
        scratch_shapes=[] if nk == 1 else [pltpu.VMEM((tm, tn), F32)],
        compiler_params=_params(("parallel", "parallel", "arbitrary")),
    )(*args)


def _rms_fwd(x, g, *, name):
    t, d = x.shape
    tr = _tile(t, 256, 8)

    def body(x_ref, g_ref, o_ref):
        xf = x_ref[...]
        r = lax.rsqrt(jnp.mean(xf * xf, axis=-1, keepdims=True) + EPS)
        o_ref[...] = (xf * r * g_ref[...]).astype(o_ref.dtype)

    return pl.pallas_call(
        body, name=name, grid=(t // tr,),
        in_specs=[pl.BlockSpec((tr, d), lambda i: (i, 0)), pl.BlockSpec((1, d), lambda i: (0, 0))],
        out_specs=pl.BlockSpec((tr, d), lambda i: (i, 0)),
        out_shape=jax.ShapeDtypeStruct((t, d), MXU_DTYPE),
        compiler_params=_params(("parallel",)),
    )(x, g)


def _rms_bwd(x, dhn, g, res, *, name, low_copy=False):
    t, d = x.shape
    tr = _tile(t, 256, 8)

    def body(*refs):
        if res is None:
            x_ref, dh_ref, g_ref = refs[:3]
            outs = refs[3:]
            res_ref = None
        else:
            x_ref, dh_ref, g_ref, res_ref = refs[:4]
            outs = refs[4:]
        dx_ref, dg_ref = outs[0], outs[-1]
        xf = x_ref[...]
        dh = dh_ref[...].astype(F32)
        r = lax.rsqrt(jnp.mean(xf * xf, axis=-1, keepdims=True) + EPS)
        dyg = dh * g_ref[...]
        dx = r * dyg - xf * (r * r * r) * jnp.mean(dyg * xf, axis=-1, keepdims=True)
        if res_ref is not None:
            dx = dx + res_ref[...]
        dx_ref[...] = dx
        if low_copy:
            outs[1][...] = dx.astype(outs[1].dtype)
        part = jnp.sum(dh * xf * r, axis=0, keepdims=True)

        @pl.when(pl.program_id(0) == 0)
        def _():
            dg_ref[...] = part

        @pl.when(pl.program_id(0) > 0)
        def _():
            dg_ref[...] += part

    row = pl.BlockSpec((tr, d), lambda i: (i, 0))
    vec = pl.BlockSpec((1, d), lambda i: (0, 0))
    in_specs = [row, row, vec] + ([] if res is None else [row])
    args = [x, dhn, g] + ([] if res is None else [res])
    out_specs = [row] + ([row] if low_copy else []) + [vec]
    out_shape = [jax.ShapeDtypeStruct((t, d), F32)] + ([jax.ShapeDtypeStruct((t, d), MXU_DTYPE)] if low_copy else []) \
        + [jax.ShapeDtypeStruct((1, d), F32)]
    return pl.pallas_call(
        body, name=name, grid=(t // tr,), in_specs=in_specs, out_specs=out_specs, out_shape=out_shape,
        compiler_params=_params(("arbitrary",)),
    )(*args)


def _rope_tables(t):
    pos = jnp.arange(t, dtype=F32)[:, None]
    inv_r = 1.0 / (ROPE_BASE ** (jnp.arange(0, RET_HEAD_DIM, 2, dtype=F32) / RET_HEAD_DIM))
    ang_r = pos * inv_r[None, :]
    inv_m = 1.0 / (ROPE_BASE ** (jnp.arange(0, MLA_ROPE, 2, dtype=F32) / MLA_ROPE))
    ang_m = pos * inv_m[None, :]
    cm, sm = jnp.cos(ang_m), jnp.sin(ang_m)
    z = jnp.zeros_like(cm)
    cos_m = jnp.concatenate([cm, cm, z, z], axis=1)
    sin_m = jnp.concatenate([-sm, sm, z, z], axis=1)
    return jnp.cos(ang_r), jnp.sin(ang_r), cos_m, sin_m


def _rope256(x, c, s, inverse=False):
    x1, x2 = x[:, :128], x[:, 128:]
    if inverse:
        s = -s
    return jnp.concatenate([x1 * c - x2 * s, x2 * c + x1 * s], axis=1)


def _rope64(x, cos_m, sin_m, inverse=False):
    lane = lax.broadcasted_iota(jnp.int32, x.shape, 1)
    partner = jnp.where(lane < 32, pltpu.roll(x, 96, 1), pltpu.roll(x, 32, 1))
    s = -sin_m if inverse else sin_m
    return x * cos_m + partner * s


def _ret_prep(proj, cos_r, sin_r):
    t = proj.shape[0]
    tr = _tile(t, 512, 8)
    hb = RET_WIDTH // RET_HEAD_DIM

    def body(q_ref, k_ref, v_ref, c_ref, s_ref, qo_ref, ko_ref, vo_ref):
        c, s = c_ref[...], s_ref[...]
        qo_ref[...] = _rope256(q_ref[...], c, s).astype(qo_ref.dtype)
        ko_ref[...] = (_rope256(k_ref[...], c, s) * (RET_HEAD_DIM ** -0.5)).astype(ko_ref.dtype)
        vo_ref[...] = v_ref[...].astype(vo_ref.dtype)

    head = lambda off: pl.BlockSpec((tr, RET_HEAD_DIM), lambda i, h: (i, h + off))
    tab = pl.BlockSpec((tr, 128), lambda i, h: (i, 0))
    out = jax.ShapeDtypeStruct((t, RET_WIDTH), MXU_DTYPE)
    return pl.pallas_call(
        body, name="ret_prep", grid=(t // tr, RET_HEADS),
        in_specs=[head(0), head(hb), head(2 * hb), tab, tab],
        out_specs=[head(0), head(0), head(0)], out_shape=[out, out, out],
        compiler_params=_params(("parallel", "parallel")),
    )(proj, proj, proj, cos_r, sin_r)


def _ret_log_gamma():
    return jnp.asarray(np.log1p(-np.exp2(-5.0 - np.arange(RET_HEADS, dtype=np.float64))), dtype=F32)


def _decay_full(lg, i, j, blk):
    r = lax.broadcasted_iota(jnp.int32, (blk, 1), 0).astype(F32)
    c = lax.broadcasted_iota(jnp.int32, (1, blk), 1).astype(F32)
    off = ((i - j) * blk).astype(F32)
    return jnp.exp(lg * r), jnp.exp(lg * (off - c))


def _decay_diag(lg, blk):
    r = lax.broadcasted_iota(jnp.int32, (blk, blk), 0)
    c = lax.broadcasted_iota(jnp.int32, (blk, blk), 1)
    ok = (c // CHUNK) <= (r // CHUNK)
    return jnp.where(ok, jnp.exp(lg * jnp.abs(r - c).astype(F32)), 0.0)


_NT = (((1,), (1,)), ((), ()))
_TN = (((0,), (0,)), ((), ()))
_NN = (((1,), (0,)), ((), ()))


def _causal_pairs(nb, query_major):
    if query_major:
        pairs = [(i, j) for i in range(nb) for j in range(i + 1)]
    else:
        pairs = [(i, j) for j in range(nb) for i in range(j, nb)]
    arr = np.asarray(pairs, dtype=np.int32)
    return jnp.asarray(arr[:, 0]), jnp.asarray(arr[:, 1])


def _ret_fwd(q, k, v, proj, lg):
    t = q.shape[0]
    blk = _tile(t, ATTN_BLOCK)
    nb = t // blk
    hps, d = HEADS_PER_STEP, RET_HEAD_DIM
    gate_off = 3 * RET_WIDTH // (hps * d)

    def body(ii_ref, jj_ref, lg_ref, q_ref, k_ref, v_ref, g_ref, raw_ref, ro_ref, acc_ref):
        hg, pair = pl.program_id(0), pl.program_id(1)
        i, j = ii_ref[pair], jj_ref[pair]

        @pl.when(j == 0)
        def _():
            acc_ref[...] = jnp.zeros_like(acc_ref)

        def step(diag):
            for h in range(hps):
                cols = slice(h * d, (h + 1) * d)
                lgh = lg_ref[hg * hps + h]
                s = lax.dot_general(q_ref[:, cols], k_ref[:, cols], _NT, preferred_element_type=F32)
                if diag:
                    w = s * _decay_diag(lgh, blk)
                else:
                    a, b = _decay_full(lgh, i, j, blk)
                    w = s * a * b
                acc_ref[h] += lax.dot_general(w.astype(MXU_DTYPE), v_ref[:, cols], _NN, preferred_element_type=F32)

        @pl.when(j < i)
        def _():
            step(False)

        @pl.when(j == i)
        def _():
            step(True)
            for h in range(hps):
                cols = slice(h * d, (h + 1) * d)
                o = acc_ref[h]
                raw_ref[:, cols] = o
                mu = jnp.mean(o, axis=-1, keepdims=True)
                var = jnp.mean(jnp.square(o - mu), axis=-1, keepdims=True)
                hn = (o - mu) * lax.rsqrt(var + EPS)
                g = g_ref[:, cols]
                ro_ref[:, cols] = (g * _sigmoid(g) * hn).astype(ro_ref.dtype)

    qs = pl.BlockSpec((blk, hps * d), lambda h, p, ii, jj: (ii[p], h))
    ks = pl.BlockSpec((blk, hps * d), lambda h, p, ii, jj: (jj[p], h))
    gs = pl.BlockSpec((blk, hps * d), lambda h, p, ii, jj: (ii[p], h + gate_off))
    ii, jj = _causal_pairs(nb, query_major=True)
    return pl.pallas_call(
        body, name="ret_fwd",
        grid_spec=pltpu.PrefetchScalarGridSpec(
            num_scalar_prefetch=2, grid=(RET_HEADS // hps, ii.shape[0]),
            in_specs=[pl.BlockSpec(memory_space=pltpu.SMEM), qs, ks, ks, gs], out_specs=[qs, qs],
            scratch_shapes=[pltpu.VMEM((hps, blk, d), F32)]),
        out_shape=[jax.ShapeDtypeStruct((t, RET_WIDTH), F32), jax.ShapeDtypeStruct((t, RET_WIDTH), MXU_DTYPE)],
        compiler_params=_params(("parallel", "arbitrary")),
    )(ii, jj, lg, q, k, v, proj)


def _ret_gate_bwd(raw, proj, dattn):
    t = raw.shape[0]
    tr = _tile(t, 512, 8)
    gate_off = 3 * RET_WIDTH // RET_HEAD_DIM

    def body(o_ref, g_ref, d_ref, do_ref, dg_ref):
        o, g, d = o_ref[...], g_ref[...], d_ref[...]
        mu = jnp.mean(o, axis=-1, keepdims=True)
        rstd = lax.rsqrt(jnp.mean(jnp.square(o - mu), axis=-1, keepdims=True) + EPS)
        hn = (o - mu) * rstd
        sg = _sigmoid(g)
        dg_ref[...] = (d * hn * (sg * (1.0 + g * (1.0 - sg)))).astype(dg_ref.dtype)
        dhn = d * (g * sg)
        do = rstd * (dhn - jnp.mean(dhn, axis=-1, keepdims=True) - hn * jnp.mean(dhn * hn, axis=-1, keepdims=True))
        do_ref[...] = do.astype(do_ref.dtype)

    hs = pl.BlockSpec((tr, RET_HEAD_DIM), lambda i, h: (i, h))
    gs = pl.BlockSpec((tr, RET_HEAD_DIM), lambda i, h: (i, h + gate_off))
    out = jax.ShapeDtypeStruct((t, RET_WIDTH), MXU_DTYPE)
    return pl.pallas_call(
        body, name="ret_gate_bwd", grid=(t // tr, RET_HEADS),
        in_specs=[hs, gs, hs], out_specs=[hs, hs], out_shape=[out, out],
        compiler_params=_params(("parallel", "parallel")),
    )(raw, proj, dattn)


def _ret_bwd(q, k, v, do, lg):
    t = q.shape[0]
    blk = _tile(t, ATTN_BLOCK)
    nb = t // blk

    hps, d = HEADS_PER_STEP, RET_HEAD_DIM

    def body(ii_ref, jj_ref, lg_ref, q_ref, k_ref, v_ref, do_ref, dq_ref, dk_ref, dv_ref, dk_acc, dv_acc):
        hg, pair = pl.program_id(0), pl.program_id(1)
        i, j = ii_ref[pair], jj_ref[pair]

        @pl.when(pair == 0)
        def _():
            dq_ref[...] = jnp.zeros_like(dq_ref)

        @pl.when(i == j)
        def _():
            dk_acc[...] = jnp.zeros_like(dk_acc)
            dv_acc[...] = jnp.zeros_like(dv_acc)

        def step(diag):
            rows = pl.ds(pl.multiple_of(i * blk, blk), blk)
            for h in range(hps):
                cols = slice(h * d, (h + 1) * d)
                lgh = lg_ref[hg * hps + h]
                if diag:
                    decay = _decay_diag(lgh, blk)
                else:
                    a, b = _decay_full(lgh, i, j, blk)
                    decay = a * b
                qb, kb, vb, dob = q_ref[:, cols], k_ref[:, cols], v_ref[:, cols], do_ref[:, cols]
                s = lax.dot_general(qb, kb, _NT, preferred_element_type=F32)
                w = (s * decay).astype(MXU_DTYPE)
                dv_acc[h] += lax.dot_general(w, dob, _TN, preferred_element_type=F32)
                dw = lax.dot_general(dob, vb, _NT, preferred_element_type=F32)
                ds = (dw * decay).astype(MXU_DTYPE)
                dq_ref[rows, cols] += lax.dot_general(ds, kb, _NN, preferred_element_type=F32)
                dk_acc[h] += lax.dot_general(ds, qb, _TN, preferred_element_type=F32)

        @pl.when(i > j)
        def _():
            step(False)

        @pl.when(i == j)
        def _():
            step(True)

        @pl.when(i == nb - 1)
        def _():
            for h in range(hps):
                cols = slice(h * d, (h + 1) * d)
                dk_ref[:, cols] = dk_acc[h]
                dv_ref[:, cols] = dv_acc[h].astype(dv_ref.dtype)

    qs = pl.BlockSpec((blk, hps * d), lambda h, p, ii, jj: (ii[p], h))
    ks = pl.BlockSpec((blk, hps * d), lambda h, p, ii, jj: (jj[p], h))
    ii, jj = _causal_pairs(nb, query_major=False)
    return pl.pallas_call(
        body, name="ret_bwd",
        grid_spec=pltpu.PrefetchScalarGridSpec(
            num_scalar_prefetch=2, grid=(RET_HEADS // hps, ii.shape[0]),
            in_specs=[pl.BlockSpec(memory_space=pltpu.SMEM), qs, ks, ks, qs],
            out_specs=[pl.BlockSpec((t, hps * d), lambda h, p, ii, jj: (0, h)), ks, ks],
            scratch_shapes=[pltpu.VMEM((hps, blk, d), F32), pltpu.VMEM((hps, blk, d), F32)]),
        out_shape=[jax.ShapeDtypeStruct((t, RET_WIDTH), F32), jax.ShapeDtypeStruct((t, RET_WIDTH), F32),
                   jax.ShapeDtypeStruct((t, RET_WIDTH), MXU_DTYPE)],
        compiler_params=_params(("parallel", "arbitrary")),
    )(ii, jj, lg, q, k, v, do)


def _ret_unrope(dq, dk, cos_r, sin_r):
    t = dq.shape[0]
    tr = _tile(t, 512, 8)

    def body(dq_ref, dk_ref, c_ref, s_ref, oq_ref, ok_ref):
        c, s = c_ref[...], s_ref[...]
        oq_ref[...] = _rope256(dq_ref[...], c, s, inverse=True).astype(oq_ref.dtype)
        ok_ref[...] = (_rope256(dk_ref[...], c, s, inverse=True) * (RET_HEAD_DIM ** -0.5)).astype(ok_ref.dtype)

    hs = pl.BlockSpec((tr, RET_HEAD_DIM), lambda i, h: (i, h))
    tab = pl.BlockSpec((tr, 128), lambda i, h: (i, 0))
    out = jax.ShapeDtypeStruct((t, RET_WIDTH), MXU_DTYPE)
    return pl.pallas_call(
        body, name="ret_unrope", grid=(t // tr, RET_HEADS),
        in_specs=[hs, hs, tab, tab], out_specs=[hs, hs], out_shape=[out, out],
        compiler_params=_params(("parallel", "parallel")),
    )(dq, dk, cos_r, sin_r)


def _mla_prep(cq, ckv, kr, g_q, g_kv, cos_m, sin_m):
    t = cq.shape[0]
    tr = _tile(t, 512, 8)

    def body(cq_ref, ckv_ref, kr_ref, gq_ref, gkv_ref, c_ref, s_ref, cqn_ref, kvn_ref, kro_ref):
        for x_ref, g_ref, o_ref in ((cq_ref, gq_ref, cqn_ref), (ckv_ref, gkv_ref, kvn_ref)):
            xf = x_ref[...]
            r = lax.rsqrt(jnp.mean(xf * xf, axis=-1, keepdims=True) + EPS)
            o_ref[...] = (xf * r * g_ref[...]).astype(o_ref.dtype)
        kro_ref[...] = _rope64(kr_ref[...], c_ref[...], s_ref[...]).astype(kro_ref.dtype)

    row = lambda w: pl.BlockSpec((tr, w), lambda i: (i, 0))
    vec = lambda w: pl.BlockSpec((1, w), lambda i: (0, 0))
    return pl.pallas_call(
        body, name="mla_prep", grid=(t // tr,),
        in_specs=[row(Q_LORA), row(KV_LORA), row(128), vec(Q_LORA), vec(KV_LORA), row(128), row(128)],
        out_specs=[row(Q_LORA), row(KV_LORA), row(128)],
        out_shape=[jax.ShapeDtypeStruct((t, Q_LORA), MXU_DTYPE), jax.ShapeDtypeStruct((t, KV_LORA), MXU_DTYPE),
                   jax.ShapeDtypeStruct((t, 128), MXU_DTYPE)],
        compiler_params=_params(("parallel",)),
    )(cq, ckv, kr, g_q, g_kv, cos_m, sin_m)


def _mla_q_rope(q_lin, cos_m, sin_m, *, inverse, name):
    t = q_lin.shape[0]
    tr = _tile(t, 256, 8)

    def body(q_ref, c_ref, s_ref, o_ref):
        c, s = c_ref[...], s_ref[...]
        for h in range(MLA_HEADS):
            lo = h * MLA_QK_PAD
            o_ref[:, lo:lo + MLA_NOPE] = q_ref[:, lo:lo + MLA_NOPE].astype(o_ref.dtype)
            roped = _rope64(q_ref[:, lo + MLA_NOPE:lo + MLA_QK_PAD].astype(F32), c, s, inverse=inverse)
            o_ref[:, lo + MLA_NOPE:lo + MLA_QK_PAD] = roped.astype(o_ref.dtype)

    rows = pl.BlockSpec((tr, Q_WIDTH_PAD), lambda i: (i, 0))
    tab = pl.BlockSpec((tr, 128), lambda i: (i, 0))
    return pl.pallas_call(
        body, name=name, grid=(t // tr,),
        in_specs=[rows, tab, tab], out_specs=rows, out_shape=jax.ShapeDtypeStruct((t, Q_WIDTH_PAD), MXU_DTYPE),
        compiler_params=_params(("parallel",)),
    )(q_lin, cos_m, sin_m)


_MLA_SCALE = (MLA_NOPE + MLA_ROPE) ** -0.5
_NEG = -1e30


def _mla_mask(blk):
    r = lax.broadcasted_iota(jnp.int32, (blk, blk), 0)
    c = lax.broadcasted_iota(jnp.int32, (blk, blk), 1)
    return (c // CHUNK) <= (r // CHUNK)


def _mla_fwd(q, kv, kr):
    t = q.shape[0]
    blk = _tile(t, ATTN_BLOCK)
    nb = t // blk

    hps, dq_, dv_ = HEADS_PER_STEP, MLA_QK_PAD, MLA_V

    def body(ii_ref, jj_ref, q_ref, kv_ref, kr_ref, o_ref, lse_ref, m_ref, l_ref, acc_ref):
        pair = pl.program_id(1)
        i, j = ii_ref[pair], jj_ref[pair]

        @pl.when(j == 0)
        def _():
            m_ref[...] = jnp.full_like(m_ref, _NEG)
            l_ref[...] = jnp.zeros_like(l_ref)
            acc_ref[...] = jnp.zeros_like(acc_ref)

        def step(masked):
            krb = kr_ref[...]
            for h in range(hps):
                kb = jnp.concatenate([kv_ref[:, h * dq_:h * dq_ + MLA_NOPE], krb], axis=1)
                vb = kv_ref[:, h * dq_ + MLA_NOPE:(h + 1) * dq_]
                s = lax.dot_general(q_ref[:, h * dq_:(h + 1) * dq_], kb, _NT, preferred_element_type=F32) * _MLA_SCALE
                if masked:
                    s = jnp.where(_mla_mask(blk), s, _NEG)
                m_prev = m_ref[h]
                m_new = jnp.maximum(m_prev, jnp.max(s, axis=-1, keepdims=True))
                alpha = jnp.exp(m_prev - m_new)
                p = jnp.exp(s - m_new)
                l_ref[h] = alpha * l_ref[h] + jnp.sum(p, axis=-1, keepdims=True)
                acc_ref[h] = alpha * acc_ref[h] + lax.dot_general(p.astype(MXU_DTYPE), vb, _NN,
                                                                  preferred_element_type=F32)
                m_ref[h] = m_new

        @pl.when(j < i)
        def _():
            step(False)

        @pl.when(j == i)
        def _():
            step(True)
            for h in range(hps):
                cols = slice(h * dv_, (h + 1) * dv_)
                o_ref[:, cols] = (acc_ref[h] / l_ref[h]).astype(o_ref.dtype)
                lse_ref[:, cols] = jnp.broadcast_to(m_ref[h] + jnp.log(l_ref[h]), (blk, dv_))

    os_ = pl.BlockSpec((blk, hps * dv_), lambda h, p, ii, jj: (ii[p], h))
    ii, jj = _causal_pairs(nb, query_major=True)
    return pl.pallas_call(
        body, name="mla_fwd",
        grid_spec=pltpu.PrefetchScalarGridSpec(
            num_scalar_prefetch=2, grid=(MLA_HEADS // hps, ii.shape[0]),
            in_specs=[pl.BlockSpec((blk, hps * dq_), lambda h, p, ii, jj: (ii[p], h)),
                      pl.BlockSpec((blk, hps * dq_), lambda h, p, ii, jj: (jj[p], h)),
                      pl.BlockSpec((blk, 128), lambda h, p, ii, jj: (jj[p], 0))],
            out_specs=[os_, os_],
            scratch_shapes=[pltpu.VMEM((hps, blk, 1), F32), pltpu.VMEM((hps, blk, 1), F32),
                            pltpu.VMEM((hps, blk, dv_), F32)]),
        out_shape=[jax.ShapeDtypeStruct((t, MLA_HEADS * MLA_V), MXU_DTYPE),
                   jax.ShapeDtypeStruct((t, MLA_HEADS * 128), F32)],
        compiler_params=_params(("parallel", "arbitrary")),
    )(ii, jj, q, kv, kr)


def _mla_bwd(q, kv, kr, o, lse, dattn):
    t = q.shape[0]
    blk = _tile(t, ATTN_BLOCK)
    nb = t // blk
    hps, dq_, dv_ = HEADS_PER_STEP, MLA_QK_PAD, MLA_V
    do_off = RET_WIDTH // (hps * dv_)

    def body(ii_ref, jj_ref, q_ref, kv_ref, kr_ref, o_ref, lse_ref, do_ref, dq_ref, dk_ref, dv_ref, dk_acc, dv_acc):
        pair = pl.program_id(1)
        i, j = ii_ref[pair], jj_ref[pair]

        @pl.when(pair == 0)
        def _():
            dq_ref[...] = jnp.zeros_like(dq_ref)

        @pl.when(i == j)
        def _():
            dk_acc[...] = jnp.zeros_like(dk_acc)
            dv_acc[...] = jnp.zeros_like(dv_acc)

        def step(masked):
            krb = kr_ref[...]
            rows = pl.ds(pl.multiple_of(i * blk, blk), blk)
            for h in range(hps):
                qcols, vcols = slice(h * dq_, (h + 1) * dq_), slice(h * dv_, (h + 1) * dv_)
                qb = q_ref[:, qcols]
                kb = jnp.concatenate([kv_ref[:, h * dq_:h * dq_ + MLA_NOPE], krb], axis=1)
                vb = kv_ref[:, h * dq_ + MLA_NOPE:(h + 1) * dq_]
                dof = do_ref[:, vcols]
                dob = dof.astype(MXU_DTYPE)
                s = lax.dot_general(qb, kb, _NT, preferred_element_type=F32) * _MLA_SCALE
                if masked:
                    s = jnp.where(_mla_mask(blk), s, _NEG)
                p = jnp.exp(s - lse_ref[:, vcols][:, :1])
                delta = jnp.sum(dof * o_ref[:, vcols].astype(F32), axis=-1, keepdims=True)
                dv_acc[h] += lax.dot_general(p.astype(MXU_DTYPE), dob, _TN, preferred_element_type=F32)
                dp = lax.dot_general(dob, vb, _NT, preferred_element_type=F32)
                ds = (p * (dp - delta) * _MLA_SCALE).astype(MXU_DTYPE)
                dq_ref[rows, qcols] += lax.dot_general(ds, kb, _NN, preferred_element_type=F32)
                dk_acc[h] += lax.dot_general(ds, qb, _TN, preferred_element_type=F32)

        @pl.when(i > j)
        def _():
            step(False)

        @pl.when(i == j)
        def _():
            step(True)

        @pl.when(i == nb - 1)
        def _():
            for h in range(hps):
                dk_ref[:, h * dq_:(h + 1) * dq_] = dk_acc[h]
                dv_ref[:, h * dv_:(h + 1) * dv_] = dv_acc[h].astype(dv_ref.dtype)

    qmap = lambda off: (lambda h, p, ii, jj: (ii[p], h + off))
    kmap = lambda h, p, ii, jj: (jj[p], h)
    ii, jj = _causal_pairs(nb, query_major=False)
    return pl.pallas_call(
        body, name="mla_bwd",
        grid_spec=pltpu.PrefetchScalarGridSpec(
            num_scalar_prefetch=2, grid=(MLA_HEADS // hps, ii.shape[0]),
            in_specs=[pl.BlockSpec((blk, hps * dq_), qmap(0)), pl.BlockSpec((blk, hps * dq_), kmap),
                      pl.BlockSpec((blk, 128), lambda h, p, ii, jj: (jj[p], 0)),
                      pl.BlockSpec((blk, hps * dv_), qmap(0)), pl.BlockSpec((blk, hps * dv_), qmap(0)),
                      pl.BlockSpec((blk, hps * dv_), qmap(do_off))],
            out_specs=[pl.BlockSpec((t, hps * dq_), lambda h, p, ii, jj: (0, h)),
                       pl.BlockSpec((blk, hps * dq_), kmap), pl.BlockSpec((blk, hps * dv_), kmap)],
            scratch_shapes=[pltpu.VMEM((hps, blk, dq_), F32), pltpu.VMEM((hps, blk, dv_), F32)]),
        out_shape=[jax.ShapeDtypeStruct((t, Q_WIDTH_PAD), F32), jax.ShapeDtypeStruct((t, Q_WIDTH_PAD), F32),
                   jax.ShapeDtypeStruct((t, MLA_HEADS * MLA_V), MXU_DTYPE)],
        compiler_params=_params(("parallel", "arbitrary")),
    )(ii, jj, q, kv, kr, o, lse, dattn)


def _mla_kv_grad(dk, dv, cos_m, sin_m):
    t = dk.shape[0]
    tr = _tile(t, 256, 8)

    def body(dk_ref, dv_ref, c_ref, s_ref, dkv_ref, dkr_ref):
        acc = jnp.zeros((tr, 128), F32)
        for h in range(MLA_HEADS):
            dkv_ref[:, h * 256:h * 256 + 128] = dk_ref[:, h * 256:h * 256 + 128].astype(dkv_ref.dtype)
            dkv_ref[:, h * 256 + 128:h * 256 + 256] = dv_ref[:, h * 128:(h + 1) * 128].astype(dkv_ref.dtype)
            acc = acc + dk_ref[:, h * 256 + 128:h * 256 + 256]
        dkr_ref[...] = _rope64(acc, c_ref[...], s_ref[...], inverse=True).astype(dkr_ref.dtype)

    row = lambda w: pl.BlockSpec((tr, w), lambda i: (i, 0))
    return pl.pallas_call(
        body, name="mla_kv_grad", grid=(t // tr,),
        in_specs=[row(Q_WIDTH_PAD), row(MLA_HEADS * MLA_V), row(128), row(128)],
        out_specs=[row(Q_WIDTH_PAD), row(128)],
        out_shape=[jax.ShapeDtypeStruct((t, Q_WIDTH_PAD), MXU_DTYPE), jax.ShapeDtypeStruct((t, 128), MXU_DTYPE)],
        compiler_params=_params(("parallel",)),
    )(dk, dv, cos_m, sin_m)


_FFN_COLS = 256
_FFN_ROWS = 256


def _shift_down(cur, prev8, n):
    row = lax.broadcasted_iota(jnp.int32, cur.shape, 0)
    out = pltpu.roll(cur, n, 0)
    for r in range(n):
        out = jnp.where(row == r, prev8[8 - n + r:8 - n + r + 1, :], out)
    return out


def _shift_up(cur, next8, n):
    rows = cur.shape[0]
    row = lax.broadcasted_iota(jnp.int32, cur.shape, 0)
    out = pltpu.roll(cur, rows - n, 0)
    for r in range(n):
        out = jnp.where(row == rows - n + r, next8[r:r + 1, :], out)
    return out


def _conv_pre(g_ref, cw_ref, cb_ref, c, rc):
    r0 = pl.multiple_of(c * rc, rc)
    cur = g_ref[pl.ds(r0, rc), :]
    prev8 = g_ref[pl.ds(pl.multiple_of(jnp.maximum(r0 - 8, 0), 8), 8), :]
    prev8 = jnp.where(c > 0, prev8, 0.0)
    s1, s2 = _shift_down(cur, prev8, 1), _shift_down(cur, prev8, 2)
    a = cb_ref[...] + cw_ref[2:3, :] * cur + cw_ref[1:2, :] * s1 + cw_ref[0:1, :] * s2
    return r0, cur, s1, s2, a


def _ffn_act_fwd(gpre, u, cw, cb):
    t, f = gpre.shape
    tc = _tile(f, _FFN_COLS)
    rc = _tile(t, _FFN_ROWS, 8)

    def body(g_ref, u_ref, cw_ref, cb_ref, o_ref):
        def chunk(c, carry):
            r0, _, _, _, a = _conv_pre(g_ref, cw_ref, cb_ref, c, rc)
            o_ref[pl.ds(r0, rc), :] = (a * _sigmoid(a) * u_ref[pl.ds(r0, rc), :]).astype(o_ref.dtype)
            return carry
        lax.fori_loop(0, t // rc, chunk, 0)

    col = pl.BlockSpec((t, tc), lambda j: (0, j))
    return pl.pallas_call(
        body, name="ffn_act_fwd", grid=(f // tc,),
        in_specs=[col, col, pl.BlockSpec((CONV_WIDTH, tc), lambda j: (0, j)), pl.BlockSpec((1, tc), lambda j: (0, j))],
        out_specs=col, out_shape=jax.ShapeDtypeStruct((t, f), MXU_DTYPE),
        compiler_params=_params(("parallel",)),
    )(gpre, u, cw, cb)


def _ffn_act_bwd(gpre, u, dact, cw, cb):
    t, f = gpre.shape
    tc = _tile(f, _FFN_COLS)
    rc = _tile(t, _FFN_ROWS, 8)
    nc = t // rc

    def body(g_ref, u_ref, d_ref, cw_ref, cb_ref, dg_ref, du_ref, dcw_ref, dcb_ref, da_ref):
        def chunk(c, carry):
            w0, w1, w2, b = carry
            r0, cur, s1, s2, a = _conv_pre(g_ref, cw_ref, cb_ref, c, rc)
            sg = _sigmoid(a)
            d = d_ref[pl.ds(r0, rc), :]
            du_ref[pl.ds(r0, rc), :] = (d * (a * sg)).astype(du_ref.dtype)
            da = d * u_ref[pl.ds(r0, rc), :] * (sg * (1.0 + a * (1.0 - sg)))
            da_ref[pl.ds(r0, rc), :] = da
            return (w0 + jnp.sum(da * s2, axis=0, keepdims=True), w1 + jnp.sum(da * s1, axis=0, keepdims=True),
                    w2 + jnp.sum(da * cur, axis=0, keepdims=True), b + jnp.sum(da, axis=0, keepdims=True))
        z = jnp.zeros((1, tc), F32)
        w0, w1, w2, b = lax.fori_loop(0, nc, chunk, (z, z, z, z))
        dcw_ref[0:1, :] = w0
        dcw_ref[1:2, :] = w1
        dcw_ref[2:3, :] = w2
        dcb_ref[...] = b

        def chunk2(c, carry):
            r0 = pl.multiple_of(c * rc, rc)
            cur = da_ref[pl.ds(r0, rc), :]
            nxt = da_ref[pl.ds(pl.multiple_of(jnp.minimum(r0 + rc, t - 8), 8), 8), :]
            nxt = jnp.where(c < nc - 1, nxt, 0.0)
            dg = cw_ref[2:3, :] * cur + cw_ref[1:2, :] * _shift_up(cur, nxt, 1) + cw_ref[0:1, :] * _shift_up(cur, nxt, 2)
            dg_ref[pl.ds(r0, rc), :] = dg.astype(dg_ref.dtype)
            return carry
        lax.fori_loop(0, nc, chunk2, 0)

    col = pl.BlockSpec((t, tc), lambda j: (0, j))
    w3 = pl.BlockSpec((CONV_WIDTH, tc), lambda j: (0, j))
    w1 = pl.BlockSpec((1, tc), lambda j: (0, j))
    low = jax.ShapeDtypeStruct((t, f), MXU_DTYPE)
    return pl.pallas_call(
        body, name="ffn_act_bwd", grid=(f // tc,),
        in_specs=[col, col, col, w3, w1], out_specs=[col, col, w3, w1],
        out_shape=[low, low, jax.ShapeDtypeStruct((CONV_WIDTH, f), F32), jax.ShapeDtypeStruct((1, f), F32)],
        scratch_shapes=[pltpu.VMEM((t, tc), F32)],
        compiler_params=_params(("parallel",)),
    )(gpre, u, dact, cw, cb)


def _head_fwd_bwd(h2, glin, pp, target, g_final):
    t, d = h2.shape
    tr = _tile(t, 128, 8)

    def body(h_ref, gl_ref, pp_ref, t_ref, g_ref, loss_ref, dh_ref, dgl_ref, dpp_ref, dg_ref):
        gate = _sigmoid(gl_ref[...])
        ppv = pp_ref[...]
        h3 = h_ref[...] + gate * ppv
        r = lax.rsqrt(jnp.mean(h3 * h3, axis=-1, keepdims=True) + EPS)
        yh = h3 * r
        g = g_ref[...]
        diff = yh * g - t_ref[...]
        lpart = 0.5 * jnp.sum(jnp.mean(diff * diff, axis=-1, keepdims=True), axis=0, keepdims=True)
        dy = diff * (1.0 / d)
        dyg = dy * g
        dh3 = r * dyg - h3 * (r * r * r) * jnp.mean(dyg * h3, axis=-1, keepdims=True)
        dh_ref[...] = dh3
        dgl_ref[...] = (dh3 * ppv * gate * (1.0 - gate)).astype(dgl_ref.dtype)
        dpp_ref[...] = (dh3 * gate).astype(dpp_ref.dtype)
        dgp = jnp.sum(dy * yh, axis=0, keepdims=True)

        @pl.when(pl.program_id(0) == 0)
        def _():
            loss_ref[...] = jnp.broadcast_to(lpart, loss_ref.shape)
            dg_ref[...] = dgp

        @pl.when(pl.program_id(0) > 0)
        def _():
            loss_ref[...] += jnp.broadcast_to(lpart, loss_ref.shape)
            dg_ref[...] += dgp

    row = pl.BlockSpec((tr, d), lambda i: (i, 0))
    vec = pl.BlockSpec((1, d), lambda i: (0, 0))
    low = jax.ShapeDtypeStruct((t, d), MXU_DTYPE)
    return pl.pallas_call(
        body, name="head_fwd_bwd", grid=(t // tr,),
        in_specs=[row, row, row, row, vec],
        out_specs=[pl.BlockSpec((8, 128), lambda i: (0, 0)), row, row, row, vec],
        out_shape=[jax.ShapeDtypeStruct((8, 128), F32), jax.ShapeDtypeStruct((t, d), F32), low, low,
                   jax.ShapeDtypeStruct((1, d), F32)],
        compiler_params=_params(("arbitrary",)),
    )(h2, glin, pp, target, g_final)


class _Order:
    def __init__(self):
        self.last = None

    def tie(self, x):
        return x if self.last is None else lax.optimization_barrier((x, self.last))[0]

    def run(self, fn, first, *args, **kwargs):
        out = fn(self.tie(first), *args, **kwargs)
        self.last = out[0] if isinstance(out, (tuple, list)) else out
        return out


def _local_step(x, p, target, vec, ops):
    t = x.shape[0]
    cos_r, sin_r, cos_m, sin_m = _rope_tables(t)
    lg = _ret_log_gamma()
    low = MXU_DTYPE
    run = ops.order.run
    w = ops.weight

    ops.start_gather("w_in")
    hn1 = run(_rms_fwd, x, vec["g_attn"], name="rms1_fwd")
    for n in ("w_uq", "w_ukv", "w_o"):
        ops.start_gather(n, after=hn1)
    proj = run(_matmul, hn1, w("w_in"), tb=True, name="mm_proj")
    ops.start_gather("w_ffn_gate", after=proj)
    rq, rk, rv = run(_ret_prep, proj, cos_r, sin_r)
    ops.start_gather("w_ffn_up", after=rq)
    c0 = 4 * RET_WIDTH
    cq = proj[:, c0:c0 + Q_LORA]
    ckv = proj[:, c0 + Q_LORA:c0 + Q_LORA + KV_LORA]
    kr_in = proj[:, c0 + Q_LORA + KV_LORA:c0 + Q_LORA + KV_LORA + 128]
    cqn, kvn, kr = run(_mla_prep, cq, ckv, kr_in, vec["g_q_lora"], vec["g_kv_lora"], cos_m, sin_m)
    ops.start_gather("w_ffn_down", after=cqn)
    q_lin = run(_matmul, cqn, w("w_uq"), tb=True, name="mm_q")
    q = run(_mla_q_rope, q_lin, cos_m, sin_m, inverse=False, name="mla_q_rope")
    kv = run(_matmul, kvn, w("w_ukv"), tb=True, name="mm_kv", out_dtype=low)
    mo, lse = run(_mla_fwd, q, kv, kr)
    ops.start_gather("w_ple_gate", after=mo)
    ops.start_gather("w_ple_proj", after=mo)
    ret_raw, ro = run(_ret_fwd, rq, rk, rv, proj, lg)
    attn = jnp.concatenate([ro, mo], axis=1)
    h1 = run(_matmul, attn, w("w_o"), name="mm_o", add=x)
    hn2 = run(_rms_fwd, h1, vec["g_ffn"], name="rms2_fwd")
    gpre = run(_matmul, hn2, w("w_ffn_gate"), tb=True, name="mm_gate")
    u = run(_matmul, hn2, w("w_ffn_up"), tb=True, name="mm_up")
    act = run(_ffn_act_fwd, gpre, u, vec["conv_w"], vec["conv_b"])
    h2 = run(_matmul, act, w("w_ffn_down"), name="mm_down", add=h1)
    hn3 = run(_rms_fwd, h2, vec["g_ple"], name="rms3_fwd")
    glin = run(_matmul, hn3, w("w_ple_gate"), name="mm_ple_gate")
    p_low = p.astype(low)
    pp = run(_matmul, p_low, w("w_ple_proj"), tb=True, name="mm_ple_proj")
    loss_part, dh3, dglin, dpp, dg_final = run(_head_fwd_bwd, h2, glin, pp, target, vec["g_final"])

    ops.grad("w_ple_proj", run(_matmul, dpp, p_low, ta=True, name="mm_d_ple_proj", out_dtype=low))
    ops.grad("w_ple_gate", run(_matmul, hn3, dglin, ta=True, name="mm_d_ple_gate", out_dtype=low))
    dhn3 = run(_matmul, dglin, w("w_ple_gate"), tb=True, name="mm_dhn3")
    dh2, dh2_low, dg_ple = run(_rms_bwd, h2, dhn3, vec["g_ple"], dh3, name="rms3_bwd", low_copy=True)
    ops.reduce_add("w_ple_proj")
    ops.reduce_add("w_ple_gate")
    ops.grad("w_ffn_down", run(_matmul, act, dh2_low, ta=True, name="mm_d_down", out_dtype=low))
    dact = run(_matmul, dh2_low, w("w_ffn_down"), tb=True, name="mm_dact")
    ops.reduce_add("w_ffn_down")
    dgpre, du, dconv_w, dconv_b = run(_ffn_act_bwd, gpre, u, dact, vec["conv_w"], vec["conv_b"])
    ops.update("w_ple_proj")
    ops.update("w_ple_gate")
    ops.grad("w_ffn_gate", run(_matmul, dgpre, hn2, ta=True, name="mm_d_gate", out_dtype=low))
    ops.grad("w_ffn_up", run(_matmul, du, hn2, ta=True, name="mm_d_up", out_dtype=low))
    ops.reduce_add("w_ffn_gate")
    dhn2 = run(_matmul, dgpre, w("w_ffn_gate"), name="mm_dhn2_a")
    ops.reduce_add("w_ffn_up")
    dhn2 = run(_matmul, du, w("w_ffn_up"), name="mm_dhn2_b", add=dhn2)
    dh1, dh1_low, dg_ffn = run(_rms_bwd, h1, dhn2, vec["g_ffn"], dh2, name="rms2_bwd", low_copy=True)
    ops.update("w_ffn_down")
    ops.grad("w_o", run(_matmul, attn, dh1_low, ta=True, name="mm_d_o", out_dtype=low))
    dattn = run(_matmul, dh1_low, w("w_o"), tb=True, name="mm_dattn")
    ops.reduce_add("w_o")

    dq_r, dk_full, dv = run(_mla_bwd, q, kv, kr, mo, lse, dattn)
    ops.update("w_ffn_gate")
    dq_lin = run(_mla_q_rope, dq_r, cos_m, sin_m, inverse=True, name="mla_q_unrope")
    dkv, dkr = run(_mla_kv_grad, dk_full, dv, cos_m, sin_m)
    ops.grad("w_uq", run(_matmul, dq_lin, cqn, ta=True, name="mm_d_uq", out_dtype=low))
    dcqn = run(_matmul, dq_lin, w("w_uq"), name="mm_dcqn")
    ops.grad("w_ukv", run(_matmul, dkv, kvn, ta=True, name="mm_d_ukv", out_dtype=low))
    dkvn = run(_matmul, dkv, w("w_ukv"), name="mm_dkvn")
    dcq, dcq_low, dg_q = run(_rms_bwd, cq, dcqn, vec["g_q_lora"], None, name="rmsq_bwd", low_copy=True)
    dckv, dckv_low, dg_kv = run(_rms_bwd, ckv, dkvn, vec["g_kv_lora"], None, name="rmskv_bwd", low_copy=True)
    ops.reduce_add("w_uq")
    ops.reduce_add("w_ukv")

    do_ret, drg = run(_ret_gate_bwd, ret_raw, proj, dattn)
    dq_ret, dk_ret, drv = run(_ret_bwd, rq, rk, rv, do_ret, lg)
    drq, drk = run(_ret_unrope, dq_ret, dk_ret, cos_r, sin_r)

    pad = jnp.zeros((t, IN_WIDTH_PAD - IN_WIDTH - 64), low)
    dproj = jnp.concatenate([drq, drk, drv, drg, dcq_low, dckv_low, dkr, pad], axis=1)
    ops.grad("w_in", run(_matmul, dproj, hn1, ta=True, name="mm_d_in", out_dtype=low))
    for n in ("w_ffn_up", "w_o", "w_uq", "w_ukv"):
        ops.update(n)
    ops.reduce_add("w_in")
    dhn1 = run(_matmul, dproj, w("w_in"), name="mm_dhn1")
    grad_x, dg_attn = run(_rms_bwd, x, dhn1, vec["g_attn"], dh1, name="rms1_bwd")

    gs = {"g_attn": dg_attn, "g_q_lora": dg_q, "g_kv_lora": dg_kv, "g_ffn": dg_ffn, "conv_w": dconv_w,
          "conv_b": dconv_b, "g_ple": dg_ple, "g_final": dg_final}
    return loss_part, grad_x, gs


_COL_SHARDED = ("w_in", "w_uq", "w_ukv", "w_ffn_gate", "w_ffn_up", "w_ple_proj")
_FFN_SHARD = D_FF // N_DEV
_FFN_SHARD_PAD = D_FF_PAD // N_DEV
_HEADS_PER_SHARD = MLA_HEADS // N_DEV
_QK = MLA_NOPE + MLA_ROPE


def _pad_rows(name, a):
    lead = a.shape[:-2]
    if name == "w_uq":
        a = a.reshape(lead + (_HEADS_PER_SHARD, _QK, a.shape[-1]))
        a = jnp.pad(a, [(0, 0)] * len(lead) + [(0, 0), (0, MLA_QK_PAD - _QK), (0, 0)])
        return a.reshape(lead + (_HEADS_PER_SHARD * MLA_QK_PAD, a.shape[-1]))
    if name in ("w_ffn_gate", "w_ffn_up", "w_ffn_down"):
        return jnp.pad(a, [(0, 0)] * len(lead) + [(0, _FFN_SHARD_PAD - _FFN_SHARD), (0, 0)])
    return a


def _unpad_rows(name, a):
    lead = a.shape[:-2]
    if name == "w_uq":
        a = a.reshape(lead + (_HEADS_PER_SHARD, MLA_QK_PAD, a.shape[-1]))[..., :_QK, :]
        return a.reshape(lead + (_HEADS_PER_SHARD * _QK, a.shape[-1]))
    if name in ("w_ffn_gate", "w_ffn_up", "w_ffn_down"):
        return a[..., :_FFN_SHARD, :]
    return a


def _rows_view(name, a):
    return jnp.swapaxes(a, 0, 1) if name in _COL_SHARDED else a


def _shard_payload(name, shard):
    return _pad_rows(name, _rows_view(name, shard).astype(MXU_DTYPE))


def _full_from_gathered(name, g):
    full = g.reshape(g.shape[0] * g.shape[1], g.shape[2])
    if name == "w_in":
        full = jnp.pad(full, ((0, IN_WIDTH_PAD - IN_WIDTH), (0, 0)))
    return full


def _grad_chunks(name, gfull):
    if name == "w_in":
        gfull = gfull[:IN_WIDTH]
    return gfull.reshape(N_DEV, gfull.shape[0] // N_DEV, gfull.shape[1])


def _ffn_vec_layout(a):
    a = a.reshape(a.shape[0], N_DEV, _FFN_SHARD)
    return jnp.pad(a, ((0, 0), (0, 0), (0, _FFN_SHARD_PAD - _FFN_SHARD))).reshape(a.shape[0], D_FF_PAD)


def _ffn_vec_shards(a):
    return a.reshape(a.shape[0], N_DEV, _FFN_SHARD_PAD)[:, :, :_FFN_SHARD]


_MESH = pl.DeviceIdType.MESH
_ANY = pl.BlockSpec(memory_space=pl.ANY)


def _place():
    x, y, c = lax.axis_index("x"), lax.axis_index("y"), lax.axis_index("c")
    chips = [(1 - x, y), (x, 1 - y), (1 - x, 1 - y)]
    return x, y, c, chips


def _handshake(peers):
    barrier = pltpu.get_barrier_semaphore()
    for peer in peers:
        pl.semaphore_signal(barrier, inc=1, device_id=peer, device_id_type=_MESH)
    pl.semaphore_wait(barrier, len(peers))


_SEQUENCER = dict(axis_name="seq", num_cores=1)
_AG_COLLECTIVE_ID = 1
_RS_SIBLING_COLLECTIVE_ID = 2
_RS_CHIPS_COLLECTIVE_ID = 3


def _all_gather_seq(shard, *, name):
    def body(x_ref, out_ref, send_sems, recv_sems, local_sem):
        x, y, c, chips = _place()
        sibling = (x, y, 1 - c)
        _handshake([sibling] + [(*chip, c) for chip in chips])

        def slot(px, py, pc):
            return out_ref.at[4 * px + 2 * py + pc]

        def copy(k, block, to, src=None):
            return pltpu.make_async_remote_copy(
                src_ref=slot(*block) if src is None else src, dst_ref=slot(*block),
                send_sem=send_sems.at[k], recv_sem=recv_sems.at[k], device_id=to, device_id_type=_MESH)

        mine = pltpu.make_async_copy(x_ref, slot(x, y, c), local_sem)
        mine.start()
        first = [copy(0, (x, y, c), sibling, src=x_ref)]
        first += [copy(1 + j, (x, y, c), (*chip, c), src=x_ref) for j, chip in enumerate(chips)]
        for cp in first:
            cp.start()
        passed = [copy(4 + j, (*chip, c), sibling) for j, chip in enumerate(chips)]
        for j, chip in enumerate(chips):
            copy(1 + j, (*chip, c), (x, y, c)).wait_recv()
            passed[j].start()
        copy(0, sibling, (x, y, c)).wait_recv()
        for j, chip in enumerate(chips):
            copy(4 + j, (*chip, 1 - c), (x, y, c)).wait_recv()
        for cp in first + passed:
            cp.wait_send()
        mine.wait()

    return pl.kernel(
        body, out_type=jax.ShapeDtypeStruct((N_DEV,) + shard.shape, shard.dtype),
        mesh=plsc.ScalarSubcoreMesh(**_SEQUENCER), name=name,
        scratch_types=[pltpu.SemaphoreType.DMA((7,)), pltpu.SemaphoreType.DMA((7,)), pltpu.SemaphoreType.DMA(())],
        compiler_params=pltpu.CompilerParams(collective_id=_AG_COLLECTIVE_ID),
    )(shard)


def _exchange_sibling(g, *, name):
    def body(g_ref, out_ref, send_sems, recv_sems):
        x, y, c, _ = _place()
        sibling = (x, y, 1 - c)
        _handshake([sibling])
        copies = []
        for chip in range(4):
            cp = pltpu.make_async_remote_copy(
                src_ref=g_ref.at[2 * chip + (1 - c)], dst_ref=out_ref.at[chip],
                send_sem=send_sems.at[chip], recv_sem=recv_sems.at[chip], device_id=sibling, device_id_type=_MESH)
            cp.start()
            copies.append(cp)
        for cp in copies:
            cp.wait_recv()
        for cp in copies:
            cp.wait_send()

    return pl.kernel(
        body, out_type=jax.ShapeDtypeStruct((4,) + g.shape[1:], g.dtype),
        mesh=plsc.ScalarSubcoreMesh(**_SEQUENCER), name=name,
        scratch_types=[pltpu.SemaphoreType.DMA((4,)), pltpu.SemaphoreType.DMA((4,))],
        compiler_params=pltpu.CompilerParams(collective_id=_RS_SIBLING_COLLECTIVE_ID),
    )(g)


def _add_sibling(g, recv, *, name):
    _, r, cdim = g.shape
    tr, tc = _tile_2d(r, cdim, 6)
    g4 = g.reshape(4, 2, r, cdim)
    core = lax.axis_index("c").astype(jnp.int32).reshape(1)

    def body(c_ref, g_ref, r_ref, o_ref):
        o_ref[...] = (g_ref[...].astype(F32) + r_ref[...].astype(F32)).astype(o_ref.dtype)

    return pl.pallas_call(
        body, name=name,
        grid_spec=pltpu.PrefetchScalarGridSpec(
            num_scalar_prefetch=1, grid=(4, r // tr, cdim // tc),
            in_specs=[pl.BlockSpec((None, None, tr, tc), lambda ch, i, j, c_ref: (ch, c_ref[0], i, j)),
                      pl.BlockSpec((None, tr, tc), lambda ch, i, j, c_ref: (ch, i, j))],
            out_specs=pl.BlockSpec((None, tr, tc), lambda ch, i, j, c_ref: (ch, i, j))),
        out_shape=jax.ShapeDtypeStruct((4, r, cdim), g.dtype),
        compiler_params=_params(("parallel", "parallel", "parallel")),
    )(core, g4, recv)


def _exchange_chips(pch, *, name):
    def body(p_ref, out_ref, send_sems, recv_sems, local_sem):
        x, y, c, chips = _place()
        _handshake([(*chip, c) for chip in chips])
        me = 2 * x + y
        mine = pltpu.make_async_copy(p_ref.at[me], out_ref.at[me], local_sem)
        mine.start()
        copies = []
        for j, (px, py) in enumerate(chips):
            cp = pltpu.make_async_remote_copy(
                src_ref=p_ref.at[2 * px + py], dst_ref=out_ref.at[me],
                send_sem=send_sems.at[j], recv_sem=recv_sems.at[j], device_id=(px, py, c), device_id_type=_MESH)
            cp.start()
            copies.append(cp)
        for j, (px, py) in enumerate(chips):
            pltpu.make_async_remote_copy(
                src_ref=p_ref.at[me], dst_ref=out_ref.at[2 * px + py],
                send_sem=send_sems.at[j], recv_sem=recv_sems.at[j], device_id=(px, py, c), device_id_type=_MESH).wait_recv()
        for cp in copies:
            cp.wait_send()
        mine.wait()

    return pl.kernel(
        body, out_type=jax.ShapeDtypeStruct(pch.shape, pch.dtype),
        mesh=plsc.ScalarSubcoreMesh(**_SEQUENCER), name=name,
        scratch_types=[pltpu.SemaphoreType.DMA((3,)), pltpu.SemaphoreType.DMA((3,)), pltpu.SemaphoreType.DMA(())],
        compiler_params=pltpu.CompilerParams(collective_id=_RS_CHIPS_COLLECTIVE_ID),
    )(pch)


def _all_reduce_small(v, *, name):
    r = v.shape[0]

    def body(x_ref, out_ref, buf_ref, send_sems, recv_sems):
        x, y, c, chips = _place()
        sibling = (x, y, 1 - c)

        def slot(px, py, pc):
            return buf_ref.at[4 * px + 2 * py + pc]

        def copy(k, block, to, src=None):
            return pltpu.make_async_remote_copy(
                src_ref=slot(*block) if src is None else src, dst_ref=slot(*block),
                send_sem=send_sems.at[k], recv_sem=recv_sems.at[k], device_id=to, device_id_type=_MESH)

        first = [copy(0, (x, y, c), sibling, src=x_ref)]
        first += [copy(1 + j, (x, y, c), (*chip, c), src=x_ref) for j, chip in enumerate(chips)]
        for cp in first:
            cp.start()
        buf_ref[4 * x + 2 * y + c] = x_ref[...]
        passed = [copy(4 + j, (*chip, c), sibling) for j, chip in enumerate(chips)]
        for j, chip in enumerate(chips):
            copy(1 + j, (*chip, c), (x, y, c)).wait_recv()
            passed[j].start()
        copy(0, sibling, (x, y, c)).wait_recv()
        for j, chip in enumerate(chips):
            copy(4 + j, (*chip, 1 - c), (x, y, c)).wait_recv()
        for cp in first + passed:
            cp.wait_send()
        total = buf_ref[0]
        for k in range(1, N_DEV):
            total = total + buf_ref[k]
        out_ref[...] = total

    vm = pl.BlockSpec(memory_space=pltpu.VMEM)
    return pl.pallas_call(
        body, name=name, out_shape=jax.ShapeDtypeStruct(v.shape, v.dtype),
        in_specs=[vm], out_specs=vm,
        scratch_shapes=[pltpu.VMEM((N_DEV,) + v.shape, v.dtype), pltpu.SemaphoreType.DMA((7,)),
                        pltpu.SemaphoreType.DMA((7,))],
    )(v)


_ELEMENTWISE_VMEM = 24 * 1024 * 1024


def _tile_2d(r, c, n_arrays):
    per_block = _ELEMENTWISE_VMEM // (8 * n_arrays)
    tr = _tile(r, max(16, per_block // max(c, 128)), 16)
    if tr * c <= per_block:
        return tr, c
    return r, _tile(c, max(128, (per_block // r) // 128 * 128))


def _adam_math(w, g, m, v):
    m = ADAM_B1 * m + (1.0 - ADAM_B1) * g
    v = ADAM_B2 * v + (1.0 - ADAM_B2) * jnp.square(g)
    m_hat = m / (1.0 - ADAM_B1 ** ADAM_STEP)
    v_hat = v / (1.0 - ADAM_B2 ** ADAM_STEP)
    delta = -ADAM_LR * (m_hat / (jnp.sqrt(v_hat) + ADAM_EPS) + ADAM_WD * w)
    return delta, m, v


def _adam(w, g, m, v, *, name, parts=None):
    r, cdim = w.shape
    tr, tc = _tile_2d(r, cdim, 8)

    def body(w_ref, g_ref, m_ref, v_ref, go_ref, d_ref, mo_ref, vo_ref):
        if parts is None:
            g = g_ref[...]
        else:
            g = g_ref[0].astype(F32)
            for k in range(1, parts):
                g = g + g_ref[k].astype(F32)
        delta, m, v = _adam_math(w_ref[...], g, m_ref[...], v_ref[...])
        go_ref[...] = g
        d_ref[...] = delta
        mo_ref[...] = m
        vo_ref[...] = v

    blk = pl.BlockSpec((tr, tc), lambda i, j: (i, j))
    gblk = blk if parts is None else pl.BlockSpec((parts, tr, tc), lambda i, j: (0, i, j))
    out = jax.ShapeDtypeStruct((r, cdim), F32)
    return pl.pallas_call(
        body, name=name, grid=(r // tr, cdim // tc), in_specs=[blk, gblk, blk, blk], out_specs=[blk] * 4,
        out_shape=[out] * 4, compiler_params=_params(("parallel", "parallel")),
    )(w, g, m, v)


_BIG = ("w_in", "w_uq", "w_ukv", "w_o", "w_ffn_gate", "w_ffn_up", "w_ffn_down", "w_ple_gate", "w_ple_proj")
_WEIGHTS = ("w_in", "g_attn", "g_q_lora", "g_kv_lora", "w_uq", "w_ukv", "w_o", "g_ffn", "w_ffn_gate", "w_ffn_up",
            "conv_w", "conv_b", "w_ffn_down", "g_ple", "w_ple_gate", "w_ple_proj", "g_final")
_SMALL_PACK = (("g_attn", 1, D_MODEL), ("g_q_lora", 1, Q_LORA), ("g_kv_lora", 1, KV_LORA), ("g_ffn", 1, D_MODEL),
               ("conv_w", CONV_WIDTH, D_FF_PAD), ("conv_b", 1, D_FF_PAD), ("g_ple", 1, D_MODEL), ("g_final", 1, D_MODEL))


def _pack_small(gs):
    flat = jnp.concatenate([gs[n].reshape(-1) for n, _, _ in _SMALL_PACK])
    rows = -(-flat.shape[0] // 128)
    rows = -(-rows // 8) * 8
    return jnp.pad(flat, (0, rows * 128 - flat.shape[0])).reshape(rows, 128)


def _unpack_small(packed):
    flat = packed.reshape(-1)
    out, off = {}, 0
    for n, r, c in _SMALL_PACK:
        out[n] = flat[off:off + r * c].reshape(r, c)
        off += r * c
    return out


def kernel(x, p, w_in, g_attn, g_q_lora, g_kv_lora, w_uq, w_ukv, w_o, g_ffn, w_ffn_gate, w_ffn_up, conv_w, conv_b, w_ffn_down, g_ple, w_ple_gate, w_ple_proj, g_final, loss_target, m_w_in, m_g_attn, m_g_q_lora, m_g_kv_lora, m_w_uq, m_w_ukv, m_w_o, m_g_ffn, m_w_ffn_gate, m_w_ffn_up, m_conv_w, m_conv_b, m_w_ffn_down, m_g_ple, m_w_ple_gate, m_w_ple_proj, m_g_final, v_w_in, v_g_attn, v_g_q_lora, v_g_kv_lora, v_w_uq, v_w_ukv, v_w_o, v_g_ffn, v_w_ffn_gate, v_w_ffn_up, v_conv_w, v_conv_b, v_w_ffn_down, v_g_ple, v_w_ple_gate, v_w_ple_proj, v_g_final):
    given = dict(locals())
    wts = {n: given[n] for n in _WEIGHTS}
    mom = {n: given["m_" + n] for n in _WEIGHTS}
    var = {n: given["v_" + n] for n in _WEIGHTS}
    me = (4 * lax.axis_index("x") + 2 * lax.axis_index("y") + lax.axis_index("c")).astype(jnp.int32)
    ops = _ShardedWeights(wts, mom, var)

    conv_full = _all_gather_seq(conv_w, name="ag_conv_w")[:, 0].transpose(1, 0, 2).reshape(CONV_WIDTH, D_FF)
    vec = {"g_attn": g_attn, "g_q_lora": g_q_lora, "g_kv_lora": g_kv_lora, "g_ffn": g_ffn, "g_ple": g_ple,
           "g_final": g_final[None, :], "conv_w": _ffn_vec_layout(conv_full), "conv_b": _ffn_vec_layout(conv_b)}

    loss_part, grad_x, gs = _local_step(x[0], p[0, 0], loss_target[0], vec, ops)
    loss = lax.psum(loss_part[0, 0], ("x", "y", "c"))

    small = _unpack_small(_all_reduce_small(ops.order.tie(_pack_small(gs)), name="ar_small"))
    conv_w_shards = _ffn_vec_shards(small["conv_w"])
    small_g = {
        "g_attn": small["g_attn"], "g_q_lora": small["g_q_lora"], "g_kv_lora": small["g_kv_lora"],
        "g_ffn": small["g_ffn"], "g_ple": small["g_ple"], "g_final": small["g_final"],
        "conv_b": _ffn_vec_shards(small["conv_b"]).reshape(1, D_FF),
        "conv_w": lax.dynamic_index_in_dim(conv_w_shards, me, axis=1, keepdims=False),
    }
    results = dict(ops.results)
    for n, g in small_g.items():
        shape = wts[n].shape
        outs = ops.order.run(_adam, wts[n].reshape(g.shape), g, mom[n].reshape(g.shape), var[n].reshape(g.shape),
                             name="adam_" + n)
        results[n] = tuple(a.reshape(shape) for a in outs)
    ops.update("w_in")
    results["w_in"] = ops.results["w_in"]

    return (loss, grad_x[None], *[results[n][0] for n in _WEIGHTS], *[results[n][1] for n in _WEIGHTS],
            *[results[n][2] for n in _WEIGHTS], *[results[n][3] for n in _WEIGHTS])


class _ShardedWeights:
    def __init__(self, wts, mom, var):
        self.wts, self.mom, self.var = wts, mom, var
        self.order = _Order()
        self.full, self.stage, self.results = {}, {}, {}

    def start_gather(self, name, after=None):
        payload = _shard_payload(name, self.wts[name][0])
        if after is not None:
            payload = lax.optimization_barrier((payload, after))[0]
        self.full[name] = _full_from_gathered(name, _all_gather_seq(payload, name="ag_" + name))

    def weight(self, name):
        return self.full[name]

    def grad(self, name, gfull):
        chunks = _grad_chunks(name, gfull)
        self.stage[name] = (chunks, _exchange_sibling(chunks, name="rs_sib_" + name))

    def reduce_add(self, name):
        chunks, from_sibling = self.stage[name]
        per_chip = self.order.run(_add_sibling, chunks, from_sibling, name="rs_add_" + name)
        self.stage[name] = _exchange_chips(per_chip, name="rs_chip_" + name)

    def update(self, name):
        parts = self.stage[name]
        if name == "w_uq":
            parts = _unpad_rows(name, parts)
        rows = lambda a: _rows_view(name, a[0])
        outs = self.order.run(_adam, rows(self.wts[name]), parts, rows(self.mom[name]), rows(self.var[name]),
                              name="adam_" + name, parts=4)
        self.results[name] = tuple(_rows_view(name, a)[None] for a in outs)
```

```python
import functools

import numpy as np

import jax
import jax.numpy as jnp
from jax import lax
from jax.experimental import pallas as pl
from jax.experimental.pallas import tpu as pltpu
from jax.experimental.pallas import tpu_sc as plsc

D_MODEL = 4096
CHUNK = 64
PLE_DIM = 256
RET_HEADS = 8
RET_HEAD_DIM = 256
RET_WIDTH = 2048
MLA_HEADS = 16
MLA_NOPE = 128
MLA_ROPE = 64
MLA_V = 128
Q_LORA = 1024
KV_LORA = 512
D_FF = 11008
CONV_WIDTH = 3
ROPE_BASE = 10000.0
EPS = 1e-6
IN_WIDTH = 9792
ADAM_LR, ADAM_B1, ADAM_B2, ADAM_EPS, ADAM_WD, ADAM_STEP = 0.001, 0.9, 0.999, 1e-08, 0.01, 10

IN_WIDTH_PAD = 10240
D_FF_PAD = 11264
MLA_QK_PAD = 256
Q_WIDTH_PAD = MLA_HEADS * MLA_QK_PAD

N_DEV = 8
MXU_DTYPE = jnp.bfloat16
ATTN_BLOCK = 512
HEADS_PER_STEP = 2
VMEM_LIMIT = 56 * 1024 * 1024

F32 = jnp.float32


def _tile(n, want, align=128):
    if n <= want:
        return n
    t = (want // align) * align
    while t >= align:
        if n % t == 0:
            return t
        t -= align
    return n


def _params(sem):
    return pltpu.CompilerParams(dimension_semantics=sem, vmem_limit_bytes=VMEM_LIMIT)


def _sigmoid(x):
    return 1.0 / (1.0 + jnp.exp(-x))


def _matmul(a, b, *, name, ta=False, tb=False, out_dtype=F32, add=None, tm=1024, tn=512, tk=4096):
    m, k = (a.shape[1], a.shape[0]) if ta else a.shape
    k2, n = (b.shape[1], b.shape[0]) if tb else b.shape
    assert k == k2, (a.shape, b.shape, ta, tb)
    tk = _tile(k, tk)
    nk = k // tk
    if nk > 1:
        tn = 2 * tn
    tm, tn = _tile(m, tm), _tile(n, tn)
    dims = (((0 if ta else 1,), (1 if tb else 0,)), ((), ()))

    def body(*refs):
        a_ref, b_ref, o_ref = refs[0], refs[1], refs[3 if add is not None else 2]
        c_ref = refs[2] if add is not None else None
        part = lax.dot_general(a_ref[...].astype(MXU_DTYPE), b_ref[...].astype(MXU_DTYPE), dims,
                               preferred_element_type=F32)

        def finish(r):
            if c_ref is not None:
                r = r + c_ref[...].astype(F32)
            o_ref[...] = r.astype(out_dtype)

        if nk == 1:
            finish(part)
            return
        acc_ref = refs[-1]
        kk = pl.program_id(2)

        @pl.when(kk == 0)
        def _():
            acc_ref[...] = part

        @pl.when((kk > 0) & (kk < nk - 1))
        def _():
            acc_ref[...] += part

        @pl.when(kk == nk - 1)
        def _():
            finish(acc_ref[...] + part)

    a_spec = pl.BlockSpec((tk, tm), lambda i, j, kk: (kk, i)) if ta else pl.BlockSpec((tm, tk), lambda i, j, kk: (i, kk))
    b_spec = pl.BlockSpec((tn, tk), lambda i, j, kk: (j, kk)) if tb else pl.BlockSpec((tk, tn), lambda i, j, kk: (kk, j))
    in_specs = [a_spec, b_spec]
    args = [a, b]
    if add is not None:
        in_specs.append(pl.BlockSpec((tm, tn), lambda i, j, kk: (i, j)))
        args.append(add)
    return pl.pallas_call(
        body, name=name, grid=(m // tm, n // tn, nk),
        in_specs=in_specs, out_specs=pl.BlockSpec((tm, tn), lambda i, j, kk: (i, j)),
        out_shape=jax.ShapeDtypeStruct((m, n), out_dtype),
        scratch_shapes=[] if nk == 1 else [pltpu.VMEM((tm, tn), F32)],
        compiler_params=_params(("parallel", "parallel", "arbitrary")),
    )(*args)


def _rms_fwd(x, g, *, name):
    t, d = x.shape
    tr = _tile(t, 256, 8)

    def body(x_ref, g_ref, o_ref):
        xf = x_ref[...]
        r = lax.rsqrt(jnp.mean(xf * xf, axis=-1, keepdims=True) + EPS)
        o_ref[...] = (xf * r * g_ref[...]).astype(o_ref.dtype)

    return pl.pallas_call(
        body, name=name, grid=(t // tr,),
        in_specs=[pl.BlockSpec((tr, d), lambda i: (i, 0)), pl.BlockSpec((1, d), lambda i: (0, 0))],
        out_specs=pl.BlockSpec((tr, d), lambda i: (i, 0)),
        out_shape=jax.ShapeDtypeStruct((t, d), MXU_DTYPE),
        compiler_params=_params(("parallel",)),
    )(x, g)


def _rms_bwd(x, dhn, g, res, *, name, low_copy=False):
    t, d = x.shape
    tr = _tile(t, 256, 8)

    def body(*refs):
        if res is None:
            x_ref, dh_ref, g_ref = refs[:3]
            outs = refs[3:]
            res_ref = None
        else:
            x_ref, dh_ref, g_ref, res_ref = refs[:4]
            outs = refs[4:]
        dx_ref, dg_ref = outs[0], outs[-1]
        xf = x_ref[...]
        dh = dh_ref[...].astype(F32)
        r = lax.rsqrt(jnp.mean(xf * xf, axis=-1, keepdims=True) + EPS)
        dyg = dh * g_ref[...]
        dx = r * dyg - xf * (r * r * r) * jnp.mean(dyg * xf, axis=-1, keepdims=True)
        if res_ref is not None:
            dx = dx + res_ref[...]
        dx_ref[...] = dx
        if low_copy:
            outs[1][...] = dx.astype(outs[1].dtype)
        part = jnp.sum(dh * xf * r, axis=0, keepdims=True)

        @pl.when(pl.program_id(0) == 0)
        def _():
            dg_ref[...] = part

        @pl.when(pl.program_id(0) > 0)
        def _():
            dg_ref[...] += part

    row = pl.BlockSpec((tr, d), lambda i: (i, 0))
    vec = pl.BlockSpec((1, d), lambda i: (0, 0))
    in_specs = [row, row, vec] + ([] if res is None else [row])
    args = [x, dhn, g] + ([] if res is None else [res])
    out_specs = [row] + ([row] if low_copy else []) + [vec]
    out_shape = [jax.ShapeDtypeStruct((t, d), F32)] + ([jax.ShapeDtypeStruct((t, d), MXU_DTYPE)] if low_copy else []) \
        + [jax.ShapeDtypeStruct((1, d), F32)]
    return pl.pallas_call(
        body, name=name, grid=(t // tr,), in_specs=in_specs, out_specs=out_specs, out_shape=out_shape,
        compiler_params=_params(("arbitrary",)),
    )(*args)


def _rope_tables(t):
    pos = jnp.arange(t, dtype=F32)[:, None]
    inv_r = 1.0 / (ROPE_BASE ** (jnp.arange(0, RET_HEAD_DIM, 2, dtype=F32) / RET_HEAD_DIM))
    ang_r = pos * inv_r[None, :]
    inv_m = 1.0 / (ROPE_BASE ** (jnp.arange(0, MLA_ROPE, 2, dtype=F32) / MLA_ROPE))
    ang_m = pos * inv_m[None, :]
    cm, sm = jnp.cos(ang_m), jnp.sin(ang_m)
    z = jnp.zeros_like(cm)
    cos_m = jnp.concatenate([cm, cm, z, z], axis=1)
    sin_m = jnp.concatenate([-sm, sm, z, z], axis=1)
    return jnp.cos(ang_r), jnp.sin(ang_r), cos_m, sin_m


def _rope256(x, c, s, inverse=False):
    x1, x2 = x[:, :128], x[:, 128:]
    if inverse:
        s = -s
    return jnp.concatenate([x1 * c - x2 * s, x2 * c + x1 * s], axis=1)


def _rope64(x, cos_m, sin_m, inverse=False):
    lane = lax.broadcasted_iota(jnp.int32, x.shape, 1)
    partner = jnp.where(lane < 32, pltpu.roll(x, 96, 1), pltpu.roll(x, 32, 1))
    s = -sin_m if inverse else sin_m
    return x * cos_m + partner * s


def _ret_prep(proj, cos_r, sin_r):
    t = proj.shape[0]
    tr = _tile(t, 512, 8)
    hb = RET_WIDTH // RET_HEAD_DIM

    def body(q_ref, k_ref, v_ref, c_ref, s_ref, qo_ref, ko_ref, vo_ref):
        c, s = c_ref[...], s_ref[...]
        qo_ref[...] = _rope256(q_ref[...], c, s).astype(qo_ref.dtype)
        ko_ref[...] = (_rope256(k_ref[...], c, s) * (RET_HEAD_DIM ** -0.5)).astype(ko_ref.dtype)
        vo_ref[...] = v_ref[...].astype(vo_ref.dtype)

    head = lambda off: pl.BlockSpec((tr, RET_HEAD_DIM), lambda i, h: (i, h + off))
    tab = pl.BlockSpec((tr, 128), lambda i, h: (i, 0))
    out = jax.ShapeDtypeStruct((t, RET_WIDTH), MXU_DTYPE)
    return pl.pallas_call(
        body, name="ret_prep", grid=(t // tr, RET_HEADS),
        in_specs=[head(0), head(hb), head(2 * hb), tab, tab],
        out_specs=[head(0), head(0), head(0)], out_shape=[out, out, out],
        compiler_params=_params(("parallel", "parallel")),
    )(proj, proj, proj, cos_r, sin_r)


def _ret_log_gamma():
    return jnp.asarray(np.log1p(-np.exp2(-5.0 - np.arange(RET_HEADS, dtype=np.float64))), dtype=F32)


def _decay_full(lg, i, j, blk):
    r = lax.broadcasted_iota(jnp.int32, (blk, 1), 0).astype(F32)
    c = lax.broadcasted_iota(jnp.int32, (1, blk), 1).astype(F32)
    off = ((i - j) * blk).astype(F32)
    return jnp.exp(lg * r), jnp.exp(lg * (off - c))


def _decay_diag(lg, blk):
    r = lax.broadcasted_iota(jnp.int32, (blk, blk), 0)
    c = lax.broadcasted_iota(jnp.int32, (blk, blk), 1)
    ok = (c // CHUNK) <= (r // CHUNK)
    return jnp.where(ok, jnp.exp(lg * jnp.abs(r - c).astype(F32)), 0.0)


_NT = (((1,), (1,)), ((), ()))
_TN = (((0,), (0,)), ((), ()))
_NN = (((1,), (0,)), ((), ()))


def _causal_pairs(nb, query_major):
    if query_major:
        pairs = [(i, j) for i in range(nb) for j in range(i + 1)]
    else:
        pairs = [(i, j) for j in range(nb) for i in range(j, nb)]
    arr = np.asarray(pairs, dtype=np.int32)
    return jnp.asarray(arr[:, 0]), jnp.asarray(arr[:, 1])


def _ret_fwd(q, k, v, proj, lg):
    t = q.shape[0]
    blk = _tile(t, ATTN_BLOCK)
    nb = t // blk
    hps, d = HEADS_PER_STEP, RET_HEAD_DIM
    gate_off = 3 * RET_WIDTH // (hps * d)

    def body(ii_ref, jj_ref, lg_ref, q_ref, k_ref, v_ref, g_ref, raw_ref, ro_ref, acc_ref):
        hg, pair = pl.program_id(0), pl.program_id(1)
        i, j = ii_ref[pair], jj_ref[pair]

        @pl.when(j == 0)
        def _():
            acc_ref[...] = jnp.zeros_like(acc_ref)

        def step(diag):
            for h in range(hps):
                cols = slice(h * d, (h + 1) * d)
                lgh = lg_ref[hg * hps + h]
                s = lax.dot_general(q_ref[:, cols], k_ref[:, cols], _NT, preferred_element_type=F32)
                if diag:
                    w = s * _decay_diag(lgh, blk)
                else:
                    a, b = _decay_full(lgh, i, j, blk)
                    w = s * a * b
                acc_ref[h] += lax.dot_general(w.astype(MXU_DTYPE), v_ref[:, cols], _NN, preferred_element_type=F32)

        @pl.when(j < i)
        def _():
            step(False)

        @pl.when(j == i)
        def _():
            step(True)
            for h in range(hps):
                cols = slice(h * d, (h + 1) * d)
                o = acc_ref[h]
                raw_ref[:, cols] = o
                mu = jnp.mean(o, axis=-1, keepdims=True)
                var = jnp.mean(jnp.square(o - mu), axis=-1, keepdims=True)
                hn = (o - mu) * lax.rsqrt(var + EPS)
                g = g_ref[:, cols]
                ro_ref[:, cols] = (g * _sigmoid(g) * hn).astype(ro_ref.dtype)

    qs = pl.BlockSpec((blk, hps * d), lambda h, p, ii, jj: (ii[p], h))
    ks = pl.BlockSpec((blk, hps * d), lambda h, p, ii, jj: (jj[p], h))
    gs = pl.BlockSpec((blk, hps * d), lambda h, p, ii, jj: (ii[p], h + gate_off))
    ii, jj = _causal_pairs(nb, query_major=True)
    return pl.pallas_call(
        body, name="ret_fwd",
        grid_spec=pltpu.PrefetchScalarGridSpec(
            num_scalar_prefetch=2, grid=(RET_HEADS // hps, ii.shape[0]),
            in_specs=[pl.BlockSpec(memory_space=pltpu.SMEM), qs, ks, ks, gs], out_specs=[qs, qs],
            scratch_shapes=[pltpu.VMEM((hps, blk, d), F32)]),
        out_shape=[jax.ShapeDtypeStruct((t, RET_WIDTH), F32), jax.ShapeDtypeStruct((t, RET_WIDTH), MXU_DTYPE)],
        compiler_params=_params(("parallel", "arbitrary")),
    )(ii, jj, lg, q, k, v, proj)


def _ret_gate_bwd(raw, proj, dattn):
    t = raw.shape[0]
    tr = _tile(t, 512, 8)
    gate_off = 3 * RET_WIDTH // RET_HEAD_DIM

    def body(o_ref, g_ref, d_ref, do_ref, dg_ref):
        o, g, d = o_ref[...], g_ref[...], d_ref[...]
        mu = jnp.mean(o, axis=-1, keepdims=True)
        rstd = lax.rsqrt(jnp.mean(jnp.square(o - mu), axis=-1, keepdims=True) + EPS)
        hn = (o - mu) * rstd
        sg = _sigmoid(g)
        dg_ref[...] = (d * hn * (sg * (1.0 + g * (1.0 - sg)))).astype(dg_ref.dtype)
        dhn = d * (g * sg)
        do = rstd * (dhn - jnp.mean(dhn, axis=-1, keepdims=True) - hn * jnp.mean(dhn * hn, axis=-1, keepdims=True))
        do_ref[...] = do.astype(do_ref.dtype)

    hs = pl.BlockSpec((tr, RET_HEAD_DIM), lambda i, h: (i, h))
    gs = pl.BlockSpec((tr, RET_HEAD_DIM), lambda i, h: (i, h + gate_off))
    out = jax.ShapeDtypeStruct((t, RET_WIDTH), MXU_DTYPE)
    return pl.pallas_call(
        body, name="ret_gate_bwd", grid=(t // tr, RET_HEADS),
        in_specs=[hs, gs, hs], out_specs=[hs, hs], out_shape=[out, out],
        compiler_params=_params(("parallel", "parallel")),
    )(raw, proj, dattn)


def _ret_bwd(q, k, v, do, lg):
    t = q.shape[0]
    blk = _tile(t, ATTN_BLOCK)
    nb = t // blk

    hps, d = HEADS_PER_STEP, RET_HEAD_DIM

    def body(ii_ref, jj_ref, lg_ref, q_ref, k_ref, v_ref, do_ref, dq_ref, dk_ref, dv_ref, dk_acc, dv_acc):
        hg, pair = pl.program_id(0), pl.program_id(1)
        i, j = ii_ref[pair], jj_ref[pair]

        @pl.when(pair == 0)
        def _():
            dq_ref[...] = jnp.zeros_like(dq_ref)

        @pl.when(i == j)
        def _():
            dk_acc[...] = jnp.zeros_like(dk_acc)
            dv_acc[...] = jnp.zeros_like(dv_acc)

        def step(diag):
            rows = pl.ds(pl.multiple_of(i * blk, blk), blk)
            for h in range(hps):
                cols = slice(h * d, (h + 1) * d)
                lgh = lg_ref[hg * hps + h]
                if diag:
                    decay = _decay_diag(lgh, blk)
                else:
                    a, b = _decay_full(lgh, i, j, blk)
                    decay = a * b
                qb, kb, vb, dob = q_ref[:, cols], k_ref[:, cols], v_ref[:, cols], do_ref[:, cols]
                s = lax.dot_general(qb, kb, _NT, preferred_element_type=F32)
                w = (s * decay).astype(MXU_DTYPE)
                dv_acc[h] += lax.dot_general(w, dob, _TN, preferred_element_type=F32)
                dw = lax.dot_general(dob, vb, _NT, preferred_element_type=F32)
                ds = (dw * decay).astype(MXU_DTYPE)
                dq_ref[rows, cols] += lax.dot_general(ds, kb, _NN, preferred_element_type=F32)
                dk_acc[h] += lax.dot_general(ds, qb, _TN, preferred_element_type=F32)

        @pl.when(i > j)
        def _():
            step(False)

        @pl.when(i == j)
        def _():
            step(True)

        @pl.when(i == nb - 1)
        def _():
            for h in range(hps):
                cols = slice(h * d, (h + 1) * d)
                dk_ref[:, cols] = dk_acc[h]
                dv_ref[:, cols] = dv_acc[h].astype(dv_ref.dtype)

    qs = pl.BlockSpec((blk, hps * d), lambda h, p, ii, jj: (ii[p], h))
    ks = pl.BlockSpec((blk, hps * d), lambda h, p, ii, jj: (jj[p], h))
    ii, jj = _causal_pairs(nb, query_major=False)
    return pl.pallas_call(
        body, name="ret_bwd",
        grid_spec=pltpu.PrefetchScalarGridSpec(
            num_scalar_prefetch=2, grid=(RET_HEADS // hps, ii.shape[0]),
            in_specs=[pl.BlockSpec(memory_space=pltpu.SMEM), qs, ks, ks, qs],
            out_specs=[pl.BlockSpec((t, hps * d), lambda h, p, ii, jj: (0, h)), ks, ks],
            scratch_shapes=[pltpu.VMEM((hps, blk, d), F32), pltpu.VMEM((hps, blk, d), F32)]),
        out_shape=[jax.ShapeDtypeStruct((t, RET_WIDTH), F32), jax.ShapeDtypeStruct((t, RET_WIDTH), F32),
                   jax.ShapeDtypeStruct((t, RET_WIDTH), MXU_DTYPE)],
        compiler_params=_params(("parallel", "arbitrary")),
    )(ii, jj, lg, q, k, v, do)


def _ret_unrope(dq, dk, cos_r, sin_r):
    t = dq.shape[0]
    tr = _tile(t, 512, 8)

    def body(dq_ref, dk_ref, c_ref, s_ref, oq_ref, ok_ref):
        c, s = c_ref[...], s_ref[...]
        oq_ref[...] = _rope256(dq_ref[...], c, s, inverse=True).astype(oq_ref.dtype)
        ok_ref[...] = (_rope256(dk_ref[...], c, s, inverse=True) * (RET_HEAD_DIM ** -0.5)).astype(ok_ref.dtype)

    hs = pl.BlockSpec((tr, RET_HEAD_DIM), lambda i, h: (i, h))
    tab = pl.BlockSpec((tr, 128), lambda i, h: (i, 0))
    out = jax.ShapeDtypeStruct((t, RET_WIDTH), MXU_DTYPE)
    return pl.pallas_call(
        body, name="ret_unrope", grid=(t // tr, RET_HEADS),
        in_specs=[hs, hs, tab, tab], out_specs=[hs, hs], out_shape=[out, out],
        compiler_params=_params(("parallel", "parallel")),
    )(dq, dk, cos_r, sin_r)


def _mla_prep(cq, ckv, kr, g_q, g_kv, cos_m, sin_m):
    t = cq.shape[0]
    tr = _tile(t, 512, 8)

    def body(cq_ref, ckv_ref, kr_ref, gq_ref, gkv_ref, c_ref, s_ref, cqn_ref, kvn_ref, kro_ref):
        for x_ref, g_ref, o_ref in ((cq_ref, gq_ref, cqn_ref), (ckv_ref, gkv_ref, kvn_ref)):
            xf = x_ref[...]
            r = lax.rsqrt(jnp.mean(xf * xf, axis=-1, keepdims=True) + EPS)
            o_ref[...] = (xf * r * g_ref[...]).astype(o_ref.dtype)
        kro_ref[...] = _rope64(kr_ref[...], c_ref[...], s_ref[...]).astype(kro_ref.dtype)

    row = lambda w: pl.BlockSpec((tr, w), lambda i: (i, 0))
    vec = lambda w: pl.BlockSpec((1, w), lambda i: (0, 0))
    return pl.pallas_call(
        body, name="mla_prep", grid=(t // tr,),
        in_specs=[row(Q_LORA), row(KV_LORA), row(128), vec(Q_LORA), vec(KV_LORA), row(128), row(128)],
        out_specs=[row(Q_LORA), row(KV_LORA), row(128)],
        out_shape=[jax.ShapeDtypeStruct((t, Q_LORA), MXU_DTYPE), jax.ShapeDtypeStruct((t, KV_LORA), MXU_DTYPE),
                   jax.ShapeDtypeStruct((t, 128), MXU_DTYPE)],
        compiler_params=_params(("parallel",)),
    )(cq, ckv, kr, g_q, g_kv, cos_m, sin_m)


def _mla_q_rope(q_lin, cos_m, sin_m, *, inverse, name):
    t = q_lin.shape[0]
    tr = _tile(t, 256, 8)

    def body(q_ref, c_ref, s_ref, o_ref):
        c, s = c_ref[...], s_ref[...]
        for h in range(MLA_HEADS):
            lo = h * MLA_QK_PAD
            o_ref[:, lo:lo + MLA_NOPE] = q_ref[:, lo:lo + MLA_NOPE].astype(o_ref.dtype)
            roped = _rope64(q_ref[:, lo + MLA_NOPE:lo + MLA_QK_PAD].astype(F32), c, s, inverse=inverse)
            o_ref[:, lo + MLA_NOPE:lo + MLA_QK_PAD] = roped.astype(o_ref.dtype)

    rows = pl.BlockSpec((tr, Q_WIDTH_PAD), lambda i: (i, 0))
    tab = pl.BlockSpec((tr, 128), lambda i: (i, 0))
    return pl.pallas_call(
        body, name=name, grid=(t // tr,),
        in_specs=[rows, tab, tab], out_specs=rows, out_shape=jax.ShapeDtypeStruct((t, Q_WIDTH_PAD), MXU_DTYPE),
        compiler_params=_params(("parallel",)),
    )(q_lin, cos_m, sin_m)


_MLA_SCALE = (MLA_NOPE + MLA_ROPE) ** -0.5
_LOG2_E = 1.4426950408889634
_MLA_SCALE_LOG2 = _MLA_SCALE * _LOG2_E
_NEG = -1e30


def _mla_mask(blk):
    r = lax.broadcasted_iota(jnp.int32, (blk, blk), 0)
    c = lax.broadcasted_iota(jnp.int32, (blk, blk), 1)
    return (c // CHUNK) <= (r // CHUNK)


def _mla_fwd(q, kv, kr):
    t = q.shape[0]
    blk = _tile(t, ATTN_BLOCK)
    nb = t // blk

    hps, dq_, dv_ = HEADS_PER_STEP, MLA_QK_PAD, MLA_V

    def body(ii_ref, jj_ref, q_ref, kv_ref, kr_ref, o_ref, lse_ref, m_ref, acc_ref):
        pair = pl.program_id(1)
        i, j = ii_ref[pair], jj_ref[pair]

        @pl.when(j == 0)
        def _():
            m_ref[...] = jnp.full_like(m_ref, _NEG)
            acc_ref[...] = jnp.zeros_like(acc_ref)

        def step(masked):
            krb = kr_ref[...]
            ones = jnp.ones((blk, dv_), MXU_DTYPE)
            for h in range(hps):
                kb = jnp.concatenate([kv_ref[:, h * dq_:h * dq_ + MLA_NOPE], krb], axis=1)
                vb = jnp.concatenate([kv_ref[:, h * dq_ + MLA_NOPE:(h + 1) * dq_], ones], axis=1)
                s = lax.dot_general(q_ref[:, h * dq_:(h + 1) * dq_], kb, _NT, preferred_element_type=F32)
                if masked:
                    s = jnp.where(_mla_mask(blk), s, _NEG)
                m_prev = m_ref[h]
                m_new = jnp.maximum(m_prev, jnp.max(s, axis=-1, keepdims=True))
                alpha = jnp.exp2((m_prev - m_new) * _MLA_SCALE_LOG2)
                p = jnp.exp2((s - jnp.tile(m_new, (1, blk // 128))) * _MLA_SCALE_LOG2)
                acc_ref[h] = jnp.tile(alpha, (1, 2)) * acc_ref[h] + lax.dot_general(
                    p.astype(MXU_DTYPE), vb, _NN, preferred_element_type=F32)
                m_ref[h] = m_new

        @pl.when(j < i)
        def _():
            step(False)

        @pl.when(j == i)
        def _():
            step(True)
            for h in range(hps):
                cols = slice(h * dv_, (h + 1) * dv_)
                acc = acc_ref[h]
                row_sum = acc[:, dv_:]
                o_ref[:, cols] = (acc[:, :dv_] / row_sum).astype(o_ref.dtype)
                lse_ref[:, cols] = m_ref[h] * _MLA_SCALE + jnp.log(row_sum)

    os_ = pl.BlockSpec((blk, hps * dv_), lambda h, p, ii, jj: (ii[p], h))
    ii, jj = _causal_pairs(nb, query_major=True)
    return pl.pallas_call(
        body, name="mla_fwd",
        grid_spec=pltpu.PrefetchScalarGridSpec(
            num_scalar_prefetch=2, grid=(MLA_HEADS // hps, ii.shape[0]),
            in_specs=[pl.BlockSpec((blk, hps * dq_), lambda h, p, ii, jj: (ii[p], h)),
                      pl.BlockSpec((blk, hps * dq_), lambda h, p, ii, jj: (jj[p], h)),
                      pl.BlockSpec((blk, 128), lambda h, p, ii, jj: (jj[p], 0))],
            out_specs=[os_, os_],
            scratch_shapes=[pltpu.VMEM((hps, blk, 128), F32), pltpu.VMEM((hps, blk, 2 * dv_), F32)]),
        out_shape=[jax.ShapeDtypeStruct((t, MLA_HEADS * MLA_V), MXU_DTYPE),
                   jax.ShapeDtypeStruct((t, MLA_HEADS * 128), F32)],
        compiler_params=_params(("parallel", "arbitrary")),
    )(ii, jj, q, kv, kr)


def _mla_bwd(q, kv, kr, o, lse, dattn):
    t = q.shape[0]
    blk = _tile(t, ATTN_BLOCK)
    nb = t // blk
    hps, dq_, dv_ = HEADS_PER_STEP, MLA_QK_PAD, MLA_V
    do_off = RET_WIDTH // (hps * dv_)

    def body(ii_ref, jj_ref, q_ref, kv_ref, kr_ref, o_ref, lse_ref, do_ref, dq_ref, dk_ref, dv_ref, dk_acc, dv_acc):
        pair = pl.program_id(1)
        i, j = ii_ref[pair], jj_ref[pair]

        @pl.when(pair == 0)
        def _():
            dq_ref[...] = jnp.zeros_like(dq_ref)

        @pl.when(i == j)
        def _():
            dk_acc[...] = jnp.zeros_like(dk_acc)
            dv_acc[...] = jnp.zeros_like(dv_acc)

        def step(masked):
            krb = kr_ref[...]
            rows = pl.ds(pl.multiple_of(i * blk, blk), blk)
            for h in range(hps):
                qcols, vcols = slice(h * dq_, (h + 1) * dq_), slice(h * dv_, (h + 1) * dv_)
                qb = q_ref[:, qcols]
                kb = jnp.concatenate([kv_ref[:, h * dq_:h * dq_ + MLA_NOPE], krb], axis=1)
                vb = kv_ref[:, h * dq_ + MLA_NOPE:(h + 1) * dq_]
                dof = do_ref[:, vcols]
                dob = dof.astype(MXU_DTYPE)
                s = lax.dot_general(qb, kb, _NT, preferred_element_type=F32)
                if masked:
                    s = jnp.where(_mla_mask(blk), s, _NEG)
                lse2 = lse_ref[:, vcols] * _LOG2_E
                p = jnp.exp2(s * _MLA_SCALE_LOG2 - jnp.tile(lse2, (1, blk // 128)))
                delta = jnp.sum(dof * o_ref[:, vcols].astype(F32), axis=-1, keepdims=True)
                dv_acc[h] += lax.dot_general(p.astype(MXU_DTYPE), dob, _TN, preferred_element_type=F32)
                dp = lax.dot_general(dob, vb, _NT, preferred_element_type=F32)
                ds = (p * (dp - delta) * _MLA_SCALE).astype(MXU_DTYPE)
                dq_ref[rows, qcols] += lax.dot_general(ds, kb, _NN, preferred_element_type=F32)
                dk_acc[h] += lax.dot_general(ds, qb, _TN, preferred_element_type=F32)

        @pl.when(i > j)
        def _():
            step(False)

        @pl.when(i == j)
        def _():
            step(True)

        @pl.when(i == nb - 1)
        def _():
            for h in range(hps):
                dk_ref[:, h * dq_:(h + 1) * dq_] = dk_acc[h]
                dv_ref[:, h * dv_:(h + 1) * dv_] = dv_acc[h].astype(dv_ref.dtype)

    qmap = lambda off: (lambda h, p, ii, jj: (ii[p], h + off))
    kmap = lambda h, p, ii, jj: (jj[p], h)
    ii, jj = _causal_pairs(nb, query_major=False)
    return pl.pallas_call(
        body, name="mla_bwd",
        grid_spec=pltpu.PrefetchScalarGridSpec(
            num_scalar_prefetch=2, grid=(MLA_HEADS // hps, ii.shape[0]),
            in_specs=[pl.BlockSpec((blk, hps * dq_), qmap(0)), pl.BlockSpec((blk, hps * dq_), kmap),
                      pl.BlockSpec((blk, 128), lambda h, p, ii, jj: (jj[p], 0)),
                      pl.BlockSpec((blk, hps * dv_), qmap(0)), pl.BlockSpec((blk, hps * dv_), qmap(0)),
                      pl.BlockSpec((blk, hps * dv_), qmap(do_off))],
            out_specs=[pl.BlockSpec((t, hps * dq_), lambda h, p, ii, jj: (0, h)),
                       pl.BlockSpec((blk, hps * dq_), kmap), pl.BlockSpec((blk, hps * dv_), kmap)],
            scratch_shapes=[pltpu.VMEM((hps, blk, dq_), F32), pltpu.VMEM((hps, blk, dv_), F32)]),
        out_shape=[jax.ShapeDtypeStruct((t, Q_WIDTH_PAD), F32), jax.ShapeDtypeStruct((t, Q_WIDTH_PAD), F32),
                   jax.ShapeDtypeStruct((t, MLA_HEADS * MLA_V), MXU_DTYPE)],
        compiler_params=_params(("parallel", "arbitrary")),
    )(ii, jj, q, kv, kr, o, lse, dattn)


def _mla_kv_grad(dk, dv, cos_m, sin_m):
    t = dk.shape[0]
    tr = _tile(t, 256, 8)

    def body(dk_ref, dv_ref, c_ref, s_ref, dkv_ref, dkr_ref):
        acc = jnp.zeros((tr, 128), F32)
        for h in range(MLA_HEADS):
            dkv_ref[:, h * 256:h * 256 + 128] = dk_ref[:, h * 256:h * 256 + 128].astype(dkv_ref.dtype)
            dkv_ref[:, h * 256 + 128:h * 256 + 256] = dv_ref[:, h * 128:(h + 1) * 128].astype(dkv_ref.dtype)
            acc = acc + dk_ref[:, h * 256 + 128:h * 256 + 256]
        dkr_ref[...] = _rope64(acc, c_ref[...], s_ref[...], inverse=True).astype(dkr_ref.dtype)

    row = lambda w: pl.BlockSpec((tr, w), lambda i: (i, 0))
    return pl.pallas_call(
        body, name="mla_kv_grad", grid=(t // tr,),
        in_specs=[row(Q_WIDTH_PAD), row(MLA_HEADS * MLA_V), row(128), row(128)],
        out_specs=[row(Q_WIDTH_PAD), row(128)],
        out_shape=[jax.ShapeDtypeStruct((t, Q_WIDTH_PAD), MXU_DTYPE), jax.ShapeDtypeStruct((t, 128), MXU_DTYPE)],
        compiler_params=_params(("parallel",)),
    )(dk, dv, cos_m, sin_m)


_FFN_COLS = 256
_FFN_ROWS = 256


def _shift_down(cur, prev8, n):
    row = lax.broadcasted_iota(jnp.int32, cur.shape, 0)
    out = pltpu.roll(cur, n, 0)
    for r in range(n):
        out = jnp.where(row == r, prev8[8 - n + r:8 - n + r + 1, :], out)
    return out


def _shift_up(cur, next8, n):
    rows = cur.shape[0]
    row = lax.broadcasted_iota(jnp.int32, cur.shape, 0)
    out = pltpu.roll(cur, rows - n, 0)
    for r in range(n):
        out = jnp.where(row == rows - n + r, next8[r:r + 1, :], out)
    return out


def _conv_pre(g_ref, cw_ref, cb_ref, c, rc):
    r0 = pl.multiple_of(c * rc, rc)
    cur = g_ref[pl.ds(r0, rc), :]
    prev8 = g_ref[pl.ds(pl.multiple_of(jnp.maximum(r0 - 8, 0), 8), 8), :]
    prev8 = jnp.where(c > 0, prev8, 0.0)
    s1, s2 = _shift_down(cur, prev8, 1), _shift_down(cur, prev8, 2)
    a = cb_ref[...] + cw_ref[2:3, :] * cur + cw_ref[1:2, :] * s1 + cw_ref[0:1, :] * s2
    return r0, cur, s1, s2, a


def _ffn_act_fwd(gpre, u, cw, cb):
    t, f = gpre.shape
    tc = _tile(f, _FFN_COLS)
    rc = _tile(t, _FFN_ROWS, 8)

    def body(g_ref, u_ref, cw_ref, cb_ref, o_ref):
        def chunk(c, carry):
            r0, _, _, _, a = _conv_pre(g_ref, cw_ref, cb_ref, c, rc)
            o_ref[pl.ds(r0, rc), :] = (a * _sigmoid(a) * u_ref[pl.ds(r0, rc), :]).astype(o_ref.dtype)
            return carry
        lax.fori_loop(0, t // rc, chunk, 0)

    col = pl.BlockSpec((t, tc), lambda j: (0, j))
    return pl.pallas_call(
        body, name="ffn_act_fwd", grid=(f // tc,),
        in_specs=[col, col, pl.BlockSpec((CONV_WIDTH, tc), lambda j: (0, j)), pl.BlockSpec((1, tc), lambda j: (0, j))],
        out_specs=col, out_shape=jax.ShapeDtypeStruct((t, f), MXU_DTYPE),
        compiler_params=_params(("parallel",)),
    )(gpre, u, cw, cb)


def _ffn_act_bwd(gpre, u, dact, cw, cb):
    t, f = gpre.shape
    tc = _tile(f, _FFN_COLS)
    rc = _tile(t, _FFN_ROWS, 8)
    nc = t // rc

    def body(g_ref, u_ref, d_ref, cw_ref, cb_ref, dg_ref, du_ref, dcw_ref, dcb_ref, da_ref):
        def chunk(c, carry):
            w0, w1, w2, b = carry
            r0, cur, s1, s2, a = _conv_pre(g_ref, cw_ref, cb_ref, c, rc)
            sg = _sigmoid(a)
            d = d_ref[pl.ds(r0, rc), :]
            du_ref[pl.ds(r0, rc), :] = (d * (a * sg)).astype(du_ref.dtype)
            da = d * u_ref[pl.ds(r0, rc), :] * (sg * (1.0 + a * (1.0 - sg)))
            da_ref[pl.ds(r0, rc), :] = da
            return (w0 + jnp.sum(da * s2, axis=0, keepdims=True), w1 + jnp.sum(da * s1, axis=0, keepdims=True),
                    w2 + jnp.sum(da * cur, axis=0, keepdims=True), b + jnp.sum(da, axis=0, keepdims=True))
        z = jnp.zeros((1, tc), F32)
        w0, w1, w2, b = lax.fori_loop(0, nc, chunk, (z, z, z, z))
        dcw_ref[0:1, :] = w0
        dcw_ref[1:2, :] = w1
        dcw_ref[2:3, :] = w2
        dcb_ref[...] = b

        def chunk2(c, carry):
            r0 = pl.multiple_of(c * rc, rc)
            cur = da_ref[pl.ds(r0, rc), :]
            nxt = da_ref[pl.ds(pl.multiple_of(jnp.minimum(r0 + rc, t - 8), 8), 8), :]
            nxt = jnp.where(c < nc - 1, nxt, 0.0)
            dg = cw_ref[2:3, :] * cur + cw_ref[1:2, :] * _shift_up(cur, nxt, 1) + cw_ref[0:1, :] * _shift_up(cur, nxt, 2)
            dg_ref[pl.ds(r0, rc), :] = dg.astype(dg_ref.dtype)
            return carry
        lax.fori_loop(0, nc, chunk2, 0)

    col = pl.BlockSpec((t, tc), lambda j: (0, j))
    w3 = pl.BlockSpec((CONV_WIDTH, tc), lambda j: (0, j))
    w1 = pl.BlockSpec((1, tc), lambda j: (0, j))
    low = jax.ShapeDtypeStruct((t, f), MXU_DTYPE)
    return pl.pallas_call(
        body, name="ffn_act_bwd", grid=(f // tc,),
        in_specs=[col, col, col, w3, w1], out_specs=[col, col, w3, w1],
        out_shape=[low, low, jax.ShapeDtypeStruct((CONV_WIDTH, f), F32), jax.ShapeDtypeStruct((1, f), F32)],
        scratch_shapes=[pltpu.VMEM((t, tc), F32)],
        compiler_params=_params(("parallel",)),
    )(gpre, u, dact, cw, cb)


def _head_fwd_bwd(h2, glin, pp, target, g_final):
    t, d = h2.shape
    tr = _tile(t, 128, 8)

    def body(h_ref, gl_ref, pp_ref, t_ref, g_ref, loss_ref, dh_ref, dgl_ref, dpp_ref, dg_ref):
        gate = _sigmoid(gl_ref[...])
        ppv = pp_ref[...]
        h3 = h_ref[...] + gate * ppv
        r = lax.rsqrt(jnp.mean(h3 * h3, axis=-1, keepdims=True) + EPS)
        yh = h3 * r
        g = g_ref[...]
        diff = yh * g - t_ref[...]
        lpart = 0.5 * jnp.sum(jnp.mean(diff * diff, axis=-1, keepdims=True), axis=0, keepdims=True)
        dy = diff * (1.0 / d)
        dyg = dy * g
        dh3 = r * dyg - h3 * (r * r * r) * jnp.mean(dyg * h3, axis=-1, keepdims=True)
        dh_ref[...] = dh3
        dgl_ref[...] = (dh3 * ppv * gate * (1.0 - gate)).astype(dgl_ref.dtype)
        dpp_ref[...] = (dh3 * gate).astype(dpp_ref.dtype)
        dgp = jnp.sum(dy * yh, axis=0, keepdims=True)

        @pl.when(pl.program_id(0) == 0)
        def _():
            loss_ref[...] = jnp.broadcast_to(lpart, loss_ref.shape)
            dg_ref[...] = dgp

        @pl.when(pl.program_id(0) > 0)
        def _():
            loss_ref[...] += jnp.broadcast_to(lpart, loss_ref.shape)
            dg_ref[...] += dgp

    row = pl.BlockSpec((tr, d), lambda i: (i, 0))
    vec = pl.BlockSpec((1, d), lambda i: (0, 0))
    low = jax.ShapeDtypeStruct((t, d), MXU_DTYPE)
    return pl.pallas_call(
        body, name="head_fwd_bwd", grid=(t // tr,),
        in_specs=[row, row, row, row, vec],
        out_specs=[pl.BlockSpec((8, 128), lambda i: (0, 0)), row, row, row, vec],
        out_shape=[jax.ShapeDtypeStruct((8, 128), F32), jax.ShapeDtypeStruct((t, d), F32), low, low,
                   jax.ShapeDtypeStruct((1, d), F32)],
        compiler_params=_params(("arbitrary",)),
    )(h2, glin, pp, target, g_final)


class _Order:
    def __init__(self):
        self.last = None

    def tie(self, x):
        return x if self.last is None else lax.optimization_barrier((x, self.last))[0]

    def run(self, fn, first, *args, **kwargs):
        out = fn(self.tie(first), *args, **kwargs)
        self.last = out[0] if isinstance(out, (tuple, list)) else out
        return out


def _local_step(x, p, target, vec, ops):
    t = x.shape[0]
    cos_r, sin_r, cos_m, sin_m = _rope_tables(t)
    lg = _ret_log_gamma()
    low = MXU_DTYPE
    run = ops.order.run
    w = ops.weight

    ops.start_gather("w_in")
    hn1 = run(_rms_fwd, x, vec["g_attn"], name="rms1_fwd")
    for n in ("w_uq", "w_ukv", "w_o"):
        ops.start_gather(n, after=hn1)
    proj = run(_matmul, hn1, w("w_in"), tb=True, name="mm_proj")
    ops.start_gather("w_ffn_gate", after=proj)
    rq, rk, rv = run(_ret_prep, proj, cos_r, sin_r)
    ops.start_gather("w_ffn_up", after=rq)
    c0 = 4 * RET_WIDTH
    cq = proj[:, c0:c0 + Q_LORA]
    ckv = proj[:, c0 + Q_LORA:c0 + Q_LORA + KV_LORA]
    kr_in = proj[:, c0 + Q_LORA + KV_LORA:c0 + Q_LORA + KV_LORA + 128]
    cqn, kvn, kr = run(_mla_prep, cq, ckv, kr_in, vec["g_q_lora"], vec["g_kv_lora"], cos_m, sin_m)
    ops.start_gather("w_ffn_down", after=cqn)
    q_lin = run(_matmul, cqn, w("w_uq"), tb=True, name="mm_q")
    q = run(_mla_q_rope, q_lin, cos_m, sin_m, inverse=False, name="mla_q_rope")
    kv = run(_matmul, kvn, w("w_ukv"), tb=True, name="mm_kv", out_dtype=low)
    mo, lse = run(_mla_fwd, q, kv, kr)
    ops.start_gather("w_ple_gate", after=mo)
    ops.start_gather("w_ple_proj", after=mo)
    ret_raw, ro = run(_ret_fwd, rq, rk, rv, proj, lg)
    attn = jnp.concatenate([ro, mo], axis=1)
    h1 = run(_matmul, attn, w("w_o"), name="mm_o", add=x)
    hn2 = run(_rms_fwd, h1, vec["g_ffn"], name="rms2_fwd")
    gpre = run(_matmul, hn2, w("w_ffn_gate"), tb=True, name="mm_gate")
    u = run(_matmul, hn2, w("w_ffn_up"), tb=True, name="mm_up")
    act = run(_ffn_act_fwd, gpre, u, vec["conv_w"], vec["conv_b"])
    h2 = run(_matmul, act, w("w_ffn_down"), name="mm_down", add=h1)
    hn3 = run(_rms_fwd, h2, vec["g_ple"], name="rms3_fwd")
    glin = run(_matmul, hn3, w("w_ple_gate"), name="mm_ple_gate")
    p_low = p.astype(low)
    pp = run(_matmul, p_low, w("w_ple_proj"), tb=True, name="mm_ple_proj")
    loss_part, dh3, dglin, dpp, dg_final = run(_head_fwd_bwd, h2, glin, pp, target, vec["g_final"])

    ops.grad("w_ple_proj", run(_matmul, dpp, p_low, ta=True, name="mm_d_ple_proj", out_dtype=low))
    ops.grad("w_ple_gate", run(_matmul, hn3, dglin, ta=True, name="mm_d_ple_gate", out_dtype=low))
    dhn3 = run(_matmul, dglin, w("w_ple_gate"), tb=True, name="mm_dhn3", out_dtype=low)
    dh2, dh2_low, dg_ple = run(_rms_bwd, h2, dhn3, vec["g_ple"], dh3, name="rms3_bwd", low_copy=True)
    ops.reduce_add("w_ple_proj")
    ops.reduce_add("w_ple_gate")
    ops.grad("w_ffn_down", run(_matmul, act, dh2_low, ta=True, name="mm_d_down", out_dtype=low))
    dact = run(_matmul, dh2_low, w("w_ffn_down"), tb=True, name="mm_dact", out_dtype=low)
    ops.reduce_add("w_ffn_down")
    dgpre, du, dconv_w, dconv_b = run(_ffn_act_bwd, gpre, u, dact, vec["conv_w"], vec["conv_b"])
    ops.update("w_ple_proj")
    ops.update("w_ple_gate")
    ops.grad("w_ffn_gate", run(_matmul, dgpre, hn2, ta=True, name="mm_d_gate", out_dtype=low))
    ops.grad("w_ffn_up", run(_matmul, du, hn2, ta=True, name="mm_d_up", out_dtype=low))
    ops.reduce_add("w_ffn_gate")
    dhn2 = run(_matmul, dgpre, w("w_ffn_gate"), name="mm_dhn2_a")
    ops.reduce_add("w_ffn_up")
    dhn2 = run(_matmul, du, w("w_ffn_up"), name="mm_dhn2_b", add=dhn2, out_dtype=low)
    dh1, dh1_low, dg_ffn = run(_rms_bwd, h1, dhn2, vec["g_ffn"], dh2, name="rms2_bwd", low_copy=True)
    ops.update("w_ffn_down")
    ops.grad("w_o", run(_matmul, attn, dh1_low, ta=True, name="mm_d_o", out_dtype=low))
    dattn = run(_matmul, dh1_low, w("w_o"), tb=True, name="mm_dattn")
    ops.reduce_add("w_o")

    dq_r, dk_full, dv = run(_mla_bwd, q, kv, kr, mo, lse, dattn)
    ops.update("w_ffn_gate")
    dq_lin = run(_mla_q_rope, dq_r, cos_m, sin_m, inverse=True, name="mla_q_unrope")
    dkv, dkr = run(_mla_kv_grad, dk_full, dv, cos_m, sin_m)
    ops.grad("w_uq", run(_matmul, dq_lin, cqn, ta=True, name="mm_d_uq", out_dtype=low))
    dcqn = run(_matmul, dq_lin, w("w_uq"), name="mm_dcqn")
    ops.grad("w_ukv", run(_matmul, dkv, kvn, ta=True, name="mm_d_ukv", out_dtype=low))
    dkvn = run(_matmul, dkv, w("w_ukv"), name="mm_dkvn")
    dcq, dcq_low, dg_q = run(_rms_bwd, cq, dcqn, vec["g_q_lora"], None, name="rmsq_bwd", low_copy=True)
    dckv, dckv_low, dg_kv = run(_rms_bwd, ckv, dkvn, vec["g_kv_lora"], None, name="rmskv_bwd", low_copy=True)
    ops.reduce_add("w_uq")
    ops.reduce_add("w_ukv")

    do_ret, drg = run(_ret_gate_bwd, ret_raw, proj, dattn)
    dq_ret, dk_ret, drv = run(_ret_bwd, rq, rk, rv, do_ret, lg)
    drq, drk = run(_ret_unrope, dq_ret, dk_ret, cos_r, sin_r)

    pad = jnp.zeros((t, IN_WIDTH_PAD - IN_WIDTH - 64), low)
    dproj = jnp.concatenate([drq, drk, drv, drg, dcq_low, dckv_low, dkr, pad], axis=1)
    ops.grad("w_in", run(_matmul, dproj, hn1, ta=True, name="mm_d_in", out_dtype=low))
    for n in ("w_ffn_up", "w_o", "w_uq", "w_ukv"):
        ops.update(n)
    ops.reduce_add("w_in")
    dhn1 = run(_matmul, dproj, w("w_in"), name="mm_dhn1", out_dtype=low)
    grad_x, dg_attn = run(_rms_bwd, x, dhn1, vec["g_attn"], dh1, name="rms1_bwd")

    gs = {"g_attn": dg_attn, "g_q_lora": dg_q, "g_kv_lora": dg_kv, "g_ffn": dg_ffn, "conv_w": dconv_w,
          "conv_b": dconv_b, "g_ple": dg_ple, "g_final": dg_final}
    return loss_part, grad_x, gs


_COL_SHARDED = ("w_in", "w_uq", "w_ukv", "w_ffn_gate", "w_ffn_up", "w_ple_proj")
_FFN_SHARD = D_FF // N_DEV
_FFN_SHARD_PAD = D_FF_PAD // N_DEV
_HEADS_PER_SHARD = MLA_HEADS // N_DEV
_QK = MLA_NOPE + MLA_ROPE


def _pad_rows(name, a):
    lead = a.shape[:-2]
    if name == "w_uq":
        a = a.reshape(lead + (_HEADS_PER_SHARD, _QK, a.shape[-1]))
        a = jnp.pad(a, [(0, 0)] * len(lead) + [(0, 0), (0, MLA_QK_PAD - _QK), (0, 0)])
        return a.reshape(lead + (_HEADS_PER_SHARD * MLA_QK_PAD, a.shape[-1]))
    if name in ("w_ffn_gate", "w_ffn_up", "w_ffn_down"):
        return jnp.pad(a, [(0, 0)] * len(lead) + [(0, _FFN_SHARD_PAD - _FFN_SHARD), (0, 0)])
    return a


def _unpad_rows(name, a):
    lead = a.shape[:-2]
    if name == "w_uq":
        a = a.reshape(lead + (_HEADS_PER_SHARD, MLA_QK_PAD, a.shape[-1]))[..., :_QK, :]
        return a.reshape(lead + (_HEADS_PER_SHARD * _QK, a.shape[-1]))
    if name in ("w_ffn_gate", "w_ffn_up", "w_ffn_down"):
        return a[..., :_FFN_SHARD, :]
    return a


def _rows_view(name, a):
    return jnp.swapaxes(a, 0, 1) if name in _COL_SHARDED else a


def _shard_payload(name, shard):
    return _pad_rows(name, _rows_view(name, shard).astype(MXU_DTYPE))


def _full_from_gathered(name, g):
    full = g.reshape(g.shape[0] * g.shape[1], g.shape[2])
    if name == "w_in":
        full = jnp.pad(full, ((0, IN_WIDTH_PAD - IN_WIDTH), (0, 0)))
    return full


def _grad_chunks(name, gfull):
    if name == "w_in":
        gfull = gfull[:IN_WIDTH]
    return gfull.reshape(N_DEV, gfull.shape[0] // N_DEV, gfull.shape[1])


def _ffn_vec_layout(a):
    a = a.reshape(a.shape[0], N_DEV, _FFN_SHARD)
    return jnp.pad(a, ((0, 0), (0, 0), (0, _FFN_SHARD_PAD - _FFN_SHARD))).reshape(a.shape[0], D_FF_PAD)


def _ffn_vec_shards(a):
    return a.reshape(a.shape[0], N_DEV, _FFN_SHARD_PAD)[:, :, :_FFN_SHARD]


_MESH = pl.DeviceIdType.MESH
_ANY = pl.BlockSpec(memory_space=pl.ANY)


def _place():
    x, y, c = lax.axis_index("x"), lax.axis_index("y"), lax.axis_index("c")
    chips = [(1 - x, y), (x, 1 - y), (1 - x, 1 - y)]
    return x, y, c, chips


def _handshake(peers):
    barrier = pltpu.get_barrier_semaphore()
    for peer in peers:
        pl.semaphore_signal(barrier, inc=1, device_id=peer, device_id_type=_MESH)
    pl.semaphore_wait(barrier, len(peers))


_SEQUENCER = dict(axis_name="seq", num_cores=1)
_AG_COLLECTIVE_ID = 1
_RS_SIBLING_COLLECTIVE_ID = 2
_RS_CHIPS_COLLECTIVE_ID = 3


def _all_gather_seq(shard, *, name):
    def body(x_ref, out_ref, send_sems, recv_sems, local_sem):
        x, y, c, chips = _place()
        sibling = (x, y, 1 - c)
        _handshake([sibling] + [(*chip, c) for chip in chips])

        def slot(px, py, pc):
            return out_ref.at[4 * px + 2 * py + pc]

        def copy(k, block, to, src=None):
            return pltpu.make_async_remote_copy(
                src_ref=slot(*block) if src is None else src, dst_ref=slot(*block),
                send_sem=send_sems.at[k], recv_sem=recv_sems.at[k], device_id=to, device_id_type=_MESH)

        mine = pltpu.make_async_copy(x_ref, slot(x, y, c), local_sem)
        mine.start()
        first = [copy(0, (x, y, c), sibling, src=x_ref)]
        first += [copy(1 + j, (x, y, c), (*chip, c), src=x_ref) for j, chip in enumerate(chips)]
        for cp in first:
            cp.start()
        passed = [copy(4 + j, (*chip, c), sibling) for j, chip in enumerate(chips)]
        for j, chip in enumerate(chips):
            copy(1 + j, (*chip, c), (x, y, c)).wait_recv()
            passed[j].start()
        copy(0, sibling, (x, y, c)).wait_recv()
        for j, chip in enumerate(chips):
            copy(4 + j, (*chip, 1 - c), (x, y, c)).wait_recv()
        for cp in first + passed:
            cp.wait_send()
        mine.wait()

    return pl.kernel(
        body, out_type=jax.ShapeDtypeStruct((N_DEV,) + shard.shape, shard.dtype),
        mesh=plsc.ScalarSubcoreMesh(**_SEQUENCER), name=name,
        scratch_types=[pltpu.SemaphoreType.DMA((7,)), pltpu.SemaphoreType.DMA((7,)), pltpu.SemaphoreType.DMA(())],
        compiler_params=pltpu.CompilerParams(collective_id=_AG_COLLECTIVE_ID),
    )(shard)


def _exchange_sibling(g, *, name):
    def body(g_ref, out_ref, send_sems, recv_sems):
        x, y, c, _ = _place()
        sibling = (x, y, 1 - c)
        _handshake([sibling])
        copies = []
        for chip in range(4):
            cp = pltpu.make_async_remote_copy(
                src_ref=g_ref.at[2 * chip + (1 - c)], dst_ref=out_ref.at[chip],
                send_sem=send_sems.at[chip], recv_sem=recv_sems.at[chip], device_id=sibling, device_id_type=_MESH)
            cp.start()
            copies.append(cp)
        for cp in copies:
            cp.wait_recv()
        for cp in copies:
            cp.wait_send()

    return pl.kernel(
        body, out_type=jax.ShapeDtypeStruct((4,) + g.shape[1:], g.dtype),
        mesh=plsc.ScalarSubcoreMesh(**_SEQUENCER), name=name,
        scratch_types=[pltpu.SemaphoreType.DMA((4,)), pltpu.SemaphoreType.DMA((4,))],
        compiler_params=pltpu.CompilerParams(collective_id=_RS_SIBLING_COLLECTIVE_ID),
    )(g)


def _add_sibling(g, recv, *, name):
    _, r, cdim = g.shape
    tr, tc = _tile_2d(r, cdim, 6)
    g4 = g.reshape(4, 2, r, cdim)
    core = lax.axis_index("c").astype(jnp.int32).reshape(1)

    def body(c_ref, g_ref, r_ref, o_ref):
        o_ref[...] = (g_ref[...].astype(F32) + r_ref[...].astype(F32)).astype(o_ref.dtype)

    return pl.pallas_call(
        body, name=name,
        grid_spec=pltpu.PrefetchScalarGridSpec(
            num_scalar_prefetch=1, grid=(4, r // tr, cdim // tc),
            in_specs=[pl.BlockSpec((None, None, tr, tc), lambda ch, i, j, c_ref: (ch, c_ref[0], i, j)),
                      pl.BlockSpec((None, tr, tc), lambda ch, i, j, c_ref: (ch, i, j))],
            out_specs=pl.BlockSpec((None, tr, tc), lambda ch, i, j, c_ref: (ch, i, j))),
        out_shape=jax.ShapeDtypeStruct((4, r, cdim), g.dtype),
        compiler_params=_params(("parallel", "parallel", "parallel")),
    )(core, g4, recv)


def _exchange_chips(pch, *, name):
    def body(p_ref, out_ref, send_sems, recv_sems, local_sem):
        x, y, c, chips = _place()
        _handshake([(*chip, c) for chip in chips])
        me = 2 * x + y
        mine = pltpu.make_async_copy(p_ref.at[me], out_ref.at[me], local_sem)
        mine.start()
        copies = []
        for j, (px, py) in enumerate(chips):
            cp = pltpu.make_async_remote_copy(
                src_ref=p_ref.at[2 * px + py], dst_ref=out_ref.at[me],
                send_sem=send_sems.at[j], recv_sem=recv_sems.at[j], device_id=(px, py, c), device_id_type=_MESH)
            cp.start()
            copies.append(cp)
        for j, (px, py) in enumerate(chips):
            pltpu.make_async_remote_copy(
                src_ref=p_ref.at[me], dst_ref=out_ref.at[2 * px + py],
                send_sem=send_sems.at[j], recv_sem=recv_sems.at[j], device_id=(px, py, c), device_id_type=_MESH).wait_recv()
        for cp in copies:
            cp.wait_send()
        mine.wait()

    return pl.kernel(
        body, out_type=jax.ShapeDtypeStruct(pch.shape, pch.dtype),
        mesh=plsc.ScalarSubcoreMesh(**_SEQUENCER), name=name,
        scratch_types=[pltpu.SemaphoreType.DMA((3,)), pltpu.SemaphoreType.DMA((3,)), pltpu.SemaphoreType.DMA(())],
        compiler_params=pltpu.CompilerParams(collective_id=_RS_CHIPS_COLLECTIVE_ID),
    )(pch)


def _all_reduce_small(v, *, name):
    r = v.shape[0]

    def body(x_ref, out_ref, buf_ref, send_sems, recv_sems):
        x, y, c, chips = _place()
        sibling = (x, y, 1 - c)

        def slot(px, py, pc):
            return buf_ref.at[4 * px + 2 * py + pc]

        def copy(k, block, to, src=None):
            return pltpu.make_async_remote_copy(
                src_ref=slot(*block) if src is None else src, dst_ref=slot(*block),
                send_sem=send_sems.at[k], recv_sem=recv_sems.at[k], device_id=to, device_id_type=_MESH)

        first = [copy(0, (x, y, c), sibling, src=x_ref)]
        first += [copy(1 + j, (x, y, c), (*chip, c), src=x_ref) for j, chip in enumerate(chips)]
        for cp in first:
            cp.start()
        buf_ref[4 * x + 2 * y + c] = x_ref[...]
        passed = [copy(4 + j, (*chip, c), sibling) for j, chip in enumerate(chips)]
        for j, chip in enumerate(chips):
            copy(1 + j, (*chip, c), (x, y, c)).wait_recv()
            passed[j].start()
        copy(0, sibling, (x, y, c)).wait_recv()
        for j, chip in enumerate(chips):
            copy(4 + j, (*chip, 1 - c), (x, y, c)).wait_recv()
        for cp in first + passed:
            cp.wait_send()
        total = buf_ref[0]
        for k in range(1, N_DEV):
            total = total + buf_ref[k]
        out_ref[...] = total

    vm = pl.BlockSpec(memory_space=pltpu.VMEM)
    return pl.pallas_call(
        body, name=name, out_shape=jax.ShapeDtypeStruct(v.shape, v.dtype),
        in_specs=[vm], out_specs=vm,
        scratch_shapes=[pltpu.VMEM((N_DEV,) + v.shape, v.dtype), pltpu.SemaphoreType.DMA((7,)),
                        pltpu.SemaphoreType.DMA((7,))],
    )(v)


_ELEMENTWISE_VMEM = 24 * 1024 * 1024


def _tile_2d(r, c, n_arrays):
    per_block = _ELEMENTWISE_VMEM // (8 * n_arrays)
    tr = _tile(r, max(16, per_block // max(c, 128)), 16)
    if tr * c <= per_block:
        return tr, c
    return r, _tile(c, max(128, (per_block // r) // 128 * 128))


def _adam_math(w, g, m, v):
    m = ADAM_B1 * m + (1.0 - ADAM_B1) * g
    v = ADAM_B2 * v + (1.0 - ADAM_B2) * jnp.square(g)
    m_hat = m / (1.0 - ADAM_B1 ** ADAM_STEP)
    v_hat = v / (1.0 - ADAM_B2 ** ADAM_STEP)
    delta = -ADAM_LR * (m_hat / (jnp.sqrt(v_hat) + ADAM_EPS) + ADAM_WD * w)
    return delta, m, v


def _adam(w, g, m, v, *, name, parts=None):
    r, cdim = w.shape
    tr, tc = _tile_2d(r, cdim, 8)

    def body(w_ref, g_ref, m_ref, v_ref, go_ref, d_ref, mo_ref, vo_ref):
        if parts is None:
            g = g_ref[...]
        else:
            g = g_ref[0].astype(F32)
            for k in range(1, parts):
                g = g + g_ref[k].astype(F32)
        delta, m, v = _adam_math(w_ref[...], g, m_ref[...], v_ref[...])
        go_ref[...] = g
        d_ref[...] = delta
        mo_ref[...] = m
        vo_ref[...] = v

    blk = pl.BlockSpec((tr, tc), lambda i, j: (i, j))
    gblk = blk if parts is None else pl.BlockSpec((parts, tr, tc), lambda i, j: (0, i, j))
    out = jax.ShapeDtypeStruct((r, cdim), F32)
    return pl.pallas_call(
        body, name=name, grid=(r // tr, cdim // tc), in_specs=[blk, gblk, blk, blk], out_specs=[blk] * 4,
        out_shape=[out] * 4, compiler_params=_params(("parallel", "parallel")),
    )(w, g, m, v)


_BIG = ("w_in", "w_uq", "w_ukv", "w_o", "w_ffn_gate", "w_ffn_up", "w_ffn_down", "w_ple_gate", "w_ple_proj")
_WEIGHTS = ("w_in", "g_attn", "g_q_lora", "g_kv_lora", "w_uq", "w_ukv", "w_o", "g_ffn", "w_ffn_gate", "w_ffn_up",
            "conv_w", "conv_b", "w_ffn_down", "g_ple", "w_ple_gate", "w_ple_proj", "g_final")
_SMALL_PACK = (("g_attn", 1, D_MODEL), ("g_q_lora", 1, Q_LORA), ("g_kv_lora", 1, KV_LORA), ("g_ffn", 1, D_MODEL),
               ("conv_w", CONV_WIDTH, D_FF_PAD), ("conv_b", 1, D_FF_PAD), ("g_ple", 1, D_MODEL), ("g_final", 1, D_MODEL))


def _pack_small(gs):
    flat = jnp.concatenate([gs[n].reshape(-1) for n, _, _ in _SMALL_PACK])
    rows = -(-flat.shape[0] // 128)
    rows = -(-rows // 8) * 8
    return jnp.pad(flat, (0, rows * 128 - flat.shape[0])).reshape(rows, 128)


def _unpack_small(packed):
    flat = packed.reshape(-1)
    out, off = {}, 0
    for n, r, c in _SMALL_PACK:
        out[n] = flat[off:off + r * c].reshape(r, c)
        off += r * c
    return out


def kernel(x, p, w_in, g_attn, g_q_lora, g_kv_lora, w_uq, w_ukv, w_o, g_ffn, w_ffn_gate, w_ffn_up, conv_w, conv_b, w_ffn_down, g_ple, w_ple_gate, w_ple_proj, g_final, loss_target, m_w_in, m_g_attn, m_g_q_lora, m_g_kv_lora, m_w_uq, m_w_ukv, m_w_o, m_g_ffn, m_w_ffn_gate, m_w_ffn_up, m_conv_w, m_conv_b, m_w_ffn_down, m_g_ple, m_w_ple_gate, m_w_ple_proj, m_g_final, v_w_in, v_g_attn, v_g_q_lora, v_g_kv_lora, v_w_uq, v_w_ukv, v_w_o, v_g_ffn, v_w_ffn_gate, v_w_ffn_up, v_conv_w, v_conv_b, v_w_ffn_down, v_g_ple, v_w_ple_gate, v_w_ple_proj, v_g_final):
    given = dict(locals())
    wts = {n: given[n] for n in _WEIGHTS}
    mom = {n: given["m_" + n] for n in _WEIGHTS}
    var = {n: given["v_" + n] for n in _WEIGHTS}
    me = (4 * lax.axis_index("x") + 2 * lax.axis_index("y") + lax.axis_index("c")).astype(jnp.int32)
    ops = _ShardedWeights(wts, mom, var)

    conv_full = _all_gather_seq(conv_w, name="ag_conv_w")[:, 0].transpose(1, 0, 2).reshape(CONV_WIDTH, D_FF)
    vec = {"g_attn": g_attn, "g_q_lora": g_q_lora, "g_kv_lora": g_kv_lora, "g_ffn": g_ffn, "g_ple": g_ple,
           "g_final": g_final[None, :], "conv_w": _ffn_vec_layout(conv_full), "conv_b": _ffn_vec_layout(conv_b)}

    loss_part, grad_x, gs = _local_step(x[0], p[0, 0], loss_target[0], vec, ops)
    loss = lax.psum(loss_part[0, 0], ("x", "y", "c"))

    small = _unpack_small(_all_reduce_small(ops.order.tie(_pack_small(gs)), name="ar_small"))
    conv_w_shards = _ffn_vec_shards(small["conv_w"])
    small_g = {
        "g_attn": small["g_attn"], "g_q_lora": small["g_q_lora"], "g_kv_lora": small["g_kv_lora"],
        "g_ffn": small["g_ffn"], "g_ple": small["g_ple"], "g_final": small["g_final"],
        "conv_b": _ffn_vec_shards(small["conv_b"]).reshape(1, D_FF),
        "conv_w": lax.dynamic_index_in_dim(conv_w_shards, me, axis=1, keepdims=False),
    }
    results = dict(ops.results)
    for n, g in small_g.items():
        shape = wts[n].shape
        outs = ops.order.run(_adam, wts[n].reshape(g.shape), g, mom[n].reshape(g.shape), var[n].reshape(g.shape),
                             name="adam_" + n)
        results[n] = tuple(a.reshape(shape) for a in outs)
    ops.update("w_in")
    results["w_in"] = ops.results["w_in"]

    return (loss, grad_x[None], *[results[n][0] for n in _WEIGHTS], *[results[n][1] for n in _WEIGHTS],
            *[results[n][2] for n in _WEIGHTS], *[results[n][3] for n in _WEIGHTS])


class _ShardedWeights:
    def __init__(self, wts, mom, var):
        self.wts, self.mom, self.var = wts, mom, var
        self.order = _Order()
        self.full, self.stage, self.results = {}, {}, {}

    def start_gather(self, name, after=None):
        payload = _shard_payload(name, self.wts[name][0])
        if after is not None:
            payload = lax.optimization_barrier((payload, after))[0]
        self.full[name] = _full_from_gathered(name, _all_gather_seq(payload, name="ag_" + name))

    def weight(self, name):
        return self.full[name]

    def grad(self, name, gfull):
        chunks = _grad_chunks(name, gfull)
        self.stage[name] = (chunks, _exchange_sibling(chunks, name="rs_sib_" + name))

    def reduce_add(self, name):
        chunks, from_sibling = self.stage[name]
        per_chip = self.order.run(_add_sibling, chunks, from_sibling, name="rs_add_" + name)
        self.stage[name] = _exchange_chips(per_chip, name="rs_chip_" + name)

    def update(self, name):
        parts = self.stage[name]
        if name == "w_uq":
            parts = _unpad_rows(name, parts)
        rows = lambda a: _rows_view(name, a[0])
        outs = self.order.run(_adam, rows(self.wts[name]), parts, rows(self.mom[name]), rows(self.var[name]),
                              name="adam_" + name, parts=4)
        self.results[name] = tuple(_rows_view(name, a)[None] for a in outs)
```

```python
import functools

import numpy as np

import jax
import jax.numpy as jnp
from jax import lax
from jax.experimental import pallas as pl
from jax.experimental.pallas import tpu as pltpu
from jax.experimental.pallas import tpu_sc as plsc

D_MODEL = 4096
CHUNK = 64
PLE_DIM = 256
RET_HEADS = 8
RET_HEAD_DIM = 256
RET_WIDTH = 2048
MLA_HEADS = 16
MLA_NOPE = 128
MLA_ROPE = 64
MLA_V = 128
Q_LORA = 1024
KV_LORA = 512
D_FF = 11008
CONV_WIDTH = 3
ROPE_BASE = 10000.0
EPS = 1e-6
IN_WIDTH = 9792
ADAM_LR, ADAM_B1, ADAM_B2, ADAM_EPS, ADAM_WD, ADAM_STEP = 0.001, 0.9, 0.999, 1e-08, 0.01, 10

IN_WIDTH_PAD = 10240
D_FF_PAD = 11264
MLA_QK_PAD = 256
Q_WIDTH_PAD = MLA_HEADS * MLA_QK_PAD

N_DEV = 8
MXU_DTYPE = jnp.bfloat16
ATTN_BLOCK = 512
HEADS_PER_STEP = 2
VMEM_LIMIT = 56 * 1024 * 1024

F32 = jnp.float32


def _tile(n, want, align=128):
    if n <= want:
        return n
    t = (want // align) * align
    while t >= align:
        if n % t == 0:
            return t
        t -= align
    return n


def _params(sem):
    return pltpu.CompilerParams(dimension_semantics=sem, vmem_limit_bytes=VMEM_LIMIT)


def _sigmoid(x):
    return 1.0 / (1.0 + jnp.exp(-x))


def _matmul(a, b, *, name, ta=False, tb=False, out_dtype=F32, add=None, tm=1024, tn=1024, tk=4096):
    m, k = (a.shape[1], a.shape[0]) if ta else a.shape
    k2, n = (b.shape[1], b.shape[0]) if tb else b.shape
    assert k == k2, (a.shape, b.shape, ta, tb)
    tm, tn, tk = _tile(m, tm), _tile(n, tn), _tile(k, tk)
    nk = k // tk
    dims = (((0 if ta else 1,), (1 if tb else 0,)), ((), ()))

    def body(*refs):
        a_ref, b_ref, o_ref = refs[0], refs[1], refs[3 if add is not None else 2]
        c_ref = refs[2] if add is not None else None
        part = lax.dot_general(a_ref[...].astype(MXU_DTYPE), b_ref[...].astype(MXU_DTYPE), dims,
                               preferred_element_type=F32)

        def finish(r):
            if c_ref is not None:
                r = r + c_ref[...].astype(F32)
            o_ref[...] = r.astype(out_dtype)

        if nk == 1:
            finish(part)
            return
        acc_ref = refs[-1]
        kk = pl.program_id(2)

        @pl.when(kk == 0)
        def _():
            acc_ref[...] = part

        @pl.when((kk > 0) & (kk < nk - 1))
        def _():
            acc_ref[...] += part

        @pl.when(kk == nk - 1)
        def _():
            finish(acc_ref[...] + part)

    a_spec = pl.BlockSpec((tk, tm), lambda i, j, kk: (kk, i)) if ta else pl.BlockSpec((tm, tk), lambda i, j, kk: (i, kk))
    b_spec = pl.BlockSpec((tn, tk), lambda i, j, kk: (j, kk)) if tb else pl.BlockSpec((tk, tn), lambda i, j, kk: (kk, j))
    in_specs = [a_spec, b_spec]
    args = [a, b]
    if add is not None:
        in_specs.append(pl.BlockSpec((tm, tn), lambda i, j, kk: (i, j)))
        args.append(add)
    return pl.pallas_call(
        body, name=name, grid=(m // tm, n // tn, nk),
        in_specs=in_specs, out_specs=pl.BlockSpec((tm, tn), lambda i, j, kk: (i, j)),
        out_shape=jax.ShapeDtypeStruct((m, n), out_dtype),
        scratch_shapes=[] if nk == 1 else [pltpu.VMEM((tm, tn), F32)],
        compiler_params=_params(("parallel", "parallel", "arbitrary")),
    )(*args)


def _rms_fwd(x, g, *, name):
    t, d = x.shape
    tr = _tile(t, 256, 8)

    def body(x_ref, g_ref, o_ref):
        xf = x_ref[...]
        r = lax.rsqrt(jnp.mean(xf * xf, axis=-1, keepdims=True) + EPS)
        o_ref[...] = (xf * r * g_ref[...]).astype(o_ref.dtype)

    return pl.pallas_call(
        body, name=name, grid=(t // tr,),
        in_specs=[pl.BlockSpec((tr, d), lambda i: (i, 0)), pl.BlockSpec((1, d), lambda i: (0, 0))],
        out_specs=pl.BlockSpec((tr, d), lambda i: (i, 0)),
        out_shape=jax.ShapeDtypeStruct((t, d), MXU_DTYPE),
        compiler_params=_params(("parallel",)),
    )(x, g)


def _rms_bwd(x, dhn, g, res, *, name, low_copy=False):
    t, d = x.shape
    tr = _tile(t, 256, 8)

    def body(*refs):
        if res is None:
            x_ref, dh_ref, g_ref = refs[:3]
            outs = refs[3:]
            res_ref = None
        else:
            x_ref, dh_ref, g_ref, res_ref = refs[:4]
            outs = refs[4:]
        dx_ref, dg_ref = outs[0], outs[-1]
        xf = x_ref[...]
        dh = dh_ref[...].astype(F32)
        r = lax.rsqrt(jnp.mean(xf * xf, axis=-1, keepdims=True) + EPS)
        dyg = dh * g_ref[...]
        dx = r * dyg - xf * (r * r * r) * jnp.mean(dyg * xf, axis=-1, keepdims=True)
        if res_ref is not None:
            dx = dx + res_ref[...]
        dx_ref[...] = dx
        if low_copy:
            outs[1][...] = dx.astype(outs[1].dtype)
        part = jnp.sum(dh * xf * r, axis=0, keepdims=True)

        @pl.when(pl.program_id(0) == 0)
        def _():
            dg_ref[...] = part

        @pl.when(pl.program_id(0) > 0)
        def _():
            dg_ref[...] += part

    row = pl.BlockSpec((tr, d), lambda i: (i, 0))
    vec = pl.BlockSpec((1, d), lambda i: (0, 0))
    in_specs = [row, row, vec] + ([] if res is None else [row])
    args = [x, dhn, g] + ([] if res is None else [res])
    out_specs = [row] + ([row] if low_copy else []) + [vec]
    out_shape = [jax.ShapeDtypeStruct((t, d), F32)] + ([jax.ShapeDtypeStruct((t, d), MXU_DTYPE)] if low_copy else []) \
        + [jax.ShapeDtypeStruct((1, d), F32)]
    return pl.pallas_call(
        body, name=name, grid=(t // tr,), in_specs=in_specs, out_specs=out_specs, out_shape=out_shape,
        compiler_params=_params(("arbitrary",)),
    )(*args)


def _rope_tables(t):
    pos = jnp.arange(t, dtype=F32)[:, None]
    inv_r = 1.0 / (ROPE_BASE ** (jnp.arange(0, RET_HEAD_DIM, 2, dtype=F32) / RET_HEAD_DIM))
    ang_r = pos * inv_r[None, :]
    inv_m = 1.0 / (ROPE_BASE ** (jnp.arange(0, MLA_ROPE, 2, dtype=F32) / MLA_ROPE))
    ang_m = pos * inv_m[None, :]
    cm, sm = jnp.cos(ang_m), jnp.sin(ang_m)
    z = jnp.zeros_like(cm)
    cos_m = jnp.concatenate([cm, cm, z, z], axis=1)
    sin_m = jnp.concatenate([-sm, sm, z, z], axis=1)
    return jnp.cos(ang_r), jnp.sin(ang_r), cos_m, sin_m


def _rope256(x, c, s, inverse=False):
    x1, x2 = x[:, :128], x[:, 128:]
    if inverse:
        s = -s
    return jnp.concatenate([x1 * c - x2 * s, x2 * c + x1 * s], axis=1)


def _rope64(x, cos_m, sin_m, inverse=False):
    lane = lax.broadcasted_iota(jnp.int32, x.shape, 1)
    partner = jnp.where(lane < 32, pltpu.roll(x, 96, 1), pltpu.roll(x, 32, 1))
    s = -sin_m if inverse else sin_m
    return x * cos_m + partner * s


def _ret_prep(proj, cos_r, sin_r):
    t = proj.shape[0]
    tr = _tile(t, 512, 8)
    hb = RET_WIDTH // RET_HEAD_DIM

    def body(q_ref, k_ref, v_ref, c_ref, s_ref, qo_ref, ko_ref, vo_ref):
        c, s = c_ref[...], s_ref[...]
        qo_ref[...] = _rope256(q_ref[...], c, s).astype(qo_ref.dtype)
        ko_ref[...] = (_rope256(k_ref[...], c, s) * (RET_HEAD_DIM ** -0.5)).astype(ko_ref.dtype)
        vo_ref[...] = v_ref[...].astype(vo_ref.dtype)

    head = lambda off: pl.BlockSpec((tr, RET_HEAD_DIM), lambda i, h: (i, h + off))
    tab = pl.BlockSpec((tr, 128), lambda i, h: (i, 0))
    out = jax.ShapeDtypeStruct((t, RET_WIDTH), MXU_DTYPE)
    return pl.pallas_call(
        body, name="ret_prep", grid=(t // tr, RET_HEADS),
        in_specs=[head(0), head(hb), head(2 * hb), tab, tab],
        out_specs=[head(0), head(0), head(0)], out_shape=[out, out, out],
        compiler_params=_params(("parallel", "parallel")),
    )(proj, proj, proj, cos_r, sin_r)


def _ret_log_gamma():
    return jnp.asarray(np.log1p(-np.exp2(-5.0 - np.arange(RET_HEADS, dtype=np.float64))), dtype=F32)


def _decay_full(lg, i, j, blk):
    r = lax.broadcasted_iota(jnp.int32, (blk, 1), 0).astype(F32)
    c = lax.broadcasted_iota(jnp.int32, (1, blk), 1).astype(F32)
    off = ((i - j) * blk).astype(F32)
    return jnp.exp(lg * r), jnp.exp(lg * (off - c))


def _decay_diag(lg, blk):
    r = lax.broadcasted_iota(jnp.int32, (blk, blk), 0)
    c = lax.broadcasted_iota(jnp.int32, (blk, blk), 1)
    ok = (c // CHUNK) <= (r // CHUNK)
    return jnp.where(ok, jnp.exp(lg * jnp.abs(r - c).astype(F32)), 0.0)


_NT = (((1,), (1,)), ((), ()))
_TN = (((0,), (0,)), ((), ()))
_NN = (((1,), (0,)), ((), ()))


def _causal_pairs(nb, query_major):
    if query_major:
        pairs = [(i, j) for i in range(nb) for j in range(i + 1)]
    else:
        pairs = [(i, j) for j in range(nb) for i in range(j, nb)]
    arr = np.asarray(pairs, dtype=np.int32)
    return jnp.asarray(arr[:, 0]), jnp.asarray(arr[:, 1])


def _ret_fwd(q, k, v, proj, lg):
    t = q.shape[0]
    blk = _tile(t, ATTN_BLOCK)
    nb = t // blk
    hps, d = HEADS_PER_STEP, RET_HEAD_DIM
    gate_off = 3 * RET_WIDTH // (hps * d)

    def body(ii_ref, jj_ref, lg_ref, q_ref, k_ref, v_ref, g_ref, raw_ref, ro_ref, acc_ref):
        hg, pair = pl.program_id(0), pl.program_id(1)
        i, j = ii_ref[pair], jj_ref[pair]

        @pl.when(j == 0)
        def _():
            acc_ref[...] = jnp.zeros_like(acc_ref)

        def step(diag):
            for h in range(hps):
                cols = slice(h * d, (h + 1) * d)
                lgh = lg_ref[hg * hps + h]
                s = lax.dot_general(q_ref[:, cols], k_ref[:, cols], _NT, preferred_element_type=F32)
                if diag:
                    w = s * _decay_diag(lgh, blk)
                else:
                    a, b = _decay_full(lgh, i, j, blk)
                    w = s * a * b
                acc_ref[h] += lax.dot_general(w.astype(MXU_DTYPE), v_ref[:, cols], _NN, preferred_element_type=F32)

        @pl.when(j < i)
        def _():
            step(False)

        @pl.when(j == i)
        def _():
            step(True)
            for h in range(hps):
                cols = slice(h * d, (h + 1) * d)
                o = acc_ref[h]
                raw_ref[:, cols] = o
                mu = jnp.mean(o, axis=-1, keepdims=True)
                var = jnp.mean(jnp.square(o - mu), axis=-1, keepdims=True)
                hn = (o - mu) * lax.rsqrt(var + EPS)
                g = g_ref[:, cols]
                ro_ref[:, cols] = (g * _sigmoid(g) * hn).astype(ro_ref.dtype)

    qs = pl.BlockSpec((blk, hps * d), lambda h, p, ii, jj: (ii[p], h))
    ks = pl.BlockSpec((blk, hps * d), lambda h, p, ii, jj: (jj[p], h))
    gs = pl.BlockSpec((blk, hps * d), lambda h, p, ii, jj: (ii[p], h + gate_off))
    ii, jj = _causal_pairs(nb, query_major=True)
    return pl.pallas_call(
        body, name="ret_fwd",
        grid_spec=pltpu.PrefetchScalarGridSpec(
            num_scalar_prefetch=2, grid=(RET_HEADS // hps, ii.shape[0]),
            in_specs=[pl.BlockSpec(memory_space=pltpu.SMEM), qs, ks, ks, gs], out_specs=[qs, qs],
            scratch_shapes=[pltpu.VMEM((hps, blk, d), F32)]),
        out_shape=[jax.ShapeDtypeStruct((t, RET_WIDTH), F32), jax.ShapeDtypeStruct((t, RET_WIDTH), MXU_DTYPE)],
        compiler_params=_params(("parallel", "arbitrary")),
    )(ii, jj, lg, q, k, v, proj)


def _ret_gate_bwd(raw, proj, dattn):
    t = raw.shape[0]
    tr = _tile(t, 512, 8)
    gate_off = 3 * RET_WIDTH // RET_HEAD_DIM

    def body(o_ref, g_ref, d_ref, do_ref, dg_ref):
        o, g, d = o_ref[...], g_ref[...], d_ref[...]
        mu = jnp.mean(o, axis=-1, keepdims=True)
        rstd = lax.rsqrt(jnp.mean(jnp.square(o - mu), axis=-1, keepdims=True) + EPS)
        hn = (o - mu) * rstd
        sg = _sigmoid(g)
        dg_ref[...] = (d * hn * (sg * (1.0 + g * (1.0 - sg)))).astype(dg_ref.dtype)
        dhn = d * (g * sg)
        do = rstd * (dhn - jnp.mean(dhn, axis=-1, keepdims=True) - hn * jnp.mean(dhn * hn, axis=-1, keepdims=True))
        do_ref[...] = do.astype(do_ref.dtype)

    hs = pl.BlockSpec((tr, RET_HEAD_DIM), lambda i, h: (i, h))
    gs = pl.BlockSpec((tr, RET_HEAD_DIM), lambda i, h: (i, h + gate_off))
    out = jax.ShapeDtypeStruct((t, RET_WIDTH), MXU_DTYPE)
    return pl.pallas_call(
        body, name="ret_gate_bwd", grid=(t // tr, RET_HEADS),
        in_specs=[hs, gs, hs], out_specs=[hs, hs], out_shape=[out, out],
        compiler_params=_params(("parallel", "parallel")),
    )(raw, proj, dattn)


def _ret_bwd(q, k, v, do, lg):
    t = q.shape[0]
    blk = _tile(t, ATTN_BLOCK)
    nb = t // blk

    hps, d = HEADS_PER_STEP, RET_HEAD_DIM

    def body(ii_ref, jj_ref, lg_ref, q_ref, k_ref, v_ref, do_ref, dq_ref, dk_ref, dv_ref, dk_acc, dv_acc):
        hg, pair = pl.program_id(0), pl.program_id(1)
        i, j = ii_ref[pair], jj_ref[pair]

        @pl.when(pair == 0)
        def _():
            dq_ref[...] = jnp.zeros_like(dq_ref)

        @pl.when(i == j)
        def _():
            dk_acc[...] = jnp.zeros_like(dk_acc)
            dv_acc[...] = jnp.zeros_like(dv_acc)

        def step(diag):
            rows = pl.ds(pl.multiple_of(i * blk, blk), blk)
            for h in range(hps):
                cols = slice(h * d, (h + 1) * d)
                lgh = lg_ref[hg * hps + h]
                if diag:
                    decay = _decay_diag(lgh, blk)
                else:
                    a, b = _decay_full(lgh, i, j, blk)
                    decay = a * b
                qb, kb, vb, dob = q_ref[:, cols], k_ref[:, cols], v_ref[:, cols], do_ref[:, cols]
                s = lax.dot_general(qb, kb, _NT, preferred_element_type=F32)
                w = (s * decay).astype(MXU_DTYPE)
                dv_acc[h] += lax.dot_general(w, dob, _TN, preferred_element_type=F32)
                dw = lax.dot_general(dob, vb, _NT, preferred_element_type=F32)
                ds = (dw * decay).astype(MXU_DTYPE)
                dq_ref[rows, cols] += lax.dot_general(ds, kb, _NN, preferred_element_type=F32)
                dk_acc[h] += lax.dot_general(ds, qb, _TN, preferred_element_type=F32)

        @pl.when(i > j)
        def _():
            step(False)

        @pl.when(i == j)
        def _():
            step(True)

        @pl.when(i == nb - 1)
        def _():
            for h in range(hps):
                cols = slice(h * d, (h + 1) * d)
                dk_ref[:, cols] = dk_acc[h]
                dv_ref[:, cols] = dv_acc[h].astype(dv_ref.dtype)

    qs = pl.BlockSpec((blk, hps * d), lambda h, p, ii, jj: (ii[p], h))
    ks = pl.BlockSpec((blk, hps * d), lambda h, p, ii, jj: (jj[p], h))
    ii, jj = _causal_pairs(nb, query_major=False)
    return pl.pallas_call(
        body, name="ret_bwd",
        grid_spec=pltpu.PrefetchScalarGridSpec(
            num_scalar_prefetch=2, grid=(RET_HEADS // hps, ii.shape[0]),
            in_specs=[pl.BlockSpec(memory_space=pltpu.SMEM), qs, ks, ks, qs],
            out_specs=[pl.BlockSpec((t, hps * d), lambda h, p, ii, jj: (0, h)), ks, ks],
            scratch_shapes=[pltpu.VMEM((hps, blk, d), F32), pltpu.VMEM((hps, blk, d), F32)]),
        out_shape=[jax.ShapeDtypeStruct((t, RET_WIDTH), F32), jax.ShapeDtypeStruct((t, RET_WIDTH), F32),
                   jax.ShapeDtypeStruct((t, RET_WIDTH), MXU_DTYPE)],
        compiler_params=_params(("parallel", "arbitrary")),
    )(ii, jj, lg, q, k, v, do)


def _ret_unrope(dq, dk, cos_r, sin_r):
    t = dq.shape[0]
    tr = _tile(t, 512, 8)

    def body(dq_ref, dk_ref, c_ref, s_ref, oq_ref, ok_ref):
        c, s = c_ref[...], s_ref[...]
        oq_ref[...] = _rope256(dq_ref[...], c, s, inverse=True).astype(oq_ref.dtype)
        ok_ref[...] = (_rope256(dk_ref[...], c, s, inverse=True) * (RET_HEAD_DIM ** -0.5)).astype(ok_ref.dtype)

    hs = pl.BlockSpec((tr, RET_HEAD_DIM), lambda i, h: (i, h))
    tab = pl.BlockSpec((tr, 128), lambda i, h: (i, 0))
    out = jax.ShapeDtypeStruct((t, RET_WIDTH), MXU_DTYPE)
    return pl.pallas_call(
        body, name="ret_unrope", grid=(t // tr, RET_HEADS),
        in_specs=[hs, hs, tab, tab], out_specs=[hs, hs], out_shape=[out, out],
        compiler_params=_params(("parallel", "parallel")),
    )(dq, dk, cos_r, sin_r)


def _mla_prep(cq, ckv, kr, g_q, g_kv, cos_m, sin_m):
    t = cq.shape[0]
    tr = _tile(t, 512, 8)

    def body(cq_ref, ckv_ref, kr_ref, gq_ref, gkv_ref, c_ref, s_ref, cqn_ref, kvn_ref, kro_ref):
        for x_ref, g_ref, o_ref in ((cq_ref, gq_ref, cqn_ref), (ckv_ref, gkv_ref, kvn_ref)):
            xf = x_ref[...]
            r = lax.rsqrt(jnp.mean(xf * xf, axis=-1, keepdims=True) + EPS)
            o_ref[...] = (xf * r * g_ref[...]).astype(o_ref.dtype)
        kro_ref[...] = _rope64(kr_ref[...], c_ref[...], s_ref[...]).astype(kro_ref.dtype)

    row = lambda w: pl.BlockSpec((tr, w), lambda i: (i, 0))
    vec = lambda w: pl.BlockSpec((1, w), lambda i: (0, 0))
    return pl.pallas_call(
        body, name="mla_prep", grid=(t // tr,),
        in_specs=[row(Q_LORA), row(KV_LORA), row(128), vec(Q_LORA), vec(KV_LORA), row(128), row(128)],
        out_specs=[row(Q_LORA), row(KV_LORA), row(128)],
        out_shape=[jax.ShapeDtypeStruct((t, Q_LORA), MXU_DTYPE), jax.ShapeDtypeStruct((t, KV_LORA), MXU_DTYPE),
                   jax.ShapeDtypeStruct((t, 128), MXU_DTYPE)],
        compiler_params=_params(("parallel",)),
    )(cq, ckv, kr, g_q, g_kv, cos_m, sin_m)


def _mla_q_rope(q_lin, cos_m, sin_m, *, inverse, name):
    t = q_lin.shape[0]
    tr = _tile(t, 256, 8)

    def body(q_ref, c_ref, s_ref, o_ref):
        c, s = c_ref[...], s_ref[...]
        for h in range(MLA_HEADS):
            lo = h * MLA_QK_PAD
            o_ref[:, lo:lo + MLA_NOPE] = q_ref[:, lo:lo + MLA_NOPE].astype(o_ref.dtype)
            roped = _rope64(q_ref[:, lo + MLA_NOPE:lo + MLA_QK_PAD].astype(F32), c, s, inverse=inverse)
            o_ref[:, lo + MLA_NOPE:lo + MLA_QK_PAD] = roped.astype(o_ref.dtype)

    rows = pl.BlockSpec((tr, Q_WIDTH_PAD), lambda i: (i, 0))
    tab = pl.BlockSpec((tr, 128), lambda i: (i, 0))
    return pl.pallas_call(
        body, name=name, grid=(t // tr,),
        in_specs=[rows, tab, tab], out_specs=rows, out_shape=jax.ShapeDtypeStruct((t, Q_WIDTH_PAD), MXU_DTYPE),
        compiler_params=_params(("parallel",)),
    )(q_lin, cos_m, sin_m)


_MLA_SCALE = (MLA_NOPE + MLA_ROPE) ** -0.5
_LOG2_E = 1.4426950408889634
_MLA_SCALE_LOG2 = _MLA_SCALE * _LOG2_E
_NEG = -1e30


def _mla_mask(blk):
    r = lax.broadcasted_iota(jnp.int32, (blk, blk), 0)
    c = lax.broadcasted_iota(jnp.int32, (blk, blk), 1)
    return (c // CHUNK) <= (r // CHUNK)


def _mla_fwd(q, kv, kr):
    t = q.shape[0]
    blk = _tile(t, ATTN_BLOCK)
    nb = t // blk

    hps, dq_, dv_ = HEADS_PER_STEP, MLA_QK_PAD, MLA_V

    def body(ii_ref, jj_ref, q_ref, kv_ref, kr_ref, o_ref, lse_ref, m_ref, acc_ref):
        pair = pl.program_id(1)
        i, j = ii_ref[pair], jj_ref[pair]

        @pl.when(j == 0)
        def _():
            m_ref[...] = jnp.full_like(m_ref, _NEG)
            acc_ref[...] = jnp.zeros_like(acc_ref)

        def step(masked):
            krb = kr_ref[...]
            ones = jnp.ones((blk, dv_), MXU_DTYPE)
            for h in range(hps):
                kb = jnp.concatenate([kv_ref[:, h * dq_:h * dq_ + MLA_NOPE], krb], axis=1)
                vb = jnp.concatenate([kv_ref[:, h * dq_ + MLA_NOPE:(h + 1) * dq_], ones], axis=1)
                s = lax.dot_general(q_ref[:, h * dq_:(h + 1) * dq_], kb, _NT, preferred_element_type=F32)
                if masked:
                    s = jnp.where(_mla_mask(blk), s, _NEG)
                m_prev = m_ref[h]
                m_new = jnp.maximum(m_prev, jnp.max(s, axis=-1, keepdims=True))
                alpha = jnp.exp2((m_prev - m_new) * _MLA_SCALE_LOG2)
                p = jnp.exp2((s - jnp.tile(m_new, (1, blk // 128))) * _MLA_SCALE_LOG2)
                acc_ref[h] = jnp.tile(alpha, (1, 2)) * acc_ref[h] + lax.dot_general(
                    p.astype(MXU_DTYPE), vb, _NN, preferred_element_type=F32)
                m_ref[h] = m_new

        @pl.when(j < i)
        def _():
            step(False)

        @pl.when(j == i)
        def _():
            step(True)
            for h in range(hps):
                cols = slice(h * dv_, (h + 1) * dv_)
                acc = acc_ref[h]
                row_sum = acc[:, dv_:]
                o_ref[:, cols] = (acc[:, :dv_] / row_sum).astype(o_ref.dtype)
                lse_ref[:, cols] = m_ref[h] * _MLA_SCALE + jnp.log(row_sum)

    os_ = pl.BlockSpec((blk, hps * dv_), lambda h, p, ii, jj: (ii[p], h))
    ii, jj = _causal_pairs(nb, query_major=True)
    return pl.pallas_call(
        body, name="mla_fwd",
        grid_spec=pltpu.PrefetchScalarGridSpec(
            num_scalar_prefetch=2, grid=(MLA_HEADS // hps, ii.shape[0]),
            in_specs=[pl.BlockSpec((blk, hps * dq_), lambda h, p, ii, jj: (ii[p], h)),
                      pl.BlockSpec((blk, hps * dq_), lambda h, p, ii, jj: (jj[p], h)),
                      pl.BlockSpec((blk, 128), lambda h, p, ii, jj: (jj[p], 0))],
            out_specs=[os_, os_],
            scratch_shapes=[pltpu.VMEM((hps, blk, 128), F32), pltpu.VMEM((hps, blk, 2 * dv_), F32)]),
        out_shape=[jax.ShapeDtypeStruct((t, MLA_HEADS * MLA_V), MXU_DTYPE),
                   jax.ShapeDtypeStruct((t, MLA_HEADS * 128), F32)],
        compiler_params=_params(("parallel", "arbitrary")),
    )(ii, jj, q, kv, kr)


def _mla_bwd(q, kv, kr, o, lse, dattn):
    t = q.shape[0]
    blk = _tile(t, ATTN_BLOCK)
    nb = t // blk
    hps, dq_, dv_ = HEADS_PER_STEP, MLA_QK_PAD, MLA_V
    do_off = RET_WIDTH // (hps * dv_)

    def body(ii_ref, jj_ref, q_ref, kv_ref, kr_ref, o_ref, lse_ref, do_ref, dq_ref, dk_ref, dv_ref, dk_acc, dv_acc):
        pair = pl.program_id(1)
        i, j = ii_ref[pair], jj_ref[pair]

        @pl.when(pair == 0)
        def _():
            dq_ref[...] = jnp.zeros_like(dq_ref)

        @pl.when(i == j)
        def _():
            dk_acc[...] = jnp.zeros_like(dk_acc)
            dv_acc[...] = jnp.zeros_like(dv_acc)

        def step(masked):
            krb = kr_ref[...]
            rows = pl.ds(pl.multiple_of(i * blk, blk), blk)
            for h in range(hps):
                qcols, vcols = slice(h * dq_, (h + 1) * dq_), slice(h * dv_, (h + 1) * dv_)
                qb = q_ref[:, qcols]
                kb = jnp.concatenate([kv_ref[:, h * dq_:h * dq_ + MLA_NOPE], krb], axis=1)
                vb = kv_ref[:, h * dq_ + MLA_NOPE:(h + 1) * dq_]
                dof = do_ref[:, vcols]
                dob = dof.astype(MXU_DTYPE)
                s = lax.dot_general(qb, kb, _NT, preferred_element_type=F32)
                if masked:
                    s = jnp.where(_mla_mask(blk), s, _NEG)
                lse2 = lse_ref[:, vcols] * _LOG2_E
                p = jnp.exp2(s * _MLA_SCALE_LOG2 - jnp.tile(lse2, (1, blk // 128)))
                delta = jnp.sum(dof * o_ref[:, vcols].astype(F32), axis=-1, keepdims=True)
                dv_acc[h] += lax.dot_general(p.astype(MXU_DTYPE), dob, _TN, preferred_element_type=F32)
                dp = lax.dot_general(dob, vb, _NT, preferred_element_type=F32)
                ds = (p * (dp - delta) * _MLA_SCALE).astype(MXU_DTYPE)
                dq_ref[rows, qcols] += lax.dot_general(ds, kb, _NN, preferred_element_type=F32)
                dk_acc[h] += lax.dot_general(ds, qb, _TN, preferred_element_type=F32)

        @pl.when(i > j)
        def _():
            step(False)

        @pl.when(i == j)
        def _():
            step(True)

        @pl.when(i == nb - 1)
        def _():
            for h in range(hps):
                dk_ref[:, h * dq_:(h + 1) * dq_] = dk_acc[h]
                dv_ref[:, h * dv_:(h + 1) * dv_] = dv_acc[h].astype(dv_ref.dtype)

    qmap = lambda off: (lambda h, p, ii, jj: (ii[p], h + off))
    kmap = lambda h, p, ii, jj: (jj[p], h)
    ii, jj = _causal_pairs(nb, query_major=False)
    return pl.pallas_call(
        body, name="mla_bwd",
        grid_spec=pltpu.PrefetchScalarGridSpec(
            num_scalar_prefetch=2, grid=(MLA_HEADS // hps, ii.shape[0]),
            in_specs=[pl.BlockSpec((blk, hps * dq_), qmap(0)), pl.BlockSpec((blk, hps * dq_), kmap),
                      pl.BlockSpec((blk, 128), lambda h, p, ii, jj: (jj[p], 0)),
                      pl.BlockSpec((blk, hps * dv_), qmap(0)), pl.BlockSpec((blk, hps * dv_), qmap(0)),
                      pl.BlockSpec((blk, hps * dv_), qmap(do_off))],
            out_specs=[pl.BlockSpec((t, hps * dq_), lambda h, p, ii, jj: (0, h)),
                       pl.BlockSpec((blk, hps * dq_), kmap), pl.BlockSpec((blk, hps * dv_), kmap)],
            scratch_shapes=[pltpu.VMEM((hps, blk, dq_), F32), pltpu.VMEM((hps, blk, dv_), F32)]),
        out_shape=[jax.ShapeDtypeStruct((t, Q_WIDTH_PAD), F32), jax.ShapeDtypeStruct((t, Q_WIDTH_PAD), F32),
                   jax.ShapeDtypeStruct((t, MLA_HEADS * MLA_V), MXU_DTYPE)],
        compiler_params=_params(("parallel", "arbitrary")),
    )(ii, jj, q, kv, kr, o, lse, dattn)


def _mla_kv_grad(dk, dv, cos_m, sin_m):
    t = dk.shape[0]
    tr = _tile(t, 256, 8)

    def body(dk_ref, dv_ref, c_ref, s_ref, dkv_ref, dkr_ref):
        acc = jnp.zeros((tr, 128), F32)
        for h in range(MLA_HEADS):
            dkv_ref[:, h * 256:h * 256 + 128] = dk_ref[:, h * 256:h * 256 + 128].astype(dkv_ref.dtype)
            dkv_ref[:, h * 256 + 128:h * 256 + 256] = dv_ref[:, h * 128:(h + 1) * 128].astype(dkv_ref.dtype)
            acc = acc + dk_ref[:, h * 256 + 128:h * 256 + 256]
        dkr_ref[...] = _rope64(acc, c_ref[...], s_ref[...], inverse=True).astype(dkr_ref.dtype)

    row = lambda w: pl.BlockSpec((tr, w), lambda i: (i, 0))
    return pl.pallas_call(
        body, name="mla_kv_grad", grid=(t // tr,),
        in_specs=[row(Q_WIDTH_PAD), row(MLA_HEADS * MLA_V), row(128), row(128)],
        out_specs=[row(Q_WIDTH_PAD), row(128)],
        out_shape=[jax.ShapeDtypeStruct((t, Q_WIDTH_PAD), MXU_DTYPE), jax.ShapeDtypeStruct((t, 128), MXU_DTYPE)],
        compiler_params=_params(("parallel",)),
    )(dk, dv, cos_m, sin_m)


_FFN_COLS = 256
_FFN_ROWS = 256


def _shift_down(cur, prev8, n):
    out = pltpu.roll(cur, n, 0)
    head = out[:8]
    row = lax.broadcasted_iota(jnp.int32, head.shape, 0)
    for r in range(n):
        head = jnp.where(row == r, prev8[8 - n + r:8 - n + r + 1, :], head)
    return jnp.concatenate([head, out[8:]], axis=0)


def _shift_up(cur, next8, n):
    rows = cur.shape[0]
    out = pltpu.roll(cur, rows - n, 0)
    tail = out[rows - 8:]
    row = lax.broadcasted_iota(jnp.int32, tail.shape, 0)
    for r in range(n):
        tail = jnp.where(row == 8 - n + r, next8[r:r + 1, :], tail)
    return jnp.concatenate([out[:rows - 8], tail], axis=0)


def _conv_pre(g_ref, cw_ref, cb_ref, c, rc):
    r0 = pl.multiple_of(c * rc, rc)
    cur = g_ref[pl.ds(r0, rc), :].astype(F32)
    prev16 = g_ref[pl.ds(pl.multiple_of(jnp.maximum(r0 - 16, 0), 16), 16), :].astype(F32)
    prev8 = jnp.where(c > 0, prev16[8:], 0.0)
    s1, s2 = _shift_down(cur, prev8, 1), _shift_down(cur, prev8, 2)
    a = cb_ref[...] + cw_ref[2:3, :] * cur + cw_ref[1:2, :] * s1 + cw_ref[0:1, :] * s2
    return r0, cur, s1, s2, a


def _ffn_act_fwd(gpre, u, cw, cb):
    t, f = gpre.shape
    tc = _tile(f, _FFN_COLS)
    rc = _tile(t, _FFN_ROWS, 8)

    def body(g_ref, u_ref, cw_ref, cb_ref, o_ref):
        def chunk(c, carry):
            r0, _, _, _, a = _conv_pre(g_ref, cw_ref, cb_ref, c, rc)
            o_ref[pl.ds(r0, rc), :] = (a * _sigmoid(a) * u_ref[pl.ds(r0, rc), :]).astype(o_ref.dtype)
            return carry
        lax.fori_loop(0, t // rc, chunk, 0)

    col = pl.BlockSpec((t, tc), lambda j: (0, j))
    return pl.pallas_call(
        body, name="ffn_act_fwd", grid=(f // tc,),
        in_specs=[col, col, pl.BlockSpec((CONV_WIDTH, tc), lambda j: (0, j)), pl.BlockSpec((1, tc), lambda j: (0, j))],
        out_specs=col, out_shape=jax.ShapeDtypeStruct((t, f), MXU_DTYPE),
        compiler_params=_params(("parallel",)),
    )(gpre, u, cw, cb)


def _ffn_act_bwd(gpre, u, dact, cw, cb):
    t, f = gpre.shape
    tc = _tile(f, _FFN_COLS)
    rc = _tile(t, _FFN_ROWS, 8)
    nc = t // rc

    def body(g_ref, u_ref, d_ref, cw_ref, cb_ref, dg_ref, du_ref, dcw_ref, dcb_ref, da_ref):
        def chunk(c, carry):
            w0, w1, w2, b = carry
            r0, cur, s1, s2, a = _conv_pre(g_ref, cw_ref, cb_ref, c, rc)
            sg = _sigmoid(a)
            d = d_ref[pl.ds(r0, rc), :].astype(F32)
            du_ref[pl.ds(r0, rc), :] = (d * (a * sg)).astype(du_ref.dtype)
            da = d * u_ref[pl.ds(r0, rc), :].astype(F32) * (sg * (1.0 + a * (1.0 - sg)))
            da_ref[pl.ds(r0, rc), :] = da
            return (w0 + jnp.sum(da * s2, axis=0, keepdims=True), w1 + jnp.sum(da * s1, axis=0, keepdims=True),
                    w2 + jnp.sum(da * cur, axis=0, keepdims=True), b + jnp.sum(da, axis=0, keepdims=True))
        z = jnp.zeros((1, tc), F32)
        w0, w1, w2, b = lax.fori_loop(0, nc, chunk, (z, z, z, z))
        dcw_ref[0:1, :] = w0
        dcw_ref[1:2, :] = w1
        dcw_ref[2:3, :] = w2
        dcb_ref[...] = b

        def chunk2(c, carry):
            r0 = pl.multiple_of(c * rc, rc)
            cur = da_ref[pl.ds(r0, rc), :]
            nxt = da_ref[pl.ds(pl.multiple_of(jnp.minimum(r0 + rc, t - 8), 8), 8), :]
            nxt = jnp.where(c < nc - 1, nxt, 0.0)
            dg = cw_ref[2:3, :] * cur + cw_ref[1:2, :] * _shift_up(cur, nxt, 1) + cw_ref[0:1, :] * _shift_up(cur, nxt, 2)
            dg_ref[pl.ds(r0, rc), :] = dg.astype(dg_ref.dtype)
            return carry
        lax.fori_loop(0, nc, chunk2, 0)

    col = pl.BlockSpec((t, tc), lambda j: (0, j))
    w3 = pl.BlockSpec((CONV_WIDTH, tc), lambda j: (0, j))
    w1 = pl.BlockSpec((1, tc), lambda j: (0, j))
    low = jax.ShapeDtypeStruct((t, f), MXU_DTYPE)
    return pl.pallas_call(
        body, name="ffn_act_bwd", grid=(f // tc,),
        in_specs=[col, col, col, w3, w1], out_specs=[col, col, w3, w1],
        out_shape=[low, low, jax.ShapeDtypeStruct((CONV_WIDTH, f), F32), jax.ShapeDtypeStruct((1, f), F32)],
        scratch_shapes=[pltpu.VMEM((t, tc), F32)],
        compiler_params=_params(("parallel",)),
    )(gpre, u, dact, cw, cb)


def _head_fwd_bwd(h2, glin, pp, target, g_final):
    t, d = h2.shape
    tr = _tile(t, 128, 8)

    def body(h_ref, gl_ref, pp_ref, t_ref, g_ref, loss_ref, dh_ref, dgl_ref, dpp_ref, dg_ref):
        gate = _sigmoid(gl_ref[...])
        ppv = pp_ref[...]
        h3 = h_ref[...] + gate * ppv
        r = lax.rsqrt(jnp.mean(h3 * h3, axis=-1, keepdims=True) + EPS)
        yh = h3 * r
        g = g_ref[...]
        diff = yh * g - t_ref[...]
        lpart = 0.5 * jnp.sum(jnp.mean(diff * diff, axis=-1, keepdims=True), axis=0, keepdims=True)
        dy = diff * (1.0 / d)
        dyg = dy * g
        dh3 = r * dyg - h3 * (r * r * r) * jnp.mean(dyg * h3, axis=-1, keepdims=True)
        dh_ref[...] = dh3
        dgl_ref[...] = (dh3 * ppv * gate * (1.0 - gate)).astype(dgl_ref.dtype)
        dpp_ref[...] = (dh3 * gate).astype(dpp_ref.dtype)
        dgp = jnp.sum(dy * yh, axis=0, keepdims=True)

        @pl.when(pl.program_id(0) == 0)
        def _():
            loss_ref[...] = jnp.broadcast_to(lpart, loss_ref.shape)
            dg_ref[...] = dgp

        @pl.when(pl.program_id(0) > 0)
        def _():
            loss_ref[...] += jnp.broadcast_to(lpart, loss_ref.shape)
            dg_ref[...] += dgp

    row = pl.BlockSpec((tr, d), lambda i: (i, 0))
    vec = pl.BlockSpec((1, d), lambda i: (0, 0))
    low = jax.ShapeDtypeStruct((t, d), MXU_DTYPE)
    return pl.pallas_call(
        body, name="head_fwd_bwd", grid=(t // tr,),
        in_specs=[row, row, row, row, vec],
        out_specs=[pl.BlockSpec((8, 128), lambda i: (0, 0)), row, row, row, vec],
        out_shape=[jax.ShapeDtypeStruct((8, 128), F32), jax.ShapeDtypeStruct((t, d), F32), low, low,
                   jax.ShapeDtypeStruct((1, d), F32)],
        compiler_params=_params(("arbitrary",)),
    )(h2, glin, pp, target, g_final)


class _Order:
    def __init__(self):
        self.last = None

    def tie(self, x):
        return x if self.last is None else lax.optimization_barrier((x, self.last))[0]

    def run(self, fn, first, *args, **kwargs):
        out = fn(self.tie(first), *args, **kwargs)
        self.last = out[0] if isinstance(out, (tuple, list)) else out
        return out


def _local_step(x, p, target, vec, ops):
    t = x.shape[0]
    cos_r, sin_r, cos_m, sin_m = _rope_tables(t)
    lg = _ret_log_gamma()
    low = MXU_DTYPE
    run = ops.order.run
    w = ops.weight

    ops.start_gather("w_in")
    hn1 = run(_rms_fwd, x, vec["g_attn"], name="rms1_fwd")
    for n in ("w_uq", "w_ukv", "w_o"):
        ops.start_gather(n, after=hn1)
    proj = run(_matmul, hn1, w("w_in"), tb=True, name="mm_proj")
    ops.start_gather("w_ffn_gate", after=proj)
    rq, rk, rv = run(_ret_prep, proj, cos_r, sin_r)
    ops.start_gather("w_ffn_up", after=rq)
    c0 = 4 * RET_WIDTH
    cq = proj[:, c0:c0 + Q_LORA]
    ckv = proj[:, c0 + Q_LORA:c0 + Q_LORA + KV_LORA]
    kr_in = proj[:, c0 + Q_LORA + KV_LORA:c0 + Q_LORA + KV_LORA + 128]
    cqn, kvn, kr = run(_mla_prep, cq, ckv, kr_in, vec["g_q_lora"], vec["g_kv_lora"], cos_m, sin_m)
    ops.start_gather("w_ffn_down", after=cqn)
    q_lin = run(_matmul, cqn, w("w_uq"), tb=True, name="mm_q")
    q = run(_mla_q_rope, q_lin, cos_m, sin_m, inverse=False, name="mla_q_rope")
    kv = run(_matmul, kvn, w("w_ukv"), tb=True, name="mm_kv", out_dtype=low)
    mo, lse = run(_mla_fwd, q, kv, kr)
    ops.start_gather("w_ple_gate", after=mo)
    ops.start_gather("w_ple_proj", after=mo)
    ret_raw, ro = run(_ret_fwd, rq, rk, rv, proj, lg)
    attn = jnp.concatenate([ro, mo], axis=1)
    h1 = run(_matmul, attn, w("w_o"), name="mm_o", add=x)
    hn2 = run(_rms_fwd, h1, vec["g_ffn"], name="rms2_fwd")
    gpre = run(_matmul, hn2, w("w_ffn_gate"), tb=True, name="mm_gate", out_dtype=low)
    u = run(_matmul, hn2, w("w_ffn_up"), tb=True, name="mm_up", out_dtype=low)
    act = run(_ffn_act_fwd, gpre, u, vec["conv_w"], vec["conv_b"])
    h2 = run(_matmul, act, w("w_ffn_down"), name="mm_down", add=h1)
    hn3 = run(_rms_fwd, h2, vec["g_ple"], name="rms3_fwd")
    glin = run(_matmul, hn3, w("w_ple_gate"), name="mm_ple_gate")
    p_low = p.astype(low)
    pp = run(_matmul, p_low, w("w_ple_proj"), tb=True, name="mm_ple_proj")
    loss_part, dh3, dglin, dpp, dg_final = run(_head_fwd_bwd, h2, glin, pp, target, vec["g_final"])

    ops.grad("w_ple_proj", run(_matmul, dpp, p_low, ta=True, name="mm_d_ple_proj", out_dtype=low))
    ops.grad("w_ple_gate", run(_matmul, hn3, dglin, ta=True, name="mm_d_ple_gate", out_dtype=low))
    dhn3 = run(_matmul, dglin, w("w_ple_gate"), tb=True, name="mm_dhn3", out_dtype=low)
    dh2, dh2_low, dg_ple = run(_rms_bwd, h2, dhn3, vec["g_ple"], dh3, name="rms3_bwd", low_copy=True)
    ops.reduce_add("w_ple_proj")
    ops.reduce_add("w_ple_gate")
    ops.grad("w_ffn_down", run(_matmul, act, dh2_low, ta=True, name="mm_d_down", out_dtype=low))
    dact = run(_matmul, dh2_low, w("w_ffn_down"), tb=True, name="mm_dact", out_dtype=low)
    ops.reduce_add("w_ffn_down")
    dgpre, du, dconv_w, dconv_b = run(_ffn_act_bwd, gpre, u, dact, vec["conv_w"], vec["conv_b"])
    ops.update("w_ple_proj")
    ops.update("w_ple_gate")
    ops.grad("w_ffn_gate", run(_matmul, dgpre, hn2, ta=True, name="mm_d_gate", out_dtype=low))
    ops.grad("w_ffn_up", run(_matmul, du, hn2, ta=True, name="mm_d_up", out_dtype=low))
    ops.reduce_add("w_ffn_gate")
    dhn2 = run(_matmul, dgpre, w("w_ffn_gate"), name="mm_dhn2_a")
    ops.reduce_add("w_ffn_up")
    dhn2 = run(_matmul, du, w("w_ffn_up"), name="mm_dhn2_b", add=dhn2, out_dtype=low)
    dh1, dh1_low, dg_ffn = run(_rms_bwd, h1, dhn2, vec["g_ffn"], dh2, name="rms2_bwd", low_copy=True)
    ops.update("w_ffn_down")
    ops.grad("w_o", run(_matmul, attn, dh1_low, ta=True, name="mm_d_o", out_dtype=low))
    dattn = run(_matmul, dh1_low, w("w_o"), tb=True, name="mm_dattn")
    ops.reduce_add("w_o")

    dq_r, dk_full, dv = run(_mla_bwd, q, kv, kr, mo, lse, dattn)
    ops.update("w_ffn_gate")
    dq_lin = run(_mla_q_rope, dq_r, cos_m, sin_m, inverse=True, name="mla_q_unrope")
    dkv, dkr = run(_mla_kv_grad, dk_full, dv, cos_m, sin_m)
    ops.grad("w_uq", run(_matmul, dq_lin, cqn, ta=True, name="mm_d_uq", out_dtype=low))
    dcqn = run(_matmul, dq_lin, w("w_uq"), name="mm_dcqn")
    ops.grad("w_ukv", run(_matmul, dkv, kvn, ta=True, name="mm_d_ukv", out_dtype=low))
    dkvn = run(_matmul, dkv, w("w_ukv"), name="mm_dkvn")
    dcq, dcq_low, dg_q = run(_rms_bwd, cq, dcqn, vec["g_q_lora"], None, name="rmsq_bwd", low_copy=True)
    dckv, dckv_low, dg_kv = run(_rms_bwd, ckv, dkvn, vec["g_kv_lora"], None, name="rmskv_bwd", low_copy=True)
    ops.reduce_add("w_uq")
    ops.reduce_add("w_ukv")

    do_ret, drg = run(_ret_gate_bwd, ret_raw, proj, dattn)
    dq_ret, dk_ret, drv = run(_ret_bwd, rq, rk, rv, do_ret, lg)
    drq, drk = run(_ret_unrope, dq_ret, dk_ret, cos_r, sin_r)

    pad = jnp.zeros((t, IN_WIDTH_PAD - IN_WIDTH - 64), low)
    dproj = jnp.concatenate([drq, drk, drv, drg, dcq_low, dckv_low, dkr, pad], axis=1)
    ops.grad("w_in", run(_matmul, dproj, hn1, ta=True, name="mm_d_in", out_dtype=low))
    for n in ("w_ffn_up", "w_o", "w_uq", "w_ukv"):
        ops.update(n)
    ops.reduce_add("w_in")
    dhn1 = run(_matmul, dproj, w("w_in"), name="mm_dhn1", out_dtype=low)
    grad_x, dg_attn = run(_rms_bwd, x, dhn1, vec["g_attn"], dh1, name="rms1_bwd")

    gs = {"g_attn": dg_attn, "g_q_lora": dg_q, "g_kv_lora": dg_kv, "g_ffn": dg_ffn, "conv_w": dconv_w,
          "conv_b": dconv_b, "g_ple": dg_ple, "g_final": dg_final}
    return loss_part, grad_x, gs


_COL_SHARDED = ("w_in", "w_uq", "w_ukv", "w_ffn_gate", "w_ffn_up", "w_ple_proj")
_FFN_SHARD = D_FF // N_DEV
_FFN_SHARD_PAD = D_FF_PAD // N_DEV
_HEADS_PER_SHARD = MLA_HEADS // N_DEV
_QK = MLA_NOPE + MLA_ROPE


def _pad_rows(name, a):
    lead = a.shape[:-2]
    if name == "w_uq":
        a = a.reshape(lead + (_HEADS_PER_SHARD, _QK, a.shape[-1]))
        a = jnp.pad(a, [(0, 0)] * len(lead) + [(0, 0), (0, MLA_QK_PAD - _QK), (0, 0)])
        return a.reshape(lead + (_HEADS_PER_SHARD * MLA_QK_PAD, a.shape[-1]))
    if name in ("w_ffn_gate", "w_ffn_up", "w_ffn_down"):
        return jnp.pad(a, [(0, 0)] * len(lead) + [(0, _FFN_SHARD_PAD - _FFN_SHARD), (0, 0)])
    return a


def _unpad_rows(name, a):
    lead = a.shape[:-2]
    if name == "w_uq":
        a = a.reshape(lead + (_HEADS_PER_SHARD, MLA_QK_PAD, a.shape[-1]))[..., :_QK, :]
        return a.reshape(lead + (_HEADS_PER_SHARD * _QK, a.shape[-1]))
    if name in ("w_ffn_gate", "w_ffn_up", "w_ffn_down"):
        return a[..., :_FFN_SHARD, :]
    return a


def _rows_view(name, a):
    return jnp.swapaxes(a, 0, 1) if name in _COL_SHARDED else a


def _shard_payload(name, shard):
    return _pad_rows(name, _rows_view(name, shard).astype(MXU_DTYPE))


def _full_from_gathered(name, g):
    full = g.reshape(g.shape[0] * g.shape[1], g.shape[2])
    if name == "w_in":
        full = jnp.pad(full, ((0, IN_WIDTH_PAD - IN_WIDTH), (0, 0)))
    return full


def _grad_chunks(name, gfull):
    if name == "w_in":
        gfull = gfull[:IN_WIDTH]
    return gfull.reshape(N_DEV, gfull.shape[0] // N_DEV, gfull.shape[1])


def _ffn_vec_layout(a):
    a = a.reshape(a.shape[0], N_DEV, _FFN_SHARD)
    return jnp.pad(a, ((0, 0), (0, 0), (0, _FFN_SHARD_PAD - _FFN_SHARD))).reshape(a.shape[0], D_FF_PAD)


def _ffn_vec_shards(a):
    return a.reshape(a.shape[0], N_DEV, _FFN_SHARD_PAD)[:, :, :_FFN_SHARD]


_MESH = pl.DeviceIdType.MESH
_ANY = pl.BlockSpec(memory_space=pl.ANY)


def _place():
    x, y, c = lax.axis_index("x"), lax.axis_index("y"), lax.axis_index("c")
    chips = [(1 - x, y), (x, 1 - y), (1 - x, 1 - y)]
    return x, y, c, chips


def _handshake(peers):
    barrier = pltpu.get_barrier_semaphore()
    for peer in peers:
        pl.semaphore_signal(barrier, inc=1, device_id=peer, device_id_type=_MESH)
    pl.semaphore_wait(barrier, len(peers))


_SEQUENCER = dict(axis_name="seq", num_cores=1)
_AG_COLLECTIVE_ID = 1
_RS_SIBLING_COLLECTIVE_ID = 2
_RS_CHIPS_COLLECTIVE_ID = 3


def _all_gather_seq(shard, *, name):
    def body(x_ref, out_ref, send_sems, recv_sems, local_sem):
        x, y, c, chips = _place()
        sibling = (x, y, 1 - c)
        _handshake([sibling] + [(*chip, c) for chip in chips])

        def slot(px, py, pc):
            return out_ref.at[4 * px + 2 * py + pc]

        def copy(k, block, to, src=None):
            return pltpu.make_async_remote_copy(
                src_ref=slot(*block) if src is None else src, dst_ref=slot(*block),
                send_sem=send_sems.at[k], recv_sem=recv_sems.at[k], device_id=to, device_id_type=_MESH)

        mine = pltpu.make_async_copy(x_ref, slot(x, y, c), local_sem)
        mine.start()
        first = [copy(0, (x, y, c), sibling, src=x_ref)]
        first += [copy(1 + j, (x, y, c), (*chip, c), src=x_ref) for j, chip in enumerate(chips)]
        for cp in first:
            cp.start()
        passed = [copy(4 + j, (*chip, c), sibling) for j, chip in enumerate(chips)]
        for j, chip in enumerate(chips):
            copy(1 + j, (*chip, c), (x, y, c)).wait_recv()
            passed[j].start()
        copy(0, sibling, (x, y, c)).wait_recv()
        for j, chip in enumerate(chips):
            copy(4 + j, (*chip, 1 - c), (x, y, c)).wait_recv()
        for cp in first + passed:
            cp.wait_send()
        mine.wait()

    return pl.kernel(
        body, out_type=jax.ShapeDtypeStruct((N_DEV,) + shard.shape, shard.dtype),
        mesh=plsc.ScalarSubcoreMesh(**_SEQUENCER), name=name,
        scratch_types=[pltpu.SemaphoreType.DMA((7,)), pltpu.SemaphoreType.DMA((7,)), pltpu.SemaphoreType.DMA(())],
        compiler_params=pltpu.CompilerParams(collective_id=_AG_COLLECTIVE_ID),
    )(shard)


def _exchange_sibling(g, *, name):
    def body(g_ref, out_ref, send_sems, recv_sems):
        x, y, c, _ = _place()
        sibling = (x, y, 1 - c)
        _handshake([sibling])
        copies = []
        for chip in range(4):
            cp = pltpu.make_async_remote_copy(
                src_ref=g_ref.at[2 * chip + (1 - c)], dst_ref=out_ref.at[chip],
                send_sem=send_sems.at[chip], recv_sem=recv_sems.at[chip], device_id=sibling, device_id_type=_MESH)
            cp.start()
            copies.append(cp)
        for cp in copies:
            cp.wait_recv()
        for cp in copies:
            cp.wait_send()

    return pl.kernel(
        body, out_type=jax.ShapeDtypeStruct((4,) + g.shape[1:], g.dtype),
        mesh=plsc.ScalarSubcoreMesh(**_SEQUENCER), name=name,
        scratch_types=[pltpu.SemaphoreType.DMA((4,)), pltpu.SemaphoreType.DMA((4,))],
        compiler_params=pltpu.CompilerParams(collective_id=_RS_SIBLING_COLLECTIVE_ID),
    )(g)


def _add_sibling(g, recv, *, name):
    _, r, cdim = g.shape
    tr, tc = _tile_2d(r, cdim, 6)
    g4 = g.reshape(4, 2, r, cdim)
    core = lax.axis_index("c").astype(jnp.int32).reshape(1)

    def body(c_ref, g_ref, r_ref, o_ref):
        o_ref[...] = (g_ref[...].astype(F32) + r_ref[...].astype(F32)).astype(o_ref.dtype)

    return pl.pallas_call(
        body, name=name,
        grid_spec=pltpu.PrefetchScalarGridSpec(
            num_scalar_prefetch=1, grid=(4, r // tr, cdim // tc),
            in_specs=[pl.BlockSpec((None, None, tr, tc), lambda ch, i, j, c_ref: (ch, c_ref[0], i, j)),
                      pl.BlockSpec((None, tr, tc), lambda ch, i, j, c_ref: (ch, i, j))],
            out_specs=pl.BlockSpec((None, tr, tc), lambda ch, i, j, c_ref: (ch, i, j))),
        out_shape=jax.ShapeDtypeStruct((4, r, cdim), g.dtype),
        compiler_params=_params(("parallel", "parallel", "parallel")),
    )(core, g4, recv)


def _exchange_chips(pch, *, name):
    def body(p_ref, out_ref, send_sems, recv_sems, local_sem):
        x, y, c, chips = _place()
        _handshake([(*chip, c) for chip in chips])
        me = 2 * x + y
        mine = pltpu.make_async_copy(p_ref.at[me], out_ref.at[me], local_sem)
        mine.start()
        copies = []
        for j, (px, py) in enumerate(chips):
            cp = pltpu.make_async_remote_copy(
                src_ref=p_ref.at[2 * px + py], dst_ref=out_ref.at[me],
                send_sem=send_sems.at[j], recv_sem=recv_sems.at[j], device_id=(px, py, c), device_id_type=_MESH)
            cp.start()
            copies.append(cp)
        for j, (px, py) in enumerate(chips):
            pltpu.make_async_remote_copy(
                src_ref=p_ref.at[me], dst_ref=out_ref.at[2 * px + py],
                send_sem=send_sems.at[j], recv_sem=recv_sems.at[j], device_id=(px, py, c), device_id_type=_MESH).wait_recv()
        for cp in copies:
            cp.wait_send()
        mine.wait()

    return pl.kernel(
        body, out_type=jax.ShapeDtypeStruct(pch.shape, pch.dtype),
        mesh=plsc.ScalarSubcoreMesh(**_SEQUENCER), name=name,
        scratch_types=[pltpu.SemaphoreType.DMA((3,)), pltpu.SemaphoreType.DMA((3,)), pltpu.SemaphoreType.DMA(())],
        compiler_params=pltpu.CompilerParams(collective_id=_RS_CHIPS_COLLECTIVE_ID),
    )(pch)


def _all_reduce_small(v, *, name):
    r = v.shape[0]

    def body(x_ref, out_ref, buf_ref, send_sems, recv_sems):
        x, y, c, chips = _place()
        sibling = (x, y, 1 - c)

        def slot(px, py, pc):
            return buf_ref.at[4 * px + 2 * py + pc]

        def copy(k, block, to, src=None):
            return pltpu.make_async_remote_copy(
                src_ref=slot(*block) if src is None else src, dst_ref=slot(*block),
                send_sem=send_sems.at[k], recv_sem=recv_sems.at[k], device_id=to, device_id_type=_MESH)

        first = [copy(0, (x, y, c), sibling, src=x_ref)]
        first += [copy(1 + j, (x, y, c), (*chip, c), src=x_ref) for j, chip in enumerate(chips)]
        for cp in first:
            cp.start()
        buf_ref[4 * x + 2 * y + c] = x_ref[...]
        passed = [copy(4 + j, (*chip, c), sibling) for j, chip in enumerate(chips)]
        for j, chip in enumerate(chips):
            copy(1 + j, (*chip, c), (x, y, c)).wait_recv()
            passed[j].start()
        copy(0, sibling, (x, y, c)).wait_recv()
        for j, chip in enumerate(chips):
            copy(4 + j, (*chip, 1 - c), (x, y, c)).wait_recv()
        for cp in first + passed:
            cp.wait_send()
        total = buf_ref[0]
        for k in range(1, N_DEV):
            total = total + buf_ref[k]
        out_ref[...] = total

    vm = pl.BlockSpec(memory_space=pltpu.VMEM)
    return pl.pallas_call(
        body, name=name, out_shape=jax.ShapeDtypeStruct(v.shape, v.dtype),
        in_specs=[vm], out_specs=vm,
        scratch_shapes=[pltpu.VMEM((N_DEV,) + v.shape, v.dtype), pltpu.SemaphoreType.DMA((7,)),
                        pltpu.SemaphoreType.DMA((7,))],
    )(v)


_ELEMENTWISE_VMEM = 24 * 1024 * 1024


def _tile_2d(r, c, n_arrays):
    per_block = _ELEMENTWISE_VMEM // (8 * n_arrays)
    tr = _tile(r, max(16, per_block // max(c, 128)), 16)
    if tr * c <= per_block:
        return tr, c
    return r, _tile(c, max(128, (per_block // r) // 128 * 128))


def _adam_math(w, g, m, v):
    m = ADAM_B1 * m + (1.0 - ADAM_B1) * g
    v = ADAM_B2 * v + (1.0 - ADAM_B2) * jnp.square(g)
    m_hat = m / (1.0 - ADAM_B1 ** ADAM_STEP)
    v_hat = v / (1.0 - ADAM_B2 ** ADAM_STEP)
    delta = -ADAM_LR * (m_hat / (jnp.sqrt(v_hat) + ADAM_EPS) + ADAM_WD * w)
    return delta, m, v


def _adam(w, g, m, v, *, name, parts=None):
    r, cdim = w.shape
    tr, tc = _tile_2d(r, cdim, 8)

    def body(w_ref, g_ref, m_ref, v_ref, go_ref, d_ref, mo_ref, vo_ref):
        if parts is None:
            g = g_ref[...]
        else:
            g = g_ref[0].astype(F32)
            for k in range(1, parts):
                g = g + g_ref[k].astype(F32)
        delta, m, v = _adam_math(w_ref[...], g, m_ref[...], v_ref[...])
        go_ref[...] = g
        d_ref[...] = delta
        mo_ref[...] = m
        vo_ref[...] = v

    blk = pl.BlockSpec((tr, tc), lambda i, j: (i, j))
    gblk = blk if parts is None else pl.BlockSpec((parts, tr, tc), lambda i, j: (0, i, j))
    out = jax.ShapeDtypeStruct((r, cdim), F32)
    return pl.pallas_call(
        body, name=name, grid=(r // tr, cdim // tc), in_specs=[blk, gblk, blk, blk], out_specs=[blk] * 4,
        out_shape=[out] * 4, compiler_params=_params(("parallel", "parallel")),
    )(w, g, m, v)


_BIG = ("w_in", "w_uq", "w_ukv", "w_o", "w_ffn_gate", "w_ffn_up", "w_ffn_down", "w_ple_gate", "w_ple_proj")
_WEIGHTS = ("w_in", "g_attn", "g_q_lora", "g_kv_lora", "w_uq", "w_ukv", "w_o", "g_ffn", "w_ffn_gate", "w_ffn_up",
            "conv_w", "conv_b", "w_ffn_down", "g_ple", "w_ple_gate", "w_ple_proj", "g_final")
_SMALL_PACK = (("g_attn", 1, D_MODEL), ("g_q_lora", 1, Q_LORA), ("g_kv_lora", 1, KV_LORA), ("g_ffn", 1, D_MODEL),
               ("conv_w", CONV_WIDTH, D_FF_PAD), ("conv_b", 1, D_FF_PAD), ("g_ple", 1, D_MODEL), ("g_final", 1, D_MODEL))


def _pack_small(gs):
    flat = jnp.concatenate([gs[n].reshape(-1) for n, _, _ in _SMALL_PACK])
    rows = -(-flat.shape[0] // 128)
    rows = -(-rows // 8) * 8
    return jnp.pad(flat, (0, rows * 128 - flat.shape[0])).reshape(rows, 128)


def _unpack_small(packed):
    flat = packed.reshape(-1)
    out, off = {}, 0
    for n, r, c in _SMALL_PACK:
        out[n] = flat[off:off + r * c].reshape(r, c)
        off += r * c
    return out


def kernel(x, p, w_in, g_attn, g_q_lora, g_kv_lora, w_uq, w_ukv, w_o, g_ffn, w_ffn_gate, w_ffn_up, conv_w, conv_b, w_ffn_down, g_ple, w_ple_gate, w_ple_proj, g_final, loss_target, m_w_in, m_g_attn, m_g_q_lora, m_g_kv_lora, m_w_uq, m_w_ukv, m_w_o, m_g_ffn, m_w_ffn_gate, m_w_ffn_up, m_conv_w, m_conv_b, m_w_ffn_down, m_g_ple, m_w_ple_gate, m_w_ple_proj, m_g_final, v_w_in, v_g_attn, v_g_q_lora, v_g_kv_lora, v_w_uq, v_w_ukv, v_w_o, v_g_ffn, v_w_ffn_gate, v_w_ffn_up, v_conv_w, v_conv_b, v_w_ffn_down, v_g_ple, v_w_ple_gate, v_w_ple_proj, v_g_final):
    given = dict(locals())
    wts = {n: given[n] for n in _WEIGHTS}
    mom = {n: given["m_" + n] for n in _WEIGHTS}
    var = {n: given["v_" + n] for n in _WEIGHTS}
    me = (4 * lax.axis_index("x") + 2 * lax.axis_index("y") + lax.axis_index("c")).astype(jnp.int32)
    ops = _ShardedWeights(wts, mom, var)

    conv_full = _all_gather_seq(conv_w, name="ag_conv_w")[:, 0].transpose(1, 0, 2).reshape(CONV_WIDTH, D_FF)
    vec = {"g_attn": g_attn, "g_q_lora": g_q_lora, "g_kv_lora": g_kv_lora, "g_ffn": g_ffn, "g_ple": g_ple,
           "g_final": g_final[None, :], "conv_w": _ffn_vec_layout(conv_full), "conv_b": _ffn_vec_layout(conv_b)}

    loss_part, grad_x, gs = _local_step(x[0], p[0, 0], loss_target[0], vec, ops)
    loss = lax.psum(loss_part[0, 0], ("x", "y", "c"))

    small = _unpack_small(_all_reduce_small(ops.order.tie(_pack_small(gs)), name="ar_small"))
    conv_w_shards = _ffn_vec_shards(small["conv_w"])
    small_g = {
        "g_attn": small["g_attn"], "g_q_lora": small["g_q_lora"], "g_kv_lora": small["g_kv_lora"],
        "g_ffn": small["g_ffn"], "g_ple": small["g_ple"], "g_final": small["g_final"],
        "conv_b": _ffn_vec_shards(small["conv_b"]).reshape(1, D_FF),
        "conv_w": lax.dynamic_index_in_dim(conv_w_shards, me, axis=1, keepdims=False),
    }
    results = dict(ops.results)
    for n, g in small_g.items():
        shape = wts[n].shape
        outs = ops.order.run(_adam, wts[n].reshape(g.shape), g, mom[n].reshape(g.shape), var[n].reshape(g.shape),
                             name="adam_" + n)
        results[n] = tuple(a.reshape(shape) for a in outs)
    ops.update("w_in")
    results["w_in"] = ops.results["w_in"]

    return (loss, grad_x[None], *[results[n][0] for n in _WEIGHTS], *[results[n][1] for n in _WEIGHTS],
            *[results[n][2] for n in _WEIGHTS], *[results[n][3] for n in _WEIGHTS])


class _ShardedWeights:
    def __init__(self, wts, mom, var):
        self.wts, self.mom, self.var = wts, mom, var
        self.order = _Order()
        self.full, self.stage, self.results = {}, {}, {}

    def start_gather(self, name, after=None):
        payload = _shard_payload(name, self.wts[name][0])
        if after is not None:
            payload = lax.optimization_barrier((payload, after))[0]
        self.full[name] = _full_from_gathered(name, _all_gather_seq(payload, name="ag_" + name))

    def weight(self, name):
        return self.full[name]

    def grad(self, name, gfull):
        chunks = _grad_chunks(name, gfull)
        self.stage[name] = (chunks, _exchange_sibling(chunks, name="rs_sib_" + name))

    def reduce_add(self, name):
        chunks, from_sibling = self.stage[name]
        per_chip = self.order.run(_add_sibling, chunks, from_sibling, name="rs_add_" + name)
        self.stage[name] = _exchange_chips(per_chip, name="rs_chip_" + name)

    def update(self, name):
        parts = self.stage[name]
        if name == "w_uq":
            parts = _unpad_rows(name, parts)
        rows = lambda a: _rows_view(name, a[0])
        outs = self.order.run(_adam, rows(self.wts[name]), parts, rows(self.mom[name]), rows(self.var[name]),
                              name="adam_" + name, parts=4)
        self.results[name] = tuple(_rows_view(name, a)[None] for a in outs)
```

```python
import functools

import numpy as np

import jax
import jax.numpy as jnp
from jax import lax
from jax.experimental import pallas as pl
from jax.experimental.pallas import tpu as pltpu
from jax.experimental.pallas import tpu_sc as plsc

D_MODEL = 4096
CHUNK = 64
PLE_DIM = 256
RET_HEADS = 8
RET_HEAD_DIM = 256
RET_WIDTH = 2048
MLA_HEADS = 16
MLA_NOPE = 128
MLA_ROPE = 64
MLA_V = 128
Q_LORA = 1024
KV_LORA = 512
D_FF = 11008
CONV_WIDTH = 3
ROPE_BASE = 10000.0
EPS = 1e-6
IN_WIDTH = 9792
ADAM_LR, ADAM_B1, ADAM_B2, ADAM_EPS, ADAM_WD, ADAM_STEP = 0.001, 0.9, 0.999, 1e-08, 0.01, 10

IN_WIDTH_PAD = 10240
D_FF_PAD = 11264
MLA_QK_PAD = 256
Q_WIDTH_PAD = MLA_HEADS * MLA_QK_PAD

N_DEV = 8
MXU_DTYPE = jnp.bfloat16
ATTN_BLOCK = 512
HEADS_PER_STEP = 2
VMEM_LIMIT = 56 * 1024 * 1024

F32 = jnp.float32


def _tile(n, want, align=128):
    if n <= want:
        return n
    t = (want // align) * align
    while t >= align:
        if n % t == 0:
            return t
        t -= align
    return n


def _params(sem):
    return pltpu.CompilerParams(dimension_semantics=sem, vmem_limit_bytes=VMEM_LIMIT)


def _sigmoid(x):
    return 1.0 / (1.0 + jnp.exp(-x))


def _matmul(a, b, *, name, ta=False, tb=False, out_dtype=F32, add=None, tm=1024, tn=1024, tk=4096):
    m, k = (a.shape[1], a.shape[0]) if ta else a.shape
    k2, n = (b.shape[1], b.shape[0]) if tb else b.shape
    assert k == k2, (a.shape, b.shape, ta, tb)
    tm, tn, tk = _tile(m, tm), _tile(n, tn), _tile(k, tk)
    nk = k // tk
    dims = (((0 if ta else 1,), (1 if tb else 0,)), ((), ()))

    def body(*refs):
        a_ref, b_ref, o_ref = refs[0], refs[1], refs[3 if add is not None else 2]
        c_ref = refs[2] if add is not None else None
        part = lax.dot_general(a_ref[...].astype(MXU_DTYPE), b_ref[...].astype(MXU_DTYPE), dims,
                               preferred_element_type=F32)

        def finish(r):
            if c_ref is not None:
                r = r + c_ref[...].astype(F32)
            o_ref[...] = r.astype(out_dtype)

        if nk == 1:
            finish(part)
            return
        acc_ref = refs[-1]
        kk = pl.program_id(2)

        @pl.when(kk == 0)
        def _():
            acc_ref[...] = part

        @pl.when((kk > 0) & (kk < nk - 1))
        def _():
            acc_ref[...] += part

        @pl.when(kk == nk - 1)
        def _():
            finish(acc_ref[...] + part)

    a_spec = pl.BlockSpec((tk, tm), lambda i, j, kk: (kk, i)) if ta else pl.BlockSpec((tm, tk), lambda i, j, kk: (i, kk))
    b_spec = pl.BlockSpec((tn, tk), lambda i, j, kk: (j, kk)) if tb else pl.BlockSpec((tk, tn), lambda i, j, kk: (kk, j))
    in_specs = [a_spec, b_spec]
    args = [a, b]
    if add is not None:
        in_specs.append(pl.BlockSpec((tm, tn), lambda i, j, kk: (i, j)))
        args.append(add)
    return pl.pallas_call(
        body, name=name, grid=(m // tm, n // tn, nk),
        in_specs=in_specs, out_specs=pl.BlockSpec((tm, tn), lambda i, j, kk: (i, j)),
        out_shape=jax.ShapeDtypeStruct((m, n), out_dtype),
        scratch_shapes=[] if nk == 1 else [pltpu.VMEM((tm, tn), F32)],
        compiler_params=_params(("parallel", "parallel", "arbitrary")),
    )(*args)


def _rms_fwd(x, g, *, name):
    t, d = x.shape
    tr = _tile(t, 256, 8)

    def body(x_ref, g_ref, o_ref):
        xf = x_ref[...]
        r = lax.rsqrt(jnp.mean(xf * xf, axis=-1, keepdims=True) + EPS)
        o_ref[...] = (xf * r * g_ref[...]).astype(o_ref.dtype)

    return pl.pallas_call(
        body, name=name, grid=(t // tr,),
        in_specs=[pl.BlockSpec((tr, d), lambda i: (i, 0)), pl.BlockSpec((1, d), lambda i: (0, 0))],
        out_specs=pl.BlockSpec((tr, d), lambda i: (i, 0)),
        out_shape=jax.ShapeDtypeStruct((t, d), MXU_DTYPE),
        compiler_params=_params(("parallel",)),
    )(x, g)


def _rms_bwd(x, dhn, g, res, *, name, low_copy=False):
    t, d = x.shape
    tr = _tile(t, 256, 8)

    def body(*refs):
        if res is None:
            x_ref, dh_ref, g_ref = refs[:3]
            outs = refs[3:]
            res_ref = None
        else:
            x_ref, dh_ref, g_ref, res_ref = refs[:4]
            outs = refs[4:]
        dx_ref, dg_ref = outs[0], outs[-1]
        xf = x_ref[...]
        dh = dh_ref[...].astype(F32)
        r = lax.rsqrt(jnp.mean(xf * xf, axis=-1, keepdims=True) + EPS)
        dyg = dh * g_ref[...]
        dx = r * dyg - xf * (r * r * r) * jnp.mean(dyg * xf, axis=-1, keepdims=True)
        if res_ref is not None:
            dx = dx + res_ref[...]
        dx_ref[...] = dx
        if low_copy:
            outs[1][...] = dx.astype(outs[1].dtype)
        part = jnp.sum(dh * xf * r, axis=0, keepdims=True)

        @pl.when(pl.program_id(0) == 0)
        def _():
            dg_ref[...] = part

        @pl.when(pl.program_id(0) > 0)
        def _():
            dg_ref[...] += part

    row = pl.BlockSpec((tr, d), lambda i: (i, 0))
    vec = pl.BlockSpec((1, d), lambda i: (0, 0))
    in_specs = [row, row, vec] + ([] if res is None else [row])
    args = [x, dhn, g] + ([] if res is None else [res])
    out_specs = [row] + ([row] if low_copy else []) + [vec]
    out_shape = [jax.ShapeDtypeStruct((t, d), F32)] + ([jax.ShapeDtypeStruct((t, d), MXU_DTYPE)] if low_copy else []) \
        + [jax.ShapeDtypeStruct((1, d), F32)]
    return pl.pallas_call(
        body, name=name, grid=(t // tr,), in_specs=in_specs, out_specs=out_specs, out_shape=out_shape,
        compiler_params=_params(("arbitrary",)),
    )(*args)


def _rope_tables(t):
    pos = jnp.arange(t, dtype=F32)[:, None]
    inv_r = 1.0 / (ROPE_BASE ** (jnp.arange(0, RET_HEAD_DIM, 2, dtype=F32) / RET_HEAD_DIM))
    ang_r = pos * inv_r[None, :]
    inv_m = 1.0 / (ROPE_BASE ** (jnp.arange(0, MLA_ROPE, 2, dtype=F32) / MLA_ROPE))
    ang_m = pos * inv_m[None, :]
    cm, sm = jnp.cos(ang_m), jnp.sin(ang_m)
    z = jnp.zeros_like(cm)
    cos_m = jnp.concatenate([cm, cm, z, z], axis=1)
    sin_m = jnp.concatenate([-sm, sm, z, z], axis=1)
    return jnp.cos(ang_r), jnp.sin(ang_r), cos_m, sin_m


def _rope256(x, c, s, inverse=False):
    x1, x2 = x[:, :128], x[:, 128:]
    if inverse:
        s = -s
    return jnp.concatenate([x1 * c - x2 * s, x2 * c + x1 * s], axis=1)


def _rope64(x, cos_m, sin_m, inverse=False):
    lane = lax.broadcasted_iota(jnp.int32, x.shape, 1)
    partner = jnp.where(lane < 32, pltpu.roll(x, 96, 1), pltpu.roll(x, 32, 1))
    s = -sin_m if inverse else sin_m
    return x * cos_m + partner * s


def _ret_prep(proj, cos_r, sin_r):
    t = proj.shape[0]
    tr = _tile(t, 512, 8)
    hb = RET_WIDTH // RET_HEAD_DIM

    def body(q_ref, k_ref, v_ref, c_ref, s_ref, qo_ref, ko_ref, vo_ref):
        c, s = c_ref[...], s_ref[...]
        qo_ref[...] = _rope256(q_ref[...], c, s).astype(qo_ref.dtype)
        ko_ref[...] = (_rope256(k_ref[...], c, s) * (RET_HEAD_DIM ** -0.5)).astype(ko_ref.dtype)
        vo_ref[...] = v_ref[...].astype(vo_ref.dtype)

    head = lambda off: pl.BlockSpec((tr, RET_HEAD_DIM), lambda i, h: (i, h + off))
    tab = pl.BlockSpec((tr, 128), lambda i, h: (i, 0))
    out = jax.ShapeDtypeStruct((t, RET_WIDTH), MXU_DTYPE)
    return pl.pallas_call(
        body, name="ret_prep", grid=(t // tr, RET_HEADS),
        in_specs=[head(0), head(hb), head(2 * hb), tab, tab],
        out_specs=[head(0), head(0), head(0)], out_shape=[out, out, out],
        compiler_params=_params(("parallel", "parallel")),
    )(proj, proj, proj, cos_r, sin_r)


def _ret_log_gamma():
    return jnp.asarray(np.log1p(-np.exp2(-5.0 - np.arange(RET_HEADS, dtype=np.float64))), dtype=F32)


def _decay_full(lg, i, j, blk):
    r = lax.broadcasted_iota(jnp.int32, (blk, 1), 0).astype(F32)
    c = lax.broadcasted_iota(jnp.int32, (1, blk), 1).astype(F32)
    off = ((i - j) * blk).astype(F32)
    return jnp.exp(lg * r), jnp.exp(lg * (off - c))


def _decay_diag(lg, blk):
    r = lax.broadcasted_iota(jnp.int32, (blk, blk), 0)
    c = lax.broadcasted_iota(jnp.int32, (blk, blk), 1)
    ok = (c // CHUNK) <= (r // CHUNK)
    return jnp.where(ok, jnp.exp(lg * jnp.abs(r - c).astype(F32)), 0.0)


_NT = (((1,), (1,)), ((), ()))
_TN = (((0,), (0,)), ((), ()))
_NN = (((1,), (0,)), ((), ()))


def _causal_pairs(nb, query_major):
    if query_major:
        pairs = [(i, j) for i in range(nb) for j in range(i + 1)]
    else:
        pairs = [(i, j) for j in range(nb) for i in range(j, nb)]
    arr = np.asarray(pairs, dtype=np.int32)
    return jnp.asarray(arr[:, 0]), jnp.asarray(arr[:, 1])


def _ret_fwd(q, k, v, proj, lg):
    t = q.shape[0]
    blk = _tile(t, ATTN_BLOCK)
    nb = t // blk
    hps, d = HEADS_PER_STEP, RET_HEAD_DIM
    gate_off = 3 * RET_WIDTH // (hps * d)

    def body(ii_ref, jj_ref, lg_ref, q_ref, k_ref, v_ref, g_ref, raw_ref, ro_ref, acc_ref):
        hg, pair = pl.program_id(0), pl.program_id(1)
        i, j = ii_ref[pair], jj_ref[pair]

        @pl.when(j == 0)
        def _():
            acc_ref[...] = jnp.zeros_like(acc_ref)

        def step(diag):
            for h in range(hps):
                cols = slice(h * d, (h + 1) * d)
                lgh = lg_ref[hg * hps + h]
                s = lax.dot_general(q_ref[:, cols], k_ref[:, cols], _NT, preferred_element_type=F32)
                if diag:
                    w = s * _decay_diag(lgh, blk)
                else:
                    a, b = _decay_full(lgh, i, j, blk)
                    w = s * a * b
                acc_ref[h] += lax.dot_general(w.astype(MXU_DTYPE), v_ref[:, cols], _NN, preferred_element_type=F32)

        @pl.when(j < i)
        def _():
            step(False)

        @pl.when(j == i)
        def _():
            step(True)
            for h in range(hps):
                cols = slice(h * d, (h + 1) * d)
                o = acc_ref[h]
                raw_ref[:, cols] = o
                mu = jnp.mean(o, axis=-1, keepdims=True)
                var = jnp.mean(jnp.square(o - mu), axis=-1, keepdims=True)
                hn = (o - mu) * lax.rsqrt(var + EPS)
                g = g_ref[:, cols]
                ro_ref[:, cols] = (g * _sigmoid(g) * hn).astype(ro_ref.dtype)

    qs = pl.BlockSpec((blk, hps * d), lambda h, p, ii, jj: (ii[p], h))
    ks = pl.BlockSpec((blk, hps * d), lambda h, p, ii, jj: (jj[p], h))
    gs = pl.BlockSpec((blk, hps * d), lambda h, p, ii, jj: (ii[p], h + gate_off))
    ii, jj = _causal_pairs(nb, query_major=True)
    return pl.pallas_call(
        body, name="ret_fwd",
        grid_spec=pltpu.PrefetchScalarGridSpec(
            num_scalar_prefetch=2, grid=(RET_HEADS // hps, ii.shape[0]),
            in_specs=[pl.BlockSpec(memory_space=pltpu.SMEM), qs, ks, ks, gs], out_specs=[qs, qs],
            scratch_shapes=[pltpu.VMEM((hps, blk, d), F32)]),
        out_shape=[jax.ShapeDtypeStruct((t, RET_WIDTH), F32), jax.ShapeDtypeStruct((t, RET_WIDTH), MXU_DTYPE)],
        compiler_params=_params(("parallel", "arbitrary")),
    )(ii, jj, lg, q, k, v, proj)


def _ret_gate_bwd(raw, proj, dattn):
    t = raw.shape[0]
    tr = _tile(t, 512, 8)
    gate_off = 3 * RET_WIDTH // RET_HEAD_DIM

    def body(o_ref, g_ref, d_ref, do_ref, dg_ref):
        o, g, d = o_ref[...], g_ref[...], d_ref[...]
        mu = jnp.mean(o, axis=-1, keepdims=True)
        rstd = lax.rsqrt(jnp.mean(jnp.square(o - mu), axis=-1, keepdims=True) + EPS)
        hn = (o - mu) * rstd
        sg = _sigmoid(g)
        dg_ref[...] = (d * hn * (sg * (1.0 + g * (1.0 - sg)))).astype(dg_ref.dtype)
        dhn = d * (g * sg)
        do = rstd * (dhn - jnp.mean(dhn, axis=-1, keepdims=True) - hn * jnp.mean(dhn * hn, axis=-1, keepdims=True))
        do_ref[...] = do.astype(do_ref.dtype)

    hs = pl.BlockSpec((tr, RET_HEAD_DIM), lambda i, h: (i, h))
    gs = pl.BlockSpec((tr, RET_HEAD_DIM), lambda i, h: (i, h + gate_off))
    out = jax.ShapeDtypeStruct((t, RET_WIDTH), MXU_DTYPE)
    return pl.pallas_call(
        body, name="ret_gate_bwd", grid=(t // tr, RET_HEADS),
        in_specs=[hs, gs, hs], out_specs=[hs, hs], out_shape=[out, out],
        compiler_params=_params(("parallel", "parallel")),
    )(raw, proj, dattn)


def _ret_bwd(q, k, v, do, lg):
    t = q.shape[0]
    blk = _tile(t, ATTN_BLOCK)
    nb = t // blk

    hps, d = HEADS_PER_STEP, RET_HEAD_DIM

    def body(ii_ref, jj_ref, lg_ref, q_ref, k_ref, v_ref, do_ref, dq_ref, dk_ref, dv_ref, dk_acc, dv_acc):
        hg, pair = pl.program_id(0), pl.program_id(1)
        i, j = ii_ref[pair], jj_ref[pair]

        @pl.when(pair == 0)
        def _():
            dq_ref[...] = jnp.zeros_like(dq_ref)

        @pl.when(i == j)
        def _():
            dk_acc[...] = jnp.zeros_like(dk_acc)
            dv_acc[...] = jnp.zeros_like(dv_acc)

        def step(diag):
            rows = pl.ds(pl.multiple_of(i * blk, blk), blk)
            for h in range(hps):
                cols = slice(h * d, (h + 1) * d)
                lgh = lg_ref[hg * hps + h]
                if diag:
                    decay = _decay_diag(lgh, blk)
                else:
                    a, b = _decay_full(lgh, i, j, blk)
                    decay = a * b
                qb, kb, vb, dob = q_ref[:, cols], k_ref[:, cols], v_ref[:, cols], do_ref[:, cols]
                s = lax.dot_general(qb, kb, _NT, preferred_element_type=F32)
                w = (s * decay).astype(MXU_DTYPE)
                dv_acc[h] += lax.dot_general(w, dob, _TN, preferred_element_type=F32)
                dw = lax.dot_general(dob, vb, _NT, preferred_element_type=F32)
                ds = (dw * decay).astype(MXU_DTYPE)
                dq_ref[rows, cols] += lax.dot_general(ds, kb, _NN, preferred_element_type=F32)
                dk_acc[h] += lax.dot_general(ds, qb, _TN, preferred_element_type=F32)

        @pl.when(i > j)
        def _():
            step(False)

        @pl.when(i == j)
        def _():
            step(True)

        @pl.when(i == nb - 1)
        def _():
            for h in range(hps):
                cols = slice(h * d, (h + 1) * d)
                dk_ref[:, cols] = dk_acc[h]
                dv_ref[:, cols] = dv_acc[h].astype(dv_ref.dtype)

    qs = pl.BlockSpec((blk, hps * d), lambda h, p, ii, jj: (ii[p], h))
    ks = pl.BlockSpec((blk, hps * d), lambda h, p, ii, jj: (jj[p], h))
    ii, jj = _causal_pairs(nb, query_major=False)
    return pl.pallas_call(
        body, name="ret_bwd",
        grid_spec=pltpu.PrefetchScalarGridSpec(
            num_scalar_prefetch=2, grid=(RET_HEADS // hps, ii.shape[0]),
            in_specs=[pl.BlockSpec(memory_space=pltpu.SMEM), qs, ks, ks, qs],
            out_specs=[pl.BlockSpec((t, hps * d), lambda h, p, ii, jj: (0, h)), ks, ks],
            scratch_shapes=[pltpu.VMEM((hps, blk, d), F32), pltpu.VMEM((hps, blk, d), F32)]),
        out_shape=[jax.ShapeDtypeStruct((t, RET_WIDTH), F32), jax.ShapeDtypeStruct((t, RET_WIDTH), F32),
                   jax.ShapeDtypeStruct((t, RET_WIDTH), MXU_DTYPE)],
        compiler_params=_params(("parallel", "arbitrary")),
    )(ii, jj, lg, q, k, v, do)


def _ret_unrope(dq, dk, cos_r, sin_r):
    t = dq.shape[0]
    tr = _tile(t, 512, 8)

    def body(dq_ref, dk_ref, c_ref, s_ref, oq_ref, ok_ref):
        c, s = c_ref[...], s_ref[...]
        oq_ref[...] = _rope256(dq_ref[...], c, s, inverse=True).astype(oq_ref.dtype)
        ok_ref[...] = (_rope256(dk_ref[...], c, s, inverse=True) * (RET_HEAD_DIM ** -0.5)).astype(ok_ref.dtype)

    hs = pl.BlockSpec((tr, RET_HEAD_DIM), lambda i, h: (i, h))
    tab = pl.BlockSpec((tr, 128), lambda i, h: (i, 0))
    out = jax.ShapeDtypeStruct((t, RET_WIDTH), MXU_DTYPE)
    return pl.pallas_call(
        body, name="ret_unrope", grid=(t // tr, RET_HEADS),
        in_specs=[hs, hs, tab, tab], out_specs=[hs, hs], out_shape=[out, out],
        compiler_params=_params(("parallel", "parallel")),
    )(dq, dk, cos_r, sin_r)


def _mla_prep(cq, ckv, kr, g_q, g_kv, cos_m, sin_m):
    t = cq.shape[0]
    tr = _tile(t, 512, 8)

    def body(cq_ref, ckv_ref, kr_ref, gq_ref, gkv_ref, c_ref, s_ref, cqn_ref, kvn_ref, kro_ref):
        for x_ref, g_ref, o_ref in ((cq_ref, gq_ref, cqn_ref), (ckv_ref, gkv_ref, kvn_ref)):
            xf = x_ref[...]
            r = lax.rsqrt(jnp.mean(xf * xf, axis=-1, keepdims=True) + EPS)
            o_ref[...] = (xf * r * g_ref[...]).astype(o_ref.dtype)
        kro_ref[...] = _rope64(kr_ref[...], c_ref[...], s_ref[...]).astype(kro_ref.dtype)

    row = lambda w: pl.BlockSpec((tr, w), lambda i: (i, 0))
    vec = lambda w: pl.BlockSpec((1, w), lambda i: (0, 0))
    return pl.pallas_call(
        body, name="mla_prep", grid=(t // tr,),
        in_specs=[row(Q_LORA), row(KV_LORA), row(128), vec(Q_LORA), vec(KV_LORA), row(128), row(128)],
        out_specs=[row(Q_LORA), row(KV_LORA), row(128)],
        out_shape=[jax.ShapeDtypeStruct((t, Q_LORA), MXU_DTYPE), jax.ShapeDtypeStruct((t, KV_LORA), MXU_DTYPE),
                   jax.ShapeDtypeStruct((t, 128), MXU_DTYPE)],
        compiler_params=_params(("parallel",)),
    )(cq, ckv, kr, g_q, g_kv, cos_m, sin_m)


def _mla_q_rope(q_lin, cos_m, sin_m, *, inverse, name):
    t = q_lin.shape[0]
    tr = _tile(t, 256, 8)

    def body(q_ref, c_ref, s_ref, o_ref):
        c, s = c_ref[...], s_ref[...]
        for h in range(MLA_HEADS):
            lo = h * MLA_QK_PAD
            o_ref[:, lo:lo + MLA_NOPE] = q_ref[:, lo:lo + MLA_NOPE].astype(o_ref.dtype)
            roped = _rope64(q_ref[:, lo + MLA_NOPE:lo + MLA_QK_PAD].astype(F32), c, s, inverse=inverse)
            o_ref[:, lo + MLA_NOPE:lo + MLA_QK_PAD] = roped.astype(o_ref.dtype)

    rows = pl.BlockSpec((tr, Q_WIDTH_PAD), lambda i: (i, 0))
    tab = pl.BlockSpec((tr, 128), lambda i: (i, 0))
    return pl.pallas_call(
        body, name=name, grid=(t // tr,),
        in_specs=[rows, tab, tab], out_specs=rows, out_shape=jax.ShapeDtypeStruct((t, Q_WIDTH_PAD), MXU_DTYPE),
        compiler_params=_params(("parallel",)),
    )(q_lin, cos_m, sin_m)


_MLA_SCALE = (MLA_NOPE + MLA_ROPE) ** -0.5
_LOG2_E = 1.4426950408889634
_MLA_SCALE_LOG2 = _MLA_SCALE * _LOG2_E
_NEG = -1e30


def _mla_mask(blk):
    r = lax.broadcasted_iota(jnp.int32, (blk, blk), 0)
    c = lax.broadcasted_iota(jnp.int32, (blk, blk), 1)
    return (c // CHUNK) <= (r // CHUNK)


def _mla_fwd(q, kv, kr):
    t = q.shape[0]
    blk = _tile(t, ATTN_BLOCK)
    nb = t // blk

    hps, dq_, dv_ = HEADS_PER_STEP, MLA_QK_PAD, MLA_V

    def body(ii_ref, jj_ref, q_ref, kv_ref, kr_ref, o_ref, lse_ref, m_ref, acc_ref):
        pair = pl.program_id(1)
        i, j = ii_ref[pair], jj_ref[pair]

        @pl.when(j == 0)
        def _():
            m_ref[...] = jnp.full_like(m_ref, _NEG)
            acc_ref[...] = jnp.zeros_like(acc_ref)

        def step(masked):
            krb = kr_ref[...]
            ones = jnp.ones((blk, dv_), MXU_DTYPE)
            for h in range(hps):
                kb = jnp.concatenate([kv_ref[:, h * dq_:h * dq_ + MLA_NOPE], krb], axis=1)
                vb = jnp.concatenate([kv_ref[:, h * dq_ + MLA_NOPE:(h + 1) * dq_], ones], axis=1)
                s = lax.dot_general(q_ref[:, h * dq_:(h + 1) * dq_], kb, _NT, preferred_element_type=F32)
                if masked:
                    s = jnp.where(_mla_mask(blk), s, _NEG)
                m_prev = m_ref[h]
                m_new = jnp.maximum(m_prev, jnp.max(s, axis=-1, keepdims=True))
                alpha = jnp.exp2((m_prev - m_new) * _MLA_SCALE_LOG2)
                p = jnp.exp2((s - jnp.tile(m_new, (1, blk // 128))) * _MLA_SCALE_LOG2)
                acc_ref[h] = jnp.tile(alpha, (1, 2)) * acc_ref[h] + lax.dot_general(
                    p.astype(MXU_DTYPE), vb, _NN, preferred_element_type=F32)
                m_ref[h] = m_new

        @pl.when(j < i)
        def _():
            step(False)

        @pl.when(j == i)
        def _():
            step(True)
            for h in range(hps):
                cols = slice(h * dv_, (h + 1) * dv_)
                acc = acc_ref[h]
                row_sum = acc[:, dv_:]
                o_ref[:, cols] = (acc[:, :dv_] / row_sum).astype(o_ref.dtype)
                lse_ref[:, cols] = m_ref[h] * _MLA_SCALE + jnp.log(row_sum)

    os_ = pl.BlockSpec((blk, hps * dv_), lambda h, p, ii, jj: (ii[p], h))
    ii, jj = _causal_pairs(nb, query_major=True)
    return pl.pallas_call(
        body, name="mla_fwd",
        grid_spec=pltpu.PrefetchScalarGridSpec(
            num_scalar_prefetch=2, grid=(MLA_HEADS // hps, ii.shape[0]),
            in_specs=[pl.BlockSpec((blk, hps * dq_), lambda h, p, ii, jj: (ii[p], h)),
                      pl.BlockSpec((blk, hps * dq_), lambda h, p, ii, jj: (jj[p], h)),
                      pl.BlockSpec((blk, 128), lambda h, p, ii, jj: (jj[p], 0))],
            out_specs=[os_, os_],
            scratch_shapes=[pltpu.VMEM((hps, blk, 128), F32), pltpu.VMEM((hps, blk, 2 * dv_), F32)]),
        out_shape=[jax.ShapeDtypeStruct((t, MLA_HEADS * MLA_V), MXU_DTYPE),
                   jax.ShapeDtypeStruct((t, MLA_HEADS * 128), F32)],
        compiler_params=_params(("parallel", "arbitrary")),
    )(ii, jj, q, kv, kr)


def _mla_bwd(q, kv, kr, o, lse, dattn):
    t = q.shape[0]
    blk = _tile(t, ATTN_BLOCK)
    nb = t // blk
    hps, dq_, dv_ = HEADS_PER_STEP, MLA_QK_PAD, MLA_V
    do_off = RET_WIDTH // (hps * dv_)

    def body(ii_ref, jj_ref, q_ref, kv_ref, kr_ref, o_ref, lse_ref, do_ref, dq_ref, dk_ref, dv_ref, dk_acc, dv_acc):
        pair = pl.program_id(1)
        i, j = ii_ref[pair], jj_ref[pair]

        @pl.when(pair == 0)
        def _():
            dq_ref[...] = jnp.zeros_like(dq_ref)

        @pl.when(i == j)
        def _():
            dk_acc[...] = jnp.zeros_like(dk_acc)
            dv_acc[...] = jnp.zeros_like(dv_acc)

        def step(masked):
            krb = kr_ref[...]
            rows = pl.ds(pl.multiple_of(i * blk, blk), blk)
            for h in range(hps):
                qcols, vcols = slice(h * dq_, (h + 1) * dq_), slice(h * dv_, (h + 1) * dv_)
                qb = q_ref[:, qcols]
                kb = jnp.concatenate([kv_ref[:, h * dq_:h * dq_ + MLA_NOPE], krb], axis=1)
                vb = kv_ref[:, h * dq_ + MLA_NOPE:(h + 1) * dq_]
                dof = do_ref[:, vcols]
                dob = dof.astype(MXU_DTYPE)
                s = lax.dot_general(qb, kb, _NT, preferred_element_type=F32)
                if masked:
                    s = jnp.where(_mla_mask(blk), s, _NEG)
                lse2 = lse_ref[:, vcols] * _LOG2_E
                p = jnp.exp2(s * _MLA_SCALE_LOG2 - jnp.tile(lse2, (1, blk // 128)))
                delta = jnp.sum(dof * o_ref[:, vcols].astype(F32), axis=-1, keepdims=True)
                dv_acc[h] += lax.dot_general(p.astype(MXU_DTYPE), dob, _TN, preferred_element_type=F32)
                dp = lax.dot_general(dob, vb, _NT, preferred_element_type=F32)
                ds = (p * (dp - delta) * _MLA_SCALE).astype(MXU_DTYPE)
                dq_ref[rows, qcols] += lax.dot_general(ds, kb, _NN, preferred_element_type=F32)
                dk_acc[h] += lax.dot_general(ds, qb, _TN, preferred_element_type=F32)

        @pl.when(i > j)
        def _():
            step(False)

        @pl.when(i == j)
        def _():
            step(True)

        @pl.when(i == nb - 1)
        def _():
            for h in range(hps):
                dk_ref[:, h * dq_:(h + 1) * dq_] = dk_acc[h]
                dv_ref[:, h * dv_:(h + 1) * dv_] = dv_acc[h].astype(dv_ref.dtype)

    qmap = lambda off: (lambda h, p, ii, jj: (ii[p], h + off))
    kmap = lambda h, p, ii, jj: (jj[p], h)
    ii, jj = _causal_pairs(nb, query_major=False)
    return pl.pallas_call(
        body, name="mla_bwd",
        grid_spec=pltpu.PrefetchScalarGridSpec(
            num_scalar_prefetch=2, grid=(MLA_HEADS // hps, ii.shape[0]),
            in_specs=[pl.BlockSpec((blk, hps * dq_), qmap(0)), pl.BlockSpec((blk, hps * dq_), kmap),
                      pl.BlockSpec((blk, 128), lambda h, p, ii, jj: (jj[p], 0)),
                      pl.BlockSpec((blk, hps * dv_), qmap(0)), pl.BlockSpec((blk, hps * dv_), qmap(0)),
                      pl.BlockSpec((blk, hps * dv_), qmap(do_off))],
            out_specs=[pl.BlockSpec((t, hps * dq_), lambda h, p, ii, jj: (0, h)),
                       pl.BlockSpec((blk, hps * dq_), kmap), pl.BlockSpec((blk, hps * dv_), kmap)],
            scratch_shapes=[pltpu.VMEM((hps, blk, dq_), F32), pltpu.VMEM((hps, blk, dv_), F32)]),
        out_shape=[jax.ShapeDtypeStruct((t, Q_WIDTH_PAD), F32), jax.ShapeDtypeStruct((t, Q_WIDTH_PAD), F32),
                   jax.ShapeDtypeStruct((t, MLA_HEADS * MLA_V), MXU_DTYPE)],
        compiler_params=_params(("parallel", "arbitrary")),
    )(ii, jj, q, kv, kr, o, lse, dattn)


def _mla_kv_grad(dk, dv, cos_m, sin_m):
    t = dk.shape[0]
    tr = _tile(t, 256, 8)

    def body(dk_ref, dv_ref, c_ref, s_ref, dkv_ref, dkr_ref):
        acc = jnp.zeros((tr, 128), F32)
        for h in range(MLA_HEADS):
            dkv_ref[:, h * 256:h * 256 + 128] = dk_ref[:, h * 256:h * 256 + 128].astype(dkv_ref.dtype)
            dkv_ref[:, h * 256 + 128:h * 256 + 256] = dv_ref[:, h * 128:(h + 1) * 128].astype(dkv_ref.dtype)
            acc = acc + dk_ref[:, h * 256 + 128:h * 256 + 256]
        dkr_ref[...] = _rope64(acc, c_ref[...], s_ref[...], inverse=True).astype(dkr_ref.dtype)

    row = lambda w: pl.BlockSpec((tr, w), lambda i: (i, 0))
    return pl.pallas_call(
        body, name="mla_kv_grad", grid=(t // tr,),
        in_specs=[row(Q_WIDTH_PAD), row(MLA_HEADS * MLA_V), row(128), row(128)],
        out_specs=[row(Q_WIDTH_PAD), row(128)],
        out_shape=[jax.ShapeDtypeStruct((t, Q_WIDTH_PAD), MXU_DTYPE), jax.ShapeDtypeStruct((t, 128), MXU_DTYPE)],
        compiler_params=_params(("parallel",)),
    )(dk, dv, cos_m, sin_m)


_FFN_COLS = 256
_FFN_ROWS = 256


def _shift_down(cur, prev8, n):
    out = pltpu.roll(cur, n, 0)
    head = out[:8]
    row = lax.broadcasted_iota(jnp.int32, head.shape, 0)
    for r in range(n):
        head = jnp.where(row == r, prev8[8 - n + r:8 - n + r + 1, :], head)
    return jnp.concatenate([head, out[8:]], axis=0)


def _shift_up(cur, next8, n):
    rows = cur.shape[0]
    out = pltpu.roll(cur, rows - n, 0)
    tail = out[rows - 8:]
    row = lax.broadcasted_iota(jnp.int32, tail.shape, 0)
    for r in range(n):
        tail = jnp.where(row == 8 - n + r, next8[r:r + 1, :], tail)
    return jnp.concatenate([out[:rows - 8], tail], axis=0)


def _conv_pre(g_ref, cw_ref, cb_ref, c, rc):
    r0 = pl.multiple_of(c * rc, rc)
    cur = g_ref[pl.ds(r0, rc), :].astype(F32)
    prev16 = g_ref[pl.ds(pl.multiple_of(jnp.maximum(r0 - 16, 0), 16), 16), :].astype(F32)
    prev8 = jnp.where(c > 0, prev16[8:], 0.0)
    s1, s2 = _shift_down(cur, prev8, 1), _shift_down(cur, prev8, 2)
    a = cb_ref[...] + cw_ref[2:3, :] * cur + cw_ref[1:2, :] * s1 + cw_ref[0:1, :] * s2
    return r0, cur, s1, s2, a


def _ffn_act_fwd(gpre, u, cw, cb):
    t, f = gpre.shape
    tc = _tile(f, _FFN_COLS)
    rc = _tile(t, _FFN_ROWS, 8)

    def body(g_ref, u_ref, cw_ref, cb_ref, o_ref):
        def chunk(c, carry):
            r0, _, _, _, a = _conv_pre(g_ref, cw_ref, cb_ref, c, rc)
            o_ref[pl.ds(r0, rc), :] = (a * _sigmoid(a) * u_ref[pl.ds(r0, rc), :]).astype(o_ref.dtype)
            return carry
        lax.fori_loop(0, t // rc, chunk, 0)

    col = pl.BlockSpec((t, tc), lambda j: (0, j))
    return pl.pallas_call(
        body, name="ffn_act_fwd", grid=(f // tc,),
        in_specs=[col, col, pl.BlockSpec((CONV_WIDTH, tc), lambda j: (0, j)), pl.BlockSpec((1, tc), lambda j: (0, j))],
        out_specs=col, out_shape=jax.ShapeDtypeStruct((t, f), MXU_DTYPE),
        compiler_params=_params(("parallel",)),
    )(gpre, u, cw, cb)


def _ffn_act_bwd(gpre, u, dact, cw, cb):
    t, f = gpre.shape
    tc = _tile(f, _FFN_COLS)
    rc = _tile(t, _FFN_ROWS, 8)
    nc = t // rc

    def body(g_ref, u_ref, d_ref, cw_ref, cb_ref, dg_ref, du_ref, dcw_ref, dcb_ref, da_ref):
        def chunk(c, carry):
            w0, w1, w2, b = carry
            r0, cur, s1, s2, a = _conv_pre(g_ref, cw_ref, cb_ref, c, rc)
            sg = _sigmoid(a)
            d = d_ref[pl.ds(r0, rc), :].astype(F32)
            du_ref[pl.ds(r0, rc), :] = (d * (a * sg)).astype(du_ref.dtype)
            da = d * u_ref[pl.ds(r0, rc), :].astype(F32) * (sg * (1.0 + a * (1.0 - sg)))
            da_ref[pl.ds(r0, rc), :] = da
            return (w0 + jnp.sum(da * s2, axis=0, keepdims=True), w1 + jnp.sum(da * s1, axis=0, keepdims=True),
                    w2 + jnp.sum(da * cur, axis=0, keepdims=True), b + jnp.sum(da, axis=0, keepdims=True))
        z = jnp.zeros((1, tc), F32)
        w0, w1, w2, b = lax.fori_loop(0, nc, chunk, (z, z, z, z))
        dcw_ref[0:1, :] = w0
        dcw_ref[1:2, :] = w1
        dcw_ref[2:3, :] = w2
        dcb_ref[...] = b

        def chunk2(c, carry):
            r0 = pl.multiple_of(c * rc, rc)
            cur = da_ref[pl.ds(r0, rc), :]
            nxt = da_ref[pl.ds(pl.multiple_of(jnp.minimum(r0 + rc, t - 8), 8), 8), :]
            nxt = jnp.where(c < nc - 1, nxt, 0.0)
            dg = cw_ref[2:3, :] * cur + cw_ref[1:2, :] * _shift_up(cur, nxt, 1) + cw_ref[0:1, :] * _shift_up(cur, nxt, 2)
            dg_ref[pl.ds(r0, rc), :] = dg.astype(dg_ref.dtype)
            return carry
        lax.fori_loop(0, nc, chunk2, 0)

    col = pl.BlockSpec((t, tc), lambda j: (0, j))
    w3 = pl.BlockSpec((CONV_WIDTH, tc), lambda j: (0, j))
    w1 = pl.BlockSpec((1, tc), lambda j: (0, j))
    low = jax.ShapeDtypeStruct((t, f), MXU_DTYPE)
    return pl.pallas_call(
        body, name="ffn_act_bwd", grid=(f // tc,),
        in_specs=[col, col, col, w3, w1], out_specs=[col, col, w3, w1],
        out_shape=[low, low, jax.ShapeDtypeStruct((CONV_WIDTH, f), F32), jax.ShapeDtypeStruct((1, f), F32)],
        scratch_shapes=[pltpu.VMEM((t, tc), F32)],
        compiler_params=_params(("parallel",)),
    )(gpre, u, dact, cw, cb)


def _head_fwd_bwd(h2, glin, pp, target, g_final):
    t, d = h2.shape
    tr = _tile(t, 128, 8)

    def body(h_ref, gl_ref, pp_ref, t_ref, g_ref, loss_ref, dh_ref, dgl_ref, dpp_ref, dg_ref):
        gate = _sigmoid(gl_ref[...])
        ppv = pp_ref[...]
        h3 = h_ref[...] + gate * ppv
        r = lax.rsqrt(jnp.mean(h3 * h3, axis=-1, keepdims=True) + EPS)
        yh = h3 * r
        g = g_ref[...]
        diff = yh * g - t_ref[...]
        lpart = 0.5 * jnp.sum(jnp.mean(diff * diff, axis=-1, keepdims=True), axis=0, keepdims=True)
        dy = diff * (1.0 / d)
        dyg = dy * g
        dh3 = r * dyg - h3 * (r * r * r) * jnp.mean(dyg * h3, axis=-1, keepdims=True)
        dh_ref[...] = dh3
        dgl_ref[...] = (dh3 * ppv * gate * (1.0 - gate)).astype(dgl_ref.dtype)
        dpp_ref[...] = (dh3 * gate).astype(dpp_ref.dtype)
        dgp = jnp.sum(dy * yh, axis=0, keepdims=True)

        @pl.when(pl.program_id(0) == 0)
        def _():
            loss_ref[...] = jnp.broadcast_to(lpart, loss_ref.shape)
            dg_ref[...] = dgp

        @pl.when(pl.program_id(0) > 0)
        def _():
            loss_ref[...] += jnp.broadcast_to(lpart, loss_ref.shape)
            dg_ref[...] += dgp

    row = pl.BlockSpec((tr, d), lambda i: (i, 0))
    vec = pl.BlockSpec((1, d), lambda i: (0, 0))
    low = jax.ShapeDtypeStruct((t, d), MXU_DTYPE)
    return pl.pallas_call(
        body, name="head_fwd_bwd", grid=(t // tr,),
        in_specs=[row, row, row, row, vec],
        out_specs=[pl.BlockSpec((8, 128), lambda i: (0, 0)), row, row, row, vec],
        out_shape=[jax.ShapeDtypeStruct((8, 128), F32), jax.ShapeDtypeStruct((t, d), F32), low, low,
                   jax.ShapeDtypeStruct((1, d), F32)],
        compiler_params=_params(("arbitrary",)),
    )(h2, glin, pp, target, g_final)


class _Order:
    def __init__(self):
        self.last = None

    def tie(self, x):
        return x if self.last is None else lax.optimization_barrier((x, self.last))[0]

    def run(self, fn, first, *args, **kwargs):
        out = fn(self.tie(first), *args, **kwargs)
        self.last = out[0] if isinstance(out, (tuple, list)) else out
        return out


def _local_step(x, p, target, vec, ops):
    t = x.shape[0]
    cos_r, sin_r, cos_m, sin_m = _rope_tables(t)
    lg = _ret_log_gamma()
    low = MXU_DTYPE
    run = ops.order.run
    w = ops.weight

    ops.start_gather("w_in", halves=True)
    hn1 = run(_rms_fwd, x, vec["g_attn"], name="rms1_fwd")
    for n in ("w_uq", "w_ukv", "w_o"):
        ops.start_gather(n, after=hn1)
    half = x.shape[1] // 2
    proj = run(_matmul, hn1[:, :half], w("w_in/0"), tb=True, name="mm_proj_a")
    proj = run(_matmul, hn1[:, half:], w("w_in/1"), tb=True, name="mm_proj_b", add=proj)
    ops.start_gather("w_ffn_gate", after=proj)
    rq, rk, rv = run(_ret_prep, proj, cos_r, sin_r)
    ops.start_gather("w_ffn_up", after=rq)
    c0 = 4 * RET_WIDTH
    cq = proj[:, c0:c0 + Q_LORA]
    ckv = proj[:, c0 + Q_LORA:c0 + Q_LORA + KV_LORA]
    kr_in = proj[:, c0 + Q_LORA + KV_LORA:c0 + Q_LORA + KV_LORA + 128]
    cqn, kvn, kr = run(_mla_prep, cq, ckv, kr_in, vec["g_q_lora"], vec["g_kv_lora"], cos_m, sin_m)
    ops.start_gather("w_ffn_down", after=cqn)
    q_lin = run(_matmul, cqn, w("w_uq"), tb=True, name="mm_q")
    q = run(_mla_q_rope, q_lin, cos_m, sin_m, inverse=False, name="mla_q_rope")
    kv = run(_matmul, kvn, w("w_ukv"), tb=True, name="mm_kv", out_dtype=low)
    mo, lse = run(_mla_fwd, q, kv, kr)
    ops.start_gather("w_ple_gate", after=mo)
    ops.start_gather("w_ple_proj", after=mo)
    ret_raw, ro = run(_ret_fwd, rq, rk, rv, proj, lg)
    attn = jnp.concatenate([ro, mo], axis=1)
    h1 = run(_matmul, attn, w("w_o"), name="mm_o", add=x)
    hn2 = run(_rms_fwd, h1, vec["g_ffn"], name="rms2_fwd")
    gpre = run(_matmul, hn2, w("w_ffn_gate"), tb=True, name="mm_gate", out_dtype=low)
    u = run(_matmul, hn2, w("w_ffn_up"), tb=True, name="mm_up", out_dtype=low)
    act = run(_ffn_act_fwd, gpre, u, vec["conv_w"], vec["conv_b"])
    h2 = run(_matmul, act, w("w_ffn_down"), name="mm_down", add=h1)
    hn3 = run(_rms_fwd, h2, vec["g_ple"], name="rms3_fwd")
    glin = run(_matmul, hn3, w("w_ple_gate"), name="mm_ple_gate")
    p_low = p.astype(low)
    pp = run(_matmul, p_low, w("w_ple_proj"), tb=True, name="mm_ple_proj")
    loss_part, dh3, dglin, dpp, dg_final = run(_head_fwd_bwd, h2, glin, pp, target, vec["g_final"])

    ops.grad("w_ple_proj", run(_matmul, dpp, p_low, ta=True, name="mm_d_ple_proj", out_dtype=low))
    ops.grad("w_ple_gate", run(_matmul, hn3, dglin, ta=True, name="mm_d_ple_gate", out_dtype=low))
    dhn3 = run(_matmul, dglin, w("w_ple_gate"), tb=True, name="mm_dhn3", out_dtype=low)
    dh2, dh2_low, dg_ple = run(_rms_bwd, h2, dhn3, vec["g_ple"], dh3, name="rms3_bwd", low_copy=True)
    ops.reduce_add("w_ple_proj")
    ops.reduce_add("w_ple_gate")
    ops.grad("w_ffn_down", run(_matmul, act, dh2_low, ta=True, name="mm_d_down", out_dtype=low))
    dact = run(_matmul, dh2_low, w("w_ffn_down"), tb=True, name="mm_dact", out_dtype=low)
    ops.reduce_add("w_ffn_down")
    dgpre, du, dconv_w, dconv_b = run(_ffn_act_bwd, gpre, u, dact, vec["conv_w"], vec["conv_b"])
    ops.update("w_ple_proj")
    ops.update("w_ple_gate")
    ops.grad("w_ffn_gate", run(_matmul, dgpre, hn2, ta=True, name="mm_d_gate", out_dtype=low))
    ops.grad("w_ffn_up", run(_matmul, du, hn2, ta=True, name="mm_d_up", out_dtype=low))
    ops.reduce_add("w_ffn_gate")
    dhn2 = run(_matmul, dgpre, w("w_ffn_gate"), name="mm_dhn2_a")
    ops.reduce_add("w_ffn_up")
    dhn2 = run(_matmul, du, w("w_ffn_up"), name="mm_dhn2_b", add=dhn2, out_dtype=low)
    dh1, dh1_low, dg_ffn = run(_rms_bwd, h1, dhn2, vec["g_ffn"], dh2, name="rms2_bwd", low_copy=True)
    ops.update("w_ffn_down")
    ops.grad("w_o", run(_matmul, attn, dh1_low, ta=True, name="mm_d_o", out_dtype=low))
    dattn = run(_matmul, dh1_low, w("w_o"), tb=True, name="mm_dattn")
    ops.reduce_add("w_o")

    dq_r, dk_full, dv = run(_mla_bwd, q, kv, kr, mo, lse, dattn)
    ops.update("w_ffn_gate")
    dq_lin = run(_mla_q_rope, dq_r, cos_m, sin_m, inverse=True, name="mla_q_unrope")
    dkv, dkr = run(_mla_kv_grad, dk_full, dv, cos_m, sin_m)
    ops.grad("w_uq", run(_matmul, dq_lin, cqn, ta=True, name="mm_d_uq", out_dtype=low))
    dcqn = run(_matmul, dq_lin, w("w_uq"), name="mm_dcqn")
    ops.grad("w_ukv", run(_matmul, dkv, kvn, ta=True, name="mm_d_ukv", out_dtype=low))
    dkvn = run(_matmul, dkv, w("w_ukv"), name="mm_dkvn")
    dcq, dcq_low, dg_q = run(_rms_bwd, cq, dcqn, vec["g_q_lora"], None, name="rmsq_bwd", low_copy=True)
    dckv, dckv_low, dg_kv = run(_rms_bwd, ckv, dkvn, vec["g_kv_lora"], None, name="rmskv_bwd", low_copy=True)
    ops.reduce_add("w_uq")
    ops.reduce_add("w_ukv")

    do_ret, drg = run(_ret_gate_bwd, ret_raw, proj, dattn)
    dq_ret, dk_ret, drv = run(_ret_bwd, rq, rk, rv, do_ret, lg)
    drq, drk = run(_ret_unrope, dq_ret, dk_ret, cos_r, sin_r)

    pad = jnp.zeros((t, IN_WIDTH_PAD - IN_WIDTH - 64), low)
    dproj = jnp.concatenate([drq, drk, drv, drg, dcq_low, dckv_low, dkr, pad], axis=1)
    ops.grad("w_in", run(_matmul, dproj, hn1, ta=True, name="mm_d_in", out_dtype=low))
    for n in ("w_ffn_up", "w_o", "w_uq", "w_ukv"):
        ops.update(n)
    ops.reduce_add("w_in")
    dhn1 = jnp.concatenate([run(_matmul, dproj, w(f"w_in/{part}"), name=f"mm_dhn1_{part}", out_dtype=low)
                            for part in range(2)], axis=1)
    grad_x, dg_attn = run(_rms_bwd, x, dhn1, vec["g_attn"], dh1, name="rms1_bwd")

    gs = {"g_attn": dg_attn, "g_q_lora": dg_q, "g_kv_lora": dg_kv, "g_ffn": dg_ffn, "conv_w": dconv_w,
          "conv_b": dconv_b, "g_ple": dg_ple, "g_final": dg_final}
    return loss_part, grad_x, gs


_COL_SHARDED = ("w_in", "w_uq", "w_ukv", "w_ffn_gate", "w_ffn_up", "w_ple_proj")
_FFN_SHARD = D_FF // N_DEV
_FFN_SHARD_PAD = D_FF_PAD // N_DEV
_HEADS_PER_SHARD = MLA_HEADS // N_DEV
_QK = MLA_NOPE + MLA_ROPE


def _pad_rows(name, a):
    lead = a.shape[:-2]
    if name == "w_uq":
        a = a.reshape(lead + (_HEADS_PER_SHARD, _QK, a.shape[-1]))
        a = jnp.pad(a, [(0, 0)] * len(lead) + [(0, 0), (0, MLA_QK_PAD - _QK), (0, 0)])
        return a.reshape(lead + (_HEADS_PER_SHARD * MLA_QK_PAD, a.shape[-1]))
    if name in ("w_ffn_gate", "w_ffn_up", "w_ffn_down"):
        return jnp.pad(a, [(0, 0)] * len(lead) + [(0, _FFN_SHARD_PAD - _FFN_SHARD), (0, 0)])
    return a


def _unpad_rows(name, a):
    lead = a.shape[:-2]
    if name == "w_uq":
        a = a.reshape(lead + (_HEADS_PER_SHARD, MLA_QK_PAD, a.shape[-1]))[..., :_QK, :]
        return a.reshape(lead + (_HEADS_PER_SHARD * _QK, a.shape[-1]))
    if name in ("w_ffn_gate", "w_ffn_up", "w_ffn_down"):
        return a[..., :_FFN_SHARD, :]
    return a


def _rows_view(name, a):
    return jnp.swapaxes(a, 0, 1) if name in _COL_SHARDED else a


def _shard_payload(name, shard):
    return _pad_rows(name, _rows_view(name, shard).astype(MXU_DTYPE))


def _full_from_gathered(name, g):
    full = g.reshape(g.shape[0] * g.shape[1], g.shape[2])
    if name == "w_in":
        full = jnp.pad(full, ((0, IN_WIDTH_PAD - IN_WIDTH), (0, 0)))
    return full


def _grad_chunks(name, gfull):
    if name == "w_in":
        gfull = gfull[:IN_WIDTH]
    return gfull.reshape(N_DEV, gfull.shape[0] // N_DEV, gfull.shape[1])


def _ffn_vec_layout(a):
    a = a.reshape(a.shape[0], N_DEV, _FFN_SHARD)
    return jnp.pad(a, ((0, 0), (0, 0), (0, _FFN_SHARD_PAD - _FFN_SHARD))).reshape(a.shape[0], D_FF_PAD)


def _ffn_vec_shards(a):
    return a.reshape(a.shape[0], N_DEV, _FFN_SHARD_PAD)[:, :, :_FFN_SHARD]


_MESH = pl.DeviceIdType.MESH
_ANY = pl.BlockSpec(memory_space=pl.ANY)


def _place():
    x, y, c = lax.axis_index("x"), lax.axis_index("y"), lax.axis_index("c")
    chips = [(1 - x, y), (x, 1 - y), (1 - x, 1 - y)]
    return x, y, c, chips


def _handshake(peers):
    barrier = pltpu.get_barrier_semaphore()
    for peer in peers:
        pl.semaphore_signal(barrier, inc=1, device_id=peer, device_id_type=_MESH)
    pl.semaphore_wait(barrier, len(peers))


_SEQUENCER = dict(axis_name="seq", num_cores=1)
_AG_COLLECTIVE_ID = 1
_RS_SIBLING_COLLECTIVE_ID = 2
_RS_CHIPS_COLLECTIVE_ID = 3


def _all_gather_seq(shard, *, name):
    def body(x_ref, out_ref, send_sems, recv_sems, local_sem):
        x, y, c, chips = _place()
        sibling = (x, y, 1 - c)
        _handshake([sibling] + [(*chip, c) for chip in chips])

        def slot(px, py, pc):
            return out_ref.at[4 * px + 2 * py + pc]

        def copy(k, block, to, src=None):
            return pltpu.make_async_remote_copy(
                src_ref=slot(*block) if src is None else src, dst_ref=slot(*block),
                send_sem=send_sems.at[k], recv_sem=recv_sems.at[k], device_id=to, device_id_type=_MESH)

        mine = pltpu.make_async_copy(x_ref, slot(x, y, c), local_sem)
        mine.start()
        first = [copy(0, (x, y, c), sibling, src=x_ref)]
        first += [copy(1 + j, (x, y, c), (*chip, c), src=x_ref) for j, chip in enumerate(chips)]
        for cp in first:
            cp.start()
        passed = [copy(4 + j, (*chip, c), sibling) for j, chip in enumerate(chips)]
        for j, chip in enumerate(chips):
            copy(1 + j, (*chip, c), (x, y, c)).wait_recv()
            passed[j].start()
        copy(0, sibling, (x, y, c)).wait_recv()
        for j, chip in enumerate(chips):
            copy(4 + j, (*chip, 1 - c), (x, y, c)).wait_recv()
        for cp in first + passed:
            cp.wait_send()
        mine.wait()

    return pl.kernel(
        body, out_type=jax.ShapeDtypeStruct((N_DEV,) + shard.shape, shard.dtype),
        mesh=plsc.ScalarSubcoreMesh(**_SEQUENCER), name=name,
        scratch_types=[pltpu.SemaphoreType.DMA((7,)), pltpu.SemaphoreType.DMA((7,)), pltpu.SemaphoreType.DMA(())],
        compiler_params=pltpu.CompilerParams(collective_id=_AG_COLLECTIVE_ID),
    )(shard)


def _exchange_sibling(g, *, name):
    def body(g_ref, out_ref, send_sems, recv_sems):
        x, y, c, _ = _place()
        sibling = (x, y, 1 - c)
        _handshake([sibling])
        copies = []
        for chip in range(4):
            cp = pltpu.make_async_remote_copy(
                src_ref=g_ref.at[2 * chip + (1 - c)], dst_ref=out_ref.at[chip],
                send_sem=send_sems.at[chip], recv_sem=recv_sems.at[chip], device_id=sibling, device_id_type=_MESH)
            cp.start()
            copies.append(cp)
        for cp in copies:
            cp.wait_recv()
        for cp in copies:
            cp.wait_send()

    return pl.kernel(
        body, out_type=jax.ShapeDtypeStruct((4,) + g.shape[1:], g.dtype),
        mesh=plsc.ScalarSubcoreMesh(**_SEQUENCER), name=name,
        scratch_types=[pltpu.SemaphoreType.DMA((4,)), pltpu.SemaphoreType.DMA((4,))],
        compiler_params=pltpu.CompilerParams(collective_id=_RS_SIBLING_COLLECTIVE_ID),
    )(g)


def _add_sibling(g, recv, *, name):
    _, r, cdim = g.shape
    tr, tc = _tile_2d(r, cdim, 6)
    g4 = g.reshape(4, 2, r, cdim)
    core = lax.axis_index("c").astype(jnp.int32).reshape(1)

    def body(c_ref, g_ref, r_ref, o_ref):
        o_ref[...] = (g_ref[...].astype(F32) + r_ref[...].astype(F32)).astype(o_ref.dtype)

    return pl.pallas_call(
        body, name=name,
        grid_spec=pltpu.PrefetchScalarGridSpec(
            num_scalar_prefetch=1, grid=(4, r // tr, cdim // tc),
            in_specs=[pl.BlockSpec((None, None, tr, tc), lambda ch, i, j, c_ref: (ch, c_ref[0], i, j)),
                      pl.BlockSpec((None, tr, tc), lambda ch, i, j, c_ref: (ch, i, j))],
            out_specs=pl.BlockSpec((None, tr, tc), lambda ch, i, j, c_ref: (ch, i, j))),
        out_shape=jax.ShapeDtypeStruct((4, r, cdim), g.dtype),
        compiler_params=_params(("parallel", "parallel", "parallel")),
    )(core, g4, recv)


def _exchange_chips(pch, *, name):
    def body(p_ref, out_ref, send_sems, recv_sems, local_sem):
        x, y, c, chips = _place()
        _handshake([(*chip, c) for chip in chips])
        me = 2 * x + y
        mine = pltpu.make_async_copy(p_ref.at[me], out_ref.at[me], local_sem)
        mine.start()
        copies = []
        for j, (px, py) in enumerate(chips):
            cp = pltpu.make_async_remote_copy(
                src_ref=p_ref.at[2 * px + py], dst_ref=out_ref.at[me],
                send_sem=send_sems.at[j], recv_sem=recv_sems.at[j], device_id=(px, py, c), device_id_type=_MESH)
            cp.start()
            copies.append(cp)
        for j, (px, py) in enumerate(chips):
            pltpu.make_async_remote_copy(
                src_ref=p_ref.at[me], dst_ref=out_ref.at[2 * px + py],
                send_sem=send_sems.at[j], recv_sem=recv_sems.at[j], device_id=(px, py, c), device_id_type=_MESH).wait_recv()
        for cp in copies:
            cp.wait_send()
        mine.wait()

    return pl.kernel(
        body, out_type=jax.ShapeDtypeStruct(pch.shape, pch.dtype),
        mesh=plsc.ScalarSubcoreMesh(**_SEQUENCER), name=name,
        scratch_types=[pltpu.SemaphoreType.DMA((3,)), pltpu.SemaphoreType.DMA((3,)), pltpu.SemaphoreType.DMA(())],
        compiler_params=pltpu.CompilerParams(collective_id=_RS_CHIPS_COLLECTIVE_ID),
    )(pch)


def _all_reduce_small(v, *, name):
    r = v.shape[0]

    def body(x_ref, out_ref, buf_ref, send_sems, recv_sems):
        x, y, c, chips = _place()
        sibling = (x, y, 1 - c)

        def slot(px, py, pc):
            return buf_ref.at[4 * px + 2 * py + pc]

        def copy(k, block, to, src=None):
            return pltpu.make_async_remote_copy(
                src_ref=slot(*block) if src is None else src, dst_ref=slot(*block),
                send_sem=send_sems.at[k], recv_sem=recv_sems.at[k], device_id=to, device_id_type=_MESH)

        first = [copy(0, (x, y, c), sibling, src=x_ref)]
        first += [copy(1 + j, (x, y, c), (*chip, c), src=x_ref) for j, chip in enumerate(chips)]
        for cp in first:
            cp.start()
        buf_ref[4 * x + 2 * y + c] = x_ref[...]
        passed = [copy(4 + j, (*chip, c), sibling) for j, chip in enumerate(chips)]
        for j, chip in enumerate(chips):
            copy(1 + j, (*chip, c), (x, y, c)).wait_recv()
            passed[j].start()
        copy(0, sibling, (x, y, c)).wait_recv()
        for j, chip in enumerate(chips):
            copy(4 + j, (*chip, 1 - c), (x, y, c)).wait_recv()
        for cp in first + passed:
            cp.wait_send()
        total = buf_ref[0]
        for k in range(1, N_DEV):
            total = total + buf_ref[k]
        out_ref[...] = total

    vm = pl.BlockSpec(memory_space=pltpu.VMEM)
    return pl.pallas_call(
        body, name=name, out_shape=jax.ShapeDtypeStruct(v.shape, v.dtype),
        in_specs=[vm], out_specs=vm,
        scratch_shapes=[pltpu.VMEM((N_DEV,) + v.shape, v.dtype), pltpu.SemaphoreType.DMA((7,)),
                        pltpu.SemaphoreType.DMA((7,))],
    )(v)


_ELEMENTWISE_VMEM = 24 * 1024 * 1024


def _tile_2d(r, c, n_arrays):
    per_block = _ELEMENTWISE_VMEM // (8 * n_arrays)
    tr = _tile(r, max(16, per_block // max(c, 128)), 16)
    if tr * c <= per_block:
        return tr, c
    return r, _tile(c, max(128, (per_block // r) // 128 * 128))


def _adam_math(w, g, m, v):
    m = ADAM_B1 * m + (1.0 - ADAM_B1) * g
    v = ADAM_B2 * v + (1.0 - ADAM_B2) * jnp.square(g)
    m_hat = m / (1.0 - ADAM_B1 ** ADAM_STEP)
    v_hat = v / (1.0 - ADAM_B2 ** ADAM_STEP)
    delta = -ADAM_LR * (m_hat / (jnp.sqrt(v_hat) + ADAM_EPS) + ADAM_WD * w)
    return delta, m, v


def _adam(w, g, m, v, *, name, parts=None):
    r, cdim = w.shape
    tr, tc = _tile_2d(r, cdim, 8)

    def body(w_ref, g_ref, m_ref, v_ref, go_ref, d_ref, mo_ref, vo_ref):
        if parts is None:
            g = g_ref[...]
        else:
            g = g_ref[0].astype(F32)
            for k in range(1, parts):
                g = g + g_ref[k].astype(F32)
        delta, m, v = _adam_math(w_ref[...], g, m_ref[...], v_ref[...])
        go_ref[...] = g
        d_ref[...] = delta
        mo_ref[...] = m
        vo_ref[...] = v

    blk = pl.BlockSpec((tr, tc), lambda i, j: (i, j))
    gblk = blk if parts is None else pl.BlockSpec((parts, tr, tc), lambda i, j: (0, i, j))
    out = jax.ShapeDtypeStruct((r, cdim), F32)
    return pl.pallas_call(
        body, name=name, grid=(r // tr, cdim // tc), in_specs=[blk, gblk, blk, blk], out_specs=[blk] * 4,
        out_shape=[out] * 4, compiler_params=_params(("parallel", "parallel")),
    )(w, g, m, v)


_BIG = ("w_in", "w_uq", "w_ukv", "w_o", "w_ffn_gate", "w_ffn_up", "w_ffn_down", "w_ple_gate", "w_ple_proj")
_WEIGHTS = ("w_in", "g_attn", "g_q_lora", "g_kv_lora", "w_uq", "w_ukv", "w_o", "g_ffn", "w_ffn_gate", "w_ffn_up",
            "conv_w", "conv_b", "w_ffn_down", "g_ple", "w_ple_gate", "w_ple_proj", "g_final")
_SMALL_PACK = (("g_attn", 1, D_MODEL), ("g_q_lora", 1, Q_LORA), ("g_kv_lora", 1, KV_LORA), ("g_ffn", 1, D_MODEL),
               ("conv_w", CONV_WIDTH, D_FF_PAD), ("conv_b", 1, D_FF_PAD), ("g_ple", 1, D_MODEL), ("g_final", 1, D_MODEL))


def _pack_small(gs):
    flat = jnp.concatenate([gs[n].reshape(-1) for n, _, _ in _SMALL_PACK])
    rows = -(-flat.shape[0] // 128)
    rows = -(-rows // 8) * 8
    return jnp.pad(flat, (0, rows * 128 - flat.shape[0])).reshape(rows, 128)


def _unpack_small(packed):
    flat = packed.reshape(-1)
    out, off = {}, 0
    for n, r, c in _SMALL_PACK:
        out[n] = flat[off:off + r * c].reshape(r, c)
        off += r * c
    return out


def kernel(x, p, w_in, g_attn, g_q_lora, g_kv_lora, w_uq, w_ukv, w_o, g_ffn, w_ffn_gate, w_ffn_up, conv_w, conv_b, w_ffn_down, g_ple, w_ple_gate, w_ple_proj, g_final, loss_target, m_w_in, m_g_attn, m_g_q_lora, m_g_kv_lora, m_w_uq, m_w_ukv, m_w_o, m_g_ffn, m_w_ffn_gate, m_w_ffn_up, m_conv_w, m_conv_b, m_w_ffn_down, m_g_ple, m_w_ple_gate, m_w_ple_proj, m_g_final, v_w_in, v_g_attn, v_g_q_lora, v_g_kv_lora, v_w_uq, v_w_ukv, v_w_o, v_g_ffn, v_w_ffn_gate, v_w_ffn_up, v_conv_w, v_conv_b, v_w_ffn_down, v_g_ple, v_w_ple_gate, v_w_ple_proj, v_g_final):
    given = dict(locals())
    wts = {n: given[n] for n in _WEIGHTS}
    mom = {n: given["m_" + n] for n in _WEIGHTS}
    var = {n: given["v_" + n] for n in _WEIGHTS}
    me = (4 * lax.axis_index("x") + 2 * lax.axis_index("y") + lax.axis_index("c")).astype(jnp.int32)
    ops = _ShardedWeights(wts, mom, var)

    conv_full = _all_gather_seq(conv_w, name="ag_conv_w")[:, 0].transpose(1, 0, 2).reshape(CONV_WIDTH, D_FF)
    vec = {"g_attn": g_attn, "g_q_lora": g_q_lora, "g_kv_lora": g_kv_lora, "g_ffn": g_ffn, "g_ple": g_ple,
           "g_final": g_final[None, :], "conv_w": _ffn_vec_layout(conv_full), "conv_b": _ffn_vec_layout(conv_b)}

    loss_part, grad_x, gs = _local_step(x[0], p[0, 0], loss_target[0], vec, ops)
    loss = lax.psum(loss_part[0, 0], ("x", "y", "c"))

    small = _unpack_small(_all_reduce_small(ops.order.tie(_pack_small(gs)), name="ar_small"))
    conv_w_shards = _ffn_vec_shards(small["conv_w"])
    small_g = {
        "g_attn": small["g_attn"], "g_q_lora": small["g_q_lora"], "g_kv_lora": small["g_kv_lora"],
        "g_ffn": small["g_ffn"], "g_ple": small["g_ple"], "g_final": small["g_final"],
        "conv_b": _ffn_vec_shards(small["conv_b"]).reshape(1, D_FF),
        "conv_w": lax.dynamic_index_in_dim(conv_w_shards, me, axis=1, keepdims=False),
    }
    results = dict(ops.results)
    for n, g in small_g.items():
        shape = wts[n].shape
        outs = ops.order.run(_adam, wts[n].reshape(g.shape), g, mom[n].reshape(g.shape), var[n].reshape(g.shape),
                             name="adam_" + n)
        results[n] = tuple(a.reshape(shape) for a in outs)
    ops.update("w_in")
    results["w_in"] = ops.results["w_in"]

    return (loss, grad_x[None], *[results[n][0] for n in _WEIGHTS], *[results[n][1] for n in _WEIGHTS],
            *[results[n][2] for n in _WEIGHTS], *[results[n][3] for n in _WEIGHTS])


class _ShardedWeights:
    def __init__(self, wts, mom, var):
        self.wts, self.mom, self.var = wts, mom, var
        self.order = _Order()
        self.full, self.stage, self.results = {}, {}, {}

    def start_gather(self, name, after=None, halves=False):
        payload = _shard_payload(name, self.wts[name][0])
        if after is not None:
            payload = lax.optimization_barrier((payload, after))[0]
        if not halves:
            self.full[name] = _full_from_gathered(name, _all_gather_seq(payload, name="ag_" + name))
            return
        half = payload.shape[1] // 2
        for part in range(2):
            piece = payload[:, part * half:(part + 1) * half]
            self.full[f"{name}/{part}"] = _full_from_gathered(name, _all_gather_seq(piece, name=f"ag_{name}_{part}"))

    def weight(self, name):
        return self.full[name]

    def grad(self, name, gfull):
        chunks = _grad_chunks(name, gfull)
        self.stage[name] = (chunks, _exchange_sibling(chunks, name="rs_sib_" + name))

    def reduce_add(self, name):
        chunks, from_sibling = self.stage[name]
        per_chip = self.order.run(_add_sibling, chunks, from_sibling, name="rs_add_" + name)
        self.stage[name] = _exchange_chips(per_chip, name="rs_chip_" + name)

    def update(self, name):
        parts = self.stage[name]
        if name == "w_uq":
            parts = _unpad_rows(name, parts)
        rows = lambda a: _rows_view(name, a[0])
        outs = self.order.run(_adam, rows(self.wts[name]), parts, rows(self.mom[name]), rows(self.var[name]),
                              name="adam_" + name, parts=4)
        self.results[name] = tuple(_rows_view(name, a)[None] for a in outs)
```

```python
import functools

import numpy as np

import jax
import jax.numpy as jnp
from jax import lax
from jax.experimental import pallas as pl
from jax.experimental.pallas import tpu as pltpu
from jax.experimental.pallas import tpu_sc as plsc

D_MODEL = 4096
CHUNK = 64
PLE_DIM = 256
RET_HEADS = 8
RET_HEAD_DIM = 256
RET_WIDTH = 2048
MLA_HEADS = 16
MLA_NOPE = 128
MLA_ROPE = 64
MLA_V = 128
Q_LORA = 1024
KV_LORA = 512
D_FF = 11008
CONV_WIDTH = 3
ROPE_BASE = 10000.0
EPS = 1e-6
IN_WIDTH = 9792
ADAM_LR, ADAM_B1, ADAM_B2, ADAM_EPS, ADAM_WD, ADAM_STEP = 0.001, 0.9, 0.999, 1e-08, 0.01, 10

IN_WIDTH_PAD = 10240
D_FF_PAD = 11264
MLA_QK_PAD = 256
Q_WIDTH_PAD = MLA_HEADS * MLA_QK_PAD

N_DEV = 8
MXU_DTYPE = jnp.bfloat16
ATTN_BLOCK = 512
HEADS_PER_STEP = 4
FWD_HEADS_PER_STEP = 8
VMEM_LIMIT = 56 * 1024 * 1024

F32 = jnp.float32


def _tile(n, want, align=128):
    if n <= want:
        return n
    t = (want // align) * align
    while t >= align:
        if n % t == 0:
            return t
        t -= align
    return n


def _params(sem):
    return pltpu.CompilerParams(dimension_semantics=sem, vmem_limit_bytes=VMEM_LIMIT)


def _sigmoid(x):
    return 1.0 / (1.0 + jnp.exp(-x))


def _matmul(a, b, *, name, ta=False, tb=False, out_dtype=F32, add=None, tm=1024, tn=1024, tk=4096):
    m, k = (a.shape[1], a.shape[0]) if ta else a.shape
    k2, n = (b.shape[1], b.shape[0]) if tb else b.shape
    assert k == k2, (a.shape, b.shape, ta, tb)
    tm, tn, tk = _tile(m, tm), _tile(n, tn), _tile(k, tk)
    nk = k // tk
    dims = (((0 if ta else 1,), (1 if tb else 0,)), ((), ()))

    def body(*refs):
        a_ref, b_ref, o_ref = refs[0], refs[1], refs[3 if add is not None else 2]
        c_ref = refs[2] if add is not None else None
        part = lax.dot_general(a_ref[...].astype(MXU_DTYPE), b_ref[...].astype(MXU_DTYPE), dims,
                               preferred_element_type=F32)

        def finish(r):
            if c_ref is not None:
                r = r + c_ref[...].astype(F32)
            o_ref[...] = r.astype(out_dtype)

        if nk == 1:
            finish(part)
            return
        acc_ref = refs[-1]
        kk = pl.program_id(2)

        @pl.when(kk == 0)
        def _():
            acc_ref[...] = part

        @pl.when((kk > 0) & (kk < nk - 1))
        def _():
            acc_ref[...] += part

        @pl.when(kk == nk - 1)
        def _():
            finish(acc_ref[...] + part)

    a_spec = pl.BlockSpec((tk, tm), lambda i, j, kk: (kk, i)) if ta else pl.BlockSpec((tm, tk), lambda i, j, kk: (i, kk))
    b_spec = pl.BlockSpec((tn, tk), lambda i, j, kk: (j, kk)) if tb else pl.BlockSpec((tk, tn), lambda i, j, kk: (kk, j))
    in_specs = [a_spec, b_spec]
    args = [a, b]
    if add is not None:
        in_specs.append(pl.BlockSpec((tm, tn), lambda i, j, kk: (i, j)))
        args.append(add)
    return pl.pallas_call(
        body, name=name, grid=(m // tm, n // tn, nk),
        in_specs=in_specs, out_specs=pl.BlockSpec((tm, tn), lambda i, j, kk: (i, j)),
        out_shape=jax.ShapeDtypeStruct((m, n), out_dtype),
        scratch_shapes=[] if nk == 1 else [pltpu.VMEM((tm, tn), F32)],
        compiler_params=_params(("parallel", "parallel", "arbitrary")),
    )(*args)


def _rms_fwd(x, g, *, name):
    t, d = x.shape
    tr = _tile(t, 256, 8)

    def body(x_ref, g_ref, o_ref):
        xf = x_ref[...]
        r = lax.rsqrt(jnp.mean(xf * xf, axis=-1, keepdims=True) + EPS)
        o_ref[...] = (xf * r * g_ref[...]).astype(o_ref.dtype)

    return pl.pallas_call(
        body, name=name, grid=(t // tr,),
        in_specs=[pl.BlockSpec((tr, d), lambda i: (i, 0)), pl.BlockSpec((1, d), lambda i: (0, 0))],
        out_specs=pl.BlockSpec((tr, d), lambda i: (i, 0)),
        out_shape=jax.ShapeDtypeStruct((t, d), MXU_DTYPE),
        compiler_params=_params(("parallel",)),
    )(x, g)


def _rms_bwd(x, dhn, g, res, *, name, low_copy=False):
    t, d = x.shape
    tr = _tile(t, 256, 8)

    def body(*refs):
        if res is None:
            x_ref, dh_ref, g_ref = refs[:3]
            outs = refs[3:]
            res_ref = None
        else:
            x_ref, dh_ref, g_ref, res_ref = refs[:4]
            outs = refs[4:]
        dx_ref, dg_ref = outs[0], outs[-1]
        xf = x_ref[...]
        dh = dh_ref[...].astype(F32)
        r = lax.rsqrt(jnp.mean(xf * xf, axis=-1, keepdims=True) + EPS)
        dyg = dh * g_ref[...]
        dx = r * dyg - xf * (r * r * r) * jnp.mean(dyg * xf, axis=-1, keepdims=True)
        if res_ref is not None:
            dx = dx + res_ref[...]
        dx_ref[...] = dx
        if low_copy:
            outs[1][...] = dx.astype(outs[1].dtype)
        part = jnp.sum(dh * xf * r, axis=0, keepdims=True)

        @pl.when(pl.program_id(0) == 0)
        def _():
            dg_ref[...] = part

        @pl.when(pl.program_id(0) > 0)
        def _():
            dg_ref[...] += part

    row = pl.BlockSpec((tr, d), lambda i: (i, 0))
    vec = pl.BlockSpec((1, d), lambda i: (0, 0))
    in_specs = [row, row, vec] + ([] if res is None else [row])
    args = [x, dhn, g] + ([] if res is None else [res])
    out_specs = [row] + ([row] if low_copy else []) + [vec]
    out_shape = [jax.ShapeDtypeStruct((t, d), F32)] + ([jax.ShapeDtypeStruct((t, d), MXU_DTYPE)] if low_copy else []) \
        + [jax.ShapeDtypeStruct((1, d), F32)]
    return pl.pallas_call(
        body, name=name, grid=(t // tr,), in_specs=in_specs, out_specs=out_specs, out_shape=out_shape,
        compiler_params=_params(("arbitrary",)),
    )(*args)


def _rope_tables(t):
    pos = jnp.arange(t, dtype=F32)[:, None]
    inv_r = 1.0 / (ROPE_BASE ** (jnp.arange(0, RET_HEAD_DIM, 2, dtype=F32) / RET_HEAD_DIM))
    ang_r = pos * inv_r[None, :]
    inv_m = 1.0 / (ROPE_BASE ** (jnp.arange(0, MLA_ROPE, 2, dtype=F32) / MLA_ROPE))
    ang_m = pos * inv_m[None, :]
    cm, sm = jnp.cos(ang_m), jnp.sin(ang_m)
    z = jnp.zeros_like(cm)
    cos_m = jnp.concatenate([cm, cm, z, z], axis=1)
    sin_m = jnp.concatenate([-sm, sm, z, z], axis=1)
    return jnp.cos(ang_r), jnp.sin(ang_r), cos_m, sin_m


def _rope256(x, c, s, inverse=False):
    x1, x2 = x[:, :128], x[:, 128:]
    if inverse:
        s = -s
    return jnp.concatenate([x1 * c - x2 * s, x2 * c + x1 * s], axis=1)


def _rope64(x, cos_m, sin_m, inverse=False):
    lane = lax.broadcasted_iota(jnp.int32, x.shape, 1)
    partner = jnp.where(lane < 32, pltpu.roll(x, 96, 1), pltpu.roll(x, 32, 1))
    s = -sin_m if inverse else sin_m
    return x * cos_m + partner * s


def _ret_prep(proj, cos_r, sin_r):
    t = proj.shape[0]
    tr = _tile(t, 256, 8)

    def body(q_ref, k_ref, v_ref, c_ref, s_ref, qo_ref, ko_ref, vo_ref):
        c, s = c_ref[...], s_ref[...]
        for h in range(RET_HEADS):
            cols = slice(h * RET_HEAD_DIM, (h + 1) * RET_HEAD_DIM)
            qo_ref[:, cols] = _rope256(q_ref[:, cols], c, s).astype(qo_ref.dtype)
            ko_ref[:, cols] = (_rope256(k_ref[:, cols], c, s) * (RET_HEAD_DIM ** -0.5)).astype(ko_ref.dtype)
        vo_ref[...] = v_ref[...].astype(vo_ref.dtype)

    group = lambda off: pl.BlockSpec((tr, RET_WIDTH), lambda i: (i, off))
    tab = pl.BlockSpec((tr, 128), lambda i: (i, 0))
    out = jax.ShapeDtypeStruct((t, RET_WIDTH), MXU_DTYPE)
    return pl.pallas_call(
        body, name="ret_prep", grid=(t // tr,),
        in_specs=[group(0), group(1), group(2), tab, tab],
        out_specs=[group(0), group(0), group(0)], out_shape=[out, out, out],
        compiler_params=_params(("parallel",)),
    )(proj, proj, proj, cos_r, sin_r)


def _ret_log_gamma():
    return jnp.asarray(np.log1p(-np.exp2(-5.0 - np.arange(RET_HEADS, dtype=np.float64))), dtype=F32)


def _decay_full(lg, i, j, blk):
    r = lax.broadcasted_iota(jnp.int32, (blk, 1), 0).astype(F32)
    c = lax.broadcasted_iota(jnp.int32, (1, blk), 1).astype(F32)
    off = ((i - j) * blk).astype(F32)
    return jnp.exp(lg * r), jnp.exp(lg * (off - c))


def _decay_diag(lg, blk):
    r = lax.broadcasted_iota(jnp.int32, (blk, blk), 0)
    c = lax.broadcasted_iota(jnp.int32, (blk, blk), 1)
    ok = (c // CHUNK) <= (r // CHUNK)
    return jnp.where(ok, jnp.exp(lg * jnp.abs(r - c).astype(F32)), 0.0)


_NT = (((1,), (1,)), ((), ()))
_TN = (((0,), (0,)), ((), ()))
_NN = (((1,), (0,)), ((), ()))


def _causal_pairs(nb, query_major):
    if query_major:
        pairs = [(i, j) for i in range(nb) for j in range(i + 1)]
    else:
        pairs = [(i, j) for j in range(nb) for i in range(j, nb)]
    arr = np.asarray(pairs, dtype=np.int32)
    return jnp.asarray(arr[:, 0]), jnp.asarray(arr[:, 1])


def _ret_fwd(q, k, v, proj, lg):
    t = q.shape[0]
    blk = _tile(t, ATTN_BLOCK)
    nb = t // blk
    hps, d = FWD_HEADS_PER_STEP, RET_HEAD_DIM
    gate_off = 3 * RET_WIDTH // (hps * d)

    def body(ii_ref, jj_ref, lg_ref, q_ref, k_ref, v_ref, g_ref, raw_ref, ro_ref, acc_ref):
        hg, pair = pl.program_id(0), pl.program_id(1)
        i, j = ii_ref[pair], jj_ref[pair]

        @pl.when(j == 0)
        def _():
            acc_ref[...] = jnp.zeros_like(acc_ref)

        def step(diag):
            for h in range(hps):
                cols = slice(h * d, (h + 1) * d)
                lgh = lg_ref[hg * hps + h]
                s = lax.dot_general(q_ref[:, cols], k_ref[:, cols], _NT, preferred_element_type=F32)
                if diag:
                    w = s * _decay_diag(lgh, blk)
                else:
                    a, b = _decay_full(lgh, i, j, blk)
                    w = s * a * b
                acc_ref[h] += lax.dot_general(w.astype(MXU_DTYPE), v_ref[:, cols], _NN, preferred_element_type=F32)

        @pl.when(j < i)
        def _():
            step(False)

        @pl.when(j == i)
        def _():
            step(True)
            for h in range(hps):
                cols = slice(h * d, (h + 1) * d)
                o = acc_ref[h]
                raw_ref[:, cols] = o
                mu = jnp.mean(o, axis=-1, keepdims=True)
                var = jnp.mean(jnp.square(o - mu), axis=-1, keepdims=True)
                hn = (o - mu) * lax.rsqrt(var + EPS)
                g = g_ref[:, cols]
                ro_ref[:, cols] = (g * _sigmoid(g) * hn).astype(ro_ref.dtype)

    qs = pl.BlockSpec((blk, hps * d), lambda h, p, ii, jj: (ii[p], h))
    ks = pl.BlockSpec((blk, hps * d), lambda h, p, ii, jj: (jj[p], h))
    gs = pl.BlockSpec((blk, hps * d), lambda h, p, ii, jj: (ii[p], h + gate_off))
    ii, jj = _causal_pairs(nb, query_major=True)
    return pl.pallas_call(
        body, name="ret_fwd",
        grid_spec=pltpu.PrefetchScalarGridSpec(
            num_scalar_prefetch=2, grid=(RET_HEADS // hps, ii.shape[0]),
            in_specs=[pl.BlockSpec(memory_space=pltpu.SMEM), qs, ks, ks, gs], out_specs=[qs, qs],
            scratch_shapes=[pltpu.VMEM((hps, blk, d), F32)]),
        out_shape=[jax.ShapeDtypeStruct((t, RET_WIDTH), F32), jax.ShapeDtypeStruct((t, RET_WIDTH), MXU_DTYPE)],
        compiler_params=_params(("parallel", "arbitrary")),
    )(ii, jj, lg, q, k, v, proj)


def _ret_gate_bwd(raw, proj, dattn):
    t = raw.shape[0]
    tr = _tile(t, 256, 8)

    def body(o_ref, g_ref, d_ref, do_ref, dg_ref):
        for h in range(RET_HEADS):
            cols = slice(h * RET_HEAD_DIM, (h + 1) * RET_HEAD_DIM)
            o, g, d = o_ref[:, cols], g_ref[:, cols], d_ref[:, cols]
            mu = jnp.mean(o, axis=-1, keepdims=True)
            rstd = lax.rsqrt(jnp.mean(jnp.square(o - mu), axis=-1, keepdims=True) + EPS)
            hn = (o - mu) * rstd
            sg = _sigmoid(g)
            dg_ref[:, cols] = (d * hn * (sg * (1.0 + g * (1.0 - sg)))).astype(dg_ref.dtype)
            dhn = d * (g * sg)
            do = rstd * (dhn - jnp.mean(dhn, axis=-1, keepdims=True)
                         - hn * jnp.mean(dhn * hn, axis=-1, keepdims=True))
            do_ref[:, cols] = do.astype(do_ref.dtype)

    group = lambda off: pl.BlockSpec((tr, RET_WIDTH), lambda i: (i, off))
    out = jax.ShapeDtypeStruct((t, RET_WIDTH), MXU_DTYPE)
    return pl.pallas_call(
        body, name="ret_gate_bwd", grid=(t // tr,),
        in_specs=[group(0), group(3), group(0)], out_specs=[group(0), group(0)], out_shape=[out, out],
        compiler_params=_params(("parallel",)),
    )(raw, proj, dattn)


def _ret_bwd(q, k, v, do, lg):
    t = q.shape[0]
    blk = _tile(t, ATTN_BLOCK)
    nb = t // blk

    hps, d = HEADS_PER_STEP, RET_HEAD_DIM

    def body(ii_ref, jj_ref, lg_ref, q_ref, k_ref, v_ref, do_ref, dq_ref, dk_ref, dv_ref, dk_acc, dv_acc):
        hg, pair = pl.program_id(0), pl.program_id(1)
        i, j = ii_ref[pair], jj_ref[pair]

        @pl.when(pair == 0)
        def _():
            dq_ref[...] = jnp.zeros_like(dq_ref)

        @pl.when(i == j)
        def _():
            dk_acc[...] = jnp.zeros_like(dk_acc)
            dv_acc[...] = jnp.zeros_like(dv_acc)

        def step(diag):
            rows = pl.ds(pl.multiple_of(i * blk, blk), blk)
            for h in range(hps):
                cols = slice(h * d, (h + 1) * d)
                lgh = lg_ref[hg * hps + h]
                if diag:
                    decay = _decay_diag(lgh, blk)
                else:
                    a, b = _decay_full(lgh, i, j, blk)
                    decay = a * b
                qb, kb, vb, dob = q_ref[:, cols], k_ref[:, cols], v_ref[:, cols], do_ref[:, cols]
                s = lax.dot_general(qb, kb, _NT, preferred_element_type=F32)
                w = (s * decay).astype(MXU_DTYPE)
                dv_acc[h] += lax.dot_general(w, dob, _TN, preferred_element_type=F32)
                dw = lax.dot_general(dob, vb, _NT, preferred_element_type=F32)
                ds = (dw * decay).astype(MXU_DTYPE)
                dq_ref[rows, cols] += lax.dot_general(ds, kb, _NN, preferred_element_type=F32)
                dk_acc[h] += lax.dot_general(ds, qb, _TN, preferred_element_type=F32)

        @pl.when(i > j)
        def _():
            step(False)

        @pl.when(i == j)
        def _():
            step(True)

        @pl.when(i == nb - 1)
        def _():
            for h in range(hps):
                cols = slice(h * d, (h + 1) * d)
                dk_ref[:, cols] = dk_acc[h]
                dv_ref[:, cols] = dv_acc[h].astype(dv_ref.dtype)

    qs = pl.BlockSpec((blk, hps * d), lambda h, p, ii, jj: (ii[p], h))
    ks = pl.BlockSpec((blk, hps * d), lambda h, p, ii, jj: (jj[p], h))
    ii, jj = _causal_pairs(nb, query_major=False)
    return pl.pallas_call(
        body, name="ret_bwd",
        grid_spec=pltpu.PrefetchScalarGridSpec(
            num_scalar_prefetch=2, grid=(RET_HEADS // hps, ii.shape[0]),
            in_specs=[pl.BlockSpec(memory_space=pltpu.SMEM), qs, ks, ks, qs],
            out_specs=[pl.BlockSpec((t, hps * d), lambda h, p, ii, jj: (0, h), pipeline_mode=pl.Buffered(1)), ks, ks],
            scratch_shapes=[pltpu.VMEM((hps, blk, d), F32), pltpu.VMEM((hps, blk, d), F32)]),
        out_shape=[jax.ShapeDtypeStruct((t, RET_WIDTH), F32), jax.ShapeDtypeStruct((t, RET_WIDTH), F32),
                   jax.ShapeDtypeStruct((t, RET_WIDTH), MXU_DTYPE)],
        compiler_params=_params(("parallel", "arbitrary")),
    )(ii, jj, lg, q, k, v, do)


def _ret_unrope(dq, dk, cos_r, sin_r):
    t = dq.shape[0]
    tr = _tile(t, 256, 8)

    def body(dq_ref, dk_ref, c_ref, s_ref, oq_ref, ok_ref):
        c, s = c_ref[...], s_ref[...]
        for h in range(RET_HEADS):
            cols = slice(h * RET_HEAD_DIM, (h + 1) * RET_HEAD_DIM)
            oq_ref[:, cols] = _rope256(dq_ref[:, cols], c, s, inverse=True).astype(oq_ref.dtype)
            ok_ref[:, cols] = (_rope256(dk_ref[:, cols], c, s, inverse=True)
                               * (RET_HEAD_DIM ** -0.5)).astype(ok_ref.dtype)

    rows = pl.BlockSpec((tr, RET_WIDTH), lambda i: (i, 0))
    tab = pl.BlockSpec((tr, 128), lambda i: (i, 0))
    out = jax.ShapeDtypeStruct((t, RET_WIDTH), MXU_DTYPE)
    return pl.pallas_call(
        body, name="ret_unrope", grid=(t // tr,),
        in_specs=[rows, rows, tab, tab], out_specs=[rows, rows], out_shape=[out, out],
        compiler_params=_params(("parallel",)),
    )(dq, dk, cos_r, sin_r)


def _mla_prep(cq, ckv, kr, g_q, g_kv, cos_m, sin_m):
    t = cq.shape[0]
    tr = _tile(t, 512, 8)

    def body(cq_ref, ckv_ref, kr_ref, gq_ref, gkv_ref, c_ref, s_ref, cqn_ref, kvn_ref, kro_ref):
        for x_ref, g_ref, o_ref in ((cq_ref, gq_ref, cqn_ref), (ckv_ref, gkv_ref, kvn_ref)):
            xf = x_ref[...]
            r = lax.rsqrt(jnp.mean(xf * xf, axis=-1, keepdims=True) + EPS)
            o_ref[...] = (xf * r * g_ref[...]).astype(o_ref.dtype)
        kro_ref[...] = _rope64(kr_ref[...], c_ref[...], s_ref[...]).astype(kro_ref.dtype)

    row = lambda w: pl.BlockSpec((tr, w), lambda i: (i, 0))
    vec = lambda w: pl.BlockSpec((1, w), lambda i: (0, 0))
    return pl.pallas_call(
        body, name="mla_prep", grid=(t // tr,),
        in_specs=[row(Q_LORA), row(KV_LORA), row(128), vec(Q_LORA), vec(KV_LORA), row(128), row(128)],
        out_specs=[row(Q_LORA), row(KV_LORA), row(128)],
        out_shape=[jax.ShapeDtypeStruct((t, Q_LORA), MXU_DTYPE), jax.ShapeDtypeStruct((t, KV_LORA), MXU_DTYPE),
                   jax.ShapeDtypeStruct((t, 128), MXU_DTYPE)],
        compiler_params=_params(("parallel",)),
    )(cq, ckv, kr, g_q, g_kv, cos_m, sin_m)


def _mla_q_rope(q_lin, cos_m, sin_m, *, inverse, name):
    t = q_lin.shape[0]
    tr = _tile(t, 256, 8)

    def body(q_ref, c_ref, s_ref, o_ref):
        c, s = c_ref[...], s_ref[...]
        for h in range(MLA_HEADS):
            lo = h * MLA_QK_PAD
            o_ref[:, lo:lo + MLA_NOPE] = q_ref[:, lo:lo + MLA_NOPE].astype(o_ref.dtype)
            roped = _rope64(q_ref[:, lo + MLA_NOPE:lo + MLA_QK_PAD].astype(F32), c, s, inverse=inverse)
            o_ref[:, lo + MLA_NOPE:lo + MLA_QK_PAD] = roped.astype(o_ref.dtype)

    rows = pl.BlockSpec((tr, Q_WIDTH_PAD), lambda i: (i, 0))
    tab = pl.BlockSpec((tr, 128), lambda i: (i, 0))
    return pl.pallas_call(
        body, name=name, grid=(t // tr,),
        in_specs=[rows, tab, tab], out_specs=rows, out_shape=jax.ShapeDtypeStruct((t, Q_WIDTH_PAD), MXU_DTYPE),
        compiler_params=_params(("parallel",)),
    )(q_lin, cos_m, sin_m)


_MLA_SCALE = (MLA_NOPE + MLA_ROPE) ** -0.5
_LOG2_E = 1.4426950408889634
_MLA_SCALE_LOG2 = _MLA_SCALE * _LOG2_E
_NEG = -1e30


def _mla_mask(blk):
    r = lax.broadcasted_iota(jnp.int32, (blk, blk), 0)
    c = lax.broadcasted_iota(jnp.int32, (blk, blk), 1)
    return (c // CHUNK) <= (r // CHUNK)


def _mla_fwd(q, kv, kr):
    t = q.shape[0]
    blk = _tile(t, ATTN_BLOCK)
    nb = t // blk

    hps, dq_, dv_ = FWD_HEADS_PER_STEP, MLA_QK_PAD, MLA_V

    def body(ii_ref, jj_ref, q_ref, kv_ref, kr_ref, o_ref, lse_ref, m_ref, acc_ref):
        pair = pl.program_id(1)
        i, j = ii_ref[pair], jj_ref[pair]

        @pl.when(j == 0)
        def _():
            m_ref[...] = jnp.full_like(m_ref, _NEG)
            acc_ref[...] = jnp.zeros_like(acc_ref)

        def step(masked):
            krb = kr_ref[...]
            ones = jnp.ones((blk, dv_), MXU_DTYPE)
            for h in range(hps):
                kb = jnp.concatenate([kv_ref[:, h * dq_:h * dq_ + MLA_NOPE], krb], axis=1)
                vb = jnp.concatenate([kv_ref[:, h * dq_ + MLA_NOPE:(h + 1) * dq_], ones], axis=1)
                s = lax.dot_general(q_ref[:, h * dq_:(h + 1) * dq_], kb, _NT, preferred_element_type=F32)
                if masked:
                    s = jnp.where(_mla_mask(blk), s, _NEG)
                m_prev = m_ref[h]
                m_new = jnp.maximum(m_prev, jnp.max(s, axis=-1, keepdims=True))
                alpha = jnp.exp2((m_prev - m_new) * _MLA_SCALE_LOG2)
                p = jnp.exp2((s - jnp.tile(m_new, (1, blk // 128))) * _MLA_SCALE_LOG2)
                acc_ref[h] = jnp.tile(alpha, (1, 2)) * acc_ref[h] + lax.dot_general(
                    p.astype(MXU_DTYPE), vb, _NN, preferred_element_type=F32)
                m_ref[h] = m_new

        @pl.when(j < i)
        def _():
            step(False)

        @pl.when(j == i)
        def _():
            step(True)
            for h in range(hps):
                cols = slice(h * dv_, (h + 1) * dv_)
                acc = acc_ref[h]
                row_sum = acc[:, dv_:]
                o_ref[:, cols] = (acc[:, :dv_] / row_sum).astype(o_ref.dtype)
                lse_ref[:, cols] = m_ref[h] * _MLA_SCALE + jnp.log(row_sum)

    os_ = pl.BlockSpec((blk, hps * dv_), lambda h, p, ii, jj: (ii[p], h))
    ii, jj = _causal_pairs(nb, query_major=True)
    return pl.pallas_call(
        body, name="mla_fwd",
        grid_spec=pltpu.PrefetchScalarGridSpec(
            num_scalar_prefetch=2, grid=(MLA_HEADS // hps, ii.shape[0]),
            in_specs=[pl.BlockSpec((blk, hps * dq_), lambda h, p, ii, jj: (ii[p], h)),
                      pl.BlockSpec((blk, hps * dq_), lambda h, p, ii, jj: (jj[p], h)),
                      pl.BlockSpec((blk, 128), lambda h, p, ii, jj: (jj[p], 0))],
            out_specs=[os_, os_],
            scratch_shapes=[pltpu.VMEM((hps, blk, 128), F32), pltpu.VMEM((hps, blk, 2 * dv_), F32)]),
        out_shape=[jax.ShapeDtypeStruct((t, MLA_HEADS * MLA_V), MXU_DTYPE),
                   jax.ShapeDtypeStruct((t, MLA_HEADS * 128), F32)],
        compiler_params=_params(("parallel", "arbitrary")),
    )(ii, jj, q, kv, kr)


def _mla_bwd(q, kv, kr, o, lse, dattn):
    t = q.shape[0]
    blk = _tile(t, ATTN_BLOCK)
    nb = t // blk
    hps, dq_, dv_ = HEADS_PER_STEP, MLA_QK_PAD, MLA_V
    do_off = RET_WIDTH // (hps * dv_)

    def body(ii_ref, jj_ref, q_ref, kv_ref, kr_ref, o_ref, lse_ref, do_ref, dq_ref, dk_ref, dv_ref, dk_acc, dv_acc):
        pair = pl.program_id(1)
        i, j = ii_ref[pair], jj_ref[pair]

        @pl.when(pair == 0)
        def _():
            dq_ref[...] = jnp.zeros_like(dq_ref)

        @pl.when(i == j)
        def _():
            dk_acc[...] = jnp.zeros_like(dk_acc)
            dv_acc[...] = jnp.zeros_like(dv_acc)

        def step(masked):
            krb = kr_ref[...]
            rows = pl.ds(pl.multiple_of(i * blk, blk), blk)
            for h in range(hps):
                qcols, vcols = slice(h * dq_, (h + 1) * dq_), slice(h * dv_, (h + 1) * dv_)
                qb = q_ref[:, qcols]
                kb = jnp.concatenate([kv_ref[:, h * dq_:h * dq_ + MLA_NOPE], krb], axis=1)
                vb = kv_ref[:, h * dq_ + MLA_NOPE:(h + 1) * dq_]
                dof = do_ref[:, vcols]
                dob = dof.astype(MXU_DTYPE)
                s = lax.dot_general(qb, kb, _NT, preferred_element_type=F32)
                if masked:
                    s = jnp.where(_mla_mask(blk), s, _NEG)
                lse2 = lse_ref[:, vcols] * _LOG2_E
                p = jnp.exp2(s * _MLA_SCALE_LOG2 - jnp.tile(lse2, (1, blk // 128)))
                delta = jnp.sum(dof * o_ref[:, vcols].astype(F32), axis=-1, keepdims=True)
                dv_acc[h] += lax.dot_general(p.astype(MXU_DTYPE), dob, _TN, preferred_element_type=F32)
                dp = lax.dot_general(dob, vb, _NT, preferred_element_type=F32)
                ds = (p * (dp - delta) * _MLA_SCALE).astype(MXU_DTYPE)
                dq_ref[rows, qcols] += lax.dot_general(ds, kb, _NN, preferred_element_type=F32)
                dk_acc[h] += lax.dot_general(ds, qb, _TN, preferred_element_type=F32)

        @pl.when(i > j)
        def _():
            step(False)

        @pl.when(i == j)
        def _():
            step(True)

        @pl.when(i == nb - 1)
        def _():
            for h in range(hps):
                dk_ref[:, h * dq_:(h + 1) * dq_] = dk_acc[h]
                dv_ref[:, h * dv_:(h + 1) * dv_] = dv_acc[h].astype(dv_ref.dtype)

    qmap = lambda off: (lambda h, p, ii, jj: (ii[p], h + off))
    kmap = lambda h, p, ii, jj: (jj[p], h)
    ii, jj = _causal_pairs(nb, query_major=False)
    return pl.pallas_call(
        body, name="mla_bwd",
        grid_spec=pltpu.PrefetchScalarGridSpec(
            num_scalar_prefetch=2, grid=(MLA_HEADS // hps, ii.shape[0]),
            in_specs=[pl.BlockSpec((blk, hps * dq_), qmap(0)), pl.BlockSpec((blk, hps * dq_), kmap),
                      pl.BlockSpec((blk, 128), lambda h, p, ii, jj: (jj[p], 0)),
                      pl.BlockSpec((blk, hps * dv_), qmap(0)), pl.BlockSpec((blk, hps * dv_), qmap(0)),
                      pl.BlockSpec((blk, hps * dv_), qmap(do_off))],
            out_specs=[pl.BlockSpec((t, hps * dq_), lambda h, p, ii, jj: (0, h), pipeline_mode=pl.Buffered(1)),
                       pl.BlockSpec((blk, hps * dq_), kmap), pl.BlockSpec((blk, hps * dv_), kmap)],
            scratch_shapes=[pltpu.VMEM((hps, blk, dq_), F32), pltpu.VMEM((hps, blk, dv_), F32)]),
        out_shape=[jax.ShapeDtypeStruct((t, Q_WIDTH_PAD), F32), jax.ShapeDtypeStruct((t, Q_WIDTH_PAD), F32),
                   jax.ShapeDtypeStruct((t, MLA_HEADS * MLA_V), MXU_DTYPE)],
        compiler_params=_params(("parallel", "arbitrary")),
    )(ii, jj, q, kv, kr, o, lse, dattn)


def _mla_kv_grad(dk, dv, cos_m, sin_m):
    t = dk.shape[0]
    tr = _tile(t, 256, 8)

    def body(dk_ref, dv_ref, c_ref, s_ref, dkv_ref, dkr_ref):
        acc = jnp.zeros((tr, 128), F32)
        for h in range(MLA_HEADS):
            dkv_ref[:, h * 256:h * 256 + 128] = dk_ref[:, h * 256:h * 256 + 128].astype(dkv_ref.dtype)
            dkv_ref[:, h * 256 + 128:h * 256 + 256] = dv_ref[:, h * 128:(h + 1) * 128].astype(dkv_ref.dtype)
            acc = acc + dk_ref[:, h * 256 + 128:h * 256 + 256]
        dkr_ref[...] = _rope64(acc, c_ref[...], s_ref[...], inverse=True).astype(dkr_ref.dtype)

    row = lambda w: pl.BlockSpec((tr, w), lambda i: (i, 0))
    return pl.pallas_call(
        body, name="mla_kv_grad", grid=(t // tr,),
        in_specs=[row(Q_WIDTH_PAD), row(MLA_HEADS * MLA_V), row(128), row(128)],
        out_specs=[row(Q_WIDTH_PAD), row(128)],
        out_shape=[jax.ShapeDtypeStruct((t, Q_WIDTH_PAD), MXU_DTYPE), jax.ShapeDtypeStruct((t, 128), MXU_DTYPE)],
        compiler_params=_params(("parallel",)),
    )(dk, dv, cos_m, sin_m)


_FFN_COLS = 256
_FFN_ROWS = 256


def _shift_down(cur, prev8, n):
    out = pltpu.roll(cur, n, 0)
    head = out[:8]
    row = lax.broadcasted_iota(jnp.int32, head.shape, 0)
    for r in range(n):
        head = jnp.where(row == r, prev8[8 - n + r:8 - n + r + 1, :], head)
    return jnp.concatenate([head, out[8:]], axis=0)


def _shift_up(cur, next8, n):
    rows = cur.shape[0]
    out = pltpu.roll(cur, rows - n, 0)
    tail = out[rows - 8:]
    row = lax.broadcasted_iota(jnp.int32, tail.shape, 0)
    for r in range(n):
        tail = jnp.where(row == 8 - n + r, next8[r:r + 1, :], tail)
    return jnp.concatenate([out[:rows - 8], tail], axis=0)


def _conv_pre(g_ref, cw_ref, cb_ref, c, rc):
    r0 = pl.multiple_of(c * rc, rc)
    cur = g_ref[pl.ds(r0, rc), :].astype(F32)
    prev16 = g_ref[pl.ds(pl.multiple_of(jnp.maximum(r0 - 16, 0), 16), 16), :].astype(F32)
    prev8 = jnp.where(c > 0, prev16[8:], 0.0)
    s1, s2 = _shift_down(cur, prev8, 1), _shift_down(cur, prev8, 2)
    a = cb_ref[...] + cw_ref[2:3, :] * cur + cw_ref[1:2, :] * s1 + cw_ref[0:1, :] * s2
    return r0, cur, s1, s2, a


def _ffn_act_fwd(gpre, u, cw, cb):
    t, f = gpre.shape
    tc = _tile(f, _FFN_COLS)
    rc = _tile(t, _FFN_ROWS, 8)

    def body(g_ref, u_ref, cw_ref, cb_ref, o_ref):
        def chunk(c, carry):
            r0, _, _, _, a = _conv_pre(g_ref, cw_ref, cb_ref, c, rc)
            o_ref[pl.ds(r0, rc), :] = (a * _sigmoid(a) * u_ref[pl.ds(r0, rc), :]).astype(o_ref.dtype)
            return carry
        lax.fori_loop(0, t // rc, chunk, 0)

    col = pl.BlockSpec((t, tc), lambda j: (0, j))
    return pl.pallas_call(
        body, name="ffn_act_fwd", grid=(f // tc,),
        in_specs=[col, col, pl.BlockSpec((CONV_WIDTH, tc), lambda j: (0, j)), pl.BlockSpec((1, tc), lambda j: (0, j))],
        out_specs=col, out_shape=jax.ShapeDtypeStruct((t, f), MXU_DTYPE),
        compiler_params=_params(("parallel",)),
    )(gpre, u, cw, cb)


def _ffn_act_bwd(gpre, u, dact, cw, cb):
    t, f = gpre.shape
    tc = _tile(f, _FFN_COLS)
    rc = _tile(t, _FFN_ROWS, 8)
    nc = t // rc

    def body(g_ref, u_ref, d_ref, cw_ref, cb_ref, dg_ref, du_ref, dcw_ref, dcb_ref, da_ref):
        def chunk(c, carry):
            w0, w1, w2, b = carry
            r0, cur, s1, s2, a = _conv_pre(g_ref, cw_ref, cb_ref, c, rc)
            sg = _sigmoid(a)
            d = d_ref[pl.ds(r0, rc), :].astype(F32)
            du_ref[pl.ds(r0, rc), :] = (d * (a * sg)).astype(du_ref.dtype)
            da = d * u_ref[pl.ds(r0, rc), :].astype(F32) * (sg * (1.0 + a * (1.0 - sg)))
            da_ref[pl.ds(r0, rc), :] = da
            return (w0 + jnp.sum(da * s2, axis=0, keepdims=True), w1 + jnp.sum(da * s1, axis=0, keepdims=True),
                    w2 + jnp.sum(da * cur, axis=0, keepdims=True), b + jnp.sum(da, axis=0, keepdims=True))
        z = jnp.zeros((1, tc), F32)
        w0, w1, w2, b = lax.fori_loop(0, nc, chunk, (z, z, z, z))
        dcw_ref[0:1, :] = w0
        dcw_ref[1:2, :] = w1
        dcw_ref[2:3, :] = w2
        dcb_ref[...] = b

        def chunk2(c, carry):
            r0 = pl.multiple_of(c * rc, rc)
            cur = da_ref[pl.ds(r0, rc), :]
            nxt = da_ref[pl.ds(pl.multiple_of(jnp.minimum(r0 + rc, t - 8), 8), 8), :]
            nxt = jnp.where(c < nc - 1, nxt, 0.0)
            dg = cw_ref[2:3, :] * cur + cw_ref[1:2, :] * _shift_up(cur, nxt, 1) + cw_ref[0:1, :] * _shift_up(cur, nxt, 2)
            dg_ref[pl.ds(r0, rc), :] = dg.astype(dg_ref.dtype)
            return carry
        lax.fori_loop(0, nc, chunk2, 0)

    col = pl.BlockSpec((t, tc), lambda j: (0, j))
    w3 = pl.BlockSpec((CONV_WIDTH, tc), lambda j: (0, j))
    w1 = pl.BlockSpec((1, tc), lambda j: (0, j))
    low = jax.ShapeDtypeStruct((t, f), MXU_DTYPE)
    return pl.pallas_call(
        body, name="ffn_act_bwd", grid=(f // tc,),
        in_specs=[col, col, col, w3, w1], out_specs=[col, col, w3, w1],
        out_shape=[low, low, jax.ShapeDtypeStruct((CONV_WIDTH, f), F32), jax.ShapeDtypeStruct((1, f), F32)],
        scratch_shapes=[pltpu.VMEM((t, tc), F32)],
        compiler_params=_params(("parallel",)),
    )(gpre, u, dact, cw, cb)


def _head_fwd_bwd(h2, glin, pp, target, g_final):
    t, d = h2.shape
    tr = _tile(t, 128, 8)

    def body(h_ref, gl_ref, pp_ref, t_ref, g_ref, loss_ref, dh_ref, dgl_ref, dpp_ref, dg_ref):
        gate = _sigmoid(gl_ref[...])
        ppv = pp_ref[...]
        h3 = h_ref[...] + gate * ppv
        r = lax.rsqrt(jnp.mean(h3 * h3, axis=-1, keepdims=True) + EPS)
        yh = h3 * r
        g = g_ref[...]
        diff = yh * g - t_ref[...]
        lpart = 0.5 * jnp.sum(jnp.mean(diff * diff, axis=-1, keepdims=True), axis=0, keepdims=True)
        dy = diff * (1.0 / d)
        dyg = dy * g
        dh3 = r * dyg - h3 * (r * r * r) * jnp.mean(dyg * h3, axis=-1, keepdims=True)
        dh_ref[...] = dh3
        dgl_ref[...] = (dh3 * ppv * gate * (1.0 - gate)).astype(dgl_ref.dtype)
        dpp_ref[...] = (dh3 * gate).astype(dpp_ref.dtype)
        dgp = jnp.sum(dy * yh, axis=0, keepdims=True)

        @pl.when(pl.program_id(0) == 0)
        def _():
            loss_ref[...] = jnp.broadcast_to(lpart, loss_ref.shape)
            dg_ref[...] = dgp

        @pl.when(pl.program_id(0) > 0)
        def _():
            loss_ref[...] += jnp.broadcast_to(lpart, loss_ref.shape)
            dg_ref[...] += dgp

    row = pl.BlockSpec((tr, d), lambda i: (i, 0))
    vec = pl.BlockSpec((1, d), lambda i: (0, 0))
    low = jax.ShapeDtypeStruct((t, d), MXU_DTYPE)
    return pl.pallas_call(
        body, name="head_fwd_bwd", grid=(t // tr,),
        in_specs=[row, row, row, row, vec],
        out_specs=[pl.BlockSpec((8, 128), lambda i: (0, 0)), row, row, row, vec],
        out_shape=[jax.ShapeDtypeStruct((8, 128), F32), jax.ShapeDtypeStruct((t, d), F32), low, low,
                   jax.ShapeDtypeStruct((1, d), F32)],
        compiler_params=_params(("arbitrary",)),
    )(h2, glin, pp, target, g_final)


class _Order:
    def __init__(self):
        self.last = None

    def tie(self, x):
        return x if self.last is None else lax.optimization_barrier((x, self.last))[0]

    def run(self, fn, first, *args, **kwargs):
        out = fn(self.tie(first), *args, **kwargs)
        self.last = out[0] if isinstance(out, (tuple, list)) else out
        return out


def _local_step(x, p, target, vec, ops):
    t = x.shape[0]
    cos_r, sin_r, cos_m, sin_m = _rope_tables(t)
    lg = _ret_log_gamma()
    low = MXU_DTYPE
    run = ops.order.run
    w = ops.weight

    ops.start_gather("w_in", halves=True)
    hn1 = run(_rms_fwd, x, vec["g_attn"], name="rms1_fwd")
    for n in ("w_uq", "w_ukv", "w_o"):
        ops.start_gather(n, after=hn1)
    half = x.shape[1] // 2
    proj = run(_matmul, hn1[:, :half], w("w_in/0"), tb=True, name="mm_proj_a")
    proj = run(_matmul, hn1[:, half:], w("w_in/1"), tb=True, name="mm_proj_b", add=proj)
    ops.start_gather("w_ffn_gate", after=proj)
    rq, rk, rv = run(_ret_prep, proj, cos_r, sin_r)
    ops.start_gather("w_ffn_up", after=rq)
    c0 = 4 * RET_WIDTH
    cq = proj[:, c0:c0 + Q_LORA]
    ckv = proj[:, c0 + Q_LORA:c0 + Q_LORA + KV_LORA]
    kr_in = proj[:, c0 + Q_LORA + KV_LORA:c0 + Q_LORA + KV_LORA + 128]
    cqn, kvn, kr = run(_mla_prep, cq, ckv, kr_in, vec["g_q_lora"], vec["g_kv_lora"], cos_m, sin_m)
    ops.start_gather("w_ffn_down", after=cqn)
    q_lin = run(_matmul, cqn, w("w_uq"), tb=True, name="mm_q")
    q = run(_mla_q_rope, q_lin, cos_m, sin_m, inverse=False, name="mla_q_rope")
    kv = run(_matmul, kvn, w("w_ukv"), tb=True, name="mm_kv", out_dtype=low)
    mo, lse = run(_mla_fwd, q, kv, kr)
    ops.start_gather("w_ple_gate", after=mo)
    ops.start_gather("w_ple_proj", after=mo)
    ret_raw, ro = run(_ret_fwd, rq, rk, rv, proj, lg)
    attn = jnp.concatenate([ro, mo], axis=1)
    h1 = run(_matmul, attn, w("w_o"), name="mm_o", add=x)
    hn2 = run(_rms_fwd, h1, vec["g_ffn"], name="rms2_fwd")
    gpre = run(_matmul, hn2, w("w_ffn_gate"), tb=True, name="mm_gate", out_dtype=low)
    u = run(_matmul, hn2, w("w_ffn_up"), tb=True, name="mm_up", out_dtype=low)
    act = run(_ffn_act_fwd, gpre, u, vec["conv_w"], vec["conv_b"])
    h2 = run(_matmul, act, w("w_ffn_down"), name="mm_down", add=h1)
    hn3 = run(_rms_fwd, h2, vec["g_ple"], name="rms3_fwd")
    glin = run(_matmul, hn3, w("w_ple_gate"), name="mm_ple_gate")
    p_low = p.astype(low)
    pp = run(_matmul, p_low, w("w_ple_proj"), tb=True, name="mm_ple_proj")
    loss_part, dh3, dglin, dpp, dg_final = run(_head_fwd_bwd, h2, glin, pp, target, vec["g_final"])

    ops.grad("w_ple_proj", run(_matmul, dpp, p_low, ta=True, name="mm_d_ple_proj", out_dtype=low))
    ops.grad("w_ple_gate", run(_matmul, hn3, dglin, ta=True, name="mm_d_ple_gate", out_dtype=low))
    dhn3 = run(_matmul, dglin, w("w_ple_gate"), tb=True, name="mm_dhn3", out_dtype=low)
    dh2, dh2_low, dg_ple = run(_rms_bwd, h2, dhn3, vec["g_ple"], dh3, name="rms3_bwd", low_copy=True)
    ops.reduce_add("w_ple_proj")
    ops.reduce_add("w_ple_gate")
    ops.grad("w_ffn_down", run(_matmul, act, dh2_low, ta=True, name="mm_d_down", out_dtype=low))
    dact = run(_matmul, dh2_low, w("w_ffn_down"), tb=True, name="mm_dact", out_dtype=low)
    ops.reduce_add("w_ffn_down")
    dgpre, du, dconv_w, dconv_b = run(_ffn_act_bwd, gpre, u, dact, vec["conv_w"], vec["conv_b"])
    ops.update("w_ple_proj")
    ops.update("w_ple_gate")
    ops.grad("w_ffn_gate", run(_matmul, dgpre, hn2, ta=True, name="mm_d_gate", out_dtype=low))
    ops.grad("w_ffn_up", run(_matmul, du, hn2, ta=True, name="mm_d_up", out_dtype=low))
    ops.reduce_add("w_ffn_gate")
    dhn2 = run(_matmul, dgpre, w("w_ffn_gate"), name="mm_dhn2_a")
    ops.reduce_add("w_ffn_up")
    dhn2 = run(_matmul, du, w("w_ffn_up"), name="mm_dhn2_b", add=dhn2, out_dtype=low)
    dh1, dh1_low, dg_ffn = run(_rms_bwd, h1, dhn2, vec["g_ffn"], dh2, name="rms2_bwd", low_copy=True)
    ops.update("w_ffn_down")
    ops.grad("w_o", run(_matmul, attn, dh1_low, ta=True, name="mm_d_o", out_dtype=low))
    dattn = run(_matmul, dh1_low, w("w_o"), tb=True, name="mm_dattn")
    ops.reduce_add("w_o")

    dq_r, dk_full, dv = run(_mla_bwd, q, kv, kr, mo, lse, dattn)
    ops.update("w_ffn_gate")
    dq_lin = run(_mla_q_rope, dq_r, cos_m, sin_m, inverse=True, name="mla_q_unrope")
    dkv, dkr = run(_mla_kv_grad, dk_full, dv, cos_m, sin_m)
    ops.grad("w_uq", run(_matmul, dq_lin, cqn, ta=True, name="mm_d_uq", out_dtype=low))
    dcqn = run(_matmul, dq_lin, w("w_uq"), name="mm_dcqn")
    ops.grad("w_ukv", run(_matmul, dkv, kvn, ta=True, name="mm_d_ukv", out_dtype=low))
    dkvn = run(_matmul, dkv, w("w_ukv"), name="mm_dkvn")
    dcq, dcq_low, dg_q = run(_rms_bwd, cq, dcqn, vec["g_q_lora"], None, name="rmsq_bwd", low_copy=True)
    dckv, dckv_low, dg_kv = run(_rms_bwd, ckv, dkvn, vec["g_kv_lora"], None, name="rmskv_bwd", low_copy=True)
    ops.reduce_add("w_uq")
    ops.reduce_add("w_ukv")

    do_ret, drg = run(_ret_gate_bwd, ret_raw, proj, dattn)
    dq_ret, dk_ret, drv = run(_ret_bwd, rq, rk, rv, do_ret, lg)
    drq, drk = run(_ret_unrope, dq_ret, dk_ret, cos_r, sin_r)

    pad = jnp.zeros((t, IN_WIDTH_PAD - IN_WIDTH - 64), low)
    dproj = jnp.concatenate([drq, drk, drv, drg, dcq_low, dckv_low, dkr, pad], axis=1)
    ops.grad("w_in", run(_matmul, dproj, hn1, ta=True, name="mm_d_in", out_dtype=low))
    for n in ("w_ffn_up", "w_o", "w_uq", "w_ukv"):
        ops.update(n)
    ops.reduce_add("w_in")
    dhn1 = jnp.concatenate([run(_matmul, dproj, w(f"w_in/{part}"), name=f"mm_dhn1_{part}", out_dtype=low)
                            for part in range(2)], axis=1)
    grad_x, dg_attn = run(_rms_bwd, x, dhn1, vec["g_attn"], dh1, name="rms1_bwd")

    gs = {"g_attn": dg_attn, "g_q_lora": dg_q, "g_kv_lora": dg_kv, "g_ffn": dg_ffn, "conv_w": dconv_w,
          "conv_b": dconv_b, "g_ple": dg_ple, "g_final": dg_final}
    return loss_part, grad_x, gs


_COL_SHARDED = ("w_in", "w_uq", "w_ukv", "w_ffn_gate", "w_ffn_up", "w_ple_proj")
_FFN_SHARD = D_FF // N_DEV
_FFN_SHARD_PAD = D_FF_PAD // N_DEV
_HEADS_PER_SHARD = MLA_HEADS // N_DEV
_QK = MLA_NOPE + MLA_ROPE


def _pad_rows(name, a):
    lead = a.shape[:-2]
    if name == "w_uq":
        a = a.reshape(lead + (_HEADS_PER_SHARD, _QK, a.shape[-1]))
        a = jnp.pad(a, [(0, 0)] * len(lead) + [(0, 0), (0, MLA_QK_PAD - _QK), (0, 0)])
        return a.reshape(lead + (_HEADS_PER_SHARD * MLA_QK_PAD, a.shape[-1]))
    if name in ("w_ffn_gate", "w_ffn_up", "w_ffn_down"):
        return jnp.pad(a, [(0, 0)] * len(lead) + [(0, _FFN_SHARD_PAD - _FFN_SHARD), (0, 0)])
    return a


def _unpad_rows(name, a):
    lead = a.shape[:-2]
    if name == "w_uq":
        a = a.reshape(lead + (_HEADS_PER_SHARD, MLA_QK_PAD, a.shape[-1]))[..., :_QK, :]
        return a.reshape(lead + (_HEADS_PER_SHARD * _QK, a.shape[-1]))
    if name in ("w_ffn_gate", "w_ffn_up", "w_ffn_down"):
        return a[..., :_FFN_SHARD, :]
    return a


def _rows_view(name, a):
    return jnp.swapaxes(a, 0, 1) if name in _COL_SHARDED else a


def _shard_payload(name, shard):
    return _pad_rows(name, _rows_view(name, shard).astype(MXU_DTYPE))


def _full_from_gathered(name, g):
    full = g.reshape(g.shape[0] * g.shape[1], g.shape[2])
    if name == "w_in":
        full = jnp.pad(full, ((0, IN_WIDTH_PAD - IN_WIDTH), (0, 0)))
    return full


def _grad_chunks(name, gfull):
    if name == "w_in":
        gfull = gfull[:IN_WIDTH]
    return gfull.reshape(N_DEV, gfull.shape[0] // N_DEV, gfull.shape[1])


def _ffn_vec_layout(a):
    a = a.reshape(a.shape[0], N_DEV, _FFN_SHARD)
    return jnp.pad(a, ((0, 0), (0, 0), (0, _FFN_SHARD_PAD - _FFN_SHARD))).reshape(a.shape[0], D_FF_PAD)


def _ffn_vec_shards(a):
    return a.reshape(a.shape[0], N_DEV, _FFN_SHARD_PAD)[:, :, :_FFN_SHARD]


_MESH = pl.DeviceIdType.MESH
_ANY = pl.BlockSpec(memory_space=pl.ANY)


def _place():
    x, y, c = lax.axis_index("x"), lax.axis_index("y"), lax.axis_index("c")
    chips = [(1 - x, y), (x, 1 - y), (1 - x, 1 - y)]
    return x, y, c, chips


def _handshake(peers):
    barrier = pltpu.get_barrier_semaphore()
    for peer in peers:
        pl.semaphore_signal(barrier, inc=1, device_id=peer, device_id_type=_MESH)
    pl.semaphore_wait(barrier, len(peers))


_SEQUENCER = dict(axis_name="seq", num_cores=1)
_AG_COLLECTIVE_ID = 1
_RS_SIBLING_COLLECTIVE_ID = 2
_RS_CHIPS_COLLECTIVE_ID = 3


def _all_gather_seq(shard, *, name):
    def body(x_ref, out_ref, send_sems, recv_sems, local_sem):
        x, y, c, chips = _place()
        sibling = (x, y, 1 - c)
        _handshake([sibling] + [(*chip, c) for chip in chips])

        def slot(px, py, pc):
            return out_ref.at[4 * px + 2 * py + pc]

        def copy(k, block, to, src=None):
            return pltpu.make_async_remote_copy(
                src_ref=slot(*block) if src is None else src, dst_ref=slot(*block),
                send_sem=send_sems.at[k], recv_sem=recv_sems.at[k], device_id=to, device_id_type=_MESH)

        mine = pltpu.make_async_copy(x_ref, slot(x, y, c), local_sem)
        mine.start()
        first = [copy(0, (x, y, c), sibling, src=x_ref)]
        first += [copy(1 + j, (x, y, c), (*chip, c), src=x_ref) for j, chip in enumerate(chips)]
        for cp in first:
            cp.start()
        passed = [copy(4 + j, (*chip, c), sibling) for j, chip in enumerate(chips)]
        for j, chip in enumerate(chips):
            copy(1 + j, (*chip, c), (x, y, c)).wait_recv()
            passed[j].start()
        copy(0, sibling, (x, y, c)).wait_recv()
        for j, chip in enumerate(chips):
            copy(4 + j, (*chip, 1 - c), (x, y, c)).wait_recv()
        for cp in first + passed:
            cp.wait_send()
        mine.wait()

    return pl.kernel(
        body, out_type=jax.ShapeDtypeStruct((N_DEV,) + shard.shape, shard.dtype),
        mesh=plsc.ScalarSubcoreMesh(**_SEQUENCER), name=name,
        scratch_types=[pltpu.SemaphoreType.DMA((7,)), pltpu.SemaphoreType.DMA((7,)), pltpu.SemaphoreType.DMA(())],
        compiler_params=pltpu.CompilerParams(collective_id=_AG_COLLECTIVE_ID),
    )(shard)


def _exchange_sibling(g, *, name):
    def body(g_ref, out_ref, send_sems, recv_sems):
        x, y, c, _ = _place()
        sibling = (x, y, 1 - c)
        _handshake([sibling])
        copies = []
        for chip in range(4):
            cp = pltpu.make_async_remote_copy(
                src_ref=g_ref.at[2 * chip + (1 - c)], dst_ref=out_ref.at[chip],
                send_sem=send_sems.at[chip], recv_sem=recv_sems.at[chip], device_id=sibling, device_id_type=_MESH)
            cp.start()
            copies.append(cp)
        for cp in copies:
            cp.wait_recv()
        for cp in copies:
            cp.wait_send()

    return pl.kernel(
        body, out_type=jax.ShapeDtypeStruct((4,) + g.shape[1:], g.dtype),
        mesh=plsc.ScalarSubcoreMesh(**_SEQUENCER), name=name,
        scratch_types=[pltpu.SemaphoreType.DMA((4,)), pltpu.SemaphoreType.DMA((4,))],
        compiler_params=pltpu.CompilerParams(collective_id=_RS_SIBLING_COLLECTIVE_ID),
    )(g)


def _add_sibling(g, recv, *, name):
    _, r, cdim = g.shape
    tr, tc = _tile_2d(r, cdim, 6)
    g4 = g.reshape(4, 2, r, cdim)
    core = lax.axis_index("c").astype(jnp.int32).reshape(1)

    def body(c_ref, g_ref, r_ref, o_ref):
        o_ref[...] = (g_ref[...].astype(F32) + r_ref[...].astype(F32)).astype(o_ref.dtype)

    return pl.pallas_call(
        body, name=name,
        grid_spec=pltpu.PrefetchScalarGridSpec(
            num_scalar_prefetch=1, grid=(4, r // tr, cdim // tc),
            in_specs=[pl.BlockSpec((None, None, tr, tc), lambda ch, i, j, c_ref: (ch, c_ref[0], i, j)),
                      pl.BlockSpec((None, tr, tc), lambda ch, i, j, c_ref: (ch, i, j))],
            out_specs=pl.BlockSpec((None, tr, tc), lambda ch, i, j, c_ref: (ch, i, j))),
        out_shape=jax.ShapeDtypeStruct((4, r, cdim), g.dtype),
        compiler_params=_params(("parallel", "parallel", "parallel")),
    )(core, g4, recv)


def _exchange_chips(pch, *, name):
    def body(p_ref, out_ref, send_sems, recv_sems, local_sem):
        x, y, c, chips = _place()
        _handshake([(*chip, c) for chip in chips])
        me = 2 * x + y
        mine = pltpu.make_async_copy(p_ref.at[me], out_ref.at[me], local_sem)
        mine.start()
        copies = []
        for j, (px, py) in enumerate(chips):
            cp = pltpu.make_async_remote_copy(
                src_ref=p_ref.at[2 * px + py], dst_ref=out_ref.at[me],
                send_sem=send_sems.at[j], recv_sem=recv_sems.at[j], device_id=(px, py, c), device_id_type=_MESH)
            cp.start()
            copies.append(cp)
        for j, (px, py) in enumerate(chips):
            pltpu.make_async_remote_copy(
                src_ref=p_ref.at[me], dst_ref=out_ref.at[2 * px + py],
                send_sem=send_sems.at[j], recv_sem=recv_sems.at[j], device_id=(px, py, c), device_id_type=_MESH).wait_recv()
        for cp in copies:
            cp.wait_send()
        mine.wait()

    return pl.kernel(
        body, out_type=jax.ShapeDtypeStruct(pch.shape, pch.dtype),
        mesh=plsc.ScalarSubcoreMesh(**_SEQUENCER), name=name,
        scratch_types=[pltpu.SemaphoreType.DMA((3,)), pltpu.SemaphoreType.DMA((3,)), pltpu.SemaphoreType.DMA(())],
        compiler_params=pltpu.CompilerParams(collective_id=_RS_CHIPS_COLLECTIVE_ID),
    )(pch)


def _all_reduce_small(v, *, name):
    r = v.shape[0]

    def body(x_ref, out_ref, buf_ref, send_sems, recv_sems):
        x, y, c, chips = _place()
        sibling = (x, y, 1 - c)

        def slot(px, py, pc):
            return buf_ref.at[4 * px + 2 * py + pc]

        def copy(k, block, to, src=None):
            return pltpu.make_async_remote_copy(
                src_ref=slot(*block) if src is None else src, dst_ref=slot(*block),
                send_sem=send_sems.at[k], recv_sem=recv_sems.at[k], device_id=to, device_id_type=_MESH)

        first = [copy(0, (x, y, c), sibling, src=x_ref)]
        first += [copy(1 + j, (x, y, c), (*chip, c), src=x_ref) for j, chip in enumerate(chips)]
        for cp in first:
            cp.start()
        buf_ref[4 * x + 2 * y + c] = x_ref[...]
        passed = [copy(4 + j, (*chip, c), sibling) for j, chip in enumerate(chips)]
        for j, chip in enumerate(chips):
            copy(1 + j, (*chip, c), (x, y, c)).wait_recv()
            passed[j].start()
        copy(0, sibling, (x, y, c)).wait_recv()
        for j, chip in enumerate(chips):
            copy(4 + j, (*chip, 1 - c), (x, y, c)).wait_recv()
        for cp in first + passed:
            cp.wait_send()
        total = buf_ref[0]
        for k in range(1, N_DEV):
            total = total + buf_ref[k]
        out_ref[...] = total

    vm = pl.BlockSpec(memory_space=pltpu.VMEM)
    return pl.pallas_call(
        body, name=name, out_shape=jax.ShapeDtypeStruct(v.shape, v.dtype),
        in_specs=[vm], out_specs=vm,
        scratch_shapes=[pltpu.VMEM((N_DEV,) + v.shape, v.dtype), pltpu.SemaphoreType.DMA((7,)),
                        pltpu.SemaphoreType.DMA((7,))],
    )(v)


_ELEMENTWISE_VMEM = 24 * 1024 * 1024


def _tile_2d(r, c, n_arrays):
    per_block = _ELEMENTWISE_VMEM // (8 * n_arrays)
    tr = _tile(r, max(16, per_block // max(c, 128)), 16)
    if tr * c <= per_block:
        return tr, c
    return r, _tile(c, max(128, (per_block // r) // 128 * 128))


def _adam_math(w, g, m, v):
    m = ADAM_B1 * m + (1.0 - ADAM_B1) * g
    v = ADAM_B2 * v + (1.0 - ADAM_B2) * jnp.square(g)
    m_hat = m / (1.0 - ADAM_B1 ** ADAM_STEP)
    v_hat = v / (1.0 - ADAM_B2 ** ADAM_STEP)
    delta = -ADAM_LR * (m_hat / (jnp.sqrt(v_hat) + ADAM_EPS) + ADAM_WD * w)
    return delta, m, v


def _adam(w, g, m, v, *, name, parts=None):
    r, cdim = w.shape
    tr, tc = _tile_2d(r, cdim, 8)

    def body(w_ref, g_ref, m_ref, v_ref, go_ref, d_ref, mo_ref, vo_ref):
        if parts is None:
            g = g_ref[...]
        else:
            g = g_ref[0].astype(F32)
            for k in range(1, parts):
                g = g + g_ref[k].astype(F32)
        delta, m, v = _adam_math(w_ref[...], g, m_ref[...], v_ref[...])
        go_ref[...] = g
        d_ref[...] = delta
        mo_ref[...] = m
        vo_ref[...] = v

    blk = pl.BlockSpec((tr, tc), lambda i, j: (i, j))
    gblk = blk if parts is None else pl.BlockSpec((parts, tr, tc), lambda i, j: (0, i, j))
    out = jax.ShapeDtypeStruct((r, cdim), F32)
    return pl.pallas_call(
        body, name=name, grid=(r // tr, cdim // tc), in_specs=[blk, gblk, blk, blk], out_specs=[blk] * 4,
        out_shape=[out] * 4, compiler_params=_params(("parallel", "parallel")),
    )(w, g, m, v)


_BIG = ("w_in", "w_uq", "w_ukv", "w_o", "w_ffn_gate", "w_ffn_up", "w_ffn_down", "w_ple_gate", "w_ple_proj")
_WEIGHTS = ("w_in", "g_attn", "g_q_lora", "g_kv_lora", "w_uq", "w_ukv", "w_o", "g_ffn", "w_ffn_gate", "w_ffn_up",
            "conv_w", "conv_b", "w_ffn_down", "g_ple", "w_ple_gate", "w_ple_proj", "g_final")
_SMALL_PACK = (("g_attn", 1, D_MODEL), ("g_q_lora", 1, Q_LORA), ("g_kv_lora", 1, KV_LORA), ("g_ffn", 1, D_MODEL),
               ("conv_w", CONV_WIDTH, D_FF_PAD), ("conv_b", 1, D_FF_PAD), ("g_ple", 1, D_MODEL), ("g_final", 1, D_MODEL))


def _pack_small(gs):
    flat = jnp.concatenate([gs[n].reshape(-1) for n, _, _ in _SMALL_PACK])
    rows = -(-flat.shape[0] // 128)
    rows = -(-rows // 8) * 8
    return jnp.pad(flat, (0, rows * 128 - flat.shape[0])).reshape(rows, 128)


def _unpack_small(packed):
    flat = packed.reshape(-1)
    out, off = {}, 0
    for n, r, c in _SMALL_PACK:
        out[n] = flat[off:off + r * c].reshape(r, c)
        off += r * c
    return out


def kernel(x, p, w_in, g_attn, g_q_lora, g_kv_lora, w_uq, w_ukv, w_o, g_ffn, w_ffn_gate, w_ffn_up, conv_w, conv_b, w_ffn_down, g_ple, w_ple_gate, w_ple_proj, g_final, loss_target, m_w_in, m_g_attn, m_g_q_lora, m_g_kv_lora, m_w_uq, m_w_ukv, m_w_o, m_g_ffn, m_w_ffn_gate, m_w_ffn_up, m_conv_w, m_conv_b, m_w_ffn_down, m_g_ple, m_w_ple_gate, m_w_ple_proj, m_g_final, v_w_in, v_g_attn, v_g_q_lora, v_g_kv_lora, v_w_uq, v_w_ukv, v_w_o, v_g_ffn, v_w_ffn_gate, v_w_ffn_up, v_conv_w, v_conv_b, v_w_ffn_down, v_g_ple, v_w_ple_gate, v_w_ple_proj, v_g_final):
    given = dict(locals())
    wts = {n: given[n] for n in _WEIGHTS}
    mom = {n: given["m_" + n] for n in _WEIGHTS}
    var = {n: given["v_" + n] for n in _WEIGHTS}
    me = (4 * lax.axis_index("x") + 2 * lax.axis_index("y") + lax.axis_index("c")).astype(jnp.int32)
    ops = _ShardedWeights(wts, mom, var)

    conv_full = _all_gather_seq(conv_w, name="ag_conv_w")[:, 0].transpose(1, 0, 2).reshape(CONV_WIDTH, D_FF)
    vec = {"g_attn": g_attn, "g_q_lora": g_q_lora, "g_kv_lora": g_kv_lora, "g_ffn": g_ffn, "g_ple": g_ple,
           "g_final": g_final[None, :], "conv_w": _ffn_vec_layout(conv_full), "conv_b": _ffn_vec_layout(conv_b)}

    loss_part, grad_x, gs = _local_step(x[0], p[0, 0], loss_target[0], vec, ops)
    loss = lax.psum(loss_part[0, 0], ("x", "y", "c"))

    small = _unpack_small(_all_reduce_small(ops.order.tie(_pack_small(gs)), name="ar_small"))
    conv_w_shards = _ffn_vec_shards(small["conv_w"])
    small_g = {
        "g_attn": small["g_attn"], "g_q_lora": small["g_q_lora"], "g_kv_lora": small["g_kv_lora"],
        "g_ffn": small["g_ffn"], "g_ple": small["g_ple"], "g_final": small["g_final"],
        "conv_b": _ffn_vec_shards(small["conv_b"]).reshape(1, D_FF),
        "conv_w": lax.dynamic_index_in_dim(conv_w_shards, me, axis=1, keepdims=False),
    }
    results = dict(ops.results)
    for n, g in small_g.items():
        shape = wts[n].shape
        outs = ops.order.run(_adam, wts[n].reshape(g.shape), g, mom[n].reshape(g.shape), var[n].reshape(g.shape),
                             name="adam_" + n)
        results[n] = tuple(a.reshape(shape) for a in outs)
    ops.update("w_in")
    results["w_in"] = ops.results["w_in"]

    return (loss, grad_x[None], *[results[n][0] for n in _WEIGHTS], *[results[n][1] for n in _WEIGHTS],
            *[results[n][2] for n in _WEIGHTS], *[results[n][3] for n in _WEIGHTS])


class _ShardedWeights:
    def __init__(self, wts, mom, var):
        self.wts, self.mom, self.var = wts, mom, var
        self.order = _Order()
        self.full, self.stage, self.results = {}, {}, {}

    def start_gather(self, name, after=None, halves=False):
        payload = _shard_payload(name, self.wts[name][0])
        if after is not None:
            payload = lax.optimization_barrier((payload, after))[0]
        if not halves:
            self.full[name] = _full_from_gathered(name, _all_gather_seq(payload, name="ag_" + name))
            return
        half = payload.shape[1] // 2
        for part in range(2):
            piece = payload[:, part * half:(part + 1) * half]
            self.full[f"{name}/{part}"] = _full_from_gathered(name, _all_gather_seq(piece, name=f"ag_{name}_{part}"))

    def weight(self, name):
        return self.full[name]

    def grad(self, name, gfull):
        chunks = _grad_chunks(name, gfull)
        self.stage[name] = (chunks, _exchange_sibling(chunks, name="rs_sib_" + name))

    def reduce_add(self, name):
        chunks, from_sibling = self.stage[name]
        per_chip = self.order.run(_add_sibling, chunks, from_sibling, name="rs_add_" + name)
        self.stage[name] = _exchange_chips(per_chip, name="rs_chip_" + name)

    def update(self, name):
        parts = self.stage[name]
        if name == "w_uq":
            parts = _unpad_rows(name, parts)
        rows = lambda a: _rows_view(name, a[0])
        outs = self.order.run(_adam, rows(self.wts[name]), parts, rows(self.mom[name]), rows(self.var[name]),
                              name="adam_" + name, parts=4)
        self.results[name] = tuple(_rows_view(name, a)[None] for a in outs)
```

```python
import functools

import numpy as np

import jax
import jax.numpy as jnp
from jax import lax
from jax.experimental import pallas as pl
from jax.experimental.pallas import tpu as pltpu
from jax.experimental.pallas import tpu_sc as plsc

D_MODEL = 4096
CHUNK = 64
PLE_DIM = 256
RET_HEADS = 8
RET_HEAD_DIM = 256
RET_WIDTH = 2048
MLA_HEADS = 16
MLA_NOPE = 128
MLA_ROPE = 64
MLA_V = 128
Q_LORA = 1024
KV_LORA = 512
D_FF = 11008
CONV_WIDTH = 3
ROPE_BASE = 10000.0
EPS = 1e-6
IN_WIDTH = 9792
ADAM_LR, ADAM_B1, ADAM_B2, ADAM_EPS, ADAM_WD, ADAM_STEP = 0.001, 0.9, 0.999, 1e-08, 0.01, 10

IN_WIDTH_PAD = 10240
D_FF_PAD = 11264
MLA_QK_PAD = 256
Q_WIDTH_PAD = MLA_HEADS * MLA_QK_PAD

N_DEV = 8
MXU_DTYPE = jnp.bfloat16
ATTN_BLOCK = 512
HEADS_PER_STEP = 4
FWD_HEADS_PER_STEP = 8
VMEM_LIMIT = 56 * 1024 * 1024

F32 = jnp.float32


def _tile(n, want, align=128):
    if n <= want:
        return n
    t = (want // align) * align
    while t >= align:
        if n % t == 0:
            return t
        t -= align
    return n


def _params(sem):
    return pltpu.CompilerParams(dimension_semantics=sem, vmem_limit_bytes=VMEM_LIMIT)


def _sigmoid(x):
    return 1.0 / (1.0 + jnp.exp(-x))


def _matmul(a, b, *, name, ta=False, tb=False, out_dtype=F32, add=None, tm=1024, tn=1024, tk=4096):
    m, k = (a.shape[1], a.shape[0]) if ta else a.shape
    k2, n = (b.shape[1], b.shape[0]) if tb else b.shape
    assert k == k2, (a.shape, b.shape, ta, tb)
    tm, tn, tk = _tile(m, tm), _tile(n, tn), _tile(k, tk)
    nk = k // tk
    dims = (((0 if ta else 1,), (1 if tb else 0,)), ((), ()))

    def body(*refs):
        a_ref, b_ref, o_ref = refs[0], refs[1], refs[3 if add is not None else 2]
        c_ref = refs[2] if add is not None else None
        part = lax.dot_general(a_ref[...].astype(MXU_DTYPE), b_ref[...].astype(MXU_DTYPE), dims,
                               preferred_element_type=F32)

        def finish(r):
            if c_ref is not None:
                r = r + c_ref[...].astype(F32)
            o_ref[...] = r.astype(out_dtype)

        if nk == 1:
            finish(part)
            return
        acc_ref = refs[-1]
        kk = pl.program_id(2)

        @pl.when(kk == 0)
        def _():
            acc_ref[...] = part

        @pl.when((kk > 0) & (kk < nk - 1))
        def _():
            acc_ref[...] += part

        @pl.when(kk == nk - 1)
        def _():
            finish(acc_ref[...] + part)

    a_spec = pl.BlockSpec((tk, tm), lambda i, j, kk: (kk, i)) if ta else pl.BlockSpec((tm, tk), lambda i, j, kk: (i, kk))
    b_spec = pl.BlockSpec((tn, tk), lambda i, j, kk: (j, kk)) if tb else pl.BlockSpec((tk, tn), lambda i, j, kk: (kk, j))
    in_specs = [a_spec, b_spec]
    args = [a, b]
    if add is not None:
        in_specs.append(pl.BlockSpec((tm, tn), lambda i, j, kk: (i, j)))
        args.append(add)
    return pl.pallas_call(
        body, name=name, grid=(m // tm, n // tn, nk),
        in_specs=in_specs, out_specs=pl.BlockSpec((tm, tn), lambda i, j, kk: (i, j)),
        out_shape=jax.ShapeDtypeStruct((m, n), out_dtype),
        scratch_shapes=[] if nk == 1 else [pltpu.VMEM((tm, tn), F32)],
        compiler_params=_params(("parallel", "parallel", "arbitrary")),
    )(*args)


def _rms_fwd(x, g, *, name):
    t, d = x.shape
    tr = _tile(t, 256, 8)

    def body(x_ref, g_ref, o_ref):
        xf = x_ref[...]
        r = lax.rsqrt(jnp.mean(xf * xf, axis=-1, keepdims=True) + EPS)
        o_ref[...] = (xf * r * g_ref[...]).astype(o_ref.dtype)

    return pl.pallas_call(
        body, name=name, grid=(t // tr,),
        in_specs=[pl.BlockSpec((tr, d), lambda i: (i, 0)), pl.BlockSpec((1, d), lambda i: (0, 0))],
        out_specs=pl.BlockSpec((tr, d), lambda i: (i, 0)),
        out_shape=jax.ShapeDtypeStruct((t, d), MXU_DTYPE),
        compiler_params=_params(("parallel",)),
    )(x, g)


def _rms_bwd(x, dhn, g, res, *, name, low_copy=False):
    t, d = x.shape
    tr = _tile(t, 256, 8)

    def body(*refs):
        if res is None:
            x_ref, dh_ref, g_ref = refs[:3]
            outs = refs[3:]
            res_ref = None
        else:
            x_ref, dh_ref, g_ref, res_ref = refs[:4]
            outs = refs[4:]
        dx_ref, dg_ref = outs[0], outs[-1]
        xf = x_ref[...]
        dh = dh_ref[...].astype(F32)
        r = lax.rsqrt(jnp.mean(xf * xf, axis=-1, keepdims=True) + EPS)
        dyg = dh * g_ref[...]
        dx = r * dyg - xf * (r * r * r) * jnp.mean(dyg * xf, axis=-1, keepdims=True)
        if res_ref is not None:
            dx = dx + res_ref[...]
        dx_ref[...] = dx
        if low_copy:
            outs[1][...] = dx.astype(outs[1].dtype)
        part = jnp.sum(dh * xf * r, axis=0, keepdims=True)

        @pl.when(pl.program_id(0) == 0)
        def _():
            dg_ref[...] = part

        @pl.when(pl.program_id(0) > 0)
        def _():
            dg_ref[...] += part

    row = pl.BlockSpec((tr, d), lambda i: (i, 0))
    vec = pl.BlockSpec((1, d), lambda i: (0, 0))
    in_specs = [row, row, vec] + ([] if res is None else [row])
    args = [x, dhn, g] + ([] if res is None else [res])
    out_specs = [row] + ([row] if low_copy else []) + [vec]
    out_shape = [jax.ShapeDtypeStruct((t, d), F32)] + ([jax.ShapeDtypeStruct((t, d), MXU_DTYPE)] if low_copy else []) \
        + [jax.ShapeDtypeStruct((1, d), F32)]
    return pl.pallas_call(
        body, name=name, grid=(t // tr,), in_specs=in_specs, out_specs=out_specs, out_shape=out_shape,
        compiler_params=_params(("arbitrary",)),
    )(*args)


def _rope_tables(t):
    pos = jnp.arange(t, dtype=F32)[:, None]
    inv_r = 1.0 / (ROPE_BASE ** (jnp.arange(0, RET_HEAD_DIM, 2, dtype=F32) / RET_HEAD_DIM))
    ang_r = pos * inv_r[None, :]
    inv_m = 1.0 / (ROPE_BASE ** (jnp.arange(0, MLA_ROPE, 2, dtype=F32) / MLA_ROPE))
    ang_m = pos * inv_m[None, :]
    cm, sm = jnp.cos(ang_m), jnp.sin(ang_m)
    z = jnp.zeros_like(cm)
    cos_m = jnp.concatenate([cm, cm, z, z], axis=1)
    sin_m = jnp.concatenate([-sm, sm, z, z], axis=1)
    return jnp.cos(ang_r), jnp.sin(ang_r), cos_m, sin_m


def _rope256(x, c, s, inverse=False):
    x1, x2 = x[:, :128], x[:, 128:]
    if inverse:
        s = -s
    return jnp.concatenate([x1 * c - x2 * s, x2 * c + x1 * s], axis=1)


def _rope64(x, cos_m, sin_m, inverse=False):
    lane = lax.broadcasted_iota(jnp.int32, x.shape, 1)
    partner = jnp.where(lane < 32, pltpu.roll(x, 96, 1), pltpu.roll(x, 32, 1))
    s = -sin_m if inverse else sin_m
    return x * cos_m + partner * s


def _ret_prep(proj, cos_r, sin_r):
    t = proj.shape[0]
    tr = _tile(t, 256, 8)

    def body(q_ref, k_ref, v_ref, c_ref, s_ref, qo_ref, ko_ref, vo_ref):
        c, s = c_ref[...], s_ref[...]
        for h in range(RET_HEADS):
            cols = slice(h * RET_HEAD_DIM, (h + 1) * RET_HEAD_DIM)
            qo_ref[:, cols] = _rope256(q_ref[:, cols], c, s).astype(qo_ref.dtype)
            ko_ref[:, cols] = (_rope256(k_ref[:, cols], c, s) * (RET_HEAD_DIM ** -0.5)).astype(ko_ref.dtype)
        vo_ref[...] = v_ref[...].astype(vo_ref.dtype)

    group = lambda off: pl.BlockSpec((tr, RET_WIDTH), lambda i: (i, off))
    tab = pl.BlockSpec((tr, 128), lambda i: (i, 0))
    out = jax.ShapeDtypeStruct((t, RET_WIDTH), MXU_DTYPE)
    return pl.pallas_call(
        body, name="ret_prep", grid=(t // tr,),
        in_specs=[group(0), group(1), group(2), tab, tab],
        out_specs=[group(0), group(0), group(0)], out_shape=[out, out, out],
        compiler_params=_params(("parallel",)),
    )(proj, proj, proj, cos_r, sin_r)


def _ret_log_gamma():
    return jnp.asarray(np.log1p(-np.exp2(-5.0 - np.arange(RET_HEADS, dtype=np.float64))), dtype=F32)


def _decay_full(lg, i, j, blk):
    r = lax.broadcasted_iota(jnp.int32, (blk, 1), 0).astype(F32)
    c = lax.broadcasted_iota(jnp.int32, (1, blk), 1).astype(F32)
    off = ((i - j) * blk).astype(F32)
    return jnp.exp(lg * r), jnp.exp(lg * (off - c))


def _decay_diag(lg, blk):
    r = lax.broadcasted_iota(jnp.int32, (blk, blk), 0)
    c = lax.broadcasted_iota(jnp.int32, (blk, blk), 1)
    ok = (c // CHUNK) <= (r // CHUNK)
    return jnp.where(ok, jnp.exp(lg * jnp.abs(r - c).astype(F32)), 0.0)


_NT = (((1,), (1,)), ((), ()))
_TN = (((0,), (0,)), ((), ()))
_NN = (((1,), (0,)), ((), ()))


def _causal_pairs(nb, query_major):
    if query_major:
        pairs = [(i, j) for i in range(nb) for j in range(i + 1)]
    else:
        pairs = [(i, j) for j in range(nb) for i in range(j, nb)]
    arr = np.asarray(pairs, dtype=np.int32)
    return jnp.asarray(arr[:, 0]), jnp.asarray(arr[:, 1])


def _ret_fwd(q, k, v, proj, lg):
    t = q.shape[0]
    blk = _tile(t, ATTN_BLOCK)
    nb = t // blk
    hps, d = FWD_HEADS_PER_STEP, RET_HEAD_DIM
    gate_off = 3 * RET_WIDTH // (hps * d)

    def body(ii_ref, jj_ref, lg_ref, q_ref, k_ref, v_ref, g_ref, raw_ref, ro_ref, acc_ref):
        hg, pair = pl.program_id(0), pl.program_id(1)
        i, j = ii_ref[pair], jj_ref[pair]

        @pl.when(j == 0)
        def _():
            acc_ref[...] = jnp.zeros_like(acc_ref)

        def step(diag):
            for h in range(hps):
                cols = slice(h * d, (h + 1) * d)
                lgh = lg_ref[hg * hps + h]
                s = lax.dot_general(q_ref[:, cols], k_ref[:, cols], _NT, preferred_element_type=F32)
                if diag:
                    w = s * _decay_diag(lgh, blk)
                else:
                    a, b = _decay_full(lgh, i, j, blk)
                    w = s * a * b
                acc_ref[h] += lax.dot_general(w.astype(MXU_DTYPE), v_ref[:, cols], _NN, preferred_element_type=F32)

        @pl.when(j < i)
        def _():
            step(False)

        @pl.when(j == i)
        def _():
            step(True)
            for h in range(hps):
                cols = slice(h * d, (h + 1) * d)
                o = acc_ref[h]
                raw_ref[:, cols] = o
                mu = jnp.mean(o, axis=-1, keepdims=True)
                var = jnp.mean(jnp.square(o - mu), axis=-1, keepdims=True)
                hn = (o - mu) * lax.rsqrt(var + EPS)
                g = g_ref[:, cols]
                ro_ref[:, cols] = (g * _sigmoid(g) * hn).astype(ro_ref.dtype)

    qs = pl.BlockSpec((blk, hps * d), lambda h, p, ii, jj: (ii[p], h))
    ks = pl.BlockSpec((blk, hps * d), lambda h, p, ii, jj: (jj[p], h))
    gs = pl.BlockSpec((blk, hps * d), lambda h, p, ii, jj: (ii[p], h + gate_off))
    ii, jj = _causal_pairs(nb, query_major=True)
    return pl.pallas_call(
        body, name="ret_fwd",
        grid_spec=pltpu.PrefetchScalarGridSpec(
            num_scalar_prefetch=2, grid=(RET_HEADS // hps, ii.shape[0]),
            in_specs=[pl.BlockSpec(memory_space=pltpu.SMEM), qs, ks, ks, gs], out_specs=[qs, qs],
            scratch_shapes=[pltpu.VMEM((hps, blk, d), F32)]),
        out_shape=[jax.ShapeDtypeStruct((t, RET_WIDTH), F32), jax.ShapeDtypeStruct((t, RET_WIDTH), MXU_DTYPE)],
        compiler_params=_params(("parallel", "arbitrary")),
    )(ii, jj, lg, q, k, v, proj)


def _ret_gate_bwd(raw, proj, dattn):
    t = raw.shape[0]
    tr = _tile(t, 256, 8)

    def body(o_ref, g_ref, d_ref, do_ref, dg_ref):
        for h in range(RET_HEADS):
            cols = slice(h * RET_HEAD_DIM, (h + 1) * RET_HEAD_DIM)
            o, g, d = o_ref[:, cols], g_ref[:, cols], d_ref[:, cols]
            mu = jnp.mean(o, axis=-1, keepdims=True)
            rstd = lax.rsqrt(jnp.mean(jnp.square(o - mu), axis=-1, keepdims=True) + EPS)
            hn = (o - mu) * rstd
            sg = _sigmoid(g)
            dg_ref[:, cols] = (d * hn * (sg * (1.0 + g * (1.0 - sg)))).astype(dg_ref.dtype)
            dhn = d * (g * sg)
            do = rstd * (dhn - jnp.mean(dhn, axis=-1, keepdims=True)
                         - hn * jnp.mean(dhn * hn, axis=-1, keepdims=True))
            do_ref[:, cols] = do.astype(do_ref.dtype)

    group = lambda off: pl.BlockSpec((tr, RET_WIDTH), lambda i: (i, off))
    out = jax.ShapeDtypeStruct((t, RET_WIDTH), MXU_DTYPE)
    return pl.pallas_call(
        body, name="ret_gate_bwd", grid=(t // tr,),
        in_specs=[group(0), group(3), group(0)], out_specs=[group(0), group(0)], out_shape=[out, out],
        compiler_params=_params(("parallel",)),
    )(raw, proj, dattn)


def _ret_bwd(q, k, v, do, lg):
    t = q.shape[0]
    blk = _tile(t, ATTN_BLOCK)
    nb = t // blk

    hps, d = HEADS_PER_STEP, RET_HEAD_DIM

    def body(ii_ref, jj_ref, lg_ref, q_ref, k_ref, v_ref, do_ref, dq_ref, dk_ref, dv_ref, dk_acc, dv_acc):
        hg, pair = pl.program_id(0), pl.program_id(1)
        i, j = ii_ref[pair], jj_ref[pair]

        @pl.when(pair == 0)
        def _():
            dq_ref[...] = jnp.zeros_like(dq_ref)

        @pl.when(i == j)
        def _():
            dk_acc[...] = jnp.zeros_like(dk_acc)
            dv_acc[...] = jnp.zeros_like(dv_acc)

        def step(diag):
            rows = pl.ds(pl.multiple_of(i * blk, blk), blk)
            for h in range(hps):
                cols = slice(h * d, (h + 1) * d)
                lgh = lg_ref[hg * hps + h]
                if diag:
                    decay = _decay_diag(lgh, blk)
                else:
                    a, b = _decay_full(lgh, i, j, blk)
                    decay = a * b
                qb, kb, vb, dob = q_ref[:, cols], k_ref[:, cols], v_ref[:, cols], do_ref[:, cols]
                s = lax.dot_general(qb, kb, _NT, preferred_element_type=F32)
                w = (s * decay).astype(MXU_DTYPE)
                dv_acc[h] += lax.dot_general(w, dob, _TN, preferred_element_type=F32)
                dw = lax.dot_general(dob, vb, _NT, preferred_element_type=F32)
                ds = (dw * decay).astype(MXU_DTYPE)
                dq_ref[rows, cols] += lax.dot_general(ds, kb, _NN, preferred_element_type=F32)
                dk_acc[h] += lax.dot_general(ds, qb, _TN, preferred_element_type=F32)

        @pl.when(i > j)
        def _():
            step(False)

        @pl.when(i == j)
        def _():
            step(True)

        @pl.when(i == nb - 1)
        def _():
            for h in range(hps):
                cols = slice(h * d, (h + 1) * d)
                dk_ref[:, cols] = dk_acc[h]
                dv_ref[:, cols] = dv_acc[h].astype(dv_ref.dtype)

    qs = pl.BlockSpec((blk, hps * d), lambda h, p, ii, jj: (ii[p], h))
    ks = pl.BlockSpec((blk, hps * d), lambda h, p, ii, jj: (jj[p], h))
    ii, jj = _causal_pairs(nb, query_major=False)
    return pl.pallas_call(
        body, name="ret_bwd",
        grid_spec=pltpu.PrefetchScalarGridSpec(
            num_scalar_prefetch=2, grid=(RET_HEADS // hps, ii.shape[0]),
            in_specs=[pl.BlockSpec(memory_space=pltpu.SMEM), qs, ks, ks, qs],
            out_specs=[pl.BlockSpec((t, hps * d), lambda h, p, ii, jj: (0, h), pipeline_mode=pl.Buffered(1)), ks, ks],
            scratch_shapes=[pltpu.VMEM((hps, blk, d), F32), pltpu.VMEM((hps, blk, d), F32)]),
        out_shape=[jax.ShapeDtypeStruct((t, RET_WIDTH), F32), jax.ShapeDtypeStruct((t, RET_WIDTH), F32),
                   jax.ShapeDtypeStruct((t, RET_WIDTH), MXU_DTYPE)],
        compiler_params=_params(("parallel", "arbitrary")),
    )(ii, jj, lg, q, k, v, do)


def _ret_unrope(dq, dk, cos_r, sin_r):
    t = dq.shape[0]
    tr = _tile(t, 256, 8)

    def body(dq_ref, dk_ref, c_ref, s_ref, oq_ref, ok_ref):
        c, s = c_ref[...], s_ref[...]
        for h in range(RET_HEADS):
            cols = slice(h * RET_HEAD_DIM, (h + 1) * RET_HEAD_DIM)
            oq_ref[:, cols] = _rope256(dq_ref[:, cols], c, s, inverse=True).astype(oq_ref.dtype)
            ok_ref[:, cols] = (_rope256(dk_ref[:, cols], c, s, inverse=True)
                               * (RET_HEAD_DIM ** -0.5)).astype(ok_ref.dtype)

    rows = pl.BlockSpec((tr, RET_WIDTH), lambda i: (i, 0))
    tab = pl.BlockSpec((tr, 128), lambda i: (i, 0))
    out = jax.ShapeDtypeStruct((t, RET_WIDTH), MXU_DTYPE)
    return pl.pallas_call(
        body, name="ret_unrope", grid=(t // tr,),
        in_specs=[rows, rows, tab, tab], out_specs=[rows, rows], out_shape=[out, out],
        compiler_params=_params(("parallel",)),
    )(dq, dk, cos_r, sin_r)


def _mla_prep(cq, ckv, kr, g_q, g_kv, cos_m, sin_m):
    t = cq.shape[0]
    tr = _tile(t, 512, 8)

    def body(cq_ref, ckv_ref, kr_ref, gq_ref, gkv_ref, c_ref, s_ref, cqn_ref, kvn_ref, kro_ref):
        for x_ref, g_ref, o_ref in ((cq_ref, gq_ref, cqn_ref), (ckv_ref, gkv_ref, kvn_ref)):
            xf = x_ref[...]
            r = lax.rsqrt(jnp.mean(xf * xf, axis=-1, keepdims=True) + EPS)
            o_ref[...] = (xf * r * g_ref[...]).astype(o_ref.dtype)
        kro_ref[...] = _rope64(kr_ref[...], c_ref[...], s_ref[...]).astype(kro_ref.dtype)

    row = lambda w: pl.BlockSpec((tr, w), lambda i: (i, 0))
    vec = lambda w: pl.BlockSpec((1, w), lambda i: (0, 0))
    return pl.pallas_call(
        body, name="mla_prep", grid=(t // tr,),
        in_specs=[row(Q_LORA), row(KV_LORA), row(128), vec(Q_LORA), vec(KV_LORA), row(128), row(128)],
        out_specs=[row(Q_LORA), row(KV_LORA), row(128)],
        out_shape=[jax.ShapeDtypeStruct((t, Q_LORA), MXU_DTYPE), jax.ShapeDtypeStruct((t, KV_LORA), MXU_DTYPE),
                   jax.ShapeDtypeStruct((t, 128), MXU_DTYPE)],
        compiler_params=_params(("parallel",)),
    )(cq, ckv, kr, g_q, g_kv, cos_m, sin_m)


def _mla_q_rope(q_lin, cos_m, sin_m, *, inverse, name):
    t = q_lin.shape[0]
    tr = _tile(t, 256, 8)

    def body(q_ref, c_ref, s_ref, o_ref):
        c, s = c_ref[...], s_ref[...]
        for h in range(MLA_HEADS):
            lo = h * MLA_QK_PAD
            o_ref[:, lo:lo + MLA_NOPE] = q_ref[:, lo:lo + MLA_NOPE].astype(o_ref.dtype)
            roped = _rope64(q_ref[:, lo + MLA_NOPE:lo + MLA_QK_PAD].astype(F32), c, s, inverse=inverse)
            o_ref[:, lo + MLA_NOPE:lo + MLA_QK_PAD] = roped.astype(o_ref.dtype)

    rows = pl.BlockSpec((tr, Q_WIDTH_PAD), lambda i: (i, 0))
    tab = pl.BlockSpec((tr, 128), lambda i: (i, 0))
    return pl.pallas_call(
        body, name=name, grid=(t // tr,),
        in_specs=[rows, tab, tab], out_specs=rows, out_shape=jax.ShapeDtypeStruct((t, Q_WIDTH_PAD), MXU_DTYPE),
        compiler_params=_params(("parallel",)),
    )(q_lin, cos_m, sin_m)


_MLA_SCALE = (MLA_NOPE + MLA_ROPE) ** -0.5
_LOG2_E = 1.4426950408889634
_MLA_SCALE_LOG2 = _MLA_SCALE * _LOG2_E
_NEG = -1e30


def _mla_mask(blk):
    r = lax.broadcasted_iota(jnp.int32, (blk, blk), 0)
    c = lax.broadcasted_iota(jnp.int32, (blk, blk), 1)
    return (c // CHUNK) <= (r // CHUNK)


def _mla_fwd(q, kv, kr):
    t = q.shape[0]
    blk = _tile(t, ATTN_BLOCK)
    nb = t // blk

    hps, dq_, dv_ = MLA_HEADS, MLA_QK_PAD, MLA_V

    def body(ii_ref, jj_ref, q_ref, kv_ref, kr_ref, o_ref, lse_ref, m_ref, acc_ref):
        pair = pl.program_id(1)
        i, j = ii_ref[pair], jj_ref[pair]

        @pl.when(j == 0)
        def _():
            m_ref[...] = jnp.full_like(m_ref, _NEG)
            acc_ref[...] = jnp.zeros_like(acc_ref)

        def step(masked):
            krb = kr_ref[...]
            ones = jnp.ones((blk, dv_), MXU_DTYPE)
            for h in range(hps):
                kb = jnp.concatenate([kv_ref[:, h * dq_:h * dq_ + MLA_NOPE], krb], axis=1)
                vb = jnp.concatenate([kv_ref[:, h * dq_ + MLA_NOPE:(h + 1) * dq_], ones], axis=1)
                s = lax.dot_general(q_ref[:, h * dq_:(h + 1) * dq_], kb, _NT, preferred_element_type=F32)
                if masked:
                    s = jnp.where(_mla_mask(blk), s, _NEG)
                m_prev = m_ref[h]
                m_new = jnp.maximum(m_prev, jnp.max(s, axis=-1, keepdims=True))
                alpha = jnp.exp2((m_prev - m_new) * _MLA_SCALE_LOG2)
                p = jnp.exp2((s - jnp.tile(m_new, (1, blk // 128))) * _MLA_SCALE_LOG2)
                acc_ref[h] = jnp.tile(alpha, (1, 2)) * acc_ref[h] + lax.dot_general(
                    p.astype(MXU_DTYPE), vb, _NN, preferred_element_type=F32)
                m_ref[h] = m_new

        @pl.when(j < i)
        def _():
            step(False)

        @pl.when(j == i)
        def _():
            step(True)
            for h in range(hps):
                cols = slice(h * dv_, (h + 1) * dv_)
                acc = acc_ref[h]
                row_sum = acc[:, dv_:]
                o_ref[:, cols] = (acc[:, :dv_] / row_sum).astype(o_ref.dtype)
                lse_ref[:, cols] = m_ref[h] * _MLA_SCALE + jnp.log(row_sum)

    os_ = pl.BlockSpec((blk, hps * dv_), lambda h, p, ii, jj: (ii[p], h))
    ii, jj = _causal_pairs(nb, query_major=True)
    return pl.pallas_call(
        body, name="mla_fwd",
        grid_spec=pltpu.PrefetchScalarGridSpec(
            num_scalar_prefetch=2, grid=(MLA_HEADS // hps, ii.shape[0]),
            in_specs=[pl.BlockSpec((blk, hps * dq_), lambda h, p, ii, jj: (ii[p], h)),
                      pl.BlockSpec((blk, hps * dq_), lambda h, p, ii, jj: (jj[p], h)),
                      pl.BlockSpec((blk, 128), lambda h, p, ii, jj: (jj[p], 0))],
            out_specs=[os_, os_],
            scratch_shapes=[pltpu.VMEM((hps, blk, 128), F32), pltpu.VMEM((hps, blk, 2 * dv_), F32)]),
        out_shape=[jax.ShapeDtypeStruct((t, MLA_HEADS * MLA_V), MXU_DTYPE),
                   jax.ShapeDtypeStruct((t, MLA_HEADS * 128), F32)],
        compiler_params=_params(("parallel", "arbitrary")),
    )(ii, jj, q, kv, kr)


def _mla_bwd(q, kv, kr, o, lse, dattn):
    t = q.shape[0]
    blk = _tile(t, ATTN_BLOCK)
    nb = t // blk
    hps, dq_, dv_ = HEADS_PER_STEP, MLA_QK_PAD, MLA_V
    do_off = RET_WIDTH // (hps * dv_)

    def body(ii_ref, jj_ref, q_ref, kv_ref, kr_ref, o_ref, lse_ref, do_ref, dq_ref, dk_ref, dv_ref, dk_acc, dv_acc):
        pair = pl.program_id(1)
        i, j = ii_ref[pair], jj_ref[pair]

        @pl.when(pair == 0)
        def _():
            dq_ref[...] = jnp.zeros_like(dq_ref)

        @pl.when(i == j)
        def _():
            dk_acc[...] = jnp.zeros_like(dk_acc)
            dv_acc[...] = jnp.zeros_like(dv_acc)

        def step(masked):
            krb = kr_ref[...]
            rows = pl.ds(pl.multiple_of(i * blk, blk), blk)
            for h in range(hps):
                qcols, vcols = slice(h * dq_, (h + 1) * dq_), slice(h * dv_, (h + 1) * dv_)
                qb = q_ref[:, qcols]
                kb = jnp.concatenate([kv_ref[:, h * dq_:h * dq_ + MLA_NOPE], krb], axis=1)
                vb = kv_ref[:, h * dq_ + MLA_NOPE:(h + 1) * dq_]
                dof = do_ref[:, vcols]
                dob = dof.astype(MXU_DTYPE)
                s = lax.dot_general(qb, kb, _NT, preferred_element_type=F32)
                if masked:
                    s = jnp.where(_mla_mask(blk), s, _NEG)
                lse2 = lse_ref[:, vcols] * _LOG2_E
                p = jnp.exp2(s * _MLA_SCALE_LOG2 - jnp.tile(lse2, (1, blk // 128)))
                delta = jnp.sum(dof * o_ref[:, vcols].astype(F32), axis=-1, keepdims=True)
                dv_acc[h] += lax.dot_general(p.astype(MXU_DTYPE), dob, _TN, preferred_element_type=F32)
                dp = lax.dot_general(dob, vb, _NT, preferred_element_type=F32)
                ds = (p * (dp - delta) * _MLA_SCALE).astype(MXU_DTYPE)
                dq_ref[rows, qcols] += lax.dot_general(ds, kb, _NN, preferred_element_type=F32)
                dk_acc[h] += lax.dot_general(ds, qb, _TN, preferred_element_type=F32)

        @pl.when(i > j)
        def _():
            step(False)

        @pl.when(i == j)
        def _():
            step(True)

        @pl.when(i == nb - 1)
        def _():
            for h in range(hps):
                dk_ref[:, h * dq_:(h + 1) * dq_] = dk_acc[h]
                dv_ref[:, h * dv_:(h + 1) * dv_] = dv_acc[h].astype(dv_ref.dtype)

    qmap = lambda off: (lambda h, p, ii, jj: (ii[p], h + off))
    kmap = lambda h, p, ii, jj: (jj[p], h)
    ii, jj = _causal_pairs(nb, query_major=False)
    return pl.pallas_call(
        body, name="mla_bwd",
        grid_spec=pltpu.PrefetchScalarGridSpec(
            num_scalar_prefetch=2, grid=(MLA_HEADS // hps, ii.shape[0]),
            in_specs=[pl.BlockSpec((blk, hps * dq_), qmap(0)), pl.BlockSpec((blk, hps * dq_), kmap),
                      pl.BlockSpec((blk, 128), lambda h, p, ii, jj: (jj[p], 0)),
                      pl.BlockSpec((blk, hps * dv_), qmap(0)), pl.BlockSpec((blk, hps * dv_), qmap(0)),
                      pl.BlockSpec((blk, hps * dv_), qmap(do_off))],
            out_specs=[pl.BlockSpec((t, hps * dq_), lambda h, p, ii, jj: (0, h), pipeline_mode=pl.Buffered(1)),
                       pl.BlockSpec((blk, hps * dq_), kmap), pl.BlockSpec((blk, hps * dv_), kmap)],
            scratch_shapes=[pltpu.VMEM((hps, blk, dq_), F32), pltpu.VMEM((hps, blk, dv_), F32)]),
        out_shape=[jax.ShapeDtypeStruct((t, Q_WIDTH_PAD), F32), jax.ShapeDtypeStruct((t, Q_WIDTH_PAD), F32),
                   jax.ShapeDtypeStruct((t, MLA_HEADS * MLA_V), MXU_DTYPE)],
        compiler_params=_params(("parallel", "arbitrary")),
    )(ii, jj, q, kv, kr, o, lse, dattn)


def _mla_kv_grad(dk, dv, cos_m, sin_m):
    t = dk.shape[0]
    tr = _tile(t, 256, 8)

    def body(dk_ref, dv_ref, c_ref, s_ref, dkv_ref, dkr_ref):
        acc = jnp.zeros((tr, 128), F32)
        for h in range(MLA_HEADS):
            dkv_ref[:, h * 256:h * 256 + 128] = dk_ref[:, h * 256:h * 256 + 128].astype(dkv_ref.dtype)
            dkv_ref[:, h * 256 + 128:h * 256 + 256] = dv_ref[:, h * 128:(h + 1) * 128].astype(dkv_ref.dtype)
            acc = acc + dk_ref[:, h * 256 + 128:h * 256 + 256]
        dkr_ref[...] = _rope64(acc, c_ref[...], s_ref[...], inverse=True).astype(dkr_ref.dtype)

    row = lambda w: pl.BlockSpec((tr, w), lambda i: (i, 0))
    return pl.pallas_call(
        body, name="mla_kv_grad", grid=(t // tr,),
        in_specs=[row(Q_WIDTH_PAD), row(MLA_HEADS * MLA_V), row(128), row(128)],
        out_specs=[row(Q_WIDTH_PAD), row(128)],
        out_shape=[jax.ShapeDtypeStruct((t, Q_WIDTH_PAD), MXU_DTYPE), jax.ShapeDtypeStruct((t, 128), MXU_DTYPE)],
        compiler_params=_params(("parallel",)),
    )(dk, dv, cos_m, sin_m)


_FFN_COLS = 256
_FFN_ROWS = 256


def _shift_down(cur, prev8, n):
    out = pltpu.roll(cur, n, 0)
    head = out[:8]
    row = lax.broadcasted_iota(jnp.int32, head.shape, 0)
    for r in range(n):
        head = jnp.where(row == r, prev8[8 - n + r:8 - n + r + 1, :], head)
    return jnp.concatenate([head, out[8:]], axis=0)


def _shift_up(cur, next8, n):
    rows = cur.shape[0]
    out = pltpu.roll(cur, rows - n, 0)
    tail = out[rows - 8:]
    row = lax.broadcasted_iota(jnp.int32, tail.shape, 0)
    for r in range(n):
        tail = jnp.where(row == 8 - n + r, next8[r:r + 1, :], tail)
    return jnp.concatenate([out[:rows - 8], tail], axis=0)


def _conv_pre(g_ref, cw_ref, cb_ref, c, rc):
    r0 = pl.multiple_of(c * rc, rc)
    cur = g_ref[pl.ds(r0, rc), :].astype(F32)
    prev16 = g_ref[pl.ds(pl.multiple_of(jnp.maximum(r0 - 16, 0), 16), 16), :].astype(F32)
    prev8 = jnp.where(c > 0, prev16[8:], 0.0)
    s1, s2 = _shift_down(cur, prev8, 1), _shift_down(cur, prev8, 2)
    a = cb_ref[...] + cw_ref[2:3, :] * cur + cw_ref[1:2, :] * s1 + cw_ref[0:1, :] * s2
    return r0, cur, s1, s2, a


def _ffn_act_fwd(gpre, u, cw, cb):
    t, f = gpre.shape
    tc = _tile(f, _FFN_COLS)
    rc = _tile(t, _FFN_ROWS, 8)

    def body(g_ref, u_ref, cw_ref, cb_ref, o_ref):
        def chunk(c, carry):
            r0, _, _, _, a = _conv_pre(g_ref, cw_ref, cb_ref, c, rc)
            o_ref[pl.ds(r0, rc), :] = (a * _sigmoid(a) * u_ref[pl.ds(r0, rc), :]).astype(o_ref.dtype)
            return carry
        lax.fori_loop(0, t // rc, chunk, 0)

    col = pl.BlockSpec((t, tc), lambda j: (0, j))
    return pl.pallas_call(
        body, name="ffn_act_fwd", grid=(f // tc,),
        in_specs=[col, col, pl.BlockSpec((CONV_WIDTH, tc), lambda j: (0, j)), pl.BlockSpec((1, tc), lambda j: (0, j))],
        out_specs=col, out_shape=jax.ShapeDtypeStruct((t, f), MXU_DTYPE),
        compiler_params=_params(("parallel",)),
    )(gpre, u, cw, cb)


def _ffn_act_bwd(gpre, u, dact, cw, cb):
    t, f = gpre.shape
    tc = _tile(f, _FFN_COLS)
    rc = _tile(t, _FFN_ROWS, 8)
    nc = t // rc

    def body(g_ref, u_ref, d_ref, cw_ref, cb_ref, dg_ref, du_ref, dcw_ref, dcb_ref, da_ref):
        def chunk(c, carry):
            w0, w1, w2, b = carry
            r0, cur, s1, s2, a = _conv_pre(g_ref, cw_ref, cb_ref, c, rc)
            sg = _sigmoid(a)
            d = d_ref[pl.ds(r0, rc), :].astype(F32)
            du_ref[pl.ds(r0, rc), :] = (d * (a * sg)).astype(du_ref.dtype)
            da = d * u_ref[pl.ds(r0, rc), :].astype(F32) * (sg * (1.0 + a * (1.0 - sg)))
            da_ref[pl.ds(r0, rc), :] = da
            return (w0 + jnp.sum(da * s2, axis=0, keepdims=True), w1 + jnp.sum(da * s1, axis=0, keepdims=True),
                    w2 + jnp.sum(da * cur, axis=0, keepdims=True), b + jnp.sum(da, axis=0, keepdims=True))
        z = jnp.zeros((1, tc), F32)
        w0, w1, w2, b = lax.fori_loop(0, nc, chunk, (z, z, z, z))
        dcw_ref[0:1, :] = w0
        dcw_ref[1:2, :] = w1
        dcw_ref[2:3, :] = w2
        dcb_ref[...] = b

        def chunk2(c, carry):
            r0 = pl.multiple_of(c * rc, rc)
            cur = da_ref[pl.ds(r0, rc), :]
            nxt = da_ref[pl.ds(pl.multiple_of(jnp.minimum(r0 + rc, t - 8), 8), 8), :]
            nxt = jnp.where(c < nc - 1, nxt, 0.0)
            dg = cw_ref[2:3, :] * cur + cw_ref[1:2, :] * _shift_up(cur, nxt, 1) + cw_ref[0:1, :] * _shift_up(cur, nxt, 2)
            dg_ref[pl.ds(r0, rc), :] = dg.astype(dg_ref.dtype)
            return carry
        lax.fori_loop(0, nc, chunk2, 0)

    col = pl.BlockSpec((t, tc), lambda j: (0, j))
    w3 = pl.BlockSpec((CONV_WIDTH, tc), lambda j: (0, j))
    w1 = pl.BlockSpec((1, tc), lambda j: (0, j))
    low = jax.ShapeDtypeStruct((t, f), MXU_DTYPE)
    return pl.pallas_call(
        body, name="ffn_act_bwd", grid=(f // tc,),
        in_specs=[col, col, col, w3, w1], out_specs=[col, col, w3, w1],
        out_shape=[low, low, jax.ShapeDtypeStruct((CONV_WIDTH, f), F32), jax.ShapeDtypeStruct((1, f), F32)],
        scratch_shapes=[pltpu.VMEM((t, tc), F32)],
        compiler_params=_params(("parallel",)),
    )(gpre, u, dact, cw, cb)


def _head_fwd_bwd(h2, glin, pp, target, g_final):
    t, d = h2.shape
    tr = _tile(t, 128, 8)

    def body(h_ref, gl_ref, pp_ref, t_ref, g_ref, loss_ref, dh_ref, dgl_ref, dpp_ref, dg_ref):
        gate = _sigmoid(gl_ref[...])
        ppv = pp_ref[...]
        h3 = h_ref[...] + gate * ppv
        r = lax.rsqrt(jnp.mean(h3 * h3, axis=-1, keepdims=True) + EPS)
        yh = h3 * r
        g = g_ref[...]
        diff = yh * g - t_ref[...]
        lpart = 0.5 * jnp.sum(jnp.mean(diff * diff, axis=-1, keepdims=True), axis=0, keepdims=True)
        dy = diff * (1.0 / d)
        dyg = dy * g
        dh3 = r * dyg - h3 * (r * r * r) * jnp.mean(dyg * h3, axis=-1, keepdims=True)
        dh_ref[...] = dh3
        dgl_ref[...] = (dh3 * ppv * gate * (1.0 - gate)).astype(dgl_ref.dtype)
        dpp_ref[...] = (dh3 * gate).astype(dpp_ref.dtype)
        dgp = jnp.sum(dy * yh, axis=0, keepdims=True)

        @pl.when(pl.program_id(0) == 0)
        def _():
            loss_ref[...] = jnp.broadcast_to(lpart, loss_ref.shape)
            dg_ref[...] = dgp

        @pl.when(pl.program_id(0) > 0)
        def _():
            loss_ref[...] += jnp.broadcast_to(lpart, loss_ref.shape)
            dg_ref[...] += dgp

    row = pl.BlockSpec((tr, d), lambda i: (i, 0))
    vec = pl.BlockSpec((1, d), lambda i: (0, 0))
    low = jax.ShapeDtypeStruct((t, d), MXU_DTYPE)
    return pl.pallas_call(
        body, name="head_fwd_bwd", grid=(t // tr,),
        in_specs=[row, row, row, row, vec],
        out_specs=[pl.BlockSpec((8, 128), lambda i: (0, 0)), row, row, row, vec],
        out_shape=[jax.ShapeDtypeStruct((8, 128), F32), jax.ShapeDtypeStruct((t, d), F32), low, low,
                   jax.ShapeDtypeStruct((1, d), F32)],
        compiler_params=_params(("arbitrary",)),
    )(h2, glin, pp, target, g_final)


class _Order:
    def __init__(self):
        self.last = None

    def tie(self, x):
        return x if self.last is None else lax.optimization_barrier((x, self.last))[0]

    def run(self, fn, first, *args, **kwargs):
        out = fn(self.tie(first), *args, **kwargs)
        self.last = out[0] if isinstance(out, (tuple, list)) else out
        return out


def _local_step(x, p, target, vec, ops):
    t = x.shape[0]
    cos_r, sin_r, cos_m, sin_m = _rope_tables(t)
    lg = _ret_log_gamma()
    low = MXU_DTYPE
    run = ops.order.run
    w = ops.weight

    ops.start_gather("w_in", halves=True)
    hn1 = run(_rms_fwd, x, vec["g_attn"], name="rms1_fwd")
    for n in ("w_uq", "w_ukv", "w_o"):
        ops.start_gather(n, after=hn1)
    half = x.shape[1] // 2
    proj = run(_matmul, hn1[:, :half], w("w_in/0"), tb=True, name="mm_proj_a")
    proj = run(_matmul, hn1[:, half:], w("w_in/1"), tb=True, name="mm_proj_b", add=proj)
    ops.start_gather("w_ffn_gate", after=proj)
    rq, rk, rv = run(_ret_prep, proj, cos_r, sin_r)
    ops.start_gather("w_ffn_up", after=rq)
    c0 = 4 * RET_WIDTH
    cq = proj[:, c0:c0 + Q_LORA]
    ckv = proj[:, c0 + Q_LORA:c0 + Q_LORA + KV_LORA]
    kr_in = proj[:, c0 + Q_LORA + KV_LORA:c0 + Q_LORA + KV_LORA + 128]
    cqn, kvn, kr = run(_mla_prep, cq, ckv, kr_in, vec["g_q_lora"], vec["g_kv_lora"], cos_m, sin_m)
    ops.start_gather("w_ffn_down", after=cqn)
    q_lin = run(_matmul, cqn, w("w_uq"), tb=True, name="mm_q")
    q = run(_mla_q_rope, q_lin, cos_m, sin_m, inverse=False, name="mla_q_rope")
    kv = run(_matmul, kvn, w("w_ukv"), tb=True, name="mm_kv", out_dtype=low)
    mo, lse = run(_mla_fwd, q, kv, kr)
    ops.start_gather("w_ple_gate", after=mo)
    ops.start_gather("w_ple_proj", after=mo)
    ret_raw, ro = run(_ret_fwd, rq, rk, rv, proj, lg)
    attn = jnp.concatenate([ro, mo], axis=1)
    h1 = run(_matmul, attn, w("w_o"), name="mm_o", add=x)
    hn2 = run(_rms_fwd, h1, vec["g_ffn"], name="rms2_fwd")
    gpre = run(_matmul, hn2, w("w_ffn_gate"), tb=True, name="mm_gate", out_dtype=low)
    u = run(_matmul, hn2, w("w_ffn_up"), tb=True, name="mm_up", out_dtype=low)
    act = run(_ffn_act_fwd, gpre, u, vec["conv_w"], vec["conv_b"])
    h2 = run(_matmul, act, w("w_ffn_down"), name="mm_down", add=h1)
    hn3 = run(_rms_fwd, h2, vec["g_ple"], name="rms3_fwd")
    glin = run(_matmul, hn3, w("w_ple_gate"), name="mm_ple_gate")
    p_low = p.astype(low)
    pp = run(_matmul, p_low, w("w_ple_proj"), tb=True, name="mm_ple_proj")
    loss_part, dh3, dglin, dpp, dg_final = run(_head_fwd_bwd, h2, glin, pp, target, vec["g_final"])

    ops.grad("w_ple_proj", run(_matmul, dpp, p_low, ta=True, name="mm_d_ple_proj", out_dtype=low))
    ops.grad("w_ple_gate", run(_matmul, hn3, dglin, ta=True, name="mm_d_ple_gate", out_dtype=low))
    dhn3 = run(_matmul, dglin, w("w_ple_gate"), tb=True, name="mm_dhn3", out_dtype=low)
    dh2, dh2_low, dg_ple = run(_rms_bwd, h2, dhn3, vec["g_ple"], dh3, name="rms3_bwd", low_copy=True)
    ops.reduce_add("w_ple_proj")
    ops.reduce_add("w_ple_gate")
    ops.grad("w_ffn_down", run(_matmul, act, dh2_low, ta=True, name="mm_d_down", out_dtype=low))
    dact = run(_matmul, dh2_low, w("w_ffn_down"), tb=True, name="mm_dact", out_dtype=low)
    ops.reduce_add("w_ffn_down")
    dgpre, du, dconv_w, dconv_b = run(_ffn_act_bwd, gpre, u, dact, vec["conv_w"], vec["conv_b"])
    ops.update("w_ple_proj")
    ops.update("w_ple_gate")
    ops.grad("w_ffn_gate", run(_matmul, dgpre, hn2, ta=True, name="mm_d_gate", out_dtype=low))
    ops.grad("w_ffn_up", run(_matmul, du, hn2, ta=True, name="mm_d_up", out_dtype=low))
    ops.reduce_add("w_ffn_gate")
    dhn2 = run(_matmul, dgpre, w("w_ffn_gate"), name="mm_dhn2_a")
    ops.reduce_add("w_ffn_up")
    dhn2 = run(_matmul, du, w("w_ffn_up"), name="mm_dhn2_b", add=dhn2, out_dtype=low)
    dh1, dh1_low, dg_ffn = run(_rms_bwd, h1, dhn2, vec["g_ffn"], dh2, name="rms2_bwd", low_copy=True)
    ops.update("w_ffn_down")
    ops.grad("w_o", run(_matmul, attn, dh1_low, ta=True, name="mm_d_o", out_dtype=low))
    dattn = run(_matmul, dh1_low, w("w_o"), tb=True, name="mm_dattn")
    ops.reduce_add("w_o")

    dq_r, dk_full, dv = run(_mla_bwd, q, kv, kr, mo, lse, dattn)
    ops.update("w_ffn_gate")
    dq_lin = run(_mla_q_rope, dq_r, cos_m, sin_m, inverse=True, name="mla_q_unrope")
    dkv, dkr = run(_mla_kv_grad, dk_full, dv, cos_m, sin_m)
    ops.grad("w_uq", run(_matmul, dq_lin, cqn, ta=True, name="mm_d_uq", out_dtype=low))
    dcqn = run(_matmul, dq_lin, w("w_uq"), name="mm_dcqn")
    ops.grad("w_ukv", run(_matmul, dkv, kvn, ta=True, name="mm_d_ukv", out_dtype=low))
    dkvn = run(_matmul, dkv, w("w_ukv"), name="mm_dkvn")
    dcq, dcq_low, dg_q = run(_rms_bwd, cq, dcqn, vec["g_q_lora"], None, name="rmsq_bwd", low_copy=True)
    dckv, dckv_low, dg_kv = run(_rms_bwd, ckv, dkvn, vec["g_kv_lora"], None, name="rmskv_bwd", low_copy=True)
    ops.reduce_add("w_uq")
    ops.reduce_add("w_ukv")

    do_ret, drg = run(_ret_gate_bwd, ret_raw, proj, dattn)
    dq_ret, dk_ret, drv = run(_ret_bwd, rq, rk, rv, do_ret, lg)
    drq, drk = run(_ret_unrope, dq_ret, dk_ret, cos_r, sin_r)

    pad = jnp.zeros((t, IN_WIDTH_PAD - IN_WIDTH - 64), low)
    dproj = jnp.concatenate([drq, drk, drv, drg, dcq_low, dckv_low, dkr, pad], axis=1)
    ops.grad("w_in", run(_matmul, dproj, hn1, ta=True, name="mm_d_in", out_dtype=low))
    for n in ("w_ffn_up", "w_o", "w_uq", "w_ukv"):
        ops.update(n)
    ops.reduce_add("w_in")
    dhn1 = jnp.concatenate([run(_matmul, dproj, w(f"w_in/{part}"), name=f"mm_dhn1_{part}", out_dtype=low)
                            for part in range(2)], axis=1)
    grad_x, dg_attn = run(_rms_bwd, x, dhn1, vec["g_attn"], dh1, name="rms1_bwd")

    gs = {"g_attn": dg_attn, "g_q_lora": dg_q, "g_kv_lora": dg_kv, "g_ffn": dg_ffn, "conv_w": dconv_w,
          "conv_b": dconv_b, "g_ple": dg_ple, "g_final": dg_final}
    return loss_part, grad_x, gs


_COL_SHARDED = ("w_in", "w_uq", "w_ukv", "w_ffn_gate", "w_ffn_up", "w_ple_proj")
_FFN_SHARD = D_FF // N_DEV
_FFN_SHARD_PAD = D_FF_PAD // N_DEV
_HEADS_PER_SHARD = MLA_HEADS // N_DEV
_QK = MLA_NOPE + MLA_ROPE


def _pad_rows(name, a):
    lead = a.shape[:-2]
    if name == "w_uq":
        a = a.reshape(lead + (_HEADS_PER_SHARD, _QK, a.shape[-1]))
        a = jnp.pad(a, [(0, 0)] * len(lead) + [(0, 0), (0, MLA_QK_PAD - _QK), (0, 0)])
        return a.reshape(lead + (_HEADS_PER_SHARD * MLA_QK_PAD, a.shape[-1]))
    if name in ("w_ffn_gate", "w_ffn_up", "w_ffn_down"):
        return jnp.pad(a, [(0, 0)] * len(lead) + [(0, _FFN_SHARD_PAD - _FFN_SHARD), (0, 0)])
    return a


def _unpad_rows(name, a):
    lead = a.shape[:-2]
    if name == "w_uq":
        a = a.reshape(lead + (_HEADS_PER_SHARD, MLA_QK_PAD, a.shape[-1]))[..., :_QK, :]
        return a.reshape(lead + (_HEADS_PER_SHARD * _QK, a.shape[-1]))
    if name in ("w_ffn_gate", "w_ffn_up", "w_ffn_down"):
        return a[..., :_FFN_SHARD, :]
    return a


def _rows_view(name, a):
    return jnp.swapaxes(a, 0, 1) if name in _COL_SHARDED else a


def _shard_payload(name, shard):
    return _pad_rows(name, _rows_view(name, shard).astype(MXU_DTYPE))


def _full_from_gathered(name, g):
    full = g.reshape(g.shape[0] * g.shape[1], g.shape[2])
    if name == "w_in":
        full = jnp.pad(full, ((0, IN_WIDTH_PAD - IN_WIDTH), (0, 0)))
    return full


def _grad_chunks(name, gfull):
    if name == "w_in":
        gfull = gfull[:IN_WIDTH]
    return gfull.reshape(N_DEV, gfull.shape[0] // N_DEV, gfull.shape[1])


def _ffn_vec_layout(a):
    a = a.reshape(a.shape[0], N_DEV, _FFN_SHARD)
    return jnp.pad(a, ((0, 0), (0, 0), (0, _FFN_SHARD_PAD - _FFN_SHARD))).reshape(a.shape[0], D_FF_PAD)


def _ffn_vec_shards(a):
    return a.reshape(a.shape[0], N_DEV, _FFN_SHARD_PAD)[:, :, :_FFN_SHARD]


_MESH = pl.DeviceIdType.MESH
_ANY = pl.BlockSpec(memory_space=pl.ANY)


def _place():
    x, y, c = lax.axis_index("x"), lax.axis_index("y"), lax.axis_index("c")
    chips = [(1 - x, y), (x, 1 - y), (1 - x, 1 - y)]
    return x, y, c, chips


def _handshake(peers):
    barrier = pltpu.get_barrier_semaphore()
    for peer in peers:
        pl.semaphore_signal(barrier, inc=1, device_id=peer, device_id_type=_MESH)
    pl.semaphore_wait(barrier, len(peers))


_SEQUENCER = dict(axis_name="seq", num_cores=1)
_AG_COLLECTIVE_ID = 1
_RS_SIBLING_COLLECTIVE_ID = 2
_RS_CHIPS_COLLECTIVE_ID = 3


def _all_gather_seq(shard, *, name):
    def body(x_ref, out_ref, send_sems, recv_sems, local_sem):
        x, y, c, chips = _place()
        sibling = (x, y, 1 - c)
        _handshake([sibling] + [(*chip, c) for chip in chips])

        def slot(px, py, pc):
            return out_ref.at[4 * px + 2 * py + pc]

        def copy(k, block, to, src=None):
            return pltpu.make_async_remote_copy(
                src_ref=slot(*block) if src is None else src, dst_ref=slot(*block),
                send_sem=send_sems.at[k], recv_sem=recv_sems.at[k], device_id=to, device_id_type=_MESH)

        mine = pltpu.make_async_copy(x_ref, slot(x, y, c), local_sem)
        mine.start()
        first = [copy(0, (x, y, c), sibling, src=x_ref)]
        first += [copy(1 + j, (x, y, c), (*chip, c), src=x_ref) for j, chip in enumerate(chips)]
        for cp in first:
            cp.start()
        passed = [copy(4 + j, (*chip, c), sibling) for j, chip in enumerate(chips)]
        for j, chip in enumerate(chips):
            copy(1 + j, (*chip, c), (x, y, c)).wait_recv()
            passed[j].start()
        copy(0, sibling, (x, y, c)).wait_recv()
        for j, chip in enumerate(chips):
            copy(4 + j, (*chip, 1 - c), (x, y, c)).wait_recv()
        for cp in first + passed:
            cp.wait_send()
        mine.wait()

    return pl.kernel(
        body, out_type=jax.ShapeDtypeStruct((N_DEV,) + shard.shape, shard.dtype),
        mesh=plsc.ScalarSubcoreMesh(**_SEQUENCER), name=name,
        scratch_types=[pltpu.SemaphoreType.DMA((7,)), pltpu.SemaphoreType.DMA((7,)), pltpu.SemaphoreType.DMA(())],
        compiler_params=pltpu.CompilerParams(collective_id=_AG_COLLECTIVE_ID),
    )(shard)


def _exchange_sibling(g, *, name):
    def body(g_ref, out_ref, send_sems, recv_sems):
        x, y, c, _ = _place()
        sibling = (x, y, 1 - c)
        _handshake([sibling])
        copies = []
        for chip in range(4):
            cp = pltpu.make_async_remote_copy(
                src_ref=g_ref.at[2 * chip + (1 - c)], dst_ref=out_ref.at[chip],
                send_sem=send_sems.at[chip], recv_sem=recv_sems.at[chip], device_id=sibling, device_id_type=_MESH)
            cp.start()
            copies.append(cp)
        for cp in copies:
            cp.wait_recv()
        for cp in copies:
            cp.wait_send()

    return pl.kernel(
        body, out_type=jax.ShapeDtypeStruct((4,) + g.shape[1:], g.dtype),
        mesh=plsc.ScalarSubcoreMesh(**_SEQUENCER), name=name,
        scratch_types=[pltpu.SemaphoreType.DMA((4,)), pltpu.SemaphoreType.DMA((4,))],
        compiler_params=pltpu.CompilerParams(collective_id=_RS_SIBLING_COLLECTIVE_ID),
    )(g)


def _add_sibling(g, recv, *, name):
    _, r, cdim = g.shape
    tr, tc = _tile_2d(r, cdim, 6)
    g4 = g.reshape(4, 2, r, cdim)
    core = lax.axis_index("c").astype(jnp.int32).reshape(1)

    def body(c_ref, g_ref, r_ref, o_ref):
        o_ref[...] = (g_ref[...].astype(F32) + r_ref[...].astype(F32)).astype(o_ref.dtype)

    return pl.pallas_call(
        body, name=name,
        grid_spec=pltpu.PrefetchScalarGridSpec(
            num_scalar_prefetch=1, grid=(4, r // tr, cdim // tc),
            in_specs=[pl.BlockSpec((None, None, tr, tc), lambda ch, i, j, c_ref: (ch, c_ref[0], i, j)),
                      pl.BlockSpec((None, tr, tc), lambda ch, i, j, c_ref: (ch, i, j))],
            out_specs=pl.BlockSpec((None, tr, tc), lambda ch, i, j, c_ref: (ch, i, j))),
        out_shape=jax.ShapeDtypeStruct((4, r, cdim), g.dtype),
        compiler_params=_params(("parallel", "parallel", "parallel")),
    )(core, g4, recv)


def _exchange_chips(pch, *, name):
    def body(p_ref, out_ref, send_sems, recv_sems, local_sem):
        x, y, c, chips = _place()
        _handshake([(*chip, c) for chip in chips])
        me = 2 * x + y
        mine = pltpu.make_async_copy(p_ref.at[me], out_ref.at[me], local_sem)
        mine.start()
        copies = []
        for j, (px, py) in enumerate(chips):
            cp = pltpu.make_async_remote_copy(
                src_ref=p_ref.at[2 * px + py], dst_ref=out_ref.at[me],
                send_sem=send_sems.at[j], recv_sem=recv_sems.at[j], device_id=(px, py, c), device_id_type=_MESH)
            cp.start()
            copies.append(cp)
        for j, (px, py) in enumerate(chips):
            pltpu.make_async_remote_copy(
                src_ref=p_ref.at[me], dst_ref=out_ref.at[2 * px + py],
                send_sem=send_sems.at[j], recv_sem=recv_sems.at[j], device_id=(px, py, c), device_id_type=_MESH).wait_recv()
        for cp in copies:
            cp.wait_send()
        mine.wait()

    return pl.kernel(
        body, out_type=jax.ShapeDtypeStruct(pch.shape, pch.dtype),
        mesh=plsc.ScalarSubcoreMesh(**_SEQUENCER), name=name,
        scratch_types=[pltpu.SemaphoreType.DMA((3,)), pltpu.SemaphoreType.DMA((3,)), pltpu.SemaphoreType.DMA(())],
        compiler_params=pltpu.CompilerParams(collective_id=_RS_CHIPS_COLLECTIVE_ID),
    )(pch)


def _all_reduce_small(v, *, name):
    r = v.shape[0]

    def body(x_ref, out_ref, buf_ref, send_sems, recv_sems):
        x, y, c, chips = _place()
        sibling = (x, y, 1 - c)

        def slot(px, py, pc):
            return buf_ref.at[4 * px + 2 * py + pc]

        def copy(k, block, to, src=None):
            return pltpu.make_async_remote_copy(
                src_ref=slot(*block) if src is None else src, dst_ref=slot(*block),
                send_sem=send_sems.at[k], recv_sem=recv_sems.at[k], device_id=to, device_id_type=_MESH)

        first = [copy(0, (x, y, c), sibling, src=x_ref)]
        first += [copy(1 + j, (x, y, c), (*chip, c), src=x_ref) for j, chip in enumerate(chips)]
        for cp in first:
            cp.start()
        buf_ref[4 * x + 2 * y + c] = x_ref[...]
        passed = [copy(4 + j, (*chip, c), sibling) for j, chip in enumerate(chips)]
        for j, chip in enumerate(chips):
            copy(1 + j, (*chip, c), (x, y, c)).wait_recv()
            passed[j].start()
        copy(0, sibling, (x, y, c)).wait_recv()
        for j, chip in enumerate(chips):
            copy(4 + j, (*chip, 1 - c), (x, y, c)).wait_recv()
        for cp in first + passed:
            cp.wait_send()
        total = buf_ref[0]
        for k in range(1, N_DEV):
            total = total + buf_ref[k]
        out_ref[...] = total

    vm = pl.BlockSpec(memory_space=pltpu.VMEM)
    return pl.pallas_call(
        body, name=name, out_shape=jax.ShapeDtypeStruct(v.shape, v.dtype),
        in_specs=[vm], out_specs=vm,
        scratch_shapes=[pltpu.VMEM((N_DEV,) + v.shape, v.dtype), pltpu.SemaphoreType.DMA((7,)),
                        pltpu.SemaphoreType.DMA((7,))],
    )(v)


_ELEMENTWISE_VMEM = 24 * 1024 * 1024


def _tile_2d(r, c, n_arrays):
    per_block = _ELEMENTWISE_VMEM // (8 * n_arrays)
    tr = _tile(r, max(16, per_block // max(c, 128)), 16)
    by_rows = (tr, c) if tr * c <= per_block else None
    tc = _tile(c, max(128, (per_block // r) // 128 * 128))
    by_cols = (r, tc) if r * tc <= per_block else None
    if by_rows is None or (by_cols is not None and r * tc > tr * c):
        assert by_cols is not None, (r, c, n_arrays)
        return by_cols
    return by_rows


def _adam_math(w, g, m, v):
    m = ADAM_B1 * m + (1.0 - ADAM_B1) * g
    v = ADAM_B2 * v + (1.0 - ADAM_B2) * jnp.square(g)
    m_hat = m / (1.0 - ADAM_B1 ** ADAM_STEP)
    v_hat = v / (1.0 - ADAM_B2 ** ADAM_STEP)
    delta = -ADAM_LR * (m_hat / (jnp.sqrt(v_hat) + ADAM_EPS) + ADAM_WD * w)
    return delta, m, v


def _adam(w, g, m, v, *, name, parts=None):
    r, cdim = w.shape
    tr, tc = _tile_2d(r, cdim, 8)

    def body(w_ref, g_ref, m_ref, v_ref, go_ref, d_ref, mo_ref, vo_ref):
        if parts is None:
            g = g_ref[...]
        else:
            g = g_ref[0].astype(F32)
            for k in range(1, parts):
                g = g + g_ref[k].astype(F32)
        delta, m, v = _adam_math(w_ref[...], g, m_ref[...], v_ref[...])
        go_ref[...] = g
        d_ref[...] = delta
        mo_ref[...] = m
        vo_ref[...] = v

    blk = pl.BlockSpec((tr, tc), lambda i, j: (i, j))
    gblk = blk if parts is None else pl.BlockSpec((parts, tr, tc), lambda i, j: (0, i, j))
    out = jax.ShapeDtypeStruct((r, cdim), F32)
    return pl.pallas_call(
        body, name=name, grid=(r // tr, cdim // tc), in_specs=[blk, gblk, blk, blk], out_specs=[blk] * 4,
        out_shape=[out] * 4, compiler_params=_params(("parallel", "parallel")),
    )(w, g, m, v)


_BIG = ("w_in", "w_uq", "w_ukv", "w_o", "w_ffn_gate", "w_ffn_up", "w_ffn_down", "w_ple_gate", "w_ple_proj")
_WEIGHTS = ("w_in", "g_attn", "g_q_lora", "g_kv_lora", "w_uq", "w_ukv", "w_o", "g_ffn", "w_ffn_gate", "w_ffn_up",
            "conv_w", "conv_b", "w_ffn_down", "g_ple", "w_ple_gate", "w_ple_proj", "g_final")
_SMALL_PACK = (("g_attn", 1, D_MODEL), ("g_q_lora", 1, Q_LORA), ("g_kv_lora", 1, KV_LORA), ("g_ffn", 1, D_MODEL),
               ("conv_w", CONV_WIDTH, D_FF_PAD), ("conv_b", 1, D_FF_PAD), ("g_ple", 1, D_MODEL), ("g_final", 1, D_MODEL))


def _pack_small(gs):
    flat = jnp.concatenate([gs[n].reshape(-1) for n, _, _ in _SMALL_PACK])
    rows = -(-flat.shape[0] // 128)
    rows = -(-rows // 8) * 8
    return jnp.pad(flat, (0, rows * 128 - flat.shape[0])).reshape(rows, 128)


def _unpack_small(packed):
    flat = packed.reshape(-1)
    out, off = {}, 0
    for n, r, c in _SMALL_PACK:
        out[n] = flat[off:off + r * c].reshape(r, c)
        off += r * c
    return out


def kernel(x, p, w_in, g_attn, g_q_lora, g_kv_lora, w_uq, w_ukv, w_o, g_ffn, w_ffn_gate, w_ffn_up, conv_w, conv_b, w_ffn_down, g_ple, w_ple_gate, w_ple_proj, g_final, loss_target, m_w_in, m_g_attn, m_g_q_lora, m_g_kv_lora, m_w_uq, m_w_ukv, m_w_o, m_g_ffn, m_w_ffn_gate, m_w_ffn_up, m_conv_w, m_conv_b, m_w_ffn_down, m_g_ple, m_w_ple_gate, m_w_ple_proj, m_g_final, v_w_in, v_g_attn, v_g_q_lora, v_g_kv_lora, v_w_uq, v_w_ukv, v_w_o, v_g_ffn, v_w_ffn_gate, v_w_ffn_up, v_conv_w, v_conv_b, v_w_ffn_down, v_g_ple, v_w_ple_gate, v_w_ple_proj, v_g_final):
    given = dict(locals())
    wts = {n: given[n] for n in _WEIGHTS}
    mom = {n: given["m_" + n] for n in _WEIGHTS}
    var = {n: given["v_" + n] for n in _WEIGHTS}
    me = (4 * lax.axis_index("x") + 2 * lax.axis_index("y") + lax.axis_index("c")).astype(jnp.int32)
    ops = _ShardedWeights(wts, mom, var)

    conv_full = _all_gather_seq(conv_w, name="ag_conv_w")[:, 0].transpose(1, 0, 2).reshape(CONV_WIDTH, D_FF)
    vec = {"g_attn": g_attn, "g_q_lora": g_q_lora, "g_kv_lora": g_kv_lora, "g_ffn": g_ffn, "g_ple": g_ple,
           "g_final": g_final[None, :], "conv_w": _ffn_vec_layout(conv_full), "conv_b": _ffn_vec_layout(conv_b)}

    loss_part, grad_x, gs = _local_step(x[0], p[0, 0], loss_target[0], vec, ops)
    loss = lax.psum(loss_part[0, 0], ("x", "y", "c"))

    small = _unpack_small(_all_reduce_small(ops.order.tie(_pack_small(gs)), name="ar_small"))
    conv_w_shards = _ffn_vec_shards(small["conv_w"])
    small_g = {
        "g_attn": small["g_attn"], "g_q_lora": small["g_q_lora"], "g_kv_lora": small["g_kv_lora"],
        "g_ffn": small["g_ffn"], "g_ple": small["g_ple"], "g_final": small["g_final"],
        "conv_b": _ffn_vec_shards(small["conv_b"]).reshape(1, D_FF),
        "conv_w": lax.dynamic_index_in_dim(conv_w_shards, me, axis=1, keepdims=False),
    }
    results = dict(ops.results)
    for n, g in small_g.items():
        shape = wts[n].shape
        outs = ops.order.run(_adam, wts[n].reshape(g.shape), g, mom[n].reshape(g.shape), var[n].reshape(g.shape),
                             name="adam_" + n)
        results[n] = tuple(a.reshape(shape) for a in outs)
    ops.update("w_in")
    results["w_in"] = ops.results["w_in"]

    return (loss, grad_x[None], *[results[n][0] for n in _WEIGHTS], *[results[n][1] for n in _WEIGHTS],
            *[results[n][2] for n in _WEIGHTS], *[results[n][3] for n in _WEIGHTS])


class _ShardedWeights:
    def __init__(self, wts, mom, var):
        self.wts, self.mom, self.var = wts, mom, var
        self.order = _Order()
        self.full, self.stage, self.results = {}, {}, {}

    def start_gather(self, name, after=None, halves=False):
        payload = _shard_payload(name, self.wts[name][0])
        if after is not None:
            payload = lax.optimization_barrier((payload, after))[0]
        if not halves:
            self.full[name] = _full_from_gathered(name, _all_gather_seq(payload, name="ag_" + name))
            return
        half = payload.shape[1] // 2
        for part in range(2):
            piece = payload[:, part * half:(part + 1) * half]
            self.full[f"{name}/{part}"] = _full_from_gathered(name, _all_gather_seq(piece, name=f"ag_{name}_{part}"))

    def weight(self, name):
        return self.full[name]

    def grad(self, name, gfull):
        chunks = _grad_chunks(name, gfull)
        self.stage[name] = (chunks, _exchange_sibling(chunks, name="rs_sib_" + name))

    def reduce_add(self, name):
        chunks, from_sibling = self.stage[name]
        per_chip = self.order.run(_add_sibling, chunks, from_sibling, name="rs_add_" + name)
        self.stage[name] = _exchange_chips(per_chip, name="rs_chip_" + name)

    def update(self, name):
        parts = self.stage[name]
        if name == "w_uq":
            parts = _unpad_rows(name, parts)
        rows = lambda a: _rows_view(name, a[0])
        outs = self.order.run(_adam, rows(self.wts[name]), parts, rows(self.mom[name]), rows(self.var[name]),
                              name="adam_" + name, parts=4)
        self.results[name] = tuple(_rows_view(name, a)[None] for a in outs)
```

```python
import functools

import numpy as np

import jax
import jax.numpy as jnp
from jax import lax
from jax.experimental import pallas as pl
from jax.experimental.pallas import tpu as pltpu
from jax.experimental.pallas import tpu_sc as plsc

D_MODEL = 4096
CHUNK = 64
PLE_DIM = 256
RET_HEADS = 8
RET_HEAD_DIM = 256
RET_WIDTH = 2048
MLA_HEADS = 16
MLA_NOPE = 128
MLA_ROPE = 64
MLA_V = 128
Q_LORA = 1024
KV_LORA = 512
D_FF = 11008
CONV_WIDTH = 3
ROPE_BASE = 10000.0
EPS = 1e-6
IN_WIDTH = 9792
ADAM_LR, ADAM_B1, ADAM_B2, ADAM_EPS, ADAM_WD, ADAM_STEP = 0.001, 0.9, 0.999, 1e-08, 0.01, 10

IN_WIDTH_PAD = 10240
D_FF_PAD = 11264
MLA_QK_PAD = 256
Q_WIDTH_PAD = MLA_HEADS * MLA_QK_PAD

N_DEV = 8
MXU_DTYPE = jnp.bfloat16
ATTN_BLOCK = 512
HEADS_PER_STEP = 4
FWD_HEADS_PER_STEP = 8
VMEM_LIMIT = 56 * 1024 * 1024

F32 = jnp.float32


def _tile(n, want, align=128):
    if n <= want:
        return n
    t = (want // align) * align
    while t >= align:
        if n % t == 0:
            return t
        t -= align
    return n


def _params(sem):
    return pltpu.CompilerParams(dimension_semantics=sem, vmem_limit_bytes=VMEM_LIMIT)


def _sigmoid(x):
    return 1.0 / (1.0 + jnp.exp(-x))


def _matmul(a, b, *, name, ta=False, tb=False, out_dtype=F32, add=None, tm=1024, tn=1024, tk=4096):
    m, k = (a.shape[1], a.shape[0]) if ta else a.shape
    k2, n = (b.shape[1], b.shape[0]) if tb else b.shape
    assert k == k2, (a.shape, b.shape, ta, tb)
    tm, tn, tk = _tile(m, tm), _tile(n, tn), _tile(k, tk)
    nk = k // tk
    dims = (((0 if ta else 1,), (1 if tb else 0,)), ((), ()))

    def body(*refs):
        a_ref, b_ref, o_ref = refs[0], refs[1], refs[3 if add is not None else 2]
        c_ref = refs[2] if add is not None else None
        part = lax.dot_general(a_ref[...].astype(MXU_DTYPE), b_ref[...].astype(MXU_DTYPE), dims,
                               preferred_element_type=F32)

        def finish(r):
            if c_ref is not None:
                r = r + c_ref[...].astype(F32)
            o_ref[...] = r.astype(out_dtype)

        if nk == 1:
            finish(part)
            return
        acc_ref = refs[-1]
        kk = pl.program_id(2)

        @pl.when(kk == 0)
        def _():
            acc_ref[...] = part

        @pl.when((kk > 0) & (kk < nk - 1))
        def _():
            acc_ref[...] += part

        @pl.when(kk == nk - 1)
        def _():
            finish(acc_ref[...] + part)

    a_spec = pl.BlockSpec((tk, tm), lambda i, j, kk: (kk, i)) if ta else pl.BlockSpec((tm, tk), lambda i, j, kk: (i, kk))
    b_spec = pl.BlockSpec((tn, tk), lambda i, j, kk: (j, kk)) if tb else pl.BlockSpec((tk, tn), lambda i, j, kk: (kk, j))
    in_specs = [a_spec, b_spec]
    args = [a, b]
    if add is not None:
        in_specs.append(pl.BlockSpec((tm, tn), lambda i, j, kk: (i, j)))
        args.append(add)
    return pl.pallas_call(
        body, name=name, grid=(m // tm, n // tn, nk),
        in_specs=in_specs, out_specs=pl.BlockSpec((tm, tn), lambda i, j, kk: (i, j)),
        out_shape=jax.ShapeDtypeStruct((m, n), out_dtype),
        scratch_shapes=[] if nk == 1 else [pltpu.VMEM((tm, tn), F32)],
        compiler_params=_params(("parallel", "parallel", "arbitrary")),
    )(*args)


def _grad_matmul_half(a, b, core_class, *, name, add=None):
    t, m = a.shape
    r = m // N_DEV
    cdim = b.shape[1]
    tn = _tile(cdim, 512 if r > 1024 else 1024)
    cls = jnp.asarray(core_class, jnp.int32).reshape(1)

    def body(cls_ref, a_ref, b_ref, *rest):
        part = lax.dot_general(a_ref[...].astype(MXU_DTYPE), b_ref[...].astype(MXU_DTYPE), _TN,
                               preferred_element_type=F32)
        if add is not None:
            part = part + rest[0][...].astype(F32)
        rest[-1][...] = part.astype(rest[-1].dtype)

    slot = pl.BlockSpec((None, r, tn), lambda q, j, cls_ref: (q, 0, j))
    in_specs = [pl.BlockSpec((t, r), lambda q, j, cls_ref: (0, 2 * q + cls_ref[0])),
                pl.BlockSpec((t, tn), lambda q, j, cls_ref: (0, j))] + ([slot] if add is not None else [])
    args = [a, b] + ([add] if add is not None else [])
    return pl.pallas_call(
        body, name=name,
        grid_spec=pltpu.PrefetchScalarGridSpec(
            num_scalar_prefetch=1, grid=(N_DEV // 2, cdim // tn), in_specs=in_specs, out_specs=slot),
        out_shape=jax.ShapeDtypeStruct((N_DEV // 2, r, cdim), MXU_DTYPE),
        compiler_params=_params(("parallel", "parallel")),
    )(cls, *args)


def _rms_fwd(x, g, *, name):
    t, d = x.shape
    tr = _tile(t, 256, 8)

    def body(x_ref, g_ref, o_ref):
        xf = x_ref[...]
        r = lax.rsqrt(jnp.mean(xf * xf, axis=-1, keepdims=True) + EPS)
        o_ref[...] = (xf * r * g_ref[...]).astype(o_ref.dtype)

    return pl.pallas_call(
        body, name=name, grid=(t // tr,),
        in_specs=[pl.BlockSpec((tr, d), lambda i: (i, 0)), pl.BlockSpec((1, d), lambda i: (0, 0))],
        out_specs=pl.BlockSpec((tr, d), lambda i: (i, 0)),
        out_shape=jax.ShapeDtypeStruct((t, d), MXU_DTYPE),
        compiler_params=_params(("parallel",)),
    )(x, g)


def _rms_bwd(x, dhn, g, res, *, name, low_copy=False):
    t, d = x.shape
    tr = _tile(t, 256, 8)

    def body(*refs):
        if res is None:
            x_ref, dh_ref, g_ref = refs[:3]
            outs = refs[3:]
            res_ref = None
        else:
            x_ref, dh_ref, g_ref, res_ref = refs[:4]
            outs = refs[4:]
        dx_ref, dg_ref = outs[0], outs[-1]
        xf = x_ref[...]
        dh = dh_ref[...].astype(F32)
        r = lax.rsqrt(jnp.mean(xf * xf, axis=-1, keepdims=True) + EPS)
        dyg = dh * g_ref[...]
        dx = r * dyg - xf * (r * r * r) * jnp.mean(dyg * xf, axis=-1, keepdims=True)
        if res_ref is not None:
            dx = dx + res_ref[...]
        dx_ref[...] = dx
        if low_copy:
            outs[1][...] = dx.astype(outs[1].dtype)
        part = jnp.sum(dh * xf * r, axis=0, keepdims=True)

        @pl.when(pl.program_id(0) == 0)
        def _():
            dg_ref[...] = part

        @pl.when(pl.program_id(0) > 0)
        def _():
            dg_ref[...] += part

    row = pl.BlockSpec((tr, d), lambda i: (i, 0))
    vec = pl.BlockSpec((1, d), lambda i: (0, 0))
    in_specs = [row, row, vec] + ([] if res is None else [row])
    args = [x, dhn, g] + ([] if res is None else [res])
    out_specs = [row] + ([row] if low_copy else []) + [vec]
    out_shape = [jax.ShapeDtypeStruct((t, d), F32)] + ([jax.ShapeDtypeStruct((t, d), MXU_DTYPE)] if low_copy else []) \
        + [jax.ShapeDtypeStruct((1, d), F32)]
    return pl.pallas_call(
        body, name=name, grid=(t // tr,), in_specs=in_specs, out_specs=out_specs, out_shape=out_shape,
        compiler_params=_params(("arbitrary",)),
    )(*args)


def _rope_tables(t):
    pos = jnp.arange(t, dtype=F32)[:, None]
    inv_r = 1.0 / (ROPE_BASE ** (jnp.arange(0, RET_HEAD_DIM, 2, dtype=F32) / RET_HEAD_DIM))
    ang_r = pos * inv_r[None, :]
    inv_m = 1.0 / (ROPE_BASE ** (jnp.arange(0, MLA_ROPE, 2, dtype=F32) / MLA_ROPE))
    ang_m = pos * inv_m[None, :]
    cm, sm = jnp.cos(ang_m), jnp.sin(ang_m)
    z = jnp.zeros_like(cm)
    cos_m = jnp.concatenate([cm, cm, z, z], axis=1)
    sin_m = jnp.concatenate([-sm, sm, z, z], axis=1)
    return jnp.cos(ang_r), jnp.sin(ang_r), cos_m, sin_m


def _rope256(x, c, s, inverse=False):
    x1, x2 = x[:, :128], x[:, 128:]
    if inverse:
        s = -s
    return jnp.concatenate([x1 * c - x2 * s, x2 * c + x1 * s], axis=1)


def _rope64(x, cos_m, sin_m, inverse=False):
    lane = lax.broadcasted_iota(jnp.int32, x.shape, 1)
    partner = jnp.where(lane < 32, pltpu.roll(x, 96, 1), pltpu.roll(x, 32, 1))
    s = -sin_m if inverse else sin_m
    return x * cos_m + partner * s


def _ret_prep(proj, cos_r, sin_r):
    t = proj.shape[0]
    tr = _tile(t, 256, 8)

    def body(q_ref, k_ref, v_ref, c_ref, s_ref, qo_ref, ko_ref, vo_ref):
        c, s = c_ref[...], s_ref[...]
        for h in range(RET_HEADS):
            cols = slice(h * RET_HEAD_DIM, (h + 1) * RET_HEAD_DIM)
            qo_ref[:, cols] = _rope256(q_ref[:, cols], c, s).astype(qo_ref.dtype)
            ko_ref[:, cols] = (_rope256(k_ref[:, cols], c, s) * (RET_HEAD_DIM ** -0.5)).astype(ko_ref.dtype)
        vo_ref[...] = v_ref[...].astype(vo_ref.dtype)

    group = lambda off: pl.BlockSpec((tr, RET_WIDTH), lambda i: (i, off))
    tab = pl.BlockSpec((tr, 128), lambda i: (i, 0))
    out = jax.ShapeDtypeStruct((t, RET_WIDTH), MXU_DTYPE)
    return pl.pallas_call(
        body, name="ret_prep", grid=(t // tr,),
        in_specs=[group(0), group(1), group(2), tab, tab],
        out_specs=[group(0), group(0), group(0)], out_shape=[out, out, out],
        compiler_params=_params(("parallel",)),
    )(proj, proj, proj, cos_r, sin_r)


def _ret_log_gamma():
    return jnp.asarray(np.log1p(-np.exp2(-5.0 - np.arange(RET_HEADS, dtype=np.float64))), dtype=F32)


def _decay_full(lg, i, j, blk):
    r = lax.broadcasted_iota(jnp.int32, (blk, 1), 0).astype(F32)
    c = lax.broadcasted_iota(jnp.int32, (1, blk), 1).astype(F32)
    off = ((i - j) * blk).astype(F32)
    return jnp.exp(lg * r), jnp.exp(lg * (off - c))


def _decay_diag(lg, blk):
    r = lax.broadcasted_iota(jnp.int32, (blk, blk), 0)
    c = lax.broadcasted_iota(jnp.int32, (blk, blk), 1)
    ok = (c // CHUNK) <= (r // CHUNK)
    return jnp.where(ok, jnp.exp(lg * jnp.abs(r - c).astype(F32)), 0.0)


_NT = (((1,), (1,)), ((), ()))
_TN = (((0,), (0,)), ((), ()))
_NN = (((1,), (0,)), ((), ()))


def _causal_pairs(nb, query_major):
    if query_major:
        pairs = [(i, j) for i in range(nb) for j in range(i + 1)]
    else:
        pairs = [(i, j) for j in range(nb) for i in range(j, nb)]
    arr = np.asarray(pairs, dtype=np.int32)
    return jnp.asarray(arr[:, 0]), jnp.asarray(arr[:, 1])


def _ret_fwd(q, k, v, proj, lg):
    t = q.shape[0]
    blk = _tile(t, ATTN_BLOCK)
    nb = t // blk
    hps, d = FWD_HEADS_PER_STEP, RET_HEAD_DIM
    gate_off = 3 * RET_WIDTH // (hps * d)

    def body(ii_ref, jj_ref, lg_ref, q_ref, k_ref, v_ref, g_ref, raw_ref, ro_ref, acc_ref):
        hg, pair = pl.program_id(0), pl.program_id(1)
        i, j = ii_ref[pair], jj_ref[pair]

        @pl.when(j == 0)
        def _():
            acc_ref[...] = jnp.zeros_like(acc_ref)

        def step(diag):
            for h in range(hps):
                cols = slice(h * d, (h + 1) * d)
                lgh = lg_ref[hg * hps + h]
                s = lax.dot_general(q_ref[:, cols], k_ref[:, cols], _NT, preferred_element_type=F32)
                if diag:
                    w = s * _decay_diag(lgh, blk)
                else:
                    a, b = _decay_full(lgh, i, j, blk)
                    w = s * a * b
                acc_ref[h] += lax.dot_general(w.astype(MXU_DTYPE), v_ref[:, cols], _NN, preferred_element_type=F32)

        @pl.when(j < i)
        def _():
            step(False)

        @pl.when(j == i)
        def _():
            step(True)
            for h in range(hps):
                cols = slice(h * d, (h + 1) * d)
                o = acc_ref[h]
                raw_ref[:, cols] = o
                mu = jnp.mean(o, axis=-1, keepdims=True)
                var = jnp.mean(jnp.square(o - mu), axis=-1, keepdims=True)
                hn = (o - mu) * lax.rsqrt(var + EPS)
                g = g_ref[:, cols]
                ro_ref[:, cols] = (g * _sigmoid(g) * hn).astype(ro_ref.dtype)

    qs = pl.BlockSpec((blk, hps * d), lambda h, p, ii, jj: (ii[p], h))
    ks = pl.BlockSpec((blk, hps * d), lambda h, p, ii, jj: (jj[p], h))
    gs = pl.BlockSpec((blk, hps * d), lambda h, p, ii, jj: (ii[p], h + gate_off))
    ii, jj = _causal_pairs(nb, query_major=True)
    return pl.pallas_call(
        body, name="ret_fwd",
        grid_spec=pltpu.PrefetchScalarGridSpec(
            num_scalar_prefetch=2, grid=(RET_HEADS // hps, ii.shape[0]),
            in_specs=[pl.BlockSpec(memory_space=pltpu.SMEM), qs, ks, ks, gs], out_specs=[qs, qs],
            scratch_shapes=[pltpu.VMEM((hps, blk, d), F32)]),
        out_shape=[jax.ShapeDtypeStruct((t, RET_WIDTH), F32), jax.ShapeDtypeStruct((t, RET_WIDTH), MXU_DTYPE)],
        compiler_params=_params(("parallel", "arbitrary")),
    )(ii, jj, lg, q, k, v, proj)


def _ret_gate_bwd(raw, proj, dattn):
    t = raw.shape[0]
    tr = _tile(t, 256, 8)

    def body(o_ref, g_ref, d_ref, do_ref, dg_ref):
        for h in range(RET_HEADS):
            cols = slice(h * RET_HEAD_DIM, (h + 1) * RET_HEAD_DIM)
            o, g, d = o_ref[:, cols], g_ref[:, cols], d_ref[:, cols]
            mu = jnp.mean(o, axis=-1, keepdims=True)
            rstd = lax.rsqrt(jnp.mean(jnp.square(o - mu), axis=-1, keepdims=True) + EPS)
            hn = (o - mu) * rstd
            sg = _sigmoid(g)
            dg_ref[:, cols] = (d * hn * (sg * (1.0 + g * (1.0 - sg)))).astype(dg_ref.dtype)
            dhn = d * (g * sg)
            do = rstd * (dhn - jnp.mean(dhn, axis=-1, keepdims=True)
                         - hn * jnp.mean(dhn * hn, axis=-1, keepdims=True))
            do_ref[:, cols] = do.astype(do_ref.dtype)

    group = lambda off: pl.BlockSpec((tr, RET_WIDTH), lambda i: (i, off))
    out = jax.ShapeDtypeStruct((t, RET_WIDTH), MXU_DTYPE)
    return pl.pallas_call(
        body, name="ret_gate_bwd", grid=(t // tr,),
        in_specs=[group(0), group(3), group(0)], out_specs=[group(0), group(0)], out_shape=[out, out],
        compiler_params=_params(("parallel",)),
    )(raw, proj, dattn)


def _ret_bwd(q, k, v, do, lg):
    t = q.shape[0]
    blk = _tile(t, ATTN_BLOCK)
    nb = t // blk

    hps, d = HEADS_PER_STEP, RET_HEAD_DIM

    def body(ii_ref, jj_ref, lg_ref, q_ref, k_ref, v_ref, do_ref, dq_ref, dk_ref, dv_ref, dk_acc, dv_acc):
        hg, pair = pl.program_id(0), pl.program_id(1)
        i, j = ii_ref[pair], jj_ref[pair]

        @pl.when(pair == 0)
        def _():
            dq_ref[...] = jnp.zeros_like(dq_ref)

        @pl.when(i == j)
        def _():
            dk_acc[...] = jnp.zeros_like(dk_acc)
            dv_acc[...] = jnp.zeros_like(dv_acc)

        def step(diag):
            rows = pl.ds(pl.multiple_of(i * blk, blk), blk)
            for h in range(hps):
                cols = slice(h * d, (h + 1) * d)
                lgh = lg_ref[hg * hps + h]
                if diag:
                    decay = _decay_diag(lgh, blk)
                else:
                    a, b = _decay_full(lgh, i, j, blk)
                    decay = a * b
                qb, kb, vb, dob = q_ref[:, cols], k_ref[:, cols], v_ref[:, cols], do_ref[:, cols]
                s = lax.dot_general(qb, kb, _NT, preferred_element_type=F32)
                w = (s * decay).astype(MXU_DTYPE)
                dv_acc[h] += lax.dot_general(w, dob, _TN, preferred_element_type=F32)
                dw = lax.dot_general(dob, vb, _NT, preferred_element_type=F32)
                ds = (dw * decay).astype(MXU_DTYPE)
                dq_ref[rows, cols] += lax.dot_general(ds, kb, _NN, preferred_element_type=F32)
                dk_acc[h] += lax.dot_general(ds, qb, _TN, preferred_element_type=F32)

        @pl.when(i > j)
        def _():
            step(False)

        @pl.when(i == j)
        def _():
            step(True)

        @pl.when(i == nb - 1)
        def _():
            for h in range(hps):
                cols = slice(h * d, (h + 1) * d)
                dk_ref[:, cols] = dk_acc[h]
                dv_ref[:, cols] = dv_acc[h].astype(dv_ref.dtype)

    qs = pl.BlockSpec((blk, hps * d), lambda h, p, ii, jj: (ii[p], h))
    ks = pl.BlockSpec((blk, hps * d), lambda h, p, ii, jj: (jj[p], h))
    ii, jj = _causal_pairs(nb, query_major=False)
    return pl.pallas_call(
        body, name="ret_bwd",
        grid_spec=pltpu.PrefetchScalarGridSpec(
            num_scalar_prefetch=2, grid=(RET_HEADS // hps, ii.shape[0]),
            in_specs=[pl.BlockSpec(memory_space=pltpu.SMEM), qs, ks, ks, qs],
            out_specs=[pl.BlockSpec((t, hps * d), lambda h, p, ii, jj: (0, h), pipeline_mode=pl.Buffered(1)), ks, ks],
            scratch_shapes=[pltpu.VMEM((hps, blk, d), F32), pltpu.VMEM((hps, blk, d), F32)]),
        out_shape=[jax.ShapeDtypeStruct((t, RET_WIDTH), F32), jax.ShapeDtypeStruct((t, RET_WIDTH), F32),
                   jax.ShapeDtypeStruct((t, RET_WIDTH), MXU_DTYPE)],
        compiler_params=_params(("parallel", "arbitrary")),
    )(ii, jj, lg, q, k, v, do)


def _ret_unrope(dq, dk, cos_r, sin_r):
    t = dq.shape[0]
    tr = _tile(t, 256, 8)

    def body(dq_ref, dk_ref, c_ref, s_ref, oq_ref, ok_ref):
        c, s = c_ref[...], s_ref[...]
        for h in range(RET_HEADS):
            cols = slice(h * RET_HEAD_DIM, (h + 1) * RET_HEAD_DIM)
            oq_ref[:, cols] = _rope256(dq_ref[:, cols], c, s, inverse=True).astype(oq_ref.dtype)
            ok_ref[:, cols] = (_rope256(dk_ref[:, cols], c, s, inverse=True)
                               * (RET_HEAD_DIM ** -0.5)).astype(ok_ref.dtype)

    rows = pl.BlockSpec((tr, RET_WIDTH), lambda i: (i, 0))
    tab = pl.BlockSpec((tr, 128), lambda i: (i, 0))
    out = jax.ShapeDtypeStruct((t, RET_WIDTH), MXU_DTYPE)
    return pl.pallas_call(
        body, name="ret_unrope", grid=(t // tr,),
        in_specs=[rows, rows, tab, tab], out_specs=[rows, rows], out_shape=[out, out],
        compiler_params=_params(("parallel",)),
    )(dq, dk, cos_r, sin_r)


def _mla_prep(cq, ckv, kr, g_q, g_kv, cos_m, sin_m):
    t = cq.shape[0]
    tr = _tile(t, 512, 8)

    def body(cq_ref, ckv_ref, kr_ref, gq_ref, gkv_ref, c_ref, s_ref, cqn_ref, kvn_ref, kro_ref):
        for x_ref, g_ref, o_ref in ((cq_ref, gq_ref, cqn_ref), (ckv_ref, gkv_ref, kvn_ref)):
            xf = x_ref[...]
            r = lax.rsqrt(jnp.mean(xf * xf, axis=-1, keepdims=True) + EPS)
            o_ref[...] = (xf * r * g_ref[...]).astype(o_ref.dtype)
        kro_ref[...] = _rope64(kr_ref[...], c_ref[...], s_ref[...]).astype(kro_ref.dtype)

    row = lambda w: pl.BlockSpec((tr, w), lambda i: (i, 0))
    vec = lambda w: pl.BlockSpec((1, w), lambda i: (0, 0))
    return pl.pallas_call(
        body, name="mla_prep", grid=(t // tr,),
        in_specs=[row(Q_LORA), row(KV_LORA), row(128), vec(Q_LORA), vec(KV_LORA), row(128), row(128)],
        out_specs=[row(Q_LORA), row(KV_LORA), row(128)],
        out_shape=[jax.ShapeDtypeStruct((t, Q_LORA), MXU_DTYPE), jax.ShapeDtypeStruct((t, KV_LORA), MXU_DTYPE),
                   jax.ShapeDtypeStruct((t, 128), MXU_DTYPE)],
        compiler_params=_params(("parallel",)),
    )(cq, ckv, kr, g_q, g_kv, cos_m, sin_m)


def _mla_q_rope(q_lin, cos_m, sin_m, *, inverse, name):
    t = q_lin.shape[0]
    tr = _tile(t, 256, 8)

    def body(q_ref, c_ref, s_ref, o_ref):
        c, s = c_ref[...], s_ref[...]
        for h in range(MLA_HEADS):
            lo = h * MLA_QK_PAD
            o_ref[:, lo:lo + MLA_NOPE] = q_ref[:, lo:lo + MLA_NOPE].astype(o_ref.dtype)
            roped = _rope64(q_ref[:, lo + MLA_NOPE:lo + MLA_QK_PAD].astype(F32), c, s, inverse=inverse)
            o_ref[:, lo + MLA_NOPE:lo + MLA_QK_PAD] = roped.astype(o_ref.dtype)

    rows = pl.BlockSpec((tr, Q_WIDTH_PAD), lambda i: (i, 0))
    tab = pl.BlockSpec((tr, 128), lambda i: (i, 0))
    return pl.pallas_call(
        body, name=name, grid=(t // tr,),
        in_specs=[rows, tab, tab], out_specs=rows, out_shape=jax.ShapeDtypeStruct((t, Q_WIDTH_PAD), MXU_DTYPE),
        compiler_params=_params(("parallel",)),
    )(q_lin, cos_m, sin_m)


_MLA_SCALE = (MLA_NOPE + MLA_ROPE) ** -0.5
_LOG2_E = 1.4426950408889634
_MLA_SCALE_LOG2 = _MLA_SCALE * _LOG2_E
_NEG = -1e30


def _mla_mask(blk):
    r = lax.broadcasted_iota(jnp.int32, (blk, blk), 0)
    c = lax.broadcasted_iota(jnp.int32, (blk, blk), 1)
    return (c // CHUNK) <= (r // CHUNK)


def _mla_fwd(q, kv, kr):
    t = q.shape[0]
    blk = _tile(t, ATTN_BLOCK)
    nb = t // blk

    hps, dq_, dv_ = MLA_HEADS, MLA_QK_PAD, MLA_V

    def body(ii_ref, jj_ref, q_ref, kv_ref, kr_ref, o_ref, lse_ref, m_ref, acc_ref):
        pair = pl.program_id(1)
        i, j = ii_ref[pair], jj_ref[pair]

        @pl.when(j == 0)
        def _():
            m_ref[...] = jnp.full_like(m_ref, _NEG)
            acc_ref[...] = jnp.zeros_like(acc_ref)

        def step(masked):
            krb = kr_ref[...]
            ones = jnp.ones((blk, dv_), MXU_DTYPE)
            for h in range(hps):
                kb = jnp.concatenate([kv_ref[:, h * dq_:h * dq_ + MLA_NOPE], krb], axis=1)
                vb = jnp.concatenate([kv_ref[:, h * dq_ + MLA_NOPE:(h + 1) * dq_], ones], axis=1)
                s = lax.dot_general(q_ref[:, h * dq_:(h + 1) * dq_], kb, _NT, preferred_element_type=F32)
                if masked:
                    s = jnp.where(_mla_mask(blk), s, _NEG)
                m_prev = m_ref[h]
                m_new = jnp.maximum(m_prev, jnp.max(s, axis=-1, keepdims=True))
                alpha = jnp.exp2((m_prev - m_new) * _MLA_SCALE_LOG2)
                p = jnp.exp2((s - jnp.tile(m_new, (1, blk // 128))) * _MLA_SCALE_LOG2)
                acc_ref[h] = jnp.tile(alpha, (1, 2)) * acc_ref[h] + lax.dot_general(
                    p.astype(MXU_DTYPE), vb, _NN, preferred_element_type=F32)
                m_ref[h] = m_new

        @pl.when(j < i)
        def _():
            step(False)

        @pl.when(j == i)
        def _():
            step(True)
            for h in range(hps):
                cols = slice(h * dv_, (h + 1) * dv_)
                acc = acc_ref[h]
                row_sum = acc[:, dv_:]
                o_ref[:, cols] = (acc[:, :dv_] / row_sum).astype(o_ref.dtype)
                lse_ref[:, cols] = m_ref[h] * _MLA_SCALE + jnp.log(row_sum)

    os_ = pl.BlockSpec((blk, hps * dv_), lambda h, p, ii, jj: (ii[p], h))
    ii, jj = _causal_pairs(nb, query_major=True)
    return pl.pallas_call(
        body, name="mla_fwd",
        grid_spec=pltpu.PrefetchScalarGridSpec(
            num_scalar_prefetch=2, grid=(MLA_HEADS // hps, ii.shape[0]),
            in_specs=[pl.BlockSpec((blk, hps * dq_), lambda h, p, ii, jj: (ii[p], h)),
                      pl.BlockSpec((blk, hps * dq_), lambda h, p, ii, jj: (jj[p], h)),
                      pl.BlockSpec((blk, 128), lambda h, p, ii, jj: (jj[p], 0))],
            out_specs=[os_, os_],
            scratch_shapes=[pltpu.VMEM((hps, blk, 128), F32), pltpu.VMEM((hps, blk, 2 * dv_), F32)]),
        out_shape=[jax.ShapeDtypeStruct((t, MLA_HEADS * MLA_V), MXU_DTYPE),
                   jax.ShapeDtypeStruct((t, MLA_HEADS * 128), F32)],
        compiler_params=_params(("parallel", "arbitrary")),
    )(ii, jj, q, kv, kr)


def _mla_bwd(q, kv, kr, o, lse, dattn):
    t = q.shape[0]
    blk = _tile(t, ATTN_BLOCK)
    nb = t // blk
    hps, dq_, dv_ = HEADS_PER_STEP, MLA_QK_PAD, MLA_V
    do_off = RET_WIDTH // (hps * dv_)

    def body(ii_ref, jj_ref, q_ref, kv_ref, kr_ref, o_ref, lse_ref, do_ref, dq_ref, dk_ref, dv_ref, dk_acc, dv_acc):
        pair = pl.program_id(1)
        i, j = ii_ref[pair], jj_ref[pair]

        @pl.when(pair == 0)
        def _():
            dq_ref[...] = jnp.zeros_like(dq_ref)

        @pl.when(i == j)
        def _():
            dk_acc[...] = jnp.zeros_like(dk_acc)
            dv_acc[...] = jnp.zeros_like(dv_acc)

        def step(masked):
            krb = kr_ref[...]
            rows = pl.ds(pl.multiple_of(i * blk, blk), blk)
            for h in range(hps):
                qcols, vcols = slice(h * dq_, (h + 1) * dq_), slice(h * dv_, (h + 1) * dv_)
                qb = q_ref[:, qcols]
                kb = jnp.concatenate([kv_ref[:, h * dq_:h * dq_ + MLA_NOPE], krb], axis=1)
                vb = kv_ref[:, h * dq_ + MLA_NOPE:(h + 1) * dq_]
                dof = do_ref[:, vcols]
                dob = dof.astype(MXU_DTYPE)
                s = lax.dot_general(qb, kb, _NT, preferred_element_type=F32)
                if masked:
                    s = jnp.where(_mla_mask(blk), s, _NEG)
                lse2 = lse_ref[:, vcols] * _LOG2_E
                p = jnp.exp2(s * _MLA_SCALE_LOG2 - jnp.tile(lse2, (1, blk // 128)))
                delta = jnp.sum(dof * o_ref[:, vcols].astype(F32), axis=-1, keepdims=True)
                dv_acc[h] += lax.dot_general(p.astype(MXU_DTYPE), dob, _TN, preferred_element_type=F32)
                dp = lax.dot_general(dob, vb, _NT, preferred_element_type=F32)
                ds = (p * (dp - delta) * _MLA_SCALE).astype(MXU_DTYPE)
                dq_ref[rows, qcols] += lax.dot_general(ds, kb, _NN, preferred_element_type=F32)
                dk_acc[h] += lax.dot_general(ds, qb, _TN, preferred_element_type=F32)

        @pl.when(i > j)
        def _():
            step(False)

        @pl.when(i == j)
        def _():
            step(True)

        @pl.when(i == nb - 1)
        def _():
            for h in range(hps):
                dk_ref[:, h * dq_:(h + 1) * dq_] = dk_acc[h]
                dv_ref[:, h * dv_:(h + 1) * dv_] = dv_acc[h].astype(dv_ref.dtype)

    qmap = lambda off: (lambda h, p, ii, jj: (ii[p], h + off))
    kmap = lambda h, p, ii, jj: (jj[p], h)
    ii, jj = _causal_pairs(nb, query_major=False)
    return pl.pallas_call(
        body, name="mla_bwd",
        grid_spec=pltpu.PrefetchScalarGridSpec(
            num_scalar_prefetch=2, grid=(MLA_HEADS // hps, ii.shape[0]),
            in_specs=[pl.BlockSpec((blk, hps * dq_), qmap(0)), pl.BlockSpec((blk, hps * dq_), kmap),
                      pl.BlockSpec((blk, 128), lambda h, p, ii, jj: (jj[p], 0)),
                      pl.BlockSpec((blk, hps * dv_), qmap(0)), pl.BlockSpec((blk, hps * dv_), qmap(0)),
                      pl.BlockSpec((blk, hps * dv_), qmap(do_off))],
            out_specs=[pl.BlockSpec((t, hps * dq_), lambda h, p, ii, jj: (0, h), pipeline_mode=pl.Buffered(1)),
                       pl.BlockSpec((blk, hps * dq_), kmap), pl.BlockSpec((blk, hps * dv_), kmap)],
            scratch_shapes=[pltpu.VMEM((hps, blk, dq_), F32), pltpu.VMEM((hps, blk, dv_), F32)]),
        out_shape=[jax.ShapeDtypeStruct((t, Q_WIDTH_PAD), F32), jax.ShapeDtypeStruct((t, Q_WIDTH_PAD), F32),
                   jax.ShapeDtypeStruct((t, MLA_HEADS * MLA_V), MXU_DTYPE)],
        compiler_params=_params(("parallel", "arbitrary")),
    )(ii, jj, q, kv, kr, o, lse, dattn)


def _mla_kv_grad(dk, dv, cos_m, sin_m):
    t = dk.shape[0]
    tr = _tile(t, 256, 8)

    def body(dk_ref, dv_ref, c_ref, s_ref, dkv_ref, dkr_ref):
        acc = jnp.zeros((tr, 128), F32)
        for h in range(MLA_HEADS):
            dkv_ref[:, h * 256:h * 256 + 128] = dk_ref[:, h * 256:h * 256 + 128].astype(dkv_ref.dtype)
            dkv_ref[:, h * 256 + 128:h * 256 + 256] = dv_ref[:, h * 128:(h + 1) * 128].astype(dkv_ref.dtype)
            acc = acc + dk_ref[:, h * 256 + 128:h * 256 + 256]
        dkr_ref[...] = _rope64(acc, c_ref[...], s_ref[...], inverse=True).astype(dkr_ref.dtype)

    row = lambda w: pl.BlockSpec((tr, w), lambda i: (i, 0))
    return pl.pallas_call(
        body, name="mla_kv_grad", grid=(t // tr,),
        in_specs=[row(Q_WIDTH_PAD), row(MLA_HEADS * MLA_V), row(128), row(128)],
        out_specs=[row(Q_WIDTH_PAD), row(128)],
        out_shape=[jax.ShapeDtypeStruct((t, Q_WIDTH_PAD), MXU_DTYPE), jax.ShapeDtypeStruct((t, 128), MXU_DTYPE)],
        compiler_params=_params(("parallel",)),
    )(dk, dv, cos_m, sin_m)


_FFN_COLS = 256
_FFN_ROWS = 256


def _shift_down(cur, prev8, n):
    out = pltpu.roll(cur, n, 0)
    head = out[:8]
    row = lax.broadcasted_iota(jnp.int32, head.shape, 0)
    for r in range(n):
        head = jnp.where(row == r, prev8[8 - n + r:8 - n + r + 1, :], head)
    return jnp.concatenate([head, out[8:]], axis=0)


def _shift_up(cur, next8, n):
    rows = cur.shape[0]
    out = pltpu.roll(cur, rows - n, 0)
    tail = out[rows - 8:]
    row = lax.broadcasted_iota(jnp.int32, tail.shape, 0)
    for r in range(n):
        tail = jnp.where(row == 8 - n + r, next8[r:r + 1, :], tail)
    return jnp.concatenate([out[:rows - 8], tail], axis=0)


def _conv_pre(g_ref, cw_ref, cb_ref, c, rc):
    r0 = pl.multiple_of(c * rc, rc)
    cur = g_ref[pl.ds(r0, rc), :].astype(F32)
    prev16 = g_ref[pl.ds(pl.multiple_of(jnp.maximum(r0 - 16, 0), 16), 16), :].astype(F32)
    prev8 = jnp.where(c > 0, prev16[8:], 0.0)
    s1, s2 = _shift_down(cur, prev8, 1), _shift_down(cur, prev8, 2)
    a = cb_ref[...] + cw_ref[2:3, :] * cur + cw_ref[1:2, :] * s1 + cw_ref[0:1, :] * s2
    return r0, cur, s1, s2, a


def _ffn_act_fwd(gpre, u, cw, cb):
    t, f = gpre.shape
    tc = _tile(f, _FFN_COLS)
    rc = _tile(t, _FFN_ROWS, 8)

    def body(g_ref, u_ref, cw_ref, cb_ref, o_ref):
        def chunk(c, carry):
            r0, _, _, _, a = _conv_pre(g_ref, cw_ref, cb_ref, c, rc)
            o_ref[pl.ds(r0, rc), :] = (a * _sigmoid(a) * u_ref[pl.ds(r0, rc), :]).astype(o_ref.dtype)
            return carry
        lax.fori_loop(0, t // rc, chunk, 0)

    col = pl.BlockSpec((t, tc), lambda j: (0, j))
    return pl.pallas_call(
        body, name="ffn_act_fwd", grid=(f // tc,),
        in_specs=[col, col, pl.BlockSpec((CONV_WIDTH, tc), lambda j: (0, j)), pl.BlockSpec((1, tc), lambda j: (0, j))],
        out_specs=col, out_shape=jax.ShapeDtypeStruct((t, f), MXU_DTYPE),
        compiler_params=_params(("parallel",)),
    )(gpre, u, cw, cb)


def _ffn_act_bwd(gpre, u, dact, cw, cb):
    t, f = gpre.shape
    tc = _tile(f, _FFN_COLS)
    rc = _tile(t, _FFN_ROWS, 8)
    nc = t // rc

    def body(g_ref, u_ref, d_ref, cw_ref, cb_ref, dg_ref, du_ref, dcw_ref, dcb_ref, da_ref):
        def chunk(c, carry):
            w0, w1, w2, b = carry
            r0, cur, s1, s2, a = _conv_pre(g_ref, cw_ref, cb_ref, c, rc)
            sg = _sigmoid(a)
            d = d_ref[pl.ds(r0, rc), :].astype(F32)
            du_ref[pl.ds(r0, rc), :] = (d * (a * sg)).astype(du_ref.dtype)
            da = d * u_ref[pl.ds(r0, rc), :].astype(F32) * (sg * (1.0 + a * (1.0 - sg)))
            da_ref[pl.ds(r0, rc), :] = da
            return (w0 + jnp.sum(da * s2, axis=0, keepdims=True), w1 + jnp.sum(da * s1, axis=0, keepdims=True),
                    w2 + jnp.sum(da * cur, axis=0, keepdims=True), b + jnp.sum(da, axis=0, keepdims=True))
        z = jnp.zeros((1, tc), F32)
        w0, w1, w2, b = lax.fori_loop(0, nc, chunk, (z, z, z, z))
        dcw_ref[0:1, :] = w0
        dcw_ref[1:2, :] = w1
        dcw_ref[2:3, :] = w2
        dcb_ref[...] = b

        def chunk2(c, carry):
            r0 = pl.multiple_of(c * rc, rc)
            cur = da_ref[pl.ds(r0, rc), :]
            nxt = da_ref[pl.ds(pl.multiple_of(jnp.minimum(r0 + rc, t - 8), 8), 8), :]
            nxt = jnp.where(c < nc - 1, nxt, 0.0)
            dg = cw_ref[2:3, :] * cur + cw_ref[1:2, :] * _shift_up(cur, nxt, 1) + cw_ref[0:1, :] * _shift_up(cur, nxt, 2)
            dg_ref[pl.ds(r0, rc), :] = dg.astype(dg_ref.dtype)
            return carry
        lax.fori_loop(0, nc, chunk2, 0)

    col = pl.BlockSpec((t, tc), lambda j: (0, j))
    w3 = pl.BlockSpec((CONV_WIDTH, tc), lambda j: (0, j))
    w1 = pl.BlockSpec((1, tc), lambda j: (0, j))
    low = jax.ShapeDtypeStruct((t, f), MXU_DTYPE)
    return pl.pallas_call(
        body, name="ffn_act_bwd", grid=(f // tc,),
        in_specs=[col, col, col, w3, w1], out_specs=[col, col, w3, w1],
        out_shape=[low, low, jax.ShapeDtypeStruct((CONV_WIDTH, f), F32), jax.ShapeDtypeStruct((1, f), F32)],
        scratch_shapes=[pltpu.VMEM((t, tc), F32)],
        compiler_params=_params(("parallel",)),
    )(gpre, u, dact, cw, cb)


def _head_fwd_bwd(h2, glin, pp, target, g_final):
    t, d = h2.shape
    tr = _tile(t, 128, 8)

    def body(h_ref, gl_ref, pp_ref, t_ref, g_ref, loss_ref, dh_ref, dgl_ref, dpp_ref, dg_ref):
        gate = _sigmoid(gl_ref[...])
        ppv = pp_ref[...]
        h3 = h_ref[...] + gate * ppv
        r = lax.rsqrt(jnp.mean(h3 * h3, axis=-1, keepdims=True) + EPS)
        yh = h3 * r
        g = g_ref[...]
        diff = yh * g - t_ref[...]
        lpart = 0.5 * jnp.sum(jnp.mean(diff * diff, axis=-1, keepdims=True), axis=0, keepdims=True)
        dy = diff * (1.0 / d)
        dyg = dy * g
        dh3 = r * dyg - h3 * (r * r * r) * jnp.mean(dyg * h3, axis=-1, keepdims=True)
        dh_ref[...] = dh3
        dgl_ref[...] = (dh3 * ppv * gate * (1.0 - gate)).astype(dgl_ref.dtype)
        dpp_ref[...] = (dh3 * gate).astype(dpp_ref.dtype)
        dgp = jnp.sum(dy * yh, axis=0, keepdims=True)

        @pl.when(pl.program_id(0) == 0)
        def _():
            loss_ref[...] = jnp.broadcast_to(lpart, loss_ref.shape)
            dg_ref[...] = dgp

        @pl.when(pl.program_id(0) > 0)
        def _():
            loss_ref[...] += jnp.broadcast_to(lpart, loss_ref.shape)
            dg_ref[...] += dgp

    row = pl.BlockSpec((tr, d), lambda i: (i, 0))
    vec = pl.BlockSpec((1, d), lambda i: (0, 0))
    low = jax.ShapeDtypeStruct((t, d), MXU_DTYPE)
    return pl.pallas_call(
        body, name="head_fwd_bwd", grid=(t // tr,),
        in_specs=[row, row, row, row, vec],
        out_specs=[pl.BlockSpec((8, 128), lambda i: (0, 0)), row, row, row, vec],
        out_shape=[jax.ShapeDtypeStruct((8, 128), F32), jax.ShapeDtypeStruct((t, d), F32), low, low,
                   jax.ShapeDtypeStruct((1, d), F32)],
        compiler_params=_params(("arbitrary",)),
    )(h2, glin, pp, target, g_final)


class _Order:
    def __init__(self):
        self.last = None

    def tie(self, x):
        return x if self.last is None else lax.optimization_barrier((x, self.last))[0]

    def run(self, fn, first, *args, **kwargs):
        out = fn(self.tie(first), *args, **kwargs)
        self.last = out[0] if isinstance(out, (tuple, list)) else out
        return out


def _local_step(x, p, target, vec, ops):
    t = x.shape[0]
    cos_r, sin_r, cos_m, sin_m = _rope_tables(t)
    lg = _ret_log_gamma()
    low = MXU_DTYPE
    run = ops.order.run
    w = ops.weight

    ops.start_gather("w_in", halves=True)
    hn1 = run(_rms_fwd, x, vec["g_attn"], name="rms1_fwd")
    for n in ("w_uq", "w_ukv", "w_o"):
        ops.start_gather(n, after=hn1)
    half = x.shape[1] // 2
    proj = run(_matmul, hn1[:, :half], w("w_in/0"), tb=True, name="mm_proj_a")
    proj = run(_matmul, hn1[:, half:], w("w_in/1"), tb=True, name="mm_proj_b", add=proj)
    ops.start_gather("w_ffn_gate", after=proj)
    rq, rk, rv = run(_ret_prep, proj, cos_r, sin_r)
    ops.start_gather("w_ffn_up", after=rq)
    c0 = 4 * RET_WIDTH
    cq = proj[:, c0:c0 + Q_LORA]
    ckv = proj[:, c0 + Q_LORA:c0 + Q_LORA + KV_LORA]
    kr_in = proj[:, c0 + Q_LORA + KV_LORA:c0 + Q_LORA + KV_LORA + 128]
    cqn, kvn, kr = run(_mla_prep, cq, ckv, kr_in, vec["g_q_lora"], vec["g_kv_lora"], cos_m, sin_m)
    ops.start_gather("w_ffn_down", after=cqn)
    q_lin = run(_matmul, cqn, w("w_uq"), tb=True, name="mm_q")
    q = run(_mla_q_rope, q_lin, cos_m, sin_m, inverse=False, name="mla_q_rope")
    kv = run(_matmul, kvn, w("w_ukv"), tb=True, name="mm_kv", out_dtype=low)
    mo, lse = run(_mla_fwd, q, kv, kr)
    ops.start_gather("w_ple_gate", after=mo)
    ops.start_gather("w_ple_proj", after=mo)
    ret_raw, ro = run(_ret_fwd, rq, rk, rv, proj, lg)
    attn = jnp.concatenate([ro, mo], axis=1)
    h1 = run(_matmul, attn, w("w_o"), name="mm_o", add=x)
    hn2 = run(_rms_fwd, h1, vec["g_ffn"], name="rms2_fwd")
    gpre = run(_matmul, hn2, w("w_ffn_gate"), tb=True, name="mm_gate", out_dtype=low)
    u = run(_matmul, hn2, w("w_ffn_up"), tb=True, name="mm_up", out_dtype=low)
    act = run(_ffn_act_fwd, gpre, u, vec["conv_w"], vec["conv_b"])
    h2 = run(_matmul, act, w("w_ffn_down"), name="mm_down", add=h1)
    hn3 = run(_rms_fwd, h2, vec["g_ple"], name="rms3_fwd")
    glin = run(_matmul, hn3, w("w_ple_gate"), name="mm_ple_gate")
    p_low = p.astype(low)
    pp = run(_matmul, p_low, w("w_ple_proj"), tb=True, name="mm_ple_proj")
    loss_part, dh3, dglin, dpp, dg_final = run(_head_fwd_bwd, h2, glin, pp, target, vec["g_final"])

    ops.grad("w_ple_proj", dpp, p_low)
    ops.grad("w_ple_gate", hn3, dglin)
    dhn3 = run(_matmul, dglin, w("w_ple_gate"), tb=True, name="mm_dhn3", out_dtype=low)
    dh2, dh2_low, dg_ple = run(_rms_bwd, h2, dhn3, vec["g_ple"], dh3, name="rms3_bwd", low_copy=True)
    ops.reduce_add("w_ple_proj")
    ops.reduce_add("w_ple_gate")
    ops.grad("w_ffn_down", act, dh2_low)
    dact = run(_matmul, dh2_low, w("w_ffn_down"), tb=True, name="mm_dact", out_dtype=low)
    ops.reduce_add("w_ffn_down")
    dgpre, du, dconv_w, dconv_b = run(_ffn_act_bwd, gpre, u, dact, vec["conv_w"], vec["conv_b"])
    ops.update("w_ple_proj")
    ops.update("w_ple_gate")
    ops.grad("w_ffn_gate", dgpre, hn2)
    ops.grad("w_ffn_up", du, hn2)
    ops.reduce_add("w_ffn_gate")
    dhn2 = run(_matmul, dgpre, w("w_ffn_gate"), name="mm_dhn2_a")
    ops.reduce_add("w_ffn_up")
    dhn2 = run(_matmul, du, w("w_ffn_up"), name="mm_dhn2_b", add=dhn2, out_dtype=low)
    dh1, dh1_low, dg_ffn = run(_rms_bwd, h1, dhn2, vec["g_ffn"], dh2, name="rms2_bwd", low_copy=True)
    ops.update("w_ffn_down")
    ops.grad("w_o", attn, dh1_low)
    dattn = run(_matmul, dh1_low, w("w_o"), tb=True, name="mm_dattn")
    ops.reduce_add("w_o")

    dq_r, dk_full, dv = run(_mla_bwd, q, kv, kr, mo, lse, dattn)
    ops.update("w_ffn_gate")
    dq_lin = run(_mla_q_rope, dq_r, cos_m, sin_m, inverse=True, name="mla_q_unrope")
    dkv, dkr = run(_mla_kv_grad, dk_full, dv, cos_m, sin_m)
    ops.grad("w_uq", dq_lin, cqn)
    dcqn = run(_matmul, dq_lin, w("w_uq"), name="mm_dcqn")
    ops.grad("w_ukv", dkv, kvn)
    dkvn = run(_matmul, dkv, w("w_ukv"), name="mm_dkvn")
    dcq, dcq_low, dg_q = run(_rms_bwd, cq, dcqn, vec["g_q_lora"], None, name="rmsq_bwd", low_copy=True)
    dckv, dckv_low, dg_kv = run(_rms_bwd, ckv, dkvn, vec["g_kv_lora"], None, name="rmskv_bwd", low_copy=True)
    ops.reduce_add("w_uq")
    ops.reduce_add("w_ukv")

    do_ret, drg = run(_ret_gate_bwd, ret_raw, proj, dattn)
    dq_ret, dk_ret, drv = run(_ret_bwd, rq, rk, rv, do_ret, lg)
    drq, drk = run(_ret_unrope, dq_ret, dk_ret, cos_r, sin_r)

    pad = jnp.zeros((t, IN_WIDTH_PAD - IN_WIDTH - 64), low)
    dproj = jnp.concatenate([drq, drk, drv, drg, dcq_low, dckv_low, dkr, pad], axis=1)
    ops.grad("w_in", dproj, hn1)
    for n in ("w_ffn_up", "w_o", "w_uq", "w_ukv"):
        ops.update(n)
    ops.reduce_add("w_in")
    dhn1 = jnp.concatenate([run(_matmul, dproj, w(f"w_in/{part}"), name=f"mm_dhn1_{part}", out_dtype=low)
                            for part in range(2)], axis=1)
    grad_x, dg_attn = run(_rms_bwd, x, dhn1, vec["g_attn"], dh1, name="rms1_bwd")

    gs = {"g_attn": dg_attn, "g_q_lora": dg_q, "g_kv_lora": dg_kv, "g_ffn": dg_ffn, "conv_w": dconv_w,
          "conv_b": dconv_b, "g_ple": dg_ple, "g_final": dg_final}
    return loss_part, grad_x, gs


_COL_SHARDED = ("w_in", "w_uq", "w_ukv", "w_ffn_gate", "w_ffn_up", "w_ple_proj")
_FFN_SHARD = D_FF // N_DEV
_FFN_SHARD_PAD = D_FF_PAD // N_DEV
_HEADS_PER_SHARD = MLA_HEADS // N_DEV
_QK = MLA_NOPE + MLA_ROPE


def _pad_rows(name, a):
    lead = a.shape[:-2]
    if name == "w_uq":
        a = a.reshape(lead + (_HEADS_PER_SHARD, _QK, a.shape[-1]))
        a = jnp.pad(a, [(0, 0)] * len(lead) + [(0, 0), (0, MLA_QK_PAD - _QK), (0, 0)])
        return a.reshape(lead + (_HEADS_PER_SHARD * MLA_QK_PAD, a.shape[-1]))
    if name in ("w_ffn_gate", "w_ffn_up", "w_ffn_down"):
        return jnp.pad(a, [(0, 0)] * len(lead) + [(0, _FFN_SHARD_PAD - _FFN_SHARD), (0, 0)])
    return a


def _unpad_rows(name, a):
    lead = a.shape[:-2]
    if name == "w_uq":
        a = a.reshape(lead + (_HEADS_PER_SHARD, MLA_QK_PAD, a.shape[-1]))[..., :_QK, :]
        return a.reshape(lead + (_HEADS_PER_SHARD * _QK, a.shape[-1]))
    if name in ("w_ffn_gate", "w_ffn_up", "w_ffn_down"):
        return a[..., :_FFN_SHARD, :]
    return a


def _rows_view(name, a):
    return jnp.swapaxes(a, 0, 1) if name in _COL_SHARDED else a


def _shard_payload(name, shard):
    return _pad_rows(name, _rows_view(name, shard).astype(MXU_DTYPE))


def _full_from_gathered(name, g):
    full = g.reshape(g.shape[0] * g.shape[1], g.shape[2])
    if name == "w_in":
        full = jnp.pad(full, ((0, IN_WIDTH_PAD - IN_WIDTH), (0, 0)))
    return full


def _grad_chunks(name, gfull):
    if name == "w_in":
        gfull = gfull[:IN_WIDTH]
    return gfull.reshape(N_DEV, gfull.shape[0] // N_DEV, gfull.shape[1])


def _ffn_vec_layout(a):
    a = a.reshape(a.shape[0], N_DEV, _FFN_SHARD)
    return jnp.pad(a, ((0, 0), (0, 0), (0, _FFN_SHARD_PAD - _FFN_SHARD))).reshape(a.shape[0], D_FF_PAD)


def _ffn_vec_shards(a):
    return a.reshape(a.shape[0], N_DEV, _FFN_SHARD_PAD)[:, :, :_FFN_SHARD]


_MESH = pl.DeviceIdType.MESH
_ANY = pl.BlockSpec(memory_space=pl.ANY)


def _place():
    x, y, c = lax.axis_index("x"), lax.axis_index("y"), lax.axis_index("c")
    chips = [(1 - x, y), (x, 1 - y), (1 - x, 1 - y)]
    return x, y, c, chips


def _handshake(peers):
    barrier = pltpu.get_barrier_semaphore()
    for peer in peers:
        pl.semaphore_signal(barrier, inc=1, device_id=peer, device_id_type=_MESH)
    pl.semaphore_wait(barrier, len(peers))


_SEQUENCER = dict(axis_name="seq", num_cores=1)
_AG_COLLECTIVE_ID = 1
_RS_SIBLING_COLLECTIVE_ID = 2
_RS_CHIPS_COLLECTIVE_ID = 3


def _all_gather_seq(shard, *, name):
    def body(x_ref, out_ref, send_sems, recv_sems, local_sem):
        x, y, c, chips = _place()
        sibling = (x, y, 1 - c)
        _handshake([sibling] + [(*chip, c) for chip in chips])

        def slot(px, py, pc):
            return out_ref.at[4 * px + 2 * py + pc]

        def copy(k, block, to, src=None):
            return pltpu.make_async_remote_copy(
                src_ref=slot(*block) if src is None else src, dst_ref=slot(*block),
                send_sem=send_sems.at[k], recv_sem=recv_sems.at[k], device_id=to, device_id_type=_MESH)

        mine = pltpu.make_async_copy(x_ref, slot(x, y, c), local_sem)
        mine.start()
        first = [copy(0, (x, y, c), sibling, src=x_ref)]
        first += [copy(1 + j, (x, y, c), (*chip, c), src=x_ref) for j, chip in enumerate(chips)]
        for cp in first:
            cp.start()
        passed = [copy(4 + j, (*chip, c), sibling) for j, chip in enumerate(chips)]
        for j, chip in enumerate(chips):
            copy(1 + j, (*chip, c), (x, y, c)).wait_recv()
            passed[j].start()
        copy(0, sibling, (x, y, c)).wait_recv()
        for j, chip in enumerate(chips):
            copy(4 + j, (*chip, 1 - c), (x, y, c)).wait_recv()
        for cp in first + passed:
            cp.wait_send()
        mine.wait()

    return pl.kernel(
        body, out_type=jax.ShapeDtypeStruct((N_DEV,) + shard.shape, shard.dtype),
        mesh=plsc.ScalarSubcoreMesh(**_SEQUENCER), name=name,
        scratch_types=[pltpu.SemaphoreType.DMA((7,)), pltpu.SemaphoreType.DMA((7,)), pltpu.SemaphoreType.DMA(())],
        compiler_params=pltpu.CompilerParams(collective_id=_AG_COLLECTIVE_ID),
    )(shard)


def _exchange_sibling(g, *, name):
    all_slots = g.shape[0] == N_DEV

    def body(g_ref, out_ref, send_sems, recv_sems):
        x, y, c, _ = _place()
        sibling = (x, y, 1 - c)
        _handshake([sibling])
        copies = []
        for chip in range(4):
            cp = pltpu.make_async_remote_copy(
                src_ref=g_ref.at[2 * chip + (1 - c) if all_slots else chip], dst_ref=out_ref.at[chip],
                send_sem=send_sems.at[chip], recv_sem=recv_sems.at[chip], device_id=sibling, device_id_type=_MESH)
            cp.start()
            copies.append(cp)
        for cp in copies:
            cp.wait_recv()
        for cp in copies:
            cp.wait_send()

    return pl.kernel(
        body, out_type=jax.ShapeDtypeStruct((4,) + g.shape[1:], g.dtype),
        mesh=plsc.ScalarSubcoreMesh(**_SEQUENCER), name=name,
        scratch_types=[pltpu.SemaphoreType.DMA((4,)), pltpu.SemaphoreType.DMA((4,))],
        compiler_params=pltpu.CompilerParams(collective_id=_RS_SIBLING_COLLECTIVE_ID),
    )(g)


def _add_sibling(g, recv, *, name):
    _, r, cdim = g.shape
    tr, tc = _tile_2d(r, cdim, 6)
    g4 = g.reshape(4, 2, r, cdim)
    core = lax.axis_index("c").astype(jnp.int32).reshape(1)

    def body(c_ref, g_ref, r_ref, o_ref):
        o_ref[...] = (g_ref[...].astype(F32) + r_ref[...].astype(F32)).astype(o_ref.dtype)

    return pl.pallas_call(
        body, name=name,
        grid_spec=pltpu.PrefetchScalarGridSpec(
            num_scalar_prefetch=1, grid=(4, r // tr, cdim // tc),
            in_specs=[pl.BlockSpec((None, None, tr, tc), lambda ch, i, j, c_ref: (ch, c_ref[0], i, j)),
                      pl.BlockSpec((None, tr, tc), lambda ch, i, j, c_ref: (ch, i, j))],
            out_specs=pl.BlockSpec((None, tr, tc), lambda ch, i, j, c_ref: (ch, i, j))),
        out_shape=jax.ShapeDtypeStruct((4, r, cdim), g.dtype),
        compiler_params=_params(("parallel", "parallel", "parallel")),
    )(core, g4, recv)


def _exchange_chips(pch, *, name):
    def body(p_ref, out_ref, send_sems, recv_sems, local_sem):
        x, y, c, chips = _place()
        _handshake([(*chip, c) for chip in chips])
        me = 2 * x + y
        mine = pltpu.make_async_copy(p_ref.at[me], out_ref.at[me], local_sem)
        mine.start()
        copies = []
        for j, (px, py) in enumerate(chips):
            cp = pltpu.make_async_remote_copy(
                src_ref=p_ref.at[2 * px + py], dst_ref=out_ref.at[me],
                send_sem=send_sems.at[j], recv_sem=recv_sems.at[j], device_id=(px, py, c), device_id_type=_MESH)
            cp.start()
            copies.append(cp)
        for j, (px, py) in enumerate(chips):
            pltpu.make_async_remote_copy(
                src_ref=p_ref.at[me], dst_ref=out_ref.at[2 * px + py],
                send_sem=send_sems.at[j], recv_sem=recv_sems.at[j], device_id=(px, py, c), device_id_type=_MESH).wait_recv()
        for cp in copies:
            cp.wait_send()
        mine.wait()

    return pl.kernel(
        body, out_type=jax.ShapeDtypeStruct(pch.shape, pch.dtype),
        mesh=plsc.ScalarSubcoreMesh(**_SEQUENCER), name=name,
        scratch_types=[pltpu.SemaphoreType.DMA((3,)), pltpu.SemaphoreType.DMA((3,)), pltpu.SemaphoreType.DMA(())],
        compiler_params=pltpu.CompilerParams(collective_id=_RS_CHIPS_COLLECTIVE_ID),
    )(pch)


def _all_reduce_small(v, *, name):
    r = v.shape[0]

    def body(x_ref, out_ref, buf_ref, send_sems, recv_sems):
        x, y, c, chips = _place()
        sibling = (x, y, 1 - c)

        def slot(px, py, pc):
            return buf_ref.at[4 * px + 2 * py + pc]

        def copy(k, block, to, src=None):
            return pltpu.make_async_remote_copy(
                src_ref=slot(*block) if src is None else src, dst_ref=slot(*block),
                send_sem=send_sems.at[k], recv_sem=recv_sems.at[k], device_id=to, device_id_type=_MESH)

        first = [copy(0, (x, y, c), sibling, src=x_ref)]
        first += [copy(1 + j, (x, y, c), (*chip, c), src=x_ref) for j, chip in enumerate(chips)]
        for cp in first:
            cp.start()
        buf_ref[4 * x + 2 * y + c] = x_ref[...]
        passed = [copy(4 + j, (*chip, c), sibling) for j, chip in enumerate(chips)]
        for j, chip in enumerate(chips):
            copy(1 + j, (*chip, c), (x, y, c)).wait_recv()
            passed[j].start()
        copy(0, sibling, (x, y, c)).wait_recv()
        for j, chip in enumerate(chips):
            copy(4 + j, (*chip, 1 - c), (x, y, c)).wait_recv()
        for cp in first + passed:
            cp.wait_send()
        total = buf_ref[0]
        for k in range(1, N_DEV):
            total = total + buf_ref[k]
        out_ref[...] = total

    vm = pl.BlockSpec(memory_space=pltpu.VMEM)
    return pl.pallas_call(
        body, name=name, out_shape=jax.ShapeDtypeStruct(v.shape, v.dtype),
        in_specs=[vm], out_specs=vm,
        scratch_shapes=[pltpu.VMEM((N_DEV,) + v.shape, v.dtype), pltpu.SemaphoreType.DMA((7,)),
                        pltpu.SemaphoreType.DMA((7,))],
    )(v)


_ELEMENTWISE_VMEM = 24 * 1024 * 1024


def _tile_2d(r, c, n_arrays):
    per_block = _ELEMENTWISE_VMEM // (8 * n_arrays)
    tr = _tile(r, max(16, per_block // max(c, 128)), 16)
    by_rows = (tr, c) if tr * c <= per_block else None
    tc = _tile(c, max(128, (per_block // r) // 128 * 128))
    by_cols = (r, tc) if r * tc <= per_block else None
    if by_rows is None or (by_cols is not None and r * tc > tr * c):
        assert by_cols is not None, (r, c, n_arrays)
        return by_cols
    return by_rows


def _adam_math(w, g, m, v):
    m = ADAM_B1 * m + (1.0 - ADAM_B1) * g
    v = ADAM_B2 * v + (1.0 - ADAM_B2) * jnp.square(g)
    m_hat = m / (1.0 - ADAM_B1 ** ADAM_STEP)
    v_hat = v / (1.0 - ADAM_B2 ** ADAM_STEP)
    delta = -ADAM_LR * (m_hat / (jnp.sqrt(v_hat) + ADAM_EPS) + ADAM_WD * w)
    return delta, m, v


def _adam(w, g, m, v, *, name, parts=None):
    r, cdim = w.shape
    tr, tc = _tile_2d(r, cdim, 8)

    def body(w_ref, g_ref, m_ref, v_ref, go_ref, d_ref, mo_ref, vo_ref):
        if parts is None:
            g = g_ref[...]
        else:
            g = g_ref[0].astype(F32)
            for k in range(1, parts):
                g = g + g_ref[k].astype(F32)
        delta, m, v = _adam_math(w_ref[...], g, m_ref[...], v_ref[...])
        go_ref[...] = g
        d_ref[...] = delta
        mo_ref[...] = m
        vo_ref[...] = v

    blk = pl.BlockSpec((tr, tc), lambda i, j: (i, j))
    gblk = blk if parts is None else pl.BlockSpec((parts, tr, tc), lambda i, j: (0, i, j))
    out = jax.ShapeDtypeStruct((r, cdim), F32)
    return pl.pallas_call(
        body, name=name, grid=(r // tr, cdim // tc), in_specs=[blk, gblk, blk, blk], out_specs=[blk] * 4,
        out_shape=[out] * 4, compiler_params=_params(("parallel", "parallel")),
    )(w, g, m, v)


_BIG = ("w_in", "w_uq", "w_ukv", "w_o", "w_ffn_gate", "w_ffn_up", "w_ffn_down", "w_ple_gate", "w_ple_proj")
_WEIGHTS = ("w_in", "g_attn", "g_q_lora", "g_kv_lora", "w_uq", "w_ukv", "w_o", "g_ffn", "w_ffn_gate", "w_ffn_up",
            "conv_w", "conv_b", "w_ffn_down", "g_ple", "w_ple_gate", "w_ple_proj", "g_final")
_SMALL_PACK = (("g_attn", 1, D_MODEL), ("g_q_lora", 1, Q_LORA), ("g_kv_lora", 1, KV_LORA), ("g_ffn", 1, D_MODEL),
               ("conv_w", CONV_WIDTH, D_FF_PAD), ("conv_b", 1, D_FF_PAD), ("g_ple", 1, D_MODEL), ("g_final", 1, D_MODEL))


def _pack_small(gs):
    flat = jnp.concatenate([gs[n].reshape(-1) for n, _, _ in _SMALL_PACK])
    rows = -(-flat.shape[0] // 128)
    rows = -(-rows // 8) * 8
    return jnp.pad(flat, (0, rows * 128 - flat.shape[0])).reshape(rows, 128)


def _unpack_small(packed):
    flat = packed.reshape(-1)
    out, off = {}, 0
    for n, r, c in _SMALL_PACK:
        out[n] = flat[off:off + r * c].reshape(r, c)
        off += r * c
    return out


def kernel(x, p, w_in, g_attn, g_q_lora, g_kv_lora, w_uq, w_ukv, w_o, g_ffn, w_ffn_gate, w_ffn_up, conv_w, conv_b, w_ffn_down, g_ple, w_ple_gate, w_ple_proj, g_final, loss_target, m_w_in, m_g_attn, m_g_q_lora, m_g_kv_lora, m_w_uq, m_w_ukv, m_w_o, m_g_ffn, m_w_ffn_gate, m_w_ffn_up, m_conv_w, m_conv_b, m_w_ffn_down, m_g_ple, m_w_ple_gate, m_w_ple_proj, m_g_final, v_w_in, v_g_attn, v_g_q_lora, v_g_kv_lora, v_w_uq, v_w_ukv, v_w_o, v_g_ffn, v_w_ffn_gate, v_w_ffn_up, v_conv_w, v_conv_b, v_w_ffn_down, v_g_ple, v_w_ple_gate, v_w_ple_proj, v_g_final):
    given = dict(locals())
    wts = {n: given[n] for n in _WEIGHTS}
    mom = {n: given["m_" + n] for n in _WEIGHTS}
    var = {n: given["v_" + n] for n in _WEIGHTS}
    me = (4 * lax.axis_index("x") + 2 * lax.axis_index("y") + lax.axis_index("c")).astype(jnp.int32)
    ops = _ShardedWeights(wts, mom, var)

    conv_full = _all_gather_seq(conv_w, name="ag_conv_w")[:, 0].transpose(1, 0, 2).reshape(CONV_WIDTH, D_FF)
    vec = {"g_attn": g_attn, "g_q_lora": g_q_lora, "g_kv_lora": g_kv_lora, "g_ffn": g_ffn, "g_ple": g_ple,
           "g_final": g_final[None, :], "conv_w": _ffn_vec_layout(conv_full), "conv_b": _ffn_vec_layout(conv_b)}

    loss_part, grad_x, gs = _local_step(x[0], p[0, 0], loss_target[0], vec, ops)
    loss = lax.psum(loss_part[0, 0], ("x", "y", "c"))

    small = _unpack_small(_all_reduce_small(ops.order.tie(_pack_small(gs)), name="ar_small"))
    conv_w_shards = _ffn_vec_shards(small["conv_w"])
    small_g = {
        "g_attn": small["g_attn"], "g_q_lora": small["g_q_lora"], "g_kv_lora": small["g_kv_lora"],
        "g_ffn": small["g_ffn"], "g_ple": small["g_ple"], "g_final": small["g_final"],
        "conv_b": _ffn_vec_shards(small["conv_b"]).reshape(1, D_FF),
        "conv_w": lax.dynamic_index_in_dim(conv_w_shards, me, axis=1, keepdims=False),
    }
    results = dict(ops.results)
    for n, g in small_g.items():
        shape = wts[n].shape
        outs = ops.order.run(_adam, wts[n].reshape(g.shape), g, mom[n].reshape(g.shape), var[n].reshape(g.shape),
                             name="adam_" + n)
        results[n] = tuple(a.reshape(shape) for a in outs)
    ops.update("w_in")
    results["w_in"] = ops.results["w_in"]

    return (loss, grad_x[None], *[results[n][0] for n in _WEIGHTS], *[results[n][1] for n in _WEIGHTS],
            *[results[n][2] for n in _WEIGHTS], *[results[n][3] for n in _WEIGHTS])


class _ShardedWeights:
    def __init__(self, wts, mom, var):
        self.wts, self.mom, self.var = wts, mom, var
        self.order = _Order()
        self.full, self.stage, self.results = {}, {}, {}

    def start_gather(self, name, after=None, halves=False):
        payload = _shard_payload(name, self.wts[name][0])
        if after is not None:
            payload = lax.optimization_barrier((payload, after))[0]
        if not halves:
            self.full[name] = _full_from_gathered(name, _all_gather_seq(payload, name="ag_" + name))
            return
        half = payload.shape[1] // 2
        for part in range(2):
            piece = payload[:, part * half:(part + 1) * half]
            self.full[f"{name}/{part}"] = _full_from_gathered(name, _all_gather_seq(piece, name=f"ag_{name}_{part}"))

    def weight(self, name):
        return self.full[name]

    def grad(self, name, a, b):
        core = lax.axis_index("c")
        if name == "w_in":
            chunks = _grad_chunks(name, self.order.run(_matmul, a, b, ta=True, name="mm_d_" + name,
                                                        out_dtype=MXU_DTYPE))
            self.stage[name] = (chunks, None, _exchange_sibling(chunks, name="rs_sib_" + name))
            return
        theirs = self.order.run(_grad_matmul_half, a, b, 1 - core, name="mm_d_" + name + "_sib")
        self.stage[name] = (a, b, _exchange_sibling(theirs, name="rs_sib_" + name))

    def reduce_add(self, name):
        a, b, from_sibling = self.stage[name]
        if b is None:
            per_chip = self.order.run(_add_sibling, a, from_sibling, name="rs_add_" + name)
        else:
            per_chip = self.order.run(_grad_matmul_half, a, b, lax.axis_index("c"), name="mm_d_" + name + "_own",
                                      add=from_sibling)
        self.stage[name] = _exchange_chips(per_chip, name="rs_chip_" + name)

    def update(self, name):
        parts = self.stage[name]
        if name == "w_uq":
            parts = _unpad_rows(name, parts)
        rows = lambda a: _rows_view(name, a[0])
        outs = self.order.run(_adam, rows(self.wts[name]), parts, rows(self.mom[name]), rows(self.var[name]),
                              name="adam_" + name, parts=4)
        self.results[name] = tuple(_rows_view(name, a)[None] for a in outs)
```

```python
import functools

import numpy as np

import jax
import jax.numpy as jnp
from jax import lax
from jax.experimental import pallas as pl
from jax.experimental.pallas import tpu as pltpu
from jax.experimental.pallas import tpu_sc as plsc

D_MODEL = 4096
CHUNK = 64
PLE_DIM = 256
RET_HEADS = 8
RET_HEAD_DIM = 256
RET_WIDTH = 2048
MLA_HEADS = 16
MLA_NOPE = 128
MLA_ROPE = 64
MLA_V = 128
Q_LORA = 1024
KV_LORA = 512
D_FF = 11008
CONV_WIDTH = 3
ROPE_BASE = 10000.0
EPS = 1e-6
IN_WIDTH = 9792
ADAM_LR, ADAM_B1, ADAM_B2, ADAM_EPS, ADAM_WD, ADAM_STEP = 0.001, 0.9, 0.999, 1e-08, 0.01, 10

IN_WIDTH_PAD = 10240
D_FF_PAD = 11264
MLA_QK_PAD = 256
Q_WIDTH_PAD = MLA_HEADS * MLA_QK_PAD

N_DEV = 8
MXU_DTYPE = jnp.bfloat16
ATTN_BLOCK = 512
HEADS_PER_STEP = 4
FWD_HEADS_PER_STEP = 8
VMEM_LIMIT = 56 * 1024 * 1024

F32 = jnp.float32


def _tile(n, want, align=128):
    if n <= want:
        return n
    t = (want // align) * align
    while t >= align:
        if n % t == 0:
            return t
        t -= align
    return n


def _params(sem):
    return pltpu.CompilerParams(dimension_semantics=sem, vmem_limit_bytes=VMEM_LIMIT)


def _sigmoid(x):
    return 1.0 / (1.0 + jnp.exp(-x))


def _matmul(a, b, *, name, ta=False, tb=False, out_dtype=F32, add=None, tm=1024, tn=1024, tk=4096):
    m, k = (a.shape[1], a.shape[0]) if ta else a.shape
    k2, n = (b.shape[1], b.shape[0]) if tb else b.shape
    assert k == k2, (a.shape, b.shape, ta, tb)
    tm, tn, tk = _tile(m, tm), _tile(n, tn), _tile(k, tk)
    nk = k // tk
    dims = (((0 if ta else 1,), (1 if tb else 0,)), ((), ()))

    def body(*refs):
        a_ref, b_ref, o_ref = refs[0], refs[1], refs[3 if add is not None else 2]
        c_ref = refs[2] if add is not None else None
        part = lax.dot_general(a_ref[...].astype(MXU_DTYPE), b_ref[...].astype(MXU_DTYPE), dims,
                               preferred_element_type=F32)

        def finish(r):
            if c_ref is not None:
                r = r + c_ref[...].astype(F32)
            o_ref[...] = r.astype(out_dtype)

        if nk == 1:
            finish(part)
            return
        acc_ref = refs[-1]
        kk = pl.program_id(2)

        @pl.when(kk == 0)
        def _():
            acc_ref[...] = part

        @pl.when((kk > 0) & (kk < nk - 1))
        def _():
            acc_ref[...] += part

        @pl.when(kk == nk - 1)
        def _():
            finish(acc_ref[...] + part)

    a_spec = pl.BlockSpec((tk, tm), lambda i, j, kk: (kk, i)) if ta else pl.BlockSpec((tm, tk), lambda i, j, kk: (i, kk))
    b_spec = pl.BlockSpec((tn, tk), lambda i, j, kk: (j, kk)) if tb else pl.BlockSpec((tk, tn), lambda i, j, kk: (kk, j))
    in_specs = [a_spec, b_spec]
    args = [a, b]
    if add is not None:
        in_specs.append(pl.BlockSpec((tm, tn), lambda i, j, kk: (i, j)))
        args.append(add)
    return pl.pallas_call(
        body, name=name, grid=(m // tm, n // tn, nk),
        in_specs=in_specs, out_specs=pl.BlockSpec((tm, tn), lambda i, j, kk: (i, j)),
        out_shape=jax.ShapeDtypeStruct((m, n), out_dtype),
        scratch_shapes=[] if nk == 1 else [pltpu.VMEM((tm, tn), F32)],
        compiler_params=_params(("parallel", "parallel", "arbitrary")),
    )(*args)


def _grad_matmul_half(a, b, core_class, *, name, add=None):
    t, m = a.shape
    r = m // N_DEV
    cdim = b.shape[1]
    tn = _tile(cdim, 512 if r > 1024 else 1024)
    cls = jnp.asarray(core_class, jnp.int32).reshape(1)

    def body(cls_ref, a_ref, b_ref, *rest):
        part = lax.dot_general(a_ref[...].astype(MXU_DTYPE), b_ref[...].astype(MXU_DTYPE), _TN,
                               preferred_element_type=F32)
        if add is not None:
            part = part + rest[0][...].astype(F32)
        rest[-1][...] = part.astype(rest[-1].dtype)

    slot = pl.BlockSpec((None, r, tn), lambda q, j, cls_ref: (q, 0, j))
    in_specs = [pl.BlockSpec((t, r), lambda q, j, cls_ref: (0, 2 * q + cls_ref[0])),
                pl.BlockSpec((t, tn), lambda q, j, cls_ref: (0, j))] + ([slot] if add is not None else [])
    args = [a, b] + ([add] if add is not None else [])
    return pl.pallas_call(
        body, name=name,
        grid_spec=pltpu.PrefetchScalarGridSpec(
            num_scalar_prefetch=1, grid=(N_DEV // 2, cdim // tn), in_specs=in_specs, out_specs=slot),
        out_shape=jax.ShapeDtypeStruct((N_DEV // 2, r, cdim), MXU_DTYPE),
        compiler_params=_params(("parallel", "parallel")),
    )(cls, *args)


def _rms_fwd(x, g, *, name):
    t, d = x.shape
    tr = _tile(t, 256, 8)

    def body(x_ref, g_ref, o_ref):
        xf = x_ref[...]
        r = lax.rsqrt(jnp.mean(xf * xf, axis=-1, keepdims=True) + EPS)
        o_ref[...] = (xf * r * g_ref[...]).astype(o_ref.dtype)

    return pl.pallas_call(
        body, name=name, grid=(t // tr,),
        in_specs=[pl.BlockSpec((tr, d), lambda i: (i, 0)), pl.BlockSpec((1, d), lambda i: (0, 0))],
        out_specs=pl.BlockSpec((tr, d), lambda i: (i, 0)),
        out_shape=jax.ShapeDtypeStruct((t, d), MXU_DTYPE),
        compiler_params=_params(("parallel",)),
    )(x, g)


def _rms_bwd(x, dhn, g, res, *, name, low_copy=False):
    t, d = x.shape
    tr = _tile(t, 256, 8)

    def body(*refs):
        if res is None:
            x_ref, dh_ref, g_ref = refs[:3]
            outs = refs[3:]
            res_ref = None
        else:
            x_ref, dh_ref, g_ref, res_ref = refs[:4]
            outs = refs[4:]
        dx_ref, dg_ref = outs[0], outs[-1]
        xf = x_ref[...]
        dh = dh_ref[...].astype(F32)
        r = lax.rsqrt(jnp.mean(xf * xf, axis=-1, keepdims=True) + EPS)
        dyg = dh * g_ref[...]
        dx = r * dyg - xf * (r * r * r) * jnp.mean(dyg * xf, axis=-1, keepdims=True)
        if res_ref is not None:
            dx = dx + res_ref[...]
        dx_ref[...] = dx
        if low_copy:
            outs[1][...] = dx.astype(outs[1].dtype)
        part = jnp.sum(dh * xf * r, axis=0, keepdims=True)

        @pl.when(pl.program_id(0) == 0)
        def _():
            dg_ref[...] = part

        @pl.when(pl.program_id(0) > 0)
        def _():
            dg_ref[...] += part

    row = pl.BlockSpec((tr, d), lambda i: (i, 0))
    vec = pl.BlockSpec((1, d), lambda i: (0, 0))
    in_specs = [row, row, vec] + ([] if res is None else [row])
    args = [x, dhn, g] + ([] if res is None else [res])
    out_specs = [row] + ([row] if low_copy else []) + [vec]
    out_shape = [jax.ShapeDtypeStruct((t, d), F32)] + ([jax.ShapeDtypeStruct((t, d), MXU_DTYPE)] if low_copy else []) \
        + [jax.ShapeDtypeStruct((1, d), F32)]
    return pl.pallas_call(
        body, name=name, grid=(t // tr,), in_specs=in_specs, out_specs=out_specs, out_shape=out_shape,
        compiler_params=_params(("arbitrary",)),
    )(*args)


def _rope_tables(t):
    pos = jnp.arange(t, dtype=F32)[:, None]
    inv_r = 1.0 / (ROPE_BASE ** (jnp.arange(0, RET_HEAD_DIM, 2, dtype=F32) / RET_HEAD_DIM))
    ang_r = pos * inv_r[None, :]
    inv_m = 1.0 / (ROPE_BASE ** (jnp.arange(0, MLA_ROPE, 2, dtype=F32) / MLA_ROPE))
    ang_m = pos * inv_m[None, :]
    cm, sm = jnp.cos(ang_m), jnp.sin(ang_m)
    z = jnp.zeros_like(cm)
    cos_m = jnp.concatenate([cm, cm, z, z], axis=1)
    sin_m = jnp.concatenate([-sm, sm, z, z], axis=1)
    return jnp.cos(ang_r), jnp.sin(ang_r), cos_m, sin_m


def _rope256(x, c, s, inverse=False):
    x1, x2 = x[:, :128], x[:, 128:]
    if inverse:
        s = -s
    return jnp.concatenate([x1 * c - x2 * s, x2 * c + x1 * s], axis=1)


def _rope64(x, cos_m, sin_m, inverse=False):
    lane = lax.broadcasted_iota(jnp.int32, x.shape, 1)
    partner = jnp.where(lane < 32, pltpu.roll(x, 96, 1), pltpu.roll(x, 32, 1))
    s = -sin_m if inverse else sin_m
    return x * cos_m + partner * s


def _ret_prep(proj, cos_r, sin_r):
    t = proj.shape[0]
    tr = _tile(t, 256, 8)

    def body(q_ref, k_ref, v_ref, c_ref, s_ref, qo_ref, ko_ref, vo_ref):
        c, s = c_ref[...], s_ref[...]
        for h in range(RET_HEADS):
            cols = slice(h * RET_HEAD_DIM, (h + 1) * RET_HEAD_DIM)
            qo_ref[:, cols] = _rope256(q_ref[:, cols], c, s).astype(qo_ref.dtype)
            ko_ref[:, cols] = (_rope256(k_ref[:, cols], c, s) * (RET_HEAD_DIM ** -0.5)).astype(ko_ref.dtype)
        vo_ref[...] = v_ref[...].astype(vo_ref.dtype)

    group = lambda off: pl.BlockSpec((tr, RET_WIDTH), lambda i: (i, off))
    tab = pl.BlockSpec((tr, 128), lambda i: (i, 0))
    out = jax.ShapeDtypeStruct((t, RET_WIDTH), MXU_DTYPE)
    return pl.pallas_call(
        body, name="ret_prep", grid=(t // tr,),
        in_specs=[group(0), group(1), group(2), tab, tab],
        out_specs=[group(0), group(0), group(0)], out_shape=[out, out, out],
        compiler_params=_params(("parallel",)),
    )(proj, proj, proj, cos_r, sin_r)


def _ret_log_gamma():
    return jnp.asarray(np.log1p(-np.exp2(-5.0 - np.arange(RET_HEADS, dtype=np.float64))), dtype=F32)


def _decay_full(lg, i, j, blk):
    r = lax.broadcasted_iota(jnp.int32, (blk, 1), 0).astype(F32)
    c = lax.broadcasted_iota(jnp.int32, (1, blk), 1).astype(F32)
    off = ((i - j) * blk).astype(F32)
    return jnp.exp(lg * r), jnp.exp(lg * (off - c))


def _decay_diag(lg, blk):
    r = lax.broadcasted_iota(jnp.int32, (blk, blk), 0)
    c = lax.broadcasted_iota(jnp.int32, (blk, blk), 1)
    ok = (c // CHUNK) <= (r // CHUNK)
    return jnp.where(ok, jnp.exp(lg * jnp.abs(r - c).astype(F32)), 0.0)


_NT = (((1,), (1,)), ((), ()))
_TN = (((0,), (0,)), ((), ()))
_NN = (((1,), (0,)), ((), ()))


def _causal_pairs(nb, query_major):
    if query_major:
        pairs = [(i, j) for i in range(nb) for j in range(i + 1)]
    else:
        pairs = [(i, j) for j in range(nb) for i in range(j, nb)]
    arr = np.asarray(pairs, dtype=np.int32)
    return jnp.asarray(arr[:, 0]), jnp.asarray(arr[:, 1])


def _ret_fwd(q, k, v, proj, lg):
    t = q.shape[0]
    blk = _tile(t, ATTN_BLOCK)
    nb = t // blk
    hps, d = FWD_HEADS_PER_STEP, RET_HEAD_DIM
    gate_off = 3 * RET_WIDTH // (hps * d)

    def body(ii_ref, jj_ref, lg_ref, q_ref, k_ref, v_ref, g_ref, raw_ref, ro_ref, acc_ref):
        hg, pair = pl.program_id(0), pl.program_id(1)
        i, j = ii_ref[pair], jj_ref[pair]

        @pl.when(j == 0)
        def _():
            acc_ref[...] = jnp.zeros_like(acc_ref)

        def step(diag):
            for h in range(hps):
                cols = slice(h * d, (h + 1) * d)
                lgh = lg_ref[hg * hps + h]
                s = lax.dot_general(q_ref[:, cols], k_ref[:, cols], _NT, preferred_element_type=F32)
                if diag:
                    w = s * _decay_diag(lgh, blk)
                else:
                    a, b = _decay_full(lgh, i, j, blk)
                    w = s * a * b
                acc_ref[h] += lax.dot_general(w.astype(MXU_DTYPE), v_ref[:, cols], _NN, preferred_element_type=F32)

        @pl.when(j < i)
        def _():
            step(False)

        @pl.when(j == i)
        def _():
            step(True)
            for h in range(hps):
                cols = slice(h * d, (h + 1) * d)
                o = acc_ref[h]
                raw_ref[:, cols] = o
                mu = jnp.mean(o, axis=-1, keepdims=True)
                var = jnp.mean(jnp.square(o - mu), axis=-1, keepdims=True)
                hn = (o - mu) * lax.rsqrt(var + EPS)
                g = g_ref[:, cols]
                ro_ref[:, cols] = (g * _sigmoid(g) * hn).astype(ro_ref.dtype)

    qs = pl.BlockSpec((blk, hps * d), lambda h, p, ii, jj: (ii[p], h))
    ks = pl.BlockSpec((blk, hps * d), lambda h, p, ii, jj: (jj[p], h))
    gs = pl.BlockSpec((blk, hps * d), lambda h, p, ii, jj: (ii[p], h + gate_off))
    ii, jj = _causal_pairs(nb, query_major=True)
    return pl.pallas_call(
        body, name="ret_fwd",
        grid_spec=pltpu.PrefetchScalarGridSpec(
            num_scalar_prefetch=2, grid=(RET_HEADS // hps, ii.shape[0]),
            in_specs=[pl.BlockSpec(memory_space=pltpu.SMEM), qs, ks, ks, gs], out_specs=[qs, qs],
            scratch_shapes=[pltpu.VMEM((hps, blk, d), F32)]),
        out_shape=[jax.ShapeDtypeStruct((t, RET_WIDTH), F32), jax.ShapeDtypeStruct((t, RET_WIDTH), MXU_DTYPE)],
        compiler_params=_params(("parallel", "arbitrary")),
    )(ii, jj, lg, q, k, v, proj)


def _ret_gate_bwd(raw, proj, dattn):
    t = raw.shape[0]
    tr = _tile(t, 256, 8)

    def body(o_ref, g_ref, d_ref, do_ref, dg_ref):
        for h in range(RET_HEADS):
            cols = slice(h * RET_HEAD_DIM, (h + 1) * RET_HEAD_DIM)
            o, g, d = o_ref[:, cols], g_ref[:, cols], d_ref[:, cols]
            mu = jnp.mean(o, axis=-1, keepdims=True)
            rstd = lax.rsqrt(jnp.mean(jnp.square(o - mu), axis=-1, keepdims=True) + EPS)
            hn = (o - mu) * rstd
            sg = _sigmoid(g)
            dg_ref[:, cols] = (d * hn * (sg * (1.0 + g * (1.0 - sg)))).astype(dg_ref.dtype)
            dhn = d * (g * sg)
            do = rstd * (dhn - jnp.mean(dhn, axis=-1, keepdims=True)
                         - hn * jnp.mean(dhn * hn, axis=-1, keepdims=True))
            do_ref[:, cols] = do.astype(do_ref.dtype)

    group = lambda off: pl.BlockSpec((tr, RET_WIDTH), lambda i: (i, off))
    out = jax.ShapeDtypeStruct((t, RET_WIDTH), MXU_DTYPE)
    return pl.pallas_call(
        body, name="ret_gate_bwd", grid=(t // tr,),
        in_specs=[group(0), group(3), group(0)], out_specs=[group(0), group(0)], out_shape=[out, out],
        compiler_params=_params(("parallel",)),
    )(raw, proj, dattn)


def _ret_bwd(q, k, v, do, lg):
    t = q.shape[0]
    blk = _tile(t, ATTN_BLOCK)
    nb = t // blk

    hps, d = HEADS_PER_STEP, RET_HEAD_DIM

    def body(ii_ref, jj_ref, lg_ref, q_ref, k_ref, v_ref, do_ref, dq_ref, dk_ref, dv_ref, dk_acc, dv_acc):
        hg, pair = pl.program_id(0), pl.program_id(1)
        i, j = ii_ref[pair], jj_ref[pair]

        @pl.when(pair == 0)
        def _():
            dq_ref[...] = jnp.zeros_like(dq_ref)

        @pl.when(i == j)
        def _():
            dk_acc[...] = jnp.zeros_like(dk_acc)
            dv_acc[...] = jnp.zeros_like(dv_acc)

        def step(diag):
            rows = pl.ds(pl.multiple_of(i * blk, blk), blk)
            for h in range(hps):
                cols = slice(h * d, (h + 1) * d)
                lgh = lg_ref[hg * hps + h]
                if diag:
                    decay = _decay_diag(lgh, blk)
                else:
                    a, b = _decay_full(lgh, i, j, blk)
                    decay = a * b
                qb, kb, vb, dob = q_ref[:, cols], k_ref[:, cols], v_ref[:, cols], do_ref[:, cols]
                s = lax.dot_general(qb, kb, _NT, preferred_element_type=F32)
                w = (s * decay).astype(MXU_DTYPE)
                dv_acc[h] += lax.dot_general(w, dob, _TN, preferred_element_type=F32)
                dw = lax.dot_general(dob, vb, _NT, preferred_element_type=F32)
                ds = (dw * decay).astype(MXU_DTYPE)
                dq_ref[rows, cols] += lax.dot_general(ds, kb, _NN, preferred_element_type=F32)
                dk_acc[h] += lax.dot_general(ds, qb, _TN, preferred_element_type=F32)

        @pl.when(i > j)
        def _():
            step(False)

        @pl.when(i == j)
        def _():
            step(True)

        @pl.when(i == nb - 1)
        def _():
            for h in range(hps):
                cols = slice(h * d, (h + 1) * d)
                dk_ref[:, cols] = dk_acc[h]
                dv_ref[:, cols] = dv_acc[h].astype(dv_ref.dtype)

    qs = pl.BlockSpec((blk, hps * d), lambda h, p, ii, jj: (ii[p], h))
    ks = pl.BlockSpec((blk, hps * d), lambda h, p, ii, jj: (jj[p], h))
    ii, jj = _causal_pairs(nb, query_major=False)
    return pl.pallas_call(
        body, name="ret_bwd",
        grid_spec=pltpu.PrefetchScalarGridSpec(
            num_scalar_prefetch=2, grid=(RET_HEADS // hps, ii.shape[0]),
            in_specs=[pl.BlockSpec(memory_space=pltpu.SMEM), qs, ks, ks, qs],
            out_specs=[pl.BlockSpec((t, hps * d), lambda h, p, ii, jj: (0, h), pipeline_mode=pl.Buffered(1)), ks, ks],
            scratch_shapes=[pltpu.VMEM((hps, blk, d), F32), pltpu.VMEM((hps, blk, d), F32)]),
        out_shape=[jax.ShapeDtypeStruct((t, RET_WIDTH), F32), jax.ShapeDtypeStruct((t, RET_WIDTH), F32),
                   jax.ShapeDtypeStruct((t, RET_WIDTH), MXU_DTYPE)],
        compiler_params=_params(("parallel", "arbitrary")),
    )(ii, jj, lg, q, k, v, do)


def _ret_unrope(dq, dk, cos_r, sin_r):
    t = dq.shape[0]
    tr = _tile(t, 256, 8)

    def body(dq_ref, dk_ref, c_ref, s_ref, oq_ref, ok_ref):
        c, s = c_ref[...], s_ref[...]
        for h in range(RET_HEADS):
            cols = slice(h * RET_HEAD_DIM, (h + 1) * RET_HEAD_DIM)
            oq_ref[:, cols] = _rope256(dq_ref[:, cols], c, s, inverse=True).astype(oq_ref.dtype)
            ok_ref[:, cols] = (_rope256(dk_ref[:, cols], c, s, inverse=True)
                               * (RET_HEAD_DIM ** -0.5)).astype(ok_ref.dtype)

    rows = pl.BlockSpec((tr, RET_WIDTH), lambda i: (i, 0))
    tab = pl.BlockSpec((tr, 128), lambda i: (i, 0))
    out = jax.ShapeDtypeStruct((t, RET_WIDTH), MXU_DTYPE)
    return pl.pallas_call(
        body, name="ret_unrope", grid=(t // tr,),
        in_specs=[rows, rows, tab, tab], out_specs=[rows, rows], out_shape=[out, out],
        compiler_params=_params(("parallel",)),
    )(dq, dk, cos_r, sin_r)


def _mla_prep(cq, ckv, kr, g_q, g_kv, cos_m, sin_m):
    t = cq.shape[0]
    tr = _tile(t, 512, 8)

    def body(cq_ref, ckv_ref, kr_ref, gq_ref, gkv_ref, c_ref, s_ref, cqn_ref, kvn_ref, kro_ref):
        for x_ref, g_ref, o_ref in ((cq_ref, gq_ref, cqn_ref), (ckv_ref, gkv_ref, kvn_ref)):
            xf = x_ref[...]
            r = lax.rsqrt(jnp.mean(xf * xf, axis=-1, keepdims=True) + EPS)
            o_ref[...] = (xf * r * g_ref[...]).astype(o_ref.dtype)
        kro_ref[...] = _rope64(kr_ref[...], c_ref[...], s_ref[...]).astype(kro_ref.dtype)

    row = lambda w: pl.BlockSpec((tr, w), lambda i: (i, 0))
    vec = lambda w: pl.BlockSpec((1, w), lambda i: (0, 0))
    return pl.pallas_call(
        body, name="mla_prep", grid=(t // tr,),
        in_specs=[row(Q_LORA), row(KV_LORA), row(128), vec(Q_LORA), vec(KV_LORA), row(128), row(128)],
        out_specs=[row(Q_LORA), row(KV_LORA), row(128)],
        out_shape=[jax.ShapeDtypeStruct((t, Q_LORA), MXU_DTYPE), jax.ShapeDtypeStruct((t, KV_LORA), MXU_DTYPE),
                   jax.ShapeDtypeStruct((t, 128), MXU_DTYPE)],
        compiler_params=_params(("parallel",)),
    )(cq, ckv, kr, g_q, g_kv, cos_m, sin_m)


def _mla_q_rope(q_lin, cos_m, sin_m, *, inverse, name):
    t = q_lin.shape[0]
    tr = _tile(t, 256, 8)

    def body(q_ref, c_ref, s_ref, o_ref):
        c, s = c_ref[...], s_ref[...]
        for h in range(MLA_HEADS):
            lo = h * MLA_QK_PAD
            o_ref[:, lo:lo + MLA_NOPE] = q_ref[:, lo:lo + MLA_NOPE].astype(o_ref.dtype)
            roped = _rope64(q_ref[:, lo + MLA_NOPE:lo + MLA_QK_PAD].astype(F32), c, s, inverse=inverse)
            o_ref[:, lo + MLA_NOPE:lo + MLA_QK_PAD] = roped.astype(o_ref.dtype)

    rows = pl.BlockSpec((tr, Q_WIDTH_PAD), lambda i: (i, 0))
    tab = pl.BlockSpec((tr, 128), lambda i: (i, 0))
    return pl.pallas_call(
        body, name=name, grid=(t // tr,),
        in_specs=[rows, tab, tab], out_specs=rows, out_shape=jax.ShapeDtypeStruct((t, Q_WIDTH_PAD), MXU_DTYPE),
        compiler_params=_params(("parallel",)),
    )(q_lin, cos_m, sin_m)


_MLA_SCALE = (MLA_NOPE + MLA_ROPE) ** -0.5
_LOG2_E = 1.4426950408889634
_MLA_SCALE_LOG2 = _MLA_SCALE * _LOG2_E
_NEG = -1e30


def _mla_mask(blk):
    r = lax.broadcasted_iota(jnp.int32, (blk, blk), 0)
    c = lax.broadcasted_iota(jnp.int32, (blk, blk), 1)
    return (c // CHUNK) <= (r // CHUNK)


def _mla_fwd(q, kv, kr):
    t = q.shape[0]
    blk = _tile(t, ATTN_BLOCK)
    nb = t // blk

    hps, dq_, dv_ = MLA_HEADS, MLA_QK_PAD, MLA_V

    def body(ii_ref, jj_ref, q_ref, kv_ref, kr_ref, o_ref, lse_ref, m_ref, acc_ref):
        pair = pl.program_id(1)
        i, j = ii_ref[pair], jj_ref[pair]

        @pl.when(j == 0)
        def _():
            m_ref[...] = jnp.full_like(m_ref, _NEG)
            acc_ref[...] = jnp.zeros_like(acc_ref)

        def step(masked):
            krb = kr_ref[...]
            ones = jnp.ones((blk, dv_), MXU_DTYPE)
            for h in range(hps):
                kb = jnp.concatenate([kv_ref[:, h * dq_:h * dq_ + MLA_NOPE], krb], axis=1)
                vb = jnp.concatenate([kv_ref[:, h * dq_ + MLA_NOPE:(h + 1) * dq_], ones], axis=1)
                s = lax.dot_general(q_ref[:, h * dq_:(h + 1) * dq_], kb, _NT, preferred_element_type=F32)
                if masked:
                    s = jnp.where(_mla_mask(blk), s, _NEG)
                m_prev = m_ref[h]
                m_new = jnp.maximum(m_prev, jnp.max(s, axis=-1, keepdims=True))
                alpha = jnp.exp2((m_prev - m_new) * _MLA_SCALE_LOG2)
                p = jnp.exp2((s - jnp.tile(m_new, (1, blk // 128))) * _MLA_SCALE_LOG2)
                acc_ref[h] = jnp.tile(alpha, (1, 2)) * acc_ref[h] + lax.dot_general(
                    p.astype(MXU_DTYPE), vb, _NN, preferred_element_type=F32)
                m_ref[h] = m_new

        @pl.when(j < i)
        def _():
            step(False)

        @pl.when(j == i)
        def _():
            step(True)
            for h in range(hps):
                cols = slice(h * dv_, (h + 1) * dv_)
                acc = acc_ref[h]
                row_sum = acc[:, dv_:]
                o_ref[:, cols] = (acc[:, :dv_] / row_sum).astype(o_ref.dtype)
                lse_ref[:, cols] = m_ref[h] * _MLA_SCALE + jnp.log(row_sum)

    os_ = pl.BlockSpec((blk, hps * dv_), lambda h, p, ii, jj: (ii[p], h))
    ii, jj = _causal_pairs(nb, query_major=True)
    return pl.pallas_call(
        body, name="mla_fwd",
        grid_spec=pltpu.PrefetchScalarGridSpec(
            num_scalar_prefetch=2, grid=(MLA_HEADS // hps, ii.shape[0]),
            in_specs=[pl.BlockSpec((blk, hps * dq_), lambda h, p, ii, jj: (ii[p], h)),
                      pl.BlockSpec((blk, hps * dq_), lambda h, p, ii, jj: (jj[p], h)),
                      pl.BlockSpec((blk, 128), lambda h, p, ii, jj: (jj[p], 0))],
            out_specs=[os_, os_],
            scratch_shapes=[pltpu.VMEM((hps, blk, 128), F32), pltpu.VMEM((hps, blk, 2 * dv_), F32)]),
        out_shape=[jax.ShapeDtypeStruct((t, MLA_HEADS * MLA_V), MXU_DTYPE),
                   jax.ShapeDtypeStruct((t, MLA_HEADS * 128), F32)],
        compiler_params=_params(("parallel", "arbitrary")),
    )(ii, jj, q, kv, kr)


def _mla_bwd(q, kv, kr, o, lse, dattn):
    t = q.shape[0]
    blk = _tile(t, ATTN_BLOCK)
    nb = t // blk
    hps, dq_, dv_ = HEADS_PER_STEP, MLA_QK_PAD, MLA_V
    do_off = RET_WIDTH // (hps * dv_)

    def body(ii_ref, jj_ref, q_ref, kv_ref, kr_ref, o_ref, lse_ref, do_ref, dq_ref, dk_ref, dv_ref, dk_acc, dv_acc):
        pair = pl.program_id(1)
        i, j = ii_ref[pair], jj_ref[pair]

        @pl.when(pair == 0)
        def _():
            dq_ref[...] = jnp.zeros_like(dq_ref)

        @pl.when(i == j)
        def _():
            dk_acc[...] = jnp.zeros_like(dk_acc)
            dv_acc[...] = jnp.zeros_like(dv_acc)

        def step(masked):
            krb = kr_ref[...]
            rows = pl.ds(pl.multiple_of(i * blk, blk), blk)
            for h in range(hps):
                qcols, vcols = slice(h * dq_, (h + 1) * dq_), slice(h * dv_, (h + 1) * dv_)
                qb = q_ref[:, qcols]
                kb = jnp.concatenate([kv_ref[:, h * dq_:h * dq_ + MLA_NOPE], krb], axis=1)
                vb = kv_ref[:, h * dq_ + MLA_NOPE:(h + 1) * dq_]
                dof = do_ref[:, vcols]
                dob = dof.astype(MXU_DTYPE)
                s = lax.dot_general(qb, kb, _NT, preferred_element_type=F32)
                if masked:
                    s = jnp.where(_mla_mask(blk), s, _NEG)
                lse2 = lse_ref[:, vcols] * _LOG2_E
                p = jnp.exp2(s * _MLA_SCALE_LOG2 - jnp.tile(lse2, (1, blk // 128)))
                delta = jnp.sum(dof * o_ref[:, vcols].astype(F32), axis=-1, keepdims=True)
                dv_acc[h] += lax.dot_general(p.astype(MXU_DTYPE), dob, _TN, preferred_element_type=F32)
                dp = lax.dot_general(dob, vb, _NT, preferred_element_type=F32)
                ds = (p * (dp - delta) * _MLA_SCALE).astype(MXU_DTYPE)
                dq_ref[rows, qcols] += lax.dot_general(ds, kb, _NN, preferred_element_type=F32)
                dk_acc[h] += lax.dot_general(ds, qb, _TN, preferred_element_type=F32)

        @pl.when(i > j)
        def _():
            step(False)

        @pl.when(i == j)
        def _():
            step(True)

        @pl.when(i == nb - 1)
        def _():
            for h in range(hps):
                dk_ref[:, h * dq_:(h + 1) * dq_] = dk_acc[h]
                dv_ref[:, h * dv_:(h + 1) * dv_] = dv_acc[h].astype(dv_ref.dtype)

    qmap = lambda off: (lambda h, p, ii, jj: (ii[p], h + off))
    kmap = lambda h, p, ii, jj: (jj[p], h)
    ii, jj = _causal_pairs(nb, query_major=False)
    return pl.pallas_call(
        body, name="mla_bwd",
        grid_spec=pltpu.PrefetchScalarGridSpec(
            num_scalar_prefetch=2, grid=(MLA_HEADS // hps, ii.shape[0]),
            in_specs=[pl.BlockSpec((blk, hps * dq_), qmap(0)), pl.BlockSpec((blk, hps * dq_), kmap),
                      pl.BlockSpec((blk, 128), lambda h, p, ii, jj: (jj[p], 0)),
                      pl.BlockSpec((blk, hps * dv_), qmap(0)), pl.BlockSpec((blk, hps * dv_), qmap(0)),
                      pl.BlockSpec((blk, hps * dv_), qmap(do_off))],
            out_specs=[pl.BlockSpec((t, hps * dq_), lambda h, p, ii, jj: (0, h), pipeline_mode=pl.Buffered(1)),
                       pl.BlockSpec((blk, hps * dq_), kmap), pl.BlockSpec((blk, hps * dv_), kmap)],
            scratch_shapes=[pltpu.VMEM((hps, blk, dq_), F32), pltpu.VMEM((hps, blk, dv_), F32)]),
        out_shape=[jax.ShapeDtypeStruct((t, Q_WIDTH_PAD), F32), jax.ShapeDtypeStruct((t, Q_WIDTH_PAD), F32),
                   jax.ShapeDtypeStruct((t, MLA_HEADS * MLA_V), MXU_DTYPE)],
        compiler_params=_params(("parallel", "arbitrary")),
    )(ii, jj, q, kv, kr, o, lse, dattn)


def _mla_kv_grad(dk, dv, cos_m, sin_m):
    t = dk.shape[0]
    tr = _tile(t, 256, 8)

    def body(dk_ref, dv_ref, c_ref, s_ref, dkv_ref, dkr_ref):
        acc = jnp.zeros((tr, 128), F32)
        for h in range(MLA_HEADS):
            dkv_ref[:, h * 256:h * 256 + 128] = dk_ref[:, h * 256:h * 256 + 128].astype(dkv_ref.dtype)
            dkv_ref[:, h * 256 + 128:h * 256 + 256] = dv_ref[:, h * 128:(h + 1) * 128].astype(dkv_ref.dtype)
            acc = acc + dk_ref[:, h * 256 + 128:h * 256 + 256]
        dkr_ref[...] = _rope64(acc, c_ref[...], s_ref[...], inverse=True).astype(dkr_ref.dtype)

    row = lambda w: pl.BlockSpec((tr, w), lambda i: (i, 0))
    return pl.pallas_call(
        body, name="mla_kv_grad", grid=(t // tr,),
        in_specs=[row(Q_WIDTH_PAD), row(MLA_HEADS * MLA_V), row(128), row(128)],
        out_specs=[row(Q_WIDTH_PAD), row(128)],
        out_shape=[jax.ShapeDtypeStruct((t, Q_WIDTH_PAD), MXU_DTYPE), jax.ShapeDtypeStruct((t, 128), MXU_DTYPE)],
        compiler_params=_params(("parallel",)),
    )(dk, dv, cos_m, sin_m)


_FFN_COLS = 256
_FFN_ROWS = 256


def _shift_down(cur, prev8, n):
    out = pltpu.roll(cur, n, 0)
    head = out[:8]
    row = lax.broadcasted_iota(jnp.int32, head.shape, 0)
    for r in range(n):
        head = jnp.where(row == r, prev8[8 - n + r:8 - n + r + 1, :], head)
    return jnp.concatenate([head, out[8:]], axis=0)


def _shift_up(cur, next8, n):
    rows = cur.shape[0]
    out = pltpu.roll(cur, rows - n, 0)
    tail = out[rows - 8:]
    row = lax.broadcasted_iota(jnp.int32, tail.shape, 0)
    for r in range(n):
        tail = jnp.where(row == 8 - n + r, next8[r:r + 1, :], tail)
    return jnp.concatenate([out[:rows - 8], tail], axis=0)


def _conv_pre(g_ref, cw_ref, cb_ref, c, rc):
    r0 = pl.multiple_of(c * rc, rc)
    cur = g_ref[pl.ds(r0, rc), :].astype(F32)
    prev16 = g_ref[pl.ds(pl.multiple_of(jnp.maximum(r0 - 16, 0), 16), 16), :].astype(F32)
    prev8 = jnp.where(c > 0, prev16[8:], 0.0)
    s1, s2 = _shift_down(cur, prev8, 1), _shift_down(cur, prev8, 2)
    a = cb_ref[...] + cw_ref[2:3, :] * cur + cw_ref[1:2, :] * s1 + cw_ref[0:1, :] * s2
    return r0, cur, s1, s2, a


def _ffn_up_act(hn, w_up_t, gpre, cw, cb):
    t, k = hn.shape
    f = w_up_t.shape[0]
    tm, tn = _tile(t, 1024), _tile(f, 512)
    halo = 16

    def body(a_ref, b_ref, g_ref, h_ref, cw_ref, cb_ref, u_ref, act_ref):
        u = lax.dot_general(a_ref[...].astype(MXU_DTYPE), b_ref[...].astype(MXU_DTYPE), _NT,
                            preferred_element_type=F32)
        g = g_ref[...].astype(F32)
        prev8 = jnp.where(pl.program_id(0) > 0, h_ref[...].astype(F32)[halo - 8:], 0.0)
        s1, s2 = _shift_down(g, prev8, 1), _shift_down(g, prev8, 2)
        a = cb_ref[...] + cw_ref[2:3, :] * g + cw_ref[1:2, :] * s1 + cw_ref[0:1, :] * s2
        u_ref[...] = u.astype(u_ref.dtype)
        act_ref[...] = (a * _sigmoid(a) * u).astype(act_ref.dtype)

    tile = pl.BlockSpec((tm, tn), lambda i, j: (i, j))
    low = jax.ShapeDtypeStruct((t, f), MXU_DTYPE)
    return pl.pallas_call(
        body, name="mm_up_act", grid=(t // tm, f // tn),
        in_specs=[pl.BlockSpec((tm, k), lambda i, j: (i, 0)), pl.BlockSpec((tn, k), lambda i, j: (j, 0)), tile,
                  pl.BlockSpec((halo, tn), lambda i, j: (jnp.maximum(i * (tm // halo) - 1, 0), j)),
                  pl.BlockSpec((CONV_WIDTH, tn), lambda i, j: (0, j)), pl.BlockSpec((1, tn), lambda i, j: (0, j))],
        out_specs=[tile, tile], out_shape=[low, low],
        compiler_params=_params(("parallel", "parallel")),
    )(hn, w_up_t, gpre, gpre, cw, cb)


def _ffn_act_bwd(gpre, u, dact, cw, cb):
    t, f = gpre.shape
    tc = _tile(f, _FFN_COLS)
    rc = _tile(t, _FFN_ROWS, 8)
    nc = t // rc

    def body(g_ref, u_ref, d_ref, cw_ref, cb_ref, dg_ref, du_ref, dcw_ref, dcb_ref, da_ref):
        def chunk(c, carry):
            w0, w1, w2, b = carry
            r0, cur, s1, s2, a = _conv_pre(g_ref, cw_ref, cb_ref, c, rc)
            sg = _sigmoid(a)
            d = d_ref[pl.ds(r0, rc), :].astype(F32)
            du_ref[pl.ds(r0, rc), :] = (d * (a * sg)).astype(du_ref.dtype)
            da = d * u_ref[pl.ds(r0, rc), :].astype(F32) * (sg * (1.0 + a * (1.0 - sg)))
            da_ref[pl.ds(r0, rc), :] = da
            return (w0 + jnp.sum(da * s2, axis=0, keepdims=True), w1 + jnp.sum(da * s1, axis=0, keepdims=True),
                    w2 + jnp.sum(da * cur, axis=0, keepdims=True), b + jnp.sum(da, axis=0, keepdims=True))
        z = jnp.zeros((1, tc), F32)
        w0, w1, w2, b = lax.fori_loop(0, nc, chunk, (z, z, z, z))
        dcw_ref[0:1, :] = w0
        dcw_ref[1:2, :] = w1
        dcw_ref[2:3, :] = w2
        dcb_ref[...] = b

        def chunk2(c, carry):
            r0 = pl.multiple_of(c * rc, rc)
            cur = da_ref[pl.ds(r0, rc), :]
            nxt = da_ref[pl.ds(pl.multiple_of(jnp.minimum(r0 + rc, t - 8), 8), 8), :]
            nxt = jnp.where(c < nc - 1, nxt, 0.0)
            dg = cw_ref[2:3, :] * cur + cw_ref[1:2, :] * _shift_up(cur, nxt, 1) + cw_ref[0:1, :] * _shift_up(cur, nxt, 2)
            dg_ref[pl.ds(r0, rc), :] = dg.astype(dg_ref.dtype)
            return carry
        lax.fori_loop(0, nc, chunk2, 0)

    col = pl.BlockSpec((t, tc), lambda j: (0, j))
    w3 = pl.BlockSpec((CONV_WIDTH, tc), lambda j: (0, j))
    w1 = pl.BlockSpec((1, tc), lambda j: (0, j))
    low = jax.ShapeDtypeStruct((t, f), MXU_DTYPE)
    return pl.pallas_call(
        body, name="ffn_act_bwd", grid=(f // tc,),
        in_specs=[col, col, col, w3, w1], out_specs=[col, col, w3, w1],
        out_shape=[low, low, jax.ShapeDtypeStruct((CONV_WIDTH, f), F32), jax.ShapeDtypeStruct((1, f), F32)],
        scratch_shapes=[pltpu.VMEM((t, tc), F32)],
        compiler_params=_params(("parallel",)),
    )(gpre, u, dact, cw, cb)


def _head_fwd_bwd(h2, glin, pp, target, g_final):
    t, d = h2.shape
    tr = _tile(t, 128, 8)

    def body(h_ref, gl_ref, pp_ref, t_ref, g_ref, loss_ref, dh_ref, dgl_ref, dpp_ref, dg_ref):
        gate = _sigmoid(gl_ref[...])
        ppv = pp_ref[...]
        h3 = h_ref[...] + gate * ppv
        r = lax.rsqrt(jnp.mean(h3 * h3, axis=-1, keepdims=True) + EPS)
        yh = h3 * r
        g = g_ref[...]
        diff = yh * g - t_ref[...]
        lpart = 0.5 * jnp.sum(jnp.mean(diff * diff, axis=-1, keepdims=True), axis=0, keepdims=True)
        dy = diff * (1.0 / d)
        dyg = dy * g
        dh3 = r * dyg - h3 * (r * r * r) * jnp.mean(dyg * h3, axis=-1, keepdims=True)
        dh_ref[...] = dh3
        dgl_ref[...] = (dh3 * ppv * gate * (1.0 - gate)).astype(dgl_ref.dtype)
        dpp_ref[...] = (dh3 * gate).astype(dpp_ref.dtype)
        dgp = jnp.sum(dy * yh, axis=0, keepdims=True)

        @pl.when(pl.program_id(0) == 0)
        def _():
            loss_ref[...] = jnp.broadcast_to(lpart, loss_ref.shape)
            dg_ref[...] = dgp

        @pl.when(pl.program_id(0) > 0)
        def _():
            loss_ref[...] += jnp.broadcast_to(lpart, loss_ref.shape)
            dg_ref[...] += dgp

    row = pl.BlockSpec((tr, d), lambda i: (i, 0))
    vec = pl.BlockSpec((1, d), lambda i: (0, 0))
    low = jax.ShapeDtypeStruct((t, d), MXU_DTYPE)
    return pl.pallas_call(
        body, name="head_fwd_bwd", grid=(t // tr,),
        in_specs=[row, row, row, row, vec],
        out_specs=[pl.BlockSpec((8, 128), lambda i: (0, 0)), row, row, row, vec],
        out_shape=[jax.ShapeDtypeStruct((8, 128), F32), jax.ShapeDtypeStruct((t, d), F32), low, low,
                   jax.ShapeDtypeStruct((1, d), F32)],
        compiler_params=_params(("arbitrary",)),
    )(h2, glin, pp, target, g_final)


class _Order:
    def __init__(self):
        self.last = None

    def tie(self, x):
        return x if self.last is None else lax.optimization_barrier((x, self.last))[0]

    def run(self, fn, first, *args, **kwargs):
        out = fn(self.tie(first), *args, **kwargs)
        self.last = out[0] if isinstance(out, (tuple, list)) else out
        return out


def _local_step(x, p, target, vec, ops):
    t = x.shape[0]
    cos_r, sin_r, cos_m, sin_m = _rope_tables(t)
    lg = _ret_log_gamma()
    low = MXU_DTYPE
    run = ops.order.run
    w = ops.weight

    ops.start_gather("w_in", halves=True)
    hn1 = run(_rms_fwd, x, vec["g_attn"], name="rms1_fwd")
    for n in ("w_uq", "w_ukv", "w_o"):
        ops.start_gather(n, after=hn1)
    half = x.shape[1] // 2
    proj = run(_matmul, hn1[:, :half], w("w_in/0"), tb=True, name="mm_proj_a")
    proj = run(_matmul, hn1[:, half:], w("w_in/1"), tb=True, name="mm_proj_b", add=proj)
    ops.start_gather("w_ffn_gate", after=proj)
    rq, rk, rv = run(_ret_prep, proj, cos_r, sin_r)
    ops.start_gather("w_ffn_up", after=rq)
    c0 = 4 * RET_WIDTH
    cq = proj[:, c0:c0 + Q_LORA]
    ckv = proj[:, c0 + Q_LORA:c0 + Q_LORA + KV_LORA]
    kr_in = proj[:, c0 + Q_LORA + KV_LORA:c0 + Q_LORA + KV_LORA + 128]
    cqn, kvn, kr = run(_mla_prep, cq, ckv, kr_in, vec["g_q_lora"], vec["g_kv_lora"], cos_m, sin_m)
    ops.start_gather("w_ffn_down", after=cqn)
    q_lin = run(_matmul, cqn, w("w_uq"), tb=True, name="mm_q")
    q = run(_mla_q_rope, q_lin, cos_m, sin_m, inverse=False, name="mla_q_rope")
    kv = run(_matmul, kvn, w("w_ukv"), tb=True, name="mm_kv", out_dtype=low)
    mo, lse = run(_mla_fwd, q, kv, kr)
    ops.start_gather("w_ple_gate", after=mo)
    ops.start_gather("w_ple_proj", after=mo)
    ret_raw, ro = run(_ret_fwd, rq, rk, rv, proj, lg)
    attn = jnp.concatenate([ro, mo], axis=1)
    h1 = run(_matmul, attn, w("w_o"), name="mm_o", add=x)
    hn2 = run(_rms_fwd, h1, vec["g_ffn"], name="rms2_fwd")
    gpre = run(_matmul, hn2, w("w_ffn_gate"), tb=True, name="mm_gate", out_dtype=low)
    u, act = run(_ffn_up_act, hn2, w("w_ffn_up"), gpre, vec["conv_w"], vec["conv_b"])
    h2 = run(_matmul, act, w("w_ffn_down"), name="mm_down", add=h1)
    hn3 = run(_rms_fwd, h2, vec["g_ple"], name="rms3_fwd")
    glin = run(_matmul, hn3, w("w_ple_gate"), name="mm_ple_gate")
    p_low = p.astype(low)
    pp = run(_matmul, p_low, w("w_ple_proj"), tb=True, name="mm_ple_proj")
    loss_part, dh3, dglin, dpp, dg_final = run(_head_fwd_bwd, h2, glin, pp, target, vec["g_final"])

    ops.grad("w_ple_proj", dpp, p_low)
    ops.grad("w_ple_gate", hn3, dglin)
    dhn3 = run(_matmul, dglin, w("w_ple_gate"), tb=True, name="mm_dhn3", out_dtype=low)
    dh2, dh2_low, dg_ple = run(_rms_bwd, h2, dhn3, vec["g_ple"], dh3, name="rms3_bwd", low_copy=True)
    ops.reduce_add("w_ple_proj")
    ops.reduce_add("w_ple_gate")
    ops.grad("w_ffn_down", act, dh2_low)
    dact = run(_matmul, dh2_low, w("w_ffn_down"), tb=True, name="mm_dact", out_dtype=low)
    ops.reduce_add("w_ffn_down")
    dgpre, du, dconv_w, dconv_b = run(_ffn_act_bwd, gpre, u, dact, vec["conv_w"], vec["conv_b"])
    ops.update("w_ple_proj")
    ops.update("w_ple_gate")
    ops.grad("w_ffn_gate", dgpre, hn2)
    ops.grad("w_ffn_up", du, hn2)
    ops.reduce_add("w_ffn_gate")
    dhn2 = run(_matmul, dgpre, w("w_ffn_gate"), name="mm_dhn2_a")
    ops.reduce_add("w_ffn_up")
    dhn2 = run(_matmul, du, w("w_ffn_up"), name="mm_dhn2_b", add=dhn2, out_dtype=low)
    dh1, dh1_low, dg_ffn = run(_rms_bwd, h1, dhn2, vec["g_ffn"], dh2, name="rms2_bwd", low_copy=True)
    ops.update("w_ffn_down")
    ops.grad("w_o", attn, dh1_low)
    dattn = run(_matmul, dh1_low, w("w_o"), tb=True, name="mm_dattn")
    ops.reduce_add("w_o")

    dq_r, dk_full, dv = run(_mla_bwd, q, kv, kr, mo, lse, dattn)
    ops.update("w_ffn_gate")
    dq_lin = run(_mla_q_rope, dq_r, cos_m, sin_m, inverse=True, name="mla_q_unrope")
    dkv, dkr = run(_mla_kv_grad, dk_full, dv, cos_m, sin_m)
    ops.grad("w_uq", dq_lin, cqn)
    dcqn = run(_matmul, dq_lin, w("w_uq"), name="mm_dcqn")
    ops.grad("w_ukv", dkv, kvn)
    dkvn = run(_matmul, dkv, w("w_ukv"), name="mm_dkvn")
    dcq, dcq_low, dg_q = run(_rms_bwd, cq, dcqn, vec["g_q_lora"], None, name="rmsq_bwd", low_copy=True)
    dckv, dckv_low, dg_kv = run(_rms_bwd, ckv, dkvn, vec["g_kv_lora"], None, name="rmskv_bwd", low_copy=True)
    ops.reduce_add("w_uq")
    ops.reduce_add("w_ukv")

    do_ret, drg = run(_ret_gate_bwd, ret_raw, proj, dattn)
    dq_ret, dk_ret, drv = run(_ret_bwd, rq, rk, rv, do_ret, lg)
    drq, drk = run(_ret_unrope, dq_ret, dk_ret, cos_r, sin_r)

    pad = jnp.zeros((t, IN_WIDTH_PAD - IN_WIDTH - 64), low)
    dproj = jnp.concatenate([drq, drk, drv, drg, dcq_low, dckv_low, dkr, pad], axis=1)
    ops.grad("w_in", dproj, hn1)
    for n in ("w_ffn_up", "w_o", "w_uq", "w_ukv"):
        ops.update(n)
    ops.reduce_add("w_in")
    dhn1 = jnp.concatenate([run(_matmul, dproj, w(f"w_in/{part}"), name=f"mm_dhn1_{part}", out_dtype=low)
                            for part in range(2)], axis=1)
    grad_x, dg_attn = run(_rms_bwd, x, dhn1, vec["g_attn"], dh1, name="rms1_bwd")

    gs = {"g_attn": dg_attn, "g_q_lora": dg_q, "g_kv_lora": dg_kv, "g_ffn": dg_ffn, "conv_w": dconv_w,
          "conv_b": dconv_b, "g_ple": dg_ple, "g_final": dg_final}
    return loss_part, grad_x, gs


_COL_SHARDED = ("w_in", "w_uq", "w_ukv", "w_ffn_gate", "w_ffn_up", "w_ple_proj")
_FFN_SHARD = D_FF // N_DEV
_FFN_SHARD_PAD = D_FF_PAD // N_DEV
_HEADS_PER_SHARD = MLA_HEADS // N_DEV
_QK = MLA_NOPE + MLA_ROPE


def _pad_rows(name, a):
    lead = a.shape[:-2]
    if name == "w_uq":
        a = a.reshape(lead + (_HEADS_PER_SHARD, _QK, a.shape[-1]))
        a = jnp.pad(a, [(0, 0)] * len(lead) + [(0, 0), (0, MLA_QK_PAD - _QK), (0, 0)])
        return a.reshape(lead + (_HEADS_PER_SHARD * MLA_QK_PAD, a.shape[-1]))
    if name in ("w_ffn_gate", "w_ffn_up", "w_ffn_down"):
        return jnp.pad(a, [(0, 0)] * len(lead) + [(0, _FFN_SHARD_PAD - _FFN_SHARD), (0, 0)])
    return a


def _unpad_rows(name, a):
    lead = a.shape[:-2]
    if name == "w_uq":
        a = a.reshape(lead + (_HEADS_PER_SHARD, MLA_QK_PAD, a.shape[-1]))[..., :_QK, :]
        return a.reshape(lead + (_HEADS_PER_SHARD * _QK, a.shape[-1]))
    if name in ("w_ffn_gate", "w_ffn_up", "w_ffn_down"):
        return a[..., :_FFN_SHARD, :]
    return a


def _rows_view(name, a):
    return jnp.swapaxes(a, 0, 1) if name in _COL_SHARDED else a


def _shard_payload(name, shard):
    return _pad_rows(name, _rows_view(name, shard).astype(MXU_DTYPE))


def _full_from_gathered(name, g):
    full = g.reshape(g.shape[0] * g.shape[1], g.shape[2])
    if name == "w_in":
        full = jnp.pad(full, ((0, IN_WIDTH_PAD - IN_WIDTH), (0, 0)))
    return full


def _grad_chunks(name, gfull):
    if name == "w_in":
        gfull = gfull[:IN_WIDTH]
    return gfull.reshape(N_DEV, gfull.shape[0] // N_DEV, gfull.shape[1])


def _ffn_vec_layout(a):
    a = a.reshape(a.shape[0], N_DEV, _FFN_SHARD)
    return jnp.pad(a, ((0, 0), (0, 0), (0, _FFN_SHARD_PAD - _FFN_SHARD))).reshape(a.shape[0], D_FF_PAD)


def _ffn_vec_shards(a):
    return a.reshape(a.shape[0], N_DEV, _FFN_SHARD_PAD)[:, :, :_FFN_SHARD]


_MESH = pl.DeviceIdType.MESH
_ANY = pl.BlockSpec(memory_space=pl.ANY)


def _place():
    x, y, c = lax.axis_index("x"), lax.axis_index("y"), lax.axis_index("c")
    chips = [(1 - x, y), (x, 1 - y), (1 - x, 1 - y)]
    return x, y, c, chips


def _handshake(peers):
    barrier = pltpu.get_barrier_semaphore()
    for peer in peers:
        pl.semaphore_signal(barrier, inc=1, device_id=peer, device_id_type=_MESH)
    pl.semaphore_wait(barrier, len(peers))


_SEQUENCER = dict(axis_name="seq", num_cores=1)
_AG_COLLECTIVE_ID = 1
_RS_SIBLING_COLLECTIVE_ID = 2
_RS_CHIPS_COLLECTIVE_ID = 3


def _all_gather_seq(shard, *, name):
    def body(x_ref, out_ref, send_sems, recv_sems, local_sem):
        x, y, c, chips = _place()
        sibling = (x, y, 1 - c)
        _handshake([sibling] + [(*chip, c) for chip in chips])

        def slot(px, py, pc):
            return out_ref.at[4 * px + 2 * py + pc]

        def copy(k, block, to, src=None):
            return pltpu.make_async_remote_copy(
                src_ref=slot(*block) if src is None else src, dst_ref=slot(*block),
                send_sem=send_sems.at[k], recv_sem=recv_sems.at[k], device_id=to, device_id_type=_MESH)

        mine = pltpu.make_async_copy(x_ref, slot(x, y, c), local_sem)
        mine.start()
        first = [copy(0, (x, y, c), sibling, src=x_ref)]
        first += [copy(1 + j, (x, y, c), (*chip, c), src=x_ref) for j, chip in enumerate(chips)]
        for cp in first:
            cp.start()
        passed = [copy(4 + j, (*chip, c), sibling) for j, chip in enumerate(chips)]
        for j, chip in enumerate(chips):
            copy(1 + j, (*chip, c), (x, y, c)).wait_recv()
            passed[j].start()
        copy(0, sibling, (x, y, c)).wait_recv()
        for j, chip in enumerate(chips):
            copy(4 + j, (*chip, 1 - c), (x, y, c)).wait_recv()
        for cp in first + passed:
            cp.wait_send()
        mine.wait()

    return pl.kernel(
        body, out_type=jax.ShapeDtypeStruct((N_DEV,) + shard.shape, shard.dtype),
        mesh=plsc.ScalarSubcoreMesh(**_SEQUENCER), name=name,
        scratch_types=[pltpu.SemaphoreType.DMA((7,)), pltpu.SemaphoreType.DMA((7,)), pltpu.SemaphoreType.DMA(())],
        compiler_params=pltpu.CompilerParams(collective_id=_AG_COLLECTIVE_ID),
    )(shard)


def _exchange_sibling(g, *, name):
    all_slots = g.shape[0] == N_DEV

    def body(g_ref, out_ref, send_sems, recv_sems):
        x, y, c, _ = _place()
        sibling = (x, y, 1 - c)
        _handshake([sibling])
        copies = []
        for chip in range(4):
            cp = pltpu.make_async_remote_copy(
                src_ref=g_ref.at[2 * chip + (1 - c) if all_slots else chip], dst_ref=out_ref.at[chip],
                send_sem=send_sems.at[chip], recv_sem=recv_sems.at[chip], device_id=sibling, device_id_type=_MESH)
            cp.start()
            copies.append(cp)
        for cp in copies:
            cp.wait_recv()
        for cp in copies:
            cp.wait_send()

    return pl.kernel(
        body, out_type=jax.ShapeDtypeStruct((4,) + g.shape[1:], g.dtype),
        mesh=plsc.ScalarSubcoreMesh(**_SEQUENCER), name=name,
        scratch_types=[pltpu.SemaphoreType.DMA((4,)), pltpu.SemaphoreType.DMA((4,))],
        compiler_params=pltpu.CompilerParams(collective_id=_RS_SIBLING_COLLECTIVE_ID),
    )(g)


def _add_sibling(g, recv, *, name):
    _, r, cdim = g.shape
    tr, tc = _tile_2d(r, cdim, 6)
    g4 = g.reshape(4, 2, r, cdim)
    core = lax.axis_index("c").astype(jnp.int32).reshape(1)

    def body(c_ref, g_ref, r_ref, o_ref):
        o_ref[...] = (g_ref[...].astype(F32) + r_ref[...].astype(F32)).astype(o_ref.dtype)

    return pl.pallas_call(
        body, name=name,
        grid_spec=pltpu.PrefetchScalarGridSpec(
            num_scalar_prefetch=1, grid=(4, r // tr, cdim // tc),
            in_specs=[pl.BlockSpec((None, None, tr, tc), lambda ch, i, j, c_ref: (ch, c_ref[0], i, j)),
                      pl.BlockSpec((None, tr, tc), lambda ch, i, j, c_ref: (ch, i, j))],
            out_specs=pl.BlockSpec((None, tr, tc), lambda ch, i, j, c_ref: (ch, i, j))),
        out_shape=jax.ShapeDtypeStruct((4, r, cdim), g.dtype),
        compiler_params=_params(("parallel", "parallel", "parallel")),
    )(core, g4, recv)


def _exchange_chips(pch, *, name):
    def body(p_ref, out_ref, send_sems, recv_sems, local_sem):
        x, y, c, chips = _place()
        _handshake([(*chip, c) for chip in chips])
        me = 2 * x + y
        mine = pltpu.make_async_copy(p_ref.at[me], out_ref.at[me], local_sem)
        mine.start()
        copies = []
        for j, (px, py) in enumerate(chips):
            cp = pltpu.make_async_remote_copy(
                src_ref=p_ref.at[2 * px + py], dst_ref=out_ref.at[me],
                send_sem=send_sems.at[j], recv_sem=recv_sems.at[j], device_id=(px, py, c), device_id_type=_MESH)
            cp.start()
            copies.append(cp)
        for j, (px, py) in enumerate(chips):
            pltpu.make_async_remote_copy(
                src_ref=p_ref.at[me], dst_ref=out_ref.at[2 * px + py],
                send_sem=send_sems.at[j], recv_sem=recv_sems.at[j], device_id=(px, py, c), device_id_type=_MESH).wait_recv()
        for cp in copies:
            cp.wait_send()
        mine.wait()

    return pl.kernel(
        body, out_type=jax.ShapeDtypeStruct(pch.shape, pch.dtype),
        mesh=plsc.ScalarSubcoreMesh(**_SEQUENCER), name=name,
        scratch_types=[pltpu.SemaphoreType.DMA((3,)), pltpu.SemaphoreType.DMA((3,)), pltpu.SemaphoreType.DMA(())],
        compiler_params=pltpu.CompilerParams(collective_id=_RS_CHIPS_COLLECTIVE_ID),
    )(pch)


def _all_reduce_small(v, *, name):
    r = v.shape[0]

    def body(x_ref, out_ref, buf_ref, send_sems, recv_sems):
        x, y, c, chips = _place()
        sibling = (x, y, 1 - c)

        def slot(px, py, pc):
            return buf_ref.at[4 * px + 2 * py + pc]

        def copy(k, block, to, src=None):
            return pltpu.make_async_remote_copy(
                src_ref=slot(*block) if src is None else src, dst_ref=slot(*block),
                send_sem=send_sems.at[k], recv_sem=recv_sems.at[k], device_id=to, device_id_type=_MESH)

        first = [copy(0, (x, y, c), sibling, src=x_ref)]
        first += [copy(1 + j, (x, y, c), (*chip, c), src=x_ref) for j, chip in enumerate(chips)]
        for cp in first:
            cp.start()
        buf_ref[4 * x + 2 * y + c] = x_ref[...]
        passed = [copy(4 + j, (*chip, c), sibling) for j, chip in enumerate(chips)]
        for j, chip in enumerate(chips):
            copy(1 + j, (*chip, c), (x, y, c)).wait_recv()
            passed[j].start()
        copy(0, sibling, (x, y, c)).wait_recv()
        for j, chip in enumerate(chips):
            copy(4 + j, (*chip, 1 - c), (x, y, c)).wait_recv()
        for cp in first + passed:
            cp.wait_send()
        total = buf_ref[0]
        for k in range(1, N_DEV):
            total = total + buf_ref[k]
        out_ref[...] = total

    vm = pl.BlockSpec(memory_space=pltpu.VMEM)
    return pl.pallas_call(
        body, name=name, out_shape=jax.ShapeDtypeStruct(v.shape, v.dtype),
        in_specs=[vm], out_specs=vm,
        scratch_shapes=[pltpu.VMEM((N_DEV,) + v.shape, v.dtype), pltpu.SemaphoreType.DMA((7,)),
                        pltpu.SemaphoreType.DMA((7,))],
    )(v)


_ELEMENTWISE_VMEM = 24 * 1024 * 1024


def _tile_2d(r, c, n_arrays):
    per_block = _ELEMENTWISE_VMEM // (8 * n_arrays)
    tr = _tile(r, max(16, per_block // max(c, 128)), 16)
    by_rows = (tr, c) if tr * c <= per_block else None
    tc = _tile(c, max(128, (per_block // r) // 128 * 128))
    by_cols = (r, tc) if r * tc <= per_block else None
    if by_rows is None or (by_cols is not None and r * tc > tr * c):
        assert by_cols is not None, (r, c, n_arrays)
        return by_cols
    return by_rows


def _adam_math(w, g, m, v):
    m = ADAM_B1 * m + (1.0 - ADAM_B1) * g
    v = ADAM_B2 * v + (1.0 - ADAM_B2) * jnp.square(g)
    m_hat = m / (1.0 - ADAM_B1 ** ADAM_STEP)
    v_hat = v / (1.0 - ADAM_B2 ** ADAM_STEP)
    delta = -ADAM_LR * (m_hat / (jnp.sqrt(v_hat) + ADAM_EPS) + ADAM_WD * w)
    return delta, m, v


def _adam(w, g, m, v, *, name, parts=None):
    r, cdim = w.shape
    tr, tc = _tile_2d(r, cdim, 8)

    def body(w_ref, g_ref, m_ref, v_ref, go_ref, d_ref, mo_ref, vo_ref):
        if parts is None:
            g = g_ref[...]
        else:
            g = g_ref[0].astype(F32)
            for k in range(1, parts):
                g = g + g_ref[k].astype(F32)
        delta, m, v = _adam_math(w_ref[...], g, m_ref[...], v_ref[...])
        go_ref[...] = g
        d_ref[...] = delta
        mo_ref[...] = m
        vo_ref[...] = v

    blk = pl.BlockSpec((tr, tc), lambda i, j: (i, j))
    gblk = blk if parts is None else pl.BlockSpec((parts, tr, tc), lambda i, j: (0, i, j))
    out = jax.ShapeDtypeStruct((r, cdim), F32)
    return pl.pallas_call(
        body, name=name, grid=(r // tr, cdim // tc), in_specs=[blk, gblk, blk, blk], out_specs=[blk] * 4,
        out_shape=[out] * 4, compiler_params=_params(("parallel", "parallel")),
    )(w, g, m, v)


_BIG = ("w_in", "w_uq", "w_ukv", "w_o", "w_ffn_gate", "w_ffn_up", "w_ffn_down", "w_ple_gate", "w_ple_proj")
_WEIGHTS = ("w_in", "g_attn", "g_q_lora", "g_kv_lora", "w_uq", "w_ukv", "w_o", "g_ffn", "w_ffn_gate", "w_ffn_up",
            "conv_w", "conv_b", "w_ffn_down", "g_ple", "w_ple_gate", "w_ple_proj", "g_final")
_SMALL_PACK = (("g_attn", 1, D_MODEL), ("g_q_lora", 1, Q_LORA), ("g_kv_lora", 1, KV_LORA), ("g_ffn", 1, D_MODEL),
               ("conv_w", CONV_WIDTH, D_FF_PAD), ("conv_b", 1, D_FF_PAD), ("g_ple", 1, D_MODEL), ("g_final", 1, D_MODEL))


def _pack_small(gs):
    flat = jnp.concatenate([gs[n].reshape(-1) for n, _, _ in _SMALL_PACK])
    rows = -(-flat.shape[0] // 128)
    rows = -(-rows // 8) * 8
    return jnp.pad(flat, (0, rows * 128 - flat.shape[0])).reshape(rows, 128)


def _unpack_small(packed):
    flat = packed.reshape(-1)
    out, off = {}, 0
    for n, r, c in _SMALL_PACK:
        out[n] = flat[off:off + r * c].reshape(r, c)
        off += r * c
    return out


def kernel(x, p, w_in, g_attn, g_q_lora, g_kv_lora, w_uq, w_ukv, w_o, g_ffn, w_ffn_gate, w_ffn_up, conv_w, conv_b, w_ffn_down, g_ple, w_ple_gate, w_ple_proj, g_final, loss_target, m_w_in, m_g_attn, m_g_q_lora, m_g_kv_lora, m_w_uq, m_w_ukv, m_w_o, m_g_ffn, m_w_ffn_gate, m_w_ffn_up, m_conv_w, m_conv_b, m_w_ffn_down, m_g_ple, m_w_ple_gate, m_w_ple_proj, m_g_final, v_w_in, v_g_attn, v_g_q_lora, v_g_kv_lora, v_w_uq, v_w_ukv, v_w_o, v_g_ffn, v_w_ffn_gate, v_w_ffn_up, v_conv_w, v_conv_b, v_w_ffn_down, v_g_ple, v_w_ple_gate, v_w_ple_proj, v_g_final):
    given = dict(locals())
    wts = {n: given[n] for n in _WEIGHTS}
    mom = {n: given["m_" + n] for n in _WEIGHTS}
    var = {n: given["v_" + n] for n in _WEIGHTS}
    me = (4 * lax.axis_index("x") + 2 * lax.axis_index("y") + lax.axis_index("c")).astype(jnp.int32)
    ops = _ShardedWeights(wts, mom, var)

    conv_full = _all_gather_seq(conv_w, name="ag_conv_w")[:, 0].transpose(1, 0, 2).reshape(CONV_WIDTH, D_FF)
    vec = {"g_attn": g_attn, "g_q_lora": g_q_lora, "g_kv_lora": g_kv_lora, "g_ffn": g_ffn, "g_ple": g_ple,
           "g_final": g_final[None, :], "conv_w": _ffn_vec_layout(conv_full), "conv_b": _ffn_vec_layout(conv_b)}

    loss_part, grad_x, gs = _local_step(x[0], p[0, 0], loss_target[0], vec, ops)
    loss = lax.psum(loss_part[0, 0], ("x", "y", "c"))

    small = _unpack_small(_all_reduce_small(ops.order.tie(_pack_small(gs)), name="ar_small"))
    conv_w_shards = _ffn_vec_shards(small["conv_w"])
    small_g = {
        "g_attn": small["g_attn"], "g_q_lora": small["g_q_lora"], "g_kv_lora": small["g_kv_lora"],
        "g_ffn": small["g_ffn"], "g_ple": small["g_ple"], "g_final": small["g_final"],
        "conv_b": _ffn_vec_shards(small["conv_b"]).reshape(1, D_FF),
        "conv_w": lax.dynamic_index_in_dim(conv_w_shards, me, axis=1, keepdims=False),
    }
    results = dict(ops.results)
    for n, g in small_g.items():
        shape = wts[n].shape
        outs = ops.order.run(_adam, wts[n].reshape(g.shape), g, mom[n].reshape(g.shape), var[n].reshape(g.shape),
                             name="adam_" + n)
        results[n] = tuple(a.reshape(shape) for a in outs)
    ops.update("w_in")
    results["w_in"] = ops.results["w_in"]

    return (loss, grad_x[None], *[results[n][0] for n in _WEIGHTS], *[results[n][1] for n in _WEIGHTS],
            *[results[n][2] for n in _WEIGHTS], *[results[n][3] for n in _WEIGHTS])


class _ShardedWeights:
    def __init__(self, wts, mom, var):
        self.wts, self.mom, self.var = wts, mom, var
        self.order = _Order()
        self.full, self.stage, self.results = {}, {}, {}

    def start_gather(self, name, after=None, halves=False):
        payload = _shard_payload(name, self.wts[name][0])
        if after is not None:
            payload = lax.optimization_barrier((payload, after))[0]
        if not halves:
            self.full[name] = _full_from_gathered(name, _all_gather_seq(payload, name="ag_" + name))
            return
        half = payload.shape[1] // 2
        for part in range(2):
            piece = payload[:, part * half:(part + 1) * half]
            self.full[f"{name}/{part}"] = _full_from_gathered(name, _all_gather_seq(piece, name=f"ag_{name}_{part}"))

    def weight(self, name):
        return self.full[name]

    def grad(self, name, a, b):
        core = lax.axis_index("c")
        if name == "w_in":
            chunks = _grad_chunks(name, self.order.run(_matmul, a, b, ta=True, name="mm_d_" + name,
                                                        out_dtype=MXU_DTYPE))
            self.stage[name] = (chunks, None, _exchange_sibling(chunks, name="rs_sib_" + name))
            return
        theirs = self.order.run(_grad_matmul_half, a, b, 1 - core, name="mm_d_" + name + "_sib")
        self.stage[name] = (a, b, _exchange_sibling(theirs, name="rs_sib_" + name))

    def reduce_add(self, name):
        a, b, from_sibling = self.stage[name]
        if b is None:
            per_chip = self.order.run(_add_sibling, a, from_sibling, name="rs_add_" + name)
        else:
            per_chip = self.order.run(_grad_matmul_half, a, b, lax.axis_index("c"), name="mm_d_" + name + "_own",
                                      add=from_sibling)
        self.stage[name] = _exchange_chips(per_chip, name="rs_chip_" + name)

    def update(self, name):
        parts = self.stage[name]
        if name == "w_uq":
            parts = _unpad_rows(name, parts)
        rows = lambda a: _rows_view(name, a[0])
        outs = self.order.run(_adam, rows(self.wts[name]), parts, rows(self.mom[name]), rows(self.var[name]),
                              name="adam_" + name, parts=4)
        self.results[name] = tuple(_rows_view(name, a)[None] for a in outs)
```

```python
import functools

import numpy as np

import jax
import jax.numpy as jnp
from jax import lax
from jax.experimental import pallas as pl
from jax.experimental.pallas import tpu as pltpu
from jax.experimental.pallas import tpu_sc as plsc

D_MODEL = 4096
CHUNK = 64
PLE_DIM = 256
RET_HEADS = 8
RET_HEAD_DIM = 256
RET_WIDTH = 2048
MLA_HEADS = 16
MLA_NOPE = 128
MLA_ROPE = 64
MLA_V = 128
Q_LORA = 1024
KV_LORA = 512
D_FF = 11008
CONV_WIDTH = 3
ROPE_BASE = 10000.0
EPS = 1e-6
IN_WIDTH = 9792
ADAM_LR, ADAM_B1, ADAM_B2, ADAM_EPS, ADAM_WD, ADAM_STEP = 0.001, 0.9, 0.999, 1e-08, 0.01, 10

IN_WIDTH_PAD = 10240
D_FF_PAD = 11264
MLA_QK_PAD = 256
Q_WIDTH_PAD = MLA_HEADS * MLA_QK_PAD

N_DEV = 8
MXU_DTYPE = jnp.bfloat16
ATTN_BLOCK = 512
HEADS_PER_STEP = 4
FWD_HEADS_PER_STEP = 8
VMEM_LIMIT = 56 * 1024 * 1024

F32 = jnp.float32


def _tile(n, want, align=128):
    if n <= want:
        return n
    t = (want // align) * align
    while t >= align:
        if n % t == 0:
            return t
        t -= align
    return n


def _params(sem):
    return pltpu.CompilerParams(dimension_semantics=sem, vmem_limit_bytes=VMEM_LIMIT)


def _sigmoid(x):
    return 1.0 / (1.0 + jnp.exp(-x))


def _matmul(a, b, *, name, ta=False, tb=False, out_dtype=F32, add=None, tm=1024, tn=1024, tk=4096):
    m, k = (a.shape[1], a.shape[0]) if ta else a.shape
    k2, n = (b.shape[1], b.shape[0]) if tb else b.shape
    assert k == k2, (a.shape, b.shape, ta, tb)
    tm, tn, tk = _tile(m, tm), _tile(n, tn), _tile(k, tk)
    nk = k // tk
    dims = (((0 if ta else 1,), (1 if tb else 0,)), ((), ()))

    def body(*refs):
        a_ref, b_ref, o_ref = refs[0], refs[1], refs[3 if add is not None else 2]
        c_ref = refs[2] if add is not None else None
        part = lax.dot_general(a_ref[...].astype(MXU_DTYPE), b_ref[...].astype(MXU_DTYPE), dims,
                               preferred_element_type=F32)

        def finish(r):
            if c_ref is not None:
                r = r + c_ref[...].astype(F32)
            o_ref[...] = r.astype(out_dtype)

        if nk == 1:
            finish(part)
            return
        acc_ref = refs[-1]
        kk = pl.program_id(2)

        @pl.when(kk == 0)
        def _():
            acc_ref[...] = part

        @pl.when((kk > 0) & (kk < nk - 1))
        def _():
            acc_ref[...] += part

        @pl.when(kk == nk - 1)
        def _():
            finish(acc_ref[...] + part)

    a_spec = pl.BlockSpec((tk, tm), lambda i, j, kk: (kk, i)) if ta else pl.BlockSpec((tm, tk), lambda i, j, kk: (i, kk))
    b_spec = pl.BlockSpec((tn, tk), lambda i, j, kk: (j, kk)) if tb else pl.BlockSpec((tk, tn), lambda i, j, kk: (kk, j))
    in_specs = [a_spec, b_spec]
    args = [a, b]
    if add is not None:
        in_specs.append(pl.BlockSpec((tm, tn), lambda i, j, kk: (i, j)))
        args.append(add)
    return pl.pallas_call(
        body, name=name, grid=(m // tm, n // tn, nk),
        in_specs=in_specs, out_specs=pl.BlockSpec((tm, tn), lambda i, j, kk: (i, j)),
        out_shape=jax.ShapeDtypeStruct((m, n), out_dtype),
        scratch_shapes=[] if nk == 1 else [pltpu.VMEM((tm, tn), F32)],
        compiler_params=_params(("parallel", "parallel", "arbitrary")),
    )(*args)


def _grad_matmul_half(a, b, core_class, *, name, add=None):
    t, m = a.shape
    r = m // N_DEV
    cdim = b.shape[1]
    tn = _tile(cdim, 512 if r > 1024 else 1024)
    cls = jnp.asarray(core_class, jnp.int32).reshape(1)

    def body(cls_ref, a_ref, b_ref, *rest):
        part = lax.dot_general(a_ref[...].astype(MXU_DTYPE), b_ref[...].astype(MXU_DTYPE), _TN,
                               preferred_element_type=F32)
        if add is not None:
            part = part + rest[0][...].astype(F32)
        rest[-1][...] = part.astype(rest[-1].dtype)

    slot = pl.BlockSpec((None, r, tn), lambda q, j, cls_ref: (q, 0, j))
    in_specs = [pl.BlockSpec((t, r), lambda q, j, cls_ref: (0, 2 * q + cls_ref[0])),
                pl.BlockSpec((t, tn), lambda q, j, cls_ref: (0, j))] + ([slot] if add is not None else [])
    args = [a, b] + ([add] if add is not None else [])
    return pl.pallas_call(
        body, name=name,
        grid_spec=pltpu.PrefetchScalarGridSpec(
            num_scalar_prefetch=1, grid=(N_DEV // 2, cdim // tn), in_specs=in_specs, out_specs=slot),
        out_shape=jax.ShapeDtypeStruct((N_DEV // 2, r, cdim), MXU_DTYPE),
        compiler_params=_params(("parallel", "parallel")),
    )(cls, *args)


def _rms_fwd(x, g, *, name):
    t, d = x.shape
    tr = _tile(t, 256, 8)

    def body(x_ref, g_ref, o_ref):
        xf = x_ref[...]
        r = lax.rsqrt(jnp.mean(xf * xf, axis=-1, keepdims=True) + EPS)
        o_ref[...] = (xf * r * g_ref[...]).astype(o_ref.dtype)

    return pl.pallas_call(
        body, name=name, grid=(t // tr,),
        in_specs=[pl.BlockSpec((tr, d), lambda i: (i, 0)), pl.BlockSpec((1, d), lambda i: (0, 0))],
        out_specs=pl.BlockSpec((tr, d), lambda i: (i, 0)),
        out_shape=jax.ShapeDtypeStruct((t, d), MXU_DTYPE),
        compiler_params=_params(("parallel",)),
    )(x, g)


def _rms_bwd(x, dhn, g, res, *, name, low_copy=False):
    t, d = x.shape
    tr = _tile(t, 256, 8)

    def body(*refs):
        if res is None:
            x_ref, dh_ref, g_ref = refs[:3]
            outs = refs[3:]
            res_ref = None
        else:
            x_ref, dh_ref, g_ref, res_ref = refs[:4]
            outs = refs[4:]
        dx_ref, dg_ref = outs[0], outs[-1]
        xf = x_ref[...]
        dh = dh_ref[...].astype(F32)
        r = lax.rsqrt(jnp.mean(xf * xf, axis=-1, keepdims=True) + EPS)
        dyg = dh * g_ref[...]
        dx = r * dyg - xf * (r * r * r) * jnp.mean(dyg * xf, axis=-1, keepdims=True)
        if res_ref is not None:
            dx = dx + res_ref[...]
        dx_ref[...] = dx
        if low_copy:
            outs[1][...] = dx.astype(outs[1].dtype)
        part = jnp.sum(dh * xf * r, axis=0, keepdims=True)

        @pl.when(pl.program_id(0) == 0)
        def _():
            dg_ref[...] = part

        @pl.when(pl.program_id(0) > 0)
        def _():
            dg_ref[...] += part

    row = pl.BlockSpec((tr, d), lambda i: (i, 0))
    vec = pl.BlockSpec((1, d), lambda i: (0, 0))
    in_specs = [row, row, vec] + ([] if res is None else [row])
    args = [x, dhn, g] + ([] if res is None else [res])
    out_specs = [row] + ([row] if low_copy else []) + [vec]
    out_shape = [jax.ShapeDtypeStruct((t, d), F32)] + ([jax.ShapeDtypeStruct((t, d), MXU_DTYPE)] if low_copy else []) \
        + [jax.ShapeDtypeStruct((1, d), F32)]
    return pl.pallas_call(
        body, name=name, grid=(t // tr,), in_specs=in_specs, out_specs=out_specs, out_shape=out_shape,
        compiler_params=_params(("arbitrary",)),
    )(*args)


def _rope_tables(t):
    pos = jnp.arange(t, dtype=F32)[:, None]
    inv_r = 1.0 / (ROPE_BASE ** (jnp.arange(0, RET_HEAD_DIM, 2, dtype=F32) / RET_HEAD_DIM))
    ang_r = pos * inv_r[None, :]
    inv_m = 1.0 / (ROPE_BASE ** (jnp.arange(0, MLA_ROPE, 2, dtype=F32) / MLA_ROPE))
    ang_m = pos * inv_m[None, :]
    cm, sm = jnp.cos(ang_m), jnp.sin(ang_m)
    z = jnp.zeros_like(cm)
    cos_m = jnp.concatenate([cm, cm, z, z], axis=1)
    sin_m = jnp.concatenate([-sm, sm, z, z], axis=1)
    return jnp.cos(ang_r), jnp.sin(ang_r), cos_m, sin_m


def _rope256(x, c, s, inverse=False):
    x1, x2 = x[:, :128], x[:, 128:]
    if inverse:
        s = -s
    return jnp.concatenate([x1 * c - x2 * s, x2 * c + x1 * s], axis=1)


def _rope64(x, cos_m, sin_m, inverse=False):
    lane = lax.broadcasted_iota(jnp.int32, x.shape, 1)
    partner = jnp.where(lane < 32, pltpu.roll(x, 96, 1), pltpu.roll(x, 32, 1))
    s = -sin_m if inverse else sin_m
    return x * cos_m + partner * s


def _ret_prep(proj, cos_r, sin_r):
    t = proj.shape[0]
    tr = _tile(t, 256, 8)

    def body(q_ref, k_ref, v_ref, c_ref, s_ref, qo_ref, ko_ref, vo_ref):
        c, s = c_ref[...], s_ref[...]
        for h in range(RET_HEADS):
            cols = slice(h * RET_HEAD_DIM, (h + 1) * RET_HEAD_DIM)
            qo_ref[:, cols] = _rope256(q_ref[:, cols], c, s).astype(qo_ref.dtype)
            ko_ref[:, cols] = (_rope256(k_ref[:, cols], c, s) * (RET_HEAD_DIM ** -0.5)).astype(ko_ref.dtype)
        vo_ref[...] = v_ref[...].astype(vo_ref.dtype)

    group = lambda off: pl.BlockSpec((tr, RET_WIDTH), lambda i: (i, off))
    tab = pl.BlockSpec((tr, 128), lambda i: (i, 0))
    out = jax.ShapeDtypeStruct((t, RET_WIDTH), MXU_DTYPE)
    return pl.pallas_call(
        body, name="ret_prep", grid=(t // tr,),
        in_specs=[group(0), group(1), group(2), tab, tab],
        out_specs=[group(0), group(0), group(0)], out_shape=[out, out, out],
        compiler_params=_params(("parallel",)),
    )(proj, proj, proj, cos_r, sin_r)


def _ret_log_gamma():
    return jnp.asarray(np.log1p(-np.exp2(-5.0 - np.arange(RET_HEADS, dtype=np.float64))), dtype=F32)


def _decay_full(lg, i, j, blk):
    r = lax.broadcasted_iota(jnp.int32, (blk, 1), 0).astype(F32)
    c = lax.broadcasted_iota(jnp.int32, (1, blk), 1).astype(F32)
    off = ((i - j) * blk).astype(F32)
    return jnp.exp(lg * r), jnp.exp(lg * (off - c))


def _decay_diag(lg, blk):
    r = lax.broadcasted_iota(jnp.int32, (blk, blk), 0)
    c = lax.broadcasted_iota(jnp.int32, (blk, blk), 1)
    ok = (c // CHUNK) <= (r // CHUNK)
    return jnp.where(ok, jnp.exp(lg * jnp.abs(r - c).astype(F32)), 0.0)


_NT = (((1,), (1,)), ((), ()))
_TN = (((0,), (0,)), ((), ()))
_NN = (((1,), (0,)), ((), ()))


def _causal_pairs(nb, query_major):
    if query_major:
        pairs = [(i, j) for i in range(nb) for j in range(i + 1)]
    else:
        pairs = [(i, j) for j in range(nb) for i in range(j, nb)]
    arr = np.asarray(pairs, dtype=np.int32)
    return jnp.asarray(arr[:, 0]), jnp.asarray(arr[:, 1])


def _ret_fwd(q, k, v, proj, lg):
    t = q.shape[0]
    blk = _tile(t, ATTN_BLOCK)
    nb = t // blk
    hps, d = FWD_HEADS_PER_STEP, RET_HEAD_DIM
    gate_off = 3 * RET_WIDTH // (hps * d)

    def body(ii_ref, jj_ref, lg_ref, q_ref, k_ref, v_ref, g_ref, raw_ref, ro_ref, acc_ref):
        hg, pair = pl.program_id(0), pl.program_id(1)
        i, j = ii_ref[pair], jj_ref[pair]

        @pl.when(j == 0)
        def _():
            acc_ref[...] = jnp.zeros_like(acc_ref)

        def step(diag):
            for h in range(hps):
                cols = slice(h * d, (h + 1) * d)
                lgh = lg_ref[hg * hps + h]
                s = lax.dot_general(q_ref[:, cols], k_ref[:, cols], _NT, preferred_element_type=F32)
                if diag:
                    w = s * _decay_diag(lgh, blk)
                else:
                    a, b = _decay_full(lgh, i, j, blk)
                    w = s * a * b
                acc_ref[h] += lax.dot_general(w.astype(MXU_DTYPE), v_ref[:, cols], _NN, preferred_element_type=F32)

        @pl.when(j < i)
        def _():
            step(False)

        @pl.when(j == i)
        def _():
            step(True)
            for h in range(hps):
                cols = slice(h * d, (h + 1) * d)
                o = acc_ref[h]
                raw_ref[:, cols] = o
                mu = jnp.mean(o, axis=-1, keepdims=True)
                var = jnp.mean(jnp.square(o - mu), axis=-1, keepdims=True)
                hn = (o - mu) * lax.rsqrt(var + EPS)
                g = g_ref[:, cols]
                ro_ref[:, cols] = (g * _sigmoid(g) * hn).astype(ro_ref.dtype)

    qs = pl.BlockSpec((blk, hps * d), lambda h, p, ii, jj: (ii[p], h))
    ks = pl.BlockSpec((blk, hps * d), lambda h, p, ii, jj: (jj[p], h))
    gs = pl.BlockSpec((blk, hps * d), lambda h, p, ii, jj: (ii[p], h + gate_off))
    ii, jj = _causal_pairs(nb, query_major=True)
    return pl.pallas_call(
        body, name="ret_fwd",
        grid_spec=pltpu.PrefetchScalarGridSpec(
            num_scalar_prefetch=2, grid=(RET_HEADS // hps, ii.shape[0]),
            in_specs=[pl.BlockSpec(memory_space=pltpu.SMEM), qs, ks, ks, gs], out_specs=[qs, qs],
            scratch_shapes=[pltpu.VMEM((hps, blk, d), F32)]),
        out_shape=[jax.ShapeDtypeStruct((t, RET_WIDTH), F32), jax.ShapeDtypeStruct((t, RET_WIDTH), MXU_DTYPE)],
        compiler_params=_params(("parallel", "arbitrary")),
    )(ii, jj, lg, q, k, v, proj)


def _ret_gate_bwd(raw, proj, dattn):
    t = raw.shape[0]
    tr = _tile(t, 256, 8)

    def body(o_ref, g_ref, d_ref, do_ref, dg_ref):
        for h in range(RET_HEADS):
            cols = slice(h * RET_HEAD_DIM, (h + 1) * RET_HEAD_DIM)
            o, g, d = o_ref[:, cols], g_ref[:, cols], d_ref[:, cols]
            mu = jnp.mean(o, axis=-1, keepdims=True)
            rstd = lax.rsqrt(jnp.mean(jnp.square(o - mu), axis=-1, keepdims=True) + EPS)
            hn = (o - mu) * rstd
            sg = _sigmoid(g)
            dg_ref[:, cols] = (d * hn * (sg * (1.0 + g * (1.0 - sg)))).astype(dg_ref.dtype)
            dhn = d * (g * sg)
            do = rstd * (dhn - jnp.mean(dhn, axis=-1, keepdims=True)
                         - hn * jnp.mean(dhn * hn, axis=-1, keepdims=True))
            do_ref[:, cols] = do.astype(do_ref.dtype)

    group = lambda off: pl.BlockSpec((tr, RET_WIDTH), lambda i: (i, off))
    out = jax.ShapeDtypeStruct((t, RET_WIDTH), MXU_DTYPE)
    return pl.pallas_call(
        body, name="ret_gate_bwd", grid=(t // tr,),
        in_specs=[group(0), group(3), group(0)], out_specs=[group(0), group(0)], out_shape=[out, out],
        compiler_params=_params(("parallel",)),
    )(raw, proj, dattn)


def _ret_bwd(q, k, v, do, lg):
    t = q.shape[0]
    blk = _tile(t, ATTN_BLOCK)
    nb = t // blk

    hps, d = HEADS_PER_STEP, RET_HEAD_DIM

    def body(ii_ref, jj_ref, lg_ref, q_ref, k_ref, v_ref, do_ref, dq_ref, dk_ref, dv_ref, dk_acc, dv_acc):
        hg, pair = pl.program_id(0), pl.program_id(1)
        i, j = ii_ref[pair], jj_ref[pair]

        @pl.when(pair == 0)
        def _():
            dq_ref[...] = jnp.zeros_like(dq_ref)

        @pl.when(i == j)
        def _():
            dk_acc[...] = jnp.zeros_like(dk_acc)
            dv_acc[...] = jnp.zeros_like(dv_acc)

        def step(diag):
            rows = pl.ds(pl.multiple_of(i * blk, blk), blk)
            for h in range(hps):
                cols = slice(h * d, (h + 1) * d)
                lgh = lg_ref[hg * hps + h]
                if diag:
                    decay = _decay_diag(lgh, blk)
                else:
                    a, b = _decay_full(lgh, i, j, blk)
                    decay = a * b
                qb, kb, vb, dob = q_ref[:, cols], k_ref[:, cols], v_ref[:, cols], do_ref[:, cols]
                s = lax.dot_general(qb, kb, _NT, preferred_element_type=F32)
                w = (s * decay).astype(MXU_DTYPE)
                dv_acc[h] += lax.dot_general(w, dob, _TN, preferred_element_type=F32)
                dw = lax.dot_general(dob, vb, _NT, preferred_element_type=F32)
                ds = (dw * decay).astype(MXU_DTYPE)
                dq_ref[rows, cols] += lax.dot_general(ds, kb, _NN, preferred_element_type=F32)
                dk_acc[h] += lax.dot_general(ds, qb, _TN, preferred_element_type=F32)

        @pl.when(i > j)
        def _():
            step(False)

        @pl.when(i == j)
        def _():
            step(True)

        @pl.when(i == nb - 1)
        def _():
            for h in range(hps):
                cols = slice(h * d, (h + 1) * d)
                dk_ref[:, cols] = dk_acc[h]
                dv_ref[:, cols] = dv_acc[h].astype(dv_ref.dtype)

    qs = pl.BlockSpec((blk, hps * d), lambda h, p, ii, jj: (ii[p], h))
    ks = pl.BlockSpec((blk, hps * d), lambda h, p, ii, jj: (jj[p], h))
    ii, jj = _causal_pairs(nb, query_major=False)
    return pl.pallas_call(
        body, name="ret_bwd",
        grid_spec=pltpu.PrefetchScalarGridSpec(
            num_scalar_prefetch=2, grid=(RET_HEADS // hps, ii.shape[0]),
            in_specs=[pl.BlockSpec(memory_space=pltpu.SMEM), qs, ks, ks, qs],
            out_specs=[pl.BlockSpec((t, hps * d), lambda h, p, ii, jj: (0, h), pipeline_mode=pl.Buffered(1)), ks, ks],
            scratch_shapes=[pltpu.VMEM((hps, blk, d), F32), pltpu.VMEM((hps, blk, d), F32)]),
        out_shape=[jax.ShapeDtypeStruct((t, RET_WIDTH), F32), jax.ShapeDtypeStruct((t, RET_WIDTH), F32),
                   jax.ShapeDtypeStruct((t, RET_WIDTH), MXU_DTYPE)],
        compiler_params=_params(("parallel", "arbitrary")),
    )(ii, jj, lg, q, k, v, do)


def _ret_unrope(dq, dk, cos_r, sin_r):
    t = dq.shape[0]
    tr = _tile(t, 256, 8)

    def body(dq_ref, dk_ref, c_ref, s_ref, oq_ref, ok_ref):
        c, s = c_ref[...], s_ref[...]
        for h in range(RET_HEADS):
            cols = slice(h * RET_HEAD_DIM, (h + 1) * RET_HEAD_DIM)
            oq_ref[:, cols] = _rope256(dq_ref[:, cols], c, s, inverse=True).astype(oq_ref.dtype)
            ok_ref[:, cols] = (_rope256(dk_ref[:, cols], c, s, inverse=True)
                               * (RET_HEAD_DIM ** -0.5)).astype(ok_ref.dtype)

    rows = pl.BlockSpec((tr, RET_WIDTH), lambda i: (i, 0))
    tab = pl.BlockSpec((tr, 128), lambda i: (i, 0))
    out = jax.ShapeDtypeStruct((t, RET_WIDTH), MXU_DTYPE)
    return pl.pallas_call(
        body, name="ret_unrope", grid=(t // tr,),
        in_specs=[rows, rows, tab, tab], out_specs=[rows, rows], out_shape=[out, out],
        compiler_params=_params(("parallel",)),
    )(dq, dk, cos_r, sin_r)


def _mla_prep(cq, ckv, kr, g_q, g_kv, cos_m, sin_m):
    t = cq.shape[0]
    tr = _tile(t, 512, 8)

    def body(cq_ref, ckv_ref, kr_ref, gq_ref, gkv_ref, c_ref, s_ref, cqn_ref, kvn_ref, kro_ref):
        for x_ref, g_ref, o_ref in ((cq_ref, gq_ref, cqn_ref), (ckv_ref, gkv_ref, kvn_ref)):
            xf = x_ref[...]
            r = lax.rsqrt(jnp.mean(xf * xf, axis=-1, keepdims=True) + EPS)
            o_ref[...] = (xf * r * g_ref[...]).astype(o_ref.dtype)
        kro_ref[...] = _rope64(kr_ref[...], c_ref[...], s_ref[...]).astype(kro_ref.dtype)

    row = lambda w: pl.BlockSpec((tr, w), lambda i: (i, 0))
    vec = lambda w: pl.BlockSpec((1, w), lambda i: (0, 0))
    return pl.pallas_call(
        body, name="mla_prep", grid=(t // tr,),
        in_specs=[row(Q_LORA), row(KV_LORA), row(128), vec(Q_LORA), vec(KV_LORA), row(128), row(128)],
        out_specs=[row(Q_LORA), row(KV_LORA), row(128)],
        out_shape=[jax.ShapeDtypeStruct((t, Q_LORA), MXU_DTYPE), jax.ShapeDtypeStruct((t, KV_LORA), MXU_DTYPE),
                   jax.ShapeDtypeStruct((t, 128), MXU_DTYPE)],
        compiler_params=_params(("parallel",)),
    )(cq, ckv, kr, g_q, g_kv, cos_m, sin_m)


def _mla_q_rope(q_lin, cos_m, sin_m, *, inverse, name):
    t = q_lin.shape[0]
    tr = _tile(t, 256, 8)

    def body(q_ref, c_ref, s_ref, o_ref):
        c, s = c_ref[...], s_ref[...]
        for h in range(MLA_HEADS):
            lo = h * MLA_QK_PAD
            o_ref[:, lo:lo + MLA_NOPE] = q_ref[:, lo:lo + MLA_NOPE].astype(o_ref.dtype)
            roped = _rope64(q_ref[:, lo + MLA_NOPE:lo + MLA_QK_PAD].astype(F32), c, s, inverse=inverse)
            o_ref[:, lo + MLA_NOPE:lo + MLA_QK_PAD] = roped.astype(o_ref.dtype)

    rows = pl.BlockSpec((tr, Q_WIDTH_PAD), lambda i: (i, 0))
    tab = pl.BlockSpec((tr, 128), lambda i: (i, 0))
    return pl.pallas_call(
        body, name=name, grid=(t // tr,),
        in_specs=[rows, tab, tab], out_specs=rows, out_shape=jax.ShapeDtypeStruct((t, Q_WIDTH_PAD), MXU_DTYPE),
        compiler_params=_params(("parallel",)),
    )(q_lin, cos_m, sin_m)


_MLA_SCALE = (MLA_NOPE + MLA_ROPE) ** -0.5
_LOG2_E = 1.4426950408889634
_MLA_SCALE_LOG2 = _MLA_SCALE * _LOG2_E
_NEG = -1e30


def _mla_mask(blk):
    r = lax.broadcasted_iota(jnp.int32, (blk, blk), 0)
    c = lax.broadcasted_iota(jnp.int32, (blk, blk), 1)
    return (c // CHUNK) <= (r // CHUNK)


def _mla_fwd(q, kv, kr):
    t = q.shape[0]
    blk = _tile(t, ATTN_BLOCK)
    nb = t // blk

    hps, dq_, dv_ = MLA_HEADS, MLA_QK_PAD, MLA_V

    def body(ii_ref, jj_ref, q_ref, kv_ref, kr_ref, o_ref, lse_ref, m_ref, acc_ref):
        pair = pl.program_id(1)
        i, j = ii_ref[pair], jj_ref[pair]

        @pl.when(j == 0)
        def _():
            m_ref[...] = jnp.full_like(m_ref, _NEG)
            acc_ref[...] = jnp.zeros_like(acc_ref)

        def step(masked):
            krb = kr_ref[...]
            ones = jnp.ones((blk, dv_), MXU_DTYPE)
            for h in range(hps):
                kb = jnp.concatenate([kv_ref[:, h * dq_:h * dq_ + MLA_NOPE], krb], axis=1)
                vb = jnp.concatenate([kv_ref[:, h * dq_ + MLA_NOPE:(h + 1) * dq_], ones], axis=1)
                s = lax.dot_general(q_ref[:, h * dq_:(h + 1) * dq_], kb, _NT, preferred_element_type=F32)
                if masked:
                    s = jnp.where(_mla_mask(blk), s, _NEG)
                m_prev = m_ref[h]
                m_new = jnp.maximum(m_prev, jnp.max(s, axis=-1, keepdims=True))
                alpha = jnp.exp2((m_prev - m_new) * _MLA_SCALE_LOG2)
                p = jnp.exp2((s - jnp.tile(m_new, (1, blk // 128))) * _MLA_SCALE_LOG2)
                acc_ref[h] = jnp.tile(alpha, (1, 2)) * acc_ref[h] + lax.dot_general(
                    p.astype(MXU_DTYPE), vb, _NN, preferred_element_type=F32)
                m_ref[h] = m_new

        @pl.when(j < i)
        def _():
            step(False)

        @pl.when(j == i)
        def _():
            step(True)
            for h in range(hps):
                cols = slice(h * dv_, (h + 1) * dv_)
                acc = acc_ref[h]
                row_sum = acc[:, dv_:]
                o_ref[:, cols] = (acc[:, :dv_] / row_sum).astype(o_ref.dtype)
                lse_ref[:, cols] = m_ref[h] * _MLA_SCALE + jnp.log(row_sum)

    os_ = pl.BlockSpec((blk, hps * dv_), lambda h, p, ii, jj: (ii[p], h))
    ii, jj = _causal_pairs(nb, query_major=True)
    return pl.pallas_call(
        body, name="mla_fwd",
        grid_spec=pltpu.PrefetchScalarGridSpec(
            num_scalar_prefetch=2, grid=(MLA_HEADS // hps, ii.shape[0]),
            in_specs=[pl.BlockSpec((blk, hps * dq_), lambda h, p, ii, jj: (ii[p], h)),
                      pl.BlockSpec((blk, hps * dq_), lambda h, p, ii, jj: (jj[p], h)),
                      pl.BlockSpec((blk, 128), lambda h, p, ii, jj: (jj[p], 0))],
            out_specs=[os_, os_],
            scratch_shapes=[pltpu.VMEM((hps, blk, 128), F32), pltpu.VMEM((hps, blk, 2 * dv_), F32)]),
        out_shape=[jax.ShapeDtypeStruct((t, MLA_HEADS * MLA_V), MXU_DTYPE),
                   jax.ShapeDtypeStruct((t, MLA_HEADS * 128), F32)],
        compiler_params=_params(("parallel", "arbitrary")),
    )(ii, jj, q, kv, kr)


def _mla_bwd(q, kv, kr, o, lse, dattn):
    t = q.shape[0]
    blk = _tile(t, ATTN_BLOCK)
    nb = t // blk
    hps, dq_, dv_ = HEADS_PER_STEP, MLA_QK_PAD, MLA_V
    do_off = RET_WIDTH // (hps * dv_)

    def body(ii_ref, jj_ref, q_ref, kv_ref, kr_ref, o_ref, lse_ref, do_ref, dq_ref, dk_ref, dv_ref, dk_acc, dv_acc):
        pair = pl.program_id(1)
        i, j = ii_ref[pair], jj_ref[pair]

        @pl.when(pair == 0)
        def _():
            dq_ref[...] = jnp.zeros_like(dq_ref)

        @pl.when(i == j)
        def _():
            dk_acc[...] = jnp.zeros_like(dk_acc)
            dv_acc[...] = jnp.zeros_like(dv_acc)

        def step(masked):
            krb = kr_ref[...]
            rows = pl.ds(pl.multiple_of(i * blk, blk), blk)
            for h in range(hps):
                qcols, vcols = slice(h * dq_, (h + 1) * dq_), slice(h * dv_, (h + 1) * dv_)
                qb = q_ref[:, qcols]
                kb = jnp.concatenate([kv_ref[:, h * dq_:h * dq_ + MLA_NOPE], krb], axis=1)
                vb = kv_ref[:, h * dq_ + MLA_NOPE:(h + 1) * dq_]
                dof = do_ref[:, vcols]
                dob = dof.astype(MXU_DTYPE)
                s = lax.dot_general(qb, kb, _NT, preferred_element_type=F32)
                if masked:
                    s = jnp.where(_mla_mask(blk), s, _NEG)
                lse2 = lse_ref[:, vcols] * _LOG2_E
                p = jnp.exp2(s * _MLA_SCALE_LOG2 - jnp.tile(lse2, (1, blk // 128)))
                delta = jnp.sum(dof * o_ref[:, vcols].astype(F32), axis=-1, keepdims=True)
                dv_acc[h] += lax.dot_general(p.astype(MXU_DTYPE), dob, _TN, preferred_element_type=F32)
                dp = lax.dot_general(dob, vb, _NT, preferred_element_type=F32)
                ds = (p * (dp - delta) * _MLA_SCALE).astype(MXU_DTYPE)
                dq_ref[rows, qcols] += lax.dot_general(ds, kb, _NN, preferred_element_type=F32)
                dk_acc[h] += lax.dot_general(ds, qb, _TN, preferred_element_type=F32)

        @pl.when(i > j)
        def _():
            step(False)

        @pl.when(i == j)
        def _():
            step(True)

        @pl.when(i == nb - 1)
        def _():
            for h in range(hps):
                dk_ref[:, h * dq_:(h + 1) * dq_] = dk_acc[h]
                dv_ref[:, h * dv_:(h + 1) * dv_] = dv_acc[h].astype(dv_ref.dtype)

    qmap = lambda off: (lambda h, p, ii, jj: (ii[p], h + off))
    kmap = lambda h, p, ii, jj: (jj[p], h)
    ii, jj = _causal_pairs(nb, query_major=False)
    return pl.pallas_call(
        body, name="mla_bwd",
        grid_spec=pltpu.PrefetchScalarGridSpec(
            num_scalar_prefetch=2, grid=(MLA_HEADS // hps, ii.shape[0]),
            in_specs=[pl.BlockSpec((blk, hps * dq_), qmap(0)), pl.BlockSpec((blk, hps * dq_), kmap),
                      pl.BlockSpec((blk, 128), lambda h, p, ii, jj: (jj[p], 0)),
                      pl.BlockSpec((blk, hps * dv_), qmap(0)), pl.BlockSpec((blk, hps * dv_), qmap(0)),
                      pl.BlockSpec((blk, hps * dv_), qmap(do_off))],
            out_specs=[pl.BlockSpec((t, hps * dq_), lambda h, p, ii, jj: (0, h), pipeline_mode=pl.Buffered(1)),
                       pl.BlockSpec((blk, hps * dq_), kmap), pl.BlockSpec((blk, hps * dv_), kmap)],
            scratch_shapes=[pltpu.VMEM((hps, blk, dq_), F32), pltpu.VMEM((hps, blk, dv_), F32)]),
        out_shape=[jax.ShapeDtypeStruct((t, Q_WIDTH_PAD), F32), jax.ShapeDtypeStruct((t, Q_WIDTH_PAD), F32),
                   jax.ShapeDtypeStruct((t, MLA_HEADS * MLA_V), MXU_DTYPE)],
        compiler_params=_params(("parallel", "arbitrary")),
    )(ii, jj, q, kv, kr, o, lse, dattn)


def _mla_kv_grad(dk, dv, cos_m, sin_m):
    t = dk.shape[0]
    tr = _tile(t, 256, 8)

    def body(dk_ref, dv_ref, c_ref, s_ref, dkv_ref, dkr_ref):
        acc = jnp.zeros((tr, 128), F32)
        for h in range(MLA_HEADS):
            dkv_ref[:, h * 256:h * 256 + 128] = dk_ref[:, h * 256:h * 256 + 128].astype(dkv_ref.dtype)
            dkv_ref[:, h * 256 + 128:h * 256 + 256] = dv_ref[:, h * 128:(h + 1) * 128].astype(dkv_ref.dtype)
            acc = acc + dk_ref[:, h * 256 + 128:h * 256 + 256]
        dkr_ref[...] = _rope64(acc, c_ref[...], s_ref[...], inverse=True).astype(dkr_ref.dtype)

    row = lambda w: pl.BlockSpec((tr, w), lambda i: (i, 0))
    return pl.pallas_call(
        body, name="mla_kv_grad", grid=(t // tr,),
        in_specs=[row(Q_WIDTH_PAD), row(MLA_HEADS * MLA_V), row(128), row(128)],
        out_specs=[row(Q_WIDTH_PAD), row(128)],
        out_shape=[jax.ShapeDtypeStruct((t, Q_WIDTH_PAD), MXU_DTYPE), jax.ShapeDtypeStruct((t, 128), MXU_DTYPE)],
        compiler_params=_params(("parallel",)),
    )(dk, dv, cos_m, sin_m)


_FFN_COLS = 256
_FFN_ROWS = 256


def _shift_down(cur, prev8, n):
    out = pltpu.roll(cur, n, 0)
    head = out[:8]
    row = lax.broadcasted_iota(jnp.int32, head.shape, 0)
    for r in range(n):
        head = jnp.where(row == r, prev8[8 - n + r:8 - n + r + 1, :], head)
    return jnp.concatenate([head, out[8:]], axis=0)


def _shift_up(cur, next8, n):
    rows = cur.shape[0]
    out = pltpu.roll(cur, rows - n, 0)
    tail = out[rows - 8:]
    row = lax.broadcasted_iota(jnp.int32, tail.shape, 0)
    for r in range(n):
        tail = jnp.where(row == 8 - n + r, next8[r:r + 1, :], tail)
    return jnp.concatenate([out[:rows - 8], tail], axis=0)


def _conv_pre(g_ref, cw_ref, cb_ref, c, rc):
    r0 = pl.multiple_of(c * rc, rc)
    cur = g_ref[pl.ds(r0, rc), :].astype(F32)
    prev16 = g_ref[pl.ds(pl.multiple_of(jnp.maximum(r0 - 16, 0), 16), 16), :].astype(F32)
    prev8 = jnp.where(c > 0, prev16[8:], 0.0)
    s1, s2 = _shift_down(cur, prev8, 1), _shift_down(cur, prev8, 2)
    a = cb_ref[...] + cw_ref[2:3, :] * cur + cw_ref[1:2, :] * s1 + cw_ref[0:1, :] * s2
    return r0, cur, s1, s2, a


def _ffn_up_act(hn, w_up_t, gpre, cw, cb):
    t, k = hn.shape
    f = w_up_t.shape[0]
    tm, tn = _tile(t, 1024), _tile(f, 512)
    halo = 16

    def body(a_ref, b_ref, g_ref, h_ref, cw_ref, cb_ref, u_ref, act_ref):
        u = lax.dot_general(a_ref[...].astype(MXU_DTYPE), b_ref[...].astype(MXU_DTYPE), _NT,
                            preferred_element_type=F32)
        g = g_ref[...].astype(F32)
        prev8 = jnp.where(pl.program_id(0) > 0, h_ref[...].astype(F32)[halo - 8:], 0.0)
        s1, s2 = _shift_down(g, prev8, 1), _shift_down(g, prev8, 2)
        a = cb_ref[...] + cw_ref[2:3, :] * g + cw_ref[1:2, :] * s1 + cw_ref[0:1, :] * s2
        u_ref[...] = u.astype(u_ref.dtype)
        act_ref[...] = (a * _sigmoid(a) * u).astype(act_ref.dtype)

    tile = pl.BlockSpec((tm, tn), lambda i, j: (i, j))
    low = jax.ShapeDtypeStruct((t, f), MXU_DTYPE)
    return pl.pallas_call(
        body, name="mm_up_act", grid=(t // tm, f // tn),
        in_specs=[pl.BlockSpec((tm, k), lambda i, j: (i, 0)), pl.BlockSpec((tn, k), lambda i, j: (j, 0)), tile,
                  pl.BlockSpec((halo, tn), lambda i, j: (jnp.maximum(i * (tm // halo) - 1, 0), j)),
                  pl.BlockSpec((CONV_WIDTH, tn), lambda i, j: (0, j)), pl.BlockSpec((1, tn), lambda i, j: (0, j))],
        out_specs=[tile, tile], out_shape=[low, low],
        compiler_params=_params(("parallel", "parallel")),
    )(hn, w_up_t, gpre, gpre, cw, cb)


def _ffn_act_bwd(gpre, u, dact, cw, cb):
    t, f = gpre.shape
    tc = _tile(f, _FFN_COLS)
    rc = _tile(t, _FFN_ROWS, 8)
    nc = t // rc

    def body(g_ref, u_ref, d_ref, cw_ref, cb_ref, dg_ref, du_ref, dcw_ref, dcb_ref, da_ref):
        def chunk(c, carry):
            w0, w1, w2, b = carry
            r0, cur, s1, s2, a = _conv_pre(g_ref, cw_ref, cb_ref, c, rc)
            sg = _sigmoid(a)
            d = d_ref[pl.ds(r0, rc), :].astype(F32)
            du_ref[pl.ds(r0, rc), :] = (d * (a * sg)).astype(du_ref.dtype)
            da = d * u_ref[pl.ds(r0, rc), :].astype(F32) * (sg * (1.0 + a * (1.0 - sg)))
            da_ref[pl.ds(r0, rc), :] = da
            return (w0 + jnp.sum(da * s2, axis=0, keepdims=True), w1 + jnp.sum(da * s1, axis=0, keepdims=True),
                    w2 + jnp.sum(da * cur, axis=0, keepdims=True), b + jnp.sum(da, axis=0, keepdims=True))
        z = jnp.zeros((1, tc), F32)
        w0, w1, w2, b = lax.fori_loop(0, nc, chunk, (z, z, z, z))
        dcw_ref[0:1, :] = w0
        dcw_ref[1:2, :] = w1
        dcw_ref[2:3, :] = w2
        dcb_ref[...] = b

        def chunk2(c, carry):
            r0 = pl.multiple_of(c * rc, rc)
            cur = da_ref[pl.ds(r0, rc), :]
            nxt = da_ref[pl.ds(pl.multiple_of(jnp.minimum(r0 + rc, t - 8), 8), 8), :]
            nxt = jnp.where(c < nc - 1, nxt, 0.0)
            dg = cw_ref[2:3, :] * cur + cw_ref[1:2, :] * _shift_up(cur, nxt, 1) + cw_ref[0:1, :] * _shift_up(cur, nxt, 2)
            dg_ref[pl.ds(r0, rc), :] = dg.astype(dg_ref.dtype)
            return carry
        lax.fori_loop(0, nc, chunk2, 0)

    col = pl.BlockSpec((t, tc), lambda j: (0, j))
    w3 = pl.BlockSpec((CONV_WIDTH, tc), lambda j: (0, j))
    w1 = pl.BlockSpec((1, tc), lambda j: (0, j))
    low = jax.ShapeDtypeStruct((t, f), MXU_DTYPE)
    return pl.pallas_call(
        body, name="ffn_act_bwd", grid=(f // tc,),
        in_specs=[col, col, col, w3, w1], out_specs=[col, col, w3, w1],
        out_shape=[low, low, jax.ShapeDtypeStruct((CONV_WIDTH, f), F32), jax.ShapeDtypeStruct((1, f), F32)],
        scratch_shapes=[pltpu.VMEM((t, tc), F32)],
        compiler_params=_params(("parallel",)),
    )(gpre, u, dact, cw, cb)


def _head_fwd_bwd(h2, glin, pp, target, g_final):
    t, d = h2.shape
    tr = _tile(t, 128, 8)

    def body(h_ref, gl_ref, pp_ref, t_ref, g_ref, loss_ref, dh_ref, dgl_ref, dpp_ref, dg_ref):
        gate = _sigmoid(gl_ref[...])
        ppv = pp_ref[...]
        h3 = h_ref[...] + gate * ppv
        r = lax.rsqrt(jnp.mean(h3 * h3, axis=-1, keepdims=True) + EPS)
        yh = h3 * r
        g = g_ref[...]
        diff = yh * g - t_ref[...]
        lpart = 0.5 * jnp.sum(jnp.mean(diff * diff, axis=-1, keepdims=True), axis=0, keepdims=True)
        dy = diff * (1.0 / d)
        dyg = dy * g
        dh3 = r * dyg - h3 * (r * r * r) * jnp.mean(dyg * h3, axis=-1, keepdims=True)
        dh_ref[...] = dh3
        dgl_ref[...] = (dh3 * ppv * gate * (1.0 - gate)).astype(dgl_ref.dtype)
        dpp_ref[...] = (dh3 * gate).astype(dpp_ref.dtype)
        dgp = jnp.sum(dy * yh, axis=0, keepdims=True)

        @pl.when(pl.program_id(0) == 0)
        def _():
            loss_ref[...] = jnp.broadcast_to(lpart, loss_ref.shape)
            dg_ref[...] = dgp

        @pl.when(pl.program_id(0) > 0)
        def _():
            loss_ref[...] += jnp.broadcast_to(lpart, loss_ref.shape)
            dg_ref[...] += dgp

    row = pl.BlockSpec((tr, d), lambda i: (i, 0))
    vec = pl.BlockSpec((1, d), lambda i: (0, 0))
    low = jax.ShapeDtypeStruct((t, d), MXU_DTYPE)
    return pl.pallas_call(
        body, name="head_fwd_bwd", grid=(t // tr,),
        in_specs=[row, row, row, row, vec],
        out_specs=[pl.BlockSpec((8, 128), lambda i: (0, 0)), row, row, row, vec],
        out_shape=[jax.ShapeDtypeStruct((8, 128), F32), jax.ShapeDtypeStruct((t, d), F32), low, low,
                   jax.ShapeDtypeStruct((1, d), F32)],
        compiler_params=_params(("arbitrary",)),
    )(h2, glin, pp, target, g_final)


class _Order:
    def __init__(self):
        self.last = None

    def tie(self, x):
        return x if self.last is None else lax.optimization_barrier((x, self.last))[0]

    def run(self, fn, first, *args, **kwargs):
        out = fn(self.tie(first), *args, **kwargs)
        self.last = out[0] if isinstance(out, (tuple, list)) else out
        return out


def _local_step(x, p, target, vec, ops):
    t = x.shape[0]
    cos_r, sin_r, cos_m, sin_m = _rope_tables(t)
    lg = _ret_log_gamma()
    low = MXU_DTYPE
    run = ops.order.run
    w = ops.weight

    ops.start_gather("w_in", halves=True)
    hn1 = ops.after_payloads(run(_rms_fwd, x, vec["g_attn"], name="rms1_fwd"))
    for n in ("w_uq", "w_ukv", "w_o"):
        ops.start_gather(n, after=hn1)
    half = x.shape[1] // 2
    proj = run(_matmul, hn1[:, :half], w("w_in/0"), tb=True, name="mm_proj_a")
    proj = run(_matmul, hn1[:, half:], w("w_in/1"), tb=True, name="mm_proj_b", add=proj)
    ops.start_gather("w_ffn_gate", after=proj)
    rq, rk, rv = run(_ret_prep, proj, cos_r, sin_r)
    ops.start_gather("w_ffn_up", after=rq)
    c0 = 4 * RET_WIDTH
    cq = proj[:, c0:c0 + Q_LORA]
    ckv = proj[:, c0 + Q_LORA:c0 + Q_LORA + KV_LORA]
    kr_in = proj[:, c0 + Q_LORA + KV_LORA:c0 + Q_LORA + KV_LORA + 128]
    cqn, kvn, kr = run(_mla_prep, cq, ckv, kr_in, vec["g_q_lora"], vec["g_kv_lora"], cos_m, sin_m)
    ops.start_gather("w_ffn_down", after=cqn)
    q_lin = run(_matmul, cqn, w("w_uq"), tb=True, name="mm_q")
    q = run(_mla_q_rope, q_lin, cos_m, sin_m, inverse=False, name="mla_q_rope")
    kv = run(_matmul, kvn, w("w_ukv"), tb=True, name="mm_kv", out_dtype=low)
    mo, lse = run(_mla_fwd, q, kv, kr)
    ops.start_gather("w_ple_gate", after=mo)
    ops.start_gather("w_ple_proj", after=mo)
    ret_raw, ro = run(_ret_fwd, rq, rk, rv, proj, lg)
    attn = jnp.concatenate([ro, mo], axis=1)
    h1 = run(_matmul, attn, w("w_o"), name="mm_o", add=x)
    hn2 = run(_rms_fwd, h1, vec["g_ffn"], name="rms2_fwd")
    gpre = run(_matmul, hn2, w("w_ffn_gate"), tb=True, name="mm_gate", out_dtype=low)
    u, act = run(_ffn_up_act, hn2, w("w_ffn_up"), gpre, vec["conv_w"], vec["conv_b"])
    h2 = run(_matmul, act, w("w_ffn_down"), name="mm_down", add=h1)
    hn3 = run(_rms_fwd, h2, vec["g_ple"], name="rms3_fwd")
    glin = run(_matmul, hn3, w("w_ple_gate"), name="mm_ple_gate")
    p_low = p.astype(low)
    pp = run(_matmul, p_low, w("w_ple_proj"), tb=True, name="mm_ple_proj")
    loss_part, dh3, dglin, dpp, dg_final = run(_head_fwd_bwd, h2, glin, pp, target, vec["g_final"])

    ops.grad("w_ple_proj", dpp, p_low)
    ops.grad("w_ple_gate", hn3, dglin)
    dhn3 = run(_matmul, dglin, w("w_ple_gate"), tb=True, name="mm_dhn3", out_dtype=low)
    dh2, dh2_low, dg_ple = run(_rms_bwd, h2, dhn3, vec["g_ple"], dh3, name="rms3_bwd", low_copy=True)
    ops.reduce_add("w_ple_proj")
    ops.reduce_add("w_ple_gate")
    ops.grad("w_ffn_down", act, dh2_low)
    dact = run(_matmul, dh2_low, w("w_ffn_down"), tb=True, name="mm_dact", out_dtype=low)
    ops.reduce_add("w_ffn_down")
    dgpre, du, dconv_w, dconv_b = run(_ffn_act_bwd, gpre, u, dact, vec["conv_w"], vec["conv_b"])
    ops.update("w_ple_proj")
    ops.update("w_ple_gate")
    ops.grad("w_ffn_gate", dgpre, hn2)
    ops.grad("w_ffn_up", du, hn2)
    ops.reduce_add("w_ffn_gate")
    dhn2 = run(_matmul, dgpre, w("w_ffn_gate"), name="mm_dhn2_a")
    ops.reduce_add("w_ffn_up")
    dhn2 = run(_matmul, du, w("w_ffn_up"), name="mm_dhn2_b", add=dhn2, out_dtype=low)
    dh1, dh1_low, dg_ffn = run(_rms_bwd, h1, dhn2, vec["g_ffn"], dh2, name="rms2_bwd", low_copy=True)
    ops.update("w_ffn_down")
    ops.grad("w_o", attn, dh1_low)
    dattn = run(_matmul, dh1_low, w("w_o"), tb=True, name="mm_dattn")
    ops.reduce_add("w_o")

    dq_r, dk_full, dv = run(_mla_bwd, q, kv, kr, mo, lse, dattn)
    ops.update("w_ffn_gate")
    dq_lin = run(_mla_q_rope, dq_r, cos_m, sin_m, inverse=True, name="mla_q_unrope")
    dkv, dkr = run(_mla_kv_grad, dk_full, dv, cos_m, sin_m)
    ops.grad("w_uq", dq_lin, cqn)
    dcqn = run(_matmul, dq_lin, w("w_uq"), name="mm_dcqn")
    ops.grad("w_ukv", dkv, kvn)
    dkvn = run(_matmul, dkv, w("w_ukv"), name="mm_dkvn")
    dcq, dcq_low, dg_q = run(_rms_bwd, cq, dcqn, vec["g_q_lora"], None, name="rmsq_bwd", low_copy=True)
    dckv, dckv_low, dg_kv = run(_rms_bwd, ckv, dkvn, vec["g_kv_lora"], None, name="rmskv_bwd", low_copy=True)
    ops.reduce_add("w_uq")
    ops.reduce_add("w_ukv")

    do_ret, drg = run(_ret_gate_bwd, ret_raw, proj, dattn)
    dq_ret, dk_ret, drv = run(_ret_bwd, rq, rk, rv, do_ret, lg)
    drq, drk = run(_ret_unrope, dq_ret, dk_ret, cos_r, sin_r)

    pad = jnp.zeros((t, IN_WIDTH_PAD - IN_WIDTH - 64), low)
    dproj = jnp.concatenate([drq, drk, drv, drg, dcq_low, dckv_low, dkr, pad], axis=1)
    ops.grad("w_in", dproj, hn1)
    for n in ("w_ffn_up", "w_o", "w_uq", "w_ukv"):
        ops.update(n)
    ops.reduce_add("w_in")
    dhn1 = jnp.concatenate([run(_matmul, dproj, w(f"w_in/{part}"), name=f"mm_dhn1_{part}", out_dtype=low)
                            for part in range(2)], axis=1)
    grad_x, dg_attn = run(_rms_bwd, x, dhn1, vec["g_attn"], dh1, name="rms1_bwd")

    gs = {"g_attn": dg_attn, "g_q_lora": dg_q, "g_kv_lora": dg_kv, "g_ffn": dg_ffn, "conv_w": dconv_w,
          "conv_b": dconv_b, "g_ple": dg_ple, "g_final": dg_final}
    return loss_part, grad_x, gs


_COL_SHARDED = ("w_in", "w_uq", "w_ukv", "w_ffn_gate", "w_ffn_up", "w_ple_proj")
_FFN_SHARD = D_FF // N_DEV
_FFN_SHARD_PAD = D_FF_PAD // N_DEV
_HEADS_PER_SHARD = MLA_HEADS // N_DEV
_QK = MLA_NOPE + MLA_ROPE


def _pad_rows(name, a):
    lead = a.shape[:-2]
    if name == "w_uq":
        a = a.reshape(lead + (_HEADS_PER_SHARD, _QK, a.shape[-1]))
        a = jnp.pad(a, [(0, 0)] * len(lead) + [(0, 0), (0, MLA_QK_PAD - _QK), (0, 0)])
        return a.reshape(lead + (_HEADS_PER_SHARD * MLA_QK_PAD, a.shape[-1]))
    if name in ("w_ffn_gate", "w_ffn_up", "w_ffn_down"):
        return jnp.pad(a, [(0, 0)] * len(lead) + [(0, _FFN_SHARD_PAD - _FFN_SHARD), (0, 0)])
    return a


def _unpad_rows(name, a):
    lead = a.shape[:-2]
    if name == "w_uq":
        a = a.reshape(lead + (_HEADS_PER_SHARD, MLA_QK_PAD, a.shape[-1]))[..., :_QK, :]
        return a.reshape(lead + (_HEADS_PER_SHARD * _QK, a.shape[-1]))
    if name in ("w_ffn_gate", "w_ffn_up", "w_ffn_down"):
        return a[..., :_FFN_SHARD, :]
    return a


def _rows_view(name, a):
    return jnp.swapaxes(a, 0, 1) if name in _COL_SHARDED else a


def _shard_payload(name, shard):
    return _pad_rows(name, _rows_view(name, shard).astype(MXU_DTYPE))


def _full_from_gathered(name, g):
    full = g.reshape(g.shape[0] * g.shape[1], g.shape[2])
    if name == "w_in":
        full = jnp.pad(full, ((0, IN_WIDTH_PAD - IN_WIDTH), (0, 0)))
    return full


def _grad_chunks(name, gfull):
    if name == "w_in":
        gfull = gfull[:IN_WIDTH]
    return gfull.reshape(N_DEV, gfull.shape[0] // N_DEV, gfull.shape[1])


def _ffn_vec_layout(a):
    a = a.reshape(a.shape[0], N_DEV, _FFN_SHARD)
    return jnp.pad(a, ((0, 0), (0, 0), (0, _FFN_SHARD_PAD - _FFN_SHARD))).reshape(a.shape[0], D_FF_PAD)


def _ffn_vec_shards(a):
    return a.reshape(a.shape[0], N_DEV, _FFN_SHARD_PAD)[:, :, :_FFN_SHARD]


_MESH = pl.DeviceIdType.MESH
_ANY = pl.BlockSpec(memory_space=pl.ANY)


def _place():
    x, y, c = lax.axis_index("x"), lax.axis_index("y"), lax.axis_index("c")
    chips = [(1 - x, y), (x, 1 - y), (1 - x, 1 - y)]
    return x, y, c, chips


def _handshake(peers):
    barrier = pltpu.get_barrier_semaphore()
    for peer in peers:
        pl.semaphore_signal(barrier, inc=1, device_id=peer, device_id_type=_MESH)
    pl.semaphore_wait(barrier, len(peers))


_SEQUENCER = dict(axis_name="seq", num_cores=1)
_AG_COLLECTIVE_ID = 1
_RS_SIBLING_COLLECTIVE_ID = 2
_RS_CHIPS_COLLECTIVE_ID = 3


def _all_gather_seq(shard, *, name):
    def body(x_ref, out_ref, send_sems, recv_sems, local_sem):
        x, y, c, chips = _place()
        sibling = (x, y, 1 - c)
        _handshake([sibling] + [(*chip, c) for chip in chips])

        def slot(px, py, pc):
            return out_ref.at[4 * px + 2 * py + pc]

        def copy(k, block, to, src=None):
            return pltpu.make_async_remote_copy(
                src_ref=slot(*block) if src is None else src, dst_ref=slot(*block),
                send_sem=send_sems.at[k], recv_sem=recv_sems.at[k], device_id=to, device_id_type=_MESH)

        mine = pltpu.make_async_copy(x_ref, slot(x, y, c), local_sem)
        mine.start()
        first = [copy(0, (x, y, c), sibling, src=x_ref)]
        first += [copy(1 + j, (x, y, c), (*chip, c), src=x_ref) for j, chip in enumerate(chips)]
        for cp in first:
            cp.start()
        passed = [copy(4 + j, (*chip, c), sibling) for j, chip in enumerate(chips)]
        for j, chip in enumerate(chips):
            copy(1 + j, (*chip, c), (x, y, c)).wait_recv()
            passed[j].start()
        copy(0, sibling, (x, y, c)).wait_recv()
        for j, chip in enumerate(chips):
            copy(4 + j, (*chip, 1 - c), (x, y, c)).wait_recv()
        for cp in first + passed:
            cp.wait_send()
        mine.wait()

    return pl.kernel(
        body, out_type=jax.ShapeDtypeStruct((N_DEV,) + shard.shape, shard.dtype),
        mesh=plsc.ScalarSubcoreMesh(**_SEQUENCER), name=name,
        scratch_types=[pltpu.SemaphoreType.DMA((7,)), pltpu.SemaphoreType.DMA((7,)), pltpu.SemaphoreType.DMA(())],
        compiler_params=pltpu.CompilerParams(collective_id=_AG_COLLECTIVE_ID),
    )(shard)


def _exchange_sibling(g, *, name):
    all_slots = g.shape[0] == N_DEV

    def body(g_ref, out_ref, send_sems, recv_sems):
        x, y, c, _ = _place()
        sibling = (x, y, 1 - c)
        _handshake([sibling])
        copies = []
        for chip in range(4):
            cp = pltpu.make_async_remote_copy(
                src_ref=g_ref.at[2 * chip + (1 - c) if all_slots else chip], dst_ref=out_ref.at[chip],
                send_sem=send_sems.at[chip], recv_sem=recv_sems.at[chip], device_id=sibling, device_id_type=_MESH)
            cp.start()
            copies.append(cp)
        for cp in copies:
            cp.wait_recv()
        for cp in copies:
            cp.wait_send()

    return pl.kernel(
        body, out_type=jax.ShapeDtypeStruct((4,) + g.shape[1:], g.dtype),
        mesh=plsc.ScalarSubcoreMesh(**_SEQUENCER), name=name,
        scratch_types=[pltpu.SemaphoreType.DMA((4,)), pltpu.SemaphoreType.DMA((4,))],
        compiler_params=pltpu.CompilerParams(collective_id=_RS_SIBLING_COLLECTIVE_ID),
    )(g)


def _add_sibling(g, recv, *, name):
    _, r, cdim = g.shape
    tr, tc = _tile_2d(r, cdim, 6)
    g4 = g.reshape(4, 2, r, cdim)
    core = lax.axis_index("c").astype(jnp.int32).reshape(1)

    def body(c_ref, g_ref, r_ref, o_ref):
        o_ref[...] = (g_ref[...].astype(F32) + r_ref[...].astype(F32)).astype(o_ref.dtype)

    return pl.pallas_call(
        body, name=name,
        grid_spec=pltpu.PrefetchScalarGridSpec(
            num_scalar_prefetch=1, grid=(4, r // tr, cdim // tc),
            in_specs=[pl.BlockSpec((None, None, tr, tc), lambda ch, i, j, c_ref: (ch, c_ref[0], i, j)),
                      pl.BlockSpec((None, tr, tc), lambda ch, i, j, c_ref: (ch, i, j))],
            out_specs=pl.BlockSpec((None, tr, tc), lambda ch, i, j, c_ref: (ch, i, j))),
        out_shape=jax.ShapeDtypeStruct((4, r, cdim), g.dtype),
        compiler_params=_params(("parallel", "parallel", "parallel")),
    )(core, g4, recv)


def _exchange_chips(pch, *, name):
    def body(p_ref, out_ref, send_sems, recv_sems, local_sem):
        x, y, c, chips = _place()
        _handshake([(*chip, c) for chip in chips])
        me = 2 * x + y
        mine = pltpu.make_async_copy(p_ref.at[me], out_ref.at[me], local_sem)
        mine.start()
        copies = []
        for j, (px, py) in enumerate(chips):
            cp = pltpu.make_async_remote_copy(
                src_ref=p_ref.at[2 * px + py], dst_ref=out_ref.at[me],
                send_sem=send_sems.at[j], recv_sem=recv_sems.at[j], device_id=(px, py, c), device_id_type=_MESH)
            cp.start()
            copies.append(cp)
        for j, (px, py) in enumerate(chips):
            pltpu.make_async_remote_copy(
                src_ref=p_ref.at[me], dst_ref=out_ref.at[2 * px + py],
                send_sem=send_sems.at[j], recv_sem=recv_sems.at[j], device_id=(px, py, c), device_id_type=_MESH).wait_recv()
        for cp in copies:
            cp.wait_send()
        mine.wait()

    return pl.kernel(
        body, out_type=jax.ShapeDtypeStruct(pch.shape, pch.dtype),
        mesh=plsc.ScalarSubcoreMesh(**_SEQUENCER), name=name,
        scratch_types=[pltpu.SemaphoreType.DMA((3,)), pltpu.SemaphoreType.DMA((3,)), pltpu.SemaphoreType.DMA(())],
        compiler_params=pltpu.CompilerParams(collective_id=_RS_CHIPS_COLLECTIVE_ID),
    )(pch)


def _all_reduce_small(v, *, name):
    r = v.shape[0]

    def body(x_ref, out_ref, buf_ref, send_sems, recv_sems):
        x, y, c, chips = _place()
        sibling = (x, y, 1 - c)

        def slot(px, py, pc):
            return buf_ref.at[4 * px + 2 * py + pc]

        def copy(k, block, to, src=None):
            return pltpu.make_async_remote_copy(
                src_ref=slot(*block) if src is None else src, dst_ref=slot(*block),
                send_sem=send_sems.at[k], recv_sem=recv_sems.at[k], device_id=to, device_id_type=_MESH)

        first = [copy(0, (x, y, c), sibling, src=x_ref)]
        first += [copy(1 + j, (x, y, c), (*chip, c), src=x_ref) for j, chip in enumerate(chips)]
        for cp in first:
            cp.start()
        buf_ref[4 * x + 2 * y + c] = x_ref[...]
        passed = [copy(4 + j, (*chip, c), sibling) for j, chip in enumerate(chips)]
        for j, chip in enumerate(chips):
            copy(1 + j, (*chip, c), (x, y, c)).wait_recv()
            passed[j].start()
        copy(0, sibling, (x, y, c)).wait_recv()
        for j, chip in enumerate(chips):
            copy(4 + j, (*chip, 1 - c), (x, y, c)).wait_recv()
        for cp in first + passed:
            cp.wait_send()
        total = buf_ref[0]
        for k in range(1, N_DEV):
            total = total + buf_ref[k]
        out_ref[...] = total

    vm = pl.BlockSpec(memory_space=pltpu.VMEM)
    return pl.pallas_call(
        body, name=name, out_shape=jax.ShapeDtypeStruct(v.shape, v.dtype),
        in_specs=[vm], out_specs=vm,
        scratch_shapes=[pltpu.VMEM((N_DEV,) + v.shape, v.dtype), pltpu.SemaphoreType.DMA((7,)),
                        pltpu.SemaphoreType.DMA((7,))],
    )(v)


_ELEMENTWISE_VMEM = 24 * 1024 * 1024


def _tile_2d(r, c, n_arrays):
    per_block = _ELEMENTWISE_VMEM // (8 * n_arrays)
    tr = _tile(r, max(16, per_block // max(c, 128)), 16)
    by_rows = (tr, c) if tr * c <= per_block else None
    tc = _tile(c, max(128, (per_block // r) // 128 * 128))
    by_cols = (r, tc) if r * tc <= per_block else None
    if by_rows is None or (by_cols is not None and r * tc > tr * c):
        assert by_cols is not None, (r, c, n_arrays)
        return by_cols
    return by_rows


def _adam_math(w, g, m, v):
    m = ADAM_B1 * m + (1.0 - ADAM_B1) * g
    v = ADAM_B2 * v + (1.0 - ADAM_B2) * jnp.square(g)
    m_hat = m / (1.0 - ADAM_B1 ** ADAM_STEP)
    v_hat = v / (1.0 - ADAM_B2 ** ADAM_STEP)
    delta = -ADAM_LR * (m_hat / (jnp.sqrt(v_hat) + ADAM_EPS) + ADAM_WD * w)
    return delta, m, v


def _adam(w, g, m, v, *, name, parts=None):
    r, cdim = w.shape
    tr, tc = _tile_2d(r, cdim, 8)

    def body(w_ref, g_ref, m_ref, v_ref, go_ref, d_ref, mo_ref, vo_ref):
        if parts is None:
            g = g_ref[...]
        else:
            g = g_ref[0].astype(F32)
            for k in range(1, parts):
                g = g + g_ref[k].astype(F32)
        delta, m, v = _adam_math(w_ref[...], g, m_ref[...], v_ref[...])
        go_ref[...] = g
        d_ref[...] = delta
        mo_ref[...] = m
        vo_ref[...] = v

    blk = pl.BlockSpec((tr, tc), lambda i, j: (i, j))
    gblk = blk if parts is None else pl.BlockSpec((parts, tr, tc), lambda i, j: (0, i, j))
    out = jax.ShapeDtypeStruct((r, cdim), F32)
    return pl.pallas_call(
        body, name=name, grid=(r // tr, cdim // tc), in_specs=[blk, gblk, blk, blk], out_specs=[blk] * 4,
        out_shape=[out] * 4, compiler_params=_params(("parallel", "parallel")),
    )(w, g, m, v)


_BIG = ("w_in", "w_uq", "w_ukv", "w_o", "w_ffn_gate", "w_ffn_up", "w_ffn_down", "w_ple_gate", "w_ple_proj")
_WEIGHTS = ("w_in", "g_attn", "g_q_lora", "g_kv_lora", "w_uq", "w_ukv", "w_o", "g_ffn", "w_ffn_gate", "w_ffn_up",
            "conv_w", "conv_b", "w_ffn_down", "g_ple", "w_ple_gate", "w_ple_proj", "g_final")
_SMALL_PACK = (("g_attn", 1, D_MODEL), ("g_q_lora", 1, Q_LORA), ("g_kv_lora", 1, KV_LORA), ("g_ffn", 1, D_MODEL),
               ("conv_w", CONV_WIDTH, D_FF_PAD), ("conv_b", 1, D_FF_PAD), ("g_ple", 1, D_MODEL), ("g_final", 1, D_MODEL))


def _pack_small(gs):
    flat = jnp.concatenate([gs[n].reshape(-1) for n, _, _ in _SMALL_PACK])
    rows = -(-flat.shape[0] // 128)
    rows = -(-rows // 8) * 8
    return jnp.pad(flat, (0, rows * 128 - flat.shape[0])).reshape(rows, 128)


def _unpack_small(packed):
    flat = packed.reshape(-1)
    out, off = {}, 0
    for n, r, c in _SMALL_PACK:
        out[n] = flat[off:off + r * c].reshape(r, c)
        off += r * c
    return out


def kernel(x, p, w_in, g_attn, g_q_lora, g_kv_lora, w_uq, w_ukv, w_o, g_ffn, w_ffn_gate, w_ffn_up, conv_w, conv_b, w_ffn_down, g_ple, w_ple_gate, w_ple_proj, g_final, loss_target, m_w_in, m_g_attn, m_g_q_lora, m_g_kv_lora, m_w_uq, m_w_ukv, m_w_o, m_g_ffn, m_w_ffn_gate, m_w_ffn_up, m_conv_w, m_conv_b, m_w_ffn_down, m_g_ple, m_w_ple_gate, m_w_ple_proj, m_g_final, v_w_in, v_g_attn, v_g_q_lora, v_g_kv_lora, v_w_uq, v_w_ukv, v_w_o, v_g_ffn, v_w_ffn_gate, v_w_ffn_up, v_conv_w, v_conv_b, v_w_ffn_down, v_g_ple, v_w_ple_gate, v_w_ple_proj, v_g_final):
    given = dict(locals())
    wts = {n: given[n] for n in _WEIGHTS}
    mom = {n: given["m_" + n] for n in _WEIGHTS}
    var = {n: given["v_" + n] for n in _WEIGHTS}
    me = (4 * lax.axis_index("x") + 2 * lax.axis_index("y") + lax.axis_index("c")).astype(jnp.int32)
    ops = _ShardedWeights(wts, mom, var)

    conv_full = _all_gather_seq(conv_w, name="ag_conv_w")[:, 0].transpose(1, 0, 2).reshape(CONV_WIDTH, D_FF)
    vec = {"g_attn": g_attn, "g_q_lora": g_q_lora, "g_kv_lora": g_kv_lora, "g_ffn": g_ffn, "g_ple": g_ple,
           "g_final": g_final[None, :], "conv_w": _ffn_vec_layout(conv_full), "conv_b": _ffn_vec_layout(conv_b)}

    loss_part, grad_x, gs = _local_step(x[0], p[0, 0], loss_target[0], vec, ops)
    loss = lax.psum(loss_part[0, 0], ("x", "y", "c"))

    small = _unpack_small(_all_reduce_small(ops.order.tie(_pack_small(gs)), name="ar_small"))
    conv_w_shards = _ffn_vec_shards(small["conv_w"])
    small_g = {
        "g_attn": small["g_attn"], "g_q_lora": small["g_q_lora"], "g_kv_lora": small["g_kv_lora"],
        "g_ffn": small["g_ffn"], "g_ple": small["g_ple"], "g_final": small["g_final"],
        "conv_b": _ffn_vec_shards(small["conv_b"]).reshape(1, D_FF),
        "conv_w": lax.dynamic_index_in_dim(conv_w_shards, me, axis=1, keepdims=False),
    }
    results = dict(ops.results)
    for n, g in small_g.items():
        shape = wts[n].shape
        outs = ops.order.run(_adam, wts[n].reshape(g.shape), g, mom[n].reshape(g.shape), var[n].reshape(g.shape),
                             name="adam_" + n)
        results[n] = tuple(a.reshape(shape) for a in outs)
    ops.update("w_in")
    results["w_in"] = ops.results["w_in"]

    return (loss, grad_x[None], *[results[n][0] for n in _WEIGHTS], *[results[n][1] for n in _WEIGHTS],
            *[results[n][2] for n in _WEIGHTS], *[results[n][3] for n in _WEIGHTS])


class _ShardedWeights:
    def __init__(self, wts, mom, var):
        self.wts, self.mom, self.var = wts, mom, var
        self.order = _Order()
        self.full, self.stage, self.results = {}, {}, {}
        self.payloads = {n: _shard_payload(n, wts[n][0]) for n in _BIG}

    def after_payloads(self, x):
        names = [n for n in _BIG if n != "w_in"]
        tied = lax.optimization_barrier((x, *[self.payloads[n] for n in names]))
        self.payloads.update(zip(names, tied[1:]))
        return tied[0]

    def start_gather(self, name, after=None, halves=False):
        payload = self.payloads[name]
        if after is not None:
            payload = lax.optimization_barrier((payload, after))[0]
        if not halves:
            self.full[name] = _full_from_gathered(name, _all_gather_seq(payload, name="ag_" + name))
            return
        half = payload.shape[1] // 2
        for part in range(2):
            piece = payload[:, part * half:(part + 1) * half]
            self.full[f"{name}/{part}"] = _full_from_gathered(name, _all_gather_seq(piece, name=f"ag_{name}_{part}"))

    def weight(self, name):
        return self.full[name]

    def grad(self, name, a, b):
        core = lax.axis_index("c")
        if name == "w_in":
            chunks = _grad_chunks(name, self.order.run(_matmul, a, b, ta=True, name="mm_d_" + name,
                                                        out_dtype=MXU_DTYPE))
            self.stage[name] = (chunks, None, _exchange_sibling(chunks, name="rs_sib_" + name))
            return
        theirs = self.order.run(_grad_matmul_half, a, b, 1 - core, name="mm_d_" + name + "_sib")
        self.stage[name] = (a, b, _exchange_sibling(theirs, name="rs_sib_" + name))

    def reduce_add(self, name):
        a, b, from_sibling = self.stage[name]
        if b is None:
            per_chip = self.order.run(_add_sibling, a, from_sibling, name="rs_add_" + name)
        else:
            per_chip = self.order.run(_grad_matmul_half, a, b, lax.axis_index("c"), name="mm_d_" + name + "_own",
                                      add=from_sibling)
        self.stage[name] = _exchange_chips(per_chip, name="rs_chip_" + name)

    def update(self, name):
        parts = self.stage[name]
        if name == "w_uq":
            parts = _unpad_rows(name, parts)
        rows = lambda a: _rows_view(name, a[0])
        outs = self.order.run(_adam, rows(self.wts[name]), parts, rows(self.mom[name]), rows(self.var[name]),
                              name="adam_" + name, parts=4)
        self.results[name] = tuple(_rows_view(name, a)[None] for a in outs)
```

```python
import functools

import numpy as np

import jax
import jax.numpy as jnp
from jax import lax
from jax.experimental import pallas as pl
from jax.experimental.pallas import tpu as pltpu
from jax.experimental.pallas import tpu_sc as plsc

D_MODEL = 4096
CHUNK = 64
PLE_DIM = 256
RET_HEADS = 8
RET_HEAD_DIM = 256
RET_WIDTH = 2048
MLA_HEADS = 16
MLA_NOPE = 128
MLA_ROPE = 64
MLA_V = 128
Q_LORA = 1024
KV_LORA = 512
D_FF = 11008
CONV_WIDTH = 3
ROPE_BASE = 10000.0
EPS = 1e-6
IN_WIDTH = 9792
ADAM_LR, ADAM_B1, ADAM_B2, ADAM_EPS, ADAM_WD, ADAM_STEP = 0.001, 0.9, 0.999, 1e-08, 0.01, 10

IN_WIDTH_PAD = 10240
D_FF_PAD = 11264
MLA_QK_PAD = 256
Q_WIDTH_PAD = MLA_HEADS * MLA_QK_PAD

N_DEV = 8
MXU_DTYPE = jnp.bfloat16
ATTN_BLOCK = 512
HEADS_PER_STEP = 4
FWD_HEADS_PER_STEP = 8
VMEM_LIMIT = 56 * 1024 * 1024

F32 = jnp.float32


def _tile(n, want, align=128):
    if n <= want:
        return n
    t = (want // align) * align
    while t >= align:
        if n % t == 0:
            return t
        t -= align
    return n


def _params(sem):
    return pltpu.CompilerParams(dimension_semantics=sem, vmem_limit_bytes=VMEM_LIMIT)


def _sigmoid(x):
    return 1.0 / (1.0 + jnp.exp(-x))


def _matmul(a, b, *, name, ta=False, tb=False, out_dtype=F32, add=None, tm=1024, tn=1024, tk=4096):
    m, k = (a.shape[1], a.shape[0]) if ta else a.shape
    k2, n = (b.shape[1], b.shape[0]) if tb else b.shape
    assert k == k2, (a.shape, b.shape, ta, tb)
    tm, tn, tk = _tile(m, tm), _tile(n, tn), _tile(k, tk)
    nk = k // tk
    dims = (((0 if ta else 1,), (1 if tb else 0,)), ((), ()))

    def body(*refs):
        a_ref, b_ref, o_ref = refs[0], refs[1], refs[3 if add is not None else 2]
        c_ref = refs[2] if add is not None else None
        part = lax.dot_general(a_ref[...].astype(MXU_DTYPE), b_ref[...].astype(MXU_DTYPE), dims,
                               preferred_element_type=F32)

        def finish(r):
            if c_ref is not None:
                r = r + c_ref[...].astype(F32)
            o_ref[...] = r.astype(out_dtype)

        if nk == 1:
            finish(part)
            return
        acc_ref = refs[-1]
        kk = pl.program_id(2)

        @pl.when(kk == 0)
        def _():
            acc_ref[...] = part

        @pl.when((kk > 0) & (kk < nk - 1))
        def _():
            acc_ref[...] += part

        @pl.when(kk == nk - 1)
        def _():
            finish(acc_ref[...] + part)

    a_spec = pl.BlockSpec((tk, tm), lambda i, j, kk: (kk, i)) if ta else pl.BlockSpec((tm, tk), lambda i, j, kk: (i, kk))
    b_spec = pl.BlockSpec((tn, tk), lambda i, j, kk: (j, kk)) if tb else pl.BlockSpec((tk, tn), lambda i, j, kk: (kk, j))
    in_specs = [a_spec, b_spec]
    args = [a, b]
    if add is not None:
        in_specs.append(pl.BlockSpec((tm, tn), lambda i, j, kk: (i, j)))
        args.append(add)
    return pl.pallas_call(
        body, name=name, grid=(m // tm, n // tn, nk),
        in_specs=in_specs, out_specs=pl.BlockSpec((tm, tn), lambda i, j, kk: (i, j)),
        out_shape=jax.ShapeDtypeStruct((m, n), out_dtype),
        scratch_shapes=[] if nk == 1 else [pltpu.VMEM((tm, tn), F32)],
        compiler_params=_params(("parallel", "parallel", "arbitrary")),
    )(*args)


def _grad_matmul_half(a, b, core_class, *, name, add=None):
    t, m = a.shape
    r = m // N_DEV
    cdim = b.shape[1]
    tn = _tile(cdim, 512 if r > 1024 else 1024)
    cls = jnp.asarray(core_class, jnp.int32).reshape(1)

    def body(cls_ref, a_ref, b_ref, *rest):
        part = lax.dot_general(a_ref[...].astype(MXU_DTYPE), b_ref[...].astype(MXU_DTYPE), _TN,
                               preferred_element_type=F32)
        if add is not None:
            part = part + rest[0][...].astype(F32)
        rest[-1][...] = part.astype(rest[-1].dtype)

    slot = pl.BlockSpec((None, r, tn), lambda q, j, cls_ref: (q, 0, j))
    in_specs = [pl.BlockSpec((t, r), lambda q, j, cls_ref: (0, 2 * q + cls_ref[0])),
                pl.BlockSpec((t, tn), lambda q, j, cls_ref: (0, j))] + ([slot] if add is not None else [])
    args = [a, b] + ([add] if add is not None else [])
    return pl.pallas_call(
        body, name=name,
        grid_spec=pltpu.PrefetchScalarGridSpec(
            num_scalar_prefetch=1, grid=(N_DEV // 2, cdim // tn), in_specs=in_specs, out_specs=slot),
        out_shape=jax.ShapeDtypeStruct((N_DEV // 2, r, cdim), MXU_DTYPE),
        compiler_params=_params(("parallel", "parallel")),
    )(cls, *args)


def _rms_fwd(x, g, *, name):
    t, d = x.shape
    tr = _tile(t, 256, 8)

    def body(x_ref, g_ref, o_ref):
        xf = x_ref[...]
        r = lax.rsqrt(jnp.mean(xf * xf, axis=-1, keepdims=True) + EPS)
        o_ref[...] = (xf * r * g_ref[...]).astype(o_ref.dtype)

    return pl.pallas_call(
        body, name=name, grid=(t // tr,),
        in_specs=[pl.BlockSpec((tr, d), lambda i: (i, 0)), pl.BlockSpec((1, d), lambda i: (0, 0))],
        out_specs=pl.BlockSpec((tr, d), lambda i: (i, 0)),
        out_shape=jax.ShapeDtypeStruct((t, d), MXU_DTYPE),
        compiler_params=_params(("parallel",)),
    )(x, g)


def _rms_bwd(x, dhn, g, res, *, name, low_copy=False):
    t, d = x.shape
    tr = _tile(t, 256, 8)

    def body(*refs):
        if res is None:
            x_ref, dh_ref, g_ref = refs[:3]
            outs = refs[3:]
            res_ref = None
        else:
            x_ref, dh_ref, g_ref, res_ref = refs[:4]
            outs = refs[4:]
        dx_ref, dg_ref = outs[0], outs[-1]
        xf = x_ref[...]
        dh = dh_ref[...].astype(F32)
        r = lax.rsqrt(jnp.mean(xf * xf, axis=-1, keepdims=True) + EPS)
        dyg = dh * g_ref[...]
        dx = r * dyg - xf * (r * r * r) * jnp.mean(dyg * xf, axis=-1, keepdims=True)
        if res_ref is not None:
            dx = dx + res_ref[...]
        dx_ref[...] = dx
        if low_copy:
            outs[1][...] = dx.astype(outs[1].dtype)
        part = jnp.sum(dh * xf * r, axis=0, keepdims=True)

        @pl.when(pl.program_id(0) == 0)
        def _():
            dg_ref[...] = part

        @pl.when(pl.program_id(0) > 0)
        def _():
            dg_ref[...] += part

    row = pl.BlockSpec((tr, d), lambda i: (i, 0))
    vec = pl.BlockSpec((1, d), lambda i: (0, 0))
    in_specs = [row, row, vec] + ([] if res is None else [row])
    args = [x, dhn, g] + ([] if res is None else [res])
    out_specs = [row] + ([row] if low_copy else []) + [vec]
    out_shape = [jax.ShapeDtypeStruct((t, d), F32)] + ([jax.ShapeDtypeStruct((t, d), MXU_DTYPE)] if low_copy else []) \
        + [jax.ShapeDtypeStruct((1, d), F32)]
    return pl.pallas_call(
        body, name=name, grid=(t // tr,), in_specs=in_specs, out_specs=out_specs, out_shape=out_shape,
        compiler_params=_params(("arbitrary",)),
    )(*args)


def _rope_tables(t):
    pos = jnp.arange(t, dtype=F32)[:, None]
    inv_r = 1.0 / (ROPE_BASE ** (jnp.arange(0, RET_HEAD_DIM, 2, dtype=F32) / RET_HEAD_DIM))
    ang_r = pos * inv_r[None, :]
    inv_m = 1.0 / (ROPE_BASE ** (jnp.arange(0, MLA_ROPE, 2, dtype=F32) / MLA_ROPE))
    ang_m = pos * inv_m[None, :]
    cm, sm = jnp.cos(ang_m), jnp.sin(ang_m)
    z = jnp.zeros_like(cm)
    cos_m = jnp.concatenate([cm, cm, z, z], axis=1)
    sin_m = jnp.concatenate([-sm, sm, z, z], axis=1)
    return jnp.cos(ang_r), jnp.sin(ang_r), cos_m, sin_m


def _rope256(x, c, s, inverse=False):
    x1, x2 = x[:, :128], x[:, 128:]
    if inverse:
        s = -s
    return jnp.concatenate([x1 * c - x2 * s, x2 * c + x1 * s], axis=1)


def _rope64(x, cos_m, sin_m, inverse=False):
    lane = lax.broadcasted_iota(jnp.int32, x.shape, 1)
    partner = jnp.where(lane < 32, pltpu.roll(x, 96, 1), pltpu.roll(x, 32, 1))
    s = -sin_m if inverse else sin_m
    return x * cos_m + partner * s


def _ret_prep(proj, cos_r, sin_r):
    t = proj.shape[0]
    tr = _tile(t, 256, 8)

    def body(q_ref, k_ref, v_ref, c_ref, s_ref, qo_ref, ko_ref, vo_ref):
        c, s = c_ref[...], s_ref[...]
        for h in range(RET_HEADS):
            cols = slice(h * RET_HEAD_DIM, (h + 1) * RET_HEAD_DIM)
            qo_ref[:, cols] = _rope256(q_ref[:, cols], c, s).astype(qo_ref.dtype)
            ko_ref[:, cols] = (_rope256(k_ref[:, cols], c, s) * (RET_HEAD_DIM ** -0.5)).astype(ko_ref.dtype)
        vo_ref[...] = v_ref[...].astype(vo_ref.dtype)

    group = lambda off: pl.BlockSpec((tr, RET_WIDTH), lambda i: (i, off))
    tab = pl.BlockSpec((tr, 128), lambda i: (i, 0))
    out = jax.ShapeDtypeStruct((t, RET_WIDTH), MXU_DTYPE)
    return pl.pallas_call(
        body, name="ret_prep", grid=(t // tr,),
        in_specs=[group(0), group(1), group(2), tab, tab],
        out_specs=[group(0), group(0), group(0)], out_shape=[out, out, out],
        compiler_params=_params(("parallel",)),
    )(proj, proj, proj, cos_r, sin_r)


def _ret_log_gamma():
    return jnp.asarray(np.log1p(-np.exp2(-5.0 - np.arange(RET_HEADS, dtype=np.float64))), dtype=F32)


def _decay_full(lg, i, j, blk):
    r = lax.broadcasted_iota(jnp.int32, (blk, 1), 0).astype(F32)
    c = lax.broadcasted_iota(jnp.int32, (1, blk), 1).astype(F32)
    off = ((i - j) * blk).astype(F32)
    return jnp.exp(lg * r), jnp.exp(lg * (off - c))


def _decay_diag(lg, blk):
    r = lax.broadcasted_iota(jnp.int32, (blk, blk), 0)
    c = lax.broadcasted_iota(jnp.int32, (blk, blk), 1)
    ok = (c // CHUNK) <= (r // CHUNK)
    return jnp.where(ok, jnp.exp(lg * jnp.abs(r - c).astype(F32)), 0.0)


_NT = (((1,), (1,)), ((), ()))
_TN = (((0,), (0,)), ((), ()))
_NN = (((1,), (0,)), ((), ()))


def _causal_pairs(nb, query_major):
    if query_major:
        pairs = [(i, j) for i in range(nb) for j in range(i + 1)]
    else:
        pairs = [(i, j) for j in range(nb) for i in range(j, nb)]
    arr = np.asarray(pairs, dtype=np.int32)
    return jnp.asarray(arr[:, 0]), jnp.asarray(arr[:, 1])


def _ret_fwd(q, k, v, proj, lg):
    t = q.shape[0]
    blk = _tile(t, ATTN_BLOCK)
    nb = t // blk
    hps, d = FWD_HEADS_PER_STEP, RET_HEAD_DIM
    gate_off = 3 * RET_WIDTH // (hps * d)

    def body(ii_ref, jj_ref, lg_ref, q_ref, k_ref, v_ref, g_ref, raw_ref, ro_ref, acc_ref):
        hg, pair = pl.program_id(0), pl.program_id(1)
        i, j = ii_ref[pair], jj_ref[pair]

        @pl.when(j == 0)
        def _():
            acc_ref[...] = jnp.zeros_like(acc_ref)

        def step(diag):
            for h in range(hps):
                cols = slice(h * d, (h + 1) * d)
                lgh = lg_ref[hg * hps + h]
                s = lax.dot_general(q_ref[:, cols], k_ref[:, cols], _NT, preferred_element_type=F32)
                if diag:
                    w = s * _decay_diag(lgh, blk)
                else:
                    a, b = _decay_full(lgh, i, j, blk)
                    w = s * a * b
                acc_ref[h] += lax.dot_general(w.astype(MXU_DTYPE), v_ref[:, cols], _NN, preferred_element_type=F32)

        @pl.when(j < i)
        def _():
            step(False)

        @pl.when(j == i)
        def _():
            step(True)
            for h in range(hps):
                cols = slice(h * d, (h + 1) * d)
                o = acc_ref[h]
                raw_ref[:, cols] = o
                mu = jnp.mean(o, axis=-1, keepdims=True)
                var = jnp.mean(jnp.square(o - mu), axis=-1, keepdims=True)
                hn = (o - mu) * lax.rsqrt(var + EPS)
                g = g_ref[:, cols]
                ro_ref[:, cols] = (g * _sigmoid(g) * hn).astype(ro_ref.dtype)

    qs = pl.BlockSpec((blk, hps * d), lambda h, p, ii, jj: (ii[p], h))
    ks = pl.BlockSpec((blk, hps * d), lambda h, p, ii, jj: (jj[p], h))
    gs = pl.BlockSpec((blk, hps * d), lambda h, p, ii, jj: (ii[p], h + gate_off))
    ii, jj = _causal_pairs(nb, query_major=True)
    return pl.pallas_call(
        body, name="ret_fwd",
        grid_spec=pltpu.PrefetchScalarGridSpec(
            num_scalar_prefetch=2, grid=(RET_HEADS // hps, ii.shape[0]),
            in_specs=[pl.BlockSpec(memory_space=pltpu.SMEM), qs, ks, ks, gs], out_specs=[qs, qs],
            scratch_shapes=[pltpu.VMEM((hps, blk, d), F32)]),
        out_shape=[jax.ShapeDtypeStruct((t, RET_WIDTH), F32), jax.ShapeDtypeStruct((t, RET_WIDTH), MXU_DTYPE)],
        compiler_params=_params(("parallel", "arbitrary")),
    )(ii, jj, lg, q, k, v, proj)


def _ret_gate_bwd(raw, proj, dattn):
    t = raw.shape[0]
    tr = _tile(t, 256, 8)

    def body(o_ref, g_ref, d_ref, do_ref, dg_ref):
        for h in range(RET_HEADS):
            cols = slice(h * RET_HEAD_DIM, (h + 1) * RET_HEAD_DIM)
            o, g, d = o_ref[:, cols], g_ref[:, cols], d_ref[:, cols]
            mu = jnp.mean(o, axis=-1, keepdims=True)
            rstd = lax.rsqrt(jnp.mean(jnp.square(o - mu), axis=-1, keepdims=True) + EPS)
            hn = (o - mu) * rstd
            sg = _sigmoid(g)
            dg_ref[:, cols] = (d * hn * (sg * (1.0 + g * (1.0 - sg)))).astype(dg_ref.dtype)
            dhn = d * (g * sg)
            do = rstd * (dhn - jnp.mean(dhn, axis=-1, keepdims=True)
                         - hn * jnp.mean(dhn * hn, axis=-1, keepdims=True))
            do_ref[:, cols] = do.astype(do_ref.dtype)

    group = lambda off: pl.BlockSpec((tr, RET_WIDTH), lambda i: (i, off))
    out = jax.ShapeDtypeStruct((t, RET_WIDTH), MXU_DTYPE)
    return pl.pallas_call(
        body, name="ret_gate_bwd", grid=(t // tr,),
        in_specs=[group(0), group(3), group(0)], out_specs=[group(0), group(0)], out_shape=[out, out],
        compiler_params=_params(("parallel",)),
    )(raw, proj, dattn)


def _ret_bwd(q, k, v, do, lg):
    t = q.shape[0]
    blk = _tile(t, ATTN_BLOCK)
    nb = t // blk

    hps, d = HEADS_PER_STEP, RET_HEAD_DIM

    def body(ii_ref, jj_ref, lg_ref, q_ref, k_ref, v_ref, do_ref, dq_ref, dk_ref, dv_ref, dk_acc, dv_acc):
        hg, pair = pl.program_id(0), pl.program_id(1)
        i, j = ii_ref[pair], jj_ref[pair]

        @pl.when(pair == 0)
        def _():
            dq_ref[...] = jnp.zeros_like(dq_ref)

        @pl.when(i == j)
        def _():
            dk_acc[...] = jnp.zeros_like(dk_acc)
            dv_acc[...] = jnp.zeros_like(dv_acc)

        def step(diag):
            rows = pl.ds(pl.multiple_of(i * blk, blk), blk)
            for h in range(hps):
                cols = slice(h * d, (h + 1) * d)
                lgh = lg_ref[hg * hps + h]
                if diag:
                    decay = _decay_diag(lgh, blk)
                else:
                    a, b = _decay_full(lgh, i, j, blk)
                    decay = a * b
                qb, kb, vb, dob = q_ref[:, cols], k_ref[:, cols], v_ref[:, cols], do_ref[:, cols]
                s = lax.dot_general(qb, kb, _NT, preferred_element_type=F32)
                w = (s * decay).astype(MXU_DTYPE)
                dv_acc[h] += lax.dot_general(w, dob, _TN, preferred_element_type=F32)
                dw = lax.dot_general(dob, vb, _NT, preferred_element_type=F32)
                ds = (dw * decay).astype(MXU_DTYPE)
                dq_ref[rows, cols] += lax.dot_general(ds, kb, _NN, preferred_element_type=F32)
                dk_acc[h] += lax.dot_general(ds, qb, _TN, preferred_element_type=F32)

        @pl.when(i > j)
        def _():
            step(False)

        @pl.when(i == j)
        def _():
            step(True)

        @pl.when(i == nb - 1)
        def _():
            for h in range(hps):
                cols = slice(h * d, (h + 1) * d)
                dk_ref[:, cols] = dk_acc[h]
                dv_ref[:, cols] = dv_acc[h].astype(dv_ref.dtype)

    qs = pl.BlockSpec((blk, hps * d), lambda h, p, ii, jj: (ii[p], h))
    ks = pl.BlockSpec((blk, hps * d), lambda h, p, ii, jj: (jj[p], h))
    ii, jj = _causal_pairs(nb, query_major=False)
    return pl.pallas_call(
        body, name="ret_bwd",
        grid_spec=pltpu.PrefetchScalarGridSpec(
            num_scalar_prefetch=2, grid=(RET_HEADS // hps, ii.shape[0]),
            in_specs=[pl.BlockSpec(memory_space=pltpu.SMEM), qs, ks, ks, qs],
            out_specs=[pl.BlockSpec((t, hps * d), lambda h, p, ii, jj: (0, h), pipeline_mode=pl.Buffered(1)), ks, ks],
            scratch_shapes=[pltpu.VMEM((hps, blk, d), F32), pltpu.VMEM((hps, blk, d), F32)]),
        out_shape=[jax.ShapeDtypeStruct((t, RET_WIDTH), F32), jax.ShapeDtypeStruct((t, RET_WIDTH), F32),
                   jax.ShapeDtypeStruct((t, RET_WIDTH), MXU_DTYPE)],
        compiler_params=_params(("parallel", "arbitrary")),
    )(ii, jj, lg, q, k, v, do)


def _ret_unrope(dq, dk, cos_r, sin_r):
    t = dq.shape[0]
    tr = _tile(t, 256, 8)

    def body(dq_ref, dk_ref, c_ref, s_ref, oq_ref, ok_ref):
        c, s = c_ref[...], s_ref[...]
        for h in range(RET_HEADS):
            cols = slice(h * RET_HEAD_DIM, (h + 1) * RET_HEAD_DIM)
            oq_ref[:, cols] = _rope256(dq_ref[:, cols], c, s, inverse=True).astype(oq_ref.dtype)
            ok_ref[:, cols] = (_rope256(dk_ref[:, cols], c, s, inverse=True)
                               * (RET_HEAD_DIM ** -0.5)).astype(ok_ref.dtype)

    rows = pl.BlockSpec((tr, RET_WIDTH), lambda i: (i, 0))
    tab = pl.BlockSpec((tr, 128), lambda i: (i, 0))
    out = jax.ShapeDtypeStruct((t, RET_WIDTH), MXU_DTYPE)
    return pl.pallas_call(
        body, name="ret_unrope", grid=(t // tr,),
        in_specs=[rows, rows, tab, tab], out_specs=[rows, rows], out_shape=[out, out],
        compiler_params=_params(("parallel",)),
    )(dq, dk, cos_r, sin_r)


def _mla_prep(cq, ckv, kr, g_q, g_kv, cos_m, sin_m):
    t = cq.shape[0]
    tr = _tile(t, 512, 8)

    def body(cq_ref, ckv_ref, kr_ref, gq_ref, gkv_ref, c_ref, s_ref, cqn_ref, kvn_ref, kro_ref):
        for x_ref, g_ref, o_ref in ((cq_ref, gq_ref, cqn_ref), (ckv_ref, gkv_ref, kvn_ref)):
            xf = x_ref[...]
            r = lax.rsqrt(jnp.mean(xf * xf, axis=-1, keepdims=True) + EPS)
            o_ref[...] = (xf * r * g_ref[...]).astype(o_ref.dtype)
        kro_ref[...] = _rope64(kr_ref[...], c_ref[...], s_ref[...]).astype(kro_ref.dtype)

    row = lambda w: pl.BlockSpec((tr, w), lambda i: (i, 0))
    vec = lambda w: pl.BlockSpec((1, w), lambda i: (0, 0))
    return pl.pallas_call(
        body, name="mla_prep", grid=(t // tr,),
        in_specs=[row(Q_LORA), row(KV_LORA), row(128), vec(Q_LORA), vec(KV_LORA), row(128), row(128)],
        out_specs=[row(Q_LORA), row(KV_LORA), row(128)],
        out_shape=[jax.ShapeDtypeStruct((t, Q_LORA), MXU_DTYPE), jax.ShapeDtypeStruct((t, KV_LORA), MXU_DTYPE),
                   jax.ShapeDtypeStruct((t, 128), MXU_DTYPE)],
        compiler_params=_params(("parallel",)),
    )(cq, ckv, kr, g_q, g_kv, cos_m, sin_m)


def _mla_q_rope(q_lin, cos_m, sin_m, *, inverse, name):
    t = q_lin.shape[0]
    tr = _tile(t, 256, 8)

    def body(q_ref, c_ref, s_ref, o_ref):
        c, s = c_ref[...], s_ref[...]
        for h in range(MLA_HEADS):
            lo = h * MLA_QK_PAD
            o_ref[:, lo:lo + MLA_NOPE] = q_ref[:, lo:lo + MLA_NOPE].astype(o_ref.dtype)
            roped = _rope64(q_ref[:, lo + MLA_NOPE:lo + MLA_QK_PAD].astype(F32), c, s, inverse=inverse)
            o_ref[:, lo + MLA_NOPE:lo + MLA_QK_PAD] = roped.astype(o_ref.dtype)

    rows = pl.BlockSpec((tr, Q_WIDTH_PAD), lambda i: (i, 0))
    tab = pl.BlockSpec((tr, 128), lambda i: (i, 0))
    return pl.pallas_call(
        body, name=name, grid=(t // tr,),
        in_specs=[rows, tab, tab], out_specs=rows, out_shape=jax.ShapeDtypeStruct((t, Q_WIDTH_PAD), MXU_DTYPE),
        compiler_params=_params(("parallel",)),
    )(q_lin, cos_m, sin_m)


_MLA_SCALE = (MLA_NOPE + MLA_ROPE) ** -0.5
_LOG2_E = 1.4426950408889634
_MLA_SCALE_LOG2 = _MLA_SCALE * _LOG2_E
_NEG = -1e30


def _mla_mask(blk):
    r = lax.broadcasted_iota(jnp.int32, (blk, blk), 0)
    c = lax.broadcasted_iota(jnp.int32, (blk, blk), 1)
    return (c // CHUNK) <= (r // CHUNK)


def _mla_fwd(q, kv, kr):
    t = q.shape[0]
    blk = _tile(t, ATTN_BLOCK)
    nb = t // blk

    hps, dq_, dv_ = MLA_HEADS, MLA_QK_PAD, MLA_V

    def body(ii_ref, jj_ref, q_ref, kv_ref, kr_ref, o_ref, lse_ref, m_ref, acc_ref):
        pair = pl.program_id(1)
        i, j = ii_ref[pair], jj_ref[pair]

        @pl.when(j == 0)
        def _():
            m_ref[...] = jnp.full_like(m_ref, _NEG)
            acc_ref[...] = jnp.zeros_like(acc_ref)

        def step(masked):
            krb = kr_ref[...]
            ones = jnp.ones((blk, dv_), MXU_DTYPE)
            for h in range(hps):
                kb = jnp.concatenate([kv_ref[:, h * dq_:h * dq_ + MLA_NOPE], krb], axis=1)
                vb = jnp.concatenate([kv_ref[:, h * dq_ + MLA_NOPE:(h + 1) * dq_], ones], axis=1)
                s = lax.dot_general(q_ref[:, h * dq_:(h + 1) * dq_], kb, _NT, preferred_element_type=F32)
                if masked:
                    s = jnp.where(_mla_mask(blk), s, _NEG)
                m_prev = m_ref[h]
                m_new = jnp.maximum(m_prev, jnp.max(s, axis=-1, keepdims=True))
                alpha = jnp.exp2((m_prev - m_new) * _MLA_SCALE_LOG2)
                p = jnp.exp2((s - jnp.tile(m_new, (1, blk // 128))) * _MLA_SCALE_LOG2)
                acc_ref[h] = jnp.tile(alpha, (1, 2)) * acc_ref[h] + lax.dot_general(
                    p.astype(MXU_DTYPE), vb, _NN, preferred_element_type=F32)
                m_ref[h] = m_new

        @pl.when(j < i)
        def _():
            step(False)

        @pl.when(j == i)
        def _():
            step(True)
            for h in range(hps):
                cols = slice(h * dv_, (h + 1) * dv_)
                acc = acc_ref[h]
                row_sum = acc[:, dv_:]
                o_ref[:, cols] = (acc[:, :dv_] / row_sum).astype(o_ref.dtype)
                lse_ref[:, cols] = m_ref[h] * _MLA_SCALE + jnp.log(row_sum)

    os_ = pl.BlockSpec((blk, hps * dv_), lambda h, p, ii, jj: (ii[p], h))
    ii, jj = _causal_pairs(nb, query_major=True)
    return pl.pallas_call(
        body, name="mla_fwd",
        grid_spec=pltpu.PrefetchScalarGridSpec(
            num_scalar_prefetch=2, grid=(MLA_HEADS // hps, ii.shape[0]),
            in_specs=[pl.BlockSpec((blk, hps * dq_), lambda h, p, ii, jj: (ii[p], h)),
                      pl.BlockSpec((blk, hps * dq_), lambda h, p, ii, jj: (jj[p], h)),
                      pl.BlockSpec((blk, 128), lambda h, p, ii, jj: (jj[p], 0))],
            out_specs=[os_, os_],
            scratch_shapes=[pltpu.VMEM((hps, blk, 128), F32), pltpu.VMEM((hps, blk, 2 * dv_), F32)]),
        out_shape=[jax.ShapeDtypeStruct((t, MLA_HEADS * MLA_V), MXU_DTYPE),
                   jax.ShapeDtypeStruct((t, MLA_HEADS * 128), F32)],
        compiler_params=_params(("parallel", "arbitrary")),
    )(ii, jj, q, kv, kr)


def _mla_bwd(q, kv, kr, o, lse, dattn):
    t = q.shape[0]
    blk = _tile(t, ATTN_BLOCK)
    nb = t // blk
    hps, dq_, dv_ = HEADS_PER_STEP, MLA_QK_PAD, MLA_V
    do_off = RET_WIDTH // (hps * dv_)

    def body(ii_ref, jj_ref, q_ref, kv_ref, kr_ref, o_ref, lse_ref, do_ref, dq_ref, dk_ref, dv_ref, dk_acc, dv_acc):
        pair = pl.program_id(1)
        i, j = ii_ref[pair], jj_ref[pair]

        @pl.when(pair == 0)
        def _():
            dq_ref[...] = jnp.zeros_like(dq_ref)

        @pl.when(i == j)
        def _():
            dk_acc[...] = jnp.zeros_like(dk_acc)
            dv_acc[...] = jnp.zeros_like(dv_acc)

        def step(masked):
            krb = kr_ref[...]
            rows = pl.ds(pl.multiple_of(i * blk, blk), blk)
            for h in range(hps):
                qcols, vcols = slice(h * dq_, (h + 1) * dq_), slice(h * dv_, (h + 1) * dv_)
                qb = q_ref[:, qcols]
                kb = jnp.concatenate([kv_ref[:, h * dq_:h * dq_ + MLA_NOPE], krb], axis=1)
                vb = kv_ref[:, h * dq_ + MLA_NOPE:(h + 1) * dq_]
                dof = do_ref[:, vcols]
                dob = dof.astype(MXU_DTYPE)
                s = lax.dot_general(qb, kb, _NT, preferred_element_type=F32)
                if masked:
                    s = jnp.where(_mla_mask(blk), s, _NEG)
                lse2 = lse_ref[:, vcols] * _LOG2_E
                p = jnp.exp2(s * _MLA_SCALE_LOG2 - jnp.tile(lse2, (1, blk // 128)))
                delta = jnp.sum(dof * o_ref[:, vcols].astype(F32), axis=-1, keepdims=True)
                dv_acc[h] += lax.dot_general(p.astype(MXU_DTYPE), dob, _TN, preferred_element_type=F32)
                dp = lax.dot_general(dob, vb, _NT, preferred_element_type=F32)
                ds = (p * (dp - delta) * _MLA_SCALE).astype(MXU_DTYPE)
                dq_ref[rows, qcols] += lax.dot_general(ds, kb, _NN, preferred_element_type=F32)
                dk_acc[h] += lax.dot_general(ds, qb, _TN, preferred_element_type=F32)

        @pl.when(i > j)
        def _():
            step(False)

        @pl.when(i == j)
        def _():
            step(True)

        @pl.when(i == nb - 1)
        def _():
            for h in range(hps):
                dk_ref[:, h * dq_:(h + 1) * dq_] = dk_acc[h]
                dv_ref[:, h * dv_:(h + 1) * dv_] = dv_acc[h].astype(dv_ref.dtype)

    qmap = lambda off: (lambda h, p, ii, jj: (ii[p], h + off))
    kmap = lambda h, p, ii, jj: (jj[p], h)
    ii, jj = _causal_pairs(nb, query_major=False)
    return pl.pallas_call(
        body, name="mla_bwd",
        grid_spec=pltpu.PrefetchScalarGridSpec(
            num_scalar_prefetch=2, grid=(MLA_HEADS // hps, ii.shape[0]),
            in_specs=[pl.BlockSpec((blk, hps * dq_), qmap(0)), pl.BlockSpec((blk, hps * dq_), kmap),
                      pl.BlockSpec((blk, 128), lambda h, p, ii, jj: (jj[p], 0)),
                      pl.BlockSpec((blk, hps * dv_), qmap(0)), pl.BlockSpec((blk, hps * dv_), qmap(0)),
                      pl.BlockSpec((blk, hps * dv_), qmap(do_off))],
            out_specs=[pl.BlockSpec((t, hps * dq_), lambda h, p, ii, jj: (0, h), pipeline_mode=pl.Buffered(1)),
                       pl.BlockSpec((blk, hps * dq_), kmap), pl.BlockSpec((blk, hps * dv_), kmap)],
            scratch_shapes=[pltpu.VMEM((hps, blk, dq_), F32), pltpu.VMEM((hps, blk, dv_), F32)]),
        out_shape=[jax.ShapeDtypeStruct((t, Q_WIDTH_PAD), F32), jax.ShapeDtypeStruct((t, Q_WIDTH_PAD), F32),
                   jax.ShapeDtypeStruct((t, MLA_HEADS * MLA_V), MXU_DTYPE)],
        compiler_params=_params(("parallel", "arbitrary")),
    )(ii, jj, q, kv, kr, o, lse, dattn)


def _mla_kv_grad(dk, dv, cos_m, sin_m):
    t = dk.shape[0]
    tr = _tile(t, 256, 8)

    def body(dk_ref, dv_ref, c_ref, s_ref, dkv_ref, dkr_ref):
        acc = jnp.zeros((tr, 128), F32)
        for h in range(MLA_HEADS):
            dkv_ref[:, h * 256:h * 256 + 128] = dk_ref[:, h * 256:h * 256 + 128].astype(dkv_ref.dtype)
            dkv_ref[:, h * 256 + 128:h * 256 + 256] = dv_ref[:, h * 128:(h + 1) * 128].astype(dkv_ref.dtype)
            acc = acc + dk_ref[:, h * 256 + 128:h * 256 + 256]
        dkr_ref[...] = _rope64(acc, c_ref[...], s_ref[...], inverse=True).astype(dkr_ref.dtype)

    row = lambda w: pl.BlockSpec((tr, w), lambda i: (i, 0))
    return pl.pallas_call(
        body, name="mla_kv_grad", grid=(t // tr,),
        in_specs=[row(Q_WIDTH_PAD), row(MLA_HEADS * MLA_V), row(128), row(128)],
        out_specs=[row(Q_WIDTH_PAD), row(128)],
        out_shape=[jax.ShapeDtypeStruct((t, Q_WIDTH_PAD), MXU_DTYPE), jax.ShapeDtypeStruct((t, 128), MXU_DTYPE)],
        compiler_params=_params(("parallel",)),
    )(dk, dv, cos_m, sin_m)


_FFN_COLS = 256
_FFN_ROWS = 256


def _shift_down(cur, prev8, n):
    out = pltpu.roll(cur, n, 0)
    head = out[:8]
    row = lax.broadcasted_iota(jnp.int32, head.shape, 0)
    for r in range(n):
        head = jnp.where(row == r, prev8[8 - n + r:8 - n + r + 1, :], head)
    return jnp.concatenate([head, out[8:]], axis=0)


def _shift_up(cur, next8, n):
    rows = cur.shape[0]
    out = pltpu.roll(cur, rows - n, 0)
    tail = out[rows - 8:]
    row = lax.broadcasted_iota(jnp.int32, tail.shape, 0)
    for r in range(n):
        tail = jnp.where(row == 8 - n + r, next8[r:r + 1, :], tail)
    return jnp.concatenate([out[:rows - 8], tail], axis=0)


def _conv_pre(g_ref, cw_ref, cb_ref, c, rc):
    r0 = pl.multiple_of(c * rc, rc)
    cur = g_ref[pl.ds(r0, rc), :].astype(F32)
    prev16 = g_ref[pl.ds(pl.multiple_of(jnp.maximum(r0 - 16, 0), 16), 16), :].astype(F32)
    prev8 = jnp.where(c > 0, prev16[8:], 0.0)
    s1, s2 = _shift_down(cur, prev8, 1), _shift_down(cur, prev8, 2)
    a = cb_ref[...] + cw_ref[2:3, :] * cur + cw_ref[1:2, :] * s1 + cw_ref[0:1, :] * s2
    return r0, cur, s1, s2, a


def _ffn_up_act(hn, w_up_t, gpre, cw, cb):
    t, k = hn.shape
    f = w_up_t.shape[0]
    tm, tn = _tile(t, 1024), _tile(f, 512)
    halo = 16

    def body(a_ref, b_ref, g_ref, h_ref, cw_ref, cb_ref, u_ref, act_ref):
        u = lax.dot_general(a_ref[...].astype(MXU_DTYPE), b_ref[...].astype(MXU_DTYPE), _NT,
                            preferred_element_type=F32)
        g = g_ref[...].astype(F32)
        prev8 = jnp.where(pl.program_id(0) > 0, h_ref[...].astype(F32)[halo - 8:], 0.0)
        s1, s2 = _shift_down(g, prev8, 1), _shift_down(g, prev8, 2)
        a = cb_ref[...] + cw_ref[2:3, :] * g + cw_ref[1:2, :] * s1 + cw_ref[0:1, :] * s2
        u_ref[...] = u.astype(u_ref.dtype)
        act_ref[...] = (a * _sigmoid(a) * u).astype(act_ref.dtype)

    tile = pl.BlockSpec((tm, tn), lambda i, j: (i, j))
    low = jax.ShapeDtypeStruct((t, f), MXU_DTYPE)
    return pl.pallas_call(
        body, name="mm_up_act", grid=(t // tm, f // tn),
        in_specs=[pl.BlockSpec((tm, k), lambda i, j: (i, 0)), pl.BlockSpec((tn, k), lambda i, j: (j, 0)), tile,
                  pl.BlockSpec((halo, tn), lambda i, j: (jnp.maximum(i * (tm // halo) - 1, 0), j)),
                  pl.BlockSpec((CONV_WIDTH, tn), lambda i, j: (0, j)), pl.BlockSpec((1, tn), lambda i, j: (0, j))],
        out_specs=[tile, tile], out_shape=[low, low],
        compiler_params=_params(("parallel", "parallel")),
    )(hn, w_up_t, gpre, gpre, cw, cb)


def _ffn_act_bwd(gpre, u, dact, cw, cb):
    t, f = gpre.shape
    tc = _tile(f, _FFN_COLS)
    rc = _tile(t, _FFN_ROWS, 8)
    nc = t // rc

    def body(g_ref, u_ref, d_ref, cw_ref, cb_ref, dg_ref, du_ref, dcw_ref, dcb_ref, da_ref):
        def chunk(c, carry):
            w0, w1, w2, b = carry
            r0, cur, s1, s2, a = _conv_pre(g_ref, cw_ref, cb_ref, c, rc)
            sg = _sigmoid(a)
            d = d_ref[pl.ds(r0, rc), :].astype(F32)
            du_ref[pl.ds(r0, rc), :] = (d * (a * sg)).astype(du_ref.dtype)
            da = d * u_ref[pl.ds(r0, rc), :].astype(F32) * (sg * (1.0 + a * (1.0 - sg)))
            da_ref[pl.ds(r0, rc), :] = da
            return (w0 + jnp.sum(da * s2, axis=0, keepdims=True), w1 + jnp.sum(da * s1, axis=0, keepdims=True),
                    w2 + jnp.sum(da * cur, axis=0, keepdims=True), b + jnp.sum(da, axis=0, keepdims=True))
        z = jnp.zeros((1, tc), F32)
        w0, w1, w2, b = lax.fori_loop(0, nc, chunk, (z, z, z, z))
        dcw_ref[0:1, :] = w0
        dcw_ref[1:2, :] = w1
        dcw_ref[2:3, :] = w2
        dcb_ref[...] = b

        def chunk2(c, carry):
            r0 = pl.multiple_of(c * rc, rc)
            cur = da_ref[pl.ds(r0, rc), :]
            nxt = da_ref[pl.ds(pl.multiple_of(jnp.minimum(r0 + rc, t - 8), 8), 8), :]
            nxt = jnp.where(c < nc - 1, nxt, 0.0)
            dg = cw_ref[2:3, :] * cur + cw_ref[1:2, :] * _shift_up(cur, nxt, 1) + cw_ref[0:1, :] * _shift_up(cur, nxt, 2)
            dg_ref[pl.ds(r0, rc), :] = dg.astype(dg_ref.dtype)
            return carry
        lax.fori_loop(0, nc, chunk2, 0)

    col = pl.BlockSpec((t, tc), lambda j: (0, j))
    w3 = pl.BlockSpec((CONV_WIDTH, tc), lambda j: (0, j))
    w1 = pl.BlockSpec((1, tc), lambda j: (0, j))
    low = jax.ShapeDtypeStruct((t, f), MXU_DTYPE)
    return pl.pallas_call(
        body, name="ffn_act_bwd", grid=(f // tc,),
        in_specs=[col, col, col, w3, w1], out_specs=[col, col, w3, w1],
        out_shape=[low, low, jax.ShapeDtypeStruct((CONV_WIDTH, f), F32), jax.ShapeDtypeStruct((1, f), F32)],
        scratch_shapes=[pltpu.VMEM((t, tc), F32)],
        compiler_params=_params(("parallel",)),
    )(gpre, u, dact, cw, cb)


def _head_fwd_bwd(h2, glin, pp, target, g_final):
    t, d = h2.shape
    tr = _tile(t, 128, 8)

    def body(h_ref, gl_ref, pp_ref, t_ref, g_ref, loss_ref, dh_ref, dgl_ref, dpp_ref, dg_ref):
        gate = _sigmoid(gl_ref[...])
        ppv = pp_ref[...]
        h3 = h_ref[...] + gate * ppv
        r = lax.rsqrt(jnp.mean(h3 * h3, axis=-1, keepdims=True) + EPS)
        yh = h3 * r
        g = g_ref[...]
        diff = yh * g - t_ref[...]
        lpart = 0.5 * jnp.sum(jnp.mean(diff * diff, axis=-1, keepdims=True), axis=0, keepdims=True)
        dy = diff * (1.0 / d)
        dyg = dy * g
        dh3 = r * dyg - h3 * (r * r * r) * jnp.mean(dyg * h3, axis=-1, keepdims=True)
        dh_ref[...] = dh3
        dgl_ref[...] = (dh3 * ppv * gate * (1.0 - gate)).astype(dgl_ref.dtype)
        dpp_ref[...] = (dh3 * gate).astype(dpp_ref.dtype)
        dgp = jnp.sum(dy * yh, axis=0, keepdims=True)

        @pl.when(pl.program_id(0) == 0)
        def _():
            loss_ref[...] = jnp.broadcast_to(lpart, loss_ref.shape)
            dg_ref[...] = dgp

        @pl.when(pl.program_id(0) > 0)
        def _():
            loss_ref[...] += jnp.broadcast_to(lpart, loss_ref.shape)
            dg_ref[...] += dgp

    row = pl.BlockSpec((tr, d), lambda i: (i, 0))
    vec = pl.BlockSpec((1, d), lambda i: (0, 0))
    low = jax.ShapeDtypeStruct((t, d), MXU_DTYPE)
    return pl.pallas_call(
        body, name="head_fwd_bwd", grid=(t // tr,),
        in_specs=[row, row, row, row, vec],
        out_specs=[pl.BlockSpec((8, 128), lambda i: (0, 0)), row, row, row, vec],
        out_shape=[jax.ShapeDtypeStruct((8, 128), F32), jax.ShapeDtypeStruct((t, d), F32), low, low,
                   jax.ShapeDtypeStruct((1, d), F32)],
        compiler_params=_params(("arbitrary",)),
    )(h2, glin, pp, target, g_final)


class _Order:
    def __init__(self):
        self.last = None

    def tie(self, x):
        return x if self.last is None else lax.optimization_barrier((x, self.last))[0]

    def run(self, fn, first, *args, **kwargs):
        out = fn(self.tie(first), *args, **kwargs)
        self.last = out[0] if isinstance(out, (tuple, list)) else out
        return out


def _local_step(x, p, target, vec, ops):
    t = x.shape[0]
    cos_r, sin_r, cos_m, sin_m = _rope_tables(t)
    lg = _ret_log_gamma()
    low = MXU_DTYPE
    run = ops.order.run
    w = ops.weight

    d = x.shape[1]
    in_pieces = (d // 4, d // 4, d // 2)
    ops.start_gather("w_in", pieces=in_pieces)
    hn1 = ops.after_payloads(run(_rms_fwd, x, vec["g_attn"], name="rms1_fwd"))
    for n in ("w_uq", "w_ukv", "w_o"):
        ops.start_gather(n, after=hn1)
    proj, lo = None, 0
    for part, width in enumerate(in_pieces):
        proj = run(_matmul, hn1[:, lo:lo + width], w(f"w_in/{part}"), tb=True, name=f"mm_proj_{part}", add=proj)
        lo += width
    ops.start_gather("w_ffn_gate", after=proj)
    rq, rk, rv = run(_ret_prep, proj, cos_r, sin_r)
    ops.start_gather("w_ffn_up", after=rq)
    c0 = 4 * RET_WIDTH
    cq = proj[:, c0:c0 + Q_LORA]
    ckv = proj[:, c0 + Q_LORA:c0 + Q_LORA + KV_LORA]
    kr_in = proj[:, c0 + Q_LORA + KV_LORA:c0 + Q_LORA + KV_LORA + 128]
    cqn, kvn, kr = run(_mla_prep, cq, ckv, kr_in, vec["g_q_lora"], vec["g_kv_lora"], cos_m, sin_m)
    ops.start_gather("w_ffn_down", after=cqn)
    q_lin = run(_matmul, cqn, w("w_uq"), tb=True, name="mm_q")
    q = run(_mla_q_rope, q_lin, cos_m, sin_m, inverse=False, name="mla_q_rope")
    kv = run(_matmul, kvn, w("w_ukv"), tb=True, name="mm_kv", out_dtype=low)
    mo, lse = run(_mla_fwd, q, kv, kr)
    ops.start_gather("w_ple_gate", after=mo)
    ops.start_gather("w_ple_proj", after=mo)
    ret_raw, ro = run(_ret_fwd, rq, rk, rv, proj, lg)
    attn = jnp.concatenate([ro, mo], axis=1)
    h1 = run(_matmul, attn, w("w_o"), name="mm_o", add=x)
    hn2 = run(_rms_fwd, h1, vec["g_ffn"], name="rms2_fwd")
    gpre = run(_matmul, hn2, w("w_ffn_gate"), tb=True, name="mm_gate", out_dtype=low)
    u, act = run(_ffn_up_act, hn2, w("w_ffn_up"), gpre, vec["conv_w"], vec["conv_b"])
    h2 = run(_matmul, act, w("w_ffn_down"), name="mm_down", add=h1)
    hn3 = run(_rms_fwd, h2, vec["g_ple"], name="rms3_fwd")
    glin = run(_matmul, hn3, w("w_ple_gate"), name="mm_ple_gate")
    p_low = p.astype(low)
    pp = run(_matmul, p_low, w("w_ple_proj"), tb=True, name="mm_ple_proj")
    loss_part, dh3, dglin, dpp, dg_final = run(_head_fwd_bwd, h2, glin, pp, target, vec["g_final"])

    ops.grad("w_ple_proj", dpp, p_low)
    ops.grad("w_ple_gate", hn3, dglin)
    dhn3 = run(_matmul, dglin, w("w_ple_gate"), tb=True, name="mm_dhn3", out_dtype=low)
    dh2, dh2_low, dg_ple = run(_rms_bwd, h2, dhn3, vec["g_ple"], dh3, name="rms3_bwd", low_copy=True)
    ops.reduce_add("w_ple_proj")
    ops.reduce_add("w_ple_gate")
    ops.grad("w_ffn_down", act, dh2_low)
    dact = run(_matmul, dh2_low, w("w_ffn_down"), tb=True, name="mm_dact", out_dtype=low)
    ops.reduce_add("w_ffn_down")
    dgpre, du, dconv_w, dconv_b = run(_ffn_act_bwd, gpre, u, dact, vec["conv_w"], vec["conv_b"])
    ops.update("w_ple_proj")
    ops.update("w_ple_gate")
    ops.grad("w_ffn_gate", dgpre, hn2)
    ops.grad("w_ffn_up", du, hn2)
    ops.reduce_add("w_ffn_gate")
    dhn2 = run(_matmul, dgpre, w("w_ffn_gate"), name="mm_dhn2_a")
    ops.reduce_add("w_ffn_up")
    dhn2 = run(_matmul, du, w("w_ffn_up"), name="mm_dhn2_b", add=dhn2, out_dtype=low)
    dh1, dh1_low, dg_ffn = run(_rms_bwd, h1, dhn2, vec["g_ffn"], dh2, name="rms2_bwd", low_copy=True)
    ops.update("w_ffn_down")
    ops.grad("w_o", attn, dh1_low)
    dattn = run(_matmul, dh1_low, w("w_o"), tb=True, name="mm_dattn")
    ops.reduce_add("w_o")

    dq_r, dk_full, dv = run(_mla_bwd, q, kv, kr, mo, lse, dattn)
    ops.update("w_ffn_gate")
    dq_lin = run(_mla_q_rope, dq_r, cos_m, sin_m, inverse=True, name="mla_q_unrope")
    dkv, dkr = run(_mla_kv_grad, dk_full, dv, cos_m, sin_m)
    ops.grad("w_uq", dq_lin, cqn)
    dcqn = run(_matmul, dq_lin, w("w_uq"), name="mm_dcqn")
    ops.grad("w_ukv", dkv, kvn)
    dkvn = run(_matmul, dkv, w("w_ukv"), name="mm_dkvn")
    dcq, dcq_low, dg_q = run(_rms_bwd, cq, dcqn, vec["g_q_lora"], None, name="rmsq_bwd", low_copy=True)
    dckv, dckv_low, dg_kv = run(_rms_bwd, ckv, dkvn, vec["g_kv_lora"], None, name="rmskv_bwd", low_copy=True)
    ops.reduce_add("w_uq")
    ops.reduce_add("w_ukv")

    do_ret, drg = run(_ret_gate_bwd, ret_raw, proj, dattn)
    dq_ret, dk_ret, drv = run(_ret_bwd, rq, rk, rv, do_ret, lg)
    drq, drk = run(_ret_unrope, dq_ret, dk_ret, cos_r, sin_r)

    pad = jnp.zeros((t, IN_WIDTH_PAD - IN_WIDTH - 64), low)
    dproj = jnp.concatenate([drq, drk, drv, drg, dcq_low, dckv_low, dkr, pad], axis=1)
    ops.grad("w_in", dproj, hn1)
    for n in ("w_ffn_up", "w_o", "w_uq", "w_ukv"):
        ops.update(n)
    ops.reduce_add("w_in")
    dhn1 = jnp.concatenate([run(_matmul, dproj, w(f"w_in/{part}"), name=f"mm_dhn1_{part}", out_dtype=low)
                            for part in range(len(in_pieces))], axis=1)
    grad_x, dg_attn = run(_rms_bwd, x, dhn1, vec["g_attn"], dh1, name="rms1_bwd")

    gs = {"g_attn": dg_attn, "g_q_lora": dg_q, "g_kv_lora": dg_kv, "g_ffn": dg_ffn, "conv_w": dconv_w,
          "conv_b": dconv_b, "g_ple": dg_ple, "g_final": dg_final}
    return loss_part, grad_x, gs


_COL_SHARDED = ("w_in", "w_uq", "w_ukv", "w_ffn_gate", "w_ffn_up", "w_ple_proj")
_FFN_SHARD = D_FF // N_DEV
_FFN_SHARD_PAD = D_FF_PAD // N_DEV
_HEADS_PER_SHARD = MLA_HEADS // N_DEV
_QK = MLA_NOPE + MLA_ROPE


def _pad_rows(name, a):
    lead = a.shape[:-2]
    if name == "w_uq":
        a = a.reshape(lead + (_HEADS_PER_SHARD, _QK, a.shape[-1]))
        a = jnp.pad(a, [(0, 0)] * len(lead) + [(0, 0), (0, MLA_QK_PAD - _QK), (0, 0)])
        return a.reshape(lead + (_HEADS_PER_SHARD * MLA_QK_PAD, a.shape[-1]))
    if name in ("w_ffn_gate", "w_ffn_up", "w_ffn_down"):
        return jnp.pad(a, [(0, 0)] * len(lead) + [(0, _FFN_SHARD_PAD - _FFN_SHARD), (0, 0)])
    return a


def _unpad_rows(name, a):
    lead = a.shape[:-2]
    if name == "w_uq":
        a = a.reshape(lead + (_HEADS_PER_SHARD, MLA_QK_PAD, a.shape[-1]))[..., :_QK, :]
        return a.reshape(lead + (_HEADS_PER_SHARD * _QK, a.shape[-1]))
    if name in ("w_ffn_gate", "w_ffn_up", "w_ffn_down"):
        return a[..., :_FFN_SHARD, :]
    return a


def _rows_view(name, a):
    return jnp.swapaxes(a, 0, 1) if name in _COL_SHARDED else a


def _shard_payload(name, shard):
    return _pad_rows(name, _rows_view(name, shard).astype(MXU_DTYPE))


def _full_from_gathered(name, g):
    full = g.reshape(g.shape[0] * g.shape[1], g.shape[2])
    if name == "w_in":
        full = jnp.pad(full, ((0, IN_WIDTH_PAD - IN_WIDTH), (0, 0)))
    return full


def _grad_chunks(name, gfull):
    if name == "w_in":
        gfull = gfull[:IN_WIDTH]
    return gfull.reshape(N_DEV, gfull.shape[0] // N_DEV, gfull.shape[1])


def _ffn_vec_layout(a):
    a = a.reshape(a.shape[0], N_DEV, _FFN_SHARD)
    return jnp.pad(a, ((0, 0), (0, 0), (0, _FFN_SHARD_PAD - _FFN_SHARD))).reshape(a.shape[0], D_FF_PAD)


def _ffn_vec_shards(a):
    return a.reshape(a.shape[0], N_DEV, _FFN_SHARD_PAD)[:, :, :_FFN_SHARD]


_MESH = pl.DeviceIdType.MESH
_ANY = pl.BlockSpec(memory_space=pl.ANY)


def _place():
    x, y, c = lax.axis_index("x"), lax.axis_index("y"), lax.axis_index("c")
    chips = [(1 - x, y), (x, 1 - y), (1 - x, 1 - y)]
    return x, y, c, chips


def _handshake(peers):
    barrier = pltpu.get_barrier_semaphore()
    for peer in peers:
        pl.semaphore_signal(barrier, inc=1, device_id=peer, device_id_type=_MESH)
    pl.semaphore_wait(barrier, len(peers))


_SEQUENCER = dict(axis_name="seq", num_cores=1)
_AG_COLLECTIVE_ID = 1
_RS_SIBLING_COLLECTIVE_ID = 2
_RS_CHIPS_COLLECTIVE_ID = 3


def _all_gather_seq(shard, *, name):
    def body(x_ref, out_ref, send_sems, recv_sems, local_sem):
        x, y, c, chips = _place()
        sibling = (x, y, 1 - c)
        _handshake([sibling] + [(*chip, c) for chip in chips])

        def slot(px, py, pc):
            return out_ref.at[4 * px + 2 * py + pc]

        def copy(k, block, to, src=None):
            return pltpu.make_async_remote_copy(
                src_ref=slot(*block) if src is None else src, dst_ref=slot(*block),
                send_sem=send_sems.at[k], recv_sem=recv_sems.at[k], device_id=to, device_id_type=_MESH)

        mine = pltpu.make_async_copy(x_ref, slot(x, y, c), local_sem)
        mine.start()
        first = [copy(0, (x, y, c), sibling, src=x_ref)]
        first += [copy(1 + j, (x, y, c), (*chip, c), src=x_ref) for j, chip in enumerate(chips)]
        for cp in first:
            cp.start()
        passed = [copy(4 + j, (*chip, c), sibling) for j, chip in enumerate(chips)]
        for j, chip in enumerate(chips):
            copy(1 + j, (*chip, c), (x, y, c)).wait_recv()
            passed[j].start()
        copy(0, sibling, (x, y, c)).wait_recv()
        for j, chip in enumerate(chips):
            copy(4 + j, (*chip, 1 - c), (x, y, c)).wait_recv()
        for cp in first + passed:
            cp.wait_send()
        mine.wait()

    return pl.kernel(
        body, out_type=jax.ShapeDtypeStruct((N_DEV,) + shard.shape, shard.dtype),
        mesh=plsc.ScalarSubcoreMesh(**_SEQUENCER), name=name,
        scratch_types=[pltpu.SemaphoreType.DMA((7,)), pltpu.SemaphoreType.DMA((7,)), pltpu.SemaphoreType.DMA(())],
        compiler_params=pltpu.CompilerParams(collective_id=_AG_COLLECTIVE_ID),
    )(shard)


def _exchange_sibling(g, *, name):
    all_slots = g.shape[0] == N_DEV

    def body(g_ref, out_ref, send_sems, recv_sems):
        x, y, c, _ = _place()
        sibling = (x, y, 1 - c)
        _handshake([sibling])
        copies = []
        for chip in range(4):
            cp = pltpu.make_async_remote_copy(
                src_ref=g_ref.at[2 * chip + (1 - c) if all_slots else chip], dst_ref=out_ref.at[chip],
                send_sem=send_sems.at[chip], recv_sem=recv_sems.at[chip], device_id=sibling, device_id_type=_MESH)
            cp.start()
            copies.append(cp)
        for cp in copies:
            cp.wait_recv()
        for cp in copies:
            cp.wait_send()

    return pl.kernel(
        body, out_type=jax.ShapeDtypeStruct((4,) + g.shape[1:], g.dtype),
        mesh=plsc.ScalarSubcoreMesh(**_SEQUENCER), name=name,
        scratch_types=[pltpu.SemaphoreType.DMA((4,)), pltpu.SemaphoreType.DMA((4,))],
        compiler_params=pltpu.CompilerParams(collective_id=_RS_SIBLING_COLLECTIVE_ID),
    )(g)


def _add_sibling(g, recv, *, name):
    _, r, cdim = g.shape
    tr, tc = _tile_2d(r, cdim, 6)
    g4 = g.reshape(4, 2, r, cdim)
    core = lax.axis_index("c").astype(jnp.int32).reshape(1)

    def body(c_ref, g_ref, r_ref, o_ref):
        o_ref[...] = (g_ref[...].astype(F32) + r_ref[...].astype(F32)).astype(o_ref.dtype)

    return pl.pallas_call(
        body, name=name,
        grid_spec=pltpu.PrefetchScalarGridSpec(
            num_scalar_prefetch=1, grid=(4, r // tr, cdim // tc),
            in_specs=[pl.BlockSpec((None, None, tr, tc), lambda ch, i, j, c_ref: (ch, c_ref[0], i, j)),
                      pl.BlockSpec((None, tr, tc), lambda ch, i, j, c_ref: (ch, i, j))],
            out_specs=pl.BlockSpec((None, tr, tc), lambda ch, i, j, c_ref: (ch, i, j))),
        out_shape=jax.ShapeDtypeStruct((4, r, cdim), g.dtype),
        compiler_params=_params(("parallel", "parallel", "parallel")),
    )(core, g4, recv)


def _exchange_chips(pch, *, name):
    def body(p_ref, out_ref, send_sems, recv_sems, local_sem):
        x, y, c, chips = _place()
        _handshake([(*chip, c) for chip in chips])
        me = 2 * x + y
        mine = pltpu.make_async_copy(p_ref.at[me], out_ref.at[me], local_sem)
        mine.start()
        copies = []
        for j, (px, py) in enumerate(chips):
            cp = pltpu.make_async_remote_copy(
                src_ref=p_ref.at[2 * px + py], dst_ref=out_ref.at[me],
                send_sem=send_sems.at[j], recv_sem=recv_sems.at[j], device_id=(px, py, c), device_id_type=_MESH)
            cp.start()
            copies.append(cp)
        for j, (px, py) in enumerate(chips):
            pltpu.make_async_remote_copy(
                src_ref=p_ref.at[me], dst_ref=out_ref.at[2 * px + py],
                send_sem=send_sems.at[j], recv_sem=recv_sems.at[j], device_id=(px, py, c), device_id_type=_MESH).wait_recv()
        for cp in copies:
            cp.wait_send()
        mine.wait()

    return pl.kernel(
        body, out_type=jax.ShapeDtypeStruct(pch.shape, pch.dtype),
        mesh=plsc.ScalarSubcoreMesh(**_SEQUENCER), name=name,
        scratch_types=[pltpu.SemaphoreType.DMA((3,)), pltpu.SemaphoreType.DMA((3,)), pltpu.SemaphoreType.DMA(())],
        compiler_params=pltpu.CompilerParams(collective_id=_RS_CHIPS_COLLECTIVE_ID),
    )(pch)


def _all_reduce_small(v, *, name):
    r = v.shape[0]

    def body(x_ref, out_ref, buf_ref, send_sems, recv_sems):
        x, y, c, chips = _place()
        sibling = (x, y, 1 - c)

        def slot(px, py, pc):
            return buf_ref.at[4 * px + 2 * py + pc]

        def copy(k, block, to, src=None):
            return pltpu.make_async_remote_copy(
                src_ref=slot(*block) if src is None else src, dst_ref=slot(*block),
                send_sem=send_sems.at[k], recv_sem=recv_sems.at[k], device_id=to, device_id_type=_MESH)

        first = [copy(0, (x, y, c), sibling, src=x_ref)]
        first += [copy(1 + j, (x, y, c), (*chip, c), src=x_ref) for j, chip in enumerate(chips)]
        for cp in first:
            cp.start()
        buf_ref[4 * x + 2 * y + c] = x_ref[...]
        passed = [copy(4 + j, (*chip, c), sibling) for j, chip in enumerate(chips)]
        for j, chip in enumerate(chips):
            copy(1 + j, (*chip, c), (x, y, c)).wait_recv()
            passed[j].start()
        copy(0, sibling, (x, y, c)).wait_recv()
        for j, chip in enumerate(chips):
            copy(4 + j, (*chip, 1 - c), (x, y, c)).wait_recv()
        for cp in first + passed:
            cp.wait_send()
        total = buf_ref[0]
        for k in range(1, N_DEV):
            total = total + buf_ref[k]
        out_ref[...] = total

    vm = pl.BlockSpec(memory_space=pltpu.VMEM)
    return pl.pallas_call(
        body, name=name, out_shape=jax.ShapeDtypeStruct(v.shape, v.dtype),
        in_specs=[vm], out_specs=vm,
        scratch_shapes=[pltpu.VMEM((N_DEV,) + v.shape, v.dtype), pltpu.SemaphoreType.DMA((7,)),
                        pltpu.SemaphoreType.DMA((7,))],
    )(v)


_ELEMENTWISE_VMEM = 24 * 1024 * 1024


def _tile_2d(r, c, n_arrays):
    per_block = _ELEMENTWISE_VMEM // (8 * n_arrays)
    tr = _tile(r, max(16, per_block // max(c, 128)), 16)
    by_rows = (tr, c) if tr * c <= per_block else None
    tc = _tile(c, max(128, (per_block // r) // 128 * 128))
    by_cols = (r, tc) if r * tc <= per_block else None
    if by_rows is None or (by_cols is not None and r * tc > tr * c):
        assert by_cols is not None, (r, c, n_arrays)
        return by_cols
    return by_rows


def _adam_math(w, g, m, v):
    m = ADAM_B1 * m + (1.0 - ADAM_B1) * g
    v = ADAM_B2 * v + (1.0 - ADAM_B2) * jnp.square(g)
    m_hat = m / (1.0 - ADAM_B1 ** ADAM_STEP)
    v_hat = v / (1.0 - ADAM_B2 ** ADAM_STEP)
    delta = -ADAM_LR * (m_hat / (jnp.sqrt(v_hat) + ADAM_EPS) + ADAM_WD * w)
    return delta, m, v


def _adam(w, g, m, v, *, name, parts=None):
    r, cdim = w.shape
    tr, tc = _tile_2d(r, cdim, 8)

    def body(w_ref, g_ref, m_ref, v_ref, go_ref, d_ref, mo_ref, vo_ref):
        if parts is None:
            g = g_ref[...]
        else:
            g = g_ref[0].astype(F32)
            for k in range(1, parts):
                g = g + g_ref[k].astype(F32)
        delta, m, v = _adam_math(w_ref[...], g, m_ref[...], v_ref[...])
        go_ref[...] = g
        d_ref[...] = delta
        mo_ref[...] = m
        vo_ref[...] = v

    blk = pl.BlockSpec((tr, tc), lambda i, j: (i, j))
    gblk = blk if parts is None else pl.BlockSpec((parts, tr, tc), lambda i, j: (0, i, j))
    out = jax.ShapeDtypeStruct((r, cdim), F32)
    return pl.pallas_call(
        body, name=name, grid=(r // tr, cdim // tc), in_specs=[blk, gblk, blk, blk], out_specs=[blk] * 4,
        out_shape=[out] * 4, compiler_params=_params(("parallel", "parallel")),
    )(w, g, m, v)


_BIG = ("w_in", "w_uq", "w_ukv", "w_o", "w_ffn_gate", "w_ffn_up", "w_ffn_down", "w_ple_gate", "w_ple_proj")
_WEIGHTS = ("w_in", "g_attn", "g_q_lora", "g_kv_lora", "w_uq", "w_ukv", "w_o", "g_ffn", "w_ffn_gate", "w_ffn_up",
            "conv_w", "conv_b", "w_ffn_down", "g_ple", "w_ple_gate", "w_ple_proj", "g_final")
_SMALL_PACK = (("g_attn", 1, D_MODEL), ("g_q_lora", 1, Q_LORA), ("g_kv_lora", 1, KV_LORA), ("g_ffn", 1, D_MODEL),
               ("conv_w", CONV_WIDTH, D_FF_PAD), ("conv_b", 1, D_FF_PAD), ("g_ple", 1, D_MODEL), ("g_final", 1, D_MODEL))


def _pack_small(gs):
    flat = jnp.concatenate([gs[n].reshape(-1) for n, _, _ in _SMALL_PACK])
    rows = -(-flat.shape[0] // 128)
    rows = -(-rows // 8) * 8
    return jnp.pad(flat, (0, rows * 128 - flat.shape[0])).reshape(rows, 128)


def _unpack_small(packed):
    flat = packed.reshape(-1)
    out, off = {}, 0
    for n, r, c in _SMALL_PACK:
        out[n] = flat[off:off + r * c].reshape(r, c)
        off += r * c
    return out


def kernel(x, p, w_in, g_attn, g_q_lora, g_kv_lora, w_uq, w_ukv, w_o, g_ffn, w_ffn_gate, w_ffn_up, conv_w, conv_b, w_ffn_down, g_ple, w_ple_gate, w_ple_proj, g_final, loss_target, m_w_in, m_g_attn, m_g_q_lora, m_g_kv_lora, m_w_uq, m_w_ukv, m_w_o, m_g_ffn, m_w_ffn_gate, m_w_ffn_up, m_conv_w, m_conv_b, m_w_ffn_down, m_g_ple, m_w_ple_gate, m_w_ple_proj, m_g_final, v_w_in, v_g_attn, v_g_q_lora, v_g_kv_lora, v_w_uq, v_w_ukv, v_w_o, v_g_ffn, v_w_ffn_gate, v_w_ffn_up, v_conv_w, v_conv_b, v_w_ffn_down, v_g_ple, v_w_ple_gate, v_w_ple_proj, v_g_final):
    given = dict(locals())
    wts = {n: given[n] for n in _WEIGHTS}
    mom = {n: given["m_" + n] for n in _WEIGHTS}
    var = {n: given["v_" + n] for n in _WEIGHTS}
    me = (4 * lax.axis_index("x") + 2 * lax.axis_index("y") + lax.axis_index("c")).astype(jnp.int32)
    ops = _ShardedWeights(wts, mom, var)

    conv_full = _all_gather_seq(conv_w, name="ag_conv_w")[:, 0].transpose(1, 0, 2).reshape(CONV_WIDTH, D_FF)
    vec = {"g_attn": g_attn, "g_q_lora": g_q_lora, "g_kv_lora": g_kv_lora, "g_ffn": g_ffn, "g_ple": g_ple,
           "g_final": g_final[None, :], "conv_w": _ffn_vec_layout(conv_full), "conv_b": _ffn_vec_layout(conv_b)}

    loss_part, grad_x, gs = _local_step(x[0], p[0, 0], loss_target[0], vec, ops)
    loss = lax.psum(loss_part[0, 0], ("x", "y", "c"))

    small = _unpack_small(_all_reduce_small(ops.order.tie(_pack_small(gs)), name="ar_small"))
    conv_w_shards = _ffn_vec_shards(small["conv_w"])
    small_g = {
        "g_attn": small["g_attn"], "g_q_lora": small["g_q_lora"], "g_kv_lora": small["g_kv_lora"],
        "g_ffn": small["g_ffn"], "g_ple": small["g_ple"], "g_final": small["g_final"],
        "conv_b": _ffn_vec_shards(small["conv_b"]).reshape(1, D_FF),
        "conv_w": lax.dynamic_index_in_dim(conv_w_shards, me, axis=1, keepdims=False),
    }
    results = dict(ops.results)
    for n, g in small_g.items():
        shape = wts[n].shape
        outs = ops.order.run(_adam, wts[n].reshape(g.shape), g, mom[n].reshape(g.shape), var[n].reshape(g.shape),
                             name="adam_" + n)
        results[n] = tuple(a.reshape(shape) for a in outs)
    ops.update("w_in")
    results["w_in"] = ops.results["w_in"]

    return (loss, grad_x[None], *[results[n][0] for n in _WEIGHTS], *[results[n][1] for n in _WEIGHTS],
            *[results[n][2] for n in _WEIGHTS], *[results[n][3] for n in _WEIGHTS])


class _ShardedWeights:
    def __init__(self, wts, mom, var):
        self.wts, self.mom, self.var = wts, mom, var
        self.order = _Order()
        self.full, self.stage, self.results = {}, {}, {}
        self.payloads = {n: _shard_payload(n, wts[n][0]) for n in _BIG}

    def after_payloads(self, x):
        names = [n for n in _BIG if n != "w_in"]
        tied = lax.optimization_barrier((x, *[self.payloads[n] for n in names]))
        self.payloads.update(zip(names, tied[1:]))
        return tied[0]

    def start_gather(self, name, after=None, pieces=None):
        payload = self.payloads[name]
        if after is not None:
            payload = lax.optimization_barrier((payload, after))[0]
        if pieces is None:
            self.full[name] = _full_from_gathered(name, _all_gather_seq(payload, name="ag_" + name))
            return
        lo = 0
        for part, width in enumerate(pieces):
            piece = payload[:, lo:lo + width]
            self.full[f"{name}/{part}"] = _full_from_gathered(name, _all_gather_seq(piece, name=f"ag_{name}_{part}"))
            lo += width

    def weight(self, name):
        return self.full[name]

    def grad(self, name, a, b):
        core = lax.axis_index("c")
        if name == "w_in":
            chunks = _grad_chunks(name, self.order.run(_matmul, a, b, ta=True, name="mm_d_" + name,
                                                        out_dtype=MXU_DTYPE))
            self.stage[name] = (chunks, None, _exchange_sibling(chunks, name="rs_sib_" + name))
            return
        theirs = self.order.run(_grad_matmul_half, a, b, 1 - core, name="mm_d_" + name + "_sib")
        self.stage[name] = (a, b, _exchange_sibling(theirs, name="rs_sib_" + name))

    def reduce_add(self, name):
        a, b, from_sibling = self.stage[name]
        if b is None:
            per_chip = self.order.run(_add_sibling, a, from_sibling, name="rs_add_" + name)
        else:
            per_chip = self.order.run(_grad_matmul_half, a, b, lax.axis_index("c"), name="mm_d_" + name + "_own",
                                      add=from_sibling)
        self.stage[name] = _exchange_chips(per_chip, name="rs_chip_" + name)

    def update(self, name):
        parts = self.stage[name]
        if name == "w_uq":
            parts = _unpad_rows(name, parts)
        rows = lambda a: _rows_view(name, a[0])
        outs = self.order.run(_adam, rows(self.wts[name]), parts, rows(self.mom[name]), rows(self.var[name]),
                              name="adam_" + name, parts=4)
        self.results[name] = tuple(_rows_view(name, a)[None] for a in outs)
```

```python
import functools

import numpy as np

import jax
import jax.numpy as jnp
from jax import lax
from jax.experimental import pallas as pl
from jax.experimental.pallas import tpu as pltpu
from jax.experimental.pallas import tpu_sc as plsc

D_MODEL = 4096
CHUNK = 64
PLE_DIM = 256
RET_HEADS = 8
RET_HEAD_DIM = 256
RET_WIDTH = 2048
MLA_HEADS = 16
MLA_NOPE = 128
MLA_ROPE = 64
MLA_V = 128
Q_LORA = 1024
KV_LORA = 512
D_FF = 11008
CONV_WIDTH = 3
ROPE_BASE = 10000.0
EPS = 1e-6
IN_WIDTH = 9792
ADAM_LR, ADAM_B1, ADAM_B2, ADAM_EPS, ADAM_WD, ADAM_STEP = 0.001, 0.9, 0.999, 1e-08, 0.01, 10

IN_WIDTH_PAD = 10240
D_FF_PAD = 11264
MLA_QK_PAD = 256
Q_WIDTH_PAD = MLA_HEADS * MLA_QK_PAD

N_DEV = 8
MXU_DTYPE = jnp.bfloat16
ATTN_BLOCK = 512
HEADS_PER_STEP = 4
FWD_HEADS_PER_STEP = 8
VMEM_LIMIT = 56 * 1024 * 1024

F32 = jnp.float32


def _tile(n, want, align=128):
    if n <= want:
        return n
    t = (want // align) * align
    while t >= align:
        if n % t == 0:
            return t
        t -= align
    return n


def _params(sem):
    return pltpu.CompilerParams(dimension_semantics=sem, vmem_limit_bytes=VMEM_LIMIT)


def _sigmoid(x):
    return 1.0 / (1.0 + jnp.exp(-x))


def _matmul(a, b, *, name, ta=False, tb=False, out_dtype=F32, add=None, tm=1024, tn=1024, tk=4096):
    m, k = (a.shape[1], a.shape[0]) if ta else a.shape
    k2, n = (b.shape[1], b.shape[0]) if tb else b.shape
    assert k == k2, (a.shape, b.shape, ta, tb)
    tm, tn, tk = _tile(m, tm), _tile(n, tn), _tile(k, tk)
    nk = k // tk
    dims = (((0 if ta else 1,), (1 if tb else 0,)), ((), ()))

    def body(*refs):
        a_ref, b_ref, o_ref = refs[0], refs[1], refs[3 if add is not None else 2]
        c_ref = refs[2] if add is not None else None
        part = lax.dot_general(a_ref[...].astype(MXU_DTYPE), b_ref[...].astype(MXU_DTYPE), dims,
                               preferred_element_type=F32)

        def finish(r):
            if c_ref is not None:
                r = r + c_ref[...].astype(F32)
            o_ref[...] = r.astype(out_dtype)

        if nk == 1:
            finish(part)
            return
        acc_ref = refs[-1]
        kk = pl.program_id(2)

        @pl.when(kk == 0)
        def _():
            acc_ref[...] = part

        @pl.when((kk > 0) & (kk < nk - 1))
        def _():
            acc_ref[...] += part

        @pl.when(kk == nk - 1)
        def _():
            finish(acc_ref[...] + part)

    a_spec = pl.BlockSpec((tk, tm), lambda i, j, kk: (kk, i)) if ta else pl.BlockSpec((tm, tk), lambda i, j, kk: (i, kk))
    b_spec = pl.BlockSpec((tn, tk), lambda i, j, kk: (j, kk)) if tb else pl.BlockSpec((tk, tn), lambda i, j, kk: (kk, j))
    in_specs = [a_spec, b_spec]
    args = [a, b]
    if add is not None:
        in_specs.append(pl.BlockSpec((tm, tn), lambda i, j, kk: (i, j)))
        args.append(add)
    return pl.pallas_call(
        body, name=name, grid=(m // tm, n // tn, nk),
        in_specs=in_specs, out_specs=pl.BlockSpec((tm, tn), lambda i, j, kk: (i, j)),
        out_shape=jax.ShapeDtypeStruct((m, n), out_dtype),
        scratch_shapes=[] if nk == 1 else [pltpu.VMEM((tm, tn), F32)],
        compiler_params=_params(("parallel", "parallel", "arbitrary")),
    )(*args)


def _grad_matmul_half(a, b, core_class, *, name, add=None):
    t, m = a.shape
    r = m // N_DEV
    cdim = b.shape[1]
    tn = _tile(cdim, 512 if r > 1024 else 1024)
    cls = jnp.asarray(core_class, jnp.int32).reshape(1)

    def body(cls_ref, a_ref, b_ref, *rest):
        part = lax.dot_general(a_ref[...].astype(MXU_DTYPE), b_ref[...].astype(MXU_DTYPE), _TN,
                               preferred_element_type=F32)
        if add is not None:
            part = part + rest[0][...].astype(F32)
        rest[-1][...] = part.astype(rest[-1].dtype)

    slot = pl.BlockSpec((None, r, tn), lambda q, j, cls_ref: (q, 0, j))
    in_specs = [pl.BlockSpec((t, r), lambda q, j, cls_ref: (0, 2 * q + cls_ref[0])),
                pl.BlockSpec((t, tn), lambda q, j, cls_ref: (0, j))] + ([slot] if add is not None else [])
    args = [a, b] + ([add] if add is not None else [])
    return pl.pallas_call(
        body, name=name,
        grid_spec=pltpu.PrefetchScalarGridSpec(
            num_scalar_prefetch=1, grid=(N_DEV // 2, cdim // tn), in_specs=in_specs, out_specs=slot),
        out_shape=jax.ShapeDtypeStruct((N_DEV // 2, r, cdim), MXU_DTYPE),
        compiler_params=_params(("parallel", "parallel")),
    )(cls, *args)


def _rms_fwd(x, g, *, name):
    t, d = x.shape
    tr = _tile(t, 256, 8)

    def body(x_ref, g_ref, o_ref):
        xf = x_ref[...]
        r = lax.rsqrt(jnp.mean(xf * xf, axis=-1, keepdims=True) + EPS)
        o_ref[...] = (xf * r * g_ref[...]).astype(o_ref.dtype)

    return pl.pallas_call(
        body, name=name, grid=(t // tr,),
        in_specs=[pl.BlockSpec((tr, d), lambda i: (i, 0)), pl.BlockSpec((1, d), lambda i: (0, 0))],
        out_specs=pl.BlockSpec((tr, d), lambda i: (i, 0)),
        out_shape=jax.ShapeDtypeStruct((t, d), MXU_DTYPE),
        compiler_params=_params(("parallel",)),
    )(x, g)


def _rms_bwd(x, dhn, g, res, *, name, low_copy=False):
    t, d = x.shape
    tr = _tile(t, 256, 8)

    def body(*refs):
        if res is None:
            x_ref, dh_ref, g_ref = refs[:3]
            outs = refs[3:]
            res_ref = None
        else:
            x_ref, dh_ref, g_ref, res_ref = refs[:4]
            outs = refs[4:]
        dx_ref, dg_ref = outs[0], outs[-1]
        xf = x_ref[...]
        dh = dh_ref[...].astype(F32)
        r = lax.rsqrt(jnp.mean(xf * xf, axis=-1, keepdims=True) + EPS)
        dyg = dh * g_ref[...]
        dx = r * dyg - xf * (r * r * r) * jnp.mean(dyg * xf, axis=-1, keepdims=True)
        if res_ref is not None:
            dx = dx + res_ref[...]
        dx_ref[...] = dx
        if low_copy:
            outs[1][...] = dx.astype(outs[1].dtype)
        part = jnp.sum(dh * xf * r, axis=0, keepdims=True)

        @pl.when(pl.program_id(0) == 0)
        def _():
            dg_ref[...] = part

        @pl.when(pl.program_id(0) > 0)
        def _():
            dg_ref[...] += part

    row = pl.BlockSpec((tr, d), lambda i: (i, 0))
    vec = pl.BlockSpec((1, d), lambda i: (0, 0))
    in_specs = [row, row, vec] + ([] if res is None else [row])
    args = [x, dhn, g] + ([] if res is None else [res])
    out_specs = [row] + ([row] if low_copy else []) + [vec]
    out_shape = [jax.ShapeDtypeStruct((t, d), F32)] + ([jax.ShapeDtypeStruct((t, d), MXU_DTYPE)] if low_copy else []) \
        + [jax.ShapeDtypeStruct((1, d), F32)]
    return pl.pallas_call(
        body, name=name, grid=(t // tr,), in_specs=in_specs, out_specs=out_specs, out_shape=out_shape,
        compiler_params=_params(("arbitrary",)),
    )(*args)


def _rope_tables(t):
    pos = jnp.arange(t, dtype=F32)[:, None]
    inv_r = 1.0 / (ROPE_BASE ** (jnp.arange(0, RET_HEAD_DIM, 2, dtype=F32) / RET_HEAD_DIM))
    ang_r = pos * inv_r[None, :]
    inv_m = 1.0 / (ROPE_BASE ** (jnp.arange(0, MLA_ROPE, 2, dtype=F32) / MLA_ROPE))
    ang_m = pos * inv_m[None, :]
    cm, sm = jnp.cos(ang_m), jnp.sin(ang_m)
    z = jnp.zeros_like(cm)
    cos_m = jnp.concatenate([cm, cm, z, z], axis=1)
    sin_m = jnp.concatenate([-sm, sm, z, z], axis=1)
    return jnp.cos(ang_r), jnp.sin(ang_r), cos_m, sin_m


def _rope256(x, c, s, inverse=False):
    x1, x2 = x[:, :128], x[:, 128:]
    if inverse:
        s = -s
    return jnp.concatenate([x1 * c - x2 * s, x2 * c + x1 * s], axis=1)


def _rope64(x, cos_m, sin_m, inverse=False):
    lane = lax.broadcasted_iota(jnp.int32, x.shape, 1)
    partner = jnp.where(lane < 32, pltpu.roll(x, 96, 1), pltpu.roll(x, 32, 1))
    s = -sin_m if inverse else sin_m
    return x * cos_m + partner * s


def _ret_prep(proj, cos_r, sin_r):
    t = proj.shape[0]
    tr = _tile(t, 256, 8)

    def body(q_ref, k_ref, v_ref, c_ref, s_ref, qo_ref, ko_ref, vo_ref):
        c, s = c_ref[...], s_ref[...]
        for h in range(RET_HEADS):
            cols = slice(h * RET_HEAD_DIM, (h + 1) * RET_HEAD_DIM)
            qo_ref[:, cols] = _rope256(q_ref[:, cols], c, s).astype(qo_ref.dtype)
            ko_ref[:, cols] = (_rope256(k_ref[:, cols], c, s) * (RET_HEAD_DIM ** -0.5)).astype(ko_ref.dtype)
        vo_ref[...] = v_ref[...].astype(vo_ref.dtype)

    group = lambda off: pl.BlockSpec((tr, RET_WIDTH), lambda i: (i, off))
    tab = pl.BlockSpec((tr, 128), lambda i: (i, 0))
    out = jax.ShapeDtypeStruct((t, RET_WIDTH), MXU_DTYPE)
    return pl.pallas_call(
        body, name="ret_prep", grid=(t // tr,),
        in_specs=[group(0), group(1), group(2), tab, tab],
        out_specs=[group(0), group(0), group(0)], out_shape=[out, out, out],
        compiler_params=_params(("parallel",)),
    )(proj, proj, proj, cos_r, sin_r)


def _ret_log_gamma():
    return jnp.asarray(np.log1p(-np.exp2(-5.0 - np.arange(RET_HEADS, dtype=np.float64))), dtype=F32)


def _decay_full(lg, i, j, blk):
    r = lax.broadcasted_iota(jnp.int32, (blk, 1), 0).astype(F32)
    c = lax.broadcasted_iota(jnp.int32, (1, blk), 1).astype(F32)
    off = ((i - j) * blk).astype(F32)
    return jnp.exp(lg * r), jnp.exp(lg * (off - c))


def _decay_diag(lg, blk):
    r = lax.broadcasted_iota(jnp.int32, (blk, blk), 0)
    c = lax.broadcasted_iota(jnp.int32, (blk, blk), 1)
    ok = (c // CHUNK) <= (r // CHUNK)
    return jnp.where(ok, jnp.exp(lg * jnp.abs(r - c).astype(F32)), 0.0)


_NT = (((1,), (1,)), ((), ()))
_TN = (((0,), (0,)), ((), ()))
_NN = (((1,), (0,)), ((), ()))


def _causal_pairs(nb, query_major):
    if query_major:
        pairs = [(i, j) for i in range(nb) for j in range(i + 1)]
    else:
        pairs = [(i, j) for j in range(nb) for i in range(j, nb)]
    arr = np.asarray(pairs, dtype=np.int32)
    return jnp.asarray(arr[:, 0]), jnp.asarray(arr[:, 1])


def _ret_fwd(q, k, v, proj, lg):
    t = q.shape[0]
    blk = _tile(t, ATTN_BLOCK)
    nb = t // blk
    hps, d = FWD_HEADS_PER_STEP, RET_HEAD_DIM
    gate_off = 3 * RET_WIDTH // (hps * d)

    def body(ii_ref, jj_ref, lg_ref, q_ref, k_ref, v_ref, g_ref, raw_ref, ro_ref, acc_ref):
        hg, pair = pl.program_id(0), pl.program_id(1)
        i, j = ii_ref[pair], jj_ref[pair]

        @pl.when(j == 0)
        def _():
            acc_ref[...] = jnp.zeros_like(acc_ref)

        def step(diag):
            for h in range(hps):
                cols = slice(h * d, (h + 1) * d)
                lgh = lg_ref[hg * hps + h]
                s = lax.dot_general(q_ref[:, cols], k_ref[:, cols], _NT, preferred_element_type=F32)
                if diag:
                    w = s * _decay_diag(lgh, blk)
                else:
                    a, b = _decay_full(lgh, i, j, blk)
                    w = s * a * b
                acc_ref[h] += lax.dot_general(w.astype(MXU_DTYPE), v_ref[:, cols], _NN, preferred_element_type=F32)

        @pl.when(j < i)
        def _():
            step(False)

        @pl.when(j == i)
        def _():
            step(True)
            for h in range(hps):
                cols = slice(h * d, (h + 1) * d)
                o = acc_ref[h]
                raw_ref[:, cols] = o
                mu = jnp.mean(o, axis=-1, keepdims=True)
                var = jnp.mean(jnp.square(o - mu), axis=-1, keepdims=True)
                hn = (o - mu) * lax.rsqrt(var + EPS)
                g = g_ref[:, cols]
                ro_ref[:, cols] = (g * _sigmoid(g) * hn).astype(ro_ref.dtype)

    qs = pl.BlockSpec((blk, hps * d), lambda h, p, ii, jj: (ii[p], h))
    ks = pl.BlockSpec((blk, hps * d), lambda h, p, ii, jj: (jj[p], h))
    gs = pl.BlockSpec((blk, hps * d), lambda h, p, ii, jj: (ii[p], h + gate_off))
    ii, jj = _causal_pairs(nb, query_major=True)
    return pl.pallas_call(
        body, name="ret_fwd",
        grid_spec=pltpu.PrefetchScalarGridSpec(
            num_scalar_prefetch=2, grid=(RET_HEADS // hps, ii.shape[0]),
            in_specs=[pl.BlockSpec(memory_space=pltpu.SMEM), qs, ks, ks, gs], out_specs=[qs, qs],
            scratch_shapes=[pltpu.VMEM((hps, blk, d), F32)]),
        out_shape=[jax.ShapeDtypeStruct((t, RET_WIDTH), F32), jax.ShapeDtypeStruct((t, RET_WIDTH), MXU_DTYPE)],
        compiler_params=_params(("parallel", "arbitrary")),
    )(ii, jj, lg, q, k, v, proj)


def _ret_gate_bwd(raw, proj, dattn):
    t = raw.shape[0]
    tr = _tile(t, 256, 8)

    def body(o_ref, g_ref, d_ref, do_ref, dg_ref):
        for h in range(RET_HEADS):
            cols = slice(h * RET_HEAD_DIM, (h + 1) * RET_HEAD_DIM)
            o, g, d = o_ref[:, cols], g_ref[:, cols], d_ref[:, cols]
            mu = jnp.mean(o, axis=-1, keepdims=True)
            rstd = lax.rsqrt(jnp.mean(jnp.square(o - mu), axis=-1, keepdims=True) + EPS)
            hn = (o - mu) * rstd
            sg = _sigmoid(g)
            dg_ref[:, cols] = (d * hn * (sg * (1.0 + g * (1.0 - sg)))).astype(dg_ref.dtype)
            dhn = d * (g * sg)
            do = rstd * (dhn - jnp.mean(dhn, axis=-1, keepdims=True)
                         - hn * jnp.mean(dhn * hn, axis=-1, keepdims=True))
            do_ref[:, cols] = do.astype(do_ref.dtype)

    group = lambda off: pl.BlockSpec((tr, RET_WIDTH), lambda i: (i, off))
    out = jax.ShapeDtypeStruct((t, RET_WIDTH), MXU_DTYPE)
    return pl.pallas_call(
        body, name="ret_gate_bwd", grid=(t // tr,),
        in_specs=[group(0), group(3), group(0)], out_specs=[group(0), group(0)], out_shape=[out, out],
        compiler_params=_params(("parallel",)),
    )(raw, proj, dattn)


def _ret_bwd(q, k, v, do, lg):
    t = q.shape[0]
    blk = _tile(t, ATTN_BLOCK)
    nb = t // blk

    hps, d = HEADS_PER_STEP, RET_HEAD_DIM

    def body(ii_ref, jj_ref, lg_ref, q_ref, k_ref, v_ref, do_ref, dq_ref, dk_ref, dv_ref, dk_acc, dv_acc):
        hg, pair = pl.program_id(0), pl.program_id(1)
        i, j = ii_ref[pair], jj_ref[pair]

        @pl.when(pair == 0)
        def _():
            dq_ref[...] = jnp.zeros_like(dq_ref)

        @pl.when(i == j)
        def _():
            dk_acc[...] = jnp.zeros_like(dk_acc)
            dv_acc[...] = jnp.zeros_like(dv_acc)

        def step(diag):
            rows = pl.ds(pl.multiple_of(i * blk, blk), blk)
            for h in range(hps):
                cols = slice(h * d, (h + 1) * d)
                lgh = lg_ref[hg * hps + h]
                if diag:
                    decay = _decay_diag(lgh, blk)
                else:
                    a, b = _decay_full(lgh, i, j, blk)
                    decay = a * b
                qb, kb, vb, dob = q_ref[:, cols], k_ref[:, cols], v_ref[:, cols], do_ref[:, cols]
                s = lax.dot_general(qb, kb, _NT, preferred_element_type=F32)
                w = (s * decay).astype(MXU_DTYPE)
                dv_acc[h] += lax.dot_general(w, dob, _TN, preferred_element_type=F32)
                dw = lax.dot_general(dob, vb, _NT, preferred_element_type=F32)
                ds = (dw * decay).astype(MXU_DTYPE)
                dq_ref[rows, cols] += lax.dot_general(ds, kb, _NN, preferred_element_type=F32)
                dk_acc[h] += lax.dot_general(ds, qb, _TN, preferred_element_type=F32)

        @pl.when(i > j)
        def _():
            step(False)

        @pl.when(i == j)
        def _():
            step(True)

        @pl.when(i == nb - 1)
        def _():
            for h in range(hps):
                cols = slice(h * d, (h + 1) * d)
                dk_ref[:, cols] = dk_acc[h]
                dv_ref[:, cols] = dv_acc[h].astype(dv_ref.dtype)

    qs = pl.BlockSpec((blk, hps * d), lambda h, p, ii, jj: (ii[p], h))
    ks = pl.BlockSpec((blk, hps * d), lambda h, p, ii, jj: (jj[p], h))
    ii, jj = _causal_pairs(nb, query_major=False)
    return pl.pallas_call(
        body, name="ret_bwd",
        grid_spec=pltpu.PrefetchScalarGridSpec(
            num_scalar_prefetch=2, grid=(RET_HEADS // hps, ii.shape[0]),
            in_specs=[pl.BlockSpec(memory_space=pltpu.SMEM), qs, ks, ks, qs],
            out_specs=[pl.BlockSpec((t, hps * d), lambda h, p, ii, jj: (0, h), pipeline_mode=pl.Buffered(1)), ks, ks],
            scratch_shapes=[pltpu.VMEM((hps, blk, d), F32), pltpu.VMEM((hps, blk, d), F32)]),
        out_shape=[jax.ShapeDtypeStruct((t, RET_WIDTH), F32), jax.ShapeDtypeStruct((t, RET_WIDTH), F32),
                   jax.ShapeDtypeStruct((t, RET_WIDTH), MXU_DTYPE)],
        compiler_params=_params(("parallel", "arbitrary")),
    )(ii, jj, lg, q, k, v, do)


def _ret_unrope(dq, dk, cos_r, sin_r):
    t = dq.shape[0]
    tr = _tile(t, 256, 8)

    def body(dq_ref, dk_ref, c_ref, s_ref, oq_ref, ok_ref):
        c, s = c_ref[...], s_ref[...]
        for h in range(RET_HEADS):
            cols = slice(h * RET_HEAD_DIM, (h + 1) * RET_HEAD_DIM)
            oq_ref[:, cols] = _rope256(dq_ref[:, cols], c, s, inverse=True).astype(oq_ref.dtype)
            ok_ref[:, cols] = (_rope256(dk_ref[:, cols], c, s, inverse=True)
                               * (RET_HEAD_DIM ** -0.5)).astype(ok_ref.dtype)

    rows = pl.BlockSpec((tr, RET_WIDTH), lambda i: (i, 0))
    tab = pl.BlockSpec((tr, 128), lambda i: (i, 0))
    out = jax.ShapeDtypeStruct((t, RET_WIDTH), MXU_DTYPE)
    return pl.pallas_call(
        body, name="ret_unrope", grid=(t // tr,),
        in_specs=[rows, rows, tab, tab], out_specs=[rows, rows], out_shape=[out, out],
        compiler_params=_params(("parallel",)),
    )(dq, dk, cos_r, sin_r)


def _mla_prep(cq, ckv, kr, g_q, g_kv, cos_m, sin_m):
    t = cq.shape[0]
    tr = _tile(t, 512, 8)

    def body(cq_ref, ckv_ref, kr_ref, gq_ref, gkv_ref, c_ref, s_ref, cqn_ref, kvn_ref, kro_ref):
        for x_ref, g_ref, o_ref in ((cq_ref, gq_ref, cqn_ref), (ckv_ref, gkv_ref, kvn_ref)):
            xf = x_ref[...]
            r = lax.rsqrt(jnp.mean(xf * xf, axis=-1, keepdims=True) + EPS)
            o_ref[...] = (xf * r * g_ref[...]).astype(o_ref.dtype)
        kro_ref[...] = _rope64(kr_ref[...], c_ref[...], s_ref[...]).astype(kro_ref.dtype)

    row = lambda w: pl.BlockSpec((tr, w), lambda i: (i, 0))
    vec = lambda w: pl.BlockSpec((1, w), lambda i: (0, 0))
    return pl.pallas_call(
        body, name="mla_prep", grid=(t // tr,),
        in_specs=[row(Q_LORA), row(KV_LORA), row(128), vec(Q_LORA), vec(KV_LORA), row(128), row(128)],
        out_specs=[row(Q_LORA), row(KV_LORA), row(128)],
        out_shape=[jax.ShapeDtypeStruct((t, Q_LORA), MXU_DTYPE), jax.ShapeDtypeStruct((t, KV_LORA), MXU_DTYPE),
                   jax.ShapeDtypeStruct((t, 128), MXU_DTYPE)],
        compiler_params=_params(("parallel",)),
    )(cq, ckv, kr, g_q, g_kv, cos_m, sin_m)


def _mla_q_rope(q_lin, cos_m, sin_m, *, inverse, name):
    t = q_lin.shape[0]
    tr = _tile(t, 256, 8)

    def body(q_ref, c_ref, s_ref, o_ref):
        c, s = c_ref[...], s_ref[...]
        for h in range(MLA_HEADS):
            lo = h * MLA_QK_PAD
            o_ref[:, lo:lo + MLA_NOPE] = q_ref[:, lo:lo + MLA_NOPE].astype(o_ref.dtype)
            roped = _rope64(q_ref[:, lo + MLA_NOPE:lo + MLA_QK_PAD].astype(F32), c, s, inverse=inverse)
            o_ref[:, lo + MLA_NOPE:lo + MLA_QK_PAD] = roped.astype(o_ref.dtype)

    rows = pl.BlockSpec((tr, Q_WIDTH_PAD), lambda i: (i, 0))
    tab = pl.BlockSpec((tr, 128), lambda i: (i, 0))
    return pl.pallas_call(
        body, name=name, grid=(t // tr,),
        in_specs=[rows, tab, tab], out_specs=rows, out_shape=jax.ShapeDtypeStruct((t, Q_WIDTH_PAD), MXU_DTYPE),
        compiler_params=_params(("parallel",)),
    )(q_lin, cos_m, sin_m)


_MLA_SCALE = (MLA_NOPE + MLA_ROPE) ** -0.5
_LOG2_E = 1.4426950408889634
_MLA_SCALE_LOG2 = _MLA_SCALE * _LOG2_E
_NEG = -1e30


def _mla_mask(blk):
    r = lax.broadcasted_iota(jnp.int32, (blk, blk), 0)
    c = lax.broadcasted_iota(jnp.int32, (blk, blk), 1)
    return (c // CHUNK) <= (r // CHUNK)


def _mla_fwd(q, kv, kr):
    t = q.shape[0]
    blk = _tile(t, ATTN_BLOCK)
    nb = t // blk

    hps, dq_, dv_ = MLA_HEADS, MLA_QK_PAD, MLA_V

    def body(ii_ref, jj_ref, q_ref, kv_ref, kr_ref, o_ref, lse_ref, m_ref, acc_ref):
        pair = pl.program_id(1)
        i, j = ii_ref[pair], jj_ref[pair]

        @pl.when(j == 0)
        def _():
            m_ref[...] = jnp.full_like(m_ref, _NEG)
            acc_ref[...] = jnp.zeros_like(acc_ref)

        def step(masked):
            krb = kr_ref[...]
            ones = jnp.ones((blk, dv_), MXU_DTYPE)
            for h in range(hps):
                kb = jnp.concatenate([kv_ref[:, h * dq_:h * dq_ + MLA_NOPE], krb], axis=1)
                vb = jnp.concatenate([kv_ref[:, h * dq_ + MLA_NOPE:(h + 1) * dq_], ones], axis=1)
                s = lax.dot_general(q_ref[:, h * dq_:(h + 1) * dq_], kb, _NT, preferred_element_type=F32)
                if masked:
                    s = jnp.where(_mla_mask(blk), s, _NEG)
                m_prev = m_ref[h]
                m_new = jnp.maximum(m_prev, jnp.max(s, axis=-1, keepdims=True))
                alpha = jnp.exp2((m_prev - m_new) * _MLA_SCALE_LOG2)
                p = jnp.exp2((s - jnp.tile(m_new, (1, blk // 128))) * _MLA_SCALE_LOG2)
                acc_ref[h] = jnp.tile(alpha, (1, 2)) * acc_ref[h] + lax.dot_general(
                    p.astype(MXU_DTYPE), vb, _NN, preferred_element_type=F32)
                m_ref[h] = m_new

        @pl.when(j < i)
        def _():
            step(False)

        @pl.when(j == i)
        def _():
            step(True)
            for h in range(hps):
                cols = slice(h * dv_, (h + 1) * dv_)
                acc = acc_ref[h]
                row_sum = acc[:, dv_:]
                o_ref[:, cols] = (acc[:, :dv_] / row_sum).astype(o_ref.dtype)
                lse_ref[:, cols] = m_ref[h] * _MLA_SCALE + jnp.log(row_sum)

    os_ = pl.BlockSpec((blk, hps * dv_), lambda h, p, ii, jj: (ii[p], h))
    ii, jj = _causal_pairs(nb, query_major=True)
    return pl.pallas_call(
        body, name="mla_fwd",
        grid_spec=pltpu.PrefetchScalarGridSpec(
            num_scalar_prefetch=2, grid=(MLA_HEADS // hps, ii.shape[0]),
            in_specs=[pl.BlockSpec((blk, hps * dq_), lambda h, p, ii, jj: (ii[p], h)),
                      pl.BlockSpec((blk, hps * dq_), lambda h, p, ii, jj: (jj[p], h)),
                      pl.BlockSpec((blk, 128), lambda h, p, ii, jj: (jj[p], 0))],
            out_specs=[os_, os_],
            scratch_shapes=[pltpu.VMEM((hps, blk, 128), F32), pltpu.VMEM((hps, blk, 2 * dv_), F32)]),
        out_shape=[jax.ShapeDtypeStruct((t, MLA_HEADS * MLA_V), MXU_DTYPE),
                   jax.ShapeDtypeStruct((t, MLA_HEADS * 128), F32)],
        compiler_params=_params(("parallel", "arbitrary")),
    )(ii, jj, q, kv, kr)


def _mla_bwd(q, kv, kr, o, lse, dattn):
    t = q.shape[0]
    blk = _tile(t, ATTN_BLOCK)
    nb = t // blk
    hps, dq_, dv_ = HEADS_PER_STEP, MLA_QK_PAD, MLA_V
    do_off = RET_WIDTH // (hps * dv_)

    def body(ii_ref, jj_ref, q_ref, kv_ref, kr_ref, o_ref, lse_ref, do_ref, dq_ref, dk_ref, dv_ref, dk_acc, dv_acc):
        pair = pl.program_id(1)
        i, j = ii_ref[pair], jj_ref[pair]

        @pl.when(pair == 0)
        def _():
            dq_ref[...] = jnp.zeros_like(dq_ref)

        @pl.when(i == j)
        def _():
            dk_acc[...] = jnp.zeros_like(dk_acc)
            dv_acc[...] = jnp.zeros_like(dv_acc)

        def step(masked):
            krb = kr_ref[...]
            rows = pl.ds(pl.multiple_of(i * blk, blk), blk)
            for h in range(hps):
                qcols, vcols = slice(h * dq_, (h + 1) * dq_), slice(h * dv_, (h + 1) * dv_)
                qb = q_ref[:, qcols]
                kb = jnp.concatenate([kv_ref[:, h * dq_:h * dq_ + MLA_NOPE], krb], axis=1)
                vb = kv_ref[:, h * dq_ + MLA_NOPE:(h + 1) * dq_]
                dof = do_ref[:, vcols]
                dob = dof.astype(MXU_DTYPE)
                s = lax.dot_general(qb, kb, _NT, preferred_element_type=F32)
                if masked:
                    s = jnp.where(_mla_mask(blk), s, _NEG)
                lse2 = lse_ref[:, vcols] * _LOG2_E
                p = jnp.exp2(s * _MLA_SCALE_LOG2 - jnp.tile(lse2, (1, blk // 128)))
                delta = jnp.sum(dof * o_ref[:, vcols].astype(F32), axis=-1, keepdims=True)
                dv_acc[h] += lax.dot_general(p.astype(MXU_DTYPE), dob, _TN, preferred_element_type=F32)
                dp = lax.dot_general(dob, vb, _NT, preferred_element_type=F32)
                ds = (p * (dp - delta) * _MLA_SCALE).astype(MXU_DTYPE)
                dq_ref[rows, qcols] += lax.dot_general(ds, kb, _NN, preferred_element_type=F32)
                dk_acc[h] += lax.dot_general(ds, qb, _TN, preferred_element_type=F32)

        @pl.when(i > j)
        def _():
            step(False)

        @pl.when(i == j)
        def _():
            step(True)

        @pl.when(i == nb - 1)
        def _():
            for h in range(hps):
                dk_ref[:, h * dq_:(h + 1) * dq_] = dk_acc[h]
                dv_ref[:, h * dv_:(h + 1) * dv_] = dv_acc[h].astype(dv_ref.dtype)

    qmap = lambda off: (lambda h, p, ii, jj: (ii[p], h + off))
    kmap = lambda h, p, ii, jj: (jj[p], h)
    ii, jj = _causal_pairs(nb, query_major=False)
    return pl.pallas_call(
        body, name="mla_bwd",
        grid_spec=pltpu.PrefetchScalarGridSpec(
            num_scalar_prefetch=2, grid=(MLA_HEADS // hps, ii.shape[0]),
            in_specs=[pl.BlockSpec((blk, hps * dq_), qmap(0)), pl.BlockSpec((blk, hps * dq_), kmap),
                      pl.BlockSpec((blk, 128), lambda h, p, ii, jj: (jj[p], 0)),
                      pl.BlockSpec((blk, hps * dv_), qmap(0)), pl.BlockSpec((blk, hps * dv_), qmap(0)),
                      pl.BlockSpec((blk, hps * dv_), qmap(do_off))],
            out_specs=[pl.BlockSpec((t, hps * dq_), lambda h, p, ii, jj: (0, h), pipeline_mode=pl.Buffered(1)),
                       pl.BlockSpec((blk, hps * dq_), kmap), pl.BlockSpec((blk, hps * dv_), kmap)],
            scratch_shapes=[pltpu.VMEM((hps, blk, dq_), F32), pltpu.VMEM((hps, blk, dv_), F32)]),
        out_shape=[jax.ShapeDtypeStruct((t, Q_WIDTH_PAD), F32), jax.ShapeDtypeStruct((t, Q_WIDTH_PAD), F32),
                   jax.ShapeDtypeStruct((t, MLA_HEADS * MLA_V), MXU_DTYPE)],
        compiler_params=_params(("parallel", "arbitrary")),
    )(ii, jj, q, kv, kr, o, lse, dattn)


def _mla_kv_grad(dk, dv, cos_m, sin_m):
    t = dk.shape[0]
    tr = _tile(t, 256, 8)

    def body(dk_ref, dv_ref, c_ref, s_ref, dkv_ref, dkr_ref):
        acc = jnp.zeros((tr, 128), F32)
        for h in range(MLA_HEADS):
            dkv_ref[:, h * 256:h * 256 + 128] = dk_ref[:, h * 256:h * 256 + 128].astype(dkv_ref.dtype)
            dkv_ref[:, h * 256 + 128:h * 256 + 256] = dv_ref[:, h * 128:(h + 1) * 128].astype(dkv_ref.dtype)
            acc = acc + dk_ref[:, h * 256 + 128:h * 256 + 256]
        dkr_ref[...] = _rope64(acc, c_ref[...], s_ref[...], inverse=True).astype(dkr_ref.dtype)

    row = lambda w: pl.BlockSpec((tr, w), lambda i: (i, 0))
    return pl.pallas_call(
        body, name="mla_kv_grad", grid=(t // tr,),
        in_specs=[row(Q_WIDTH_PAD), row(MLA_HEADS * MLA_V), row(128), row(128)],
        out_specs=[row(Q_WIDTH_PAD), row(128)],
        out_shape=[jax.ShapeDtypeStruct((t, Q_WIDTH_PAD), MXU_DTYPE), jax.ShapeDtypeStruct((t, 128), MXU_DTYPE)],
        compiler_params=_params(("parallel",)),
    )(dk, dv, cos_m, sin_m)


_FFN_COLS = 256
_FFN_ROWS = 256


def _shift_down(cur, prev8, n):
    out = pltpu.roll(cur, n, 0)
    head = out[:8]
    row = lax.broadcasted_iota(jnp.int32, head.shape, 0)
    for r in range(n):
        head = jnp.where(row == r, prev8[8 - n + r:8 - n + r + 1, :], head)
    return jnp.concatenate([head, out[8:]], axis=0)


def _shift_up(cur, next8, n):
    rows = cur.shape[0]
    out = pltpu.roll(cur, rows - n, 0)
    tail = out[rows - 8:]
    row = lax.broadcasted_iota(jnp.int32, tail.shape, 0)
    for r in range(n):
        tail = jnp.where(row == 8 - n + r, next8[r:r + 1, :], tail)
    return jnp.concatenate([out[:rows - 8], tail], axis=0)


def _conv_pre(g_ref, cw_ref, cb_ref, c, rc):
    r0 = pl.multiple_of(c * rc, rc)
    cur = g_ref[pl.ds(r0, rc), :].astype(F32)
    prev16 = g_ref[pl.ds(pl.multiple_of(jnp.maximum(r0 - 16, 0), 16), 16), :].astype(F32)
    prev8 = jnp.where(c > 0, prev16[8:], 0.0)
    s1, s2 = _shift_down(cur, prev8, 1), _shift_down(cur, prev8, 2)
    a = cb_ref[...] + cw_ref[2:3, :] * cur + cw_ref[1:2, :] * s1 + cw_ref[0:1, :] * s2
    return r0, cur, s1, s2, a


def _ffn_up_act(hn, w_up_t, gpre, cw, cb):
    t, k = hn.shape
    f = w_up_t.shape[0]
    tm, tn = _tile(t, 1024), _tile(f, 512)
    halo = 16

    def body(a_ref, b_ref, g_ref, h_ref, cw_ref, cb_ref, u_ref, act_ref):
        u = lax.dot_general(a_ref[...].astype(MXU_DTYPE), b_ref[...].astype(MXU_DTYPE), _NT,
                            preferred_element_type=F32)
        g = g_ref[...].astype(F32)
        prev8 = jnp.where(pl.program_id(0) > 0, h_ref[...].astype(F32)[halo - 8:], 0.0)
        s1, s2 = _shift_down(g, prev8, 1), _shift_down(g, prev8, 2)
        a = cb_ref[...] + cw_ref[2:3, :] * g + cw_ref[1:2, :] * s1 + cw_ref[0:1, :] * s2
        u_ref[...] = u.astype(u_ref.dtype)
        act_ref[...] = (a * _sigmoid(a) * u).astype(act_ref.dtype)

    tile = pl.BlockSpec((tm, tn), lambda i, j: (i, j))
    low = jax.ShapeDtypeStruct((t, f), MXU_DTYPE)
    return pl.pallas_call(
        body, name="mm_up_act", grid=(t // tm, f // tn),
        in_specs=[pl.BlockSpec((tm, k), lambda i, j: (i, 0)), pl.BlockSpec((tn, k), lambda i, j: (j, 0)), tile,
                  pl.BlockSpec((halo, tn), lambda i, j: (jnp.maximum(i * (tm // halo) - 1, 0), j)),
                  pl.BlockSpec((CONV_WIDTH, tn), lambda i, j: (0, j)), pl.BlockSpec((1, tn), lambda i, j: (0, j))],
        out_specs=[tile, tile], out_shape=[low, low],
        compiler_params=_params(("parallel", "parallel")),
    )(hn, w_up_t, gpre, gpre, cw, cb)


def _ffn_act_bwd(gpre, u, dact, cw, cb):
    t, f = gpre.shape
    tc = _tile(f, _FFN_COLS)
    rc = _tile(t, _FFN_ROWS, 8)
    nc = t // rc

    def body(g_ref, u_ref, d_ref, cw_ref, cb_ref, dg_ref, du_ref, dcw_ref, dcb_ref, da_ref):
        def chunk(c, carry):
            w0, w1, w2, b = carry
            r0, cur, s1, s2, a = _conv_pre(g_ref, cw_ref, cb_ref, c, rc)
            sg = _sigmoid(a)
            d = d_ref[pl.ds(r0, rc), :].astype(F32)
            du_ref[pl.ds(r0, rc), :] = (d * (a * sg)).astype(du_ref.dtype)
            da = d * u_ref[pl.ds(r0, rc), :].astype(F32) * (sg * (1.0 + a * (1.0 - sg)))
            da_ref[pl.ds(r0, rc), :] = da
            ones = jnp.ones((8, rc), F32)
            colsum = lambda v: lax.dot_general(ones, v, _NN, precision=lax.Precision.HIGHEST,
                                               preferred_element_type=F32)
            return (w0 + colsum(da * s2), w1 + colsum(da * s1), w2 + colsum(da * cur), b + colsum(da))
        z = jnp.zeros((8, tc), F32)
        w0, w1, w2, b = lax.fori_loop(0, nc, chunk, (z, z, z, z))
        dcw_ref[0:1, :] = w0[0:1]
        dcw_ref[1:2, :] = w1[0:1]
        dcw_ref[2:3, :] = w2[0:1]
        dcb_ref[...] = b[0:1]

        def chunk2(c, carry):
            r0 = pl.multiple_of(c * rc, rc)
            cur = da_ref[pl.ds(r0, rc), :]
            nxt = da_ref[pl.ds(pl.multiple_of(jnp.minimum(r0 + rc, t - 8), 8), 8), :]
            nxt = jnp.where(c < nc - 1, nxt, 0.0)
            dg = cw_ref[2:3, :] * cur + cw_ref[1:2, :] * _shift_up(cur, nxt, 1) + cw_ref[0:1, :] * _shift_up(cur, nxt, 2)
            dg_ref[pl.ds(r0, rc), :] = dg.astype(dg_ref.dtype)
            return carry
        lax.fori_loop(0, nc, chunk2, 0)

    col = pl.BlockSpec((t, tc), lambda j: (0, j))
    w3 = pl.BlockSpec((CONV_WIDTH, tc), lambda j: (0, j))
    w1 = pl.BlockSpec((1, tc), lambda j: (0, j))
    low = jax.ShapeDtypeStruct((t, f), MXU_DTYPE)
    return pl.pallas_call(
        body, name="ffn_act_bwd", grid=(f // tc,),
        in_specs=[col, col, col, w3, w1], out_specs=[col, col, w3, w1],
        out_shape=[low, low, jax.ShapeDtypeStruct((CONV_WIDTH, f), F32), jax.ShapeDtypeStruct((1, f), F32)],
        scratch_shapes=[pltpu.VMEM((t, tc), F32)],
        compiler_params=_params(("parallel",)),
    )(gpre, u, dact, cw, cb)


def _head_fwd_bwd(h2, glin, pp, target, g_final):
    t, d = h2.shape
    tr = _tile(t, 128, 8)

    def body(h_ref, gl_ref, pp_ref, t_ref, g_ref, loss_ref, dh_ref, dgl_ref, dpp_ref, dg_ref):
        gate = _sigmoid(gl_ref[...])
        ppv = pp_ref[...]
        h3 = h_ref[...] + gate * ppv
        r = lax.rsqrt(jnp.mean(h3 * h3, axis=-1, keepdims=True) + EPS)
        yh = h3 * r
        g = g_ref[...]
        diff = yh * g - t_ref[...]
        lpart = 0.5 * jnp.sum(jnp.mean(diff * diff, axis=-1, keepdims=True), axis=0, keepdims=True)
        dy = diff * (1.0 / d)
        dyg = dy * g
        dh3 = r * dyg - h3 * (r * r * r) * jnp.mean(dyg * h3, axis=-1, keepdims=True)
        dh_ref[...] = dh3
        dgl_ref[...] = (dh3 * ppv * gate * (1.0 - gate)).astype(dgl_ref.dtype)
        dpp_ref[...] = (dh3 * gate).astype(dpp_ref.dtype)
        dgp = jnp.sum(dy * yh, axis=0, keepdims=True)

        @pl.when(pl.program_id(0) == 0)
        def _():
            loss_ref[...] = jnp.broadcast_to(lpart, loss_ref.shape)
            dg_ref[...] = dgp

        @pl.when(pl.program_id(0) > 0)
        def _():
            loss_ref[...] += jnp.broadcast_to(lpart, loss_ref.shape)
            dg_ref[...] += dgp

    row = pl.BlockSpec((tr, d), lambda i: (i, 0))
    vec = pl.BlockSpec((1, d), lambda i: (0, 0))
    low = jax.ShapeDtypeStruct((t, d), MXU_DTYPE)
    return pl.pallas_call(
        body, name="head_fwd_bwd", grid=(t // tr,),
        in_specs=[row, row, row, row, vec],
        out_specs=[pl.BlockSpec((8, 128), lambda i: (0, 0)), row, row, row, vec],
        out_shape=[jax.ShapeDtypeStruct((8, 128), F32), jax.ShapeDtypeStruct((t, d), F32), low, low,
                   jax.ShapeDtypeStruct((1, d), F32)],
        compiler_params=_params(("arbitrary",)),
    )(h2, glin, pp, target, g_final)


class _Order:
    def __init__(self):
        self.last = None

    def tie(self, x):
        return x if self.last is None else lax.optimization_barrier((x, self.last))[0]

    def run(self, fn, first, *args, **kwargs):
        out = fn(self.tie(first), *args, **kwargs)
        self.last = out[0] if isinstance(out, (tuple, list)) else out
        return out


def _local_step(x, p, target, vec, ops):
    t = x.shape[0]
    cos_r, sin_r, cos_m, sin_m = _rope_tables(t)
    lg = _ret_log_gamma()
    low = MXU_DTYPE
    run = ops.order.run
    w = ops.weight

    ops.start_gather("w_in", halves=True)
    hn1 = ops.after_payloads(run(_rms_fwd, x, vec["g_attn"], name="rms1_fwd"))
    for n in ("w_uq", "w_ukv", "w_o"):
        ops.start_gather(n, after=hn1)
    half = x.shape[1] // 2
    proj = run(_matmul, hn1[:, :half], w("w_in/0"), tb=True, name="mm_proj_a")
    proj = run(_matmul, hn1[:, half:], w("w_in/1"), tb=True, name="mm_proj_b", add=proj)
    ops.start_gather("w_ffn_gate", after=proj)
    rq, rk, rv = run(_ret_prep, proj, cos_r, sin_r)
    ops.start_gather("w_ffn_up", after=rq)
    c0 = 4 * RET_WIDTH
    cq = proj[:, c0:c0 + Q_LORA]
    ckv = proj[:, c0 + Q_LORA:c0 + Q_LORA + KV_LORA]
    kr_in = proj[:, c0 + Q_LORA + KV_LORA:c0 + Q_LORA + KV_LORA + 128]
    cqn, kvn, kr = run(_mla_prep, cq, ckv, kr_in, vec["g_q_lora"], vec["g_kv_lora"], cos_m, sin_m)
    ops.start_gather("w_ffn_down", after=cqn)
    q_lin = run(_matmul, cqn, w("w_uq"), tb=True, name="mm_q")
    q = run(_mla_q_rope, q_lin, cos_m, sin_m, inverse=False, name="mla_q_rope")
    kv = run(_matmul, kvn, w("w_ukv"), tb=True, name="mm_kv", out_dtype=low)
    mo, lse = run(_mla_fwd, q, kv, kr)
    ops.start_gather("w_ple_gate", after=mo)
    ops.start_gather("w_ple_proj", after=mo)
    ret_raw, ro = run(_ret_fwd, rq, rk, rv, proj, lg)
    attn = jnp.concatenate([ro, mo], axis=1)
    h1 = run(_matmul, attn, w("w_o"), name="mm_o", add=x)
    hn2 = run(_rms_fwd, h1, vec["g_ffn"], name="rms2_fwd")
    gpre = run(_matmul, hn2, w("w_ffn_gate"), tb=True, name="mm_gate", out_dtype=low)
    u, act = run(_ffn_up_act, hn2, w("w_ffn_up"), gpre, vec["conv_w"], vec["conv_b"])
    h2 = run(_matmul, act, w("w_ffn_down"), name="mm_down", add=h1)
    hn3 = run(_rms_fwd, h2, vec["g_ple"], name="rms3_fwd")
    glin = run(_matmul, hn3, w("w_ple_gate"), name="mm_ple_gate")
    p_low = p.astype(low)
    pp = run(_matmul, p_low, w("w_ple_proj"), tb=True, name="mm_ple_proj")
    loss_part, dh3, dglin, dpp, dg_final = run(_head_fwd_bwd, h2, glin, pp, target, vec["g_final"])

    ops.grad("w_ple_proj", dpp, p_low)
    ops.grad("w_ple_gate", hn3, dglin)
    dhn3 = run(_matmul, dglin, w("w_ple_gate"), tb=True, name="mm_dhn3", out_dtype=low)
    dh2, dh2_low, dg_ple = run(_rms_bwd, h2, dhn3, vec["g_ple"], dh3, name="rms3_bwd", low_copy=True)
    ops.reduce_add("w_ple_proj")
    ops.reduce_add("w_ple_gate")
    ops.grad("w_ffn_down", act, dh2_low)
    dact = run(_matmul, dh2_low, w("w_ffn_down"), tb=True, name="mm_dact", out_dtype=low)
    ops.reduce_add("w_ffn_down")
    dgpre, du, dconv_w, dconv_b = run(_ffn_act_bwd, gpre, u, dact, vec["conv_w"], vec["conv_b"])
    ops.update("w_ple_proj")
    ops.update("w_ple_gate")
    ops.grad("w_ffn_gate", dgpre, hn2)
    ops.grad("w_ffn_up", du, hn2)
    ops.reduce_add("w_ffn_gate")
    dhn2 = run(_matmul, dgpre, w("w_ffn_gate"), name="mm_dhn2_a")
    ops.reduce_add("w_ffn_up")
    dhn2 = run(_matmul, du, w("w_ffn_up"), name="mm_dhn2_b", add=dhn2, out_dtype=low)
    dh1, dh1_low, dg_ffn = run(_rms_bwd, h1, dhn2, vec["g_ffn"], dh2, name="rms2_bwd", low_copy=True)
    ops.update("w_ffn_down")
    ops.grad("w_o", attn, dh1_low)
    dattn = run(_matmul, dh1_low, w("w_o"), tb=True, name="mm_dattn")
    ops.reduce_add("w_o")

    dq_r, dk_full, dv = run(_mla_bwd, q, kv, kr, mo, lse, dattn)
    ops.update("w_ffn_gate")
    dq_lin = run(_mla_q_rope, dq_r, cos_m, sin_m, inverse=True, name="mla_q_unrope")
    dkv, dkr = run(_mla_kv_grad, dk_full, dv, cos_m, sin_m)
    ops.grad("w_uq", dq_lin, cqn)
    dcqn = run(_matmul, dq_lin, w("w_uq"), name="mm_dcqn")
    ops.grad("w_ukv", dkv, kvn)
    dkvn = run(_matmul, dkv, w("w_ukv"), name="mm_dkvn")
    dcq, dcq_low, dg_q = run(_rms_bwd, cq, dcqn, vec["g_q_lora"], None, name="rmsq_bwd", low_copy=True)
    dckv, dckv_low, dg_kv = run(_rms_bwd, ckv, dkvn, vec["g_kv_lora"], None, name="rmskv_bwd", low_copy=True)
    ops.reduce_add("w_uq")
    ops.reduce_add("w_ukv")

    do_ret, drg = run(_ret_gate_bwd, ret_raw, proj, dattn)
    dq_ret, dk_ret, drv = run(_ret_bwd, rq, rk, rv, do_ret, lg)
    drq, drk = run(_ret_unrope, dq_ret, dk_ret, cos_r, sin_r)

    pad = jnp.zeros((t, IN_WIDTH_PAD - IN_WIDTH - 64), low)
    dproj = jnp.concatenate([drq, drk, drv, drg, dcq_low, dckv_low, dkr, pad], axis=1)
    ops.grad("w_in", dproj, hn1)
    for n in ("w_ffn_up", "w_o", "w_uq", "w_ukv"):
        ops.update(n)
    ops.reduce_add("w_in")
    dhn1 = jnp.concatenate([run(_matmul, dproj, w(f"w_in/{part}"), name=f"mm_dhn1_{part}", out_dtype=low)
                            for part in range(2)], axis=1)
    grad_x, dg_attn = run(_rms_bwd, x, dhn1, vec["g_attn"], dh1, name="rms1_bwd")

    gs = {"g_attn": dg_attn, "g_q_lora": dg_q, "g_kv_lora": dg_kv, "g_ffn": dg_ffn, "conv_w": dconv_w,
          "conv_b": dconv_b, "g_ple": dg_ple, "g_final": dg_final}
    return loss_part, grad_x, gs


_COL_SHARDED = ("w_in", "w_uq", "w_ukv", "w_ffn_gate", "w_ffn_up", "w_ple_proj")
_FFN_SHARD = D_FF // N_DEV
_FFN_SHARD_PAD = D_FF_PAD // N_DEV
_HEADS_PER_SHARD = MLA_HEADS // N_DEV
_QK = MLA_NOPE + MLA_ROPE


def _pad_rows(name, a):
    lead = a.shape[:-2]
    if name == "w_uq":
        a = a.reshape(lead + (_HEADS_PER_SHARD, _QK, a.shape[-1]))
        a = jnp.pad(a, [(0, 0)] * len(lead) + [(0, 0), (0, MLA_QK_PAD - _QK), (0, 0)])
        return a.reshape(lead + (_HEADS_PER_SHARD * MLA_QK_PAD, a.shape[-1]))
    if name in ("w_ffn_gate", "w_ffn_up", "w_ffn_down"):
        return jnp.pad(a, [(0, 0)] * len(lead) + [(0, _FFN_SHARD_PAD - _FFN_SHARD), (0, 0)])
    return a


def _unpad_rows(name, a):
    lead = a.shape[:-2]
    if name == "w_uq":
        a = a.reshape(lead + (_HEADS_PER_SHARD, MLA_QK_PAD, a.shape[-1]))[..., :_QK, :]
        return a.reshape(lead + (_HEADS_PER_SHARD * _QK, a.shape[-1]))
    if name in ("w_ffn_gate", "w_ffn_up", "w_ffn_down"):
        return a[..., :_FFN_SHARD, :]
    return a


def _rows_view(name, a):
    return jnp.swapaxes(a, 0, 1) if name in _COL_SHARDED else a


def _shard_payload(name, shard):
    return _pad_rows(name, _rows_view(name, shard).astype(MXU_DTYPE))


def _full_from_gathered(name, g):
    full = g.reshape(g.shape[0] * g.shape[1], g.shape[2])
    if name == "w_in":
        full = jnp.pad(full, ((0, IN_WIDTH_PAD - IN_WIDTH), (0, 0)))
    return full


def _grad_chunks(name, gfull):
    if name == "w_in":
        gfull = gfull[:IN_WIDTH]
    return gfull.reshape(N_DEV, gfull.shape[0] // N_DEV, gfull.shape[1])


def _ffn_vec_layout(a):
    a = a.reshape(a.shape[0], N_DEV, _FFN_SHARD)
    return jnp.pad(a, ((0, 0), (0, 0), (0, _FFN_SHARD_PAD - _FFN_SHARD))).reshape(a.shape[0], D_FF_PAD)


def _ffn_vec_shards(a):
    return a.reshape(a.shape[0], N_DEV, _FFN_SHARD_PAD)[:, :, :_FFN_SHARD]


_MESH = pl.DeviceIdType.MESH
_ANY = pl.BlockSpec(memory_space=pl.ANY)


def _place():
    x, y, c = lax.axis_index("x"), lax.axis_index("y"), lax.axis_index("c")
    chips = [(1 - x, y), (x, 1 - y), (1 - x, 1 - y)]
    return x, y, c, chips


def _handshake(peers):
    barrier = pltpu.get_barrier_semaphore()
    for peer in peers:
        pl.semaphore_signal(barrier, inc=1, device_id=peer, device_id_type=_MESH)
    pl.semaphore_wait(barrier, len(peers))


_SEQUENCER = dict(axis_name="seq", num_cores=1)
_AG_COLLECTIVE_ID = 1
_RS_SIBLING_COLLECTIVE_ID = 2
_RS_CHIPS_COLLECTIVE_ID = 3


def _all_gather_seq(shard, *, name):
    def body(x_ref, out_ref, send_sems, recv_sems, local_sem):
        x, y, c, chips = _place()
        sibling = (x, y, 1 - c)
        _handshake([sibling] + [(*chip, c) for chip in chips])

        def slot(px, py, pc):
            return out_ref.at[4 * px + 2 * py + pc]

        def copy(k, block, to, src=None):
            return pltpu.make_async_remote_copy(
                src_ref=slot(*block) if src is None else src, dst_ref=slot(*block),
                send_sem=send_sems.at[k], recv_sem=recv_sems.at[k], device_id=to, device_id_type=_MESH)

        mine = pltpu.make_async_copy(x_ref, slot(x, y, c), local_sem)
        mine.start()
        first = [copy(0, (x, y, c), sibling, src=x_ref)]
        first += [copy(1 + j, (x, y, c), (*chip, c), src=x_ref) for j, chip in enumerate(chips)]
        for cp in first:
            cp.start()
        passed = [copy(4 + j, (*chip, c), sibling) for j, chip in enumerate(chips)]
        for j, chip in enumerate(chips):
            copy(1 + j, (*chip, c), (x, y, c)).wait_recv()
            passed[j].start()
        copy(0, sibling, (x, y, c)).wait_recv()
        for j, chip in enumerate(chips):
            copy(4 + j, (*chip, 1 - c), (x, y, c)).wait_recv()
        for cp in first + passed:
            cp.wait_send()
        mine.wait()

    return pl.kernel(
        body, out_type=jax.ShapeDtypeStruct((N_DEV,) + shard.shape, shard.dtype),
        mesh=plsc.ScalarSubcoreMesh(**_SEQUENCER), name=name,
        scratch_types=[pltpu.SemaphoreType.DMA((7,)), pltpu.SemaphoreType.DMA((7,)), pltpu.SemaphoreType.DMA(())],
        compiler_params=pltpu.CompilerParams(collective_id=_AG_COLLECTIVE_ID),
    )(shard)


def _exchange_sibling(g, *, name):
    all_slots = g.shape[0] == N_DEV

    def body(g_ref, out_ref, send_sems, recv_sems):
        x, y, c, _ = _place()
        sibling = (x, y, 1 - c)
        _handshake([sibling])
        copies = []
        for chip in range(4):
            cp = pltpu.make_async_remote_copy(
                src_ref=g_ref.at[2 * chip + (1 - c) if all_slots else chip], dst_ref=out_ref.at[chip],
                send_sem=send_sems.at[chip], recv_sem=recv_sems.at[chip], device_id=sibling, device_id_type=_MESH)
            cp.start()
            copies.append(cp)
        for cp in copies:
            cp.wait_recv()
        for cp in copies:
            cp.wait_send()

    return pl.kernel(
        body, out_type=jax.ShapeDtypeStruct((4,) + g.shape[1:], g.dtype),
        mesh=plsc.ScalarSubcoreMesh(**_SEQUENCER), name=name,
        scratch_types=[pltpu.SemaphoreType.DMA((4,)), pltpu.SemaphoreType.DMA((4,))],
        compiler_params=pltpu.CompilerParams(collective_id=_RS_SIBLING_COLLECTIVE_ID),
    )(g)


def _add_sibling(g, recv, *, name):
    _, r, cdim = g.shape
    tr, tc = _tile_2d(r, cdim, 6)
    g4 = g.reshape(4, 2, r, cdim)
    core = lax.axis_index("c").astype(jnp.int32).reshape(1)

    def body(c_ref, g_ref, r_ref, o_ref):
        o_ref[...] = (g_ref[...].astype(F32) + r_ref[...].astype(F32)).astype(o_ref.dtype)

    return pl.pallas_call(
        body, name=name,
        grid_spec=pltpu.PrefetchScalarGridSpec(
            num_scalar_prefetch=1, grid=(4, r // tr, cdim // tc),
            in_specs=[pl.BlockSpec((None, None, tr, tc), lambda ch, i, j, c_ref: (ch, c_ref[0], i, j)),
                      pl.BlockSpec((None, tr, tc), lambda ch, i, j, c_ref: (ch, i, j))],
            out_specs=pl.BlockSpec((None, tr, tc), lambda ch, i, j, c_ref: (ch, i, j))),
        out_shape=jax.ShapeDtypeStruct((4, r, cdim), g.dtype),
        compiler_params=_params(("parallel", "parallel", "parallel")),
    )(core, g4, recv)


def _exchange_chips(pch, *, name):
    def body(p_ref, out_ref, send_sems, recv_sems, local_sem):
        x, y, c, chips = _place()
        _handshake([(*chip, c) for chip in chips])
        me = 2 * x + y
        mine = pltpu.make_async_copy(p_ref.at[me], out_ref.at[me], local_sem)
        mine.start()
        copies = []
        for j, (px, py) in enumerate(chips):
            cp = pltpu.make_async_remote_copy(
                src_ref=p_ref.at[2 * px + py], dst_ref=out_ref.at[me],
                send_sem=send_sems.at[j], recv_sem=recv_sems.at[j], device_id=(px, py, c), device_id_type=_MESH)
            cp.start()
            copies.append(cp)
        for j, (px, py) in enumerate(chips):
            pltpu.make_async_remote_copy(
                src_ref=p_ref.at[me], dst_ref=out_ref.at[2 * px + py],
                send_sem=send_sems.at[j], recv_sem=recv_sems.at[j], device_id=(px, py, c), device_id_type=_MESH).wait_recv()
        for cp in copies:
            cp.wait_send()
        mine.wait()

    return pl.kernel(
        body, out_type=jax.ShapeDtypeStruct(pch.shape, pch.dtype),
        mesh=plsc.ScalarSubcoreMesh(**_SEQUENCER), name=name,
        scratch_types=[pltpu.SemaphoreType.DMA((3,)), pltpu.SemaphoreType.DMA((3,)), pltpu.SemaphoreType.DMA(())],
        compiler_params=pltpu.CompilerParams(collective_id=_RS_CHIPS_COLLECTIVE_ID),
    )(pch)


def _all_reduce_small(v, *, name):
    r = v.shape[0]

    def body(x_ref, out_ref, buf_ref, send_sems, recv_sems):
        x, y, c, chips = _place()
        sibling = (x, y, 1 - c)

        def slot(px, py, pc):
            return buf_ref.at[4 * px + 2 * py + pc]

        def copy(k, block, to, src=None):
            return pltpu.make_async_remote_copy(
                src_ref=slot(*block) if src is None else src, dst_ref=slot(*block),
                send_sem=send_sems.at[k], recv_sem=recv_sems.at[k], device_id=to, device_id_type=_MESH)

        first = [copy(0, (x, y, c), sibling, src=x_ref)]
        first += [copy(1 + j, (x, y, c), (*chip, c), src=x_ref) for j, chip in enumerate(chips)]
        for cp in first:
            cp.start()
        buf_ref[4 * x + 2 * y + c] = x_ref[...]
        passed = [copy(4 + j, (*chip, c), sibling) for j, chip in enumerate(chips)]
        for j, chip in enumerate(chips):
            copy(1 + j, (*chip, c), (x, y, c)).wait_recv()
            passed[j].start()
        copy(0, sibling, (x, y, c)).wait_recv()
        for j, chip in enumerate(chips):
            copy(4 + j, (*chip, 1 - c), (x, y, c)).wait_recv()
        for cp in first + passed:
            cp.wait_send()
        total = buf_ref[0]
        for k in range(1, N_DEV):
            total = total + buf_ref[k]
        out_ref[...] = total

    vm = pl.BlockSpec(memory_space=pltpu.VMEM)
    return pl.pallas_call(
        body, name=name, out_shape=jax.ShapeDtypeStruct(v.shape, v.dtype),
        in_specs=[vm], out_specs=vm,
        scratch_shapes=[pltpu.VMEM((N_DEV,) + v.shape, v.dtype), pltpu.SemaphoreType.DMA((7,)),
                        pltpu.SemaphoreType.DMA((7,))],
    )(v)


_ELEMENTWISE_VMEM = 24 * 1024 * 1024


def _tile_2d(r, c, n_arrays):
    per_block = _ELEMENTWISE_VMEM // (8 * n_arrays)
    tr = _tile(r, max(16, per_block // max(c, 128)), 16)
    by_rows = (tr, c) if tr * c <= per_block else None
    tc = _tile(c, max(128, (per_block // r) // 128 * 128))
    by_cols = (r, tc) if r * tc <= per_block else None
    if by_rows is None or (by_cols is not None and r * tc > tr * c):
        assert by_cols is not None, (r, c, n_arrays)
        return by_cols
    return by_rows


def _adam_math(w, g, m, v):
    m = ADAM_B1 * m + (1.0 - ADAM_B1) * g
    v = ADAM_B2 * v + (1.0 - ADAM_B2) * jnp.square(g)
    m_hat = m / (1.0 - ADAM_B1 ** ADAM_STEP)
    v_hat = v / (1.0 - ADAM_B2 ** ADAM_STEP)
    delta = -ADAM_LR * (m_hat / (jnp.sqrt(v_hat) + ADAM_EPS) + ADAM_WD * w)
    return delta, m, v


def _adam(w, g, m, v, *, name, parts=None):
    r, cdim = w.shape
    tr, tc = _tile_2d(r, cdim, 8)

    def body(w_ref, g_ref, m_ref, v_ref, go_ref, d_ref, mo_ref, vo_ref):
        if parts is None:
            g = g_ref[...]
        else:
            g = g_ref[0].astype(F32)
            for k in range(1, parts):
                g = g + g_ref[k].astype(F32)
        delta, m, v = _adam_math(w_ref[...], g, m_ref[...], v_ref[...])
        go_ref[...] = g
        d_ref[...] = delta
        mo_ref[...] = m
        vo_ref[...] = v

    blk = pl.BlockSpec((tr, tc), lambda i, j: (i, j))
    gblk = blk if parts is None else pl.BlockSpec((parts, tr, tc), lambda i, j: (0, i, j))
    out = jax.ShapeDtypeStruct((r, cdim), F32)
    return pl.pallas_call(
        body, name=name, grid=(r // tr, cdim // tc), in_specs=[blk, gblk, blk, blk], out_specs=[blk] * 4,
        out_shape=[out] * 4, compiler_params=_params(("parallel", "parallel")),
    )(w, g, m, v)


_BIG = ("w_in", "w_uq", "w_ukv", "w_o", "w_ffn_gate", "w_ffn_up", "w_ffn_down", "w_ple_gate", "w_ple_proj")
_WEIGHTS = ("w_in", "g_attn", "g_q_lora", "g_kv_lora", "w_uq", "w_ukv", "w_o", "g_ffn", "w_ffn_gate", "w_ffn_up",
            "conv_w", "conv_b", "w_ffn_down", "g_ple", "w_ple_gate", "w_ple_proj", "g_final")
_SMALL_PACK = (("g_attn", 1, D_MODEL), ("g_q_lora", 1, Q_LORA), ("g_kv_lora", 1, KV_LORA), ("g_ffn", 1, D_MODEL),
               ("conv_w", CONV_WIDTH, D_FF_PAD), ("conv_b", 1, D_FF_PAD), ("g_ple", 1, D_MODEL), ("g_final", 1, D_MODEL))


def _pack_small(gs):
    flat = jnp.concatenate([gs[n].reshape(-1) for n, _, _ in _SMALL_PACK])
    rows = -(-flat.shape[0] // 128)
    rows = -(-rows // 8) * 8
    return jnp.pad(flat, (0, rows * 128 - flat.shape[0])).reshape(rows, 128)


def _unpack_small(packed):
    flat = packed.reshape(-1)
    out, off = {}, 0
    for n, r, c in _SMALL_PACK:
        out[n] = flat[off:off + r * c].reshape(r, c)
        off += r * c
    return out


def kernel(x, p, w_in, g_attn, g_q_lora, g_kv_lora, w_uq, w_ukv, w_o, g_ffn, w_ffn_gate, w_ffn_up, conv_w, conv_b, w_ffn_down, g_ple, w_ple_gate, w_ple_proj, g_final, loss_target, m_w_in, m_g_attn, m_g_q_lora, m_g_kv_lora, m_w_uq, m_w_ukv, m_w_o, m_g_ffn, m_w_ffn_gate, m_w_ffn_up, m_conv_w, m_conv_b, m_w_ffn_down, m_g_ple, m_w_ple_gate, m_w_ple_proj, m_g_final, v_w_in, v_g_attn, v_g_q_lora, v_g_kv_lora, v_w_uq, v_w_ukv, v_w_o, v_g_ffn, v_w_ffn_gate, v_w_ffn_up, v_conv_w, v_conv_b, v_w_ffn_down, v_g_ple, v_w_ple_gate, v_w_ple_proj, v_g_final):
    given = dict(locals())
    wts = {n: given[n] for n in _WEIGHTS}
    mom = {n: given["m_" + n] for n in _WEIGHTS}
    var = {n: given["v_" + n] for n in _WEIGHTS}
    me = (4 * lax.axis_index("x") + 2 * lax.axis_index("y") + lax.axis_index("c")).astype(jnp.int32)
    ops = _ShardedWeights(wts, mom, var)

    conv_full = _all_gather_seq(conv_w, name="ag_conv_w")[:, 0].transpose(1, 0, 2).reshape(CONV_WIDTH, D_FF)
    vec = {"g_attn": g_attn, "g_q_lora": g_q_lora, "g_kv_lora": g_kv_lora, "g_ffn": g_ffn, "g_ple": g_ple,
           "g_final": g_final[None, :], "conv_w": _ffn_vec_layout(conv_full), "conv_b": _ffn_vec_layout(conv_b)}

    loss_part, grad_x, gs = _local_step(x[0], p[0, 0], loss_target[0], vec, ops)
    loss = lax.psum(loss_part[0, 0], ("x", "y", "c"))

    small = _unpack_small(_all_reduce_small(ops.order.tie(_pack_small(gs)), name="ar_small"))
    conv_w_shards = _ffn_vec_shards(small["conv_w"])
    small_g = {
        "g_attn": small["g_attn"], "g_q_lora": small["g_q_lora"], "g_kv_lora": small["g_kv_lora"],
        "g_ffn": small["g_ffn"], "g_ple": small["g_ple"], "g_final": small["g_final"],
        "conv_b": _ffn_vec_shards(small["conv_b"]).reshape(1, D_FF),
        "conv_w": lax.dynamic_index_in_dim(conv_w_shards, me, axis=1, keepdims=False),
    }
    results = dict(ops.results)
    for n, g in small_g.items():
        shape = wts[n].shape
        outs = ops.order.run(_adam, wts[n].reshape(g.shape), g, mom[n].reshape(g.shape), var[n].reshape(g.shape),
                             name="adam_" + n)
        results[n] = tuple(a.reshape(shape) for a in outs)
    ops.update("w_in")
    results["w_in"] = ops.results["w_in"]

    return (loss, grad_x[None], *[results[n][0] for n in _WEIGHTS], *[results[n][1] for n in _WEIGHTS],
            *[results[n][2] for n in _WEIGHTS], *[results[n][3] for n in _WEIGHTS])


class _ShardedWeights:
    def __init__(self, wts, mom, var):
        self.wts, self.mom, self.var = wts, mom, var
        self.order = _Order()
        self.full, self.stage, self.results = {}, {}, {}
        self.payloads = {n: _shard_payload(n, wts[n][0]) for n in _BIG}

    def after_payloads(self, x):
        names = [n for n in _BIG if n != "w_in"]
        tied = lax.optimization_barrier((x, *[self.payloads[n] for n in names]))
        self.payloads.update(zip(names, tied[1:]))
        return tied[0]

    def start_gather(self, name, after=None, halves=False):
        payload = self.payloads[name]
        if after is not None:
            payload = lax.optimization_barrier((payload, after))[0]
        if not halves:
            self.full[name] = _full_from_gathered(name, _all_gather_seq(payload, name="ag_" + name))
            return
        half = payload.shape[1] // 2
        for part in range(2):
            piece = payload[:, part * half:(part + 1) * half]
            self.full[f"{name}/{part}"] = _full_from_gathered(name, _all_gather_seq(piece, name=f"ag_{name}_{part}"))

    def weight(self, name):
        return self.full[name]

    def grad(self, name, a, b):
        core = lax.axis_index("c")
        if name == "w_in":
            chunks = _grad_chunks(name, self.order.run(_matmul, a, b, ta=True, name="mm_d_" + name,
                                                        out_dtype=MXU_DTYPE))
            self.stage[name] = (chunks, None, _exchange_sibling(chunks, name="rs_sib_" + name))
            return
        theirs = self.order.run(_grad_matmul_half, a, b, 1 - core, name="mm_d_" + name + "_sib")
        self.stage[name] = (a, b, _exchange_sibling(theirs, name="rs_sib_" + name))

    def reduce_add(self, name):
        a, b, from_sibling = self.stage[name]
        if b is None:
            per_chip = self.order.run(_add_sibling, a, from_sibling, name="rs_add_" + name)
        else:
            per_chip = self.order.run(_grad_matmul_half, a, b, lax.axis_index("c"), name="mm_d_" + name + "_own",
                                      add=from_sibling)
        self.stage[name] = _exchange_chips(per_chip, name="rs_chip_" + name)

    def update(self, name):
        parts = self.stage[name]
        if name == "w_uq":
            parts = _unpad_rows(name, parts)
        rows = lambda a: _rows_view(name, a[0])
        outs = self.order.run(_adam, rows(self.wts[name]), parts, rows(self.mom[name]), rows(self.var[name]),
                              name="adam_" + name, parts=4)
        self.results[name] = tuple(_rows_view(name, a)[None] for a in outs)
```
